```python
import jax, jax.numpy as jnp
from jax import lax
import numpy as np

D_MODEL = 2048
BATCH = 8
SEQ = 2048
DEPTH = 1

D_MIX = D_MODEL
RWKV_WIDTH = D_MIX // 2
RWKV_HEAD = 64
RWKV_HEADS = RWKV_WIDTH // RWKV_HEAD
LORA_W = int(max(32, round(1.8 * RWKV_WIDTH ** 0.5 / 32) * 32))
LORA_A = int(max(32, round(1.8 * RWKV_WIDTH ** 0.5 / 32) * 32))
LORA_G = int(max(32, round(0.6 * RWKV_WIDTH ** 0.8 / 32) * 32))
RWKV_IN = 3 * RWKV_WIDTH + LORA_W + LORA_A + LORA_G
GMLP_WIDTH = D_MIX - RWKV_WIDTH
GMLP_GROUPS = 16
GMLP_GROUP_DIM = GMLP_WIDTH // GMLP_GROUPS
CHUNK = 128
D_IN = RWKV_IN + 2 * GMLP_WIDTH
D_FF = ((8 * D_MODEL + 3 * 256 - 1) // (3 * 256)) * 256
RMS_EPS = 1e-6
LN_EPS = 1e-5
GN_EPS = 64e-5
L2_EPS = 1e-12

kernel_name = "hybrid_rwkv7_chunked_gmlp_block"


def rms_norm(x, g):
    xf = x.astype(jnp.float32)
    y = xf * lax.rsqrt(jnp.mean(xf * xf, axis=-1, keepdims=True) + RMS_EPS)
    return (y * g.astype(jnp.float32)).astype(x.dtype)


def layer_norm(x, g, b):
    xf = x.astype(jnp.float32)
    mu = jnp.mean(xf, axis=-1, keepdims=True)
    var = jnp.mean(jnp.square(xf - mu), axis=-1, keepdims=True)
    y = (xf - mu) * lax.rsqrt(var + LN_EPS)
    return (y * g.astype(jnp.float32) + b.astype(jnp.float32)).astype(x.dtype)


def token_shift(p, mu):
    prev = jnp.pad(p[:, :-1], ((0, 0), (1, 0), (0, 0)))
    return p + (prev - p) * mu


def wkv7_scan(r, w, k, v, a, b):
    dt = r.dtype
    seqs = [jnp.swapaxes(t.astype(jnp.float32), 0, 1) for t in (r, w, k, v, a, b)]
    bsz, hh, n = r.shape[0], r.shape[2], r.shape[3]

    def step(S, inp):
        r_t, w_t, k_t, v_t, a_t, b_t = inp
        sa = jnp.einsum('bhvk,bhk->bhv', S, a_t)
        S = S * w_t[:, :, None, :] + sa[..., None] * b_t[:, :, None, :] + v_t[..., None] * k_t[:, :, None, :]
        y = jnp.einsum('bhvk,bhk->bhv', S, r_t)
        return S, y

    S0 = jnp.zeros((bsz, hh, n, n), jnp.float32)
    _, ys = lax.scan(step, S0, tuple(seqs))
    return jnp.swapaxes(ys, 0, 1).astype(dt)


def rwkv7_group(p, w0, w2, a0, a2, g2, k_k, k_a, r_k, gn_w, gn_b):
    bsz, t = p.shape[0], p.shape[1]
    idx = np.cumsum([RWKV_WIDTH, RWKV_WIDTH, RWKV_WIDTH, LORA_W, LORA_A])
    r, k, v, zw, za, zg = jnp.split(p, idx, axis=-1)
    w_log = -jax.nn.softplus(-(w0 + jnp.tanh(zw) @ w2)) - 0.5
    decay = jnp.exp(-jnp.exp(w_log.astype(jnp.float32))).astype(p.dtype)
    a = jax.nn.sigmoid(a0 + za @ a2)
    g = jax.nn.sigmoid(zg) @ g2
    hs = (bsz, t, RWKV_HEADS, RWKV_HEAD)
    kk = (k * k_k).reshape(hs).astype(jnp.float32)
    kk = (kk / jnp.maximum(jnp.linalg.norm(kk, axis=-1, keepdims=True), L2_EPS)).astype(p.dtype)
    k = k * (1.0 + (a - 1.0) * k_a)
    rh, kh, vh, ah = r.reshape(hs), k.reshape(hs), v.reshape(hs), a.reshape(hs)
    y = wkv7_scan(rh, decay.reshape(hs), kh, vh, -kk, kk * ah)
    yf = y.astype(jnp.float32)
    mu = jnp.mean(yf, axis=-1, keepdims=True)
    var = jnp.mean(jnp.square(yf - mu), axis=-1, keepdims=True)
    yn = ((yf - mu) * lax.rsqrt(var + GN_EPS)).reshape(bsz, t, RWKV_WIDTH).astype(p.dtype)
    yn = yn * gn_w + gn_b
    bonus = jnp.sum(rh * kh * r_k, axis=-1, keepdims=True) * vh
    return (yn + bonus.reshape(bsz, t, RWKV_WIDTH)) * g


def chunked_sgu_group(p, ln_g, ln_b, w_s, b_s, out_g):
    bsz, t = p.shape[0], p.shape[1]
    z = jax.nn.gelu(p, approximate=False)
    u, v = jnp.split(z, 2, axis=-1)
    v = layer_norm(v, ln_g, ln_b)
    n_chunks = t // CHUNK
    v = v.reshape(bsz, n_chunks, CHUNK, GMLP_GROUPS, GMLP_GROUP_DIM)
    mask = jnp.tril(jnp.ones((CHUNK, CHUNK), dtype=w_s.dtype))
    w_causal = w_s * mask[None]
    s = jnp.einsum('gts,bcsgd->bctgd', w_causal, v) + jnp.swapaxes(b_s, 0, 1)[None, None, :, :, None]
    y = u * s.reshape(bsz, t, GMLP_WIDTH)
    return rms_norm(y, out_g)


def hybrid_layer(x, norm1_g, w_in, shift_mu, w0, w2, a0, a2, g2, k_k, k_a, r_k, gn_w, gn_b,
                 sgu_ln_g, sgu_ln_b, sgu_w, sgu_b, sgu_out_g, w_out, norm2_g, w_gate, w_up, w_down):
    h = rms_norm(x, norm1_g)
    p = h @ w_in
    p_rwkv = token_shift(p[..., :RWKV_IN], shift_mu)
    y_rwkv = rwkv7_group(p_rwkv, w0, w2, a0, a2, g2, k_k, k_a, r_k, gn_w, gn_b)
    y_sgu = chunked_sgu_group(p[..., RWKV_IN:], sgu_ln_g, sgu_ln_b, sgu_w, sgu_b, sgu_out_g)
    x = x + jnp.concatenate([y_rwkv, y_sgu], axis=-1) @ w_out
    h = rms_norm(x, norm2_g)
    x = x + (jax.nn.silu(h @ w_gate) * (h @ w_up)) @ w_down
    return x


def _fwd_setup_inputs(seed: int = 0) -> dict:
    key = jax.random.key(seed)
    ks = jax.random.split(key, 32)
    f32 = jnp.float32
    L = DEPTH

    def nrm(k, shape, scale):
        return jax.random.normal(k, shape, f32) * scale

    def gain(k, shape):
        return 1.0 + 0.01 * jax.random.normal(k, shape, f32)

    return {
        "x": jax.random.normal(ks[0], (BATCH, SEQ, D_MODEL), f32),
        "norm1_g": gain(ks[1], (L, D_MODEL)),
        "w_in": nrm(ks[2], (L, D_MODEL, D_IN), D_MODEL ** -0.5),
        "shift_mu": jax.random.uniform(ks[3], (L, RWKV_IN), f32),
        "w0": jax.random.uniform(ks[4], (L, RWKV_WIDTH), f32, -6.0, 0.0),
        "w2": nrm(ks[5], (L, LORA_W, RWKV_WIDTH), 0.1 * LORA_W ** -0.5),
        "a0": nrm(ks[6], (L, RWKV_WIDTH), 0.1),
        "a2": nrm(ks[7], (L, LORA_A, RWKV_WIDTH), 0.1 * LORA_A ** -0.5),
        "g2": nrm(ks[8], (L, LORA_G, RWKV_WIDTH), LORA_G ** -0.5),
        "k_k": 0.85 + 0.05 * jax.random.normal(ks[9], (L, RWKV_WIDTH), f32),
        "k_a": gain(ks[10], (L, RWKV_WIDTH)),
        "r_k": nrm(ks[11], (L, RWKV_HEADS, RWKV_HEAD), 0.1),
        "gn_w": gain(ks[12], (L, RWKV_WIDTH)),
        "gn_b": nrm(ks[13], (L, RWKV_WIDTH), 0.01),
        "sgu_ln_g": gain(ks[14], (L, GMLP_WIDTH)),
        "sgu_ln_b": nrm(ks[15], (L, GMLP_WIDTH), 0.01),
        "sgu_w": nrm(ks[16], (L, GMLP_GROUPS, CHUNK, CHUNK), CHUNK ** -0.5),
        "sgu_b": gain(ks[17], (L, GMLP_GROUPS, CHUNK)),
        "sgu_out_g": gain(ks[18], (L, GMLP_WIDTH)),
        "w_out": nrm(ks[19], (L, D_MIX, D_MODEL), D_MIX ** -0.5),
        "norm2_g": gain(ks[20], (L, D_MODEL)),
        "w_gate": nrm(ks[21], (L, D_MODEL, D_FF), D_MODEL ** -0.5),
        "w_up": nrm(ks[22], (L, D_MODEL, D_FF), D_MODEL ** -0.5),
        "w_down": nrm(ks[23], (L, D_FF, D_MODEL), D_FF ** -0.5),
        "final_g": gain(ks[24], (D_MODEL,)),
    }


def _fwd_reference(x, norm1_g, w_in, shift_mu, w0, w2, a0, a2, g2, k_k, k_a, r_k, gn_w, gn_b,
              sgu_ln_g, sgu_ln_b, sgu_w, sgu_b, sgu_out_g, w_out, norm2_g, w_gate, w_up, w_down,
              final_g):
    for i in range(DEPTH):
        x = hybrid_layer(x, norm1_g[i], w_in[i], shift_mu[i], w0[i], w2[i], a0[i], a2[i], g2[i],
                         k_k[i], k_a[i], r_k[i], gn_w[i], gn_b[i], sgu_ln_g[i], sgu_ln_b[i],
                         sgu_w[i], sgu_b[i], sgu_out_g[i], w_out[i], norm2_g[i],
                         w_gate[i], w_up[i], w_down[i])
    return rms_norm(x, final_g)


import jax as _jax
import jax.numpy as _jnp

TWIN_FORMAT = 'train_step'
FWD_PARAMS = ['x', 'norm1_g', 'w_in', 'shift_mu', 'w0', 'w2', 'a0', 'a2', 'g2', 'k_k', 'k_a', 'r_k', 'gn_w', 'gn_b', 'sgu_ln_g', 'sgu_ln_b', 'sgu_w', 'sgu_b', 'sgu_out_g', 'w_out', 'norm2_g', 'w_gate', 'w_up', 'w_down', 'final_g']
TWIN_WEIGHTS = ['norm1_g', 'w_in', 'shift_mu', 'w0', 'w2', 'a0', 'a2', 'g2', 'k_k', 'k_a', 'r_k', 'gn_w', 'gn_b', 'sgu_ln_g', 'sgu_ln_b', 'sgu_w', 'sgu_b', 'sgu_out_g', 'w_out', 'norm2_g', 'w_gate', 'w_up', 'w_down', 'final_g']
TWIN_DIFF_INPUT = 'x'
TWIN_INPUTS = ['x', 'norm1_g', 'w_in', 'shift_mu', 'w0', 'w2', 'a0', 'a2', 'g2', 'k_k', 'k_a', 'r_k', 'gn_w', 'gn_b', 'sgu_ln_g', 'sgu_ln_b', 'sgu_w', 'sgu_b', 'sgu_out_g', 'w_out', 'norm2_g', 'w_gate', 'w_up', 'w_down', 'final_g', 'loss_target', 'm_norm1_g', 'm_w_in', 'm_shift_mu', 'm_w0', 'm_w2', 'm_a0', 'm_a2', 'm_g2', 'm_k_k', 'm_k_a', 'm_r_k', 'm_gn_w', 'm_gn_b', 'm_sgu_ln_g', 'm_sgu_ln_b', 'm_sgu_w', 'm_sgu_b', 'm_sgu_out_g', 'm_w_out', 'm_norm2_g', 'm_w_gate', 'm_w_up', 'm_w_down', 'm_final_g', 'v_norm1_g', 'v_w_in', 'v_shift_mu', 'v_w0', 'v_w2', 'v_a0', 'v_a2', 'v_g2', 'v_k_k', 'v_k_a', 'v_r_k', 'v_gn_w', 'v_gn_b', 'v_sgu_ln_g', 'v_sgu_ln_b', 'v_sgu_w', 'v_sgu_b', 'v_sgu_out_g', 'v_w_out', 'v_norm2_g', 'v_w_gate', 'v_w_up', 'v_w_down', 'v_final_g']
TWIN_OUTPUTS = ['loss', 'grad_x', 'grad_norm1_g', 'grad_w_in', 'grad_shift_mu', 'grad_w0', 'grad_w2', 'grad_a0', 'grad_a2', 'grad_g2', 'grad_k_k', 'grad_k_a', 'grad_r_k', 'grad_gn_w', 'grad_gn_b', 'grad_sgu_ln_g', 'grad_sgu_ln_b', 'grad_sgu_w', 'grad_sgu_b', 'grad_sgu_out_g', 'grad_w_out', 'grad_norm2_g', 'grad_w_gate', 'grad_w_up', 'grad_w_down', 'grad_final_g', 'delta_norm1_g', 'delta_w_in', 'delta_shift_mu', 'delta_w0', 'delta_w2', 'delta_a0', 'delta_a2', 'delta_g2', 'delta_k_k', 'delta_k_a', 'delta_r_k', 'delta_gn_w', 'delta_gn_b', 'delta_sgu_ln_g', 'delta_sgu_ln_b', 'delta_sgu_w', 'delta_sgu_b', 'delta_sgu_out_g', 'delta_w_out', 'delta_norm2_g', 'delta_w_gate', 'delta_w_up', 'delta_w_down', 'delta_final_g', 'new_m_norm1_g', 'new_m_w_in', 'new_m_shift_mu', 'new_m_w0', 'new_m_w2', 'new_m_a0', 'new_m_a2', 'new_m_g2', 'new_m_k_k', 'new_m_k_a', 'new_m_r_k', 'new_m_gn_w', 'new_m_gn_b', 'new_m_sgu_ln_g', 'new_m_sgu_ln_b', 'new_m_sgu_w', 'new_m_sgu_b', 'new_m_sgu_out_g', 'new_m_w_out', 'new_m_norm2_g', 'new_m_w_gate', 'new_m_w_up', 'new_m_w_down', 'new_m_final_g', 'new_v_norm1_g', 'new_v_w_in', 'new_v_shift_mu', 'new_v_w0', 'new_v_w2', 'new_v_a0', 'new_v_a2', 'new_v_g2', 'new_v_k_k', 'new_v_k_a', 'new_v_r_k', 'new_v_gn_w', 'new_v_gn_b', 'new_v_sgu_ln_g', 'new_v_sgu_ln_b', 'new_v_sgu_w', 'new_v_sgu_b', 'new_v_sgu_out_g', 'new_v_w_out', 'new_v_norm2_g', 'new_v_w_gate', 'new_v_w_up', 'new_v_w_down', 'new_v_final_g']
TWIN_LEAF_KINDS = {'loss': 'loss', 'grad_x': 'grad_x', 'grad_norm1_g': 'grad_w', 'grad_w_in': 'grad_w', 'grad_shift_mu': 'grad_w', 'grad_w0': 'grad_w', 'grad_w2': 'grad_w', 'grad_a0': 'grad_w', 'grad_a2': 'grad_w', 'grad_g2': 'grad_w', 'grad_k_k': 'grad_w', 'grad_k_a': 'grad_w', 'grad_r_k': 'grad_w', 'grad_gn_w': 'grad_w', 'grad_gn_b': 'grad_w', 'grad_sgu_ln_g': 'grad_w', 'grad_sgu_ln_b': 'grad_w', 'grad_sgu_w': 'grad_w', 'grad_sgu_b': 'grad_w', 'grad_sgu_out_g': 'grad_w', 'grad_w_out': 'grad_w', 'grad_norm2_g': 'grad_w', 'grad_w_gate': 'grad_w', 'grad_w_up': 'grad_w', 'grad_w_down': 'grad_w', 'grad_final_g': 'grad_w', 'delta_norm1_g': 'delta_w', 'delta_w_in': 'delta_w', 'delta_shift_mu': 'delta_w', 'delta_w0': 'delta_w', 'delta_w2': 'delta_w', 'delta_a0': 'delta_w', 'delta_a2': 'delta_w', 'delta_g2': 'delta_w', 'delta_k_k': 'delta_w', 'delta_k_a': 'delta_w', 'delta_r_k': 'delta_w', 'delta_gn_w': 'delta_w', 'delta_gn_b': 'delta_w', 'delta_sgu_ln_g': 'delta_w', 'delta_sgu_ln_b': 'delta_w', 'delta_sgu_w': 'delta_w', 'delta_sgu_b': 'delta_w', 'delta_sgu_out_g': 'delta_w', 'delta_w_out': 'delta_w', 'delta_norm2_g': 'delta_w', 'delta_w_gate': 'delta_w', 'delta_w_up': 'delta_w', 'delta_w_down': 'delta_w', 'delta_final_g': 'delta_w', 'new_m_norm1_g': 'new_m', 'new_m_w_in': 'new_m', 'new_m_shift_mu': 'new_m', 'new_m_w0': 'new_m', 'new_m_w2': 'new_m', 'new_m_a0': 'new_m', 'new_m_a2': 'new_m', 'new_m_g2': 'new_m', 'new_m_k_k': 'new_m', 'new_m_k_a': 'new_m', 'new_m_r_k': 'new_m', 'new_m_gn_w': 'new_m', 'new_m_gn_b': 'new_m', 'new_m_sgu_ln_g': 'new_m', 'new_m_sgu_ln_b': 'new_m', 'new_m_sgu_w': 'new_m', 'new_m_sgu_b': 'new_m', 'new_m_sgu_out_g': 'new_m', 'new_m_w_out': 'new_m', 'new_m_norm2_g': 'new_m', 'new_m_w_gate': 'new_m', 'new_m_w_up': 'new_m', 'new_m_w_down': 'new_m', 'new_m_final_g': 'new_m', 'new_v_norm1_g': 'new_v', 'new_v_w_in': 'new_v', 'new_v_shift_mu': 'new_v', 'new_v_w0': 'new_v', 'new_v_w2': 'new_v', 'new_v_a0': 'new_v', 'new_v_a2': 'new_v', 'new_v_g2': 'new_v', 'new_v_k_k': 'new_v', 'new_v_k_a': 'new_v', 'new_v_r_k': 'new_v', 'new_v_gn_w': 'new_v', 'new_v_gn_b': 'new_v', 'new_v_sgu_ln_g': 'new_v', 'new_v_sgu_ln_b': 'new_v', 'new_v_sgu_w': 'new_v', 'new_v_sgu_b': 'new_v', 'new_v_sgu_out_g': 'new_v', 'new_v_w_out': 'new_v', 'new_v_norm2_g': 'new_v', 'new_v_w_gate': 'new_v', 'new_v_w_up': 'new_v', 'new_v_w_down': 'new_v', 'new_v_final_g': 'new_v'}


def _forward(args):
    return _fwd_reference(*[args[k] for k in FWD_PARAMS])


def _output_shape():
    out = _jax.eval_shape(lambda: _forward(_fwd_setup_inputs(0)))
    return out.shape, out.dtype

N_MICROBATCH = 1
ADAM_LR = 0.001
ADAM_B1 = 0.9
ADAM_B2 = 0.999
ADAM_EPS = 1e-08
ADAM_WD = 0.01
ADAM_STEP = 10
PER_EXAMPLE_BATCH_AXIS = {'x': 0, 'loss_target': 0}
SHARED_INPUTS = []
_WEIGHT_DTYPES = {'norm1_g': _jnp.float32, 'w_in': _jnp.float32, 'shift_mu': _jnp.float32, 'w0': _jnp.float32, 'w2': _jnp.float32, 'a0': _jnp.float32, 'a2': _jnp.float32, 'g2': _jnp.float32, 'k_k': _jnp.float32, 'k_a': _jnp.float32, 'r_k': _jnp.float32, 'gn_w': _jnp.float32, 'gn_b': _jnp.float32, 'sgu_ln_g': _jnp.float32, 'sgu_ln_b': _jnp.float32, 'sgu_w': _jnp.float32, 'sgu_b': _jnp.float32, 'sgu_out_g': _jnp.float32, 'w_out': _jnp.float32, 'norm2_g': _jnp.float32, 'w_gate': _jnp.float32, 'w_up': _jnp.float32, 'w_down': _jnp.float32, 'final_g': _jnp.float32}
MOMENT_SCALE = {'norm1_g': 5.887104e-02, 'w_in': 3.687043e-02, 'shift_mu': 4.893463e-02, 'w0': 1.189791e-02, 'w2': 1.400495e-03, 'a0': 1.281931e-02, 'a2': 1.211188e-02, 'g2': 3.028718e-02, 'k_k': 3.871358e-02, 'k_a': 3.335829e-02, 'r_k': 6.440379e-02, 'gn_w': 3.293960e-02, 'gn_b': 3.391407e-02, 'sgu_ln_g': 3.160673e-02, 'sgu_ln_b': 2.954524e-02, 'sgu_w': 2.092197e-02, 'sgu_b': 3.142921e-02, 'sgu_out_g': 5.308090e-02, 'w_out': 4.296780e-02, 'norm2_g': 3.775342e-02, 'w_gate': 1.641879e-02, 'w_up': 1.590859e-02, 'w_down': 2.634493e-02, 'final_g': 8.007456e+00}


def _to_microbatches(a, axis):
    t = _jnp.moveaxis(a, axis, 0)
    t = t.reshape((N_MICROBATCH, t.shape[0] // N_MICROBATCH) + t.shape[1:])
    return _jnp.moveaxis(t, 1, axis + 1)


def setup_inputs(seed: int = 0) -> dict:
    inp = _fwd_setup_inputs(seed)
    key = _jax.random.fold_in(_jax.random.key(seed), 7919)
    shape, _ = _output_shape()
    out = dict(inp)
    out["loss_target"] = _jax.random.normal(_jax.random.fold_in(key, 0), shape, _jnp.float32)
    for i, name in enumerate(TWIN_WEIGHTS):
        w = inp[name].astype(_jnp.float32)
        if MOMENT_SCALE is None:
            s = _jnp.sqrt(_jnp.mean(_jnp.square(w)) + 1e-30)
        else:
            s = MOMENT_SCALE[name]
        km, kv = _jax.random.split(_jax.random.fold_in(key, i + 1))
        out[name] = w
        out["m_" + name] = s * _jax.random.normal(km, w.shape, _jnp.float32)
        out["v_" + name] = (s * s) * _jax.random.uniform(kv, w.shape, _jnp.float32, 0.5, 1.5)
    if N_MICROBATCH > 1:
        for name, axis in PER_EXAMPLE_BATCH_AXIS.items():
            out[name] = _to_microbatches(out[name], axis)
    return {'x': out['x'], 'norm1_g': out['norm1_g'], 'w_in': out['w_in'], 'shift_mu': out['shift_mu'], 'w0': out['w0'], 'w2': out['w2'], 'a0': out['a0'], 'a2': out['a2'], 'g2': out['g2'], 'k_k': out['k_k'], 'k_a': out['k_a'], 'r_k': out['r_k'], 'gn_w': out['gn_w'], 'gn_b': out['gn_b'], 'sgu_ln_g': out['sgu_ln_g'], 'sgu_ln_b': out['sgu_ln_b'], 'sgu_w': out['sgu_w'], 'sgu_b': out['sgu_b'], 'sgu_out_g': out['sgu_out_g'], 'w_out': out['w_out'], 'norm2_g': out['norm2_g'], 'w_gate': out['w_gate'], 'w_up': out['w_up'], 'w_down': out['w_down'], 'final_g': out['final_g'], 'loss_target': out['loss_target'], 'm_norm1_g': out['m_norm1_g'], 'm_w_in': out['m_w_in'], 'm_shift_mu': out['m_shift_mu'], 'm_w0': out['m_w0'], 'm_w2': out['m_w2'], 'm_a0': out['m_a0'], 'm_a2': out['m_a2'], 'm_g2': out['m_g2'], 'm_k_k': out['m_k_k'], 'm_k_a': out['m_k_a'], 'm_r_k': out['m_r_k'], 'm_gn_w': out['m_gn_w'], 'm_gn_b': out['m_gn_b'], 'm_sgu_ln_g': out['m_sgu_ln_g'], 'm_sgu_ln_b': out['m_sgu_ln_b'], 'm_sgu_w': out['m_sgu_w'], 'm_sgu_b': out['m_sgu_b'], 'm_sgu_out_g': out['m_sgu_out_g'], 'm_w_out': out['m_w_out'], 'm_norm2_g': out['m_norm2_g'], 'm_w_gate': out['m_w_gate'], 'm_w_up': out['m_w_up'], 'm_w_down': out['m_w_down'], 'm_final_g': out['m_final_g'], 'v_norm1_g': out['v_norm1_g'], 'v_w_in': out['v_w_in'], 'v_shift_mu': out['v_shift_mu'], 'v_w0': out['v_w0'], 'v_w2': out['v_w2'], 'v_a0': out['v_a0'], 'v_a2': out['v_a2'], 'v_g2': out['v_g2'], 'v_k_k': out['v_k_k'], 'v_k_a': out['v_k_a'], 'v_r_k': out['v_r_k'], 'v_gn_w': out['v_gn_w'], 'v_gn_b': out['v_gn_b'], 'v_sgu_ln_g': out['v_sgu_ln_g'], 'v_sgu_ln_b': out['v_sgu_ln_b'], 'v_sgu_w': out['v_sgu_w'], 'v_sgu_b': out['v_sgu_b'], 'v_sgu_out_g': out['v_sgu_out_g'], 'v_w_out': out['v_w_out'], 'v_norm2_g': out['v_norm2_g'], 'v_w_gate': out['v_w_gate'], 'v_w_up': out['v_w_up'], 'v_w_down': out['v_w_down'], 'v_final_g': out['v_final_g']}


def _loss(weights, diff, rest, loss_target):
    with _jax.named_scope("forward"):
        args = {**rest, TWIN_DIFF_INPUT: diff, **{k: w.astype(_WEIGHT_DTYPES[k]) for k, w in weights.items()}}
        y = _forward(args)
    with _jax.named_scope("loss_head"):
        err = _jnp.square(y.astype(_jnp.float32) - loss_target)
        return 0.5 * _jnp.sum(_jnp.mean(err, axis=-1)) if err.ndim else 0.5 * err


def _adamw(w, g, m, v):
    m = ADAM_B1 * m + (1.0 - ADAM_B1) * g
    v = ADAM_B2 * v + (1.0 - ADAM_B2) * _jnp.square(g)
    m_hat = m / (1.0 - ADAM_B1 ** ADAM_STEP)
    v_hat = v / (1.0 - ADAM_B2 ** ADAM_STEP)
    delta = -ADAM_LR * (m_hat / (_jnp.sqrt(v_hat) + ADAM_EPS) + ADAM_WD * w)
    return delta, m, v


def reference(x, norm1_g, w_in, shift_mu, w0, w2, a0, a2, g2, k_k, k_a, r_k, gn_w, gn_b, sgu_ln_g, sgu_ln_b, sgu_w, sgu_b, sgu_out_g, w_out, norm2_g, w_gate, w_up, w_down, final_g, loss_target, m_norm1_g, m_w_in, m_shift_mu, m_w0, m_w2, m_a0, m_a2, m_g2, m_k_k, m_k_a, m_r_k, m_gn_w, m_gn_b, m_sgu_ln_g, m_sgu_ln_b, m_sgu_w, m_sgu_b, m_sgu_out_g, m_w_out, m_norm2_g, m_w_gate, m_w_up, m_w_down, m_final_g, v_norm1_g, v_w_in, v_shift_mu, v_w0, v_w2, v_a0, v_a2, v_g2, v_k_k, v_k_a, v_r_k, v_gn_w, v_gn_b, v_sgu_ln_g, v_sgu_ln_b, v_sgu_w, v_sgu_b, v_sgu_out_g, v_w_out, v_norm2_g, v_w_gate, v_w_up, v_w_down, v_final_g):
    given = dict(x=x, norm1_g=norm1_g, w_in=w_in, shift_mu=shift_mu, w0=w0, w2=w2, a0=a0, a2=a2, g2=g2, k_k=k_k, k_a=k_a, r_k=r_k, gn_w=gn_w, gn_b=gn_b, sgu_ln_g=sgu_ln_g, sgu_ln_b=sgu_ln_b, sgu_w=sgu_w, sgu_b=sgu_b, sgu_out_g=sgu_out_g, w_out=w_out, norm2_g=norm2_g, w_gate=w_gate, w_up=w_up, w_down=w_down, final_g=final_g, loss_target=loss_target, m_norm1_g=m_norm1_g, m_w_in=m_w_in, m_shift_mu=m_shift_mu, m_w0=m_w0, m_w2=m_w2, m_a0=m_a0, m_a2=m_a2, m_g2=m_g2, m_k_k=m_k_k, m_k_a=m_k_a, m_r_k=m_r_k, m_gn_w=m_gn_w, m_gn_b=m_gn_b, m_sgu_ln_g=m_sgu_ln_g, m_sgu_ln_b=m_sgu_ln_b, m_sgu_w=m_sgu_w, m_sgu_b=m_sgu_b, m_sgu_out_g=m_sgu_out_g, m_w_out=m_w_out, m_norm2_g=m_norm2_g, m_w_gate=m_w_gate, m_w_up=m_w_up, m_w_down=m_w_down, m_final_g=m_final_g, v_norm1_g=v_norm1_g, v_w_in=v_w_in, v_shift_mu=v_shift_mu, v_w0=v_w0, v_w2=v_w2, v_a0=v_a0, v_a2=v_a2, v_g2=v_g2, v_k_k=v_k_k, v_k_a=v_k_a, v_r_k=v_r_k, v_gn_w=v_gn_w, v_gn_b=v_gn_b, v_sgu_ln_g=v_sgu_ln_g, v_sgu_ln_b=v_sgu_ln_b, v_sgu_w=v_sgu_w, v_sgu_b=v_sgu_b, v_sgu_out_g=v_sgu_out_g, v_w_out=v_w_out, v_norm2_g=v_norm2_g, v_w_gate=v_w_gate, v_w_up=v_w_up, v_w_down=v_w_down, v_final_g=v_final_g)
    weights = {n: given[n] for n in TWIN_WEIGHTS}
    shared = {n: given[n] for n in SHARED_INPUTS}
    per_example = {n: given[n] for n in ['x']}
    grad_fn = _jax.value_and_grad(_loss, argnums=(0, 1))

    def one_microbatch(ex, loss_target):
        ex = dict(ex)
        diff = ex.pop(TWIN_DIFF_INPUT)
        return grad_fn(weights, diff, {**shared, **ex}, loss_target)

    if N_MICROBATCH == 1:
        loss, (grad_w, grad_x) = one_microbatch(per_example, given["loss_target"])
    else:
        def body(carry, xs):
            loss_sum, grad_sum = carry
            l_k, (gw_k, gx_k) = one_microbatch(xs[0], xs[1])
            with _jax.named_scope("update"):
                return (loss_sum + l_k, _jax.tree.map(_jnp.add, grad_sum, gw_k)), gx_k

        init = (_jnp.zeros((), _jnp.float32), _jax.tree.map(_jnp.zeros_like, weights))
        (loss, grad_w), grad_x = _jax.lax.scan(body, init, (per_example, given["loss_target"]))
    with _jax.named_scope("update"):
        delta_w, new_m, new_v = {}, {}, {}
        for n in TWIN_WEIGHTS:
            delta_w[n], new_m[n], new_v[n] = _adamw(weights[n], grad_w[n], given["m_" + n], given["v_" + n])
    return (loss, grad_x, *[grad_w[n] for n in TWIN_WEIGHTS], *[delta_w[n] for n in TWIN_WEIGHTS],
            *[new_m[n] for n in TWIN_WEIGHTS], *[new_v[n] for n in TWIN_WEIGHTS])
```

```python
import functools

import numpy as np
import jax
import jax.numpy as jnp
from jax import lax
from jax.experimental import pallas as pl
from jax.experimental.pallas import tpu as pltpu

F32 = jnp.float32
BF16 = jnp.bfloat16

D = 2048
C = 1024
HEADS = 16
HEAD = 64
LANES = 128
P_W = 5632
P_LORA = 5120
LORA_W = 512
D_IN = 5408
D_FF = 5632
CHUNK = 128
RMS_EPS = 1e-6
LN_EPS = 1e-5
GN_EPS = 64e-5
L2_EPS = 1e-12
VMEM_LIMIT = 56 * 1024 * 1024


def _pcall(body, **kw):
    return pl.pallas_call(body, **kw)


def _cparams(sem):
    return pltpu.CompilerParams(dimension_semantics=sem, vmem_limit_bytes=VMEM_LIMIT)


def _tile(n, most):
    t = most
    while t > 8 and n % t:
        t //= 2
    assert n % t == 0, (n, most)
    return t


def mm(a, b, *, ta=False, tb=False, res=None, out_dtype=F32, name):
    m, k = (a.shape[1], a.shape[0]) if ta else a.shape
    n = b.shape[0] if tb else b.shape[1]
    assert (b.shape[1] if tb else b.shape[0]) == k
    tm = _tile(m, 1024)
    tn, tk = _tile(n, 512), _tile(k, 512)
    nk = k // tk
    dims = (((0 if ta else 1,), (1 if tb else 0,)), ((), ()))

    def body(*refs):
        if res is None:
            a_ref, b_ref, o_ref, acc_ref = refs
        else:
            a_ref, b_ref, r_ref, o_ref, acc_ref = refs
        kk = pl.program_id(2)

        @pl.when(kk == 0)
        def _():
            acc_ref[...] = jnp.zeros_like(acc_ref)

        acc_ref[...] += lax.dot_general(a_ref[...].astype(BF16), b_ref[...].astype(BF16), dims,
                                        preferred_element_type=F32)

        @pl.when(kk == nk - 1)
        def _():
            acc = acc_ref[...]
            if res is not None:
                acc = acc + r_ref[...].astype(F32)
            o_ref[...] = acc.astype(out_dtype)

    a_spec = pl.BlockSpec((tk, tm), lambda i, j, l: (l, i)) if ta else pl.BlockSpec((tm, tk), lambda i, j, l: (i, l))
    b_spec = pl.BlockSpec((tn, tk), lambda i, j, l: (j, l)) if tb else pl.BlockSpec((tk, tn), lambda i, j, l: (l, j))
    o_spec = pl.BlockSpec((tm, tn), lambda i, j, l: (i, j))
    in_specs = [a_spec, b_spec] + ([o_spec] if res is not None else [])
    args = (a, b) + ((res,) if res is not None else ())
    return _pcall(
        body, name=name, grid=(m // tm, n // tn, nk), in_specs=in_specs, out_specs=o_spec,
        out_shape=jax.ShapeDtypeStruct((m, n), out_dtype), scratch_shapes=[pltpu.VMEM((tm, tn), F32)],
        compiler_params=_cparams(("parallel", "parallel", "arbitrary")),
    )(*args)


class Tiled:
    def __init__(self, arr, width=None, col=0):
        self.arr, self.width, self.col = arr, (arr.shape[1] if width is None else width), col

    def spec(self, tm):
        col = self.col
        return pl.BlockSpec((tm, self.width), lambda i: (i, col))


def _full_spec(p):
    nd = p.ndim
    return pl.BlockSpec(p.shape, lambda i: (0,) * nd)


def row_fwd(name, fn, tiled, params, consts, outs, tm):
    t = tiled[0].arr.shape[0]
    n_in = len(tiled) + len(params) + len(consts)

    def body(*refs):
        res = fn(*[r[...] for r in refs[:n_in]])
        for o_ref, r in zip(refs[n_in:], res):
            o_ref[...] = r.astype(o_ref.dtype)

    return _pcall(
        body, name=name, grid=(t // tm,),
        in_specs=[x.spec(tm) for x in tiled] + [_full_spec(p) for p in params + consts],
        out_specs=[pl.BlockSpec((tm, w), lambda i: (i, 0)) for w, _ in outs],
        out_shape=[jax.ShapeDtypeStruct((t, w), dt) for w, dt in outs],
        compiler_params=_cparams(("parallel",)),
    )(*[x.arr for x in tiled], *params, *consts)


def row_bwd(name, fn, tiled, params, consts, cts, tm, tiled_out_dtypes=None):
    t = tiled[0].arr.shape[0]
    nt, npar, ncon = len(tiled), len(params), len(consts)
    cts = [c if isinstance(c, tuple) else (c,) for c in cts]
    flat_cts = [x for c in cts for x in c]
    tiled_out_dtypes = tiled_out_dtypes or [F32] * nt

    def body(*refs):
        n_in = nt + npar + ncon
        ins = [r[...].astype(F32) for r in refs[:nt + npar]]
        con = [r[...] for r in refs[nt + npar:n_in]]
        ct_refs = list(refs[n_in:n_in + len(flat_cts)])
        o = refs[n_in + len(flat_cts):]
        ct = []
        for c in cts:
            parts = [ct_refs.pop(0)[...].astype(F32) for _ in c]
            ct.append(functools.reduce(lambda p, q: p + q, parts))
        _, vjp = jax.vjp(lambda *a: fn(*a, *con), *ins)
        g = vjp(tuple(ct))
        for j in range(nt):
            o[j][...] = g[j].astype(o[j].dtype)
        first = pl.program_id(0) == 0

        @pl.when(first)
        def _():
            for j in range(npar):
                o[nt + j][...] = g[nt + j]

        @pl.when(jnp.logical_not(first))
        def _():
            for j in range(npar):
                o[nt + j][...] += g[nt + j]

    return _pcall(
        body, name=name, grid=(t // tm,),
        in_specs=[x.spec(tm) for x in tiled] + [_full_spec(p) for p in params + consts]
        + [x.spec(tm) for x in flat_cts],
        out_specs=[pl.BlockSpec((tm, x.width), lambda i: (i, 0)) for x in tiled] + [_full_spec(p) for p in params],
        out_shape=[jax.ShapeDtypeStruct((t, x.width), dt) for x, dt in zip(tiled, tiled_out_dtypes)]
        + [jax.ShapeDtypeStruct(p.shape, F32) for p in params],
        compiler_params=_cparams(("arbitrary",)),
    )(*[x.arr for x in tiled], *params, *consts, *[x.arr for x in flat_cts])


def _split_dot(x, w):
    hi = x.astype(BF16)
    lo = (x - hi.astype(F32)).astype(BF16)
    return jnp.dot(hi, w, preferred_element_type=F32) + jnp.dot(lo, w, preferred_element_type=F32)


@jax.custom_vjp
def headsum(x, j):
    return _split_dot(x, j)


def _headsum_fwd(x, j):
    return _split_dot(x, j), j


def _headsum_bwd(j, ct):
    return _split_dot(ct, j), jnp.zeros_like(j)


headsum.defvjp(_headsum_fwd, _headsum_bwd)


def _bdot(x, w):
    return jnp.dot(x.astype(BF16), w.astype(BF16), preferred_element_type=F32)


def _sigmoid(x):
    return 1.0 / (1.0 + jnp.exp(-x))


def _softplus(x):
    return jnp.maximum(x, 0.0) + jnp.log(1.0 + jnp.exp(-jnp.abs(x)))


def rms_fn(x, g):
    return (x * lax.rsqrt(jnp.mean(x * x, axis=-1, keepdims=True) + RMS_EPS) * g,)


def rms_res_fn(x, g):
    return rms_fn(x, g)[0], x


def pre_fn(rkv, lora, w0, w2p, a0, a2p, g2p, k_k, k_a, jm):
    r, k, v = rkv[:, :C], rkv[:, C:2 * C], rkv[:, 2 * C:]
    zwa, zg = lora[:, :LANES], lora[:, LANES:LANES + 256]
    w_log = -_softplus(-(w0 + _bdot(jnp.tanh(zwa), w2p))) - 0.5
    decay = jnp.exp(-jnp.exp(w_log))
    a = _sigmoid(a0 + _bdot(zwa, a2p))
    g = _bdot(_sigmoid(zg), g2p)
    kk = k * k_k
    kk = kk / jnp.maximum(jnp.sqrt(headsum(kk * kk, jm)), L2_EPS)
    k2 = k * (1.0 + (a - 1.0) * k_a)
    return r, decay, k2, v, -kk, kk * a, g


def post_fn(y, r, k2, v, g, gn_w, gn_b, r_k, jm):
    mu = headsum(y, jm) * (1.0 / HEAD)
    yc = y - mu
    var = headsum(yc * yc, jm) * (1.0 / HEAD)
    yn = yc * lax.rsqrt(var + GN_EPS) * gn_w + gn_b
    bonus = headsum(r * k2 * r_k, jm) * v
    return ((yn + bonus) * g,)


def _gelu(x):
    return 0.5 * x * (1.0 + lax.erf(x * np.float32(1.0 / np.sqrt(2.0))))


@jax.custom_vjp
def expand_groups(b_t, e, e_t):
    return _split_dot(b_t, e)


def _expand_groups_fwd(b_t, e, e_t):
    return _split_dot(b_t, e), (e, e_t)


def _expand_groups_bwd(res, ct):
    e, e_t = res
    return _split_dot(ct, e_t), jnp.zeros_like(e), jnp.zeros_like(e_t)


expand_groups.defvjp(_expand_groups_fwd, _expand_groups_bwd)


def sgu_fn(pu, pv, ln_g, ln_b, w_s, b_t, out_g, e, e_t):
    rows = pu.shape[0]
    b_exp = expand_groups(b_t, e, e_t)
    u, v = _gelu(pu), _gelu(pv)
    mu = jnp.mean(v, axis=-1, keepdims=True)
    vc = v - mu
    var = jnp.mean(vc * vc, axis=-1, keepdims=True)
    v = vc * lax.rsqrt(var + LN_EPS) * ln_g + ln_b
    tri = lax.broadcasted_iota(jnp.int32, (CHUNK, CHUNK), 0) >= lax.broadcasted_iota(jnp.int32, (CHUNK, CHUNK), 1)
    left = lax.broadcasted_iota(jnp.int32, (CHUNK, LANES), 1) < HEAD
    chunks = []
    for c0 in range(0, rows, CHUNK):
        cols = []
        for gp in range(HEADS // 2):
            vp = v[c0:c0 + CHUNK, gp * LANES:(gp + 1) * LANES]
            wa = jnp.where(tri, w_s[2 * gp], 0.0)
            wb = jnp.where(tri, w_s[2 * gp + 1], 0.0)
            cols.append(jnp.where(left, _bdot(wa, vp), _bdot(wb, vp)))
        chunks.append(jnp.concatenate(cols, axis=1) + b_exp)
    s = jnp.concatenate(chunks, axis=0) if len(chunks) > 1 else chunks[0]
    y = u * s
    return (y * lax.rsqrt(jnp.mean(y * y, axis=-1, keepdims=True) + RMS_EPS) * out_g,)


def swiglu_fn(gate, up):
    return (gate * _sigmoid(gate) * up,)


SCAN_CHUNK = 32
NB = C // LANES


def to_cols(a):
    t = a.shape[0]
    return a.reshape(t // 8, 8, NB, 2, HEAD).transpose(0, 4, 3, 1, 2).reshape(t // 8, HEAD, LANES)


def from_cols(a):
    t8 = a.shape[0]
    return a.reshape(t8, HEAD, 2, 8, NB).transpose(0, 3, 4, 2, 1).reshape(t8 * 8, C)


def _j2():
    i = np.arange(LANES)
    return jnp.asarray((i[:, None] // HEAD) == (i[None, :] // HEAD), BF16)


def _split(x):
    hi = x.astype(BF16)
    return hi, (x - hi.astype(F32)).astype(BF16)


def _dot2(hi, lo, j2):
    return jnp.dot(hi, j2, preferred_element_type=F32) + jnp.dot(lo, j2, preferred_element_type=F32)


def _expand(hi, lo, mask, j2):
    zero = jnp.zeros_like(hi)
    return _dot2(jnp.where(mask, hi, zero), jnp.where(mask, lo, zero), j2)


def _row(tile, j, sl):
    return jnp.broadcast_to(tile[j:j + 1, sl], (HEAD, LANES))


def _scan_step(s_ref, rows, vhi, vlo, lane63, j2, j, hist=None):
    w_t, k_t, a_t, b_t, r_t = rows
    out = []
    for cb in range(NB):
        sl = slice(cb * LANES, (cb + 1) * LANES)
        mask = lane63 == (j * 8 + cb)
        s = s_ref[:, sl]
        vexp = _expand(vhi, vlo, mask, j2)
        sab = _dot2(*_split(s * _row(a_t, j, sl)), j2)
        s_new = s * _row(w_t, j, sl) + sab * _row(b_t, j, sl) + vexp * _row(k_t, j, sl)
        s_ref[:, sl] = s_new
        if hist is None:
            out.append((mask, _dot2(*_split(s_new * _row(r_t, j, sl)), j2)))
        else:
            s_hist, sab_hist, idx = hist
            s_hist[idx + 1, :, sl] = s_new
            sab_hist[idx, :, sl] = sab
    return out


def wkv_fwd(r, w, k, v3, a, b):
    t = r.shape[0]
    nc = t // SCAN_CHUNK
    n8 = SCAN_CHUNK // 8

    def body(r_ref, w_ref, k_ref, a_ref, b_ref, v3_ref, j2_ref, y3_ref, ck_ref, s_ref):
        @pl.when(pl.program_id(0) == 0)
        def _():
            s_ref[...] = jnp.zeros_like(s_ref)

        ck_ref[0] = s_ref[...]
        j2 = j2_ref[...]
        lane63 = lax.broadcasted_iota(jnp.int32, (HEAD, LANES), 1) & (HEAD - 1)

        def t8_body(t8, carry):
            row0 = pl.multiple_of(t8 * 8, 8)
            rows = [ref[pl.ds(row0, 8), :] for ref in (w_ref, k_ref, a_ref, b_ref, r_ref)]
            vhi, vlo = _split(v3_ref[t8])
            y3 = jnp.zeros((HEAD, LANES), F32)
            for j in range(8):
                for mask, yb in _scan_step(s_ref, rows, vhi, vlo, lane63, j2, j):
                    y3 = jnp.where(mask, yb, y3)
            y3_ref[t8] = y3
            return carry

        lax.fori_loop(0, n8, t8_body, 0)

    row_spec = pl.BlockSpec((SCAN_CHUNK, C), lambda i: (i, 0))
    col_spec = pl.BlockSpec((n8, HEAD, LANES), lambda i: (i, 0, 0))
    return _pcall(
        body, name="wkv_fwd", grid=(nc,),
        in_specs=[row_spec] * 5 + [col_spec, pl.BlockSpec((LANES, LANES), lambda i: (0, 0))],
        out_specs=[col_spec, pl.BlockSpec((1, HEAD, C), lambda i: (i, 0, 0))],
        out_shape=[jax.ShapeDtypeStruct((t // 8, HEAD, LANES), F32), jax.ShapeDtypeStruct((nc, HEAD, C), F32)],
        scratch_shapes=[pltpu.VMEM((HEAD, C), F32)],
        compiler_params=_cparams(("arbitrary",)),
    )(r, w, k, a, b, v3, _j2())


def wkv_bwd(r, w, k, v3, a, b, ckpt, dy3):
    t = r.shape[0]
    nc = t // SCAN_CHUNK
    n8 = SCAN_CHUNK // 8

    def body(r_ref, w_ref, k_ref, a_ref, b_ref, v3_ref, dy3_ref, ck_ref, j2_ref,
             dr_ref, dw_ref, dk_ref, da_ref, db_ref, dv3_ref, s_ref, g_ref, s_hist, sab_hist):
        @pl.when(pl.program_id(0) == 0)
        def _():
            g_ref[...] = jnp.zeros_like(g_ref)

        j2 = j2_ref[...]
        lane63 = lax.broadcasted_iota(jnp.int32, (HEAD, LANES), 1) & (HEAD - 1)
        sub = lax.broadcasted_iota(jnp.int32, (8, LANES), 0)
        s_ref[...] = ck_ref[0]
        s_hist[0] = ck_ref[0]

        def redo(t8, carry):
            row0 = pl.multiple_of(t8 * 8, 8)
            rows = [ref[pl.ds(row0, 8), :] for ref in (w_ref, k_ref, a_ref, b_ref, r_ref)]
            vhi, vlo = _split(v3_ref[t8])
            for j in range(8):
                _scan_step(s_ref, rows, vhi, vlo, lane63, j2, j, hist=(s_hist, sab_hist, t8 * 8 + j))
            return carry

        lax.fori_loop(0, n8, redo, 0)

        def back(q, carry):
            t8 = n8 - 1 - q
            row0 = pl.multiple_of(t8 * 8, 8)
            w_t, k_t, a_t, b_t, r_t = [ref[pl.ds(row0, 8), :] for ref in (w_ref, k_ref, a_ref, b_ref, r_ref)]
            vhi, vlo = _split(v3_ref[t8])
            yhi, ylo = _split(dy3_ref[t8])
            dv3 = jnp.zeros((HEAD, LANES), F32)
            tiles = [[jnp.zeros((8, LANES), F32) for _ in range(NB)] for _ in range(5)]
            for j in range(7, -1, -1):
                idx = t8 * 8 + j
                for cb in range(NB):
                    sl = slice(cb * LANES, (cb + 1) * LANES)
                    mask = lane63 == (j * 8 + cb)
                    s_new = s_hist[idx + 1, :, sl]
                    s_old = s_hist[idx, :, sl]
                    sab = sab_hist[idx, :, sl]
                    dyexp = _expand(yhi, ylo, mask, j2)
                    vexp = _expand(vhi, vlo, mask, j2)
                    g = g_ref[:, sl] + dyexp * _row(r_t, j, sl)
                    dsab = _dot2(*_split(g * _row(b_t, j, sl)), j2)
                    dv3 = jnp.where(mask, _dot2(*_split(g * _row(k_t, j, sl)), j2), dv3)
                    sums = (s_new * dyexp, g * s_old, g * vexp, s_old * dsab, g * sab)
                    for n, prod in enumerate(sums):
                        rowsum = jnp.broadcast_to(jnp.sum(prod, axis=0, keepdims=True), (8, LANES))
                        tiles[n][cb] = jnp.where(sub == j, rowsum, tiles[n][cb])
                    g_ref[:, sl] = g * _row(w_t, j, sl) + dsab * _row(a_t, j, sl)
            dv3_ref[t8] = dv3
            for n, ref in enumerate((dr_ref, dw_ref, dk_ref, da_ref, db_ref)):
                for cb in range(NB):
                    ref[pl.ds(row0, 8), cb * LANES:(cb + 1) * LANES] = tiles[n][cb]
            return carry

        lax.fori_loop(0, n8, back, 0)

    row_spec = pl.BlockSpec((SCAN_CHUNK, C), lambda i: (nc - 1 - i, 0))
    col_spec = pl.BlockSpec((n8, HEAD, LANES), lambda i: (nc - 1 - i, 0, 0))
    return _pcall(
        body, name="wkv_bwd", grid=(nc,),
        in_specs=[row_spec] * 5 + [col_spec, col_spec, pl.BlockSpec((1, HEAD, C), lambda i: (nc - 1 - i, 0, 0)),
                                   pl.BlockSpec((LANES, LANES), lambda i: (0, 0))],
        out_specs=[row_spec] * 5 + [col_spec],
        out_shape=[jax.ShapeDtypeStruct((t, C), F32)] * 5 + [jax.ShapeDtypeStruct((t // 8, HEAD, LANES), F32)],
        scratch_shapes=[pltpu.VMEM((HEAD, C), F32), pltpu.VMEM((HEAD, C), F32),
                        pltpu.VMEM((SCAN_CHUNK + 1, HEAD, C), F32), pltpu.VMEM((SCAN_CHUNK, HEAD, C), F32)],
        compiler_params=_cparams(("arbitrary",)),
    )(r, w, k, a, b, v3, dy3, ckpt, _j2())


def _prev_rows(cur, before, first_tile):
    last = jnp.where(first_tile, 0.0, before[7:8, :])
    row = lax.broadcasted_iota(jnp.int32, cur.shape, 0)
    return jnp.where(row == 0, last, pltpu.roll(cur, 1, 0))


def shift_fwd(p, mu, col, name):
    t, width = p.shape[0], mu.shape[1]
    tm = _tile(t, 256)

    def body(p_ref, before_ref, mu_ref, o_ref):
        cur = p_ref[...]
        prev = _prev_rows(cur, before_ref[...], pl.program_id(0) == 0)
        o_ref[...] = cur + (prev - cur) * mu_ref[...]

    return _pcall(
        body, name=name, grid=(t // tm,),
        in_specs=[pl.BlockSpec((tm, width), lambda i: (i, col)),
                  pl.BlockSpec((8, width), lambda i: (jnp.maximum(i * (tm // 8) - 1, 0), col)),
                  pl.BlockSpec((1, width), lambda i: (0, 0))],
        out_specs=pl.BlockSpec((tm, width), lambda i: (i, 0)),
        out_shape=jax.ShapeDtypeStruct((t, width), F32),
        compiler_params=_cparams(("parallel",)),
    )(p, p, mu)


def shift_bwd(dps, p, mu, col, name):
    t, width = dps.shape
    tm = _tile(t, 256)
    nt = t // tm

    def body(d_ref, after_ref, p_ref, before_ref, mu_ref, dp_ref, dmu_ref):
        i = pl.program_id(0)
        d, cur, mu_v = d_ref[...], p_ref[...], mu_ref[...]
        row = lax.broadcasted_iota(jnp.int32, d.shape, 0)
        first_after = jnp.where(i == nt - 1, 0.0, after_ref[0:1, :])
        nxt = jnp.where(row == tm - 1, first_after, pltpu.roll(d, tm - 1, 0))
        dp_ref[...] = d * (1.0 - mu_v) + nxt * mu_v
        prev = _prev_rows(cur, before_ref[...], i == 0)
        part = jnp.sum(d * (prev - cur), axis=0, keepdims=True)

        @pl.when(i == 0)
        def _():
            dmu_ref[...] = part

        @pl.when(i > 0)
        def _():
            dmu_ref[...] += part

    return _pcall(
        body, name=name, grid=(nt,),
        in_specs=[pl.BlockSpec((tm, width), lambda i: (i, 0)),
                  pl.BlockSpec((8, width), lambda i: (jnp.minimum((i + 1) * (tm // 8), t // 8 - 1), 0)),
                  pl.BlockSpec((tm, width), lambda i: (i, col)),
                  pl.BlockSpec((8, width), lambda i: (jnp.maximum(i * (tm // 8) - 1, 0), col)),
                  pl.BlockSpec((1, width), lambda i: (0, 0))],
        out_specs=[pl.BlockSpec((tm, width), lambda i: (i, 0)), pl.BlockSpec((1, width), lambda i: (0, 0))],
        out_shape=[jax.ShapeDtypeStruct((t, width), F32), jax.ShapeDtypeStruct((1, width), F32)],
        compiler_params=_cparams(("arbitrary",)),
    )(dps, dps, p, p, mu)


def loss_head(x3, tgt, g):
    t, d = x3.shape
    tm = _tile(t, 256)

    def body(x_ref, t_ref, g_ref, loss_ref, dx_ref, dg_ref):
        (y,), vjp = jax.vjp(lambda a, b: rms_fn(a, b), x_ref[...], g_ref[...])
        diff = y - t_ref[...]
        part = 0.5 * jnp.sum(jnp.mean(diff * diff, axis=-1, keepdims=True), axis=0, keepdims=True)
        dx, dg = vjp((diff * (1.0 / d),))
        dx_ref[...] = dx
        part = jnp.broadcast_to(part, (1, LANES))
        first = pl.program_id(0) == 0

        @pl.when(first)
        def _():
            loss_ref[...] = part
            dg_ref[...] = dg

        @pl.when(jnp.logical_not(first))
        def _():
            loss_ref[...] += part
            dg_ref[...] += dg

    row = pl.BlockSpec((tm, d), lambda i: (i, 0))
    vec = pl.BlockSpec((1, d), lambda i: (0, 0))
    return _pcall(
        body, name="loss_head", grid=(t // tm,), in_specs=[row, row, vec],
        out_specs=[pl.BlockSpec((1, LANES), lambda i: (0, 0)), row, vec],
        out_shape=[jax.ShapeDtypeStruct((1, LANES), F32), jax.ShapeDtypeStruct((t, d), F32),
                   jax.ShapeDtypeStruct((1, d), F32)],
        compiler_params=_cparams(("arbitrary",)),
    )(x3, tgt, g)


ADAM_LR, ADAM_B1, ADAM_B2, ADAM_EPS, ADAM_WD, ADAM_STEP = 0.001, 0.9, 0.999, 1e-08, 0.01, 10


def adamw(w, g, m, v, name):
    rows, width = w.shape
    tr = _tile(rows, 256)

    def body(w_ref, g_ref, m_ref, v_ref, d_ref, nm_ref, nv_ref):
        gv = g_ref[...]
        m_new = ADAM_B1 * m_ref[...] + (1.0 - ADAM_B1) * gv
        v_new = ADAM_B2 * v_ref[...] + (1.0 - ADAM_B2) * (gv * gv)
        m_hat = m_new / (1.0 - ADAM_B1 ** ADAM_STEP)
        v_hat = v_new / (1.0 - ADAM_B2 ** ADAM_STEP)
        d_ref[...] = -ADAM_LR * (m_hat / (jnp.sqrt(v_hat) + ADAM_EPS) + ADAM_WD * w_ref[...])
        nm_ref[...] = m_new
        nv_ref[...] = v_new

    spec = pl.BlockSpec((tr, width), lambda i: (i, 0))
    return _pcall(
        body, name=name, grid=(rows // tr,), in_specs=[spec] * 4, out_specs=[spec] * 3,
        out_shape=[jax.ShapeDtypeStruct((rows, width), F32)] * 3,
        compiler_params=_cparams(("parallel",)),
    )(w, g, m, v)


def add2(a, b, out_dtype, name):
    rows, width = a.shape
    tr = _tile(rows, 512)

    def body(a_ref, b_ref, o_ref):
        o_ref[...] = (a_ref[...].astype(F32) + b_ref[...].astype(F32)).astype(out_dtype)

    spec = pl.BlockSpec((tr, width), lambda i: (i, 0))
    return _pcall(body, name=name, grid=(rows // tr,), in_specs=[spec, spec], out_specs=spec,
                  out_shape=jax.ShapeDtypeStruct((rows, width), out_dtype),
                  compiler_params=_cparams(("parallel",)))(a, b)


def sum_slots(q, name):
    n, rows, width = q.shape
    tr = _tile(rows, 512)

    def body(*refs):
        acc = refs[0][...].astype(F32)
        for r in refs[1:n]:
            acc = acc + r[...].astype(F32)
        refs[n][...] = acc

    specs = [pl.BlockSpec((None, tr, width), functools.partial(lambda s, i: (s, i, 0), s)) for s in range(n)]
    return _pcall(body, name=name, grid=(rows // tr,), in_specs=specs,
                  out_specs=pl.BlockSpec((tr, width), lambda i: (i, 0)),
                  out_shape=jax.ShapeDtypeStruct((rows, width), F32),
                  compiler_params=_cparams(("parallel",)))(*([q] * n))


MESH = pl.DeviceIdType.MESH
ANY_SPEC = pl.BlockSpec(memory_space=pl.ANY)


def _coords():
    return lax.axis_index("x"), lax.axis_index("y"), lax.axis_index("c")


def _other_chips(x, y):
    return [(1 - x, y), (x, 1 - y), (1 - x, 1 - y)]


def push(name, src, n_dst, plan_fn):
    _, rows, width = src.shape
    sends0, local0 = plan_fn(0, 0, 0)
    n_send, n_loc = len(sends0), len(local0)

    def body(src_ref, dst_ref, send_sems, recv_sems, loc_sems):
        sends, local = plan_fn(*_coords())
        here = [pltpu.make_async_copy(src_ref.at[si], dst_ref.at[di], loc_sems.at[n])
                for n, (si, di) in enumerate(local)]
        out = [pltpu.make_async_remote_copy(src_ref.at[si], dst_ref.at[di], send_sems.at[k], recv_sems.at[k],
                                            device_id=peer, device_id_type=MESH)
               for k, (si, peer, di, _) in enumerate(sends)]
        for cp in here + out:
            cp.start()
        for k, (si, peer, _, ri) in enumerate(sends):
            pltpu.make_async_remote_copy(src_ref.at[si], dst_ref.at[ri], send_sems.at[k], recv_sems.at[k],
                                         device_id=peer, device_id_type=MESH).wait_recv()
        for cp in out:
            cp.wait_send()
        for cp in here:
            cp.wait()

    return _pcall(
        body, name=name, in_specs=[ANY_SPEC], out_specs=ANY_SPEC,
        out_shape=jax.ShapeDtypeStruct((n_dst, rows, width), src.dtype),
        scratch_shapes=[pltpu.SemaphoreType.DMA((n_send,)), pltpu.SemaphoreType.DMA((n_send,)),
                        pltpu.SemaphoreType.DMA((max(n_loc, 1),))],
    )(src)


def plan_sibling_quarters(x, y, c):
    return [(2 * s + (1 - c), (x, y, 1 - c), s, s) for s in range(4)], []


def plan_chips_by_shard(x, y, c):
    me = 2 * x + y
    return [(2 * px + py, (px, py, c), me, 2 * px + py) for px, py in _other_chips(x, y)], [(me, me)]


def plan_chips_half(x, y, c):
    me = 2 * x + y
    return [(c, (px, py, c), me, 2 * px + py) for px, py in _other_chips(x, y)], [(c, me)]


def plan_sibling_whole(x, y, c):
    return [(0, (x, y, 1 - c), c, 1 - c)], [(0, c)]


def gather_shards(mine):
    _, rows, width = mine.shape

    def body(src_ref, dst_ref, ici_send, ici_recv, d2d_send, d2d_recv, loc_sems):
        x, y, c = _coords()
        me = 2 * x + y
        chips = _other_chips(x, y)
        here = [pltpu.make_async_copy(src_ref.at[h], dst_ref.at[2 * me + h], loc_sems.at[h]) for h in range(2)]
        out = [pltpu.make_async_remote_copy(src_ref.at[c], dst_ref.at[2 * me + c], ici_send.at[k], ici_recv.at[k],
                                            device_id=(px, py, c), device_id_type=MESH)
               for k, (px, py) in enumerate(chips)]
        for cp in here + out:
            cp.start()
        passed = []
        for k, (px, py) in enumerate(chips):
            slot = dst_ref.at[2 * (2 * px + py) + c]
            pltpu.make_async_remote_copy(src_ref.at[c], slot, ici_send.at[k], ici_recv.at[k],
                                         device_id=(px, py, c), device_id_type=MESH).wait_recv()
            cp = pltpu.make_async_remote_copy(slot, slot, d2d_send.at[k], d2d_recv.at[k],
                                              device_id=(x, y, 1 - c), device_id_type=MESH)
            cp.start()
            passed.append(cp)
        for k, (px, py) in enumerate(chips):
            slot = dst_ref.at[2 * (2 * px + py) + (1 - c)]
            pltpu.make_async_remote_copy(slot, slot, d2d_send.at[k], d2d_recv.at[k],
                                         device_id=(x, y, 1 - c), device_id_type=MESH).wait_recv()
        for cp in out + passed:
            cp.wait_send()
        for cp in here:
            cp.wait()

    return _pcall(
        body, name="gather_shards", in_specs=[ANY_SPEC], out_specs=ANY_SPEC,
        out_shape=jax.ShapeDtypeStruct((8, rows, width), mine.dtype),
        scratch_shapes=[pltpu.SemaphoreType.DMA((3,))] * 4 + [pltpu.SemaphoreType.DMA((2,))],
    )(mine)


def _pack(parts, dtype, rows):
    flat = jnp.concatenate([p.astype(dtype).reshape(-1) for p in parts])
    return jnp.pad(flat, (0, rows * C - flat.shape[0])).reshape(rows, C)


def _unpack(flat, shapes):
    flat = flat.reshape(-1)
    out, at = [], 0
    for s in shapes:
        n = int(np.prod(s))
        out.append(flat[at:at + n].reshape(s))
        at += n
    return out


def reduce_scatter_big(packed):
    c = lax.axis_index("c")
    _, rows, width = packed.shape
    from_sibling = push("rs_sibling", packed, 4, plan_sibling_quarters)
    kept = lax.dynamic_index_in_dim(packed.reshape(4, 2, rows, width), c, axis=1, keepdims=False)
    chip_sum = add2(kept.reshape(4 * rows, width), from_sibling.reshape(4 * rows, width), BF16, "rs_add_sibling")
    by_chip = push("rs_chips", chip_sum.reshape(4, rows, width), 4, plan_chips_by_shard)
    total_half = sum_slots(by_chip, "rs_sum_chips")
    return push("rs_halves", total_half[None], 2, plan_sibling_whole)


def all_reduce_small(vec):
    rows, width = vec.shape
    pair = push("ar_sibling", vec[None], 2, plan_sibling_whole)
    chip_sum = sum_slots(pair, "ar_sum_sibling")
    by_chip = push("ar_chips", chip_sum.reshape(2, rows // 2, width), 4, plan_chips_half)
    total_half = sum_slots(by_chip, "ar_sum_chips")
    return push("ar_halves", total_half[None], 2, plan_sibling_whole).reshape(rows, width)


BIG_ROWS = 12256
GRAD_ROWS = 12192
SMALL_ROWS = 576


def kernel(x, norm1_g, w_in, shift_mu, w0, w2, a0, a2, g2, k_k, k_a, r_k, gn_w, gn_b, sgu_ln_g, sgu_ln_b, sgu_w, sgu_b, sgu_out_g, w_out, norm2_g, w_gate, w_up, w_down, final_g, loss_target, m_norm1_g, m_w_in, m_shift_mu, m_w0, m_w2, m_a0, m_a2, m_g2, m_k_k, m_k_a, m_r_k, m_gn_w, m_gn_b, m_sgu_ln_g, m_sgu_ln_b, m_sgu_w, m_sgu_b, m_sgu_out_g, m_w_out, m_norm2_g, m_w_gate, m_w_up, m_w_down, m_final_g, v_norm1_g, v_w_in, v_shift_mu, v_w0, v_w2, v_a0, v_a2, v_g2, v_k_k, v_k_a, v_r_k, v_gn_w, v_gn_b, v_sgu_ln_g, v_sgu_ln_b, v_sgu_w, v_sgu_b, v_sgu_out_g, v_w_out, v_norm2_g, v_w_gate, v_w_up, v_w_down, v_final_g):
    args = dict(locals())
    names = ["norm1_g", "w_in", "shift_mu", "w0", "w2", "a0", "a2", "g2", "k_k", "k_a", "r_k", "gn_w", "gn_b",
             "sgu_ln_g", "sgu_ln_b", "sgu_w", "sgu_b", "sgu_out_g", "w_out", "norm2_g", "w_gate", "w_up", "w_down",
             "final_g"]
    big = ["w_in", "w_out", "w_gate", "w_up", "w_down"]
    xi, yi = lax.axis_index("x"), lax.axis_index("y")
    chip = 2 * xi + yi
    x2d, tgt = x[0], loss_target[0]
    t = x2d.shape[0]

    lora_names = ["w2", "a2", "g2"]
    mine = _pack([args[n][0] for n in big + lora_names], BF16, BIG_ROWS)
    got = gather_shards(mine.reshape(2, BIG_ROWS // 2, C)).reshape(4, BIG_ROWS * C)
    shard_shapes = [args[n][0].shape for n in big + lora_names]
    full = {}
    at = 0
    for n, s in zip(big + lora_names, shard_shapes):
        blk = got[:, at:at + s[0] * s[1]].reshape(4, s[0], s[1])
        at += s[0] * s[1]
        if n in ("w_out", "w_down"):
            full[n] = blk.reshape(4 * s[0], s[1])
        else:
            full[n] = blk.transpose(1, 0, 2).reshape(s[0], 4 * s[1])
    w_in_f = full["w_in"]
    w_in_p = jnp.concatenate([w_in_f[:, :3 * C], w_in_f[:, 3360:], w_in_f[:, 3 * C:3360],
                              jnp.zeros((D, P_W - D_IN), BF16)], axis=1)
    mu_rkv = shift_mu[:, :3 * C]
    mu_lora = jnp.pad(shift_mu[:, 3 * C:], ((0, 0), (0, LORA_W - 288)))
    w2p = jnp.pad(full["w2"], ((0, 64), (0, 0)))
    a2p = jnp.pad(full["a2"], ((64, 0), (0, 0)))
    g2p = jnp.pad(full["g2"], ((0, 96), (0, 0)))
    ii = np.arange(C)
    jm = jnp.asarray((ii[:, None] // HEAD) == (ii[None, :] // HEAD), BF16)
    e_np = (np.arange(LANES)[:, None] == (ii[None, :] // HEAD))
    e_mat, e_t = jnp.asarray(e_np, BF16), jnp.asarray(e_np.T, BF16)
    b_t = jnp.pad(sgu_b[0].T, ((0, 0), (0, LANES - HEADS)))
    r_k_flat = r_k.reshape(1, C)
    tm_row = _tile(t, 256)
    tm_vjp = _tile(t, 128)

    (h1,) = row_fwd("norm1", rms_fn, [Tiled(x2d)], [norm1_g], [], [(D, BF16)], tm_row)
    p = mm(h1, w_in_p, name="proj_in")
    ps_rkv = shift_fwd(p, mu_rkv, 0, "shift_rkv")
    ps_lora = shift_fwd(p, mu_lora, P_LORA // LORA_W, "shift_lora")
    pre_par = [w0, w2p, a0, a2p, g2p, k_k, k_a]
    r_, w_, k2, v_, an, b_, g_ = row_fwd("pre", pre_fn, [Tiled(ps_rkv), Tiled(ps_lora)], pre_par, [jm],
                                         [(C, F32)] * 7, tm_vjp)
    v3 = to_cols(v_)
    y3, ckpt = wkv_fwd(r_, w_, k2, v3, an, b_)
    y_ = from_cols(y3)
    post_in = [Tiled(z) for z in (y_, r_, k2, v_, g_)]
    post_par = [gn_w, gn_b, r_k_flat]
    (y_rwkv,) = row_fwd("post", post_fn, post_in, post_par, [jm], [(C, BF16)], tm_vjp)
    sgu_in = [Tiled(p, C, 3), Tiled(p, C, 4)]
    sgu_par = [sgu_ln_g, sgu_ln_b, sgu_w[0], b_t, sgu_out_g]
    (y_sgu,) = row_fwd("sgu", sgu_fn, sgu_in, sgu_par, [e_mat, e_t], [(C, BF16)], tm_vjp)
    y_cat = jnp.concatenate([y_rwkv, y_sgu], axis=1)
    x_mid = mm(y_cat, full["w_out"], res=x2d, name="proj_out")
    (h2,) = row_fwd("norm2", rms_fn, [Tiled(x_mid)], [norm2_g], [], [(D, BF16)], tm_row)
    gate = mm(h2, full["w_gate"], name="ffn_gate")
    up = mm(h2, full["w_up"], name="ffn_up")
    (act,) = row_fwd("swiglu", swiglu_fn, [Tiled(gate), Tiled(up)], [], [], [(D_FF, BF16)], tm_row)
    x_out = mm(act, full["w_down"], res=x_mid, name="ffn_down")
    loss_part, dx_out, d_final_g = loss_head(x_out, tgt, final_g.reshape(1, D))

    grads = {"final_g": d_final_g.reshape(D)}
    d_act = mm(dx_out, full["w_down"], tb=True, name="d_act")
    grads["w_down"] = mm(act, dx_out, ta=True, name="d_w_down")
    d_gate, d_up = row_bwd("swiglu_b", swiglu_fn, [Tiled(gate), Tiled(up)], [], [], [Tiled(d_act)], tm_row,
                           tiled_out_dtypes=[BF16, BF16])
    grads["w_gate"] = mm(h2, d_gate, ta=True, name="d_w_gate")
    grads["w_up"] = mm(h2, d_up, ta=True, name="d_w_up")
    d_h2 = mm(d_gate, full["w_gate"], tb=True, name="d_h2_gate")
    d_h2 = mm(d_up, full["w_up"], tb=True, res=d_h2, name="d_h2_up")
    dx_mid, grads["norm2_g"] = row_bwd("norm2_b", rms_res_fn, [Tiled(x_mid)], [norm2_g], [],
                                       [Tiled(d_h2), Tiled(dx_out)], tm_row)
    d_ycat = mm(dx_mid, full["w_out"], tb=True, name="d_ycat")
    grads["w_out"] = mm(y_cat, dx_mid, ta=True, name="d_w_out")
    d_pu, d_pv, grads["sgu_ln_g"], grads["sgu_ln_b"], d_sgu_w, d_b_t, grads["sgu_out_g"] = row_bwd(
        "sgu_b", sgu_fn, sgu_in, sgu_par, [e_mat, e_t], [Tiled(d_ycat, C, 1)], tm_vjp)
    grads["sgu_w"] = d_sgu_w[None]
    grads["sgu_b"] = d_b_t[:, :HEADS].T[None]
    d_y, d_r1, d_k1, d_v1, d_g, grads["gn_w"], grads["gn_b"], d_r_k = row_bwd(
        "post_b", post_fn, post_in, post_par, [jm], [Tiled(d_ycat, C, 0)], tm_vjp)
    grads["r_k"] = d_r_k.reshape(r_k.shape)
    d_r2, d_w, d_k2, d_a, d_b, d_v3 = wkv_bwd(r_, w_, k2, v3, an, b_, ckpt, to_cols(d_y))
    d_v2 = from_cols(d_v3)
    pre_cts = [(Tiled(d_r1), Tiled(d_r2)), Tiled(d_w), (Tiled(d_k1), Tiled(d_k2)), (Tiled(d_v1), Tiled(d_v2)),
               Tiled(d_a), Tiled(d_b), Tiled(d_g)]
    d_ps_rkv, d_ps_lora, grads["w0"], d_w2p, grads["a0"], d_a2p, d_g2p, grads["k_k"], grads["k_a"] = row_bwd(
        "pre_b", pre_fn, [Tiled(ps_rkv), Tiled(ps_lora)], pre_par, [jm], pre_cts, tm_vjp)
    grads["w2"], grads["a2"], grads["g2"] = d_w2p[None, :64], d_a2p[None, 64:], d_g2p[None, :160]
    d_p_rkv, d_mu_rkv = shift_bwd(d_ps_rkv, p, mu_rkv, 0, "shift_rkv_b")
    d_p_lora, d_mu_lora = shift_bwd(d_ps_lora, p, mu_lora, P_LORA // LORA_W, "shift_lora_b")
    grads["shift_mu"] = jnp.concatenate([d_mu_rkv, d_mu_lora[:, :288]], axis=1)
    d_p = jnp.concatenate([d_p_rkv, d_pu, d_pv, d_p_lora], axis=1)
    d_h1 = mm(d_p, w_in_p, tb=True, name="d_h1")
    d_w_in_p = mm(h1, d_p, ta=True, name="d_w_in")
    grads["w_in"] = jnp.concatenate([d_w_in_p[:, :3 * C], d_w_in_p[:, P_LORA:P_LORA + 288],
                                     d_w_in_p[:, 3 * C:P_LORA]], axis=1)
    dx, grads["norm1_g"] = row_bwd("norm1_b", rms_res_fn, [Tiled(x2d)], [norm1_g], [],
                                   [Tiled(d_h1), Tiled(dx_mid)], tm_row)

    parts = []
    for n in big:
        g_full, s = grads[n], args[n][0].shape
        if n in ("w_out", "w_down"):
            parts.append(g_full.astype(BF16).reshape(4, s[0] * s[1] // C, C))
        else:
            parts.append(g_full.astype(BF16).reshape(s[0], 4, s[1]).transpose(1, 0, 2).reshape(4, s[0] * s[1] // C, C))
    packed = jnp.concatenate(parts, axis=1)
    packed = jnp.pad(packed, ((0, 0), (0, GRAD_ROWS - packed.shape[1]), (0, 0)))
    summed = reduce_scatter_big(packed.reshape(8, GRAD_ROWS // 2, C))
    total = dict(zip(big, _unpack(summed, [args[n].shape for n in big])))
    small = [n for n in names if n not in big]
    small_shapes = [(1, 64, C) if n in ("w2", "a2") else (1, 160, C) if n == "g2" else args[n].shape for n in small]
    vec = _pack([grads[n].reshape(1, -1) for n in small], F32, SMALL_ROWS)
    for n, g_sum in zip(small, _unpack(all_reduce_small(vec), small_shapes)):
        total[n] = lax.dynamic_slice_in_dim(g_sum, chip * 256, 256, axis=2) if n in lora_names else g_sum
    loss = lax.psum(loss_part[0, 0], ("x", "y", "c"))

    delta, new_m, new_v = {}, {}, {}
    for n in big:
        s = args[n].shape
        d_, m_, v__ = adamw(args[n][0], total[n][0], args["m_" + n][0], args["v_" + n][0], "adamw_" + n)
        delta[n], new_m[n], new_v[n] = d_.reshape(s), m_.reshape(s), v__.reshape(s)
    adam_rows = -(-sum(int(np.prod(args[n].shape)) for n in small) // (8 * C)) * 8
    flat = [_pack([src[n].reshape(1, -1) for n in small], F32, adam_rows)
            for src in (args, total, {n: args["m_" + n] for n in small}, {n: args["v_" + n] for n in small})]
    outs = adamw(*flat, "adamw_small")
    for res, o in zip((delta, new_m, new_v), outs):
        res.update(zip(small, _unpack(o, [args[n].shape for n in small])))
    return (loss, dx[None], *[total[n] for n in names], *[delta[n] for n in names],
            *[new_m[n] for n in names], *[new_v[n] for n in names])
```

```python
import functools

import numpy as np
import jax
import jax.numpy as jnp
from jax import lax
from jax.experimental import pallas as pl
from jax.experimental.pallas import tpu as pltpu

F32 = jnp.float32
BF16 = jnp.bfloat16

D = 2048
C = 1024
HEADS = 16
HEAD = 64
LANES = 128
P_W = 5632
P_LORA = 5120
LORA_W = 512
D_IN = 5408
D_FF = 5632
CHUNK = 128
RMS_EPS = 1e-6
LN_EPS = 1e-5
GN_EPS = 64e-5
L2_EPS = 1e-12
VMEM_LIMIT = 56 * 1024 * 1024


def _pcall(body, **kw):
    return pl.pallas_call(body, **kw)


def _cparams(sem):
    return pltpu.CompilerParams(dimension_semantics=sem, vmem_limit_bytes=VMEM_LIMIT)


def _tile(n, most):
    t = most
    while t > 8 and n % t:
        t //= 2
    assert n % t == 0, (n, most)
    return t


def mm(a, b, *, ta=False, tb=False, res=None, out_dtype=F32, name, shards=None):
    if shards is not None:
        return _mm_shards(a, b, res, out_dtype, name, shards)
    m, k = (a.shape[1], a.shape[0]) if ta else a.shape
    n = b.shape[0] if tb else b.shape[1]
    assert (b.shape[1] if tb else b.shape[0]) == k
    tm = _tile(m, 1024)
    tn, tk = _tile(n, 512), _tile(k, 512)
    nk = k // tk
    dims = (((0 if ta else 1,), (1 if tb else 0,)), ((), ()))
    a_spec = pl.BlockSpec((tk, tm), lambda i, j, l: (l, i)) if ta else pl.BlockSpec((tm, tk), lambda i, j, l: (i, l))
    b_spec = pl.BlockSpec((tn, tk), lambda i, j, l: (j, l)) if tb else pl.BlockSpec((tk, tn), lambda i, j, l: (l, j))
    o_spec = pl.BlockSpec((tm, tn), lambda i, j, l: (i, j))
    return _mm_call(a, b, res, dims, (m // tm, n // tn, nk), a_spec, b_spec, o_spec, o_spec, (tm, tn),
                    jax.ShapeDtypeStruct((m, n), out_dtype), name)


def _mm_shards(a, b, res, out_dtype, name, shards):
    if shards == "b":
        (m, k), ns = a.shape, b.shape[2]
        tm, tk = _tile(m, 1024), _tile(k, 512)
        grid, dims, acc = (m // tm, 4, k // tk), (((1,), (0,)), ((), ())), (tm, ns)
        a_spec = pl.BlockSpec((tm, tk), lambda i, s, l: (i, l))
        b_spec = pl.BlockSpec((None, tk, ns), lambda i, s, l: (s, l, 0))
        o_spec = pl.BlockSpec((tm, ns), lambda i, s, l: (i, s))
        out = jax.ShapeDtypeStruct((m, 4 * ns), out_dtype)
    elif shards == "bt":
        m, (_, n, ns) = a.shape[0], b.shape
        tm, tn = _tile(m, 1024), _tile(n, 512)
        grid, dims, acc = (m // tm, n // tn, 4), (((1,), (1,)), ((), ())), (tm, tn)
        a_spec = pl.BlockSpec((tm, ns), lambda i, j, s: (i, s))
        b_spec = pl.BlockSpec((None, tn, ns), lambda i, j, s: (s, j, 0))
        o_spec = pl.BlockSpec((tm, tn), lambda i, j, s: (i, j))
        out = jax.ShapeDtypeStruct((m, n), out_dtype)
    else:
        (t, m), ns = a.shape, b.shape[1] // 4
        tm, tk = _tile(m, 1024), _tile(t, 512)
        grid, dims, acc = (m // tm, 4, t // tk), (((0,), (0,)), ((), ())), (tm, ns)
        a_spec = pl.BlockSpec((tk, tm), lambda i, s, l: (l, i))
        b_spec = pl.BlockSpec((tk, ns), lambda i, s, l: (l, s))
        o_spec = pl.BlockSpec((None, tm, ns), lambda i, s, l: (s, i, 0))
        out = jax.ShapeDtypeStruct((4, m, ns), out_dtype)
    return _mm_call(a, b, res, dims, grid, a_spec, b_spec, o_spec, o_spec, acc, out, name)


def _mm_call(a, b, res, dims, grid, a_spec, b_spec, r_spec, o_spec, acc_shape, out, name):
    nk = grid[2]
    out_dtype = out.dtype

    def body(*refs):
        if res is None:
            a_ref, b_ref, o_ref, acc_ref = refs
        else:
            a_ref, b_ref, r_ref, o_ref, acc_ref = refs
        kk = pl.program_id(2)

        @pl.when(kk == 0)
        def _():
            acc_ref[...] = jnp.zeros_like(acc_ref)

        acc_ref[...] += lax.dot_general(a_ref[...].astype(BF16), b_ref[...].astype(BF16), dims,
                                        preferred_element_type=F32)

        @pl.when(kk == nk - 1)
        def _():
            acc = acc_ref[...]
            if res is not None:
                acc = acc + r_ref[...].astype(F32)
            o_ref[...] = acc.astype(out_dtype)

    in_specs = [a_spec, b_spec] + ([r_spec] if res is not None else [])
    args = (a, b) + ((res,) if res is not None else ())
    return _pcall(
        body, name=name, grid=grid, in_specs=in_specs, out_specs=o_spec, out_shape=out,
        scratch_shapes=[pltpu.VMEM(acc_shape, F32)],
        compiler_params=_cparams(("parallel", "parallel", "arbitrary")),
    )(*args)


class Tiled:
    def __init__(self, arr, width=None, col=0):
        self.arr, self.width, self.col = arr, (arr.shape[1] if width is None else width), col

    def spec(self, tm):
        col = self.col
        return pl.BlockSpec((tm, self.width), lambda i: (i, col))


def _full_spec(p):
    nd = p.ndim
    return pl.BlockSpec(p.shape, lambda i: (0,) * nd)


def row_fwd(name, fn, tiled, params, consts, outs, tm):
    t = tiled[0].arr.shape[0]
    n_in = len(tiled) + len(params) + len(consts)

    def body(*refs):
        res = fn(*[r[...] for r in refs[:n_in]])
        for o_ref, r in zip(refs[n_in:], res):
            o_ref[...] = r.astype(o_ref.dtype)

    return _pcall(
        body, name=name, grid=(t // tm,),
        in_specs=[x.spec(tm) for x in tiled] + [_full_spec(p) for p in params + consts],
        out_specs=[pl.BlockSpec((tm, w), lambda i: (i, 0)) for w, _ in outs],
        out_shape=[jax.ShapeDtypeStruct((t, w), dt) for w, dt in outs],
        compiler_params=_cparams(("parallel",)),
    )(*[x.arr for x in tiled], *params, *consts)


def row_bwd(name, fn, tiled, params, consts, cts, tm, tiled_out_dtypes=None):
    t = tiled[0].arr.shape[0]
    nt, npar, ncon = len(tiled), len(params), len(consts)
    cts = [c if isinstance(c, tuple) else (c,) for c in cts]
    flat_cts = [x for c in cts for x in c]
    tiled_out_dtypes = tiled_out_dtypes or [F32] * nt

    def body(*refs):
        n_in = nt + npar + ncon
        ins = [r[...].astype(F32) for r in refs[:nt + npar]]
        con = [r[...] for r in refs[nt + npar:n_in]]
        ct_refs = list(refs[n_in:n_in + len(flat_cts)])
        o = refs[n_in + len(flat_cts):]
        ct = []
        for c in cts:
            parts = [ct_refs.pop(0)[...].astype(F32) for _ in c]
            ct.append(functools.reduce(lambda p, q: p + q, parts))
        _, vjp = jax.vjp(lambda *a: fn(*a, *con), *ins)
        g = vjp(tuple(ct))
        for j in range(nt):
            o[j][...] = g[j].astype(o[j].dtype)
        first = pl.program_id(0) == 0

        @pl.when(first)
        def _():
            for j in range(npar):
                o[nt + j][...] = g[nt + j]

        @pl.when(jnp.logical_not(first))
        def _():
            for j in range(npar):
                o[nt + j][...] += g[nt + j]

    return _pcall(
        body, name=name, grid=(t // tm,),
        in_specs=[x.spec(tm) for x in tiled] + [_full_spec(p) for p in params + consts]
        + [x.spec(tm) for x in flat_cts],
        out_specs=[pl.BlockSpec((tm, x.width), lambda i: (i, 0)) for x in tiled] + [_full_spec(p) for p in params],
        out_shape=[jax.ShapeDtypeStruct((t, x.width), dt) for x, dt in zip(tiled, tiled_out_dtypes)]
        + [jax.ShapeDtypeStruct(p.shape, F32) for p in params],
        compiler_params=_cparams(("arbitrary",)),
    )(*[x.arr for x in tiled], *params, *consts, *[x.arr for x in flat_cts])


def _split_dot(x, w):
    hi = x.astype(BF16)
    lo = (x - hi.astype(F32)).astype(BF16)
    return jnp.dot(hi, w, preferred_element_type=F32) + jnp.dot(lo, w, preferred_element_type=F32)


@jax.custom_vjp
def headsum(x, j):
    return _split_dot(x, j)


def _headsum_fwd(x, j):
    return _split_dot(x, j), j


def _headsum_bwd(j, ct):
    return _split_dot(ct, j), jnp.zeros_like(j)


headsum.defvjp(_headsum_fwd, _headsum_bwd)


def _bdot(x, w):
    return jnp.dot(x.astype(BF16), w.astype(BF16), preferred_element_type=F32)


def _sigmoid(x):
    return 1.0 / (1.0 + jnp.exp(-x))


def _softplus(x):
    return jnp.maximum(x, 0.0) + jnp.log(1.0 + jnp.exp(-jnp.abs(x)))


def rms_fn(x, g):
    return (x * lax.rsqrt(jnp.mean(x * x, axis=-1, keepdims=True) + RMS_EPS) * g,)


def rms_res_fn(x, g):
    return rms_fn(x, g)[0], x


def pre_fn(rkv, lora, w0, w2p, a0, a2p, g2p, k_k, k_a, jm):
    r, k, v = rkv[:, :C], rkv[:, C:2 * C], rkv[:, 2 * C:]
    zwa, zg = lora[:, :LANES], lora[:, LANES:LANES + 256]
    w_log = -_softplus(-(w0 + _bdot(jnp.tanh(zwa), w2p))) - 0.5
    decay = jnp.exp(-jnp.exp(w_log))
    a = _sigmoid(a0 + _bdot(zwa, a2p))
    g = _bdot(_sigmoid(zg), g2p)
    kk = k * k_k
    kk = kk / jnp.maximum(jnp.sqrt(headsum(kk * kk, jm)), L2_EPS)
    k2 = k * (1.0 + (a - 1.0) * k_a)
    return r, decay, k2, v, -kk, kk * a, g


def post_fn(y, r, k2, v, g, gn_w, gn_b, r_k, jm):
    mu = headsum(y, jm) * (1.0 / HEAD)
    yc = y - mu
    var = headsum(yc * yc, jm) * (1.0 / HEAD)
    yn = yc * lax.rsqrt(var + GN_EPS) * gn_w + gn_b
    bonus = headsum(r * k2 * r_k, jm) * v
    return ((yn + bonus) * g,)


def _gelu(x):
    return 0.5 * x * (1.0 + lax.erf(x * np.float32(1.0 / np.sqrt(2.0))))


@jax.custom_vjp
def expand_groups(b_t, e, e_t):
    return _split_dot(b_t, e)


def _expand_groups_fwd(b_t, e, e_t):
    return _split_dot(b_t, e), (e, e_t)


def _expand_groups_bwd(res, ct):
    e, e_t = res
    return _split_dot(ct, e_t), jnp.zeros_like(e), jnp.zeros_like(e_t)


expand_groups.defvjp(_expand_groups_fwd, _expand_groups_bwd)


def sgu_fn(pu, pv, ln_g, ln_b, w_s, b_t, out_g, e, e_t):
    rows = pu.shape[0]
    b_exp = expand_groups(b_t, e, e_t)
    u, v = _gelu(pu), _gelu(pv)
    mu = jnp.mean(v, axis=-1, keepdims=True)
    vc = v - mu
    var = jnp.mean(vc * vc, axis=-1, keepdims=True)
    v = vc * lax.rsqrt(var + LN_EPS) * ln_g + ln_b
    tri = lax.broadcasted_iota(jnp.int32, (CHUNK, CHUNK), 0) >= lax.broadcasted_iota(jnp.int32, (CHUNK, CHUNK), 1)
    left = lax.broadcasted_iota(jnp.int32, (CHUNK, LANES), 1) < HEAD
    chunks = []
    for c0 in range(0, rows, CHUNK):
        cols = []
        for gp in range(HEADS // 2):
            vp = v[c0:c0 + CHUNK, gp * LANES:(gp + 1) * LANES]
            wa = jnp.where(tri, w_s[2 * gp], 0.0)
            wb = jnp.where(tri, w_s[2 * gp + 1], 0.0)
            cols.append(jnp.where(left, _bdot(wa, vp), _bdot(wb, vp)))
        chunks.append(jnp.concatenate(cols, axis=1) + b_exp)
    s = jnp.concatenate(chunks, axis=0) if len(chunks) > 1 else chunks[0]
    y = u * s
    return (y * lax.rsqrt(jnp.mean(y * y, axis=-1, keepdims=True) + RMS_EPS) * out_g,)


def swiglu_fn(gate, up):
    return (gate * _sigmoid(gate) * up,)


SCAN_CHUNK = 32
NB = C // LANES


def to_cols(a):
    t = a.shape[0]
    return a.reshape(t // 8, 8, NB, 2, HEAD).transpose(0, 4, 3, 1, 2).reshape(t // 8, HEAD, LANES)


def from_cols(a):
    t8 = a.shape[0]
    return a.reshape(t8, HEAD, 2, 8, NB).transpose(0, 3, 4, 2, 1).reshape(t8 * 8, C)


def _j2():
    i = np.arange(LANES)
    return jnp.asarray((i[:, None] // HEAD) == (i[None, :] // HEAD), BF16)


def _split(x):
    hi = x.astype(BF16)
    return hi, (x - hi.astype(F32)).astype(BF16)


def _dot2(hi, lo, j2):
    return jnp.dot(hi, j2, preferred_element_type=F32) + jnp.dot(lo, j2, preferred_element_type=F32)


def _headsums(blocks, j2):
    out = _dot2(*_split(jnp.concatenate(blocks, axis=0)), j2)
    return [out[n * HEAD:(n + 1) * HEAD] for n in range(len(blocks))]


def _expand8(tile, lane63_all, j2):
    hi, lo = _split(tile)
    hi, lo = jnp.tile(hi, (8 * NB, 1)), jnp.tile(lo, (8 * NB, 1))
    block = lax.broadcasted_iota(jnp.int32, (8 * NB * HEAD, LANES), 0) // HEAD
    mask = lane63_all == block
    out = _dot2(jnp.where(mask, hi, jnp.zeros_like(hi)), jnp.where(mask, lo, jnp.zeros_like(lo)), j2)
    return [out[n * HEAD:(n + 1) * HEAD] for n in range(8 * NB)]


def _collapse8(blocks, lane63):
    tile = jnp.zeros((HEAD, LANES), F32)
    for n, blk in enumerate(blocks):
        tile = jnp.where(lane63 == n, blk, tile)
    return tile


def _row(tile, j, cb):
    return jnp.broadcast_to(tile[j:j + 1, cb * LANES:(cb + 1) * LANES], (HEAD, LANES))


def _scan_step(s_ref, rows, vexp, j2, j, hist=None):
    w_t, k_t, a_t, b_t, r_t = rows
    s = [s_ref[:, cb * LANES:(cb + 1) * LANES] for cb in range(NB)]
    sab = _headsums([s[cb] * _row(a_t, j, cb) for cb in range(NB)], j2)
    out = []
    for cb in range(NB):
        sl = slice(cb * LANES, (cb + 1) * LANES)
        s_new = s[cb] * _row(w_t, j, cb) + sab[cb] * _row(b_t, j, cb) + vexp[j * NB + cb] * _row(k_t, j, cb)
        s_ref[:, sl] = s_new
        if hist is None:
            out.append(s_new * _row(r_t, j, cb))
        else:
            s_hist, sab_hist, idx = hist
            s_hist[idx + 1, :, sl] = s_new
            sab_hist[idx, :, sl] = sab[cb]
    return out


def wkv_fwd(r, w, k, v3, a, b):
    t = r.shape[0]
    nc = t // SCAN_CHUNK
    n8 = SCAN_CHUNK // 8

    def body(r_ref, w_ref, k_ref, a_ref, b_ref, v3_ref, j2_ref, y3_ref, ck_ref, s_ref):
        @pl.when(pl.program_id(0) == 0)
        def _():
            s_ref[...] = jnp.zeros_like(s_ref)

        ck_ref[0] = s_ref[...]
        j2 = j2_ref[...]
        lane63 = lax.broadcasted_iota(jnp.int32, (HEAD, LANES), 1) & (HEAD - 1)
        lane63_all = lax.broadcasted_iota(jnp.int32, (8 * NB * HEAD, LANES), 1) & (HEAD - 1)

        def t8_body(t8, carry):
            row0 = pl.multiple_of(t8 * 8, 8)
            rows = [ref[pl.ds(row0, 8), :] for ref in (w_ref, k_ref, a_ref, b_ref, r_ref)]
            vexp = _expand8(v3_ref[t8], lane63_all, j2)
            sr = []
            for j in range(8):
                sr += _scan_step(s_ref, rows, vexp, j2, j)
            y3_ref[t8] = _collapse8(_headsums(sr, j2), lane63)
            return carry

        lax.fori_loop(0, n8, t8_body, 0)

    row_spec = pl.BlockSpec((SCAN_CHUNK, C), lambda i: (i, 0))
    col_spec = pl.BlockSpec((n8, HEAD, LANES), lambda i: (i, 0, 0))
    return _pcall(
        body, name="wkv_fwd", grid=(nc,),
        in_specs=[row_spec] * 5 + [col_spec, pl.BlockSpec((LANES, LANES), lambda i: (0, 0))],
        out_specs=[col_spec, pl.BlockSpec((1, HEAD, C), lambda i: (i, 0, 0))],
        out_shape=[jax.ShapeDtypeStruct((t // 8, HEAD, LANES), F32), jax.ShapeDtypeStruct((nc, HEAD, C), F32)],
        scratch_shapes=[pltpu.VMEM((HEAD, C), F32)],
        compiler_params=_cparams(("arbitrary",)),
    )(r, w, k, a, b, v3, _j2())


def wkv_bwd(r, w, k, v3, a, b, ckpt, dy3):
    t = r.shape[0]
    nc = t // SCAN_CHUNK
    n8 = SCAN_CHUNK // 8

    def body(r_ref, w_ref, k_ref, a_ref, b_ref, v3_ref, dy3_ref, ck_ref, j2_ref,
             dr_ref, dw_ref, dk_ref, da_ref, db_ref, dv3_ref, s_ref, g_ref, s_hist, sab_hist):
        @pl.when(pl.program_id(0) == 0)
        def _():
            g_ref[...] = jnp.zeros_like(g_ref)

        j2 = j2_ref[...]
        lane63 = lax.broadcasted_iota(jnp.int32, (HEAD, LANES), 1) & (HEAD - 1)
        lane63_all = lax.broadcasted_iota(jnp.int32, (8 * NB * HEAD, LANES), 1) & (HEAD - 1)
        sub = lax.broadcasted_iota(jnp.int32, (8, LANES), 0)
        s_ref[...] = ck_ref[0]
        s_hist[0] = ck_ref[0]

        def redo(t8, carry):
            row0 = pl.multiple_of(t8 * 8, 8)
            rows = [ref[pl.ds(row0, 8), :] for ref in (w_ref, k_ref, a_ref, b_ref, r_ref)]
            vexp = _expand8(v3_ref[t8], lane63_all, j2)
            for j in range(8):
                _scan_step(s_ref, rows, vexp, j2, j, hist=(s_hist, sab_hist, t8 * 8 + j))
            return carry

        lax.fori_loop(0, n8, redo, 0)

        def back(q, carry):
            t8 = n8 - 1 - q
            row0 = pl.multiple_of(t8 * 8, 8)
            w_t, k_t, a_t, b_t, r_t = [ref[pl.ds(row0, 8), :] for ref in (w_ref, k_ref, a_ref, b_ref, r_ref)]
            vexp = _expand8(v3_ref[t8], lane63_all, j2)
            dyexp = _expand8(dy3_ref[t8], lane63_all, j2)
            gk = [None] * (8 * NB)
            tiles = [[jnp.zeros((8, LANES), F32) for _ in range(NB)] for _ in range(5)]
            for j in range(7, -1, -1):
                idx = t8 * 8 + j
                g = [g_ref[:, cb * LANES:(cb + 1) * LANES] + dyexp[j * NB + cb] * _row(r_t, j, cb)
                     for cb in range(NB)]
                dsab = _headsums([g[cb] * _row(b_t, j, cb) for cb in range(NB)], j2)
                for cb in range(NB):
                    sl = slice(cb * LANES, (cb + 1) * LANES)
                    s_new = s_hist[idx + 1, :, sl]
                    s_old = s_hist[idx, :, sl]
                    sab = sab_hist[idx, :, sl]
                    gk[j * NB + cb] = g[cb] * _row(k_t, j, cb)
                    sums = (s_new * dyexp[j * NB + cb], g[cb] * s_old, g[cb] * vexp[j * NB + cb],
                            s_old * dsab[cb], g[cb] * sab)
                    for n, prod in enumerate(sums):
                        rowsum = jnp.broadcast_to(jnp.sum(prod, axis=0, keepdims=True), (8, LANES))
                        tiles[n][cb] = jnp.where(sub == j, rowsum, tiles[n][cb])
                    g_ref[:, sl] = g[cb] * _row(w_t, j, cb) + dsab[cb] * _row(a_t, j, cb)
            dv3_ref[t8] = _collapse8(_headsums(gk, j2), lane63)
            for n, ref in enumerate((dr_ref, dw_ref, dk_ref, da_ref, db_ref)):
                for cb in range(NB):
                    ref[pl.ds(row0, 8), cb * LANES:(cb + 1) * LANES] = tiles[n][cb]
            return carry

        lax.fori_loop(0, n8, back, 0)

    row_spec = pl.BlockSpec((SCAN_CHUNK, C), lambda i: (nc - 1 - i, 0))
    col_spec = pl.BlockSpec((n8, HEAD, LANES), lambda i: (nc - 1 - i, 0, 0))
    return _pcall(
        body, name="wkv_bwd", grid=(nc,),
        in_specs=[row_spec] * 5 + [col_spec, col_spec, pl.BlockSpec((1, HEAD, C), lambda i: (nc - 1 - i, 0, 0)),
                                   pl.BlockSpec((LANES, LANES), lambda i: (0, 0))],
        out_specs=[row_spec] * 5 + [col_spec],
        out_shape=[jax.ShapeDtypeStruct((t, C), F32)] * 5 + [jax.ShapeDtypeStruct((t // 8, HEAD, LANES), F32)],
        scratch_shapes=[pltpu.VMEM((HEAD, C), F32), pltpu.VMEM((HEAD, C), F32),
                        pltpu.VMEM((SCAN_CHUNK + 1, HEAD, C), F32), pltpu.VMEM((SCAN_CHUNK, HEAD, C), F32)],
        compiler_params=_cparams(("arbitrary",)),
    )(r, w, k, a, b, v3, dy3, ckpt, _j2())


def _prev_rows(cur, before, first_tile):
    last = jnp.where(first_tile, 0.0, before[7:8, :])
    row = lax.broadcasted_iota(jnp.int32, cur.shape, 0)
    return jnp.where(row == 0, last, pltpu.roll(cur, 1, 0))


def shift_fwd(p, mu, col, name):
    t, width = p.shape[0], mu.shape[1]
    tm = _tile(t, 256)

    def body(p_ref, before_ref, mu_ref, o_ref):
        cur = p_ref[...]
        prev = _prev_rows(cur, before_ref[...], pl.program_id(0) == 0)
        o_ref[...] = cur + (prev - cur) * mu_ref[...]

    return _pcall(
        body, name=name, grid=(t // tm,),
        in_specs=[pl.BlockSpec((tm, width), lambda i: (i, col)),
                  pl.BlockSpec((8, width), lambda i: (jnp.maximum(i * (tm // 8) - 1, 0), col)),
                  pl.BlockSpec((1, width), lambda i: (0, 0))],
        out_specs=pl.BlockSpec((tm, width), lambda i: (i, 0)),
        out_shape=jax.ShapeDtypeStruct((t, width), F32),
        compiler_params=_cparams(("parallel",)),
    )(p, p, mu)


def shift_bwd(dps, p, mu, col, name):
    t, width = dps.shape
    tm = _tile(t, 256)
    nt = t // tm

    def body(d_ref, after_ref, p_ref, before_ref, mu_ref, dp_ref, dmu_ref):
        i = pl.program_id(0)
        d, cur, mu_v = d_ref[...], p_ref[...], mu_ref[...]
        row = lax.broadcasted_iota(jnp.int32, d.shape, 0)
        first_after = jnp.where(i == nt - 1, 0.0, after_ref[0:1, :])
        nxt = jnp.where(row == tm - 1, first_after, pltpu.roll(d, tm - 1, 0))
        dp_ref[...] = d * (1.0 - mu_v) + nxt * mu_v
        prev = _prev_rows(cur, before_ref[...], i == 0)
        part = jnp.sum(d * (prev - cur), axis=0, keepdims=True)

        @pl.when(i == 0)
        def _():
            dmu_ref[...] = part

        @pl.when(i > 0)
        def _():
            dmu_ref[...] += part

    return _pcall(
        body, name=name, grid=(nt,),
        in_specs=[pl.BlockSpec((tm, width), lambda i: (i, 0)),
                  pl.BlockSpec((8, width), lambda i: (jnp.minimum((i + 1) * (tm // 8), t // 8 - 1), 0)),
                  pl.BlockSpec((tm, width), lambda i: (i, col)),
                  pl.BlockSpec((8, width), lambda i: (jnp.maximum(i * (tm // 8) - 1, 0), col)),
                  pl.BlockSpec((1, width), lambda i: (0, 0))],
        out_specs=[pl.BlockSpec((tm, width), lambda i: (i, 0)), pl.BlockSpec((1, width), lambda i: (0, 0))],
        out_shape=[jax.ShapeDtypeStruct((t, width), F32), jax.ShapeDtypeStruct((1, width), F32)],
        compiler_params=_cparams(("arbitrary",)),
    )(dps, dps, p, p, mu)


def loss_head(x3, tgt, g):
    t, d = x3.shape
    tm = _tile(t, 256)

    def body(x_ref, t_ref, g_ref, loss_ref, dx_ref, dg_ref):
        (y,), vjp = jax.vjp(lambda a, b: rms_fn(a, b), x_ref[...], g_ref[...])
        diff = y - t_ref[...]
        part = 0.5 * jnp.sum(jnp.mean(diff * diff, axis=-1, keepdims=True), axis=0, keepdims=True)
        dx, dg = vjp((diff * (1.0 / d),))
        dx_ref[...] = dx
        part = jnp.broadcast_to(part, (1, LANES))
        first = pl.program_id(0) == 0

        @pl.when(first)
        def _():
            loss_ref[...] = part
            dg_ref[...] = dg

        @pl.when(jnp.logical_not(first))
        def _():
            loss_ref[...] += part
            dg_ref[...] += dg

    row = pl.BlockSpec((tm, d), lambda i: (i, 0))
    vec = pl.BlockSpec((1, d), lambda i: (0, 0))
    return _pcall(
        body, name="loss_head", grid=(t // tm,), in_specs=[row, row, vec],
        out_specs=[pl.BlockSpec((1, LANES), lambda i: (0, 0)), row, vec],
        out_shape=[jax.ShapeDtypeStruct((1, LANES), F32), jax.ShapeDtypeStruct((t, d), F32),
                   jax.ShapeDtypeStruct((1, d), F32)],
        compiler_params=_cparams(("arbitrary",)),
    )(x3, tgt, g)


ADAM_LR, ADAM_B1, ADAM_B2, ADAM_EPS, ADAM_WD, ADAM_STEP = 0.001, 0.9, 0.999, 1e-08, 0.01, 10


def adamw(w, g, m, v, name):
    rows, width = w.shape
    tr = _tile(rows, 256)

    def body(w_ref, g_ref, m_ref, v_ref, d_ref, nm_ref, nv_ref):
        gv = g_ref[...]
        m_new = ADAM_B1 * m_ref[...] + (1.0 - ADAM_B1) * gv
        v_new = ADAM_B2 * v_ref[...] + (1.0 - ADAM_B2) * (gv * gv)
        m_hat = m_new / (1.0 - ADAM_B1 ** ADAM_STEP)
        v_hat = v_new / (1.0 - ADAM_B2 ** ADAM_STEP)
        d_ref[...] = -ADAM_LR * (m_hat / (jnp.sqrt(v_hat) + ADAM_EPS) + ADAM_WD * w_ref[...])
        nm_ref[...] = m_new
        nv_ref[...] = v_new

    spec = pl.BlockSpec((tr, width), lambda i: (i, 0))
    return _pcall(
        body, name=name, grid=(rows // tr,), in_specs=[spec] * 4, out_specs=[spec] * 3,
        out_shape=[jax.ShapeDtypeStruct((rows, width), F32)] * 3,
        compiler_params=_cparams(("parallel",)),
    )(w, g, m, v)


def sum_slots(q, name):
    n, rows, width = q.shape
    tr = _tile(rows, 512)

    def body(*refs):
        acc = refs[0][...].astype(F32)
        for r in refs[1:n]:
            acc = acc + r[...].astype(F32)
        refs[n][...] = acc

    specs = [pl.BlockSpec((None, tr, width), functools.partial(lambda s, i: (s, i, 0), s)) for s in range(n)]
    return _pcall(body, name=name, grid=(rows // tr,), in_specs=specs,
                  out_specs=pl.BlockSpec((tr, width), lambda i: (i, 0)),
                  out_shape=jax.ShapeDtypeStruct((rows, width), F32),
                  compiler_params=_cparams(("parallel",)))(*([q] * n))


MESH = pl.DeviceIdType.MESH
ANY_SPEC = pl.BlockSpec(memory_space=pl.ANY)


def _coords():
    return lax.axis_index("x"), lax.axis_index("y"), lax.axis_index("c")


def _other_chips(x, y):
    return [(1 - x, y), (x, 1 - y), (1 - x, 1 - y)]


def push(name, srcs, n_dst, plan_fn):
    n_arr = len(srcs)
    sends0, local0 = plan_fn(0, 0, 0)
    n_send, n_loc = len(sends0), len(local0)

    def body(*refs):
        src_refs, dst_refs = refs[:n_arr], refs[n_arr:2 * n_arr]
        send_sems, recv_sems, loc_sems = refs[2 * n_arr:]
        sends, local = plan_fn(*_coords())
        here, out = [], []
        for q, (src_ref, dst_ref) in enumerate(zip(src_refs, dst_refs)):
            here += [pltpu.make_async_copy(src_ref.at[si], dst_ref.at[di], loc_sems.at[q * n_loc + n])
                     for n, (si, di) in enumerate(local)]
            out += [pltpu.make_async_remote_copy(src_ref.at[si], dst_ref.at[di], send_sems.at[q * n_send + k],
                                                 recv_sems.at[q * n_send + k], device_id=peer, device_id_type=MESH)
                    for k, (si, peer, di, _) in enumerate(sends)]
        for cp in here + out:
            cp.start()
        for q, (src_ref, dst_ref) in enumerate(zip(src_refs, dst_refs)):
            for k, (si, peer, _, ri) in enumerate(sends):
                pltpu.make_async_remote_copy(src_ref.at[si], dst_ref.at[ri], send_sems.at[q * n_send + k],
                                             recv_sems.at[q * n_send + k], device_id=peer,
                                             device_id_type=MESH).wait_recv()
        for cp in out:
            cp.wait_send()
        for cp in here:
            cp.wait()

    return _pcall(
        body, name=name, in_specs=[ANY_SPEC] * n_arr, out_specs=[ANY_SPEC] * n_arr,
        out_shape=[jax.ShapeDtypeStruct((n_dst,) + s.shape[1:], s.dtype) for s in srcs],
        scratch_shapes=[pltpu.SemaphoreType.DMA((n_arr * n_send,)), pltpu.SemaphoreType.DMA((n_arr * n_send,)),
                        pltpu.SemaphoreType.DMA((max(n_arr * n_loc, 1),))],
    )(*srcs)


def plan_sibling_quarters(x, y, c):
    return [(2 * s + (1 - c), (x, y, 1 - c), s, s) for s in range(4)], []


def plan_chips_by_shard(x, y, c):
    me = 2 * x + y
    return [(2 * px + py, (px, py, c), me, 2 * px + py) for px, py in _other_chips(x, y)], [(me, me)]


def plan_chips_half(x, y, c):
    me = 2 * x + y
    return [(c, (px, py, c), me, 2 * px + py) for px, py in _other_chips(x, y)], [(c, me)]


def plan_sibling_whole(x, y, c):
    return [(0, (x, y, 1 - c), c, 1 - c)], [(0, c)]


def gather_shards(mine):
    n_arr = len(mine)

    def body(*refs):
        src_refs, dst_refs = refs[:n_arr], refs[n_arr:2 * n_arr]
        ici_send, ici_recv, d2d_send, d2d_recv, loc_sems = refs[2 * n_arr:]
        x, y, c = _coords()
        me = 2 * x + y
        chips = _other_chips(x, y)
        here, out = [], []
        for q, (src_ref, dst_ref) in enumerate(zip(src_refs, dst_refs)):
            here += [pltpu.make_async_copy(src_ref.at[h], dst_ref.at[2 * me + h], loc_sems.at[2 * q + h])
                     for h in range(2)]
            out += [pltpu.make_async_remote_copy(src_ref.at[c], dst_ref.at[2 * me + c], ici_send.at[3 * q + k],
                                                 ici_recv.at[3 * q + k], device_id=(px, py, c), device_id_type=MESH)
                    for k, (px, py) in enumerate(chips)]
        for cp in here + out:
            cp.start()
        passed = []
        for q, (src_ref, dst_ref) in enumerate(zip(src_refs, dst_refs)):
            for k, (px, py) in enumerate(chips):
                slot = dst_ref.at[2 * (2 * px + py) + c]
                pltpu.make_async_remote_copy(src_ref.at[c], slot, ici_send.at[3 * q + k], ici_recv.at[3 * q + k],
                                             device_id=(px, py, c), device_id_type=MESH).wait_recv()
                cp = pltpu.make_async_remote_copy(slot, slot, d2d_send.at[3 * q + k], d2d_recv.at[3 * q + k],
                                                  device_id=(x, y, 1 - c), device_id_type=MESH)
                cp.start()
                passed.append(cp)
        for q, dst_ref in enumerate(dst_refs):
            for k, (px, py) in enumerate(chips):
                slot = dst_ref.at[2 * (2 * px + py) + (1 - c)]
                pltpu.make_async_remote_copy(slot, slot, d2d_send.at[3 * q + k], d2d_recv.at[3 * q + k],
                                             device_id=(x, y, 1 - c), device_id_type=MESH).wait_recv()
        for cp in out + passed:
            cp.wait_send()
        for cp in here:
            cp.wait()

    got = _pcall(
        body, name="gather_shards", in_specs=[ANY_SPEC] * n_arr, out_specs=[ANY_SPEC] * n_arr,
        out_shape=[jax.ShapeDtypeStruct((8,) + m.shape[1:], m.dtype) for m in mine],
        scratch_shapes=[pltpu.SemaphoreType.DMA((3 * n_arr,))] * 4 + [pltpu.SemaphoreType.DMA((2 * n_arr,))],
    )(*mine)
    return [g.reshape(4, 2 * g.shape[1], g.shape[2]) for g in got]


def add_kept(g8, other, name):
    _, rows, cols = other.shape
    tr = _tile(rows, 512)

    def body(c_ref, a_ref, b_ref, o_ref):
        o_ref[...] = (a_ref[...].astype(F32) + b_ref[...].astype(F32)).astype(BF16)

    spec = pl.BlockSpec((None, tr, cols), lambda s, i, c_ref: (s, i, 0))
    grid_spec = pltpu.PrefetchScalarGridSpec(
        num_scalar_prefetch=1, grid=(4, rows // tr),
        in_specs=[pl.BlockSpec((None, tr, cols), lambda s, i, c_ref: (2 * s + c_ref[0], i, 0)), spec],
        out_specs=spec)
    return _pcall(body, name=name, grid_spec=grid_spec, out_shape=jax.ShapeDtypeStruct((4, rows, cols), BF16),
                  compiler_params=_cparams(("parallel", "parallel")))(
        lax.axis_index("c").astype(jnp.int32).reshape(1), g8, other)


def reduce_scatter_big(grads, names):
    g8 = [g.reshape(8, g.shape[1] // 2, g.shape[2]) for g in grads]
    from_sibling = push("rs_sibling", g8, 4, plan_sibling_quarters)
    chip_sum = [add_kept(a, b, "rs_add_" + n) for a, b, n in zip(g8, from_sibling, names)]
    by_chip = push("rs_chips", chip_sum, 4, plan_chips_by_shard)
    total_half = [sum_slots(q, "rs_sum_" + n)[None] for q, n in zip(by_chip, names)]
    both = push("rs_halves", total_half, 2, plan_sibling_whole)
    return [b.reshape(2 * b.shape[1], b.shape[2]) for b in both]


def _pack(parts, dtype, rows):
    flat = jnp.concatenate([p.astype(dtype).reshape(-1) for p in parts])
    return jnp.pad(flat, (0, rows * C - flat.shape[0])).reshape(rows, C)


def _unpack(flat, shapes):
    flat = flat.reshape(-1)
    out, at = [], 0
    for s in shapes:
        n = int(np.prod(s))
        out.append(flat[at:at + n].reshape(s))
        at += n
    return out


def all_reduce_small(vec):
    rows, width = vec.shape
    (pair,) = push("ar_sibling", [vec[None]], 2, plan_sibling_whole)
    chip_sum = sum_slots(pair, "ar_sum_sibling")
    (by_chip,) = push("ar_chips", [chip_sum.reshape(2, rows // 2, width)], 4, plan_chips_half)
    total_half = sum_slots(by_chip, "ar_sum_chips")
    (both,) = push("ar_halves", [total_half[None]], 2, plan_sibling_whole)
    return both.reshape(rows, width)


SMALL_ROWS = 576


def kernel(x, norm1_g, w_in, shift_mu, w0, w2, a0, a2, g2, k_k, k_a, r_k, gn_w, gn_b, sgu_ln_g, sgu_ln_b, sgu_w, sgu_b, sgu_out_g, w_out, norm2_g, w_gate, w_up, w_down, final_g, loss_target, m_norm1_g, m_w_in, m_shift_mu, m_w0, m_w2, m_a0, m_a2, m_g2, m_k_k, m_k_a, m_r_k, m_gn_w, m_gn_b, m_sgu_ln_g, m_sgu_ln_b, m_sgu_w, m_sgu_b, m_sgu_out_g, m_w_out, m_norm2_g, m_w_gate, m_w_up, m_w_down, m_final_g, v_norm1_g, v_w_in, v_shift_mu, v_w0, v_w2, v_a0, v_a2, v_g2, v_k_k, v_k_a, v_r_k, v_gn_w, v_gn_b, v_sgu_ln_g, v_sgu_ln_b, v_sgu_w, v_sgu_b, v_sgu_out_g, v_w_out, v_norm2_g, v_w_gate, v_w_up, v_w_down, v_final_g):
    args = dict(locals())
    names = ["norm1_g", "w_in", "shift_mu", "w0", "w2", "a0", "a2", "g2", "k_k", "k_a", "r_k", "gn_w", "gn_b",
             "sgu_ln_g", "sgu_ln_b", "sgu_w", "sgu_b", "sgu_out_g", "w_out", "norm2_g", "w_gate", "w_up", "w_down",
             "final_g"]
    big = ["w_in", "w_out", "w_gate", "w_up", "w_down"]
    xi, yi = lax.axis_index("x"), lax.axis_index("y")
    chip = 2 * xi + yi
    x2d, tgt = x[0], loss_target[0]
    t = x2d.shape[0]

    lora_names = ["w2", "a2", "g2"]
    gathered = big + lora_names
    mine = [args[n][0].astype(BF16) for n in gathered]
    got = gather_shards([m.reshape(2, m.shape[0] // 2, m.shape[1]) for m in mine])
    full = {}
    for n, blk in zip(gathered, got):
        if n in ("w_out", "w_down"):
            full[n] = blk.reshape(4 * blk.shape[1], blk.shape[2])
        elif n in ("w_gate", "w_up"):
            full[n] = blk
        else:
            full[n] = blk.transpose(1, 0, 2).reshape(blk.shape[1], 4 * blk.shape[2])
    w_in_f = full["w_in"]
    w_in_p = jnp.concatenate([w_in_f[:, :3 * C], w_in_f[:, 3360:], w_in_f[:, 3 * C:3360],
                              jnp.zeros((D, P_W - D_IN), BF16)], axis=1)
    mu_rkv = shift_mu[:, :3 * C]
    mu_lora = jnp.pad(shift_mu[:, 3 * C:], ((0, 0), (0, LORA_W - 288)))
    w2p = jnp.pad(full["w2"], ((0, 64), (0, 0)))
    a2p = jnp.pad(full["a2"], ((64, 0), (0, 0)))
    g2p = jnp.pad(full["g2"], ((0, 96), (0, 0)))
    ii = np.arange(C)
    jm = jnp.asarray((ii[:, None] // HEAD) == (ii[None, :] // HEAD), BF16)
    e_np = (np.arange(LANES)[:, None] == (ii[None, :] // HEAD))
    e_mat, e_t = jnp.asarray(e_np, BF16), jnp.asarray(e_np.T, BF16)
    b_t = jnp.pad(sgu_b[0].T, ((0, 0), (0, LANES - HEADS)))
    r_k_flat = r_k.reshape(1, C)
    tm_row = _tile(t, 256)
    tm_vjp = _tile(t, 128)

    (h1,) = row_fwd("norm1", rms_fn, [Tiled(x2d)], [norm1_g], [], [(D, BF16)], tm_row)
    p = mm(h1, w_in_p, name="proj_in")
    ps_rkv = shift_fwd(p, mu_rkv, 0, "shift_rkv")
    ps_lora = shift_fwd(p, mu_lora, P_LORA // LORA_W, "shift_lora")
    pre_par = [w0, w2p, a0, a2p, g2p, k_k, k_a]
    r_, w_, k2, v_, an, b_, g_ = row_fwd("pre", pre_fn, [Tiled(ps_rkv), Tiled(ps_lora)], pre_par, [jm],
                                         [(C, F32)] * 7, tm_vjp)
    v3 = to_cols(v_)
    y3, ckpt = wkv_fwd(r_, w_, k2, v3, an, b_)
    y_ = from_cols(y3)
    post_in = [Tiled(z) for z in (y_, r_, k2, v_, g_)]
    post_par = [gn_w, gn_b, r_k_flat]
    (y_rwkv,) = row_fwd("post", post_fn, post_in, post_par, [jm], [(C, BF16)], tm_vjp)
    sgu_in = [Tiled(p, C, 3), Tiled(p, C, 4)]
    sgu_par = [sgu_ln_g, sgu_ln_b, sgu_w[0], b_t, sgu_out_g]
    (y_sgu,) = row_fwd("sgu", sgu_fn, sgu_in, sgu_par, [e_mat, e_t], [(C, BF16)], tm_vjp)
    y_cat = jnp.concatenate([y_rwkv, y_sgu], axis=1)
    x_mid = mm(y_cat, full["w_out"], res=x2d, name="proj_out")
    (h2,) = row_fwd("norm2", rms_fn, [Tiled(x_mid)], [norm2_g], [], [(D, BF16)], tm_row)
    gate = mm(h2, full["w_gate"], shards="b", name="ffn_gate")
    up = mm(h2, full["w_up"], shards="b", name="ffn_up")
    (act,) = row_fwd("swiglu", swiglu_fn, [Tiled(gate), Tiled(up)], [], [], [(D_FF, BF16)], tm_row)
    x_out = mm(act, full["w_down"], res=x_mid, name="ffn_down")
    loss_part, dx_out, d_final_g = loss_head(x_out, tgt, final_g.reshape(1, D))

    grads = {"final_g": d_final_g.reshape(D)}
    d_act = mm(dx_out, full["w_down"], tb=True, name="d_act")
    grads["w_down"] = mm(act, dx_out, ta=True, out_dtype=BF16, name="d_w_down")
    d_gate, d_up = row_bwd("swiglu_b", swiglu_fn, [Tiled(gate), Tiled(up)], [], [], [Tiled(d_act)], tm_row,
                           tiled_out_dtypes=[BF16, BF16])
    grads["w_gate"] = mm(h2, d_gate, shards="out", out_dtype=BF16, name="d_w_gate")
    grads["w_up"] = mm(h2, d_up, shards="out", out_dtype=BF16, name="d_w_up")
    d_h2 = mm(d_gate, full["w_gate"], shards="bt", name="d_h2_gate")
    d_h2 = mm(d_up, full["w_up"], shards="bt", res=d_h2, name="d_h2_up")
    dx_mid, grads["norm2_g"] = row_bwd("norm2_b", rms_res_fn, [Tiled(x_mid)], [norm2_g], [],
                                       [Tiled(d_h2), Tiled(dx_out)], tm_row)
    d_ycat = mm(dx_mid, full["w_out"], tb=True, name="d_ycat")
    grads["w_out"] = mm(y_cat, dx_mid, ta=True, out_dtype=BF16, name="d_w_out")
    d_pu, d_pv, grads["sgu_ln_g"], grads["sgu_ln_b"], d_sgu_w, d_b_t, grads["sgu_out_g"] = row_bwd(
        "sgu_b", sgu_fn, sgu_in, sgu_par, [e_mat, e_t], [Tiled(d_ycat, C, 1)], tm_vjp)
    grads["sgu_w"] = d_sgu_w[None]
    grads["sgu_b"] = d_b_t[:, :HEADS].T[None]
    d_y, d_r1, d_k1, d_v1, d_g, grads["gn_w"], grads["gn_b"], d_r_k = row_bwd(
        "post_b", post_fn, post_in, post_par, [jm], [Tiled(d_ycat, C, 0)], tm_vjp)
    grads["r_k"] = d_r_k.reshape(r_k.shape)
    d_r2, d_w, d_k2, d_a, d_b, d_v3 = wkv_bwd(r_, w_, k2, v3, an, b_, ckpt, to_cols(d_y))
    d_v2 = from_cols(d_v3)
    pre_cts = [(Tiled(d_r1), Tiled(d_r2)), Tiled(d_w), (Tiled(d_k1), Tiled(d_k2)), (Tiled(d_v1), Tiled(d_v2)),
               Tiled(d_a), Tiled(d_b), Tiled(d_g)]
    d_ps_rkv, d_ps_lora, grads["w0"], d_w2p, grads["a0"], d_a2p, d_g2p, grads["k_k"], grads["k_a"] = row_bwd(
        "pre_b", pre_fn, [Tiled(ps_rkv), Tiled(ps_lora)], pre_par, [jm], pre_cts, tm_vjp)
    grads["w2"], grads["a2"], grads["g2"] = d_w2p[None, :64], d_a2p[None, 64:], d_g2p[None, :160]
    d_p_rkv, d_mu_rkv = shift_bwd(d_ps_rkv, p, mu_rkv, 0, "shift_rkv_b")
    d_p_lora, d_mu_lora = shift_bwd(d_ps_lora, p, mu_lora, P_LORA // LORA_W, "shift_lora_b")
    grads["shift_mu"] = jnp.concatenate([d_mu_rkv, d_mu_lora[:, :288]], axis=1)
    d_p = jnp.concatenate([d_p_rkv, d_pu, d_pv, d_p_lora], axis=1)
    d_h1 = mm(d_p, w_in_p, tb=True, name="d_h1")
    d_w_in_p = mm(h1, d_p, ta=True, name="d_w_in")
    d_w_in = jnp.concatenate([d_w_in_p[:, :3 * C], d_w_in_p[:, P_LORA:P_LORA + 288], d_w_in_p[:, 3 * C:P_LORA]],
                             axis=1).astype(BF16)
    grads["w_in"] = d_w_in.reshape(D, 4, D_IN // 4).transpose(1, 0, 2)
    dx, grads["norm1_g"] = row_bwd("norm1_b", rms_res_fn, [Tiled(x2d)], [norm1_g], [],
                                   [Tiled(d_h1), Tiled(dx_mid)], tm_row)

    by_shard = [grads[n].reshape((4,) + args[n].shape[1:]) for n in big]
    total = {n: g[None] for n, g in zip(big, reduce_scatter_big(by_shard, big))}
    small = [n for n in names if n not in big]
    small_shapes = [(1, 64, C) if n in ("w2", "a2") else (1, 160, C) if n == "g2" else args[n].shape for n in small]
    vec = _pack([grads[n].reshape(1, -1) for n in small], F32, SMALL_ROWS)
    for n, g_sum in zip(small, _unpack(all_reduce_small(vec), small_shapes)):
        total[n] = lax.dynamic_slice_in_dim(g_sum, chip * 256, 256, axis=2) if n in lora_names else g_sum
    loss = lax.psum(loss_part[0, 0], ("x", "y", "c"))

    delta, new_m, new_v = {}, {}, {}
    for n in big:
        s = args[n].shape
        d_, m_, v__ = adamw(args[n][0], total[n][0], args["m_" + n][0], args["v_" + n][0], "adamw_" + n)
        delta[n], new_m[n], new_v[n] = d_.reshape(s), m_.reshape(s), v__.reshape(s)
    adam_rows = -(-sum(int(np.prod(args[n].shape)) for n in small) // (8 * C)) * 8
    flat = [_pack([src[n].reshape(1, -1) for n in small], F32, adam_rows)
            for src in (args, total, {n: args["m_" + n] for n in small}, {n: args["v_" + n] for n in small})]
    outs = adamw(*flat, "adamw_small")
    for res, o in zip((delta, new_m, new_v), outs):
        res.update(zip(small, _unpack(o, [args[n].shape for n in small])))
    return (loss, dx[None], *[total[n] for n in names], *[delta[n] for n in names],
            *[new_m[n] for n in names], *[new_v[n] for n in names])
```

```python
import functools

import numpy as np
import jax
import jax.numpy as jnp
from jax import lax
from jax.experimental import pallas as pl
from jax.experimental.pallas import tpu as pltpu

F32 = jnp.float32
BF16 = jnp.bfloat16

D = 2048
C = 1024
HEADS = 16
HEAD = 64
LANES = 128
P_W = 5632
P_LORA = 5120
LORA_W = 512
D_IN = 5408
D_FF = 5632
CHUNK = 128
RMS_EPS = 1e-6
LN_EPS = 1e-5
GN_EPS = 64e-5
L2_EPS = 1e-12
VMEM_LIMIT = 56 * 1024 * 1024


def _pcall(body, **kw):
    return pl.pallas_call(body, **kw)


def _cparams(sem):
    return pltpu.CompilerParams(dimension_semantics=sem, vmem_limit_bytes=VMEM_LIMIT)


def _tile(n, most):
    t = most
    while t > 8 and n % t:
        t //= 2
    assert n % t == 0, (n, most)
    return t


def mm(a, b, *, ta=False, tb=False, res=None, out_dtype=F32, name, shards=None):
    if shards is not None:
        return _mm_shards(a, b, res, out_dtype, name, shards)
    m, k = (a.shape[1], a.shape[0]) if ta else a.shape
    n = b.shape[0] if tb else b.shape[1]
    assert (b.shape[1] if tb else b.shape[0]) == k
    tm = _tile(m, 1024)
    tn, tk = _tile(n, 512), _tile(k, 512)
    nk = k // tk
    dims = (((0 if ta else 1,), (1 if tb else 0,)), ((), ()))
    a_spec = pl.BlockSpec((tk, tm), lambda i, j, l: (l, i)) if ta else pl.BlockSpec((tm, tk), lambda i, j, l: (i, l))
    b_spec = pl.BlockSpec((tn, tk), lambda i, j, l: (j, l)) if tb else pl.BlockSpec((tk, tn), lambda i, j, l: (l, j))
    o_spec = pl.BlockSpec((tm, tn), lambda i, j, l: (i, j))
    return _mm_call(a, b, res, dims, (m // tm, n // tn, nk), a_spec, b_spec, o_spec, o_spec, (tm, tn),
                    jax.ShapeDtypeStruct((m, n), out_dtype), name)


def _mm_shards(a, b, res, out_dtype, name, shards):
    if shards == "b":
        (m, k), ns = a.shape, b.shape[2]
        tm, tk = _tile(m, 1024), _tile(k, 512)
        grid, dims, acc = (m // tm, 4, k // tk), (((1,), (0,)), ((), ())), (tm, ns)
        a_spec = pl.BlockSpec((tm, tk), lambda i, s, l: (i, l))
        b_spec = pl.BlockSpec((None, tk, ns), lambda i, s, l: (s, l, 0))
        o_spec = pl.BlockSpec((tm, ns), lambda i, s, l: (i, s))
        out = jax.ShapeDtypeStruct((m, 4 * ns), out_dtype)
    elif shards == "bt":
        m, (_, n, ns) = a.shape[0], b.shape
        tm, tn = _tile(m, 1024), _tile(n, 512)
        grid, dims, acc = (m // tm, n // tn, 4), (((1,), (1,)), ((), ())), (tm, tn)
        a_spec = pl.BlockSpec((tm, ns), lambda i, j, s: (i, s))
        b_spec = pl.BlockSpec((None, tn, ns), lambda i, j, s: (s, j, 0))
        o_spec = pl.BlockSpec((tm, tn), lambda i, j, s: (i, j))
        out = jax.ShapeDtypeStruct((m, n), out_dtype)
    else:
        (t, m), ns = a.shape, b.shape[1] // 4
        tm, tk = _tile(m, 1024), _tile(t, 512)
        grid, dims, acc = (m // tm, 4, t // tk), (((0,), (0,)), ((), ())), (tm, ns)
        a_spec = pl.BlockSpec((tk, tm), lambda i, s, l: (l, i))
        b_spec = pl.BlockSpec((tk, ns), lambda i, s, l: (l, s))
        o_spec = pl.BlockSpec((None, tm, ns), lambda i, s, l: (s, i, 0))
        out = jax.ShapeDtypeStruct((4, m, ns), out_dtype)
    return _mm_call(a, b, res, dims, grid, a_spec, b_spec, o_spec, o_spec, acc, out, name)


def _mm_call(a, b, res, dims, grid, a_spec, b_spec, r_spec, o_spec, acc_shape, out, name):
    nk = grid[2]
    out_dtype = out.dtype

    def body(*refs):
        if res is None:
            a_ref, b_ref, o_ref, acc_ref = refs
        else:
            a_ref, b_ref, r_ref, o_ref, acc_ref = refs
        kk = pl.program_id(2)

        @pl.when(kk == 0)
        def _():
            acc_ref[...] = jnp.zeros_like(acc_ref)

        acc_ref[...] += lax.dot_general(a_ref[...].astype(BF16), b_ref[...].astype(BF16), dims,
                                        preferred_element_type=F32)

        @pl.when(kk == nk - 1)
        def _():
            acc = acc_ref[...]
            if res is not None:
                acc = acc + r_ref[...].astype(F32)
            o_ref[...] = acc.astype(out_dtype)

    in_specs = [a_spec, b_spec] + ([r_spec] if res is not None else [])
    args = (a, b) + ((res,) if res is not None else ())
    return _pcall(
        body, name=name, grid=grid, in_specs=in_specs, out_specs=o_spec, out_shape=out,
        scratch_shapes=[pltpu.VMEM(acc_shape, F32)],
        compiler_params=_cparams(("parallel", "parallel", "arbitrary")),
    )(*args)


class Tiled:
    def __init__(self, arr, width=None, col=0):
        self.arr, self.width, self.col = arr, (arr.shape[1] if width is None else width), col

    def spec(self, tm):
        col = self.col
        return pl.BlockSpec((tm, self.width), lambda i: (i, col))


def _full_spec(p):
    nd = p.ndim
    return pl.BlockSpec(p.shape, lambda i: (0,) * nd)


def row_fwd(name, fn, tiled, params, consts, outs, tm):
    t = tiled[0].arr.shape[0]
    n_in = len(tiled) + len(params) + len(consts)

    def body(*refs):
        res = fn(*[r[...] for r in refs[:n_in]])
        for o_ref, r in zip(refs[n_in:], res):
            o_ref[...] = r.astype(o_ref.dtype)

    return _pcall(
        body, name=name, grid=(t // tm,),
        in_specs=[x.spec(tm) for x in tiled] + [_full_spec(p) for p in params + consts],
        out_specs=[pl.BlockSpec((tm, w), lambda i: (i, 0)) for w, _ in outs],
        out_shape=[jax.ShapeDtypeStruct((t, w), dt) for w, dt in outs],
        compiler_params=_cparams(("parallel",)),
    )(*[x.arr for x in tiled], *params, *consts)


def row_bwd(name, fn, tiled, params, consts, cts, tm, tiled_out_dtypes=None):
    t = tiled[0].arr.shape[0]
    nt, npar, ncon = len(tiled), len(params), len(consts)
    cts = [c if isinstance(c, tuple) else (c,) for c in cts]
    flat_cts = [x for c in cts for x in c]
    tiled_out_dtypes = tiled_out_dtypes or [F32] * nt

    def body(*refs):
        n_in = nt + npar + ncon
        ins = [r[...].astype(F32) for r in refs[:nt + npar]]
        con = [r[...] for r in refs[nt + npar:n_in]]
        ct_refs = list(refs[n_in:n_in + len(flat_cts)])
        o = refs[n_in + len(flat_cts):]
        ct = []
        for c in cts:
            parts = [ct_refs.pop(0)[...].astype(F32) for _ in c]
            ct.append(functools.reduce(lambda p, q: p + q, parts))
        _, vjp = jax.vjp(lambda *a: fn(*a, *con), *ins)
        g = vjp(tuple(ct))
        for j in range(nt):
            o[j][...] = g[j].astype(o[j].dtype)
        first = pl.program_id(0) == 0

        @pl.when(first)
        def _():
            for j in range(npar):
                o[nt + j][...] = g[nt + j]

        @pl.when(jnp.logical_not(first))
        def _():
            for j in range(npar):
                o[nt + j][...] += g[nt + j]

    return _pcall(
        body, name=name, grid=(t // tm,),
        in_specs=[x.spec(tm) for x in tiled] + [_full_spec(p) for p in params + consts]
        + [x.spec(tm) for x in flat_cts],
        out_specs=[pl.BlockSpec((tm, x.width), lambda i: (i, 0)) for x in tiled] + [_full_spec(p) for p in params],
        out_shape=[jax.ShapeDtypeStruct((t, x.width), dt) for x, dt in zip(tiled, tiled_out_dtypes)]
        + [jax.ShapeDtypeStruct(p.shape, F32) for p in params],
        compiler_params=_cparams(("arbitrary",)),
    )(*[x.arr for x in tiled], *params, *consts, *[x.arr for x in flat_cts])


def _split_dot(x, w):
    hi = x.astype(BF16)
    lo = (x - hi.astype(F32)).astype(BF16)
    return jnp.dot(hi, w, preferred_element_type=F32) + jnp.dot(lo, w, preferred_element_type=F32)


@jax.custom_vjp
def headsum(x, j):
    return _split_dot(x, j)


def _headsum_fwd(x, j):
    return _split_dot(x, j), j


def _headsum_bwd(j, ct):
    return _split_dot(ct, j), jnp.zeros_like(j)


headsum.defvjp(_headsum_fwd, _headsum_bwd)


def _bdot(x, w):
    return jnp.dot(x.astype(BF16), w.astype(BF16), preferred_element_type=F32)


def _sigmoid(x):
    return 1.0 / (1.0 + jnp.exp(-x))


def _softplus(x):
    return jnp.maximum(x, 0.0) + jnp.log(1.0 + jnp.exp(-jnp.abs(x)))


def rms_fn(x, g):
    return (x * lax.rsqrt(jnp.mean(x * x, axis=-1, keepdims=True) + RMS_EPS) * g,)


def rms_res_fn(x, g):
    return rms_fn(x, g)[0], x


def pre_fn(rkv, lora, w0, w2p, a0, a2p, g2p, k_k, k_a, jm):
    r, k, v = rkv[:, :C], rkv[:, C:2 * C], rkv[:, 2 * C:]
    zwa, zg = lora[:, :LANES], lora[:, LANES:LANES + 256]
    w_log = -_softplus(-(w0 + _bdot(jnp.tanh(zwa), w2p))) - 0.5
    decay = jnp.exp(-jnp.exp(w_log))
    a = _sigmoid(a0 + _bdot(zwa, a2p))
    g = _bdot(_sigmoid(zg), g2p)
    kk = k * k_k
    kk = kk / jnp.maximum(jnp.sqrt(headsum(kk * kk, jm)), L2_EPS)
    k2 = k * (1.0 + (a - 1.0) * k_a)
    return r, decay, k2, v, -kk, kk * a, g


def post_fn(y, r, k2, v, g, gn_w, gn_b, r_k, jm):
    mu = headsum(y, jm) * (1.0 / HEAD)
    yc = y - mu
    var = headsum(yc * yc, jm) * (1.0 / HEAD)
    yn = yc * lax.rsqrt(var + GN_EPS) * gn_w + gn_b
    bonus = headsum(r * k2 * r_k, jm) * v
    return ((yn + bonus) * g,)


def _gelu(x):
    return 0.5 * x * (1.0 + lax.erf(x * np.float32(1.0 / np.sqrt(2.0))))


@jax.custom_vjp
def expand_groups(b_t, e, e_t):
    return _split_dot(b_t, e)


def _expand_groups_fwd(b_t, e, e_t):
    return _split_dot(b_t, e), (e, e_t)


def _expand_groups_bwd(res, ct):
    e, e_t = res
    return _split_dot(ct, e_t), jnp.zeros_like(e), jnp.zeros_like(e_t)


expand_groups.defvjp(_expand_groups_fwd, _expand_groups_bwd)


def sgu_fn(pu, pv, ln_g, ln_b, w_s, b_t, out_g, e, e_t):
    rows = pu.shape[0]
    b_exp = expand_groups(b_t, e, e_t)
    u, v = _gelu(pu), _gelu(pv)
    mu = jnp.mean(v, axis=-1, keepdims=True)
    vc = v - mu
    var = jnp.mean(vc * vc, axis=-1, keepdims=True)
    v = vc * lax.rsqrt(var + LN_EPS) * ln_g + ln_b
    tri = lax.broadcasted_iota(jnp.int32, (CHUNK, CHUNK), 0) >= lax.broadcasted_iota(jnp.int32, (CHUNK, CHUNK), 1)
    left = lax.broadcasted_iota(jnp.int32, (CHUNK, LANES), 1) < HEAD
    chunks = []
    for c0 in range(0, rows, CHUNK):
        cols = []
        for gp in range(HEADS // 2):
            vp = v[c0:c0 + CHUNK, gp * LANES:(gp + 1) * LANES]
            wa = jnp.where(tri, w_s[2 * gp], 0.0)
            wb = jnp.where(tri, w_s[2 * gp + 1], 0.0)
            cols.append(jnp.where(left, _bdot(wa, vp), _bdot(wb, vp)))
        chunks.append(jnp.concatenate(cols, axis=1) + b_exp)
    s = jnp.concatenate(chunks, axis=0) if len(chunks) > 1 else chunks[0]
    y = u * s
    return (y * lax.rsqrt(jnp.mean(y * y, axis=-1, keepdims=True) + RMS_EPS) * out_g,)


def swiglu_fn(gate, up):
    return (gate * _sigmoid(gate) * up,)


SCAN_CHUNK = 32
NB = C // LANES
STEP_GROUP = 4


def to_cols(a):
    t = a.shape[0]
    return a.reshape(t // 8, 8, NB, 2, HEAD).transpose(0, 4, 3, 1, 2).reshape(t // 8, HEAD, LANES)


def from_cols(a):
    t8 = a.shape[0]
    return a.reshape(t8, HEAD, 2, 8, NB).transpose(0, 3, 4, 2, 1).reshape(t8 * 8, C)


def _scan_consts():
    i = np.arange(LANES)
    j2 = (i[:, None] // HEAD) == (i[None, :] // HEAD)
    sel = (i[None, :] % HEAD) == (np.arange(8 * NB * HEAD)[:, None] // HEAD)
    return jnp.asarray(j2, BF16), jnp.asarray(np.concatenate([j2, j2], 0), BF16), jnp.asarray(sel, BF16)


def _stack(blocks):
    return jnp.concatenate(blocks, axis=0) if len(blocks) > 1 else blocks[0]


def _unstack(x, n):
    return [x[i * HEAD:(i + 1) * HEAD] for i in range(n)]


def _headsums(blocks, j2k, group):
    res = []
    for g0 in range(0, len(blocks), group):
        x = _stack(blocks[g0:g0 + group])
        hi = x.astype(BF16)
        lo = (x - hi.astype(F32)).astype(BF16)
        out = jnp.dot(jnp.concatenate([hi, lo], axis=1), j2k, preferred_element_type=F32)
        res += _unstack(out, len(blocks[g0:g0 + group]))
    return res


def _headsums_out(blocks, j2):
    return _unstack(jnp.dot(_stack(blocks).astype(BF16), j2, preferred_element_type=F32), len(blocks))


def _expand8(tile, sel, j2):
    lhs = jnp.tile(tile.astype(BF16), (8 * NB, 1)) * sel
    return _unstack(jnp.dot(lhs, j2, preferred_element_type=F32), 8 * NB)


def _collapse(tile, blocks, first, lane63):
    for n, blk in enumerate(blocks):
        tile = jnp.where(jnp.tile(lane63 == first + n, (HEAD // 8, 1)), blk, tile)
    return tile


def _row(tile, j, cb):
    return jnp.broadcast_to(tile[j:j + 1, cb * LANES:(cb + 1) * LANES], (HEAD, LANES))


def _scan_step(s_ref, rows, vexp, j2k, j, hist=None):
    w_t, k_t, a_t, b_t, r_t = rows
    s = [s_ref[:, cb * LANES:(cb + 1) * LANES] for cb in range(NB)]
    sab = _headsums([s[cb] * _row(a_t, j, cb) for cb in range(NB)], j2k, STEP_GROUP)
    out = []
    for cb in range(NB):
        sl = slice(cb * LANES, (cb + 1) * LANES)
        s_new = s[cb] * _row(w_t, j, cb) + sab[cb] * _row(b_t, j, cb) + vexp[j * NB + cb] * _row(k_t, j, cb)
        s_ref[:, sl] = s_new
        if hist is None:
            out.append(s_new * _row(r_t, j, cb))
        else:
            s_hist, sab_hist, idx = hist
            s_hist[idx + 1, :, sl] = s_new
            sab_hist[idx, :, sl] = sab[cb]
    return out


def wkv_fwd(r, w, k, v3, a, b):
    t = r.shape[0]
    nc = t // SCAN_CHUNK
    n8 = SCAN_CHUNK // 8

    def body(r_ref, w_ref, k_ref, a_ref, b_ref, v3_ref, j2_ref, j2k_ref, sel_ref, y3_ref, ck_ref, s_ref):
        @pl.when(pl.program_id(0) == 0)
        def _():
            s_ref[...] = jnp.zeros_like(s_ref)

        ck_ref[0] = s_ref[...]
        j2, j2k, sel = j2_ref[...], j2k_ref[...], sel_ref[...]
        lane63 = lax.broadcasted_iota(jnp.int32, (8, LANES), 1) & (HEAD - 1)

        def t8_body(t8, carry):
            row0 = pl.multiple_of(t8 * 8, 8)
            rows = [ref[pl.ds(row0, 8), :] for ref in (w_ref, k_ref, a_ref, b_ref, r_ref)]
            vexp = _expand8(v3_ref[t8], sel, j2)
            y3 = jnp.zeros((HEAD, LANES), F32)
            for j in range(8):
                y3 = _collapse(y3, _headsums_out(_scan_step(s_ref, rows, vexp, j2k, j), j2), j * NB, lane63)
            y3_ref[t8] = y3
            return carry

        lax.fori_loop(0, n8, t8_body, 0)

    row_spec = pl.BlockSpec((SCAN_CHUNK, C), lambda i: (i, 0))
    col_spec = pl.BlockSpec((n8, HEAD, LANES), lambda i: (i, 0, 0))
    consts = _scan_consts()
    return _pcall(
        body, name="wkv_fwd", grid=(nc,),
        in_specs=[row_spec] * 5 + [col_spec] + [pl.BlockSpec(c.shape, lambda i: (0, 0)) for c in consts],
        out_specs=[col_spec, pl.BlockSpec((1, HEAD, C), lambda i: (i, 0, 0))],
        out_shape=[jax.ShapeDtypeStruct((t // 8, HEAD, LANES), F32), jax.ShapeDtypeStruct((nc, HEAD, C), F32)],
        scratch_shapes=[pltpu.VMEM((HEAD, C), F32)],
        compiler_params=_cparams(("arbitrary",)),
    )(r, w, k, a, b, v3, *consts)


def wkv_bwd(r, w, k, v3, a, b, ckpt, dy3):
    t = r.shape[0]
    nc = t // SCAN_CHUNK
    n8 = SCAN_CHUNK // 8

    def body(r_ref, w_ref, k_ref, a_ref, b_ref, v3_ref, dy3_ref, ck_ref, j2_ref, j2k_ref, sel_ref,
             dr_ref, dw_ref, dk_ref, da_ref, db_ref, dv3_ref, s_ref, g_ref, s_hist, sab_hist):
        @pl.when(pl.program_id(0) == 0)
        def _():
            g_ref[...] = jnp.zeros_like(g_ref)

        j2, j2k, sel = j2_ref[...], j2k_ref[...], sel_ref[...]
        lane63 = lax.broadcasted_iota(jnp.int32, (8, LANES), 1) & (HEAD - 1)
        sub = lax.broadcasted_iota(jnp.int32, (8, LANES), 0)
        s_ref[...] = ck_ref[0]
        s_hist[0] = ck_ref[0]

        def redo(t8, carry):
            row0 = pl.multiple_of(t8 * 8, 8)
            rows = [ref[pl.ds(row0, 8), :] for ref in (w_ref, k_ref, a_ref, b_ref, r_ref)]
            vexp = _expand8(v3_ref[t8], sel, j2)
            for j in range(8):
                _scan_step(s_ref, rows, vexp, j2k, j, hist=(s_hist, sab_hist, t8 * 8 + j))
            return carry

        lax.fori_loop(0, n8, redo, 0)

        def back(q, carry):
            t8 = n8 - 1 - q
            row0 = pl.multiple_of(t8 * 8, 8)
            w_t, k_t, a_t, b_t, r_t = [ref[pl.ds(row0, 8), :] for ref in (w_ref, k_ref, a_ref, b_ref, r_ref)]
            vexp = _expand8(v3_ref[t8], sel, j2)
            dyexp = _expand8(dy3_ref[t8], sel, j2)
            dv3 = jnp.zeros((HEAD, LANES), F32)
            tiles = [[jnp.zeros((8, LANES), F32) for _ in range(NB)] for _ in range(5)]
            for j in range(7, -1, -1):
                idx = t8 * 8 + j
                g = [g_ref[:, cb * LANES:(cb + 1) * LANES] + dyexp[j * NB + cb] * _row(r_t, j, cb)
                     for cb in range(NB)]
                dsab = _headsums([g[cb] * _row(b_t, j, cb) for cb in range(NB)], j2k, STEP_GROUP)
                dv3 = _collapse(dv3, _headsums_out([g[cb] * _row(k_t, j, cb) for cb in range(NB)], j2), j * NB, lane63)
                for cb in range(NB):
                    sl = slice(cb * LANES, (cb + 1) * LANES)
                    s_new = s_hist[idx + 1, :, sl]
                    s_old = s_hist[idx, :, sl]
                    sab = sab_hist[idx, :, sl]
                    sums = (s_new * dyexp[j * NB + cb], g[cb] * s_old, g[cb] * vexp[j * NB + cb],
                            s_old * dsab[cb], g[cb] * sab)
                    for n, prod in enumerate(sums):
                        rowsum = jnp.broadcast_to(jnp.sum(prod, axis=0, keepdims=True), (8, LANES))
                        tiles[n][cb] = jnp.where(sub == j, rowsum, tiles[n][cb])
                    g_ref[:, sl] = g[cb] * _row(w_t, j, cb) + dsab[cb] * _row(a_t, j, cb)
            dv3_ref[t8] = dv3
            for n, ref in enumerate((dr_ref, dw_ref, dk_ref, da_ref, db_ref)):
                for cb in range(NB):
                    ref[pl.ds(row0, 8), cb * LANES:(cb + 1) * LANES] = tiles[n][cb]
            return carry

        lax.fori_loop(0, n8, back, 0)

    row_spec = pl.BlockSpec((SCAN_CHUNK, C), lambda i: (nc - 1 - i, 0))
    col_spec = pl.BlockSpec((n8, HEAD, LANES), lambda i: (nc - 1 - i, 0, 0))
    consts = _scan_consts()
    return _pcall(
        body, name="wkv_bwd", grid=(nc,),
        in_specs=[row_spec] * 5 + [col_spec, col_spec, pl.BlockSpec((1, HEAD, C), lambda i: (nc - 1 - i, 0, 0))]
        + [pl.BlockSpec(c.shape, lambda i: (0, 0)) for c in consts],
        out_specs=[row_spec] * 5 + [col_spec],
        out_shape=[jax.ShapeDtypeStruct((t, C), F32)] * 5 + [jax.ShapeDtypeStruct((t // 8, HEAD, LANES), F32)],
        scratch_shapes=[pltpu.VMEM((HEAD, C), F32), pltpu.VMEM((HEAD, C), F32),
                        pltpu.VMEM((SCAN_CHUNK + 1, HEAD, C), F32), pltpu.VMEM((SCAN_CHUNK, HEAD, C), F32)],
        compiler_params=_cparams(("arbitrary",)),
    )(r, w, k, a, b, v3, dy3, ckpt, *consts)


def _prev_rows(cur, before, first_tile):
    last = jnp.where(first_tile, 0.0, before[7:8, :])
    row = lax.broadcasted_iota(jnp.int32, cur.shape, 0)
    return jnp.where(row == 0, last, pltpu.roll(cur, 1, 0))


def shift_fwd(p, mu, col, name):
    t, width = p.shape[0], mu.shape[1]
    tm = _tile(t, 256)

    def body(p_ref, before_ref, mu_ref, o_ref):
        cur = p_ref[...]
        prev = _prev_rows(cur, before_ref[...], pl.program_id(0) == 0)
        o_ref[...] = cur + (prev - cur) * mu_ref[...]

    return _pcall(
        body, name=name, grid=(t // tm,),
        in_specs=[pl.BlockSpec((tm, width), lambda i: (i, col)),
                  pl.BlockSpec((8, width), lambda i: (jnp.maximum(i * (tm // 8) - 1, 0), col)),
                  pl.BlockSpec((1, width), lambda i: (0, 0))],
        out_specs=pl.BlockSpec((tm, width), lambda i: (i, 0)),
        out_shape=jax.ShapeDtypeStruct((t, width), F32),
        compiler_params=_cparams(("parallel",)),
    )(p, p, mu)


def shift_bwd(dps, p, mu, col, name):
    t, width = dps.shape
    tm = _tile(t, 256)
    nt = t // tm

    def body(d_ref, after_ref, p_ref, before_ref, mu_ref, dp_ref, dmu_ref):
        i = pl.program_id(0)
        d, cur, mu_v = d_ref[...], p_ref[...], mu_ref[...]
        row = lax.broadcasted_iota(jnp.int32, d.shape, 0)
        first_after = jnp.where(i == nt - 1, 0.0, after_ref[0:1, :])
        nxt = jnp.where(row == tm - 1, first_after, pltpu.roll(d, tm - 1, 0))
        dp_ref[...] = d * (1.0 - mu_v) + nxt * mu_v
        prev = _prev_rows(cur, before_ref[...], i == 0)
        part = jnp.sum(d * (prev - cur), axis=0, keepdims=True)

        @pl.when(i == 0)
        def _():
            dmu_ref[...] = part

        @pl.when(i > 0)
        def _():
            dmu_ref[...] += part

    return _pcall(
        body, name=name, grid=(nt,),
        in_specs=[pl.BlockSpec((tm, width), lambda i: (i, 0)),
                  pl.BlockSpec((8, width), lambda i: (jnp.minimum((i + 1) * (tm // 8), t // 8 - 1), 0)),
                  pl.BlockSpec((tm, width), lambda i: (i, col)),
                  pl.BlockSpec((8, width), lambda i: (jnp.maximum(i * (tm // 8) - 1, 0), col)),
                  pl.BlockSpec((1, width), lambda i: (0, 0))],
        out_specs=[pl.BlockSpec((tm, width), lambda i: (i, 0)), pl.BlockSpec((1, width), lambda i: (0, 0))],
        out_shape=[jax.ShapeDtypeStruct((t, width), F32), jax.ShapeDtypeStruct((1, width), F32)],
        compiler_params=_cparams(("arbitrary",)),
    )(dps, dps, p, p, mu)


def loss_head(x3, tgt, g):
    t, d = x3.shape
    tm = _tile(t, 256)

    def body(x_ref, t_ref, g_ref, loss_ref, dx_ref, dg_ref):
        (y,), vjp = jax.vjp(lambda a, b: rms_fn(a, b), x_ref[...], g_ref[...])
        diff = y - t_ref[...]
        part = 0.5 * jnp.sum(jnp.mean(diff * diff, axis=-1, keepdims=True), axis=0, keepdims=True)
        dx, dg = vjp((diff * (1.0 / d),))
        dx_ref[...] = dx
        part = jnp.broadcast_to(part, (1, LANES))
        first = pl.program_id(0) == 0

        @pl.when(first)
        def _():
            loss_ref[...] = part
            dg_ref[...] = dg

        @pl.when(jnp.logical_not(first))
        def _():
            loss_ref[...] += part
            dg_ref[...] += dg

    row = pl.BlockSpec((tm, d), lambda i: (i, 0))
    vec = pl.BlockSpec((1, d), lambda i: (0, 0))
    return _pcall(
        body, name="loss_head", grid=(t // tm,), in_specs=[row, row, vec],
        out_specs=[pl.BlockSpec((1, LANES), lambda i: (0, 0)), row, vec],
        out_shape=[jax.ShapeDtypeStruct((1, LANES), F32), jax.ShapeDtypeStruct((t, d), F32),
                   jax.ShapeDtypeStruct((1, d), F32)],
        compiler_params=_cparams(("arbitrary",)),
    )(x3, tgt, g)


ADAM_LR, ADAM_B1, ADAM_B2, ADAM_EPS, ADAM_WD, ADAM_STEP = 0.001, 0.9, 0.999, 1e-08, 0.01, 10


def adamw(w, g, m, v, name):
    rows, width = w.shape
    tr = _tile(rows, 256)

    def body(w_ref, g_ref, m_ref, v_ref, d_ref, nm_ref, nv_ref):
        gv = g_ref[...]
        m_new = ADAM_B1 * m_ref[...] + (1.0 - ADAM_B1) * gv
        v_new = ADAM_B2 * v_ref[...] + (1.0 - ADAM_B2) * (gv * gv)
        m_hat = m_new / (1.0 - ADAM_B1 ** ADAM_STEP)
        v_hat = v_new / (1.0 - ADAM_B2 ** ADAM_STEP)
        d_ref[...] = -ADAM_LR * (m_hat / (jnp.sqrt(v_hat) + ADAM_EPS) + ADAM_WD * w_ref[...])
        nm_ref[...] = m_new
        nv_ref[...] = v_new

    spec = pl.BlockSpec((tr, width), lambda i: (i, 0))
    return _pcall(
        body, name=name, grid=(rows // tr,), in_specs=[spec] * 4, out_specs=[spec] * 3,
        out_shape=[jax.ShapeDtypeStruct((rows, width), F32)] * 3,
        compiler_params=_cparams(("parallel",)),
    )(w, g, m, v)


def sum_slots(q, name):
    n, rows, width = q.shape
    tr = _tile(rows, 512)

    def body(*refs):
        acc = refs[0][...].astype(F32)
        for r in refs[1:n]:
            acc = acc + r[...].astype(F32)
        refs[n][...] = acc

    specs = [pl.BlockSpec((None, tr, width), functools.partial(lambda s, i: (s, i, 0), s)) for s in range(n)]
    return _pcall(body, name=name, grid=(rows // tr,), in_specs=specs,
                  out_specs=pl.BlockSpec((tr, width), lambda i: (i, 0)),
                  out_shape=jax.ShapeDtypeStruct((rows, width), F32),
                  compiler_params=_cparams(("parallel",)))(*([q] * n))


MESH = pl.DeviceIdType.MESH
ANY_SPEC = pl.BlockSpec(memory_space=pl.ANY)


def _coords():
    return lax.axis_index("x"), lax.axis_index("y"), lax.axis_index("c")


def _other_chips(x, y):
    return [(1 - x, y), (x, 1 - y), (1 - x, 1 - y)]


def push(name, srcs, n_dst, plan_fn, start=None):
    n_arr = len(srcs)
    n_send = len(plan_fn(0, 0, 0))
    n_in = n_arr if start is None else 2 * n_arr

    def body(*refs):
        src_refs, dst_refs = refs[:n_arr], refs[n_in:n_in + n_arr]
        send_sems, recv_sems = refs[n_in + n_arr:]
        sends = plan_fn(*_coords())
        out = []
        for q, (src_ref, dst_ref) in enumerate(zip(src_refs, dst_refs)):
            out += [pltpu.make_async_remote_copy(src_ref.at[si], dst_ref.at[di], send_sems.at[q * n_send + k],
                                                 recv_sems.at[q * n_send + k], device_id=peer, device_id_type=MESH)
                    for k, (si, peer, di, _) in enumerate(sends)]
        for cp in out:
            cp.start()
        for q, (src_ref, dst_ref) in enumerate(zip(src_refs, dst_refs)):
            for k, (si, peer, _, ri) in enumerate(sends):
                pltpu.make_async_remote_copy(src_ref.at[si], dst_ref.at[ri], send_sems.at[q * n_send + k],
                                             recv_sems.at[q * n_send + k], device_id=peer,
                                             device_id_type=MESH).wait_recv()
        for cp in out:
            cp.wait_send()

    return _pcall(
        body, name=name, in_specs=[ANY_SPEC] * n_in, out_specs=[ANY_SPEC] * n_arr,
        out_shape=[jax.ShapeDtypeStruct((n_dst,) + s.shape[1:], s.dtype) for s in srcs],
        input_output_aliases={} if start is None else {n_arr + q: q for q in range(n_arr)},
        scratch_shapes=[pltpu.SemaphoreType.DMA((n_arr * n_send,)), pltpu.SemaphoreType.DMA((n_arr * n_send,))],
    )(*srcs, *([] if start is None else start))


def plan_sibling_quarters(x, y, c):
    return [(2 * s + (1 - c), (x, y, 1 - c), s, s) for s in range(4)]


def plan_chips_by_shard(x, y, c):
    me = 2 * x + y
    return [(2 * px + py, (px, py, c), me, 2 * px + py) for px, py in _other_chips(x, y)]


def plan_chips_half(x, y, c):
    me = 2 * x + y
    return [(c, (px, py, c), me, 2 * px + py) for px, py in _other_chips(x, y)]


def swap_halves(name, bufs):
    n_arr = len(bufs)

    def body(*refs):
        buf_refs = refs[n_arr:2 * n_arr]
        send_sems, recv_sems = refs[2 * n_arr:]
        x, y, c = _coords()
        out = [pltpu.make_async_remote_copy(b.at[c], b.at[c], send_sems.at[q], recv_sems.at[q],
                                            device_id=(x, y, 1 - c), device_id_type=MESH)
               for q, b in enumerate(buf_refs)]
        for cp in out:
            cp.start()
        for q, b in enumerate(buf_refs):
            pltpu.make_async_remote_copy(b.at[1 - c], b.at[1 - c], send_sems.at[q], recv_sems.at[q],
                                         device_id=(x, y, 1 - c), device_id_type=MESH).wait_recv()
        for cp in out:
            cp.wait_send()

    return _pcall(
        body, name=name, in_specs=[ANY_SPEC] * n_arr, out_specs=[ANY_SPEC] * n_arr,
        out_shape=[jax.ShapeDtypeStruct(b.shape, b.dtype) for b in bufs],
        input_output_aliases={q: q for q in range(n_arr)},
        scratch_shapes=[pltpu.SemaphoreType.DMA((n_arr,)), pltpu.SemaphoreType.DMA((n_arr,))],
    )(*bufs)


def gather_shards(mine):
    n_arr = len(mine)
    start = [jnp.broadcast_to(m[None], (4,) + m.shape).reshape((8,) + m.shape[1:]) for m in mine]

    def body(*refs):
        src_refs, dst_refs = refs[:n_arr], refs[2 * n_arr:3 * n_arr]
        ici_send, ici_recv, d2d_send, d2d_recv = refs[3 * n_arr:]
        x, y, c = _coords()
        me = 2 * x + y
        chips = _other_chips(x, y)
        out = []
        for q, (src_ref, dst_ref) in enumerate(zip(src_refs, dst_refs)):
            out += [pltpu.make_async_remote_copy(src_ref.at[c], dst_ref.at[2 * me + c], ici_send.at[3 * q + k],
                                                 ici_recv.at[3 * q + k], device_id=(px, py, c), device_id_type=MESH)
                    for k, (px, py) in enumerate(chips)]
        for cp in out:
            cp.start()
        passed = []
        for q, (src_ref, dst_ref) in enumerate(zip(src_refs, dst_refs)):
            for k, (px, py) in enumerate(chips):
                slot = dst_ref.at[2 * (2 * px + py) + c]
                pltpu.make_async_remote_copy(src_ref.at[c], slot, ici_send.at[3 * q + k], ici_recv.at[3 * q + k],
                                             device_id=(px, py, c), device_id_type=MESH).wait_recv()
                cp = pltpu.make_async_remote_copy(slot, slot, d2d_send.at[3 * q + k], d2d_recv.at[3 * q + k],
                                                  device_id=(x, y, 1 - c), device_id_type=MESH)
                cp.start()
                passed.append(cp)
        for q, dst_ref in enumerate(dst_refs):
            for k, (px, py) in enumerate(chips):
                slot = dst_ref.at[2 * (2 * px + py) + (1 - c)]
                pltpu.make_async_remote_copy(slot, slot, d2d_send.at[3 * q + k], d2d_recv.at[3 * q + k],
                                             device_id=(x, y, 1 - c), device_id_type=MESH).wait_recv()
        for cp in out + passed:
            cp.wait_send()

    got = _pcall(
        body, name="gather_shards", in_specs=[ANY_SPEC] * (2 * n_arr), out_specs=[ANY_SPEC] * n_arr,
        out_shape=[jax.ShapeDtypeStruct(s.shape, s.dtype) for s in start],
        input_output_aliases={n_arr + q: q for q in range(n_arr)},
        scratch_shapes=[pltpu.SemaphoreType.DMA((3 * n_arr,))] * 4,
    )(*mine, *start)
    return [g.reshape(4, 2 * g.shape[1], g.shape[2]) for g in got]


def add_kept(g8, other, name):
    _, rows, cols = other.shape
    tr = _tile(rows, 512)

    def body(c_ref, a_ref, b_ref, o_ref):
        o_ref[...] = (a_ref[...].astype(F32) + b_ref[...].astype(F32)).astype(BF16)

    spec = pl.BlockSpec((None, tr, cols), lambda s, i, c_ref: (s, i, 0))
    grid_spec = pltpu.PrefetchScalarGridSpec(
        num_scalar_prefetch=1, grid=(4, rows // tr),
        in_specs=[pl.BlockSpec((None, tr, cols), lambda s, i, c_ref: (2 * s + c_ref[0], i, 0)), spec],
        out_specs=spec)
    return _pcall(body, name=name, grid_spec=grid_spec, out_shape=jax.ShapeDtypeStruct((4, rows, cols), BF16),
                  compiler_params=_cparams(("parallel", "parallel")))(
        lax.axis_index("c").astype(jnp.int32).reshape(1), g8, other)


def reduce_scatter_big(grads, names):
    g8 = [g.reshape(8, g.shape[1] // 2, g.shape[2]) for g in grads]
    from_sibling = push("rs_sibling", g8, 4, plan_sibling_quarters)
    chip_sum = [add_kept(a, b, "rs_add_" + n) for a, b, n in zip(g8, from_sibling, names)]
    by_chip = push("rs_chips", chip_sum, 4, plan_chips_by_shard, start=chip_sum)
    total_half = [sum_slots(q, "rs_sum_" + n) for q, n in zip(by_chip, names)]
    both = swap_halves("rs_halves", [jnp.broadcast_to(t[None], (2,) + t.shape) for t in total_half])
    return [b.reshape(2 * b.shape[1], b.shape[2]) for b in both]


def _pack(parts, dtype, rows):
    flat = jnp.concatenate([p.astype(dtype).reshape(-1) for p in parts])
    return jnp.pad(flat, (0, rows * C - flat.shape[0])).reshape(rows, C)


def _unpack(flat, shapes):
    flat = flat.reshape(-1)
    out, at = [], 0
    for s in shapes:
        n = int(np.prod(s))
        out.append(flat[at:at + n].reshape(s))
        at += n
    return out


def all_reduce_small(vec):
    rows, width = vec.shape
    (pair,) = swap_halves("ar_sibling", [jnp.broadcast_to(vec[None], (2, rows, width))])
    chip_sum = sum_slots(pair, "ar_sum_sibling").reshape(2, rows // 2, width)
    mine = lax.dynamic_index_in_dim(chip_sum, lax.axis_index("c"), axis=0, keepdims=True)
    (by_chip,) = push("ar_chips", [chip_sum], 4, plan_chips_half,
                      start=[jnp.broadcast_to(mine, (4, rows // 2, width))])
    total_half = sum_slots(by_chip, "ar_sum_chips")
    (both,) = swap_halves("ar_halves", [jnp.broadcast_to(total_half[None], (2, rows // 2, width))])
    return both.reshape(rows, width)


SMALL_ROWS = 576


def kernel(x, norm1_g, w_in, shift_mu, w0, w2, a0, a2, g2, k_k, k_a, r_k, gn_w, gn_b, sgu_ln_g, sgu_ln_b, sgu_w, sgu_b, sgu_out_g, w_out, norm2_g, w_gate, w_up, w_down, final_g, loss_target, m_norm1_g, m_w_in, m_shift_mu, m_w0, m_w2, m_a0, m_a2, m_g2, m_k_k, m_k_a, m_r_k, m_gn_w, m_gn_b, m_sgu_ln_g, m_sgu_ln_b, m_sgu_w, m_sgu_b, m_sgu_out_g, m_w_out, m_norm2_g, m_w_gate, m_w_up, m_w_down, m_final_g, v_norm1_g, v_w_in, v_shift_mu, v_w0, v_w2, v_a0, v_a2, v_g2, v_k_k, v_k_a, v_r_k, v_gn_w, v_gn_b, v_sgu_ln_g, v_sgu_ln_b, v_sgu_w, v_sgu_b, v_sgu_out_g, v_w_out, v_norm2_g, v_w_gate, v_w_up, v_w_down, v_final_g):
    args = dict(locals())
    names = ["norm1_g", "w_in", "shift_mu", "w0", "w2", "a0", "a2", "g2", "k_k", "k_a", "r_k", "gn_w", "gn_b",
             "sgu_ln_g", "sgu_ln_b", "sgu_w", "sgu_b", "sgu_out_g", "w_out", "norm2_g", "w_gate", "w_up", "w_down",
             "final_g"]
    big = ["w_in", "w_out", "w_gate", "w_up", "w_down"]
    xi, yi = lax.axis_index("x"), lax.axis_index("y")
    chip = 2 * xi + yi
    x2d, tgt = x[0], loss_target[0]
    t = x2d.shape[0]

    lora_names = ["w2", "a2", "g2"]
    gathered = big + lora_names
    mine = [args[n][0].astype(BF16) for n in gathered]
    got = gather_shards([m.reshape(2, m.shape[0] // 2, m.shape[1]) for m in mine])
    full = {}
    for n, blk in zip(gathered, got):
        if n in ("w_out", "w_down"):
            full[n] = blk.reshape(4 * blk.shape[1], blk.shape[2])
        elif n in ("w_gate", "w_up"):
            full[n] = blk
        else:
            full[n] = blk.transpose(1, 0, 2).reshape(blk.shape[1], 4 * blk.shape[2])
    w_in_f = full["w_in"]
    w_in_p = jnp.concatenate([w_in_f[:, :3 * C], w_in_f[:, 3360:], w_in_f[:, 3 * C:3360],
                              jnp.zeros((D, P_W - D_IN), BF16)], axis=1)
    mu_rkv = shift_mu[:, :3 * C]
    mu_lora = jnp.pad(shift_mu[:, 3 * C:], ((0, 0), (0, LORA_W - 288)))
    w2p = jnp.pad(full["w2"], ((0, 64), (0, 0)))
    a2p = jnp.pad(full["a2"], ((64, 0), (0, 0)))
    g2p = jnp.pad(full["g2"], ((0, 96), (0, 0)))
    ii = np.arange(C)
    jm = jnp.asarray((ii[:, None] // HEAD) == (ii[None, :] // HEAD), BF16)
    e_np = (np.arange(LANES)[:, None] == (ii[None, :] // HEAD))
    e_mat, e_t = jnp.asarray(e_np, BF16), jnp.asarray(e_np.T, BF16)
    b_t = jnp.pad(sgu_b[0].T, ((0, 0), (0, LANES - HEADS)))
    r_k_flat = r_k.reshape(1, C)
    tm_row = _tile(t, 256)
    tm_vjp = _tile(t, 128)

    (h1,) = row_fwd("norm1", rms_fn, [Tiled(x2d)], [norm1_g], [], [(D, BF16)], tm_row)
    p = mm(h1, w_in_p, name="proj_in")
    ps_rkv = shift_fwd(p, mu_rkv, 0, "shift_rkv")
    ps_lora = shift_fwd(p, mu_lora, P_LORA // LORA_W, "shift_lora")
    pre_par = [w0, w2p, a0, a2p, g2p, k_k, k_a]
    r_, w_, k2, v_, an, b_, g_ = row_fwd("pre", pre_fn, [Tiled(ps_rkv), Tiled(ps_lora)], pre_par, [jm],
                                         [(C, F32)] * 7, tm_vjp)
    v3 = to_cols(v_)
    y3, ckpt = wkv_fwd(r_, w_, k2, v3, an, b_)
    y_ = from_cols(y3)
    post_in = [Tiled(z) for z in (y_, r_, k2, v_, g_)]
    post_par = [gn_w, gn_b, r_k_flat]
    (y_rwkv,) = row_fwd("post", post_fn, post_in, post_par, [jm], [(C, BF16)], tm_vjp)
    sgu_in = [Tiled(p, C, 3), Tiled(p, C, 4)]
    sgu_par = [sgu_ln_g, sgu_ln_b, sgu_w[0], b_t, sgu_out_g]
    (y_sgu,) = row_fwd("sgu", sgu_fn, sgu_in, sgu_par, [e_mat, e_t], [(C, BF16)], tm_vjp)
    y_cat = jnp.concatenate([y_rwkv, y_sgu], axis=1)
    x_mid = mm(y_cat, full["w_out"], res=x2d, name="proj_out")
    (h2,) = row_fwd("norm2", rms_fn, [Tiled(x_mid)], [norm2_g], [], [(D, BF16)], tm_row)
    gate = mm(h2, full["w_gate"], shards="b", name="ffn_gate")
    up = mm(h2, full["w_up"], shards="b", name="ffn_up")
    (act,) = row_fwd("swiglu", swiglu_fn, [Tiled(gate), Tiled(up)], [], [], [(D_FF, BF16)], tm_row)
    x_out = mm(act, full["w_down"], res=x_mid, name="ffn_down")
    loss_part, dx_out, d_final_g = loss_head(x_out, tgt, final_g.reshape(1, D))

    grads = {"final_g": d_final_g.reshape(D)}
    d_act = mm(dx_out, full["w_down"], tb=True, name="d_act")
    grads["w_down"] = mm(act, dx_out, ta=True, out_dtype=BF16, name="d_w_down")
    d_gate, d_up = row_bwd("swiglu_b", swiglu_fn, [Tiled(gate), Tiled(up)], [], [], [Tiled(d_act)], tm_row,
                           tiled_out_dtypes=[BF16, BF16])
    grads["w_gate"] = mm(h2, d_gate, shards="out", out_dtype=BF16, name="d_w_gate")
    grads["w_up"] = mm(h2, d_up, shards="out", out_dtype=BF16, name="d_w_up")
    d_h2 = mm(d_gate, full["w_gate"], shards="bt", name="d_h2_gate")
    d_h2 = mm(d_up, full["w_up"], shards="bt", res=d_h2, name="d_h2_up")
    dx_mid, grads["norm2_g"] = row_bwd("norm2_b", rms_res_fn, [Tiled(x_mid)], [norm2_g], [],
                                       [Tiled(d_h2), Tiled(dx_out)], tm_row)
    d_ycat = mm(dx_mid, full["w_out"], tb=True, name="d_ycat")
    grads["w_out"] = mm(y_cat, dx_mid, ta=True, out_dtype=BF16, name="d_w_out")
    d_pu, d_pv, grads["sgu_ln_g"], grads["sgu_ln_b"], d_sgu_w, d_b_t, grads["sgu_out_g"] = row_bwd(
        "sgu_b", sgu_fn, sgu_in, sgu_par, [e_mat, e_t], [Tiled(d_ycat, C, 1)], tm_vjp)
    grads["sgu_w"] = d_sgu_w[None]
    grads["sgu_b"] = d_b_t[:, :HEADS].T[None]
    d_y, d_r1, d_k1, d_v1, d_g, grads["gn_w"], grads["gn_b"], d_r_k = row_bwd(
        "post_b", post_fn, post_in, post_par, [jm], [Tiled(d_ycat, C, 0)], tm_vjp)
    grads["r_k"] = d_r_k.reshape(r_k.shape)
    d_r2, d_w, d_k2, d_a, d_b, d_v3 = wkv_bwd(r_, w_, k2, v3, an, b_, ckpt, to_cols(d_y))
    d_v2 = from_cols(d_v3)
    pre_cts = [(Tiled(d_r1), Tiled(d_r2)), Tiled(d_w), (Tiled(d_k1), Tiled(d_k2)), (Tiled(d_v1), Tiled(d_v2)),
               Tiled(d_a), Tiled(d_b), Tiled(d_g)]
    d_ps_rkv, d_ps_lora, grads["w0"], d_w2p, grads["a0"], d_a2p, d_g2p, grads["k_k"], grads["k_a"] = row_bwd(
        "pre_b", pre_fn, [Tiled(ps_rkv), Tiled(ps_lora)], pre_par, [jm], pre_cts, tm_vjp)
    grads["w2"], grads["a2"], grads["g2"] = d_w2p[None, :64], d_a2p[None, 64:], d_g2p[None, :160]
    d_p_rkv, d_mu_rkv = shift_bwd(d_ps_rkv, p, mu_rkv, 0, "shift_rkv_b")
    d_p_lora, d_mu_lora = shift_bwd(d_ps_lora, p, mu_lora, P_LORA // LORA_W, "shift_lora_b")
    grads["shift_mu"] = jnp.concatenate([d_mu_rkv, d_mu_lora[:, :288]], axis=1)
    d_p = jnp.concatenate([d_p_rkv, d_pu, d_pv, d_p_lora], axis=1)
    d_h1 = mm(d_p, w_in_p, tb=True, name="d_h1")
    d_w_in_p = mm(h1, d_p, ta=True, name="d_w_in")
    d_w_in = jnp.concatenate([d_w_in_p[:, :3 * C], d_w_in_p[:, P_LORA:P_LORA + 288], d_w_in_p[:, 3 * C:P_LORA]],
                             axis=1).astype(BF16)
    grads["w_in"] = d_w_in.reshape(D, 4, D_IN // 4).transpose(1, 0, 2)
    dx, grads["norm1_g"] = row_bwd("norm1_b", rms_res_fn, [Tiled(x2d)], [norm1_g], [],
                                   [Tiled(d_h1), Tiled(dx_mid)], tm_row)

    by_shard = [grads[n].reshape((4,) + args[n].shape[1:]) for n in big]
    total = {n: g[None] for n, g in zip(big, reduce_scatter_big(by_shard, big))}
    small = [n for n in names if n not in big]
    small_shapes = [(1, 64, C) if n in ("w2", "a2") else (1, 160, C) if n == "g2" else args[n].shape for n in small]
    vec = _pack([grads[n].reshape(1, -1) for n in small], F32, SMALL_ROWS)
    for n, g_sum in zip(small, _unpack(all_reduce_small(vec), small_shapes)):
        total[n] = lax.dynamic_slice_in_dim(g_sum, chip * 256, 256, axis=2) if n in lora_names else g_sum
    loss = lax.psum(loss_part[0, 0], ("x", "y", "c"))

    delta, new_m, new_v = {}, {}, {}
    for n in big:
        s = args[n].shape
        d_, m_, v__ = adamw(args[n][0], total[n][0], args["m_" + n][0], args["v_" + n][0], "adamw_" + n)
        delta[n], new_m[n], new_v[n] = d_.reshape(s), m_.reshape(s), v__.reshape(s)
    adam_rows = -(-sum(int(np.prod(args[n].shape)) for n in small) // (8 * C)) * 8
    flat = [_pack([src[n].reshape(1, -1) for n in small], F32, adam_rows)
            for src in (args, total, {n: args["m_" + n] for n in small}, {n: args["v_" + n] for n in small})]
    outs = adamw(*flat, "adamw_small")
    for res, o in zip((delta, new_m, new_v), outs):
        res.update(zip(small, _unpack(o, [args[n].shape for n in small])))
    return (loss, dx[None], *[total[n] for n in names], *[delta[n] for n in names],
            *[new_m[n] for n in names], *[new_v[n] for n in names])
```

```python
import functools

import numpy as np
import jax
import jax.numpy as jnp
from jax import lax
from jax.experimental import pallas as pl
from jax.experimental.pallas import tpu as pltpu

F32 = jnp.float32
BF16 = jnp.bfloat16

D = 2048
C = 1024
HEADS = 16
HEAD = 64
LANES = 128
P_W = 5632
P_LORA = 5120
LORA_W = 512
D_IN = 5408
D_FF = 5632
CHUNK = 128
RMS_EPS = 1e-6
LN_EPS = 1e-5
GN_EPS = 64e-5
L2_EPS = 1e-12
VMEM_LIMIT = 56 * 1024 * 1024


def _pcall(body, **kw):
    return pl.pallas_call(body, **kw)


def _cparams(sem):
    return pltpu.CompilerParams(dimension_semantics=sem, vmem_limit_bytes=VMEM_LIMIT)


def _tile(n, most):
    t = most
    while t > 8 and n % t:
        t //= 2
    assert n % t == 0, (n, most)
    return t


MM_TILE = (2048, 512, 2048)
MM_VMEM = 40 * 1024 * 1024


def _div_tile(n, most, quantum=LANES):
    for t in range(min(n, most) // quantum * quantum, 0, -quantum):
        if n % t == 0:
            return t
    raise ValueError((n, most, quantum))


def _mm_tiles(m, n, k, a_bytes, b_bytes, r_bytes, o_bytes):
    tm, tn, tk = _div_tile(m, MM_TILE[0]), _div_tile(n, MM_TILE[1]), _div_tile(k, MM_TILE[2])

    def need(tm, tk):
        return 2 * (tm * tk * a_bytes + tk * tn * b_bytes + tm * tn * (r_bytes + o_bytes)) + tm * tn * 4

    while need(tm, tk) > MM_VMEM:
        if tk >= tm and tk > 512:
            tk = _div_tile(k, tk - LANES)
        else:
            tm = _div_tile(m, tm - LANES)
    return tm, tn, tk


def mm(a, b, *, ta=False, tb=False, res=None, out_dtype=F32, name, shards=None):
    if shards is not None:
        return _mm_shards(a, b, res, out_dtype, name, shards)
    m, k = (a.shape[1], a.shape[0]) if ta else a.shape
    n = b.shape[0] if tb else b.shape[1]
    assert (b.shape[1] if tb else b.shape[0]) == k
    tm, tn, tk = _mm_tiles(m, n, k, a.dtype.itemsize, b.dtype.itemsize, 0 if res is None else res.dtype.itemsize,
                           jnp.dtype(out_dtype).itemsize)
    nk = k // tk
    dims = (((0 if ta else 1,), (1 if tb else 0,)), ((), ()))
    a_spec = pl.BlockSpec((tk, tm), lambda i, j, l: (l, i)) if ta else pl.BlockSpec((tm, tk), lambda i, j, l: (i, l))
    b_spec = pl.BlockSpec((tn, tk), lambda i, j, l: (j, l)) if tb else pl.BlockSpec((tk, tn), lambda i, j, l: (l, j))
    o_spec = pl.BlockSpec((tm, tn), lambda i, j, l: (i, j))
    return _mm_call(a, b, res, dims, (m // tm, n // tn, nk), a_spec, b_spec, o_spec, o_spec, (tm, tn),
                    jax.ShapeDtypeStruct((m, n), out_dtype), name)


def _mm_shards(a, b, res, out_dtype, name, shards):
    if shards == "b":
        (m, k), ns = a.shape, b.shape[2]
        tm, tk = _div_tile(m, 1024), _div_tile(k, 2048)
        grid, dims, acc = (m // tm, 4, k // tk), (((1,), (0,)), ((), ())), (tm, ns)
        a_spec = pl.BlockSpec((tm, tk), lambda i, s, l: (i, l))
        b_spec = pl.BlockSpec((None, tk, ns), lambda i, s, l: (s, l, 0))
        o_spec = pl.BlockSpec((tm, ns), lambda i, s, l: (i, s))
        out = jax.ShapeDtypeStruct((m, 4 * ns), out_dtype)
    elif shards == "bt":
        m, (_, n, ns) = a.shape[0], b.shape
        tm, tn = _div_tile(m, 2048), _div_tile(n, 512)
        grid, dims, acc = (m // tm, n // tn, 4), (((1,), (1,)), ((), ())), (tm, tn)
        a_spec = pl.BlockSpec((tm, ns), lambda i, j, s: (i, s))
        b_spec = pl.BlockSpec((None, tn, ns), lambda i, j, s: (s, j, 0))
        o_spec = pl.BlockSpec((tm, tn), lambda i, j, s: (i, j))
        out = jax.ShapeDtypeStruct((m, n), out_dtype)
    else:
        (t, m), ns = a.shape, b.shape[1] // 4
        tm, tk = _div_tile(m, 1024), _div_tile(t, 2048)
        grid, dims, acc = (m // tm, 4, t // tk), (((0,), (0,)), ((), ())), (tm, ns)
        a_spec = pl.BlockSpec((tk, tm), lambda i, s, l: (l, i))
        b_spec = pl.BlockSpec((tk, ns), lambda i, s, l: (l, s))
        o_spec = pl.BlockSpec((None, tm, ns), lambda i, s, l: (s, i, 0))
        out = jax.ShapeDtypeStruct((4, m, ns), out_dtype)
    return _mm_call(a, b, res, dims, grid, a_spec, b_spec, o_spec, o_spec, acc, out, name)


def _mm_call(a, b, res, dims, grid, a_spec, b_spec, r_spec, o_spec, acc_shape, out, name):
    nk = grid[2]
    out_dtype = out.dtype

    def body(*refs):
        a_ref, b_ref = refs[:2]
        r_ref = None if res is None else refs[2]
        o_ref = refs[2 + (res is not None)]

        def finish(acc):
            if r_ref is not None:
                acc = acc + r_ref[...].astype(F32)
            o_ref[...] = acc.astype(out_dtype)

        prod = lax.dot_general(a_ref[...].astype(BF16), b_ref[...].astype(BF16), dims, preferred_element_type=F32)
        if nk == 1:
            finish(prod)
            return
        acc_ref = refs[-1]
        kk = pl.program_id(2)

        @pl.when(kk == 0)
        def _():
            acc_ref[...] = prod

        @pl.when(kk > 0)
        def _():
            acc_ref[...] += prod

        @pl.when(kk == nk - 1)
        def _():
            finish(acc_ref[...])

    in_specs = [a_spec, b_spec] + ([r_spec] if res is not None else [])
    args = (a, b) + ((res,) if res is not None else ())
    return _pcall(
        body, name=name, grid=grid, in_specs=in_specs, out_specs=o_spec, out_shape=out,
        scratch_shapes=[] if nk == 1 else [pltpu.VMEM(acc_shape, F32)],
        compiler_params=_cparams(("parallel", "parallel", "arbitrary")),
    )(*args)


class Tiled:
    def __init__(self, arr, width=None, col=0):
        self.arr, self.width, self.col = arr, (arr.shape[1] if width is None else width), col

    def spec(self, tm):
        col = self.col
        return pl.BlockSpec((tm, self.width), lambda i: (i, col))


def _full_spec(p):
    nd = p.ndim
    return pl.BlockSpec(p.shape, lambda i: (0,) * nd)


def row_fwd(name, fn, tiled, params, consts, outs, tm):
    t = tiled[0].arr.shape[0]
    n_in = len(tiled) + len(params) + len(consts)

    def body(*refs):
        res = fn(*[r[...] for r in refs[:n_in]])
        for o_ref, r in zip(refs[n_in:], res):
            o_ref[...] = r.astype(o_ref.dtype)

    return _pcall(
        body, name=name, grid=(t // tm,),
        in_specs=[x.spec(tm) for x in tiled] + [_full_spec(p) for p in params + consts],
        out_specs=[pl.BlockSpec((tm, w), lambda i: (i, 0)) for w, _ in outs],
        out_shape=[jax.ShapeDtypeStruct((t, w), dt) for w, dt in outs],
        compiler_params=_cparams(("parallel",)),
    )(*[x.arr for x in tiled], *params, *consts)


def row_bwd(name, fn, tiled, params, consts, cts, tm, tiled_out_dtypes=None):
    t = tiled[0].arr.shape[0]
    nt, npar, ncon = len(tiled), len(params), len(consts)
    cts = [c if isinstance(c, tuple) else (c,) for c in cts]
    flat_cts = [x for c in cts for x in c]
    tiled_out_dtypes = tiled_out_dtypes or [F32] * nt

    def body(*refs):
        n_in = nt + npar + ncon
        ins = [r[...].astype(F32) for r in refs[:nt + npar]]
        con = [r[...] for r in refs[nt + npar:n_in]]
        ct_refs = list(refs[n_in:n_in + len(flat_cts)])
        o = refs[n_in + len(flat_cts):]
        ct = []
        for c in cts:
            parts = [ct_refs.pop(0)[...].astype(F32) for _ in c]
            ct.append(functools.reduce(lambda p, q: p + q, parts))
        _, vjp = jax.vjp(lambda *a: fn(*a, *con), *ins)
        g = vjp(tuple(ct))
        for j in range(nt):
            o[j][...] = g[j].astype(o[j].dtype)
        first = pl.program_id(0) == 0

        @pl.when(first)
        def _():
            for j in range(npar):
                o[nt + j][...] = g[nt + j]

        @pl.when(jnp.logical_not(first))
        def _():
            for j in range(npar):
                o[nt + j][...] += g[nt + j]

    return _pcall(
        body, name=name, grid=(t // tm,),
        in_specs=[x.spec(tm) for x in tiled] + [_full_spec(p) for p in params + consts]
        + [x.spec(tm) for x in flat_cts],
        out_specs=[pl.BlockSpec((tm, x.width), lambda i: (i, 0)) for x in tiled] + [_full_spec(p) for p in params],
        out_shape=[jax.ShapeDtypeStruct((t, x.width), dt) for x, dt in zip(tiled, tiled_out_dtypes)]
        + [jax.ShapeDtypeStruct(p.shape, F32) for p in params],
        compiler_params=_cparams(("arbitrary",)),
    )(*[x.arr for x in tiled], *params, *consts, *[x.arr for x in flat_cts])


def _split_dot(x, w):
    hi = x.astype(BF16)
    lo = (x - hi.astype(F32)).astype(BF16)
    return jnp.dot(hi, w, preferred_element_type=F32) + jnp.dot(lo, w, preferred_element_type=F32)


@jax.custom_vjp
def headsum(x, j):
    return _split_dot(x, j)


def _headsum_fwd(x, j):
    return _split_dot(x, j), j


def _headsum_bwd(j, ct):
    return _split_dot(ct, j), jnp.zeros_like(j)


headsum.defvjp(_headsum_fwd, _headsum_bwd)


def _bdot(x, w):
    return jnp.dot(x.astype(BF16), w.astype(BF16), preferred_element_type=F32)


def _sigmoid(x):
    return 1.0 / (1.0 + jnp.exp(-x))


def _softplus(x):
    return jnp.maximum(x, 0.0) + jnp.log(1.0 + jnp.exp(-jnp.abs(x)))


def rms_fn(x, g):
    return (x * lax.rsqrt(jnp.mean(x * x, axis=-1, keepdims=True) + RMS_EPS) * g,)


def rms_res_fn(x, g):
    return rms_fn(x, g)[0], x


def pre_fn(rkv, lora, w0, w2p, a0, a2p, g2p, k_k, k_a, jm):
    r, k, v = rkv[:, :C], rkv[:, C:2 * C], rkv[:, 2 * C:]
    zwa, zg = lora[:, :LANES], lora[:, LANES:LANES + 256]
    w_log = -_softplus(-(w0 + _bdot(jnp.tanh(zwa), w2p))) - 0.5
    decay = jnp.exp(-jnp.exp(w_log))
    a = _sigmoid(a0 + _bdot(zwa, a2p))
    g = _bdot(_sigmoid(zg), g2p)
    kk = k * k_k
    kk = kk / jnp.maximum(jnp.sqrt(headsum(kk * kk, jm)), L2_EPS)
    k2 = k * (1.0 + (a - 1.0) * k_a)
    return r, decay, k2, v, -kk, kk * a, g


def post_fn(y, r, k2, v, g, gn_w, gn_b, r_k, jm):
    mu = headsum(y, jm) * (1.0 / HEAD)
    yc = y - mu
    var = headsum(yc * yc, jm) * (1.0 / HEAD)
    yn = yc * lax.rsqrt(var + GN_EPS) * gn_w + gn_b
    bonus = headsum(r * k2 * r_k, jm) * v
    return ((yn + bonus) * g,)


def _gelu(x):
    return 0.5 * x * (1.0 + lax.erf(x * np.float32(1.0 / np.sqrt(2.0))))


@jax.custom_vjp
def expand_groups(b_t, e, e_t):
    return _split_dot(b_t, e)


def _expand_groups_fwd(b_t, e, e_t):
    return _split_dot(b_t, e), (e, e_t)


def _expand_groups_bwd(res, ct):
    e, e_t = res
    return _split_dot(ct, e_t), jnp.zeros_like(e), jnp.zeros_like(e_t)


expand_groups.defvjp(_expand_groups_fwd, _expand_groups_bwd)


def sgu_fn(pu, pv, ln_g, ln_b, w_s, b_t, out_g, e, e_t):
    rows = pu.shape[0]
    b_exp = expand_groups(b_t, e, e_t)
    u, v = _gelu(pu), _gelu(pv)
    mu = jnp.mean(v, axis=-1, keepdims=True)
    vc = v - mu
    var = jnp.mean(vc * vc, axis=-1, keepdims=True)
    v = vc * lax.rsqrt(var + LN_EPS) * ln_g + ln_b
    tri = lax.broadcasted_iota(jnp.int32, (CHUNK, CHUNK), 0) >= lax.broadcasted_iota(jnp.int32, (CHUNK, CHUNK), 1)
    left = lax.broadcasted_iota(jnp.int32, (CHUNK, LANES), 1) < HEAD
    chunks = []
    for c0 in range(0, rows, CHUNK):
        cols = []
        for gp in range(HEADS // 2):
            vp = v[c0:c0 + CHUNK, gp * LANES:(gp + 1) * LANES]
            wa = jnp.where(tri, w_s[2 * gp], 0.0)
            wb = jnp.where(tri, w_s[2 * gp + 1], 0.0)
            cols.append(jnp.where(left, _bdot(wa, vp), _bdot(wb, vp)))
        chunks.append(jnp.concatenate(cols, axis=1) + b_exp)
    s = jnp.concatenate(chunks, axis=0) if len(chunks) > 1 else chunks[0]
    y = u * s
    return (y * lax.rsqrt(jnp.mean(y * y, axis=-1, keepdims=True) + RMS_EPS) * out_g,)


def swiglu_fn(gate, up):
    return (gate * _sigmoid(gate) * up,)


SCAN_CHUNK = 32
NB = C // LANES
STEP_GROUP = 4


def to_cols(a):
    t = a.shape[0]
    return a.reshape(t // 8, 8, NB, 2, HEAD).transpose(0, 4, 3, 1, 2).reshape(t // 8, HEAD, LANES)


def from_cols(a):
    t8 = a.shape[0]
    return a.reshape(t8, HEAD, 2, 8, NB).transpose(0, 3, 4, 2, 1).reshape(t8 * 8, C)


def _scan_consts():
    i = np.arange(LANES)
    j2 = (i[:, None] // HEAD) == (i[None, :] // HEAD)
    sel = (i[None, :] % HEAD) == (np.arange(8 * NB * HEAD)[:, None] // HEAD)
    return jnp.asarray(j2, BF16), jnp.asarray(np.concatenate([j2, j2], 0), BF16), jnp.asarray(sel, BF16)


def _stack(blocks):
    return jnp.concatenate(blocks, axis=0) if len(blocks) > 1 else blocks[0]


def _unstack(x, n):
    return [x[i * HEAD:(i + 1) * HEAD] for i in range(n)]


def _headsums(blocks, j2k, group):
    res = []
    for g0 in range(0, len(blocks), group):
        x = _stack(blocks[g0:g0 + group])
        hi = x.astype(BF16)
        lo = (x - hi.astype(F32)).astype(BF16)
        out = jnp.dot(jnp.concatenate([hi, lo], axis=1), j2k, preferred_element_type=F32)
        res += _unstack(out, len(blocks[g0:g0 + group]))
    return res


def _headsums_out(blocks, j2):
    return _unstack(jnp.dot(_stack(blocks).astype(BF16), j2, preferred_element_type=F32), len(blocks))


def _expand8(tile, sel, j2):
    lhs = jnp.tile(tile.astype(BF16), (8 * NB, 1)) * sel
    return _unstack(jnp.dot(lhs, j2, preferred_element_type=F32), 8 * NB)


def _collapse(tile, blocks, first, lane63):
    for n, blk in enumerate(blocks):
        tile = jnp.where(jnp.tile(lane63 == first + n, (HEAD // 8, 1)), blk, tile)
    return tile


def _row(tile, j, cb):
    return jnp.broadcast_to(tile[j:j + 1, cb * LANES:(cb + 1) * LANES], (HEAD, LANES))


def _scan_step(s_ref, rows, vexp, j2k, j, hist=None):
    w_t, k_t, a_t, b_t, r_t = rows
    s = [s_ref[:, cb * LANES:(cb + 1) * LANES] for cb in range(NB)]
    sab = _headsums([s[cb] * _row(a_t, j, cb) for cb in range(NB)], j2k, STEP_GROUP)
    out = []
    for cb in range(NB):
        sl = slice(cb * LANES, (cb + 1) * LANES)
        s_new = s[cb] * _row(w_t, j, cb) + sab[cb] * _row(b_t, j, cb) + vexp[j * NB + cb] * _row(k_t, j, cb)
        s_ref[:, sl] = s_new
        if hist is None:
            out.append(s_new * _row(r_t, j, cb))
        else:
            s_hist, sab_hist, idx = hist
            s_hist[idx + 1, :, sl] = s_new
            sab_hist[idx, :, sl] = sab[cb]
    return out


def wkv_fwd(r, w, k, v3, a, b):
    t = r.shape[0]
    nc = t // SCAN_CHUNK
    n8 = SCAN_CHUNK // 8

    def body(r_ref, w_ref, k_ref, a_ref, b_ref, v3_ref, j2_ref, j2k_ref, sel_ref, y3_ref, ck_ref, s_ref):
        @pl.when(pl.program_id(0) == 0)
        def _():
            s_ref[...] = jnp.zeros_like(s_ref)

        ck_ref[0] = s_ref[...]
        j2, j2k, sel = j2_ref[...], j2k_ref[...], sel_ref[...]
        lane63 = lax.broadcasted_iota(jnp.int32, (8, LANES), 1) & (HEAD - 1)

        def t8_body(t8, carry):
            row0 = pl.multiple_of(t8 * 8, 8)
            rows = [ref[pl.ds(row0, 8), :] for ref in (w_ref, k_ref, a_ref, b_ref, r_ref)]
            vexp = _expand8(v3_ref[t8], sel, j2)
            y3 = jnp.zeros((HEAD, LANES), F32)
            for j in range(8):
                y3 = _collapse(y3, _headsums_out(_scan_step(s_ref, rows, vexp, j2k, j), j2), j * NB, lane63)
            y3_ref[t8] = y3
            return carry

        lax.fori_loop(0, n8, t8_body, 0)

    row_spec = pl.BlockSpec((SCAN_CHUNK, C), lambda i: (i, 0))
    col_spec = pl.BlockSpec((n8, HEAD, LANES), lambda i: (i, 0, 0))
    consts = _scan_consts()
    return _pcall(
        body, name="wkv_fwd", grid=(nc,),
        in_specs=[row_spec] * 5 + [col_spec] + [pl.BlockSpec(c.shape, lambda i: (0, 0)) for c in consts],
        out_specs=[col_spec, pl.BlockSpec((1, HEAD, C), lambda i: (i, 0, 0))],
        out_shape=[jax.ShapeDtypeStruct((t // 8, HEAD, LANES), F32), jax.ShapeDtypeStruct((nc, HEAD, C), F32)],
        scratch_shapes=[pltpu.VMEM((HEAD, C), F32)],
        compiler_params=_cparams(("arbitrary",)),
    )(r, w, k, a, b, v3, *consts)


def wkv_bwd(r, w, k, v3, a, b, ckpt, dy3):
    t = r.shape[0]
    nc = t // SCAN_CHUNK
    n8 = SCAN_CHUNK // 8

    def body(r_ref, w_ref, k_ref, a_ref, b_ref, v3_ref, dy3_ref, ck_ref, j2_ref, j2k_ref, sel_ref,
             dr_ref, dw_ref, dk_ref, da_ref, db_ref, dv3_ref, s_ref, g_ref, s_hist, sab_hist):
        @pl.when(pl.program_id(0) == 0)
        def _():
            g_ref[...] = jnp.zeros_like(g_ref)

        j2, j2k, sel = j2_ref[...], j2k_ref[...], sel_ref[...]
        lane63 = lax.broadcasted_iota(jnp.int32, (8, LANES), 1) & (HEAD - 1)
        sub = lax.broadcasted_iota(jnp.int32, (8, LANES), 0)
        s_ref[...] = ck_ref[0]
        s_hist[0] = ck_ref[0]

        def redo(t8, carry):
            row0 = pl.multiple_of(t8 * 8, 8)
            rows = [ref[pl.ds(row0, 8), :] for ref in (w_ref, k_ref, a_ref, b_ref, r_ref)]
            vexp = _expand8(v3_ref[t8], sel, j2)
            for j in range(8):
                _scan_step(s_ref, rows, vexp, j2k, j, hist=(s_hist, sab_hist, t8 * 8 + j))
            return carry

        lax.fori_loop(0, n8, redo, 0)

        def back(q, carry):
            t8 = n8 - 1 - q
            row0 = pl.multiple_of(t8 * 8, 8)
            w_t, k_t, a_t, b_t, r_t = [ref[pl.ds(row0, 8), :] for ref in (w_ref, k_ref, a_ref, b_ref, r_ref)]
            vexp = _expand8(v3_ref[t8], sel, j2)
            dyexp = _expand8(dy3_ref[t8], sel, j2)
            dv3 = jnp.zeros((HEAD, LANES), F32)
            tiles = [[jnp.zeros((8, LANES), F32) for _ in range(NB)] for _ in range(5)]
            for j in range(7, -1, -1):
                idx = t8 * 8 + j
                g = [g_ref[:, cb * LANES:(cb + 1) * LANES] + dyexp[j * NB + cb] * _row(r_t, j, cb)
                     for cb in range(NB)]
                dsab = _headsums([g[cb] * _row(b_t, j, cb) for cb in range(NB)], j2k, STEP_GROUP)
                dv3 = _collapse(dv3, _headsums_out([g[cb] * _row(k_t, j, cb) for cb in range(NB)], j2), j * NB, lane63)
                for cb in range(NB):
                    sl = slice(cb * LANES, (cb + 1) * LANES)
                    s_new = s_hist[idx + 1, :, sl]
                    s_old = s_hist[idx, :, sl]
                    sab = sab_hist[idx, :, sl]
                    sums = (s_new * dyexp[j * NB + cb], g[cb] * s_old, g[cb] * vexp[j * NB + cb],
                            s_old * dsab[cb], g[cb] * sab)
                    for n, prod in enumerate(sums):
                        rowsum = jnp.broadcast_to(jnp.sum(prod, axis=0, keepdims=True), (8, LANES))
                        tiles[n][cb] = jnp.where(sub == j, rowsum, tiles[n][cb])
                    g_ref[:, sl] = g[cb] * _row(w_t, j, cb) + dsab[cb] * _row(a_t, j, cb)
            dv3_ref[t8] = dv3
            for n, ref in enumerate((dr_ref, dw_ref, dk_ref, da_ref, db_ref)):
                for cb in range(NB):
                    ref[pl.ds(row0, 8), cb * LANES:(cb + 1) * LANES] = tiles[n][cb]
            return carry

        lax.fori_loop(0, n8, back, 0)

    row_spec = pl.BlockSpec((SCAN_CHUNK, C), lambda i: (nc - 1 - i, 0))
    col_spec = pl.BlockSpec((n8, HEAD, LANES), lambda i: (nc - 1 - i, 0, 0))
    consts = _scan_consts()
    return _pcall(
        body, name="wkv_bwd", grid=(nc,),
        in_specs=[row_spec] * 5 + [col_spec, col_spec, pl.BlockSpec((1, HEAD, C), lambda i: (nc - 1 - i, 0, 0))]
        + [pl.BlockSpec(c.shape, lambda i: (0, 0)) for c in consts],
        out_specs=[row_spec] * 5 + [col_spec],
        out_shape=[jax.ShapeDtypeStruct((t, C), F32)] * 5 + [jax.ShapeDtypeStruct((t // 8, HEAD, LANES), F32)],
        scratch_shapes=[pltpu.VMEM((HEAD, C), F32), pltpu.VMEM((HEAD, C), F32),
                        pltpu.VMEM((SCAN_CHUNK + 1, HEAD, C), F32), pltpu.VMEM((SCAN_CHUNK, HEAD, C), F32)],
        compiler_params=_cparams(("arbitrary",)),
    )(r, w, k, a, b, v3, dy3, ckpt, *consts)


def _prev_rows(cur, before, first_tile):
    last = jnp.where(first_tile, 0.0, before[7:8, :])
    row = lax.broadcasted_iota(jnp.int32, cur.shape, 0)
    return jnp.where(row == 0, last, pltpu.roll(cur, 1, 0))


def shift_fwd(p, mu, col, name):
    t, width = p.shape[0], mu.shape[1]
    tm = _tile(t, 256)

    def body(p_ref, before_ref, mu_ref, o_ref):
        cur = p_ref[...]
        prev = _prev_rows(cur, before_ref[...], pl.program_id(0) == 0)
        o_ref[...] = cur + (prev - cur) * mu_ref[...]

    return _pcall(
        body, name=name, grid=(t // tm,),
        in_specs=[pl.BlockSpec((tm, width), lambda i: (i, col)),
                  pl.BlockSpec((8, width), lambda i: (jnp.maximum(i * (tm // 8) - 1, 0), col)),
                  pl.BlockSpec((1, width), lambda i: (0, 0))],
        out_specs=pl.BlockSpec((tm, width), lambda i: (i, 0)),
        out_shape=jax.ShapeDtypeStruct((t, width), F32),
        compiler_params=_cparams(("parallel",)),
    )(p, p, mu)


def shift_bwd(dps, p, mu, col, name):
    t, width = dps.shape
    tm = _tile(t, 256)
    nt = t // tm

    def body(d_ref, after_ref, p_ref, before_ref, mu_ref, dp_ref, dmu_ref):
        i = pl.program_id(0)
        d, cur, mu_v = d_ref[...], p_ref[...], mu_ref[...]
        row = lax.broadcasted_iota(jnp.int32, d.shape, 0)
        first_after = jnp.where(i == nt - 1, 0.0, after_ref[0:1, :])
        nxt = jnp.where(row == tm - 1, first_after, pltpu.roll(d, tm - 1, 0))
        dp_ref[...] = d * (1.0 - mu_v) + nxt * mu_v
        prev = _prev_rows(cur, before_ref[...], i == 0)
        part = jnp.sum(d * (prev - cur), axis=0, keepdims=True)

        @pl.when(i == 0)
        def _():
            dmu_ref[...] = part

        @pl.when(i > 0)
        def _():
            dmu_ref[...] += part

    return _pcall(
        body, name=name, grid=(nt,),
        in_specs=[pl.BlockSpec((tm, width), lambda i: (i, 0)),
                  pl.BlockSpec((8, width), lambda i: (jnp.minimum((i + 1) * (tm // 8), t // 8 - 1), 0)),
                  pl.BlockSpec((tm, width), lambda i: (i, col)),
                  pl.BlockSpec((8, width), lambda i: (jnp.maximum(i * (tm // 8) - 1, 0), col)),
                  pl.BlockSpec((1, width), lambda i: (0, 0))],
        out_specs=[pl.BlockSpec((tm, width), lambda i: (i, 0)), pl.BlockSpec((1, width), lambda i: (0, 0))],
        out_shape=[jax.ShapeDtypeStruct((t, width), F32), jax.ShapeDtypeStruct((1, width), F32)],
        compiler_params=_cparams(("arbitrary",)),
    )(dps, dps, p, p, mu)


def loss_head(x3, tgt, g):
    t, d = x3.shape
    tm = _tile(t, 256)

    def body(x_ref, t_ref, g_ref, loss_ref, dx_ref, dg_ref):
        (y,), vjp = jax.vjp(lambda a, b: rms_fn(a, b), x_ref[...], g_ref[...])
        diff = y - t_ref[...]
        part = 0.5 * jnp.sum(jnp.mean(diff * diff, axis=-1, keepdims=True), axis=0, keepdims=True)
        dx, dg = vjp((diff * (1.0 / d),))
        dx_ref[...] = dx
        part = jnp.broadcast_to(part, (1, LANES))
        first = pl.program_id(0) == 0

        @pl.when(first)
        def _():
            loss_ref[...] = part
            dg_ref[...] = dg

        @pl.when(jnp.logical_not(first))
        def _():
            loss_ref[...] += part
            dg_ref[...] += dg

    row = pl.BlockSpec((tm, d), lambda i: (i, 0))
    vec = pl.BlockSpec((1, d), lambda i: (0, 0))
    return _pcall(
        body, name="loss_head", grid=(t // tm,), in_specs=[row, row, vec],
        out_specs=[pl.BlockSpec((1, LANES), lambda i: (0, 0)), row, vec],
        out_shape=[jax.ShapeDtypeStruct((1, LANES), F32), jax.ShapeDtypeStruct((t, d), F32),
                   jax.ShapeDtypeStruct((1, d), F32)],
        compiler_params=_cparams(("arbitrary",)),
    )(x3, tgt, g)


ADAM_LR, ADAM_B1, ADAM_B2, ADAM_EPS, ADAM_WD, ADAM_STEP = 0.001, 0.9, 0.999, 1e-08, 0.01, 10


def adamw(w, g, m, v, name):
    rows, width = w.shape
    tr = _tile(rows, 256)

    def body(w_ref, g_ref, m_ref, v_ref, d_ref, nm_ref, nv_ref):
        gv = g_ref[...]
        m_new = ADAM_B1 * m_ref[...] + (1.0 - ADAM_B1) * gv
        v_new = ADAM_B2 * v_ref[...] + (1.0 - ADAM_B2) * (gv * gv)
        m_hat = m_new / (1.0 - ADAM_B1 ** ADAM_STEP)
        v_hat = v_new / (1.0 - ADAM_B2 ** ADAM_STEP)
        d_ref[...] = -ADAM_LR * (m_hat / (jnp.sqrt(v_hat) + ADAM_EPS) + ADAM_WD * w_ref[...])
        nm_ref[...] = m_new
        nv_ref[...] = v_new

    spec = pl.BlockSpec((tr, width), lambda i: (i, 0))
    return _pcall(
        body, name=name, grid=(rows // tr,), in_specs=[spec] * 4, out_specs=[spec] * 3,
        out_shape=[jax.ShapeDtypeStruct((rows, width), F32)] * 3,
        compiler_params=_cparams(("parallel",)),
    )(w, g, m, v)


def sum_slots(q, name):
    n, rows, width = q.shape
    tr = _tile(rows, 512)

    def body(*refs):
        acc = refs[0][...].astype(F32)
        for r in refs[1:n]:
            acc = acc + r[...].astype(F32)
        refs[n][...] = acc

    specs = [pl.BlockSpec((None, tr, width), functools.partial(lambda s, i: (s, i, 0), s)) for s in range(n)]
    return _pcall(body, name=name, grid=(rows // tr,), in_specs=specs,
                  out_specs=pl.BlockSpec((tr, width), lambda i: (i, 0)),
                  out_shape=jax.ShapeDtypeStruct((rows, width), F32),
                  compiler_params=_cparams(("parallel",)))(*([q] * n))


MESH = pl.DeviceIdType.MESH
ANY_SPEC = pl.BlockSpec(memory_space=pl.ANY)


def _coords():
    return lax.axis_index("x"), lax.axis_index("y"), lax.axis_index("c")


def _other_chips(x, y):
    return [(1 - x, y), (x, 1 - y), (1 - x, 1 - y)]


def push(name, srcs, n_dst, plan_fn, start=None):
    n_arr = len(srcs)
    n_send = len(plan_fn(0, 0, 0))
    n_in = n_arr if start is None else 2 * n_arr

    def body(*refs):
        src_refs, dst_refs = refs[:n_arr], refs[n_in:n_in + n_arr]
        send_sems, recv_sems = refs[n_in + n_arr:]
        sends = plan_fn(*_coords())
        out = []
        for q, (src_ref, dst_ref) in enumerate(zip(src_refs, dst_refs)):
            out += [pltpu.make_async_remote_copy(src_ref.at[si], dst_ref.at[di], send_sems.at[q * n_send + k],
                                                 recv_sems.at[q * n_send + k], device_id=peer, device_id_type=MESH)
                    for k, (si, peer, di, _) in enumerate(sends)]
        for cp in out:
            cp.start()
        for q, (src_ref, dst_ref) in enumerate(zip(src_refs, dst_refs)):
            for k, (si, peer, _, ri) in enumerate(sends):
                pltpu.make_async_remote_copy(src_ref.at[si], dst_ref.at[ri], send_sems.at[q * n_send + k],
                                             recv_sems.at[q * n_send + k], device_id=peer,
                                             device_id_type=MESH).wait_recv()
        for cp in out:
            cp.wait_send()

    return _pcall(
        body, name=name, in_specs=[ANY_SPEC] * n_in, out_specs=[ANY_SPEC] * n_arr,
        out_shape=[jax.ShapeDtypeStruct((n_dst,) + s.shape[1:], s.dtype) for s in srcs],
        input_output_aliases={} if start is None else {n_arr + q: q for q in range(n_arr)},
        scratch_shapes=[pltpu.SemaphoreType.DMA((n_arr * n_send,)), pltpu.SemaphoreType.DMA((n_arr * n_send,))],
    )(*srcs, *([] if start is None else start))


def plan_sibling_quarters(x, y, c):
    return [(2 * s + (1 - c), (x, y, 1 - c), s, s) for s in range(4)]


def plan_chips_by_shard(x, y, c):
    me = 2 * x + y
    return [(2 * px + py, (px, py, c), me, 2 * px + py) for px, py in _other_chips(x, y)]


def plan_chips_half(x, y, c):
    me = 2 * x + y
    return [(c, (px, py, c), me, 2 * px + py) for px, py in _other_chips(x, y)]


def swap_halves(name, bufs):
    n_arr = len(bufs)

    def body(*refs):
        buf_refs = refs[n_arr:2 * n_arr]
        send_sems, recv_sems = refs[2 * n_arr:]
        x, y, c = _coords()
        out = [pltpu.make_async_remote_copy(b.at[c], b.at[c], send_sems.at[q], recv_sems.at[q],
                                            device_id=(x, y, 1 - c), device_id_type=MESH)
               for q, b in enumerate(buf_refs)]
        for cp in out:
            cp.start()
        for q, b in enumerate(buf_refs):
            pltpu.make_async_remote_copy(b.at[1 - c], b.at[1 - c], send_sems.at[q], recv_sems.at[q],
                                         device_id=(x, y, 1 - c), device_id_type=MESH).wait_recv()
        for cp in out:
            cp.wait_send()

    return _pcall(
        body, name=name, in_specs=[ANY_SPEC] * n_arr, out_specs=[ANY_SPEC] * n_arr,
        out_shape=[jax.ShapeDtypeStruct(b.shape, b.dtype) for b in bufs],
        input_output_aliases={q: q for q in range(n_arr)},
        scratch_shapes=[pltpu.SemaphoreType.DMA((n_arr,)), pltpu.SemaphoreType.DMA((n_arr,))],
    )(*bufs)


def gather_shards(mine):
    n_arr = len(mine)
    start = [jnp.broadcast_to(m[None], (4,) + m.shape).reshape((8,) + m.shape[1:]) for m in mine]

    def body(*refs):
        src_refs, dst_refs = refs[:n_arr], refs[2 * n_arr:3 * n_arr]
        ici_send, ici_recv, d2d_send, d2d_recv = refs[3 * n_arr:]
        x, y, c = _coords()
        me = 2 * x + y
        chips = _other_chips(x, y)
        out = []
        for q, (src_ref, dst_ref) in enumerate(zip(src_refs, dst_refs)):
            out += [pltpu.make_async_remote_copy(src_ref.at[c], dst_ref.at[2 * me + c], ici_send.at[3 * q + k],
                                                 ici_recv.at[3 * q + k], device_id=(px, py, c), device_id_type=MESH)
                    for k, (px, py) in enumerate(chips)]
        for cp in out:
            cp.start()
        passed = []
        for q, (src_ref, dst_ref) in enumerate(zip(src_refs, dst_refs)):
            for k, (px, py) in enumerate(chips):
                slot = dst_ref.at[2 * (2 * px + py) + c]
                pltpu.make_async_remote_copy(src_ref.at[c], slot, ici_send.at[3 * q + k], ici_recv.at[3 * q + k],
                                             device_id=(px, py, c), device_id_type=MESH).wait_recv()
                cp = pltpu.make_async_remote_copy(slot, slot, d2d_send.at[3 * q + k], d2d_recv.at[3 * q + k],
                                                  device_id=(x, y, 1 - c), device_id_type=MESH)
                cp.start()
                passed.append(cp)
        for q, dst_ref in enumerate(dst_refs):
            for k, (px, py) in enumerate(chips):
                slot = dst_ref.at[2 * (2 * px + py) + (1 - c)]
                pltpu.make_async_remote_copy(slot, slot, d2d_send.at[3 * q + k], d2d_recv.at[3 * q + k],
                                             device_id=(x, y, 1 - c), device_id_type=MESH).wait_recv()
        for cp in out + passed:
            cp.wait_send()

    got = _pcall(
        body, name="gather_shards", in_specs=[ANY_SPEC] * (2 * n_arr), out_specs=[ANY_SPEC] * n_arr,
        out_shape=[jax.ShapeDtypeStruct(s.shape, s.dtype) for s in start],
        input_output_aliases={n_arr + q: q for q in range(n_arr)},
        scratch_shapes=[pltpu.SemaphoreType.DMA((3 * n_arr,))] * 4,
    )(*mine, *start)
    return [g.reshape(4, 2 * g.shape[1], g.shape[2]) for g in got]


def add_kept(g8, other, name):
    _, rows, cols = other.shape
    tr = _tile(rows, 512)

    def body(c_ref, a_ref, b_ref, o_ref):
        o_ref[...] = (a_ref[...].astype(F32) + b_ref[...].astype(F32)).astype(BF16)

    spec = pl.BlockSpec((None, tr, cols), lambda s, i, c_ref: (s, i, 0))
    grid_spec = pltpu.PrefetchScalarGridSpec(
        num_scalar_prefetch=1, grid=(4, rows // tr),
        in_specs=[pl.BlockSpec((None, tr, cols), lambda s, i, c_ref: (2 * s + c_ref[0], i, 0)), spec],
        out_specs=spec)
    return _pcall(body, name=name, grid_spec=grid_spec, out_shape=jax.ShapeDtypeStruct((4, rows, cols), BF16),
                  compiler_params=_cparams(("parallel", "parallel")))(
        lax.axis_index("c").astype(jnp.int32).reshape(1), g8, other)


def reduce_scatter_big(grads, names):
    g8 = [g.reshape(8, g.shape[1] // 2, g.shape[2]) for g in grads]
    from_sibling = push("rs_sibling", g8, 4, plan_sibling_quarters)
    chip_sum = [add_kept(a, b, "rs_add_" + n) for a, b, n in zip(g8, from_sibling, names)]
    by_chip = push("rs_chips", chip_sum, 4, plan_chips_by_shard, start=chip_sum)
    total_half = [sum_slots(q, "rs_sum_" + n) for q, n in zip(by_chip, names)]
    both = swap_halves("rs_halves", [jnp.broadcast_to(t[None], (2,) + t.shape) for t in total_half])
    return [b.reshape(2 * b.shape[1], b.shape[2]) for b in both]


def _pack(parts, dtype, rows):
    flat = jnp.concatenate([p.astype(dtype).reshape(-1) for p in parts])
    return jnp.pad(flat, (0, rows * C - flat.shape[0])).reshape(rows, C)


def _unpack(flat, shapes):
    flat = flat.reshape(-1)
    out, at = [], 0
    for s in shapes:
        n = int(np.prod(s))
        out.append(flat[at:at + n].reshape(s))
        at += n
    return out


def all_reduce_small(vec):
    rows, width = vec.shape
    (pair,) = swap_halves("ar_sibling", [jnp.broadcast_to(vec[None], (2, rows, width))])
    chip_sum = sum_slots(pair, "ar_sum_sibling").reshape(2, rows // 2, width)
    mine = lax.dynamic_index_in_dim(chip_sum, lax.axis_index("c"), axis=0, keepdims=True)
    (by_chip,) = push("ar_chips", [chip_sum], 4, plan_chips_half,
                      start=[jnp.broadcast_to(mine, (4, rows // 2, width))])
    total_half = sum_slots(by_chip, "ar_sum_chips")
    (both,) = swap_halves("ar_halves", [jnp.broadcast_to(total_half[None], (2, rows // 2, width))])
    return both.reshape(rows, width)


SMALL_ROWS = 576


def kernel(x, norm1_g, w_in, shift_mu, w0, w2, a0, a2, g2, k_k, k_a, r_k, gn_w, gn_b, sgu_ln_g, sgu_ln_b, sgu_w, sgu_b, sgu_out_g, w_out, norm2_g, w_gate, w_up, w_down, final_g, loss_target, m_norm1_g, m_w_in, m_shift_mu, m_w0, m_w2, m_a0, m_a2, m_g2, m_k_k, m_k_a, m_r_k, m_gn_w, m_gn_b, m_sgu_ln_g, m_sgu_ln_b, m_sgu_w, m_sgu_b, m_sgu_out_g, m_w_out, m_norm2_g, m_w_gate, m_w_up, m_w_down, m_final_g, v_norm1_g, v_w_in, v_shift_mu, v_w0, v_w2, v_a0, v_a2, v_g2, v_k_k, v_k_a, v_r_k, v_gn_w, v_gn_b, v_sgu_ln_g, v_sgu_ln_b, v_sgu_w, v_sgu_b, v_sgu_out_g, v_w_out, v_norm2_g, v_w_gate, v_w_up, v_w_down, v_final_g):
    args = dict(locals())
    names = ["norm1_g", "w_in", "shift_mu", "w0", "w2", "a0", "a2", "g2", "k_k", "k_a", "r_k", "gn_w", "gn_b",
             "sgu_ln_g", "sgu_ln_b", "sgu_w", "sgu_b", "sgu_out_g", "w_out", "norm2_g", "w_gate", "w_up", "w_down",
             "final_g"]
    big = ["w_in", "w_out", "w_gate", "w_up", "w_down"]
    xi, yi = lax.axis_index("x"), lax.axis_index("y")
    chip = 2 * xi + yi
    x2d, tgt = x[0], loss_target[0]
    t = x2d.shape[0]

    lora_names = ["w2", "a2", "g2"]
    gathered = big + lora_names
    mine = [args[n][0].astype(BF16) for n in gathered]
    got = gather_shards([m.reshape(2, m.shape[0] // 2, m.shape[1]) for m in mine])
    full = {}
    for n, blk in zip(gathered, got):
        if n in ("w_out", "w_down"):
            full[n] = blk.reshape(4 * blk.shape[1], blk.shape[2])
        elif n in ("w_gate", "w_up"):
            full[n] = blk
        else:
            full[n] = blk.transpose(1, 0, 2).reshape(blk.shape[1], 4 * blk.shape[2])
    w_in_f = full["w_in"]
    w_in_p = jnp.concatenate([w_in_f[:, :3 * C], w_in_f[:, 3360:], w_in_f[:, 3 * C:3360],
                              jnp.zeros((D, P_W - D_IN), BF16)], axis=1)
    mu_rkv = shift_mu[:, :3 * C]
    mu_lora = jnp.pad(shift_mu[:, 3 * C:], ((0, 0), (0, LORA_W - 288)))
    w2p = jnp.pad(full["w2"], ((0, 64), (0, 0)))
    a2p = jnp.pad(full["a2"], ((64, 0), (0, 0)))
    g2p = jnp.pad(full["g2"], ((0, 96), (0, 0)))
    ii = np.arange(C)
    jm = jnp.asarray((ii[:, None] // HEAD) == (ii[None, :] // HEAD), BF16)
    e_np = (np.arange(LANES)[:, None] == (ii[None, :] // HEAD))
    e_mat, e_t = jnp.asarray(e_np, BF16), jnp.asarray(e_np.T, BF16)
    b_t = jnp.pad(sgu_b[0].T, ((0, 0), (0, LANES - HEADS)))
    r_k_flat = r_k.reshape(1, C)
    tm_row = _tile(t, 256)
    tm_vjp = _tile(t, 128)

    (h1,) = row_fwd("norm1", rms_fn, [Tiled(x2d)], [norm1_g], [], [(D, BF16)], tm_row)
    p = mm(h1, w_in_p, name="proj_in")
    ps_rkv = shift_fwd(p, mu_rkv, 0, "shift_rkv")
    ps_lora = shift_fwd(p, mu_lora, P_LORA // LORA_W, "shift_lora")
    pre_par = [w0, w2p, a0, a2p, g2p, k_k, k_a]
    r_, w_, k2, v_, an, b_, g_ = row_fwd("pre", pre_fn, [Tiled(ps_rkv), Tiled(ps_lora)], pre_par, [jm],
                                         [(C, F32)] * 7, tm_vjp)
    v3 = to_cols(v_)
    y3, ckpt = wkv_fwd(r_, w_, k2, v3, an, b_)
    y_ = from_cols(y3)
    post_in = [Tiled(z) for z in (y_, r_, k2, v_, g_)]
    post_par = [gn_w, gn_b, r_k_flat]
    (y_rwkv,) = row_fwd("post", post_fn, post_in, post_par, [jm], [(C, BF16)], tm_vjp)
    sgu_in = [Tiled(p, C, 3), Tiled(p, C, 4)]
    sgu_par = [sgu_ln_g, sgu_ln_b, sgu_w[0], b_t, sgu_out_g]
    (y_sgu,) = row_fwd("sgu", sgu_fn, sgu_in, sgu_par, [e_mat, e_t], [(C, BF16)], tm_vjp)
    y_cat = jnp.concatenate([y_rwkv, y_sgu], axis=1)
    x_mid = mm(y_cat, full["w_out"], res=x2d, name="proj_out")
    (h2,) = row_fwd("norm2", rms_fn, [Tiled(x_mid)], [norm2_g], [], [(D, BF16)], tm_row)
    gate = mm(h2, full["w_gate"], shards="b", name="ffn_gate")
    up = mm(h2, full["w_up"], shards="b", name="ffn_up")
    (act,) = row_fwd("swiglu", swiglu_fn, [Tiled(gate), Tiled(up)], [], [], [(D_FF, BF16)], tm_row)
    x_out = mm(act, full["w_down"], res=x_mid, name="ffn_down")
    loss_part, dx_out, d_final_g = loss_head(x_out, tgt, final_g.reshape(1, D))

    grads = {"final_g": d_final_g.reshape(D)}
    d_act = mm(dx_out, full["w_down"], tb=True, name="d_act")
    grads["w_down"] = mm(act, dx_out, ta=True, out_dtype=BF16, name="d_w_down")
    d_gate, d_up = row_bwd("swiglu_b", swiglu_fn, [Tiled(gate), Tiled(up)], [], [], [Tiled(d_act)], tm_row,
                           tiled_out_dtypes=[BF16, BF16])
    grads["w_gate"] = mm(h2, d_gate, shards="out", out_dtype=BF16, name="d_w_gate")
    grads["w_up"] = mm(h2, d_up, shards="out", out_dtype=BF16, name="d_w_up")
    d_h2 = mm(d_gate, full["w_gate"], shards="bt", name="d_h2_gate")
    d_h2 = mm(d_up, full["w_up"], shards="bt", res=d_h2, name="d_h2_up")
    dx_mid, grads["norm2_g"] = row_bwd("norm2_b", rms_res_fn, [Tiled(x_mid)], [norm2_g], [],
                                       [Tiled(d_h2), Tiled(dx_out)], tm_row)
    d_ycat = mm(dx_mid, full["w_out"], tb=True, name="d_ycat")
    grads["w_out"] = mm(y_cat, dx_mid, ta=True, out_dtype=BF16, name="d_w_out")
    d_pu, d_pv, grads["sgu_ln_g"], grads["sgu_ln_b"], d_sgu_w, d_b_t, grads["sgu_out_g"] = row_bwd(
        "sgu_b", sgu_fn, sgu_in, sgu_par, [e_mat, e_t], [Tiled(d_ycat, C, 1)], tm_vjp)
    grads["sgu_w"] = d_sgu_w[None]
    grads["sgu_b"] = d_b_t[:, :HEADS].T[None]
    d_y, d_r1, d_k1, d_v1, d_g, grads["gn_w"], grads["gn_b"], d_r_k = row_bwd(
        "post_b", post_fn, post_in, post_par, [jm], [Tiled(d_ycat, C, 0)], tm_vjp)
    grads["r_k"] = d_r_k.reshape(r_k.shape)
    d_r2, d_w, d_k2, d_a, d_b, d_v3 = wkv_bwd(r_, w_, k2, v3, an, b_, ckpt, to_cols(d_y))
    d_v2 = from_cols(d_v3)
    pre_cts = [(Tiled(d_r1), Tiled(d_r2)), Tiled(d_w), (Tiled(d_k1), Tiled(d_k2)), (Tiled(d_v1), Tiled(d_v2)),
               Tiled(d_a), Tiled(d_b), Tiled(d_g)]
    d_ps_rkv, d_ps_lora, grads["w0"], d_w2p, grads["a0"], d_a2p, d_g2p, grads["k_k"], grads["k_a"] = row_bwd(
        "pre_b", pre_fn, [Tiled(ps_rkv), Tiled(ps_lora)], pre_par, [jm], pre_cts, tm_vjp)
    grads["w2"], grads["a2"], grads["g2"] = d_w2p[None, :64], d_a2p[None, 64:], d_g2p[None, :160]
    d_p_rkv, d_mu_rkv = shift_bwd(d_ps_rkv, p, mu_rkv, 0, "shift_rkv_b")
    d_p_lora, d_mu_lora = shift_bwd(d_ps_lora, p, mu_lora, P_LORA // LORA_W, "shift_lora_b")
    grads["shift_mu"] = jnp.concatenate([d_mu_rkv, d_mu_lora[:, :288]], axis=1)
    d_p = jnp.concatenate([d_p_rkv, d_pu, d_pv, d_p_lora], axis=1)
    d_h1 = mm(d_p, w_in_p, tb=True, name="d_h1")
    d_w_in_p = mm(h1, d_p, ta=True, name="d_w_in")
    d_w_in = jnp.concatenate([d_w_in_p[:, :3 * C], d_w_in_p[:, P_LORA:P_LORA + 288], d_w_in_p[:, 3 * C:P_LORA]],
                             axis=1).astype(BF16)
    grads["w_in"] = d_w_in.reshape(D, 4, D_IN // 4).transpose(1, 0, 2)
    dx, grads["norm1_g"] = row_bwd("norm1_b", rms_res_fn, [Tiled(x2d)], [norm1_g], [],
                                   [Tiled(d_h1), Tiled(dx_mid)], tm_row)

    by_shard = [grads[n].reshape((4,) + args[n].shape[1:]) for n in big]
    total = {n: g[None] for n, g in zip(big, reduce_scatter_big(by_shard, big))}
    small = [n for n in names if n not in big]
    small_shapes = [(1, 64, C) if n in ("w2", "a2") else (1, 160, C) if n == "g2" else args[n].shape for n in small]
    vec = _pack([grads[n].reshape(1, -1) for n in small], F32, SMALL_ROWS)
    for n, g_sum in zip(small, _unpack(all_reduce_small(vec), small_shapes)):
        total[n] = lax.dynamic_slice_in_dim(g_sum, chip * 256, 256, axis=2) if n in lora_names else g_sum
    loss = lax.psum(loss_part[0, 0], ("x", "y", "c"))

    delta, new_m, new_v = {}, {}, {}
    for n in big:
        s = args[n].shape
        d_, m_, v__ = adamw(args[n][0], total[n][0], args["m_" + n][0], args["v_" + n][0], "adamw_" + n)
        delta[n], new_m[n], new_v[n] = d_.reshape(s), m_.reshape(s), v__.reshape(s)
    adam_rows = -(-sum(int(np.prod(args[n].shape)) for n in small) // (8 * C)) * 8
    flat = [_pack([src[n].reshape(1, -1) for n in small], F32, adam_rows)
            for src in (args, total, {n: args["m_" + n] for n in small}, {n: args["v_" + n] for n in small})]
    outs = adamw(*flat, "adamw_small")
    for res, o in zip((delta, new_m, new_v), outs):
        res.update(zip(small, _unpack(o, [args[n].shape for n in small])))
    return (loss, dx[None], *[total[n] for n in names], *[delta[n] for n in names],
            *[new_m[n] for n in names], *[new_v[n] for n in names])
```

```python
import functools

import numpy as np
import jax
import jax.numpy as jnp
from jax import lax
from jax.experimental import pallas as pl
from jax.experimental.pallas import tpu as pltpu

F32 = jnp.float32
BF16 = jnp.bfloat16

D = 2048
C = 1024
HEADS = 16
HEAD = 64
LANES = 128
P_W = 5632
P_LORA = 5120
LORA_W = 512
D_IN = 5408
D_FF = 5632
CHUNK = 128
RMS_EPS = 1e-6
LN_EPS = 1e-5
GN_EPS = 64e-5
L2_EPS = 1e-12
VMEM_LIMIT = 56 * 1024 * 1024


def _pcall(body, **kw):
    return pl.pallas_call(body, **kw)


def _cparams(sem):
    return pltpu.CompilerParams(dimension_semantics=sem, vmem_limit_bytes=VMEM_LIMIT)


def _tile(n, most):
    t = most
    while t > 8 and n % t:
        t //= 2
    assert n % t == 0, (n, most)
    return t


MM_TILE = (2048, 512, 2048)
MM_VMEM = 40 * 1024 * 1024


def _div_tile(n, most, quantum=LANES):
    for t in range(min(n, most) // quantum * quantum, 0, -quantum):
        if n % t == 0:
            return t
    raise ValueError((n, most, quantum))


def _mm_tiles(m, n, k, a_bytes, b_bytes, r_bytes, o_bytes):
    tm, tn, tk = _div_tile(m, MM_TILE[0]), _div_tile(n, MM_TILE[1]), _div_tile(k, MM_TILE[2])

    def need(tm, tk):
        return 2 * (tm * tk * a_bytes + tk * tn * b_bytes + tm * tn * (r_bytes + o_bytes)) + tm * tn * 4

    while need(tm, tk) > MM_VMEM:
        if tk >= tm and tk > 512:
            tk = _div_tile(k, tk - LANES)
        else:
            tm = _div_tile(m, tm - LANES)
    return tm, tn, tk


def mm(a, b, *, ta=False, tb=False, res=None, out_dtype=F32, name, shards=None):
    if shards is not None:
        return _mm_shards(a, b, res, out_dtype, name, shards)
    m, k = (a.shape[1], a.shape[0]) if ta else a.shape
    n = b.shape[0] if tb else b.shape[1]
    assert (b.shape[1] if tb else b.shape[0]) == k
    tm, tn, tk = _mm_tiles(m, n, k, a.dtype.itemsize, b.dtype.itemsize, 0 if res is None else res.dtype.itemsize,
                           jnp.dtype(out_dtype).itemsize)
    nk = k // tk
    dims = (((0 if ta else 1,), (1 if tb else 0,)), ((), ()))
    a_spec = pl.BlockSpec((tk, tm), lambda i, j, l: (l, i)) if ta else pl.BlockSpec((tm, tk), lambda i, j, l: (i, l))
    b_spec = pl.BlockSpec((tn, tk), lambda i, j, l: (j, l)) if tb else pl.BlockSpec((tk, tn), lambda i, j, l: (l, j))
    o_spec = pl.BlockSpec((tm, tn), lambda i, j, l: (i, j))
    return _mm_call(a, b, res, dims, (m // tm, n // tn, nk), a_spec, b_spec, o_spec, o_spec, (tm, tn),
                    jax.ShapeDtypeStruct((m, n), out_dtype), name)


def _mm_shards(a, b, res, out_dtype, name, shards):
    if shards == "b":
        (m, k), ns = a.shape, b.shape[2]
        tm, tk = _div_tile(m, 1024), _div_tile(k, 2048)
        grid, dims, acc = (m // tm, 4, k // tk), (((1,), (0,)), ((), ())), (tm, ns)
        a_spec = pl.BlockSpec((tm, tk), lambda i, s, l: (i, l))
        b_spec = pl.BlockSpec((None, tk, ns), lambda i, s, l: (s, l, 0))
        o_spec = pl.BlockSpec((tm, ns), lambda i, s, l: (i, s))
        out = jax.ShapeDtypeStruct((m, 4 * ns), out_dtype)
    elif shards == "bt":
        m, (_, n, ns) = a.shape[0], b.shape
        tm, tn = _div_tile(m, 2048), _div_tile(n, 512)
        grid, dims, acc = (m // tm, n // tn, 4), (((1,), (1,)), ((), ())), (tm, tn)
        a_spec = pl.BlockSpec((tm, ns), lambda i, j, s: (i, s))
        b_spec = pl.BlockSpec((None, tn, ns), lambda i, j, s: (s, j, 0))
        o_spec = pl.BlockSpec((tm, tn), lambda i, j, s: (i, j))
        out = jax.ShapeDtypeStruct((m, n), out_dtype)
    else:
        (t, m), ns = a.shape, b.shape[1] // 4
        tm, tk = _div_tile(m, 1024), _div_tile(t, 2048)
        grid, dims, acc = (m // tm, 4, t // tk), (((0,), (0,)), ((), ())), (tm, ns)
        a_spec = pl.BlockSpec((tk, tm), lambda i, s, l: (l, i))
        b_spec = pl.BlockSpec((tk, ns), lambda i, s, l: (l, s))
        o_spec = pl.BlockSpec((None, tm, ns), lambda i, s, l: (s, i, 0))
        out = jax.ShapeDtypeStruct((4, m, ns), out_dtype)
    return _mm_call(a, b, res, dims, grid, a_spec, b_spec, o_spec, o_spec, acc, out, name)


def _mm_call(a, b, res, dims, grid, a_spec, b_spec, r_spec, o_spec, acc_shape, out, name):
    nk = grid[2]
    out_dtype = out.dtype

    def body(*refs):
        a_ref, b_ref = refs[:2]
        r_ref = None if res is None else refs[2]
        o_ref = refs[2 + (res is not None)]

        def finish(acc):
            if r_ref is not None:
                acc = acc + r_ref[...].astype(F32)
            o_ref[...] = acc.astype(out_dtype)

        prod = lax.dot_general(a_ref[...].astype(BF16), b_ref[...].astype(BF16), dims, preferred_element_type=F32)
        if nk == 1:
            finish(prod)
            return
        acc_ref = refs[-1]
        kk = pl.program_id(2)

        @pl.when(kk == 0)
        def _():
            acc_ref[...] = prod

        @pl.when(kk > 0)
        def _():
            acc_ref[...] += prod

        @pl.when(kk == nk - 1)
        def _():
            finish(acc_ref[...])

    in_specs = [a_spec, b_spec] + ([r_spec] if res is not None else [])
    args = (a, b) + ((res,) if res is not None else ())
    return _pcall(
        body, name=name, grid=grid, in_specs=in_specs, out_specs=o_spec, out_shape=out,
        scratch_shapes=[] if nk == 1 else [pltpu.VMEM(acc_shape, F32)],
        compiler_params=_cparams(("parallel", "parallel", "arbitrary")),
    )(*args)


class Tiled:
    def __init__(self, arr, width=None, col=0):
        self.arr, self.width, self.col = arr, (arr.shape[1] if width is None else width), col

    def spec(self, tm):
        col = self.col
        return pl.BlockSpec((tm, self.width), lambda i: (i, col))


def _full_spec(p):
    nd = p.ndim
    return pl.BlockSpec(p.shape, lambda i: (0,) * nd)


def row_fwd(name, fn, tiled, params, consts, outs, tm):
    t = tiled[0].arr.shape[0]
    n_in = len(tiled) + len(params) + len(consts)

    def body(*refs):
        res = fn(*[r[...] for r in refs[:n_in]])
        for o_ref, r in zip(refs[n_in:], res):
            o_ref[...] = r.astype(o_ref.dtype)

    return _pcall(
        body, name=name, grid=(t // tm,),
        in_specs=[x.spec(tm) for x in tiled] + [_full_spec(p) for p in params + consts],
        out_specs=[pl.BlockSpec((tm, w), lambda i: (i, 0)) for w, _ in outs],
        out_shape=[jax.ShapeDtypeStruct((t, w), dt) for w, dt in outs],
        compiler_params=_cparams(("parallel",)),
    )(*[x.arr for x in tiled], *params, *consts)


def row_bwd(name, fn, tiled, params, consts, cts, tm, tiled_out_dtypes=None):
    t = tiled[0].arr.shape[0]
    nt, npar, ncon = len(tiled), len(params), len(consts)
    cts = [c if isinstance(c, tuple) else (c,) for c in cts]
    flat_cts = [x for c in cts for x in c]
    tiled_out_dtypes = tiled_out_dtypes or [F32] * nt

    def body(*refs):
        n_in = nt + npar + ncon
        ins = [r[...].astype(F32) for r in refs[:nt + npar]]
        con = [r[...] for r in refs[nt + npar:n_in]]
        ct_refs = list(refs[n_in:n_in + len(flat_cts)])
        o = refs[n_in + len(flat_cts):]
        ct = []
        for c in cts:
            parts = [ct_refs.pop(0)[...].astype(F32) for _ in c]
            ct.append(functools.reduce(lambda p, q: p + q, parts))
        _, vjp = jax.vjp(lambda *a: fn(*a, *con), *ins)
        g = vjp(tuple(ct))
        for j in range(nt):
            o[j][...] = g[j].astype(o[j].dtype)
        first = pl.program_id(0) == 0

        @pl.when(first)
        def _():
            for j in range(npar):
                o[nt + j][...] = g[nt + j]

        @pl.when(jnp.logical_not(first))
        def _():
            for j in range(npar):
                o[nt + j][...] += g[nt + j]

    return _pcall(
        body, name=name, grid=(t // tm,),
        in_specs=[x.spec(tm) for x in tiled] + [_full_spec(p) for p in params + consts]
        + [x.spec(tm) for x in flat_cts],
        out_specs=[pl.BlockSpec((tm, x.width), lambda i: (i, 0)) for x in tiled] + [_full_spec(p) for p in params],
        out_shape=[jax.ShapeDtypeStruct((t, x.width), dt) for x, dt in zip(tiled, tiled_out_dtypes)]
        + [jax.ShapeDtypeStruct(p.shape, F32) for p in params],
        compiler_params=_cparams(("arbitrary",)),
    )(*[x.arr for x in tiled], *params, *consts, *[x.arr for x in flat_cts])


def _split_dot(x, w):
    hi = x.astype(BF16)
    lo = (x - hi.astype(F32)).astype(BF16)
    return jnp.dot(hi, w, preferred_element_type=F32) + jnp.dot(lo, w, preferred_element_type=F32)


@jax.custom_vjp
def headsum(x, j):
    return _split_dot(x, j)


def _headsum_fwd(x, j):
    return _split_dot(x, j), j


def _headsum_bwd(j, ct):
    return _split_dot(ct, j), jnp.zeros_like(j)


headsum.defvjp(_headsum_fwd, _headsum_bwd)


def _bdot(x, w):
    return jnp.dot(x.astype(BF16), w.astype(BF16), preferred_element_type=F32)


def _sigmoid(x):
    return 1.0 / (1.0 + jnp.exp(-x))


def _softplus(x):
    return jnp.maximum(x, 0.0) + jnp.log(1.0 + jnp.exp(-jnp.abs(x)))


def rms_fn(x, g):
    return (x * lax.rsqrt(jnp.mean(x * x, axis=-1, keepdims=True) + RMS_EPS) * g,)


def rms_res_fn(x, g):
    return rms_fn(x, g)[0], x


def pre_fn(rkv, lora, w0, w2p, a0, a2p, g2p, k_k, k_a, jm):
    r, k, v = rkv[:, :C], rkv[:, C:2 * C], rkv[:, 2 * C:]
    zwa, zg = lora[:, :LANES], lora[:, LANES:LANES + 256]
    w_log = -_softplus(-(w0 + _bdot(jnp.tanh(zwa), w2p))) - 0.5
    decay = jnp.exp(-jnp.exp(w_log))
    a = _sigmoid(a0 + _bdot(zwa, a2p))
    g = _bdot(_sigmoid(zg), g2p)
    kk = k * k_k
    kk = kk / jnp.maximum(jnp.sqrt(headsum(kk * kk, jm)), L2_EPS)
    k2 = k * (1.0 + (a - 1.0) * k_a)
    return r, decay, k2, v, -kk, kk * a, g


def post_fn(y, r, k2, v, g, gn_w, gn_b, r_k, jm):
    mu = headsum(y, jm) * (1.0 / HEAD)
    yc = y - mu
    var = headsum(yc * yc, jm) * (1.0 / HEAD)
    yn = yc * lax.rsqrt(var + GN_EPS) * gn_w + gn_b
    bonus = headsum(r * k2 * r_k, jm) * v
    return ((yn + bonus) * g,)


def _gelu(x):
    return 0.5 * x * (1.0 + lax.erf(x * np.float32(1.0 / np.sqrt(2.0))))


@jax.custom_vjp
def expand_groups(b_t, e, e_t):
    return _split_dot(b_t, e)


def _expand_groups_fwd(b_t, e, e_t):
    return _split_dot(b_t, e), (e, e_t)


def _expand_groups_bwd(res, ct):
    e, e_t = res
    return _split_dot(ct, e_t), jnp.zeros_like(e), jnp.zeros_like(e_t)


expand_groups.defvjp(_expand_groups_fwd, _expand_groups_bwd)


def sgu_fn(pu, pv, ln_g, ln_b, w_s, b_t, out_g, e, e_t):
    rows = pu.shape[0]
    b_exp = expand_groups(b_t, e, e_t)
    u, v = _gelu(pu), _gelu(pv)
    mu = jnp.mean(v, axis=-1, keepdims=True)
    vc = v - mu
    var = jnp.mean(vc * vc, axis=-1, keepdims=True)
    v = vc * lax.rsqrt(var + LN_EPS) * ln_g + ln_b
    tri = lax.broadcasted_iota(jnp.int32, (CHUNK, CHUNK), 0) >= lax.broadcasted_iota(jnp.int32, (CHUNK, CHUNK), 1)
    left = lax.broadcasted_iota(jnp.int32, (CHUNK, LANES), 1) < HEAD
    chunks = []
    for c0 in range(0, rows, CHUNK):
        cols = []
        for gp in range(HEADS // 2):
            vp = v[c0:c0 + CHUNK, gp * LANES:(gp + 1) * LANES]
            wa = jnp.where(tri, w_s[2 * gp], 0.0)
            wb = jnp.where(tri, w_s[2 * gp + 1], 0.0)
            cols.append(jnp.where(left, _bdot(wa, vp), _bdot(wb, vp)))
        chunks.append(jnp.concatenate(cols, axis=1) + b_exp)
    s = jnp.concatenate(chunks, axis=0) if len(chunks) > 1 else chunks[0]
    y = u * s
    return (y * lax.rsqrt(jnp.mean(y * y, axis=-1, keepdims=True) + RMS_EPS) * out_g,)


def swiglu_fn(gate, up):
    return (gate * _sigmoid(gate) * up,)


SCAN_CHUNK = 32
NB = C // LANES
STEP_GROUP = 4


def to_cols(a):
    t = a.shape[0]
    return a.reshape(t // 8, 8, NB, 2, HEAD).transpose(0, 4, 3, 1, 2).reshape(t // 8, HEAD, LANES)


def from_cols(a):
    t8 = a.shape[0]
    return a.reshape(t8, HEAD, 2, 8, NB).transpose(0, 3, 4, 2, 1).reshape(t8 * 8, C)


def _scan_consts():
    i = np.arange(LANES)
    j2 = (i[:, None] // HEAD) == (i[None, :] // HEAD)
    sel = (i[None, :] % HEAD) == (np.arange(8 * NB * HEAD)[:, None] // HEAD)
    return jnp.asarray(j2, BF16), jnp.asarray(np.concatenate([j2, j2], 0), BF16), jnp.asarray(sel, BF16)


def _stack(blocks):
    return jnp.concatenate(blocks, axis=0) if len(blocks) > 1 else blocks[0]


def _unstack(x, n):
    return [x[i * HEAD:(i + 1) * HEAD] for i in range(n)]


def _headsums(blocks, j2k, group):
    res = []
    for g0 in range(0, len(blocks), group):
        x = _stack(blocks[g0:g0 + group])
        hi = x.astype(BF16)
        lo = (x - hi.astype(F32)).astype(BF16)
        out = jnp.dot(jnp.concatenate([hi, lo], axis=1), j2k, preferred_element_type=F32)
        res += _unstack(out, len(blocks[g0:g0 + group]))
    return res


def _headsums_out(blocks, j2):
    return _unstack(jnp.dot(_stack(blocks).astype(BF16), j2, preferred_element_type=F32), len(blocks))


def _expand8(tile, sel, j2):
    lhs = jnp.tile(tile.astype(BF16), (8 * NB, 1)) * sel
    return _unstack(jnp.dot(lhs, j2, preferred_element_type=F32), 8 * NB)


def _collapse(tile, blocks, first, lane63):
    for n, blk in enumerate(blocks):
        tile = jnp.where(jnp.tile(lane63 == first + n, (HEAD // 8, 1)), blk, tile)
    return tile


def _row(tile, j, cb):
    return jnp.broadcast_to(tile[j:j + 1, cb * LANES:(cb + 1) * LANES], (HEAD, LANES))


def _scan_step(s_ref, rows, vexp, j2k, j, hist=None):
    w_t, k_t, a_t, b_t, r_t = rows
    s = [s_ref[:, cb * LANES:(cb + 1) * LANES] for cb in range(NB)]
    sab = _headsums([s[cb] * _row(a_t, j, cb) for cb in range(NB)], j2k, STEP_GROUP)
    out = []
    for cb in range(NB):
        sl = slice(cb * LANES, (cb + 1) * LANES)
        s_new = s[cb] * _row(w_t, j, cb) + sab[cb] * _row(b_t, j, cb) + vexp[j * NB + cb] * _row(k_t, j, cb)
        s_ref[:, sl] = s_new
        if hist is None:
            out.append(s_new * _row(r_t, j, cb))
        else:
            s_hist, sab_hist, idx = hist
            s_hist[idx + 1, :, sl] = s_new
            sab_hist[idx, :, sl] = sab[cb]
    return out


def wkv_fwd(r, w, k, v3, a, b):
    t = r.shape[0]
    nc = t // SCAN_CHUNK
    n8 = SCAN_CHUNK // 8

    def body(r_ref, w_ref, k_ref, a_ref, b_ref, v3_ref, j2_ref, j2k_ref, sel_ref, y3_ref, ck_ref, s_ref):
        @pl.when(pl.program_id(0) == 0)
        def _():
            s_ref[...] = jnp.zeros_like(s_ref)

        ck_ref[0] = s_ref[...]
        j2, j2k, sel = j2_ref[...], j2k_ref[...], sel_ref[...]
        lane63 = lax.broadcasted_iota(jnp.int32, (8, LANES), 1) & (HEAD - 1)

        def t8_body(t8, carry):
            row0 = pl.multiple_of(t8 * 8, 8)
            rows = [ref[pl.ds(row0, 8), :] for ref in (w_ref, k_ref, a_ref, b_ref, r_ref)]
            vexp = _expand8(v3_ref[t8], sel, j2)
            y3 = jnp.zeros((HEAD, LANES), F32)
            for j in range(8):
                y3 = _collapse(y3, _headsums_out(_scan_step(s_ref, rows, vexp, j2k, j), j2), j * NB, lane63)
            y3_ref[t8] = y3
            return carry

        lax.fori_loop(0, n8, t8_body, 0)

    row_spec = pl.BlockSpec((SCAN_CHUNK, C), lambda i: (i, 0))
    col_spec = pl.BlockSpec((n8, HEAD, LANES), lambda i: (i, 0, 0))
    consts = _scan_consts()
    return _pcall(
        body, name="wkv_fwd", grid=(nc,),
        in_specs=[row_spec] * 5 + [col_spec] + [pl.BlockSpec(c.shape, lambda i: (0, 0)) for c in consts],
        out_specs=[col_spec, pl.BlockSpec((1, HEAD, C), lambda i: (i, 0, 0))],
        out_shape=[jax.ShapeDtypeStruct((t // 8, HEAD, LANES), F32), jax.ShapeDtypeStruct((nc, HEAD, C), F32)],
        scratch_shapes=[pltpu.VMEM((HEAD, C), F32)],
        compiler_params=_cparams(("arbitrary",)),
    )(r, w, k, a, b, v3, *consts)


def wkv_bwd(r, w, k, v3, a, b, ckpt, dy3):
    t = r.shape[0]
    nc = t // SCAN_CHUNK
    n8 = SCAN_CHUNK // 8

    def body(r_ref, w_ref, k_ref, a_ref, b_ref, v3_ref, dy3_ref, ck_ref, j2_ref, j2k_ref, sel_ref,
             dr_ref, dw_ref, dk_ref, da_ref, db_ref, dv3_ref, s_ref, g_ref, s_hist, sab_hist):
        @pl.when(pl.program_id(0) == 0)
        def _():
            g_ref[...] = jnp.zeros_like(g_ref)

        j2, j2k, sel = j2_ref[...], j2k_ref[...], sel_ref[...]
        lane63 = lax.broadcasted_iota(jnp.int32, (8, LANES), 1) & (HEAD - 1)
        sub = lax.broadcasted_iota(jnp.int32, (8, LANES), 0)
        s_ref[...] = ck_ref[0]
        s_hist[0] = ck_ref[0]

        def redo(t8, carry):
            row0 = pl.multiple_of(t8 * 8, 8)
            rows = [ref[pl.ds(row0, 8), :] for ref in (w_ref, k_ref, a_ref, b_ref, r_ref)]
            vexp = _expand8(v3_ref[t8], sel, j2)
            for j in range(8):
                _scan_step(s_ref, rows, vexp, j2k, j, hist=(s_hist, sab_hist, t8 * 8 + j))
            return carry

        lax.fori_loop(0, n8, redo, 0)

        def back(q, carry):
            t8 = n8 - 1 - q
            row0 = pl.multiple_of(t8 * 8, 8)
            w_t, k_t, a_t, b_t, r_t = [ref[pl.ds(row0, 8), :] for ref in (w_ref, k_ref, a_ref, b_ref, r_ref)]
            vexp = _expand8(v3_ref[t8], sel, j2)
            dyexp = _expand8(dy3_ref[t8], sel, j2)
            dv3 = jnp.zeros((HEAD, LANES), F32)
            tiles = [[jnp.zeros((8, LANES), F32) for _ in range(NB)] for _ in range(5)]
            for j in range(7, -1, -1):
                idx = t8 * 8 + j
                g = [g_ref[:, cb * LANES:(cb + 1) * LANES] + dyexp[j * NB + cb] * _row(r_t, j, cb)
                     for cb in range(NB)]
                dsab = _headsums([g[cb] * _row(b_t, j, cb) for cb in range(NB)], j2k, STEP_GROUP)
                dv3 = _collapse(dv3, _headsums_out([g[cb] * _row(k_t, j, cb) for cb in range(NB)], j2), j * NB, lane63)
                for cb in range(NB):
                    sl = slice(cb * LANES, (cb + 1) * LANES)
                    s_new = s_hist[idx + 1, :, sl]
                    s_old = s_hist[idx, :, sl]
                    sab = sab_hist[idx, :, sl]
                    sums = (s_new * dyexp[j * NB + cb], g[cb] * s_old, g[cb] * vexp[j * NB + cb],
                            s_old * dsab[cb], g[cb] * sab)
                    for n, prod in enumerate(sums):
                        rowsum = jnp.broadcast_to(jnp.sum(prod, axis=0, keepdims=True), (8, LANES))
                        tiles[n][cb] = jnp.where(sub == j, rowsum, tiles[n][cb])
                    g_ref[:, sl] = g[cb] * _row(w_t, j, cb) + dsab[cb] * _row(a_t, j, cb)
            dv3_ref[t8] = dv3
            for n, ref in enumerate((dr_ref, dw_ref, dk_ref, da_ref, db_ref)):
                for cb in range(NB):
                    ref[pl.ds(row0, 8), cb * LANES:(cb + 1) * LANES] = tiles[n][cb]
            return carry

        lax.fori_loop(0, n8, back, 0)

    row_spec = pl.BlockSpec((SCAN_CHUNK, C), lambda i: (nc - 1 - i, 0))
    col_spec = pl.BlockSpec((n8, HEAD, LANES), lambda i: (nc - 1 - i, 0, 0))
    consts = _scan_consts()
    return _pcall(
        body, name="wkv_bwd", grid=(nc,),
        in_specs=[row_spec] * 5 + [col_spec, col_spec, pl.BlockSpec((1, HEAD, C), lambda i: (nc - 1 - i, 0, 0))]
        + [pl.BlockSpec(c.shape, lambda i: (0, 0)) for c in consts],
        out_specs=[row_spec] * 5 + [col_spec],
        out_shape=[jax.ShapeDtypeStruct((t, C), F32)] * 5 + [jax.ShapeDtypeStruct((t // 8, HEAD, LANES), F32)],
        scratch_shapes=[pltpu.VMEM((HEAD, C), F32), pltpu.VMEM((HEAD, C), F32),
                        pltpu.VMEM((SCAN_CHUNK + 1, HEAD, C), F32), pltpu.VMEM((SCAN_CHUNK, HEAD, C), F32)],
        compiler_params=_cparams(("arbitrary",)),
    )(r, w, k, a, b, v3, dy3, ckpt, *consts)


def _prev_rows(cur, before, first_tile):
    last = jnp.where(first_tile, 0.0, before[7:8, :])
    row = lax.broadcasted_iota(jnp.int32, cur.shape, 0)
    return jnp.where(row == 0, last, pltpu.roll(cur, 1, 0))


def shift_fwd(p, mu, col, name):
    t, width = p.shape[0], mu.shape[1]
    tm = _tile(t, 256)

    def body(p_ref, before_ref, mu_ref, o_ref):
        cur = p_ref[...]
        prev = _prev_rows(cur, before_ref[...], pl.program_id(0) == 0)
        o_ref[...] = cur + (prev - cur) * mu_ref[...]

    return _pcall(
        body, name=name, grid=(t // tm,),
        in_specs=[pl.BlockSpec((tm, width), lambda i: (i, col)),
                  pl.BlockSpec((8, width), lambda i: (jnp.maximum(i * (tm // 8) - 1, 0), col)),
                  pl.BlockSpec((1, width), lambda i: (0, 0))],
        out_specs=pl.BlockSpec((tm, width), lambda i: (i, 0)),
        out_shape=jax.ShapeDtypeStruct((t, width), F32),
        compiler_params=_cparams(("parallel",)),
    )(p, p, mu)


def shift_bwd(dps, p, mu, col, name):
    t, width = dps.shape
    tm = _tile(t, 256)
    nt = t // tm

    def body(d_ref, after_ref, p_ref, before_ref, mu_ref, dp_ref, dmu_ref):
        i = pl.program_id(0)
        d, cur, mu_v = d_ref[...], p_ref[...], mu_ref[...]
        row = lax.broadcasted_iota(jnp.int32, d.shape, 0)
        first_after = jnp.where(i == nt - 1, 0.0, after_ref[0:1, :])
        nxt = jnp.where(row == tm - 1, first_after, pltpu.roll(d, tm - 1, 0))
        dp_ref[...] = d * (1.0 - mu_v) + nxt * mu_v
        prev = _prev_rows(cur, before_ref[...], i == 0)
        part = jnp.sum(d * (prev - cur), axis=0, keepdims=True)

        @pl.when(i == 0)
        def _():
            dmu_ref[...] = part

        @pl.when(i > 0)
        def _():
            dmu_ref[...] += part

    return _pcall(
        body, name=name, grid=(nt,),
        in_specs=[pl.BlockSpec((tm, width), lambda i: (i, 0)),
                  pl.BlockSpec((8, width), lambda i: (jnp.minimum((i + 1) * (tm // 8), t // 8 - 1), 0)),
                  pl.BlockSpec((tm, width), lambda i: (i, col)),
                  pl.BlockSpec((8, width), lambda i: (jnp.maximum(i * (tm // 8) - 1, 0), col)),
                  pl.BlockSpec((1, width), lambda i: (0, 0))],
        out_specs=[pl.BlockSpec((tm, width), lambda i: (i, 0)), pl.BlockSpec((1, width), lambda i: (0, 0))],
        out_shape=[jax.ShapeDtypeStruct((t, width), F32), jax.ShapeDtypeStruct((1, width), F32)],
        compiler_params=_cparams(("arbitrary",)),
    )(dps, dps, p, p, mu)


def loss_head(x3, tgt, g):
    t, d = x3.shape
    tm = _tile(t, 256)

    def body(x_ref, t_ref, g_ref, loss_ref, dx_ref, dg_ref):
        (y,), vjp = jax.vjp(lambda a, b: rms_fn(a, b), x_ref[...], g_ref[...])
        diff = y - t_ref[...]
        part = 0.5 * jnp.sum(jnp.mean(diff * diff, axis=-1, keepdims=True), axis=0, keepdims=True)
        dx, dg = vjp((diff * (1.0 / d),))
        dx_ref[...] = dx
        part = jnp.broadcast_to(part, (1, LANES))
        first = pl.program_id(0) == 0

        @pl.when(first)
        def _():
            loss_ref[...] = part
            dg_ref[...] = dg

        @pl.when(jnp.logical_not(first))
        def _():
            loss_ref[...] += part
            dg_ref[...] += dg

    row = pl.BlockSpec((tm, d), lambda i: (i, 0))
    vec = pl.BlockSpec((1, d), lambda i: (0, 0))
    return _pcall(
        body, name="loss_head", grid=(t // tm,), in_specs=[row, row, vec],
        out_specs=[pl.BlockSpec((1, LANES), lambda i: (0, 0)), row, vec],
        out_shape=[jax.ShapeDtypeStruct((1, LANES), F32), jax.ShapeDtypeStruct((t, d), F32),
                   jax.ShapeDtypeStruct((1, d), F32)],
        compiler_params=_cparams(("arbitrary",)),
    )(x3, tgt, g)


ADAM_LR, ADAM_B1, ADAM_B2, ADAM_EPS, ADAM_WD, ADAM_STEP = 0.001, 0.9, 0.999, 1e-08, 0.01, 10


def adamw(w, g, m, v, name):
    rows, width = w.shape
    tr = _tile(rows, 256)

    def body(w_ref, g_ref, m_ref, v_ref, d_ref, nm_ref, nv_ref):
        gv = g_ref[...]
        m_new = ADAM_B1 * m_ref[...] + (1.0 - ADAM_B1) * gv
        v_new = ADAM_B2 * v_ref[...] + (1.0 - ADAM_B2) * (gv * gv)
        m_hat = m_new / (1.0 - ADAM_B1 ** ADAM_STEP)
        v_hat = v_new / (1.0 - ADAM_B2 ** ADAM_STEP)
        d_ref[...] = -ADAM_LR * (m_hat / (jnp.sqrt(v_hat) + ADAM_EPS) + ADAM_WD * w_ref[...])
        nm_ref[...] = m_new
        nv_ref[...] = v_new

    spec = pl.BlockSpec((tr, width), lambda i: (i, 0))
    return _pcall(
        body, name=name, grid=(rows // tr,), in_specs=[spec] * 4, out_specs=[spec] * 3,
        out_shape=[jax.ShapeDtypeStruct((rows, width), F32)] * 3,
        compiler_params=_cparams(("parallel",)),
    )(w, g, m, v)


def sum_slots(q, name):
    n, rows, width = q.shape
    tr = _tile(rows, 512)

    def body(*refs):
        acc = refs[0][...].astype(F32)
        for r in refs[1:n]:
            acc = acc + r[...].astype(F32)
        refs[n][...] = acc

    specs = [pl.BlockSpec((None, tr, width), functools.partial(lambda s, i: (s, i, 0), s)) for s in range(n)]
    return _pcall(body, name=name, grid=(rows // tr,), in_specs=specs,
                  out_specs=pl.BlockSpec((tr, width), lambda i: (i, 0)),
                  out_shape=jax.ShapeDtypeStruct((rows, width), F32),
                  compiler_params=_cparams(("parallel",)))(*([q] * n))


MESH = pl.DeviceIdType.MESH
ANY_SPEC = pl.BlockSpec(memory_space=pl.ANY)


def _coords():
    return lax.axis_index("x"), lax.axis_index("y"), lax.axis_index("c")


def _other_chips(x, y):
    return [(1 - x, y), (x, 1 - y), (1 - x, 1 - y)]


def push(name, srcs, n_dst, plan_fn, start=None):
    n_arr = len(srcs)
    n_send = len(plan_fn(0, 0, 0))
    n_in = n_arr if start is None else 2 * n_arr

    def body(*refs):
        src_refs, dst_refs = refs[:n_arr], refs[n_in:n_in + n_arr]
        send_sems, recv_sems = refs[n_in + n_arr:]
        sends = plan_fn(*_coords())
        out = []
        for q, (src_ref, dst_ref) in enumerate(zip(src_refs, dst_refs)):
            out += [pltpu.make_async_remote_copy(src_ref.at[si], dst_ref.at[di], send_sems.at[q * n_send + k],
                                                 recv_sems.at[q * n_send + k], device_id=peer, device_id_type=MESH)
                    for k, (si, peer, di, _) in enumerate(sends)]
        for cp in out:
            cp.start()
        for q, (src_ref, dst_ref) in enumerate(zip(src_refs, dst_refs)):
            for k, (si, peer, _, ri) in enumerate(sends):
                pltpu.make_async_remote_copy(src_ref.at[si], dst_ref.at[ri], send_sems.at[q * n_send + k],
                                             recv_sems.at[q * n_send + k], device_id=peer,
                                             device_id_type=MESH).wait_recv()
        for cp in out:
            cp.wait_send()

    return _pcall(
        body, name=name, in_specs=[ANY_SPEC] * n_in, out_specs=[ANY_SPEC] * n_arr,
        out_shape=[jax.ShapeDtypeStruct((n_dst,) + s.shape[1:], s.dtype) for s in srcs],
        input_output_aliases={} if start is None else {n_arr + q: q for q in range(n_arr)},
        scratch_shapes=[pltpu.SemaphoreType.DMA((n_arr * n_send,)), pltpu.SemaphoreType.DMA((n_arr * n_send,))],
    )(*srcs, *([] if start is None else start))


def plan_sibling_quarters(x, y, c):
    return [(2 * s + (1 - c), (x, y, 1 - c), s, s) for s in range(4)]


def plan_chips_by_shard(x, y, c):
    me = 2 * x + y
    return [(2 * px + py, (px, py, c), me, 2 * px + py) for px, py in _other_chips(x, y)]


def plan_chips_half(x, y, c):
    me = 2 * x + y
    return [(c, (px, py, c), me, 2 * px + py) for px, py in _other_chips(x, y)]


def swap_halves(name, bufs):
    n_arr = len(bufs)

    def body(*refs):
        buf_refs = refs[n_arr:2 * n_arr]
        send_sems, recv_sems = refs[2 * n_arr:]
        x, y, c = _coords()
        out = [pltpu.make_async_remote_copy(b.at[c], b.at[c], send_sems.at[q], recv_sems.at[q],
                                            device_id=(x, y, 1 - c), device_id_type=MESH)
               for q, b in enumerate(buf_refs)]
        for cp in out:
            cp.start()
        for q, b in enumerate(buf_refs):
            pltpu.make_async_remote_copy(b.at[1 - c], b.at[1 - c], send_sems.at[q], recv_sems.at[q],
                                         device_id=(x, y, 1 - c), device_id_type=MESH).wait_recv()
        for cp in out:
            cp.wait_send()

    return _pcall(
        body, name=name, in_specs=[ANY_SPEC] * n_arr, out_specs=[ANY_SPEC] * n_arr,
        out_shape=[jax.ShapeDtypeStruct(b.shape, b.dtype) for b in bufs],
        input_output_aliases={q: q for q in range(n_arr)},
        scratch_shapes=[pltpu.SemaphoreType.DMA((n_arr,)), pltpu.SemaphoreType.DMA((n_arr,))],
    )(*bufs)


HBM_SPEC = pl.BlockSpec(memory_space=pltpu.HBM)
SEM_SPEC = pl.BlockSpec(memory_space=pltpu.SEMAPHORE)
DATAFLOW = pltpu.SideEffectType.DATAFLOW_SIDE_EFFECTING


def _in_hbm(a):
    return pltpu.with_memory_space_constraint(a, pltpu.HBM)


def push_start(name, srcs, lands, plan_fn, after):
    n_arr = len(srcs)
    n_send = len(plan_fn(0, 0, 0))

    def body(*refs):
        src_refs, land_refs = refs[:n_arr], refs[n_arr:2 * n_arr]
        send_sems, recv_sems = refs[2 * n_arr + 1:2 * n_arr + 3]
        token = refs[-1]
        sends = plan_fn(*_coords())
        for q, (src_ref, land_ref) in enumerate(zip(src_refs, land_refs)):
            for k, (si, peer, di, _) in enumerate(sends):
                pltpu.make_async_remote_copy(src_ref.at[si], land_ref.at[di], send_sems.at[q * n_send + k],
                                             recv_sems.at[q * n_send + k], device_id=peer,
                                             device_id_type=MESH).start()
        token[...] = jnp.zeros_like(token)

    sems = pltpu.SemaphoreType.DMA((n_arr * n_send,))
    out = _pcall(
        body, name=name,
        in_specs=[HBM_SPEC] * (2 * n_arr) + [ANY_SPEC],
        out_specs=[SEM_SPEC, SEM_SPEC] + [HBM_SPEC] * (2 * n_arr) + [pl.BlockSpec(memory_space=pltpu.VMEM)],
        out_shape=[sems, sems] + [pltpu.HBM(a.shape, a.dtype) for a in srcs + lands]
        + [jax.ShapeDtypeStruct((8, LANES), F32)],
        input_output_aliases={q: 2 + q for q in range(2 * n_arr)},
        compiler_params=pltpu.CompilerParams(has_side_effects=DATAFLOW),
    )(*[_in_hbm(a) for a in srcs + lands], after)
    return out[:-1], out[-1]


def push_wait(name, state, plan_fn, after):
    n_arr = (len(state) - 2) // 2
    n_send = len(plan_fn(0, 0, 0))
    send_sems, recv_sems = state[:2]
    srcs, lands = state[2:2 + n_arr], state[2 + n_arr:]

    def body(*refs):
        src_refs, land_refs = refs[:n_arr], refs[n_arr:2 * n_arr]
        send_ref, recv_ref = refs[2 * n_arr:2 * n_arr + 2]
        sends = plan_fn(*_coords())
        for q, (src_ref, land_ref) in enumerate(zip(src_refs, land_refs)):
            for k, (si, peer, _, ri) in enumerate(sends):
                cp = pltpu.make_async_remote_copy(src_ref.at[si], land_ref.at[ri], send_ref.at[q * n_send + k],
                                                  recv_ref.at[q * n_send + k], device_id=peer, device_id_type=MESH)
                cp.wait_send()
                cp.wait_recv()

    out = _pcall(
        body, name=name,
        in_specs=[HBM_SPEC] * (2 * n_arr) + [SEM_SPEC, SEM_SPEC, ANY_SPEC],
        out_specs=[HBM_SPEC] * (2 * n_arr),
        out_shape=[pltpu.HBM(a.shape, a.dtype) for a in list(srcs) + list(lands)],
        input_output_aliases={q: q for q in range(2 * n_arr)},
        compiler_params=pltpu.CompilerParams(has_side_effects=DATAFLOW),
    )(*srcs, *lands, send_sems, recv_sems, after)
    return list(out[n_arr:])


def plan_chips_gather(x, y, c):
    me = 2 * x + y
    return [(c, (px, py, c), 2 * me + c, 2 * (2 * px + py) + c) for px, py in _other_chips(x, y)]


def forward_halves(name, bufs):
    n_arr = len(bufs)

    def body(*refs):
        buf_refs = refs[n_arr:2 * n_arr]
        send_sems, recv_sems = refs[2 * n_arr:]
        x, y, c = _coords()
        chips = _other_chips(x, y)
        out = []
        for q, b in enumerate(buf_refs):
            for k, (px, py) in enumerate(chips):
                slot = b.at[2 * (2 * px + py) + c]
                out.append(pltpu.make_async_remote_copy(slot, slot, send_sems.at[3 * q + k], recv_sems.at[3 * q + k],
                                                        device_id=(x, y, 1 - c), device_id_type=MESH))
        for cp in out:
            cp.start()
        for q, b in enumerate(buf_refs):
            for k, (px, py) in enumerate(chips):
                slot = b.at[2 * (2 * px + py) + (1 - c)]
                pltpu.make_async_remote_copy(slot, slot, send_sems.at[3 * q + k], recv_sems.at[3 * q + k],
                                             device_id=(x, y, 1 - c), device_id_type=MESH).wait_recv()
        for cp in out:
            cp.wait_send()

    return _pcall(
        body, name=name, in_specs=[ANY_SPEC] * n_arr, out_specs=[ANY_SPEC] * n_arr,
        out_shape=[jax.ShapeDtypeStruct(b.shape, b.dtype) for b in bufs],
        input_output_aliases={q: q for q in range(n_arr)},
        scratch_shapes=[pltpu.SemaphoreType.DMA((3 * n_arr,)), pltpu.SemaphoreType.DMA((3 * n_arr,))],
    )(*bufs)


def gather_shards(mine):
    n_arr = len(mine)
    start = [jnp.broadcast_to(m[None], (4,) + m.shape).reshape((8,) + m.shape[1:]) for m in mine]

    def body(*refs):
        src_refs, dst_refs = refs[:n_arr], refs[2 * n_arr:3 * n_arr]
        ici_send, ici_recv, d2d_send, d2d_recv = refs[3 * n_arr:]
        x, y, c = _coords()
        me = 2 * x + y
        chips = _other_chips(x, y)
        out = []
        for q, (src_ref, dst_ref) in enumerate(zip(src_refs, dst_refs)):
            out += [pltpu.make_async_remote_copy(src_ref.at[c], dst_ref.at[2 * me + c], ici_send.at[3 * q + k],
                                                 ici_recv.at[3 * q + k], device_id=(px, py, c), device_id_type=MESH)
                    for k, (px, py) in enumerate(chips)]
        for cp in out:
            cp.start()
        passed = []
        for q, (src_ref, dst_ref) in enumerate(zip(src_refs, dst_refs)):
            for k, (px, py) in enumerate(chips):
                slot = dst_ref.at[2 * (2 * px + py) + c]
                pltpu.make_async_remote_copy(src_ref.at[c], slot, ici_send.at[3 * q + k], ici_recv.at[3 * q + k],
                                             device_id=(px, py, c), device_id_type=MESH).wait_recv()
                cp = pltpu.make_async_remote_copy(slot, slot, d2d_send.at[3 * q + k], d2d_recv.at[3 * q + k],
                                                  device_id=(x, y, 1 - c), device_id_type=MESH)
                cp.start()
                passed.append(cp)
        for q, dst_ref in enumerate(dst_refs):
            for k, (px, py) in enumerate(chips):
                slot = dst_ref.at[2 * (2 * px + py) + (1 - c)]
                pltpu.make_async_remote_copy(slot, slot, d2d_send.at[3 * q + k], d2d_recv.at[3 * q + k],
                                             device_id=(x, y, 1 - c), device_id_type=MESH).wait_recv()
        for cp in out + passed:
            cp.wait_send()

    got = _pcall(
        body, name="gather_shards", in_specs=[ANY_SPEC] * (2 * n_arr), out_specs=[ANY_SPEC] * n_arr,
        out_shape=[jax.ShapeDtypeStruct(s.shape, s.dtype) for s in start],
        input_output_aliases={n_arr + q: q for q in range(n_arr)},
        scratch_shapes=[pltpu.SemaphoreType.DMA((3 * n_arr,))] * 4,
    )(*mine, *start)
    return [g.reshape(4, 2 * g.shape[1], g.shape[2]) for g in got]


def add_kept(g8, other, name):
    _, rows, cols = other.shape
    tr = _tile(rows, 512)

    def body(c_ref, a_ref, b_ref, o_ref):
        o_ref[...] = (a_ref[...].astype(F32) + b_ref[...].astype(F32)).astype(BF16)

    spec = pl.BlockSpec((None, tr, cols), lambda s, i, c_ref: (s, i, 0))
    grid_spec = pltpu.PrefetchScalarGridSpec(
        num_scalar_prefetch=1, grid=(4, rows // tr),
        in_specs=[pl.BlockSpec((None, tr, cols), lambda s, i, c_ref: (2 * s + c_ref[0], i, 0)), spec],
        out_specs=spec)
    return _pcall(body, name=name, grid_spec=grid_spec, out_shape=jax.ShapeDtypeStruct((4, rows, cols), BF16),
                  compiler_params=_cparams(("parallel", "parallel")))(
        lax.axis_index("c").astype(jnp.int32).reshape(1), g8, other)


def reduce_scatter_big(grads, names):
    return reduce_scatter_end(reduce_scatter_begin(grads, names, "rs1", split=False)[0], names, "rs1", None)


def reduce_scatter_begin(grads, names, tag, split=True):
    g8 = [g.reshape(8, g.shape[1] // 2, g.shape[2]) for g in grads]
    from_sibling = push(tag + "_sibling", g8, 4, plan_sibling_quarters)
    chip_sum = [add_kept(a, b, tag + "_add_" + n) for a, b, n in zip(g8, from_sibling, names)]
    if split:
        return push_start(tag + "_start", chip_sum, chip_sum, plan_chips_by_shard, chip_sum[-1])
    return push(tag + "_chips", chip_sum, 4, plan_chips_by_shard, start=chip_sum), None


def reduce_scatter_end(state, names, tag, after):
    by_chip = state if after is None else push_wait(tag + "_wait", state, plan_chips_by_shard, after)
    total_half = [sum_slots(q, tag + "_sum_" + n) for q, n in zip(by_chip, names)]
    both = swap_halves(tag + "_halves", [jnp.broadcast_to(t[None], (2,) + t.shape) for t in total_half])
    return [b.reshape(2 * b.shape[1], b.shape[2]) for b in both]


def gather_begin(mine, after):
    lands = [jnp.broadcast_to(m[None], (4,) + m.shape).reshape((8,) + m.shape[1:]) for m in mine]
    return push_start("gather_start", mine, lands, plan_chips_gather, after)


def gather_end(state, after):
    lands = push_wait("gather_wait", state, plan_chips_gather, after)
    return [g.reshape(4, 2 * g.shape[1], g.shape[2]) for g in forward_halves("gather_forward", lands)]


def _pack(parts, dtype, rows):
    flat = jnp.concatenate([p.astype(dtype).reshape(-1) for p in parts])
    return jnp.pad(flat, (0, rows * C - flat.shape[0])).reshape(rows, C)


def _unpack(flat, shapes):
    flat = flat.reshape(-1)
    out, at = [], 0
    for s in shapes:
        n = int(np.prod(s))
        out.append(flat[at:at + n].reshape(s))
        at += n
    return out


def all_reduce_small(vec):
    rows, width = vec.shape
    (pair,) = swap_halves("ar_sibling", [jnp.broadcast_to(vec[None], (2, rows, width))])
    chip_sum = sum_slots(pair, "ar_sum_sibling").reshape(2, rows // 2, width)
    mine = lax.dynamic_index_in_dim(chip_sum, lax.axis_index("c"), axis=0, keepdims=True)
    (by_chip,) = push("ar_chips", [chip_sum], 4, plan_chips_half,
                      start=[jnp.broadcast_to(mine, (4, rows // 2, width))])
    total_half = sum_slots(by_chip, "ar_sum_chips")
    (both,) = swap_halves("ar_halves", [jnp.broadcast_to(total_half[None], (2, rows // 2, width))])
    return both.reshape(rows, width)


SMALL_ROWS = 576


def kernel(x, norm1_g, w_in, shift_mu, w0, w2, a0, a2, g2, k_k, k_a, r_k, gn_w, gn_b, sgu_ln_g, sgu_ln_b, sgu_w, sgu_b, sgu_out_g, w_out, norm2_g, w_gate, w_up, w_down, final_g, loss_target, m_norm1_g, m_w_in, m_shift_mu, m_w0, m_w2, m_a0, m_a2, m_g2, m_k_k, m_k_a, m_r_k, m_gn_w, m_gn_b, m_sgu_ln_g, m_sgu_ln_b, m_sgu_w, m_sgu_b, m_sgu_out_g, m_w_out, m_norm2_g, m_w_gate, m_w_up, m_w_down, m_final_g, v_norm1_g, v_w_in, v_shift_mu, v_w0, v_w2, v_a0, v_a2, v_g2, v_k_k, v_k_a, v_r_k, v_gn_w, v_gn_b, v_sgu_ln_g, v_sgu_ln_b, v_sgu_w, v_sgu_b, v_sgu_out_g, v_w_out, v_norm2_g, v_w_gate, v_w_up, v_w_down, v_final_g):
    args = dict(locals())
    names = ["norm1_g", "w_in", "shift_mu", "w0", "w2", "a0", "a2", "g2", "k_k", "k_a", "r_k", "gn_w", "gn_b",
             "sgu_ln_g", "sgu_ln_b", "sgu_w", "sgu_b", "sgu_out_g", "w_out", "norm2_g", "w_gate", "w_up", "w_down",
             "final_g"]
    big = ["w_in", "w_out", "w_gate", "w_up", "w_down"]
    xi, yi = lax.axis_index("x"), lax.axis_index("y")
    chip = 2 * xi + yi
    x2d, tgt = x[0], loss_target[0]
    t = x2d.shape[0]

    lora_names = ["w2", "a2", "g2"]
    first, later = ["w_in"] + lora_names, ["w_out", "w_gate", "w_up", "w_down"]

    def halves(n):
        m = args[n][0].astype(BF16)
        return m.reshape(2, m.shape[0] // 2, m.shape[1])

    got = gather_shards([halves(n) for n in first])
    later_state, later_token = gather_begin([halves(n) for n in later], got[0])
    norm1_g_fwd = norm1_g + later_token[0, 0]
    full = {n: blk.transpose(1, 0, 2).reshape(blk.shape[1], 4 * blk.shape[2]) for n, blk in zip(first, got)}
    w_in_f = full["w_in"]
    w_in_p = jnp.concatenate([w_in_f[:, :3 * C], w_in_f[:, 3360:], w_in_f[:, 3 * C:3360],
                              jnp.zeros((D, P_W - D_IN), BF16)], axis=1)
    mu_rkv = shift_mu[:, :3 * C]
    mu_lora = jnp.pad(shift_mu[:, 3 * C:], ((0, 0), (0, LORA_W - 288)))
    w2p = jnp.pad(full["w2"], ((0, 64), (0, 0)))
    a2p = jnp.pad(full["a2"], ((64, 0), (0, 0)))
    g2p = jnp.pad(full["g2"], ((0, 96), (0, 0)))
    ii = np.arange(C)
    jm = jnp.asarray((ii[:, None] // HEAD) == (ii[None, :] // HEAD), BF16)
    e_np = (np.arange(LANES)[:, None] == (ii[None, :] // HEAD))
    e_mat, e_t = jnp.asarray(e_np, BF16), jnp.asarray(e_np.T, BF16)
    b_t = jnp.pad(sgu_b[0].T, ((0, 0), (0, LANES - HEADS)))
    r_k_flat = r_k.reshape(1, C)
    tm_row = _tile(t, 256)
    tm_vjp = _tile(t, 128)

    (h1,) = row_fwd("norm1", rms_fn, [Tiled(x2d)], [norm1_g_fwd], [], [(D, BF16)], tm_row)
    p = mm(h1, w_in_p, name="proj_in")
    ps_rkv = shift_fwd(p, mu_rkv, 0, "shift_rkv")
    ps_lora = shift_fwd(p, mu_lora, P_LORA // LORA_W, "shift_lora")
    pre_par = [w0, w2p, a0, a2p, g2p, k_k, k_a]
    r_, w_, k2, v_, an, b_, g_ = row_fwd("pre", pre_fn, [Tiled(ps_rkv), Tiled(ps_lora)], pre_par, [jm],
                                         [(C, F32)] * 7, tm_vjp)
    v3 = to_cols(v_)
    y3, ckpt = wkv_fwd(r_, w_, k2, v3, an, b_)
    y_ = from_cols(y3)
    post_in = [Tiled(z) for z in (y_, r_, k2, v_, g_)]
    post_par = [gn_w, gn_b, r_k_flat]
    (y_rwkv,) = row_fwd("post", post_fn, post_in, post_par, [jm], [(C, BF16)], tm_vjp)
    sgu_in = [Tiled(p, C, 3), Tiled(p, C, 4)]
    sgu_par = [sgu_ln_g, sgu_ln_b, sgu_w[0], b_t, sgu_out_g]
    (y_sgu,) = row_fwd("sgu", sgu_fn, sgu_in, sgu_par, [e_mat, e_t], [(C, BF16)], tm_vjp)
    y_cat = jnp.concatenate([y_rwkv, y_sgu], axis=1)
    for n, blk in zip(later, gather_end(later_state, y_cat)):
        full[n] = blk if n in ("w_gate", "w_up") else blk.reshape(4 * blk.shape[1], blk.shape[2])
    x_mid = mm(y_cat, full["w_out"], res=x2d, name="proj_out")
    (h2,) = row_fwd("norm2", rms_fn, [Tiled(x_mid)], [norm2_g], [], [(D, BF16)], tm_row)
    gate = mm(h2, full["w_gate"], shards="b", name="ffn_gate")
    up = mm(h2, full["w_up"], shards="b", name="ffn_up")
    (act,) = row_fwd("swiglu", swiglu_fn, [Tiled(gate), Tiled(up)], [], [], [(D_FF, BF16)], tm_row)
    x_out = mm(act, full["w_down"], res=x_mid, name="ffn_down")
    loss_part, dx_out, d_final_g = loss_head(x_out, tgt, final_g.reshape(1, D))

    grads = {"final_g": d_final_g.reshape(D)}
    d_act = mm(dx_out, full["w_down"], tb=True, name="d_act")
    grads["w_down"] = mm(act, dx_out, ta=True, out_dtype=BF16, name="d_w_down")
    d_gate, d_up = row_bwd("swiglu_b", swiglu_fn, [Tiled(gate), Tiled(up)], [], [], [Tiled(d_act)], tm_row,
                           tiled_out_dtypes=[BF16, BF16])
    grads["w_gate"] = mm(h2, d_gate, shards="out", out_dtype=BF16, name="d_w_gate")
    grads["w_up"] = mm(h2, d_up, shards="out", out_dtype=BF16, name="d_w_up")
    d_h2 = mm(d_gate, full["w_gate"], shards="bt", name="d_h2_gate")
    d_h2 = mm(d_up, full["w_up"], shards="bt", res=d_h2, name="d_h2_up")
    dx_mid, grads["norm2_g"] = row_bwd("norm2_b", rms_res_fn, [Tiled(x_mid)], [norm2_g], [],
                                       [Tiled(d_h2), Tiled(dx_out)], tm_row)
    d_ycat = mm(dx_mid, full["w_out"], tb=True, name="d_ycat")
    grads["w_out"] = mm(y_cat, dx_mid, ta=True, out_dtype=BF16, name="d_w_out")
    rs_state, rs_token = reduce_scatter_begin([grads[n].reshape((4,) + args[n].shape[1:]) for n in later],
                                              later, "rs2")
    sgu_par_b = sgu_par[:-1] + [sgu_out_g + rs_token[0, 0]]
    d_pu, d_pv, grads["sgu_ln_g"], grads["sgu_ln_b"], d_sgu_w, d_b_t, grads["sgu_out_g"] = row_bwd(
        "sgu_b", sgu_fn, sgu_in, sgu_par_b, [e_mat, e_t], [Tiled(d_ycat, C, 1)], tm_vjp)
    grads["sgu_w"] = d_sgu_w[None]
    grads["sgu_b"] = d_b_t[:, :HEADS].T[None]
    d_y, d_r1, d_k1, d_v1, d_g, grads["gn_w"], grads["gn_b"], d_r_k = row_bwd(
        "post_b", post_fn, post_in, post_par, [jm], [Tiled(d_ycat, C, 0)], tm_vjp)
    grads["r_k"] = d_r_k.reshape(r_k.shape)
    d_r2, d_w, d_k2, d_a, d_b, d_v3 = wkv_bwd(r_, w_, k2, v3, an, b_, ckpt, to_cols(d_y))
    d_v2 = from_cols(d_v3)
    pre_cts = [(Tiled(d_r1), Tiled(d_r2)), Tiled(d_w), (Tiled(d_k1), Tiled(d_k2)), (Tiled(d_v1), Tiled(d_v2)),
               Tiled(d_a), Tiled(d_b), Tiled(d_g)]
    d_ps_rkv, d_ps_lora, grads["w0"], d_w2p, grads["a0"], d_a2p, d_g2p, grads["k_k"], grads["k_a"] = row_bwd(
        "pre_b", pre_fn, [Tiled(ps_rkv), Tiled(ps_lora)], pre_par, [jm], pre_cts, tm_vjp)
    grads["w2"], grads["a2"], grads["g2"] = d_w2p[None, :64], d_a2p[None, 64:], d_g2p[None, :160]
    d_p_rkv, d_mu_rkv = shift_bwd(d_ps_rkv, p, mu_rkv, 0, "shift_rkv_b")
    d_p_lora, d_mu_lora = shift_bwd(d_ps_lora, p, mu_lora, P_LORA // LORA_W, "shift_lora_b")
    grads["shift_mu"] = jnp.concatenate([d_mu_rkv, d_mu_lora[:, :288]], axis=1)
    d_p = jnp.concatenate([d_p_rkv, d_pu, d_pv, d_p_lora], axis=1)
    d_h1 = mm(d_p, w_in_p, tb=True, name="d_h1")
    d_w_in_p = mm(h1, d_p, ta=True, name="d_w_in")
    d_w_in = jnp.concatenate([d_w_in_p[:, :3 * C], d_w_in_p[:, P_LORA:P_LORA + 288], d_w_in_p[:, 3 * C:P_LORA]],
                             axis=1).astype(BF16)
    grads["w_in"] = d_w_in.reshape(D, 4, D_IN // 4).transpose(1, 0, 2)
    dx, grads["norm1_g"] = row_bwd("norm1_b", rms_res_fn, [Tiled(x2d)], [norm1_g], [],
                                   [Tiled(d_h1), Tiled(dx_mid)], tm_row)

    total = {n: g[None] for n, g in zip(later, reduce_scatter_end(rs_state, later, "rs2", dx))}
    total["w_in"] = reduce_scatter_big([grads["w_in"]], ["w_in"])[0][None]
    small = [n for n in names if n not in big]
    small_shapes = [(1, 64, C) if n in ("w2", "a2") else (1, 160, C) if n == "g2" else args[n].shape for n in small]
    vec = _pack([grads[n].reshape(1, -1) for n in small], F32, SMALL_ROWS)
    for n, g_sum in zip(small, _unpack(all_reduce_small(vec), small_shapes)):
        total[n] = lax.dynamic_slice_in_dim(g_sum, chip * 256, 256, axis=2) if n in lora_names else g_sum
    loss = lax.psum(loss_part[0, 0], ("x", "y", "c"))

    delta, new_m, new_v = {}, {}, {}
    for n in big:
        s = args[n].shape
        d_, m_, v__ = adamw(args[n][0], total[n][0], args["m_" + n][0], args["v_" + n][0], "adamw_" + n)
        delta[n], new_m[n], new_v[n] = d_.reshape(s), m_.reshape(s), v__.reshape(s)
    adam_rows = -(-sum(int(np.prod(args[n].shape)) for n in small) // (8 * C)) * 8
    flat = [_pack([src[n].reshape(1, -1) for n in small], F32, adam_rows)
            for src in (args, total, {n: args["m_" + n] for n in small}, {n: args["v_" + n] for n in small})]
    outs = adamw(*flat, "adamw_small")
    for res, o in zip((delta, new_m, new_v), outs):
        res.update(zip(small, _unpack(o, [args[n].shape for n in small])))
    return (loss, dx[None], *[total[n] for n in names], *[delta[n] for n in names],
            *[new_m[n] for n in names], *[new_v[n] for n in names])
```

```python
import functools

import numpy as np
import jax
import jax.numpy as jnp
from jax import lax
from jax.experimental import pallas as pl
from jax.experimental.pallas import tpu as pltpu

F32 = jnp.float32
BF16 = jnp.bfloat16

D = 2048
C = 1024
HEADS = 16
HEAD = 64
LANES = 128
P_W = 5632
P_LORA = 5120
LORA_W = 512
D_IN = 5408
D_FF = 5632
CHUNK = 128
RMS_EPS = 1e-6
LN_EPS = 1e-5
GN_EPS = 64e-5
L2_EPS = 1e-12
VMEM_LIMIT = 56 * 1024 * 1024


def _pcall(body, **kw):
    return pl.pallas_call(body, **kw)


def _cparams(sem):
    return pltpu.CompilerParams(dimension_semantics=sem, vmem_limit_bytes=VMEM_LIMIT)


def _tile(n, most):
    t = most
    while t > 8 and n % t:
        t //= 2
    assert n % t == 0, (n, most)
    return t


MM_TILE = (2048, 512, 2048)
MM_VMEM = 40 * 1024 * 1024


def _div_tile(n, most, quantum=LANES):
    for t in range(min(n, most) // quantum * quantum, 0, -quantum):
        if n % t == 0:
            return t
    raise ValueError((n, most, quantum))


def _mm_tiles(m, n, k, a_bytes, b_bytes, r_bytes, o_bytes):
    tm, tn, tk = _div_tile(m, MM_TILE[0]), _div_tile(n, MM_TILE[1]), _div_tile(k, MM_TILE[2])

    def need(tm, tk):
        return 2 * (tm * tk * a_bytes + tk * tn * b_bytes + tm * tn * (r_bytes + o_bytes)) + tm * tn * 4

    while need(tm, tk) > MM_VMEM:
        if tk >= tm and tk > 512:
            tk = _div_tile(k, tk - LANES)
        else:
            tm = _div_tile(m, tm - LANES)
    return tm, tn, tk


def mm(a, b, *, ta=False, tb=False, res=None, out_dtype=F32, name, shards=None):
    if shards is not None:
        return _mm_shards(a, b, res, out_dtype, name, shards)
    m, k = (a.shape[1], a.shape[0]) if ta else a.shape
    n = b.shape[0] if tb else b.shape[1]
    assert (b.shape[1] if tb else b.shape[0]) == k
    tm, tn, tk = _mm_tiles(m, n, k, a.dtype.itemsize, b.dtype.itemsize, 0 if res is None else res.dtype.itemsize,
                           jnp.dtype(out_dtype).itemsize)
    nk = k // tk
    dims = (((0 if ta else 1,), (1 if tb else 0,)), ((), ()))
    a_spec = pl.BlockSpec((tk, tm), lambda i, j, l: (l, i)) if ta else pl.BlockSpec((tm, tk), lambda i, j, l: (i, l))
    b_spec = pl.BlockSpec((tn, tk), lambda i, j, l: (j, l)) if tb else pl.BlockSpec((tk, tn), lambda i, j, l: (l, j))
    o_spec = pl.BlockSpec((tm, tn), lambda i, j, l: (i, j))
    return _mm_call(a, b, res, dims, (m // tm, n // tn, nk), a_spec, b_spec, o_spec, o_spec, (tm, tn),
                    jax.ShapeDtypeStruct((m, n), out_dtype), name)


def _mm_shards(a, b, res, out_dtype, name, shards):
    if shards == "b":
        (m, k), ns = a.shape, b.shape[2]
        tm, tk = _div_tile(m, 1024), _div_tile(k, 2048)
        grid, dims, acc = (m // tm, 4, k // tk), (((1,), (0,)), ((), ())), (tm, ns)
        a_spec = pl.BlockSpec((tm, tk), lambda i, s, l: (i, l))
        b_spec = pl.BlockSpec((None, tk, ns), lambda i, s, l: (s, l, 0))
        o_spec = pl.BlockSpec((tm, ns), lambda i, s, l: (i, s))
        out = jax.ShapeDtypeStruct((m, 4 * ns), out_dtype)
    elif shards == "bt":
        m, (_, n, ns) = a.shape[0], b.shape
        tm, tn = _div_tile(m, 2048), _div_tile(n, 512)
        grid, dims, acc = (m // tm, n // tn, 4), (((1,), (1,)), ((), ())), (tm, tn)
        a_spec = pl.BlockSpec((tm, ns), lambda i, j, s: (i, s))
        b_spec = pl.BlockSpec((None, tn, ns), lambda i, j, s: (s, j, 0))
        o_spec = pl.BlockSpec((tm, tn), lambda i, j, s: (i, j))
        out = jax.ShapeDtypeStruct((m, n), out_dtype)
    else:
        (t, m), ns = a.shape, b.shape[1] // 4
        tm, tk = _div_tile(m, 1024), _div_tile(t, 2048)
        grid, dims, acc = (m // tm, 4, t // tk), (((0,), (0,)), ((), ())), (tm, ns)
        a_spec = pl.BlockSpec((tk, tm), lambda i, s, l: (l, i))
        b_spec = pl.BlockSpec((tk, ns), lambda i, s, l: (l, s))
        o_spec = pl.BlockSpec((None, tm, ns), lambda i, s, l: (s, i, 0))
        out = jax.ShapeDtypeStruct((4, m, ns), out_dtype)
    return _mm_call(a, b, res, dims, grid, a_spec, b_spec, o_spec, o_spec, acc, out, name)


def _mm_call(a, b, res, dims, grid, a_spec, b_spec, r_spec, o_spec, acc_shape, out, name):
    nk = grid[2]
    out_dtype = out.dtype

    def body(*refs):
        a_ref, b_ref = refs[:2]
        r_ref = None if res is None else refs[2]
        o_ref = refs[2 + (res is not None)]

        def finish(acc):
            if r_ref is not None:
                acc = acc + r_ref[...].astype(F32)
            o_ref[...] = acc.astype(out_dtype)

        prod = lax.dot_general(a_ref[...].astype(BF16), b_ref[...].astype(BF16), dims, preferred_element_type=F32)
        if nk == 1:
            finish(prod)
            return
        acc_ref = refs[-1]
        kk = pl.program_id(2)

        @pl.when(kk == 0)
        def _():
            acc_ref[...] = prod

        @pl.when(kk > 0)
        def _():
            acc_ref[...] += prod

        @pl.when(kk == nk - 1)
        def _():
            finish(acc_ref[...])

    in_specs = [a_spec, b_spec] + ([r_spec] if res is not None else [])
    args = (a, b) + ((res,) if res is not None else ())
    return _pcall(
        body, name=name, grid=grid, in_specs=in_specs, out_specs=o_spec, out_shape=out,
        scratch_shapes=[] if nk == 1 else [pltpu.VMEM(acc_shape, F32)],
        compiler_params=_cparams(("parallel", "parallel", "arbitrary")),
    )(*args)


class Tiled:
    def __init__(self, arr, width=None, col=0):
        self.arr, self.width, self.col = arr, (arr.shape[1] if width is None else width), col

    def spec(self, tm):
        col = self.col
        return pl.BlockSpec((tm, self.width), lambda i: (i, col))


def _full_spec(p):
    nd = p.ndim
    return pl.BlockSpec(p.shape, lambda i: (0,) * nd)


def row_fwd(name, fn, tiled, params, consts, outs, tm):
    t = tiled[0].arr.shape[0]
    n_in = len(tiled) + len(params) + len(consts)

    def body(*refs):
        res = fn(*[r[...] for r in refs[:n_in]])
        for o_ref, r in zip(refs[n_in:], res):
            o_ref[...] = r.astype(o_ref.dtype)

    return _pcall(
        body, name=name, grid=(t // tm,),
        in_specs=[x.spec(tm) for x in tiled] + [_full_spec(p) for p in params + consts],
        out_specs=[pl.BlockSpec((tm, w), lambda i: (i, 0)) for w, _ in outs],
        out_shape=[jax.ShapeDtypeStruct((t, w), dt) for w, dt in outs],
        compiler_params=_cparams(("parallel",)),
    )(*[x.arr for x in tiled], *params, *consts)


def row_bwd(name, fn, tiled, params, consts, cts, tm, tiled_out_dtypes=None, bf16_copy_of=()):
    t = tiled[0].arr.shape[0]
    nt, npar, ncon = len(tiled), len(params), len(consts)
    cts = [c if isinstance(c, tuple) else (c,) for c in cts]
    flat_cts = [x for c in cts for x in c]
    tiled_out_dtypes = tiled_out_dtypes or [F32] * nt
    copies = [tiled[j] for j in bf16_copy_of]

    def body(*refs):
        n_in = nt + npar + ncon
        ins = [r[...].astype(F32) for r in refs[:nt + npar]]
        con = [r[...] for r in refs[nt + npar:n_in]]
        ct_refs = list(refs[n_in:n_in + len(flat_cts)])
        o = refs[n_in + len(flat_cts):]
        ct = []
        for c in cts:
            parts = [ct_refs.pop(0)[...].astype(F32) for _ in c]
            ct.append(functools.reduce(lambda p, q: p + q, parts))
        _, vjp = jax.vjp(lambda *a: fn(*a, *con), *ins)
        g = vjp(tuple(ct))
        for j in range(nt):
            o[j][...] = g[j].astype(o[j].dtype)
        for n, j in enumerate(bf16_copy_of):
            o[nt + npar + n][...] = g[j].astype(BF16)
        first = pl.program_id(0) == 0

        @pl.when(first)
        def _():
            for j in range(npar):
                o[nt + j][...] = g[nt + j]

        @pl.when(jnp.logical_not(first))
        def _():
            for j in range(npar):
                o[nt + j][...] += g[nt + j]

    return _pcall(
        body, name=name, grid=(t // tm,),
        in_specs=[x.spec(tm) for x in tiled] + [_full_spec(p) for p in params + consts]
        + [x.spec(tm) for x in flat_cts],
        out_specs=[pl.BlockSpec((tm, x.width), lambda i: (i, 0)) for x in tiled] + [_full_spec(p) for p in params]
        + [pl.BlockSpec((tm, x.width), lambda i: (i, 0)) for x in copies],
        out_shape=[jax.ShapeDtypeStruct((t, x.width), dt) for x, dt in zip(tiled, tiled_out_dtypes)]
        + [jax.ShapeDtypeStruct(p.shape, F32) for p in params]
        + [jax.ShapeDtypeStruct((t, x.width), BF16) for x in copies],
        compiler_params=_cparams(("arbitrary",)),
    )(*[x.arr for x in tiled], *params, *consts, *[x.arr for x in flat_cts])


def _split_dot(x, w):
    hi = x.astype(BF16)
    lo = (x - hi.astype(F32)).astype(BF16)
    return jnp.dot(hi, w, preferred_element_type=F32) + jnp.dot(lo, w, preferred_element_type=F32)


@jax.custom_vjp
def headsum(x, j):
    return _split_dot(x, j)


def _headsum_fwd(x, j):
    return _split_dot(x, j), j


def _headsum_bwd(j, ct):
    return _split_dot(ct, j), jnp.zeros_like(j)


headsum.defvjp(_headsum_fwd, _headsum_bwd)


def _bdot(x, w):
    return jnp.dot(x.astype(BF16), w.astype(BF16), preferred_element_type=F32)


def _sigmoid(x):
    return 1.0 / (1.0 + jnp.exp(-x))


def _softplus(x):
    return jnp.maximum(x, 0.0) + jnp.log(1.0 + jnp.exp(-jnp.abs(x)))


def rms_fn(x, g):
    return (x * lax.rsqrt(jnp.mean(x * x, axis=-1, keepdims=True) + RMS_EPS) * g,)


def rms_res_fn(x, g):
    return rms_fn(x, g)[0], x


def pre_fn(rkv, lora, w0, w2p, a0, a2p, g2p, k_k, k_a, jm):
    r, k, v = rkv[:, :C], rkv[:, C:2 * C], rkv[:, 2 * C:]
    zwa, zg = lora[:, :LANES], lora[:, LANES:LANES + 256]
    w_log = -_softplus(-(w0 + _bdot(jnp.tanh(zwa), w2p))) - 0.5
    decay = jnp.exp(-jnp.exp(w_log))
    a = _sigmoid(a0 + _bdot(zwa, a2p))
    g = _bdot(_sigmoid(zg), g2p)
    kk = k * k_k
    kk = kk / jnp.maximum(jnp.sqrt(headsum(kk * kk, jm)), L2_EPS)
    k2 = k * (1.0 + (a - 1.0) * k_a)
    return r, decay, k2, v, -kk, kk * a, g


def post_fn(y, r, k2, v, g, gn_w, gn_b, r_k, jm):
    mu = headsum(y, jm) * (1.0 / HEAD)
    yc = y - mu
    var = headsum(yc * yc, jm) * (1.0 / HEAD)
    yn = yc * lax.rsqrt(var + GN_EPS) * gn_w + gn_b
    bonus = headsum(r * k2 * r_k, jm) * v
    return ((yn + bonus) * g,)


def _gelu(x):
    return 0.5 * x * (1.0 + lax.erf(x * np.float32(1.0 / np.sqrt(2.0))))


@jax.custom_vjp
def expand_groups(b_t, e, e_t):
    return _split_dot(b_t, e)


def _expand_groups_fwd(b_t, e, e_t):
    return _split_dot(b_t, e), (e, e_t)


def _expand_groups_bwd(res, ct):
    e, e_t = res
    return _split_dot(ct, e_t), jnp.zeros_like(e), jnp.zeros_like(e_t)


expand_groups.defvjp(_expand_groups_fwd, _expand_groups_bwd)


def sgu_fn(pu, pv, ln_g, ln_b, w_s, b_t, out_g, e, e_t):
    rows = pu.shape[0]
    b_exp = expand_groups(b_t, e, e_t)
    u, v = _gelu(pu), _gelu(pv)
    mu = jnp.mean(v, axis=-1, keepdims=True)
    vc = v - mu
    var = jnp.mean(vc * vc, axis=-1, keepdims=True)
    v = vc * lax.rsqrt(var + LN_EPS) * ln_g + ln_b
    tri = lax.broadcasted_iota(jnp.int32, (CHUNK, CHUNK), 0) >= lax.broadcasted_iota(jnp.int32, (CHUNK, CHUNK), 1)
    left = lax.broadcasted_iota(jnp.int32, (CHUNK, LANES), 1) < HEAD
    chunks = []
    for c0 in range(0, rows, CHUNK):
        cols = []
        for gp in range(HEADS // 2):
            vp = v[c0:c0 + CHUNK, gp * LANES:(gp + 1) * LANES]
            wa = jnp.where(tri, w_s[2 * gp], 0.0)
            wb = jnp.where(tri, w_s[2 * gp + 1], 0.0)
            cols.append(jnp.where(left, _bdot(wa, vp), _bdot(wb, vp)))
        chunks.append(jnp.concatenate(cols, axis=1) + b_exp)
    s = jnp.concatenate(chunks, axis=0) if len(chunks) > 1 else chunks[0]
    y = u * s
    return (y * lax.rsqrt(jnp.mean(y * y, axis=-1, keepdims=True) + RMS_EPS) * out_g,)


def swiglu_fn(gate, up):
    return (gate * _sigmoid(gate) * up,)


SCAN_CHUNK = 32
NB = C // LANES
STEP_GROUP = 4


def to_cols(a):
    t = a.shape[0]
    return a.reshape(t // 8, LANES, HEAD).transpose(0, 2, 1)


def from_cols(a):
    return a.transpose(0, 2, 1).reshape(a.shape[0] * 8, C)


def _scan_consts():
    i = np.arange(LANES)
    spread = (i[:, None] % 2) == (i[None, :] // HEAD)
    pick = (i[:, None] // HEAD) == (i[None, :] % 2)
    j2 = (i[:, None] // HEAD) == (i[None, :] // HEAD)
    sel = (i[None, :] // 2) == (np.arange(8 * NB * HEAD)[:, None] // HEAD)
    return (jnp.asarray(spread, BF16), jnp.asarray(pick, BF16), jnp.asarray(np.concatenate([j2, j2], 0), BF16),
            jnp.asarray(sel, BF16))


def _stack(blocks):
    return jnp.concatenate(blocks, axis=0) if len(blocks) > 1 else blocks[0]


def _unstack(x, n):
    return [x[i * HEAD:(i + 1) * HEAD] for i in range(n)]


def _headsums(blocks, j2k, group):
    res = []
    for g0 in range(0, len(blocks), group):
        x = _stack(blocks[g0:g0 + group])
        hi = x.astype(BF16)
        lo = (x - hi.astype(F32)).astype(BF16)
        out = jnp.dot(jnp.concatenate([hi, lo], axis=1), j2k, preferred_element_type=F32)
        res += _unstack(out, len(blocks[g0:g0 + group]))
    return res


def _headsums_out(blocks, pick):
    return _unstack(jnp.dot(_stack(blocks).astype(BF16), pick, preferred_element_type=F32), len(blocks))


def _expand8(tile, sel, spread):
    lhs = jnp.tile(tile.astype(BF16), (8 * NB, 1)) * sel
    return _unstack(jnp.dot(lhs, spread, preferred_element_type=F32), 8 * NB)


def _collapse(tile, blocks, first, lane_pair):
    for n, blk in enumerate(blocks):
        tile = jnp.where(jnp.tile(lane_pair == first + n, (HEAD // 8, 1)), blk, tile)
    return tile


def _row(tile, j, cb):
    return jnp.broadcast_to(tile[j:j + 1, cb * LANES:(cb + 1) * LANES], (HEAD, LANES))


def _scan_step(s_ref, rows, vexp, j2k, j, hist=None):
    w_t, k_t, a_t, b_t, r_t = rows
    s = [s_ref[:, cb * LANES:(cb + 1) * LANES] for cb in range(NB)]
    sab = _headsums([s[cb] * _row(a_t, j, cb) for cb in range(NB)], j2k, STEP_GROUP)
    out = []
    for cb in range(NB):
        sl = slice(cb * LANES, (cb + 1) * LANES)
        s_new = s[cb] * _row(w_t, j, cb) + sab[cb] * _row(b_t, j, cb) + vexp[j * NB + cb] * _row(k_t, j, cb)
        s_ref[:, sl] = s_new
        if hist is None:
            out.append(s_new * _row(r_t, j, cb))
        else:
            s_hist, sab_hist, idx = hist
            s_hist[idx + 1, :, sl] = s_new
            sab_hist[idx, :, sl] = sab[cb]
    return out


def wkv_fwd(r, w, k, v3, a, b):
    t = r.shape[0]
    nc = t // SCAN_CHUNK
    n8 = SCAN_CHUNK // 8

    def body(r_ref, w_ref, k_ref, a_ref, b_ref, v3_ref, spread_ref, pick_ref, j2k_ref, sel_ref, y3_ref, ck_ref,
             s_ref):
        @pl.when(pl.program_id(0) == 0)
        def _():
            s_ref[...] = jnp.zeros_like(s_ref)

        ck_ref[0] = s_ref[...]
        spread, pick, j2k, sel = spread_ref[...], pick_ref[...], j2k_ref[...], sel_ref[...]
        lane_pair = lax.broadcasted_iota(jnp.int32, (8, LANES), 1) // 2

        def t8_body(t8, carry):
            row0 = pl.multiple_of(t8 * 8, 8)
            rows = [ref[pl.ds(row0, 8), :] for ref in (w_ref, k_ref, a_ref, b_ref, r_ref)]
            vexp = _expand8(v3_ref[t8], sel, spread)
            y3 = jnp.zeros((HEAD, LANES), F32)
            for j in range(8):
                y3 = _collapse(y3, _headsums_out(_scan_step(s_ref, rows, vexp, j2k, j), pick), j * NB, lane_pair)
            y3_ref[t8] = y3
            return carry

        lax.fori_loop(0, n8, t8_body, 0)

    row_spec = pl.BlockSpec((SCAN_CHUNK, C), lambda i: (i, 0))
    col_spec = pl.BlockSpec((n8, HEAD, LANES), lambda i: (i, 0, 0))
    consts = _scan_consts()
    return _pcall(
        body, name="wkv_fwd", grid=(nc,),
        in_specs=[row_spec] * 5 + [col_spec] + [pl.BlockSpec(c.shape, lambda i: (0, 0)) for c in consts],
        out_specs=[col_spec, pl.BlockSpec((1, HEAD, C), lambda i: (i, 0, 0))],
        out_shape=[jax.ShapeDtypeStruct((t // 8, HEAD, LANES), F32), jax.ShapeDtypeStruct((nc, HEAD, C), F32)],
        scratch_shapes=[pltpu.VMEM((HEAD, C), F32)],
        compiler_params=_cparams(("arbitrary",)),
    )(r, w, k, a, b, v3, *consts)


def wkv_bwd(r, w, k, v3, a, b, ckpt, dy3):
    t = r.shape[0]
    nc = t // SCAN_CHUNK
    n8 = SCAN_CHUNK // 8

    def body(r_ref, w_ref, k_ref, a_ref, b_ref, v3_ref, dy3_ref, ck_ref, spread_ref, pick_ref, j2k_ref, sel_ref,
             dr_ref, dw_ref, dk_ref, da_ref, db_ref, dv3_ref, s_ref, g_ref, s_hist, sab_hist):
        @pl.when(pl.program_id(0) == 0)
        def _():
            g_ref[...] = jnp.zeros_like(g_ref)

        spread, pick, j2k, sel = spread_ref[...], pick_ref[...], j2k_ref[...], sel_ref[...]
        lane_pair = lax.broadcasted_iota(jnp.int32, (8, LANES), 1) // 2
        sub = lax.broadcasted_iota(jnp.int32, (8, LANES), 0)
        s_ref[...] = ck_ref[0]
        s_hist[0] = ck_ref[0]

        def redo(t8, carry):
            row0 = pl.multiple_of(t8 * 8, 8)
            rows = [ref[pl.ds(row0, 8), :] for ref in (w_ref, k_ref, a_ref, b_ref, r_ref)]
            vexp = _expand8(v3_ref[t8], sel, spread)
            for j in range(8):
                _scan_step(s_ref, rows, vexp, j2k, j, hist=(s_hist, sab_hist, t8 * 8 + j))
            return carry

        lax.fori_loop(0, n8, redo, 0)

        def back(q, carry):
            t8 = n8 - 1 - q
            row0 = pl.multiple_of(t8 * 8, 8)
            w_t, k_t, a_t, b_t, r_t = [ref[pl.ds(row0, 8), :] for ref in (w_ref, k_ref, a_ref, b_ref, r_ref)]
            vexp = _expand8(v3_ref[t8], sel, spread)
            dyexp = _expand8(dy3_ref[t8], sel, spread)
            dv3 = jnp.zeros((HEAD, LANES), F32)
            tiles = [[jnp.zeros((8, LANES), F32) for _ in range(NB)] for _ in range(5)]
            for j in range(7, -1, -1):
                idx = t8 * 8 + j
                g = [g_ref[:, cb * LANES:(cb + 1) * LANES] + dyexp[j * NB + cb] * _row(r_t, j, cb)
                     for cb in range(NB)]
                dsab = _headsums([g[cb] * _row(b_t, j, cb) for cb in range(NB)], j2k, STEP_GROUP)
                dv3 = _collapse(dv3, _headsums_out([g[cb] * _row(k_t, j, cb) for cb in range(NB)], pick), j * NB,
                                lane_pair)
                for cb in range(NB):
                    sl = slice(cb * LANES, (cb + 1) * LANES)
                    s_new = s_hist[idx + 1, :, sl]
                    s_old = s_hist[idx, :, sl]
                    sab = sab_hist[idx, :, sl]
                    sums = (s_new * dyexp[j * NB + cb], g[cb] * s_old, g[cb] * vexp[j * NB + cb],
                            s_old * dsab[cb], g[cb] * sab)
                    for n, prod in enumerate(sums):
                        rowsum = jnp.broadcast_to(jnp.sum(prod, axis=0, keepdims=True), (8, LANES))
                        tiles[n][cb] = jnp.where(sub == j, rowsum, tiles[n][cb])
                    g_ref[:, sl] = g[cb] * _row(w_t, j, cb) + dsab[cb] * _row(a_t, j, cb)
            dv3_ref[t8] = dv3
            for n, ref in enumerate((dr_ref, dw_ref, dk_ref, da_ref, db_ref)):
                for cb in range(NB):
                    ref[pl.ds(row0, 8), cb * LANES:(cb + 1) * LANES] = tiles[n][cb]
            return carry

        lax.fori_loop(0, n8, back, 0)

    row_spec = pl.BlockSpec((SCAN_CHUNK, C), lambda i: (nc - 1 - i, 0))
    col_spec = pl.BlockSpec((n8, HEAD, LANES), lambda i: (nc - 1 - i, 0, 0))
    consts = _scan_consts()
    return _pcall(
        body, name="wkv_bwd", grid=(nc,),
        in_specs=[row_spec] * 5 + [col_spec, col_spec, pl.BlockSpec((1, HEAD, C), lambda i: (nc - 1 - i, 0, 0))]
        + [pl.BlockSpec(c.shape, lambda i: (0, 0)) for c in consts],
        out_specs=[row_spec] * 5 + [col_spec],
        out_shape=[jax.ShapeDtypeStruct((t, C), F32)] * 5 + [jax.ShapeDtypeStruct((t // 8, HEAD, LANES), F32)],
        scratch_shapes=[pltpu.VMEM((HEAD, C), F32), pltpu.VMEM((HEAD, C), F32),
                        pltpu.VMEM((SCAN_CHUNK + 1, HEAD, C), F32), pltpu.VMEM((SCAN_CHUNK, HEAD, C), F32)],
        compiler_params=_cparams(("arbitrary",)),
    )(r, w, k, a, b, v3, dy3, ckpt, *consts)


def _prev_rows(cur, before, first_tile):
    last = jnp.where(first_tile, 0.0, before[7:8, :])
    row = lax.broadcasted_iota(jnp.int32, cur.shape, 0)
    return jnp.where(row == 0, last, pltpu.roll(cur, 1, 0))


def shift_fwd(p, mu, col, name):
    t, width = p.shape[0], mu.shape[1]
    tm = _tile(t, 256)

    def body(p_ref, before_ref, mu_ref, o_ref):
        cur = p_ref[...]
        prev = _prev_rows(cur, before_ref[...], pl.program_id(0) == 0)
        o_ref[...] = cur + (prev - cur) * mu_ref[...]

    return _pcall(
        body, name=name, grid=(t // tm,),
        in_specs=[pl.BlockSpec((tm, width), lambda i: (i, col)),
                  pl.BlockSpec((8, width), lambda i: (jnp.maximum(i * (tm // 8) - 1, 0), col)),
                  pl.BlockSpec((1, width), lambda i: (0, 0))],
        out_specs=pl.BlockSpec((tm, width), lambda i: (i, 0)),
        out_shape=jax.ShapeDtypeStruct((t, width), F32),
        compiler_params=_cparams(("parallel",)),
    )(p, p, mu)


def shift_bwd(dps, p, mu, col, name):
    t, width = dps.shape
    tm = _tile(t, 256)
    nt = t // tm

    def body(d_ref, after_ref, p_ref, before_ref, mu_ref, dp_ref, dmu_ref):
        i = pl.program_id(0)
        d, cur, mu_v = d_ref[...], p_ref[...], mu_ref[...]
        row = lax.broadcasted_iota(jnp.int32, d.shape, 0)
        first_after = jnp.where(i == nt - 1, 0.0, after_ref[0:1, :])
        nxt = jnp.where(row == tm - 1, first_after, pltpu.roll(d, tm - 1, 0))
        dp_ref[...] = (d * (1.0 - mu_v) + nxt * mu_v).astype(BF16)
        prev = _prev_rows(cur, before_ref[...], i == 0)
        part = jnp.sum(d * (prev - cur), axis=0, keepdims=True)

        @pl.when(i == 0)
        def _():
            dmu_ref[...] = part

        @pl.when(i > 0)
        def _():
            dmu_ref[...] += part

    return _pcall(
        body, name=name, grid=(nt,),
        in_specs=[pl.BlockSpec((tm, width), lambda i: (i, 0)),
                  pl.BlockSpec((8, width), lambda i: (jnp.minimum((i + 1) * (tm // 8), t // 8 - 1), 0)),
                  pl.BlockSpec((tm, width), lambda i: (i, col)),
                  pl.BlockSpec((8, width), lambda i: (jnp.maximum(i * (tm // 8) - 1, 0), col)),
                  pl.BlockSpec((1, width), lambda i: (0, 0))],
        out_specs=[pl.BlockSpec((tm, width), lambda i: (i, 0)), pl.BlockSpec((1, width), lambda i: (0, 0))],
        out_shape=[jax.ShapeDtypeStruct((t, width), BF16), jax.ShapeDtypeStruct((1, width), F32)],
        compiler_params=_cparams(("arbitrary",)),
    )(dps, dps, p, p, mu)


def loss_head(x3, tgt, g):
    t, d = x3.shape
    tm = _tile(t, 256)

    def body(x_ref, t_ref, g_ref, loss_ref, dx_ref, dg_ref, dxb_ref):
        (y,), vjp = jax.vjp(lambda a, b: rms_fn(a, b), x_ref[...], g_ref[...])
        diff = y - t_ref[...]
        part = 0.5 * jnp.sum(jnp.mean(diff * diff, axis=-1, keepdims=True), axis=0, keepdims=True)
        dx, dg = vjp((diff * (1.0 / d),))
        dx_ref[...] = dx
        dxb_ref[...] = dx.astype(BF16)
        part = jnp.broadcast_to(part, (1, LANES))
        first = pl.program_id(0) == 0

        @pl.when(first)
        def _():
            loss_ref[...] = part
            dg_ref[...] = dg

        @pl.when(jnp.logical_not(first))
        def _():
            loss_ref[...] += part
            dg_ref[...] += dg

    row = pl.BlockSpec((tm, d), lambda i: (i, 0))
    vec = pl.BlockSpec((1, d), lambda i: (0, 0))
    return _pcall(
        body, name="loss_head", grid=(t // tm,), in_specs=[row, row, vec],
        out_specs=[pl.BlockSpec((1, LANES), lambda i: (0, 0)), row, vec, row],
        out_shape=[jax.ShapeDtypeStruct((1, LANES), F32), jax.ShapeDtypeStruct((t, d), F32),
                   jax.ShapeDtypeStruct((1, d), F32), jax.ShapeDtypeStruct((t, d), BF16)],
        compiler_params=_cparams(("arbitrary",)),
    )(x3, tgt, g)


ADAM_LR, ADAM_B1, ADAM_B2, ADAM_EPS, ADAM_WD, ADAM_STEP = 0.001, 0.9, 0.999, 1e-08, 0.01, 10


def adamw(w, g, m, v, name):
    rows, width = w.shape
    tr = _tile(rows, 256)

    def body(w_ref, g_ref, m_ref, v_ref, d_ref, nm_ref, nv_ref):
        gv = g_ref[...]
        m_new = ADAM_B1 * m_ref[...] + (1.0 - ADAM_B1) * gv
        v_new = ADAM_B2 * v_ref[...] + (1.0 - ADAM_B2) * (gv * gv)
        m_hat = m_new / (1.0 - ADAM_B1 ** ADAM_STEP)
        v_hat = v_new / (1.0 - ADAM_B2 ** ADAM_STEP)
        d_ref[...] = -ADAM_LR * (m_hat / (jnp.sqrt(v_hat) + ADAM_EPS) + ADAM_WD * w_ref[...])
        nm_ref[...] = m_new
        nv_ref[...] = v_new

    spec = pl.BlockSpec((tr, width), lambda i: (i, 0))
    return _pcall(
        body, name=name, grid=(rows // tr,), in_specs=[spec] * 4, out_specs=[spec] * 3,
        out_shape=[jax.ShapeDtypeStruct((rows, width), F32)] * 3,
        compiler_params=_cparams(("parallel",)),
    )(w, g, m, v)


def sum_slots(q, name):
    n, rows, width = q.shape
    tr = _tile(rows, 512)

    def body(*refs):
        acc = refs[0][...].astype(F32)
        for r in refs[1:n]:
            acc = acc + r[...].astype(F32)
        refs[n][...] = acc

    specs = [pl.BlockSpec((None, tr, width), functools.partial(lambda s, i: (s, i, 0), s)) for s in range(n)]
    return _pcall(body, name=name, grid=(rows // tr,), in_specs=specs,
                  out_specs=pl.BlockSpec((tr, width), lambda i: (i, 0)),
                  out_shape=jax.ShapeDtypeStruct((rows, width), F32),
                  compiler_params=_cparams(("parallel",)))(*([q] * n))


MESH = pl.DeviceIdType.MESH
ANY_SPEC = pl.BlockSpec(memory_space=pl.ANY)


def _coords():
    return lax.axis_index("x"), lax.axis_index("y"), lax.axis_index("c")


def _other_chips(x, y):
    return [(1 - x, y), (x, 1 - y), (1 - x, 1 - y)]


def push(name, srcs, n_dst, plan_fn):
    n_arr = len(srcs)
    n_send = len(plan_fn(0, 0, 0))

    def body(*refs):
        src_refs, dst_refs = refs[:n_arr], refs[n_arr:2 * n_arr]
        send_sems, recv_sems = refs[2 * n_arr:]
        sends = plan_fn(*_coords())
        out = []
        for q, (src_ref, dst_ref) in enumerate(zip(src_refs, dst_refs)):
            out += [pltpu.make_async_remote_copy(src_ref.at[si], dst_ref.at[di], send_sems.at[q * n_send + k],
                                                 recv_sems.at[q * n_send + k], device_id=peer, device_id_type=MESH)
                    for k, (si, peer, di, _) in enumerate(sends)]
        for cp in out:
            cp.start()
        for q, (src_ref, dst_ref) in enumerate(zip(src_refs, dst_refs)):
            for k, (si, peer, _, ri) in enumerate(sends):
                pltpu.make_async_remote_copy(src_ref.at[si], dst_ref.at[ri], send_sems.at[q * n_send + k],
                                             recv_sems.at[q * n_send + k], device_id=peer,
                                             device_id_type=MESH).wait_recv()
        for cp in out:
            cp.wait_send()

    return _pcall(
        body, name=name, in_specs=[ANY_SPEC] * n_arr, out_specs=[ANY_SPEC] * n_arr,
        out_shape=[jax.ShapeDtypeStruct((n_dst,) + s.shape[1:], s.dtype) for s in srcs],
        scratch_shapes=[pltpu.SemaphoreType.DMA((n_arr * n_send,)), pltpu.SemaphoreType.DMA((n_arr * n_send,))],
    )(*srcs)


def plan_sibling_quarters(x, y, c):
    return [(2 * s + (1 - c), (x, y, 1 - c), s, s) for s in range(4)]


def plan_chips_by_shard(x, y, c):
    me = 2 * x + y
    return [(2 * px + py, (px, py, c), me, 2 * px + py) for px, py in _other_chips(x, y)]


def plan_chips_half(x, y, c):
    me = 2 * x + y
    return [(c, (px, py, c), me, 2 * px + py) for px, py in _other_chips(x, y)]


def swap_halves(name, bufs):
    n_arr = len(bufs)

    def body(*refs):
        buf_refs = refs[n_arr:2 * n_arr]
        send_sems, recv_sems = refs[2 * n_arr:]
        x, y, c = _coords()
        out = [pltpu.make_async_remote_copy(b.at[c], b.at[c], send_sems.at[q], recv_sems.at[q],
                                            device_id=(x, y, 1 - c), device_id_type=MESH)
               for q, b in enumerate(buf_refs)]
        for cp in out:
            cp.start()
        for q, b in enumerate(buf_refs):
            pltpu.make_async_remote_copy(b.at[1 - c], b.at[1 - c], send_sems.at[q], recv_sems.at[q],
                                         device_id=(x, y, 1 - c), device_id_type=MESH).wait_recv()
        for cp in out:
            cp.wait_send()

    return _pcall(
        body, name=name, in_specs=[ANY_SPEC] * n_arr, out_specs=[ANY_SPEC] * n_arr,
        out_shape=[jax.ShapeDtypeStruct(b.shape, b.dtype) for b in bufs],
        input_output_aliases={q: q for q in range(n_arr)},
        scratch_shapes=[pltpu.SemaphoreType.DMA((n_arr,)), pltpu.SemaphoreType.DMA((n_arr,))],
    )(*bufs)


HBM_SPEC = pl.BlockSpec(memory_space=pltpu.HBM)
SEM_SPEC = pl.BlockSpec(memory_space=pltpu.SEMAPHORE)
DATAFLOW = pltpu.SideEffectType.DATAFLOW_SIDE_EFFECTING


def _in_hbm(a):
    return pltpu.with_memory_space_constraint(a, pltpu.HBM)


def push_start(name, srcs, lands, plan_fn, after):
    n_arr = len(srcs)
    n_send = len(plan_fn(0, 0, 0))

    def body(*refs):
        src_refs, land_refs = refs[:n_arr], refs[n_arr:2 * n_arr]
        send_sems, recv_sems = refs[2 * n_arr + 1:2 * n_arr + 3]
        token = refs[-1]
        sends = plan_fn(*_coords())
        for q, (src_ref, land_ref) in enumerate(zip(src_refs, land_refs)):
            for k, (si, peer, di, _) in enumerate(sends):
                pltpu.make_async_remote_copy(src_ref.at[si], land_ref.at[di], send_sems.at[q * n_send + k],
                                             recv_sems.at[q * n_send + k], device_id=peer,
                                             device_id_type=MESH).start()
        token[...] = jnp.zeros_like(token)

    sems = pltpu.SemaphoreType.DMA((n_arr * n_send,))
    out = _pcall(
        body, name=name,
        in_specs=[HBM_SPEC] * (2 * n_arr) + [ANY_SPEC],
        out_specs=[SEM_SPEC, SEM_SPEC] + [HBM_SPEC] * (2 * n_arr) + [pl.BlockSpec(memory_space=pltpu.VMEM)],
        out_shape=[sems, sems] + [pltpu.HBM(a.shape, a.dtype) for a in srcs + lands]
        + [jax.ShapeDtypeStruct((8, LANES), F32)],
        input_output_aliases={q: 2 + q for q in range(2 * n_arr)},
        compiler_params=pltpu.CompilerParams(has_side_effects=DATAFLOW),
    )(*[_in_hbm(a) for a in srcs + lands], after)
    return out[:-1], out[-1]


def push_wait(name, state, plan_fn, after):
    n_arr = (len(state) - 2) // 2
    n_send = len(plan_fn(0, 0, 0))
    send_sems, recv_sems = state[:2]
    srcs, lands = state[2:2 + n_arr], state[2 + n_arr:]

    def body(*refs):
        src_refs, land_refs = refs[:n_arr], refs[n_arr:2 * n_arr]
        send_ref, recv_ref = refs[2 * n_arr:2 * n_arr + 2]
        sends = plan_fn(*_coords())
        for q, (src_ref, land_ref) in enumerate(zip(src_refs, land_refs)):
            for k, (si, peer, _, ri) in enumerate(sends):
                cp = pltpu.make_async_remote_copy(src_ref.at[si], land_ref.at[ri], send_ref.at[q * n_send + k],
                                                  recv_ref.at[q * n_send + k], device_id=peer, device_id_type=MESH)
                cp.wait_send()
                cp.wait_recv()

    out = _pcall(
        body, name=name,
        in_specs=[HBM_SPEC] * (2 * n_arr) + [SEM_SPEC, SEM_SPEC, ANY_SPEC],
        out_specs=[HBM_SPEC] * (2 * n_arr),
        out_shape=[pltpu.HBM(a.shape, a.dtype) for a in list(srcs) + list(lands)],
        input_output_aliases={q: q for q in range(2 * n_arr)},
        compiler_params=pltpu.CompilerParams(has_side_effects=DATAFLOW),
    )(*srcs, *lands, send_sems, recv_sems, after)
    return list(out[n_arr:])


def plan_chips_gather(x, y, c):
    me = 2 * x + y
    return [(c, (px, py, c), 2 * me + c, 2 * (2 * px + py) + c) for px, py in _other_chips(x, y)]


def forward_halves(name, bufs):
    n_arr = len(bufs)

    def body(*refs):
        buf_refs = refs[n_arr:2 * n_arr]
        send_sems, recv_sems = refs[2 * n_arr:]
        x, y, c = _coords()
        chips = _other_chips(x, y)
        out = []
        for q, b in enumerate(buf_refs):
            for k, (px, py) in enumerate(chips):
                slot = b.at[2 * (2 * px + py) + c]
                out.append(pltpu.make_async_remote_copy(slot, slot, send_sems.at[3 * q + k], recv_sems.at[3 * q + k],
                                                        device_id=(x, y, 1 - c), device_id_type=MESH))
        for cp in out:
            cp.start()
        for q, b in enumerate(buf_refs):
            for k, (px, py) in enumerate(chips):
                slot = b.at[2 * (2 * px + py) + (1 - c)]
                pltpu.make_async_remote_copy(slot, slot, send_sems.at[3 * q + k], recv_sems.at[3 * q + k],
                                             device_id=(x, y, 1 - c), device_id_type=MESH).wait_recv()
        for cp in out:
            cp.wait_send()

    return _pcall(
        body, name=name, in_specs=[ANY_SPEC] * n_arr, out_specs=[ANY_SPEC] * n_arr,
        out_shape=[jax.ShapeDtypeStruct(b.shape, b.dtype) for b in bufs],
        input_output_aliases={q: q for q in range(n_arr)},
        scratch_shapes=[pltpu.SemaphoreType.DMA((3 * n_arr,)), pltpu.SemaphoreType.DMA((3 * n_arr,))],
    )(*bufs)


def gather_shards(mine):
    n_arr = len(mine)
    start = [jnp.broadcast_to(m[None], (4,) + m.shape).reshape((8,) + m.shape[1:]) for m in mine]

    def body(*refs):
        src_refs, dst_refs = refs[:n_arr], refs[2 * n_arr:3 * n_arr]
        ici_send, ici_recv, d2d_send, d2d_recv = refs[3 * n_arr:]
        x, y, c = _coords()
        me = 2 * x + y
        chips = _other_chips(x, y)
        out = []
        for q, (src_ref, dst_ref) in enumerate(zip(src_refs, dst_refs)):
            out += [pltpu.make_async_remote_copy(src_ref.at[c], dst_ref.at[2 * me + c], ici_send.at[3 * q + k],
                                                 ici_recv.at[3 * q + k], device_id=(px, py, c), device_id_type=MESH)
                    for k, (px, py) in enumerate(chips)]
        for cp in out:
            cp.start()
        passed = []
        for q, (src_ref, dst_ref) in enumerate(zip(src_refs, dst_refs)):
            for k, (px, py) in enumerate(chips):
                slot = dst_ref.at[2 * (2 * px + py) + c]
                pltpu.make_async_remote_copy(src_ref.at[c], slot, ici_send.at[3 * q + k], ici_recv.at[3 * q + k],
                                             device_id=(px, py, c), device_id_type=MESH).wait_recv()
                cp = pltpu.make_async_remote_copy(slot, slot, d2d_send.at[3 * q + k], d2d_recv.at[3 * q + k],
                                                  device_id=(x, y, 1 - c), device_id_type=MESH)
                cp.start()
                passed.append(cp)
        for q, dst_ref in enumerate(dst_refs):
            for k, (px, py) in enumerate(chips):
                slot = dst_ref.at[2 * (2 * px + py) + (1 - c)]
                pltpu.make_async_remote_copy(slot, slot, d2d_send.at[3 * q + k], d2d_recv.at[3 * q + k],
                                             device_id=(x, y, 1 - c), device_id_type=MESH).wait_recv()
        for cp in out + passed:
            cp.wait_send()

    got = _pcall(
        body, name="gather_shards", in_specs=[ANY_SPEC] * (2 * n_arr), out_specs=[ANY_SPEC] * n_arr,
        out_shape=[jax.ShapeDtypeStruct(s.shape, s.dtype) for s in start],
        input_output_aliases={n_arr + q: q for q in range(n_arr)},
        scratch_shapes=[pltpu.SemaphoreType.DMA((3 * n_arr,))] * 4,
    )(*mine, *start)
    return [g.reshape(4, 2 * g.shape[1], g.shape[2]) for g in got]


def add_kept(g8, other, name):
    _, rows, cols = other.shape
    tr = _tile(rows, 512)

    def body(c_ref, a_ref, b_ref, o_ref):
        o_ref[...] = (a_ref[...].astype(F32) + b_ref[...].astype(F32)).astype(BF16)

    spec = pl.BlockSpec((None, tr, cols), lambda s, i, c_ref: (s, i, 0))
    grid_spec = pltpu.PrefetchScalarGridSpec(
        num_scalar_prefetch=1, grid=(4, rows // tr),
        in_specs=[pl.BlockSpec((None, tr, cols), lambda s, i, c_ref: (2 * s + c_ref[0], i, 0)), spec],
        out_specs=spec)
    return _pcall(body, name=name, grid_spec=grid_spec, out_shape=jax.ShapeDtypeStruct((4, rows, cols), BF16),
                  compiler_params=_cparams(("parallel", "parallel")))(
        lax.axis_index("c").astype(jnp.int32).reshape(1), g8, other)


def reduce_scatter_begin(grads, names, tag):
    g8 = [g.reshape(8, g.shape[1] // 2, g.shape[2]) for g in grads]
    from_sibling = push(tag + "_sibling", g8, 4, plan_sibling_quarters)
    chip_sum = [add_kept(a, b, tag + "_add_" + n) for a, b, n in zip(g8, from_sibling, names)]
    return push_start(tag + "_start", chip_sum, chip_sum, plan_chips_by_shard, chip_sum[-1])


def reduce_scatter_end(state, names, tag, after):
    by_chip = push_wait(tag + "_wait", state, plan_chips_by_shard, after)
    total_half = [sum_slots(q, tag + "_sum_" + n) for q, n in zip(by_chip, names)]
    both = swap_halves(tag + "_halves", [jnp.broadcast_to(t[None], (2,) + t.shape) for t in total_half])
    return [b.reshape(2 * b.shape[1], b.shape[2]) for b in both]


def gather_begin(mine, after):
    lands = [jnp.broadcast_to(m[None], (4,) + m.shape).reshape((8,) + m.shape[1:]) for m in mine]
    return push_start("gather_start", mine, lands, plan_chips_gather, after)


def gather_end(state, after):
    lands = push_wait("gather_wait", state, plan_chips_gather, after)
    return [g.reshape(4, 2 * g.shape[1], g.shape[2]) for g in forward_halves("gather_forward", lands)]


def _pack(parts, dtype, rows):
    flat = jnp.concatenate([p.astype(dtype).reshape(-1) for p in parts])
    return jnp.pad(flat, (0, rows * C - flat.shape[0])).reshape(rows, C)


def _unpack(flat, shapes):
    flat = flat.reshape(-1)
    out, at = [], 0
    for s in shapes:
        n = int(np.prod(s))
        out.append(flat[at:at + n].reshape(s))
        at += n
    return out


def all_reduce_begin(vec):
    rows, width = vec.shape
    (pair,) = swap_halves("ar_sibling", [jnp.broadcast_to(vec[None], (2, rows, width))])
    chip_sum = sum_slots(pair, "ar_sum_sibling").reshape(2, rows // 2, width)
    mine = lax.dynamic_index_in_dim(chip_sum, lax.axis_index("c"), axis=0, keepdims=True)
    return push_start("ar_start", [chip_sum], [jnp.broadcast_to(mine, (4, rows // 2, width))], plan_chips_half,
                      chip_sum)


def all_reduce_end(state, after):
    (by_chip,) = push_wait("ar_wait", state, plan_chips_half, after)
    total_half = sum_slots(by_chip, "ar_sum_chips")
    (both,) = swap_halves("ar_halves", [jnp.broadcast_to(total_half[None], (2,) + total_half.shape)])
    return both.reshape(2 * total_half.shape[0], total_half.shape[1])


SMALL_ROWS = 576


def kernel(x, norm1_g, w_in, shift_mu, w0, w2, a0, a2, g2, k_k, k_a, r_k, gn_w, gn_b, sgu_ln_g, sgu_ln_b, sgu_w, sgu_b, sgu_out_g, w_out, norm2_g, w_gate, w_up, w_down, final_g, loss_target, m_norm1_g, m_w_in, m_shift_mu, m_w0, m_w2, m_a0, m_a2, m_g2, m_k_k, m_k_a, m_r_k, m_gn_w, m_gn_b, m_sgu_ln_g, m_sgu_ln_b, m_sgu_w, m_sgu_b, m_sgu_out_g, m_w_out, m_norm2_g, m_w_gate, m_w_up, m_w_down, m_final_g, v_norm1_g, v_w_in, v_shift_mu, v_w0, v_w2, v_a0, v_a2, v_g2, v_k_k, v_k_a, v_r_k, v_gn_w, v_gn_b, v_sgu_ln_g, v_sgu_ln_b, v_sgu_w, v_sgu_b, v_sgu_out_g, v_w_out, v_norm2_g, v_w_gate, v_w_up, v_w_down, v_final_g):
    args = dict(locals())
    names = ["norm1_g", "w_in", "shift_mu", "w0", "w2", "a0", "a2", "g2", "k_k", "k_a", "r_k", "gn_w", "gn_b",
             "sgu_ln_g", "sgu_ln_b", "sgu_w", "sgu_b", "sgu_out_g", "w_out", "norm2_g", "w_gate", "w_up", "w_down",
             "final_g"]
    big = ["w_in", "w_out", "w_gate", "w_up", "w_down"]
    xi, yi = lax.axis_index("x"), lax.axis_index("y")
    chip = 2 * xi + yi
    x2d, tgt = x[0], loss_target[0]
    t = x2d.shape[0]

    lora_names = ["w2", "a2", "g2"]
    first, later = ["w_in"] + lora_names, ["w_out", "w_gate", "w_up", "w_down"]

    def halves(n):
        m = args[n][0].astype(BF16)
        return m.reshape(2, m.shape[0] // 2, m.shape[1])

    got = gather_shards([halves(n) for n in first])
    later_state, later_token = gather_begin([halves(n) for n in later], got[0])
    norm1_g_fwd = norm1_g + later_token[0, 0]
    full = {n: blk.transpose(1, 0, 2).reshape(blk.shape[1], 4 * blk.shape[2]) for n, blk in zip(first, got)}
    w_in_f = full["w_in"]
    w_in_p = jnp.concatenate([w_in_f[:, :3 * C], w_in_f[:, 3360:], w_in_f[:, 3 * C:3360],
                              jnp.zeros((D, P_W - D_IN), BF16)], axis=1)
    mu_rkv = shift_mu[:, :3 * C]
    mu_lora = jnp.pad(shift_mu[:, 3 * C:], ((0, 0), (0, LORA_W - 288)))
    w2p = jnp.pad(full["w2"], ((0, 64), (0, 0)))
    a2p = jnp.pad(full["a2"], ((64, 0), (0, 0)))
    g2p = jnp.pad(full["g2"], ((0, 96), (0, 0)))
    ii = np.arange(C)
    jm = jnp.asarray((ii[:, None] // HEAD) == (ii[None, :] // HEAD), BF16)
    e_np = (np.arange(LANES)[:, None] == (ii[None, :] // HEAD))
    e_mat, e_t = jnp.asarray(e_np, BF16), jnp.asarray(e_np.T, BF16)
    b_t = jnp.pad(sgu_b[0].T, ((0, 0), (0, LANES - HEADS)))
    r_k_flat = r_k.reshape(1, C)
    tm_row = _tile(t, 256)
    tm_vjp = _tile(t, 128)

    (h1,) = row_fwd("norm1", rms_fn, [Tiled(x2d)], [norm1_g_fwd], [], [(D, BF16)], tm_row)
    p = mm(h1, w_in_p, name="proj_in")
    ps_rkv = shift_fwd(p, mu_rkv, 0, "shift_rkv")
    ps_lora = shift_fwd(p, mu_lora, P_LORA // LORA_W, "shift_lora")
    pre_par = [w0, w2p, a0, a2p, g2p, k_k, k_a]
    r_, w_, k2, v_, an, b_, g_ = row_fwd("pre", pre_fn, [Tiled(ps_rkv), Tiled(ps_lora)], pre_par, [jm],
                                         [(C, F32)] * 7, tm_vjp)
    v3 = to_cols(v_)
    y3, ckpt = wkv_fwd(r_, w_, k2, v3, an, b_)
    y_ = from_cols(y3)
    post_in = [Tiled(z) for z in (y_, r_, k2, v_, g_)]
    post_par = [gn_w, gn_b, r_k_flat]
    (y_rwkv,) = row_fwd("post", post_fn, post_in, post_par, [jm], [(C, BF16)], tm_vjp)
    sgu_in = [Tiled(p, C, 3), Tiled(p, C, 4)]
    sgu_par = [sgu_ln_g, sgu_ln_b, sgu_w[0], b_t, sgu_out_g]
    (y_sgu,) = row_fwd("sgu", sgu_fn, sgu_in, sgu_par, [e_mat, e_t], [(C, BF16)], tm_vjp)
    y_cat = jnp.concatenate([y_rwkv, y_sgu], axis=1)
    for n, blk in zip(later, gather_end(later_state, y_cat)):
        full[n] = blk if n in ("w_gate", "w_up") else blk.reshape(4 * blk.shape[1], blk.shape[2])
    x_mid = mm(y_cat, full["w_out"], res=x2d, name="proj_out")
    (h2,) = row_fwd("norm2", rms_fn, [Tiled(x_mid)], [norm2_g], [], [(D, BF16)], tm_row)
    gate = mm(h2, full["w_gate"], shards="b", name="ffn_gate")
    up = mm(h2, full["w_up"], shards="b", name="ffn_up")
    (act,) = row_fwd("swiglu", swiglu_fn, [Tiled(gate), Tiled(up)], [], [], [(D_FF, BF16)], tm_row)
    x_out = mm(act, full["w_down"], res=x_mid, name="ffn_down")
    loss_part, dx_out, d_final_g, dx_out_b = loss_head(x_out, tgt, final_g.reshape(1, D))

    grads = {"final_g": d_final_g.reshape(D)}
    d_act = mm(dx_out_b, full["w_down"], tb=True, name="d_act")
    grads["w_down"] = mm(act, dx_out_b, ta=True, out_dtype=BF16, name="d_w_down")
    d_gate, d_up = row_bwd("swiglu_b", swiglu_fn, [Tiled(gate), Tiled(up)], [], [], [Tiled(d_act)], tm_row,
                           tiled_out_dtypes=[BF16, BF16])
    grads["w_gate"] = mm(h2, d_gate, shards="out", out_dtype=BF16, name="d_w_gate")
    grads["w_up"] = mm(h2, d_up, shards="out", out_dtype=BF16, name="d_w_up")
    d_h2 = mm(d_gate, full["w_gate"], shards="bt", name="d_h2_gate")
    d_h2 = mm(d_up, full["w_up"], shards="bt", res=d_h2, name="d_h2_up")
    dx_mid, grads["norm2_g"], dx_mid_b = row_bwd("norm2_b", rms_res_fn, [Tiled(x_mid)], [norm2_g], [],
                                                 [Tiled(d_h2), Tiled(dx_out)], tm_row, bf16_copy_of=(0,))
    d_ycat = mm(dx_mid_b, full["w_out"], tb=True, name="d_ycat")
    grads["w_out"] = mm(y_cat, dx_mid_b, ta=True, out_dtype=BF16, name="d_w_out")
    rs_state, rs_token = reduce_scatter_begin([grads[n].reshape((4,) + args[n].shape[1:]) for n in later],
                                              later, "rs2")
    sgu_par_b = sgu_par[:-1] + [sgu_out_g + rs_token[0, 0]]
    d_pu, d_pv, grads["sgu_ln_g"], grads["sgu_ln_b"], d_sgu_w, d_b_t, grads["sgu_out_g"] = row_bwd(
        "sgu_b", sgu_fn, sgu_in, sgu_par_b, [e_mat, e_t], [Tiled(d_ycat, C, 1)], tm_vjp,
        tiled_out_dtypes=[BF16, BF16])
    grads["sgu_w"] = d_sgu_w[None]
    grads["sgu_b"] = d_b_t[:, :HEADS].T[None]
    d_y, d_r1, d_k1, d_v1, d_g, grads["gn_w"], grads["gn_b"], d_r_k = row_bwd(
        "post_b", post_fn, post_in, post_par, [jm], [Tiled(d_ycat, C, 0)], tm_vjp)
    grads["r_k"] = d_r_k.reshape(r_k.shape)
    d_r2, d_w, d_k2, d_a, d_b, d_v3 = wkv_bwd(r_, w_, k2, v3, an, b_, ckpt, to_cols(d_y))
    d_v2 = from_cols(d_v3)
    pre_cts = [(Tiled(d_r1), Tiled(d_r2)), Tiled(d_w), (Tiled(d_k1), Tiled(d_k2)), (Tiled(d_v1), Tiled(d_v2)),
               Tiled(d_a), Tiled(d_b), Tiled(d_g)]
    d_ps_rkv, d_ps_lora, grads["w0"], d_w2p, grads["a0"], d_a2p, d_g2p, grads["k_k"], grads["k_a"] = row_bwd(
        "pre_b", pre_fn, [Tiled(ps_rkv), Tiled(ps_lora)], pre_par, [jm], pre_cts, tm_vjp)
    grads["w2"], grads["a2"], grads["g2"] = d_w2p[None, :64], d_a2p[None, 64:], d_g2p[None, :160]
    d_p_rkv, d_mu_rkv = shift_bwd(d_ps_rkv, p, mu_rkv, 0, "shift_rkv_b")
    d_p_lora, d_mu_lora = shift_bwd(d_ps_lora, p, mu_lora, P_LORA // LORA_W, "shift_lora_b")
    grads["shift_mu"] = jnp.concatenate([d_mu_rkv, d_mu_lora[:, :288]], axis=1)
    d_p = jnp.concatenate([d_p_rkv, d_pu, d_pv, d_p_lora], axis=1)
    d_h1 = mm(d_p, w_in_p, tb=True, name="d_h1")
    d_w_in_p = mm(h1, d_p, ta=True, name="d_w_in")
    d_w_in = jnp.concatenate([d_w_in_p[:, :3 * C], d_w_in_p[:, P_LORA:P_LORA + 288], d_w_in_p[:, 3 * C:P_LORA]],
                             axis=1).astype(BF16)
    grads["w_in"] = d_w_in.reshape(D, 4, D_IN // 4).transpose(1, 0, 2)
    dx, grads["norm1_g"] = row_bwd("norm1_b", rms_res_fn, [Tiled(x2d)], [norm1_g], [],
                                   [Tiled(d_h1), Tiled(dx_mid)], tm_row)

    small = [n for n in names if n not in big]
    small_shapes = [(1, 64, C) if n in ("w2", "a2") else (1, 160, C) if n == "g2" else args[n].shape for n in small]
    vec = _pack([grads[n].reshape(1, -1) for n in small], F32, SMALL_ROWS)
    w_in_state, w_in_token = reduce_scatter_begin([grads["w_in"]], ["w_in"], "rs1")
    small_state, small_token = all_reduce_begin(vec)
    total = {n: g[None] for n, g in zip(later, reduce_scatter_end(rs_state, later, "rs2", w_in_token + small_token))}
    loss = lax.psum(loss_part[0, 0], ("x", "y", "c"))
    delta, new_m, new_v = {}, {}, {}

    def adam_big(n):
        s = args[n].shape
        d_, m_, v__ = adamw(args[n][0], total[n][0], args["m_" + n][0], args["v_" + n][0], "adamw_" + n)
        delta[n], new_m[n], new_v[n] = d_.reshape(s), m_.reshape(s), v__.reshape(s)

    for n in later:
        adam_big(n)
    total["w_in"] = reduce_scatter_end(w_in_state, ["w_in"], "rs1", new_v["w_down"])[0][None]
    for n, g_sum in zip(small, _unpack(all_reduce_end(small_state, new_v["w_down"]), small_shapes)):
        total[n] = lax.dynamic_slice_in_dim(g_sum, chip * 256, 256, axis=2) if n in lora_names else g_sum
    adam_big("w_in")
    adam_rows = -(-sum(int(np.prod(args[n].shape)) for n in small) // (8 * C)) * 8
    flat = [_pack([src[n].reshape(1, -1) for n in small], F32, adam_rows)
            for src in (args, total, {n: args["m_" + n] for n in small}, {n: args["v_" + n] for n in small})]
    outs = adamw(*flat, "adamw_small")
    for res, o in zip((delta, new_m, new_v), outs):
        res.update(zip(small, _unpack(o, [args[n].shape for n in small])))
    return (loss, dx[None], *[total[n] for n in names], *[delta[n] for n in names],
            *[new_m[n] for n in names], *[new_v[n] for n in names])
```

```python
import functools

import numpy as np
import jax
import jax.numpy as jnp
from jax import lax
from jax.experimental import pallas as pl
from jax.experimental.pallas import tpu as pltpu

F32 = jnp.float32
BF16 = jnp.bfloat16

D = 2048
C = 1024
HEADS = 16
HEAD = 64
LANES = 128
P_W = 5632
P_LORA = 5120
LORA_W = 512
D_IN = 5408
D_FF = 5632
CHUNK = 128
RMS_EPS = 1e-6
LN_EPS = 1e-5
GN_EPS = 64e-5
L2_EPS = 1e-12
VMEM_LIMIT = 56 * 1024 * 1024


def _pcall(body, **kw):
    return pl.pallas_call(body, **kw)


def _cparams(sem):
    return pltpu.CompilerParams(dimension_semantics=sem, vmem_limit_bytes=VMEM_LIMIT)


def _tile(n, most):
    t = most
    while t > 8 and n % t:
        t //= 2
    assert n % t == 0, (n, most)
    return t


MM_TILE = (2048, 512, 2048)
MM_VMEM = 40 * 1024 * 1024


def _div_tile(n, most, quantum=LANES):
    for t in range(min(n, most) // quantum * quantum, 0, -quantum):
        if n % t == 0:
            return t
    raise ValueError((n, most, quantum))


def _mm_tiles(m, n, k, a_bytes, b_bytes, r_bytes, o_bytes):
    tm, tn, tk = _div_tile(m, MM_TILE[0]), _div_tile(n, MM_TILE[1]), _div_tile(k, MM_TILE[2])

    def need(tm, tk):
        return 2 * (tm * tk * a_bytes + tk * tn * b_bytes + tm * tn * (r_bytes + o_bytes)) + tm * tn * 4

    while need(tm, tk) > MM_VMEM:
        if tk >= tm and tk > 512:
            tk = _div_tile(k, tk - LANES)
        else:
            tm = _div_tile(m, tm - LANES)
    return tm, tn, tk


def mm(a, b, *, ta=False, tb=False, res=None, out_dtype=F32, name, shards=None):
    if shards is not None:
        return _mm_shards(a, b, res, out_dtype, name, shards)
    m, k = (a.shape[1], a.shape[0]) if ta else a.shape
    n = b.shape[0] if tb else b.shape[1]
    assert (b.shape[1] if tb else b.shape[0]) == k
    tm, tn, tk = _mm_tiles(m, n, k, a.dtype.itemsize, b.dtype.itemsize, 0 if res is None else res.dtype.itemsize,
                           jnp.dtype(out_dtype).itemsize)
    nk = k // tk
    dims = (((0 if ta else 1,), (1 if tb else 0,)), ((), ()))
    a_spec = pl.BlockSpec((tk, tm), lambda i, j, l: (l, i)) if ta else pl.BlockSpec((tm, tk), lambda i, j, l: (i, l))
    b_spec = pl.BlockSpec((tn, tk), lambda i, j, l: (j, l)) if tb else pl.BlockSpec((tk, tn), lambda i, j, l: (l, j))
    o_spec = pl.BlockSpec((tm, tn), lambda i, j, l: (i, j))
    return _mm_call(a, b, res, dims, (m // tm, n // tn, nk), a_spec, b_spec, o_spec, o_spec, (tm, tn),
                    jax.ShapeDtypeStruct((m, n), out_dtype), name)


def _mm_shards(a, b, res, out_dtype, name, shards):
    if shards == "b":
        (m, k), ns = a.shape, b.shape[2]
        tm, tk = _div_tile(m, 1024), _div_tile(k, 2048)
        grid, dims, acc = (m // tm, 4, k // tk), (((1,), (0,)), ((), ())), (tm, ns)
        a_spec = pl.BlockSpec((tm, tk), lambda i, s, l: (i, l))
        b_spec = pl.BlockSpec((None, tk, ns), lambda i, s, l: (s, l, 0))
        o_spec = pl.BlockSpec((tm, ns), lambda i, s, l: (i, s))
        out = jax.ShapeDtypeStruct((m, 4 * ns), out_dtype)
    elif shards == "bt":
        m, (_, n, ns) = a.shape[0], b.shape
        tm, tn = _div_tile(m, 2048), _div_tile(n, 512)
        grid, dims, acc = (m // tm, n // tn, 4), (((1,), (1,)), ((), ())), (tm, tn)
        a_spec = pl.BlockSpec((tm, ns), lambda i, j, s: (i, s))
        b_spec = pl.BlockSpec((None, tn, ns), lambda i, j, s: (s, j, 0))
        o_spec = pl.BlockSpec((tm, tn), lambda i, j, s: (i, j))
        out = jax.ShapeDtypeStruct((m, n), out_dtype)
    else:
        (t, m), ns = a.shape, b.shape[1] // 4
        tm, tk = _div_tile(m, 1024), _div_tile(t, 2048)
        grid, dims, acc = (m // tm, 4, t // tk), (((0,), (0,)), ((), ())), (tm, ns)
        a_spec = pl.BlockSpec((tk, tm), lambda i, s, l: (l, i))
        b_spec = pl.BlockSpec((tk, ns), lambda i, s, l: (l, s))
        o_spec = pl.BlockSpec((None, tm, ns), lambda i, s, l: (s, i, 0))
        out = jax.ShapeDtypeStruct((4, m, ns), out_dtype)
    return _mm_call(a, b, res, dims, grid, a_spec, b_spec, o_spec, o_spec, acc, out, name)


def _mm_call(a, b, res, dims, grid, a_spec, b_spec, r_spec, o_spec, acc_shape, out, name):
    nk = grid[2]
    out_dtype = out.dtype

    def body(*refs):
        a_ref, b_ref = refs[:2]
        r_ref = None if res is None else refs[2]
        o_ref = refs[2 + (res is not None)]

        def finish(acc):
            if r_ref is not None:
                acc = acc + r_ref[...].astype(F32)
            o_ref[...] = acc.astype(out_dtype)

        prod = lax.dot_general(a_ref[...].astype(BF16), b_ref[...].astype(BF16), dims, preferred_element_type=F32)
        if nk == 1:
            finish(prod)
            return
        acc_ref = refs[-1]
        kk = pl.program_id(2)

        @pl.when(kk == 0)
        def _():
            acc_ref[...] = prod

        @pl.when(kk > 0)
        def _():
            acc_ref[...] += prod

        @pl.when(kk == nk - 1)
        def _():
            finish(acc_ref[...])

    in_specs = [a_spec, b_spec] + ([r_spec] if res is not None else [])
    args = (a, b) + ((res,) if res is not None else ())
    return _pcall(
        body, name=name, grid=grid, in_specs=in_specs, out_specs=o_spec, out_shape=out,
        scratch_shapes=[] if nk == 1 else [pltpu.VMEM(acc_shape, F32)],
        compiler_params=_cparams(("parallel", "parallel", "arbitrary")),
    )(*args)


class Tiled:
    def __init__(self, arr, width=None, col=0):
        self.arr, self.width, self.col = arr, (arr.shape[1] if width is None else width), col

    def spec(self, tm):
        col = self.col
        return pl.BlockSpec((tm, self.width), lambda i: (i, col))


def _full_spec(p):
    nd = p.ndim
    return pl.BlockSpec(p.shape, lambda i: (0,) * nd)


def row_fwd(name, fn, tiled, params, consts, outs, tm):
    t = tiled[0].arr.shape[0]
    n_in = len(tiled) + len(params) + len(consts)

    def body(*refs):
        res = fn(*[r[...] for r in refs[:n_in]])
        for o_ref, r in zip(refs[n_in:], res):
            o_ref[...] = r.astype(o_ref.dtype)

    return _pcall(
        body, name=name, grid=(t // tm,),
        in_specs=[x.spec(tm) for x in tiled] + [_full_spec(p) for p in params + consts],
        out_specs=[pl.BlockSpec((tm, w), lambda i: (i, 0)) for w, _ in outs],
        out_shape=[jax.ShapeDtypeStruct((t, w), dt) for w, dt in outs],
        compiler_params=_cparams(("parallel",)),
    )(*[x.arr for x in tiled], *params, *consts)


def row_bwd(name, fn, tiled, params, consts, cts, tm, tiled_out_dtypes=None, bf16_copy_of=()):
    t = tiled[0].arr.shape[0]
    nt, npar, ncon = len(tiled), len(params), len(consts)
    cts = [c if isinstance(c, tuple) else (c,) for c in cts]
    flat_cts = [x for c in cts for x in c]
    tiled_out_dtypes = tiled_out_dtypes or [F32] * nt
    copies = [tiled[j] for j in bf16_copy_of]

    def body(*refs):
        n_in = nt + npar + ncon
        ins = [r[...].astype(F32) for r in refs[:nt + npar]]
        con = [r[...] for r in refs[nt + npar:n_in]]
        ct_refs = list(refs[n_in:n_in + len(flat_cts)])
        o = refs[n_in + len(flat_cts):]
        ct = []
        for c in cts:
            parts = [ct_refs.pop(0)[...].astype(F32) for _ in c]
            ct.append(functools.reduce(lambda p, q: p + q, parts))
        _, vjp = jax.vjp(lambda *a: fn(*a, *con), *ins)
        g = vjp(tuple(ct))
        for j in range(nt):
            o[j][...] = g[j].astype(o[j].dtype)
        for n, j in enumerate(bf16_copy_of):
            o[nt + npar + n][...] = g[j].astype(BF16)
        first = pl.program_id(0) == 0

        @pl.when(first)
        def _():
            for j in range(npar):
                o[nt + j][...] = g[nt + j]

        @pl.when(jnp.logical_not(first))
        def _():
            for j in range(npar):
                o[nt + j][...] += g[nt + j]

    return _pcall(
        body, name=name, grid=(t // tm,),
        in_specs=[x.spec(tm) for x in tiled] + [_full_spec(p) for p in params + consts]
        + [x.spec(tm) for x in flat_cts],
        out_specs=[pl.BlockSpec((tm, x.width), lambda i: (i, 0)) for x in tiled] + [_full_spec(p) for p in params]
        + [pl.BlockSpec((tm, x.width), lambda i: (i, 0)) for x in copies],
        out_shape=[jax.ShapeDtypeStruct((t, x.width), dt) for x, dt in zip(tiled, tiled_out_dtypes)]
        + [jax.ShapeDtypeStruct(p.shape, F32) for p in params]
        + [jax.ShapeDtypeStruct((t, x.width), BF16) for x in copies],
        compiler_params=_cparams(("arbitrary",)),
    )(*[x.arr for x in tiled], *params, *consts, *[x.arr for x in flat_cts])


def _split_dot(x, w):
    hi = x.astype(BF16)
    lo = (x - hi.astype(F32)).astype(BF16)
    return jnp.dot(hi, w, preferred_element_type=F32) + jnp.dot(lo, w, preferred_element_type=F32)


@jax.custom_vjp
def headsum(x, j):
    return _split_dot(x, j)


def _headsum_fwd(x, j):
    return _split_dot(x, j), j


def _headsum_bwd(j, ct):
    return _split_dot(ct, j), jnp.zeros_like(j)


headsum.defvjp(_headsum_fwd, _headsum_bwd)


def _bdot(x, w):
    return jnp.dot(x.astype(BF16), w.astype(BF16), preferred_element_type=F32)


def _sigmoid(x):
    return 1.0 / (1.0 + jnp.exp(-x))


def _softplus(x):
    return jnp.maximum(x, 0.0) + jnp.log(1.0 + jnp.exp(-jnp.abs(x)))


def rms_fn(x, g):
    return (x * lax.rsqrt(jnp.mean(x * x, axis=-1, keepdims=True) + RMS_EPS) * g,)


def rms_res_fn(x, g):
    return rms_fn(x, g)[0], x


def pre_fn(rkv, lora, w0, w2p, a0, a2p, g2p, k_k, k_a, jm):
    r, k, v = rkv[:, :C], rkv[:, C:2 * C], rkv[:, 2 * C:]
    zwa, zg = lora[:, :LANES], lora[:, LANES:LANES + 256]
    w_log = -_softplus(-(w0 + _bdot(jnp.tanh(zwa), w2p))) - 0.5
    decay = jnp.exp(-jnp.exp(w_log))
    a = _sigmoid(a0 + _bdot(zwa, a2p))
    g = _bdot(_sigmoid(zg), g2p)
    kk = k * k_k
    kk = kk / jnp.maximum(jnp.sqrt(headsum(kk * kk, jm)), L2_EPS)
    k2 = k * (1.0 + (a - 1.0) * k_a)
    return r, decay, k2, v, -kk, kk * a, g


def post_fn(y, r, k2, v, g, gn_w, gn_b, r_k, jm):
    mu = headsum(y, jm) * (1.0 / HEAD)
    yc = y - mu
    var = headsum(yc * yc, jm) * (1.0 / HEAD)
    yn = yc * lax.rsqrt(var + GN_EPS) * gn_w + gn_b
    bonus = headsum(r * k2 * r_k, jm) * v
    return ((yn + bonus) * g,)


def _gelu(x):
    return 0.5 * x * (1.0 + lax.erf(x * np.float32(1.0 / np.sqrt(2.0))))


@jax.custom_vjp
def expand_groups(b_t, e, e_t):
    return _split_dot(b_t, e)


def _expand_groups_fwd(b_t, e, e_t):
    return _split_dot(b_t, e), (e, e_t)


def _expand_groups_bwd(res, ct):
    e, e_t = res
    return _split_dot(ct, e_t), jnp.zeros_like(e), jnp.zeros_like(e_t)


expand_groups.defvjp(_expand_groups_fwd, _expand_groups_bwd)


def sgu_fn(pu, pv, ln_g, ln_b, w_s, b_t, out_g, e, e_t):
    rows = pu.shape[0]
    b_exp = expand_groups(b_t, e, e_t)
    u, v = _gelu(pu), _gelu(pv)
    mu = jnp.mean(v, axis=-1, keepdims=True)
    vc = v - mu
    var = jnp.mean(vc * vc, axis=-1, keepdims=True)
    v = vc * lax.rsqrt(var + LN_EPS) * ln_g + ln_b
    tri = lax.broadcasted_iota(jnp.int32, (CHUNK, CHUNK), 0) >= lax.broadcasted_iota(jnp.int32, (CHUNK, CHUNK), 1)
    left = lax.broadcasted_iota(jnp.int32, (CHUNK, LANES), 1) < HEAD
    chunks = []
    for c0 in range(0, rows, CHUNK):
        cols = []
        for gp in range(HEADS // 2):
            vp = v[c0:c0 + CHUNK, gp * LANES:(gp + 1) * LANES]
            wa = jnp.where(tri, w_s[2 * gp], 0.0)
            wb = jnp.where(tri, w_s[2 * gp + 1], 0.0)
            cols.append(jnp.where(left, _bdot(wa, vp), _bdot(wb, vp)))
        chunks.append(jnp.concatenate(cols, axis=1) + b_exp)
    s = jnp.concatenate(chunks, axis=0) if len(chunks) > 1 else chunks[0]
    y = u * s
    return (y * lax.rsqrt(jnp.mean(y * y, axis=-1, keepdims=True) + RMS_EPS) * out_g,)


def swiglu_fn(gate, up):
    return (gate * _sigmoid(gate) * up,)


SCAN_CHUNK = 32
NB = C // LANES
STEP_GROUP = 4


def to_cols(a):
    t = a.shape[0]
    return a.reshape(t // 8, LANES, HEAD).transpose(0, 2, 1)


def from_cols(a):
    return a.transpose(0, 2, 1).reshape(a.shape[0] * 8, C)


def _scan_consts():
    i = np.arange(LANES)
    spread = (i[:, None] % 2) == (i[None, :] // HEAD)
    pick = (i[:, None] // HEAD) == (i[None, :] % 2)
    j2 = (i[:, None] // HEAD) == (i[None, :] // HEAD)
    sel = (i[None, :] // 2) == (np.arange(8 * NB * HEAD)[:, None] // HEAD)
    return (jnp.asarray(spread, BF16), jnp.asarray(pick, BF16), jnp.asarray(np.concatenate([j2, j2], 0), BF16),
            jnp.asarray(sel, BF16))


def _stack(blocks):
    return jnp.concatenate(blocks, axis=0) if len(blocks) > 1 else blocks[0]


def _unstack(x, n):
    return [x[i * HEAD:(i + 1) * HEAD] for i in range(n)]


def _headsums(blocks, j2k, group):
    res = []
    for g0 in range(0, len(blocks), group):
        x = _stack(blocks[g0:g0 + group])
        hi = x.astype(BF16)
        lo = (x - hi.astype(F32)).astype(BF16)
        out = jnp.dot(jnp.concatenate([hi, lo], axis=1), j2k, preferred_element_type=F32)
        res += _unstack(out, len(blocks[g0:g0 + group]))
    return res


def _headsums_out(blocks, pick):
    return _unstack(jnp.dot(_stack(blocks).astype(BF16), pick, preferred_element_type=F32), len(blocks))


def _expand8(tile, sel, spread):
    lhs = jnp.tile(tile.astype(BF16), (8 * NB, 1)) * sel
    return _unstack(jnp.dot(lhs, spread, preferred_element_type=F32), 8 * NB)


def _collapse(tile, blocks, first, lane_pair):
    for n, blk in enumerate(blocks):
        tile = jnp.where(jnp.tile(lane_pair == first + n, (HEAD // 8, 1)), blk, tile)
    return tile


def _row(tile, j, cb):
    return jnp.broadcast_to(tile[j:j + 1, cb * LANES:(cb + 1) * LANES], (HEAD, LANES))


def _scan_step(s_ref, rows, vexp, j2k, j, hist=None):
    w_t, k_t, a_t, b_t, r_t = rows
    s = [s_ref[:, cb * LANES:(cb + 1) * LANES] for cb in range(NB)]
    sab = _headsums([s[cb] * _row(a_t, j, cb) for cb in range(NB)], j2k, STEP_GROUP)
    out = []
    for cb in range(NB):
        sl = slice(cb * LANES, (cb + 1) * LANES)
        s_new = s[cb] * _row(w_t, j, cb) + sab[cb] * _row(b_t, j, cb) + vexp[j * NB + cb] * _row(k_t, j, cb)
        s_ref[:, sl] = s_new
        if hist is None:
            out.append(s_new * _row(r_t, j, cb))
        else:
            s_hist, sab_hist, idx = hist
            s_hist[idx + 1, :, sl] = s_new
            sab_hist[idx, :, sl] = sab[cb]
    return out


def wkv_fwd(r, w, k, v3, a, b):
    t = r.shape[0]
    nc = t // SCAN_CHUNK
    n8 = SCAN_CHUNK // 8

    def body(r_ref, w_ref, k_ref, a_ref, b_ref, v3_ref, spread_ref, pick_ref, j2k_ref, sel_ref, y3_ref, ck_ref,
             s_ref):
        @pl.when(pl.program_id(0) == 0)
        def _():
            s_ref[...] = jnp.zeros_like(s_ref)

        ck_ref[0] = s_ref[...]
        spread, pick, j2k, sel = spread_ref[...], pick_ref[...], j2k_ref[...], sel_ref[...]
        lane_pair = lax.broadcasted_iota(jnp.int32, (8, LANES), 1) // 2

        def t8_body(t8, carry):
            row0 = pl.multiple_of(t8 * 8, 8)
            rows = [ref[pl.ds(row0, 8), :] for ref in (w_ref, k_ref, a_ref, b_ref, r_ref)]
            vexp = _expand8(v3_ref[t8], sel, spread)
            y3 = jnp.zeros((HEAD, LANES), F32)
            for j in range(8):
                y3 = _collapse(y3, _headsums_out(_scan_step(s_ref, rows, vexp, j2k, j), pick), j * NB, lane_pair)
            y3_ref[t8] = y3
            return carry

        lax.fori_loop(0, n8, t8_body, 0)

    row_spec = pl.BlockSpec((SCAN_CHUNK, C), lambda i: (i, 0))
    col_spec = pl.BlockSpec((n8, HEAD, LANES), lambda i: (i, 0, 0))
    consts = _scan_consts()
    return _pcall(
        body, name="wkv_fwd", grid=(nc,),
        in_specs=[row_spec] * 5 + [col_spec] + [pl.BlockSpec(c.shape, lambda i: (0, 0)) for c in consts],
        out_specs=[col_spec, pl.BlockSpec((1, HEAD, C), lambda i: (i, 0, 0))],
        out_shape=[jax.ShapeDtypeStruct((t // 8, HEAD, LANES), F32), jax.ShapeDtypeStruct((nc, HEAD, C), F32)],
        scratch_shapes=[pltpu.VMEM((HEAD, C), F32)],
        compiler_params=_cparams(("arbitrary",)),
    )(r, w, k, a, b, v3, *consts)


def wkv_bwd(r, w, k, v3, a, b, ckpt, dy3):
    t = r.shape[0]
    nc = t // SCAN_CHUNK
    n8 = SCAN_CHUNK // 8

    def body(r_ref, w_ref, k_ref, a_ref, b_ref, v3_ref, dy3_ref, ck_ref, spread_ref, pick_ref, j2k_ref, sel_ref,
             dr_ref, dw_ref, dk_ref, da_ref, db_ref, dv3_ref, s_ref, g_ref, s_hist, sab_hist):
        @pl.when(pl.program_id(0) == 0)
        def _():
            g_ref[...] = jnp.zeros_like(g_ref)

        spread, pick, j2k, sel = spread_ref[...], pick_ref[...], j2k_ref[...], sel_ref[...]
        lane_pair = lax.broadcasted_iota(jnp.int32, (8, LANES), 1) // 2
        sub = lax.broadcasted_iota(jnp.int32, (8, LANES), 0)
        s_ref[...] = ck_ref[0]
        s_hist[0] = ck_ref[0]

        def redo(t8, carry):
            row0 = pl.multiple_of(t8 * 8, 8)
            rows = [ref[pl.ds(row0, 8), :] for ref in (w_ref, k_ref, a_ref, b_ref, r_ref)]
            vexp = _expand8(v3_ref[t8], sel, spread)
            for j in range(8):
                _scan_step(s_ref, rows, vexp, j2k, j, hist=(s_hist, sab_hist, t8 * 8 + j))
            return carry

        lax.fori_loop(0, n8, redo, 0)

        def back(q, carry):
            t8 = n8 - 1 - q
            row0 = pl.multiple_of(t8 * 8, 8)
            w_t, k_t, a_t, b_t, r_t = [ref[pl.ds(row0, 8), :] for ref in (w_ref, k_ref, a_ref, b_ref, r_ref)]
            vexp = _expand8(v3_ref[t8], sel, spread)
            dyexp = _expand8(dy3_ref[t8], sel, spread)
            dv3 = jnp.zeros((HEAD, LANES), F32)
            tiles = [[jnp.zeros((8, LANES), F32) for _ in range(NB)] for _ in range(5)]
            for j in range(7, -1, -1):
                idx = t8 * 8 + j
                g = [g_ref[:, cb * LANES:(cb + 1) * LANES] + dyexp[j * NB + cb] * _row(r_t, j, cb)
                     for cb in range(NB)]
                dsab = _headsums([g[cb] * _row(b_t, j, cb) for cb in range(NB)], j2k, STEP_GROUP)
                dv3 = _collapse(dv3, _headsums_out([g[cb] * _row(k_t, j, cb) for cb in range(NB)], pick), j * NB,
                                lane_pair)
                for cb in range(NB):
                    sl = slice(cb * LANES, (cb + 1) * LANES)
                    s_new = s_hist[idx + 1, :, sl]
                    s_old = s_hist[idx, :, sl]
                    sab = sab_hist[idx, :, sl]
                    sums = (s_new * dyexp[j * NB + cb], g[cb] * s_old, g[cb] * vexp[j * NB + cb],
                            s_old * dsab[cb], g[cb] * sab)
                    for n, prod in enumerate(sums):
                        rowsum = jnp.broadcast_to(jnp.sum(prod, axis=0, keepdims=True), (8, LANES))
                        tiles[n][cb] = jnp.where(sub == j, rowsum, tiles[n][cb])
                    g_ref[:, sl] = g[cb] * _row(w_t, j, cb) + dsab[cb] * _row(a_t, j, cb)
            dv3_ref[t8] = dv3
            for n, ref in enumerate((dr_ref, dw_ref, dk_ref, da_ref, db_ref)):
                for cb in range(NB):
                    ref[pl.ds(row0, 8), cb * LANES:(cb + 1) * LANES] = tiles[n][cb]
            return carry

        lax.fori_loop(0, n8, back, 0)

    row_spec = pl.BlockSpec((SCAN_CHUNK, C), lambda i: (nc - 1 - i, 0))
    col_spec = pl.BlockSpec((n8, HEAD, LANES), lambda i: (nc - 1 - i, 0, 0))
    consts = _scan_consts()
    return _pcall(
        body, name="wkv_bwd", grid=(nc,),
        in_specs=[row_spec] * 5 + [col_spec, col_spec, pl.BlockSpec((1, HEAD, C), lambda i: (nc - 1 - i, 0, 0))]
        + [pl.BlockSpec(c.shape, lambda i: (0, 0)) for c in consts],
        out_specs=[row_spec] * 5 + [col_spec],
        out_shape=[jax.ShapeDtypeStruct((t, C), F32)] * 5 + [jax.ShapeDtypeStruct((t // 8, HEAD, LANES), F32)],
        scratch_shapes=[pltpu.VMEM((HEAD, C), F32), pltpu.VMEM((HEAD, C), F32),
                        pltpu.VMEM((SCAN_CHUNK + 1, HEAD, C), F32), pltpu.VMEM((SCAN_CHUNK, HEAD, C), F32)],
        compiler_params=_cparams(("arbitrary",)),
    )(r, w, k, a, b, v3, dy3, ckpt, *consts)


def _prev_rows(cur, before, first_tile):
    last = jnp.where(first_tile, 0.0, before[7:8, :])
    row = lax.broadcasted_iota(jnp.int32, cur.shape, 0)
    return jnp.where(row == 0, last, pltpu.roll(cur, 1, 0))


def shift_fwd(p, mu, col, name):
    t, width = p.shape[0], mu.shape[1]
    tm = _tile(t, 256)

    def body(p_ref, before_ref, mu_ref, o_ref):
        cur = p_ref[...]
        prev = _prev_rows(cur, before_ref[...], pl.program_id(0) == 0)
        o_ref[...] = cur + (prev - cur) * mu_ref[...]

    return _pcall(
        body, name=name, grid=(t // tm,),
        in_specs=[pl.BlockSpec((tm, width), lambda i: (i, col)),
                  pl.BlockSpec((8, width), lambda i: (jnp.maximum(i * (tm // 8) - 1, 0), col)),
                  pl.BlockSpec((1, width), lambda i: (0, 0))],
        out_specs=pl.BlockSpec((tm, width), lambda i: (i, 0)),
        out_shape=jax.ShapeDtypeStruct((t, width), F32),
        compiler_params=_cparams(("parallel",)),
    )(p, p, mu)


def shift_bwd(dps, p, mu, col, name):
    t, width = dps.shape
    tm = _tile(t, 256)
    nt = t // tm

    def body(d_ref, after_ref, p_ref, before_ref, mu_ref, dp_ref, dmu_ref):
        i = pl.program_id(0)
        d, cur, mu_v = d_ref[...], p_ref[...], mu_ref[...]
        row = lax.broadcasted_iota(jnp.int32, d.shape, 0)
        first_after = jnp.where(i == nt - 1, 0.0, after_ref[0:1, :])
        nxt = jnp.where(row == tm - 1, first_after, pltpu.roll(d, tm - 1, 0))
        dp_ref[...] = (d * (1.0 - mu_v) + nxt * mu_v).astype(BF16)
        prev = _prev_rows(cur, before_ref[...], i == 0)
        part = jnp.sum(d * (prev - cur), axis=0, keepdims=True)

        @pl.when(i == 0)
        def _():
            dmu_ref[...] = part

        @pl.when(i > 0)
        def _():
            dmu_ref[...] += part

    return _pcall(
        body, name=name, grid=(nt,),
        in_specs=[pl.BlockSpec((tm, width), lambda i: (i, 0)),
                  pl.BlockSpec((8, width), lambda i: (jnp.minimum((i + 1) * (tm // 8), t // 8 - 1), 0)),
                  pl.BlockSpec((tm, width), lambda i: (i, col)),
                  pl.BlockSpec((8, width), lambda i: (jnp.maximum(i * (tm // 8) - 1, 0), col)),
                  pl.BlockSpec((1, width), lambda i: (0, 0))],
        out_specs=[pl.BlockSpec((tm, width), lambda i: (i, 0)), pl.BlockSpec((1, width), lambda i: (0, 0))],
        out_shape=[jax.ShapeDtypeStruct((t, width), BF16), jax.ShapeDtypeStruct((1, width), F32)],
        compiler_params=_cparams(("arbitrary",)),
    )(dps, dps, p, p, mu)


def loss_head(x3, tgt, g):
    t, d = x3.shape
    tm = _tile(t, 256)

    def body(x_ref, t_ref, g_ref, loss_ref, dx_ref, dg_ref, dxb_ref):
        (y,), vjp = jax.vjp(lambda a, b: rms_fn(a, b), x_ref[...], g_ref[...])
        diff = y - t_ref[...]
        part = 0.5 * jnp.sum(jnp.mean(diff * diff, axis=-1, keepdims=True), axis=0, keepdims=True)
        dx, dg = vjp((diff * (1.0 / d),))
        dx_ref[...] = dx
        dxb_ref[...] = dx.astype(BF16)
        part = jnp.broadcast_to(part, (1, LANES))
        first = pl.program_id(0) == 0

        @pl.when(first)
        def _():
            loss_ref[...] = part
            dg_ref[...] = dg

        @pl.when(jnp.logical_not(first))
        def _():
            loss_ref[...] += part
            dg_ref[...] += dg

    row = pl.BlockSpec((tm, d), lambda i: (i, 0))
    vec = pl.BlockSpec((1, d), lambda i: (0, 0))
    return _pcall(
        body, name="loss_head", grid=(t // tm,), in_specs=[row, row, vec],
        out_specs=[pl.BlockSpec((1, LANES), lambda i: (0, 0)), row, vec, row],
        out_shape=[jax.ShapeDtypeStruct((1, LANES), F32), jax.ShapeDtypeStruct((t, d), F32),
                   jax.ShapeDtypeStruct((1, d), F32), jax.ShapeDtypeStruct((t, d), BF16)],
        compiler_params=_cparams(("arbitrary",)),
    )(x3, tgt, g)


ADAM_LR, ADAM_B1, ADAM_B2, ADAM_EPS, ADAM_WD, ADAM_STEP = 0.001, 0.9, 0.999, 1e-08, 0.01, 10


def adamw(w, g, m, v, name):
    rows, width = w.shape
    tr = _tile(rows, 256)

    def body(w_ref, g_ref, m_ref, v_ref, d_ref, nm_ref, nv_ref):
        gv = g_ref[...]
        m_new = ADAM_B1 * m_ref[...] + (1.0 - ADAM_B1) * gv
        v_new = ADAM_B2 * v_ref[...] + (1.0 - ADAM_B2) * (gv * gv)
        m_hat = m_new / (1.0 - ADAM_B1 ** ADAM_STEP)
        v_hat = v_new / (1.0 - ADAM_B2 ** ADAM_STEP)
        d_ref[...] = -ADAM_LR * (m_hat / (jnp.sqrt(v_hat) + ADAM_EPS) + ADAM_WD * w_ref[...])
        nm_ref[...] = m_new
        nv_ref[...] = v_new

    spec = pl.BlockSpec((tr, width), lambda i: (i, 0))
    return _pcall(
        body, name=name, grid=(rows // tr,), in_specs=[spec] * 4, out_specs=[spec] * 3,
        out_shape=[jax.ShapeDtypeStruct((rows, width), F32)] * 3,
        compiler_params=_cparams(("parallel",)),
    )(w, g, m, v)


def sum_slots(q, name):
    n, rows, width = q.shape
    tr = _tile(rows, 512)

    def body(*refs):
        acc = refs[0][...].astype(F32)
        for r in refs[1:n]:
            acc = acc + r[...].astype(F32)
        refs[n][...] = acc

    specs = [pl.BlockSpec((None, tr, width), functools.partial(lambda s, i: (s, i, 0), s)) for s in range(n)]
    return _pcall(body, name=name, grid=(rows // tr,), in_specs=specs,
                  out_specs=pl.BlockSpec((tr, width), lambda i: (i, 0)),
                  out_shape=jax.ShapeDtypeStruct((rows, width), F32),
                  compiler_params=_cparams(("parallel",)))(*([q] * n))


MESH = pl.DeviceIdType.MESH
ANY_SPEC = pl.BlockSpec(memory_space=pl.ANY)


def _coords():
    return lax.axis_index("x"), lax.axis_index("y"), lax.axis_index("c")


def _other_chips(x, y):
    return [(1 - x, y), (x, 1 - y), (1 - x, 1 - y)]


def push(name, srcs, n_dst, plan_fn):
    n_arr = len(srcs)
    n_send = len(plan_fn(0, 0, 0))

    def body(*refs):
        src_refs, dst_refs = refs[:n_arr], refs[n_arr:2 * n_arr]
        send_sems, recv_sems = refs[2 * n_arr:]
        sends = plan_fn(*_coords())
        out = []
        for q, (src_ref, dst_ref) in enumerate(zip(src_refs, dst_refs)):
            out += [pltpu.make_async_remote_copy(src_ref.at[si], dst_ref.at[di], send_sems.at[q * n_send + k],
                                                 recv_sems.at[q * n_send + k], device_id=peer, device_id_type=MESH)
                    for k, (si, peer, di, _) in enumerate(sends)]
        for cp in out:
            cp.start()
        for q, (src_ref, dst_ref) in enumerate(zip(src_refs, dst_refs)):
            for k, (si, peer, _, ri) in enumerate(sends):
                pltpu.make_async_remote_copy(src_ref.at[si], dst_ref.at[ri], send_sems.at[q * n_send + k],
                                             recv_sems.at[q * n_send + k], device_id=peer,
                                             device_id_type=MESH).wait_recv()
        for cp in out:
            cp.wait_send()

    return _pcall(
        body, name=name, in_specs=[ANY_SPEC] * n_arr, out_specs=[ANY_SPEC] * n_arr,
        out_shape=[jax.ShapeDtypeStruct((n_dst,) + s.shape[1:], s.dtype) for s in srcs],
        scratch_shapes=[pltpu.SemaphoreType.DMA((n_arr * n_send,)), pltpu.SemaphoreType.DMA((n_arr * n_send,))],
    )(*srcs)


def plan_sibling_quarters(x, y, c):
    return [(2 * s + (1 - c), (x, y, 1 - c), s, s) for s in range(4)]


def plan_chips_by_shard(x, y, c):
    me = 2 * x + y
    return [(2 * px + py, (px, py, c), me, 2 * px + py) for px, py in _other_chips(x, y)]


def plan_chips_half(x, y, c):
    me = 2 * x + y
    return [(c, (px, py, c), me, 2 * px + py) for px, py in _other_chips(x, y)]


def swap_halves(name, bufs):
    n_arr = len(bufs)

    def body(*refs):
        buf_refs = refs[n_arr:2 * n_arr]
        send_sems, recv_sems = refs[2 * n_arr:]
        x, y, c = _coords()
        out = [pltpu.make_async_remote_copy(b.at[c], b.at[c], send_sems.at[q], recv_sems.at[q],
                                            device_id=(x, y, 1 - c), device_id_type=MESH)
               for q, b in enumerate(buf_refs)]
        for cp in out:
            cp.start()
        for q, b in enumerate(buf_refs):
            pltpu.make_async_remote_copy(b.at[1 - c], b.at[1 - c], send_sems.at[q], recv_sems.at[q],
                                         device_id=(x, y, 1 - c), device_id_type=MESH).wait_recv()
        for cp in out:
            cp.wait_send()

    return _pcall(
        body, name=name, in_specs=[ANY_SPEC] * n_arr, out_specs=[ANY_SPEC] * n_arr,
        out_shape=[jax.ShapeDtypeStruct(b.shape, b.dtype) for b in bufs],
        input_output_aliases={q: q for q in range(n_arr)},
        scratch_shapes=[pltpu.SemaphoreType.DMA((n_arr,)), pltpu.SemaphoreType.DMA((n_arr,))],
    )(*bufs)


HBM_SPEC = pl.BlockSpec(memory_space=pltpu.HBM)
SEM_SPEC = pl.BlockSpec(memory_space=pltpu.SEMAPHORE)
DATAFLOW = pltpu.SideEffectType.DATAFLOW_SIDE_EFFECTING


def _in_hbm(a):
    return pltpu.with_memory_space_constraint(a, pltpu.HBM)


def push_start(name, srcs, lands, plan_fn, after):
    n_arr = len(srcs)
    n_send = len(plan_fn(0, 0, 0))
    arrays = list(srcs) + ([] if lands is None else list(lands))
    n_all = len(arrays)

    def body(*refs):
        src_refs, land_refs = refs[:n_arr], refs[n_all - n_arr:n_all]
        send_sems, recv_sems = refs[n_all + 1:n_all + 3]
        token = refs[-1]
        sends = plan_fn(*_coords())
        for q, (src_ref, land_ref) in enumerate(zip(src_refs, land_refs)):
            for k, (si, peer, di, _) in enumerate(sends):
                pltpu.make_async_remote_copy(src_ref.at[si], land_ref.at[di], send_sems.at[q * n_send + k],
                                             recv_sems.at[q * n_send + k], device_id=peer,
                                             device_id_type=MESH).start()
        token[...] = jnp.zeros_like(token)

    sems = pltpu.SemaphoreType.DMA((n_arr * n_send,))
    out = _pcall(
        body, name=name,
        in_specs=[HBM_SPEC] * n_all + [ANY_SPEC],
        out_specs=[SEM_SPEC, SEM_SPEC] + [HBM_SPEC] * n_all + [pl.BlockSpec(memory_space=pltpu.VMEM)],
        out_shape=[sems, sems] + [pltpu.HBM(a.shape, a.dtype) for a in arrays]
        + [jax.ShapeDtypeStruct((8, LANES), F32)],
        input_output_aliases={q: 2 + q for q in range(n_all)},
        compiler_params=pltpu.CompilerParams(has_side_effects=DATAFLOW),
    )(*[_in_hbm(a) for a in arrays], after)
    return out[:-1], out[-1]


def push_wait(name, state, plan_fn, after, in_place=False, both=False):
    arrays = list(state[2:])
    n_all = len(arrays)
    n_arr = n_all if in_place else n_all // 2
    n_send = len(plan_fn(0, 0, 0))
    send_sems, recv_sems = state[:2]

    def body(*refs):
        src_refs, land_refs = refs[:n_arr], refs[n_all - n_arr:n_all]
        send_ref, recv_ref = refs[n_all:n_all + 2]
        sends = plan_fn(*_coords())
        for q, (src_ref, land_ref) in enumerate(zip(src_refs, land_refs)):
            for k, (si, peer, _, ri) in enumerate(sends):
                cp = pltpu.make_async_remote_copy(src_ref.at[si], land_ref.at[ri], send_ref.at[q * n_send + k],
                                                  recv_ref.at[q * n_send + k], device_id=peer, device_id_type=MESH)
                cp.wait_send()
                cp.wait_recv()

    out = _pcall(
        body, name=name,
        in_specs=[HBM_SPEC] * n_all + [SEM_SPEC, SEM_SPEC, ANY_SPEC],
        out_specs=[HBM_SPEC] * n_all,
        out_shape=[pltpu.HBM(a.shape, a.dtype) for a in arrays],
        input_output_aliases={q: q for q in range(n_all)},
        compiler_params=pltpu.CompilerParams(has_side_effects=DATAFLOW),
    )(*arrays, send_sems, recv_sems, after)
    return list(out) if both else list(out[n_all - n_arr:])


def plan_chips_gather(x, y, c):
    me = 2 * x + y
    return [(c, (px, py, c), 2 * me + c, 2 * (2 * px + py) + c) for px, py in _other_chips(x, y)]


def plan_forward(x, y, c):
    return [(2 * (2 * px + py) + c, (x, y, 1 - c), 2 * (2 * px + py) + c, 2 * (2 * px + py) + (1 - c))
            for px, py in _other_chips(x, y)]


def add_kept(g8, other, name):
    _, rows, cols = other.shape
    tr = _tile(rows, 512)

    def body(c_ref, a_ref, b_ref, o_ref):
        o_ref[...] = (a_ref[...].astype(F32) + b_ref[...].astype(F32)).astype(BF16)

    spec = pl.BlockSpec((None, tr, cols), lambda s, i, c_ref: (s, i, 0))
    grid_spec = pltpu.PrefetchScalarGridSpec(
        num_scalar_prefetch=1, grid=(4, rows // tr),
        in_specs=[pl.BlockSpec((None, tr, cols), lambda s, i, c_ref: (2 * s + c_ref[0], i, 0)), spec],
        out_specs=spec)
    return _pcall(body, name=name, grid_spec=grid_spec, out_shape=jax.ShapeDtypeStruct((4, rows, cols), BF16),
                  compiler_params=_cparams(("parallel", "parallel")))(
        lax.axis_index("c").astype(jnp.int32).reshape(1), g8, other)


def reduce_scatter_begin(grads, names, tag):
    g8 = [g.reshape(8, g.shape[1] // 2, g.shape[2]) for g in grads]
    return _chip_sums_start(g8, push(tag + "_sibling", g8, 4, plan_sibling_quarters), names, tag)


def _chip_sums_start(g8, from_sibling, names, tag):
    chip_sum = [add_kept(a, b, tag + "_add_" + n) for a, b, n in zip(g8, from_sibling, names)]
    return push_start(tag + "_start", chip_sum, chip_sum, plan_chips_by_shard, chip_sum[-1])


def sibling_exchange_begin(grads, tag):
    g8 = [g.reshape(8, g.shape[1] // 2, g.shape[2]) for g in grads]
    lands = [lax.empty((4,) + g.shape[1:], g.dtype) for g in g8]
    return push_start(tag + "_sib_start", g8, lands, plan_sibling_quarters, g8[-1])


def reduce_scatter_continue(state, names, tag, after):
    n = len(names)
    arrays = push_wait(tag + "_sib_wait", state, plan_sibling_quarters, after, both=True)
    return _chip_sums_start(arrays[:n], arrays[n:], names, tag)


def reduce_scatter_end(state, names, tag, after):
    by_chip = push_wait(tag + "_wait", state, plan_chips_by_shard, after)
    total_half = [sum_slots(q, tag + "_sum_" + n) for q, n in zip(by_chip, names)]
    both = swap_halves(tag + "_halves", [jnp.broadcast_to(t[None], (2,) + t.shape) for t in total_half])
    return [b.reshape(2 * b.shape[1], b.shape[2]) for b in both]


def gather_begin(tag, mine, after):
    lands = [jnp.broadcast_to(m[None], (4,) + m.shape).reshape((8,) + m.shape[1:]) for m in mine]
    return push_start(tag + "_start", mine, lands, plan_chips_gather, after)


def gather_forward(tag, state, after):
    lands = push_wait(tag + "_wait", state, plan_chips_gather, after)
    return push_start(tag + "_fwd_start", lands, None, plan_forward, lands[0])


def gather_end(tag, state, after):
    bufs = push_wait(tag + "_fwd_wait", state, plan_forward, after, in_place=True)
    return [g.reshape(4, 2 * g.shape[1], g.shape[2]) for g in bufs]


def _pack(parts, dtype, rows):
    flat = jnp.concatenate([p.astype(dtype).reshape(-1) for p in parts])
    return jnp.pad(flat, (0, rows * C - flat.shape[0])).reshape(rows, C)


def _unpack(flat, shapes):
    flat = flat.reshape(-1)
    out, at = [], 0
    for s in shapes:
        n = int(np.prod(s))
        out.append(flat[at:at + n].reshape(s))
        at += n
    return out


def all_reduce_begin(vec):
    rows, width = vec.shape
    (pair,) = swap_halves("ar_sibling", [jnp.broadcast_to(vec[None], (2, rows, width))])
    chip_sum = sum_slots(pair, "ar_sum_sibling").reshape(2, rows // 2, width)
    mine = lax.dynamic_index_in_dim(chip_sum, lax.axis_index("c"), axis=0, keepdims=True)
    return push_start("ar_start", [chip_sum], [jnp.broadcast_to(mine, (4, rows // 2, width))], plan_chips_half,
                      chip_sum)


def all_reduce_end(state, after):
    (by_chip,) = push_wait("ar_wait", state, plan_chips_half, after)
    total_half = sum_slots(by_chip, "ar_sum_chips")
    (both,) = swap_halves("ar_halves", [jnp.broadcast_to(total_half[None], (2,) + total_half.shape)])
    return both.reshape(2 * total_half.shape[0], total_half.shape[1])


SMALL_ROWS = 576


def kernel(x, norm1_g, w_in, shift_mu, w0, w2, a0, a2, g2, k_k, k_a, r_k, gn_w, gn_b, sgu_ln_g, sgu_ln_b, sgu_w, sgu_b, sgu_out_g, w_out, norm2_g, w_gate, w_up, w_down, final_g, loss_target, m_norm1_g, m_w_in, m_shift_mu, m_w0, m_w2, m_a0, m_a2, m_g2, m_k_k, m_k_a, m_r_k, m_gn_w, m_gn_b, m_sgu_ln_g, m_sgu_ln_b, m_sgu_w, m_sgu_b, m_sgu_out_g, m_w_out, m_norm2_g, m_w_gate, m_w_up, m_w_down, m_final_g, v_norm1_g, v_w_in, v_shift_mu, v_w0, v_w2, v_a0, v_a2, v_g2, v_k_k, v_k_a, v_r_k, v_gn_w, v_gn_b, v_sgu_ln_g, v_sgu_ln_b, v_sgu_w, v_sgu_b, v_sgu_out_g, v_w_out, v_norm2_g, v_w_gate, v_w_up, v_w_down, v_final_g):
    args = dict(locals())
    names = ["norm1_g", "w_in", "shift_mu", "w0", "w2", "a0", "a2", "g2", "k_k", "k_a", "r_k", "gn_w", "gn_b",
             "sgu_ln_g", "sgu_ln_b", "sgu_w", "sgu_b", "sgu_out_g", "w_out", "norm2_g", "w_gate", "w_up", "w_down",
             "final_g"]
    big = ["w_in", "w_out", "w_gate", "w_up", "w_down"]
    xi, yi = lax.axis_index("x"), lax.axis_index("y")
    chip = 2 * xi + yi
    x2d, tgt = x[0], loss_target[0]
    t = x2d.shape[0]

    lora_names = ["w2", "a2", "g2"]
    first, later = ["w_in"] + lora_names, ["w_out", "w_gate", "w_up", "w_down"]

    def halves(n, token=None):
        m = args[n][0] if token is None else args[n][0] + token[0, 0]
        m = m.astype(BF16)
        return m.reshape(2, m.shape[0] // 2, m.shape[1])

    first_state, first_token = gather_begin("g0", [halves(n) for n in first], norm1_g)
    later_mine = [halves(n, first_token) for n in later]
    first_state, first_token = gather_forward("g0", first_state, later_mine[-1])
    got = gather_end("g0", first_state, first_token)
    later_state, later_token = gather_begin("g1", later_mine, got[0])
    norm1_g_fwd = norm1_g + later_token[0, 0]
    full = {n: blk.transpose(1, 0, 2).reshape(blk.shape[1], 4 * blk.shape[2]) for n, blk in zip(first, got)}
    w_in_f = full["w_in"]
    w_in_p = jnp.concatenate([w_in_f[:, :3 * C], w_in_f[:, 3360:], w_in_f[:, 3 * C:3360],
                              jnp.zeros((D, P_W - D_IN), BF16)], axis=1)
    mu_rkv = shift_mu[:, :3 * C]
    mu_lora = jnp.pad(shift_mu[:, 3 * C:], ((0, 0), (0, LORA_W - 288)))
    w2p = jnp.pad(full["w2"], ((0, 64), (0, 0)))
    a2p = jnp.pad(full["a2"], ((64, 0), (0, 0)))
    g2p = jnp.pad(full["g2"], ((0, 96), (0, 0)))
    ii = np.arange(C)
    jm = jnp.asarray((ii[:, None] // HEAD) == (ii[None, :] // HEAD), BF16)
    e_np = (np.arange(LANES)[:, None] == (ii[None, :] // HEAD))
    e_mat, e_t = jnp.asarray(e_np, BF16), jnp.asarray(e_np.T, BF16)
    b_t = jnp.pad(sgu_b[0].T, ((0, 0), (0, LANES - HEADS)))
    r_k_flat = r_k.reshape(1, C)
    tm_row = _tile(t, 256)
    tm_vjp = _tile(t, 128)

    (h1,) = row_fwd("norm1", rms_fn, [Tiled(x2d)], [norm1_g_fwd], [], [(D, BF16)], tm_row)
    p = mm(h1, w_in_p, name="proj_in")
    ps_rkv = shift_fwd(p, mu_rkv, 0, "shift_rkv")
    ps_lora = shift_fwd(p, mu_lora, P_LORA // LORA_W, "shift_lora")
    pre_par = [w0, w2p, a0, a2p, g2p, k_k, k_a]
    r_, w_, k2, v_, an, b_, g_ = row_fwd("pre", pre_fn, [Tiled(ps_rkv), Tiled(ps_lora)], pre_par, [jm],
                                         [(C, F32)] * 7, tm_vjp)
    v3 = to_cols(v_)
    y3, ckpt = wkv_fwd(r_, w_, k2, v3, an, b_)
    later_state, later_token = gather_forward("g1", later_state, y3)
    y_ = from_cols(y3 + later_token[0, 0])
    post_in = [Tiled(z) for z in (y_, r_, k2, v_, g_)]
    post_par = [gn_w, gn_b, r_k_flat]
    (y_rwkv,) = row_fwd("post", post_fn, post_in, post_par, [jm], [(C, BF16)], tm_vjp)
    sgu_in = [Tiled(p, C, 3), Tiled(p, C, 4)]
    sgu_par = [sgu_ln_g, sgu_ln_b, sgu_w[0], b_t, sgu_out_g]
    (y_sgu,) = row_fwd("sgu", sgu_fn, sgu_in, sgu_par, [e_mat, e_t], [(C, BF16)], tm_vjp)
    y_cat = jnp.concatenate([y_rwkv, y_sgu], axis=1)
    for n, blk in zip(later, gather_end("g1", later_state, y_cat)):
        full[n] = blk if n in ("w_gate", "w_up") else blk.reshape(4 * blk.shape[1], blk.shape[2])
    x_mid = mm(y_cat, full["w_out"], res=x2d, name="proj_out")
    (h2,) = row_fwd("norm2", rms_fn, [Tiled(x_mid)], [norm2_g], [], [(D, BF16)], tm_row)
    gate = mm(h2, full["w_gate"], shards="b", name="ffn_gate")
    up = mm(h2, full["w_up"], shards="b", name="ffn_up")
    (act,) = row_fwd("swiglu", swiglu_fn, [Tiled(gate), Tiled(up)], [], [], [(D_FF, BF16)], tm_row)
    x_out = mm(act, full["w_down"], res=x_mid, name="ffn_down")
    loss_part, dx_out, d_final_g, dx_out_b = loss_head(x_out, tgt, final_g.reshape(1, D))

    grads = {"final_g": d_final_g.reshape(D)}
    d_act = mm(dx_out_b, full["w_down"], tb=True, name="d_act")
    grads["w_down"] = mm(act, dx_out_b, ta=True, out_dtype=BF16, name="d_w_down")
    d_gate, d_up = row_bwd("swiglu_b", swiglu_fn, [Tiled(gate), Tiled(up)], [], [], [Tiled(d_act)], tm_row,
                           tiled_out_dtypes=[BF16, BF16])
    grads["w_gate"] = mm(h2, d_gate, shards="out", out_dtype=BF16, name="d_w_gate")
    grads["w_up"] = mm(h2, d_up, shards="out", out_dtype=BF16, name="d_w_up")
    d_h2 = mm(d_gate, full["w_gate"], shards="bt", name="d_h2_gate")
    d_h2 = mm(d_up, full["w_up"], shards="bt", res=d_h2, name="d_h2_up")
    dx_mid, grads["norm2_g"], dx_mid_b = row_bwd("norm2_b", rms_res_fn, [Tiled(x_mid)], [norm2_g], [],
                                                 [Tiled(d_h2), Tiled(dx_out)], tm_row, bf16_copy_of=(0,))
    d_ycat = mm(dx_mid_b, full["w_out"], tb=True, name="d_ycat")
    grads["w_out"] = mm(y_cat, dx_mid_b, ta=True, out_dtype=BF16, name="d_w_out")
    rs_state, rs_token = sibling_exchange_begin([grads[n].reshape((4,) + args[n].shape[1:]) for n in later], "rs2")
    sgu_par_b = sgu_par[:-1] + [sgu_out_g + rs_token[0, 0]]
    d_pu, d_pv, grads["sgu_ln_g"], grads["sgu_ln_b"], d_sgu_w, d_b_t, grads["sgu_out_g"] = row_bwd(
        "sgu_b", sgu_fn, sgu_in, sgu_par_b, [e_mat, e_t], [Tiled(d_ycat, C, 1)], tm_vjp,
        tiled_out_dtypes=[BF16, BF16])
    grads["sgu_w"] = d_sgu_w[None]
    grads["sgu_b"] = d_b_t[:, :HEADS].T[None]
    d_y, d_r1, d_k1, d_v1, d_g, grads["gn_w"], grads["gn_b"], d_r_k = row_bwd(
        "post_b", post_fn, post_in, post_par, [jm], [Tiled(d_ycat, C, 0)], tm_vjp)
    grads["r_k"] = d_r_k.reshape(r_k.shape)
    rs_state, rs_token = reduce_scatter_continue(rs_state, later, "rs2", d_y)
    d_r2, d_w, d_k2, d_a, d_b, d_v3 = wkv_bwd(r_, w_, k2, v3, an, b_, ckpt, to_cols(d_y + rs_token[0, 0]))
    d_v2 = from_cols(d_v3)
    pre_cts = [(Tiled(d_r1), Tiled(d_r2)), Tiled(d_w), (Tiled(d_k1), Tiled(d_k2)), (Tiled(d_v1), Tiled(d_v2)),
               Tiled(d_a), Tiled(d_b), Tiled(d_g)]
    d_ps_rkv, d_ps_lora, grads["w0"], d_w2p, grads["a0"], d_a2p, d_g2p, grads["k_k"], grads["k_a"] = row_bwd(
        "pre_b", pre_fn, [Tiled(ps_rkv), Tiled(ps_lora)], pre_par, [jm], pre_cts, tm_vjp)
    grads["w2"], grads["a2"], grads["g2"] = d_w2p[None, :64], d_a2p[None, 64:], d_g2p[None, :160]
    d_p_rkv, d_mu_rkv = shift_bwd(d_ps_rkv, p, mu_rkv, 0, "shift_rkv_b")
    d_p_lora, d_mu_lora = shift_bwd(d_ps_lora, p, mu_lora, P_LORA // LORA_W, "shift_lora_b")
    grads["shift_mu"] = jnp.concatenate([d_mu_rkv, d_mu_lora[:, :288]], axis=1)
    d_p = jnp.concatenate([d_p_rkv, d_pu, d_pv, d_p_lora], axis=1)
    d_h1 = mm(d_p, w_in_p, tb=True, name="d_h1")
    d_w_in_p = mm(h1, d_p, ta=True, name="d_w_in")
    d_w_in = jnp.concatenate([d_w_in_p[:, :3 * C], d_w_in_p[:, P_LORA:P_LORA + 288], d_w_in_p[:, 3 * C:P_LORA]],
                             axis=1).astype(BF16)
    grads["w_in"] = d_w_in.reshape(D, 4, D_IN // 4).transpose(1, 0, 2)
    dx, grads["norm1_g"] = row_bwd("norm1_b", rms_res_fn, [Tiled(x2d)], [norm1_g], [],
                                   [Tiled(d_h1), Tiled(dx_mid)], tm_row)

    small = [n for n in names if n not in big]
    small_shapes = [(1, 64, C) if n in ("w2", "a2") else (1, 160, C) if n == "g2" else args[n].shape for n in small]
    vec = _pack([grads[n].reshape(1, -1) for n in small], F32, SMALL_ROWS)
    w_in_state, w_in_token = reduce_scatter_begin([grads["w_in"]], ["w_in"], "rs1")
    small_state, small_token = all_reduce_begin(vec)
    total = {n: g[None] for n, g in zip(later, reduce_scatter_end(rs_state, later, "rs2", w_in_token + small_token))}
    loss = lax.psum(loss_part[0, 0], ("x", "y", "c"))
    delta, new_m, new_v = {}, {}, {}

    def adam_big(n):
        s = args[n].shape
        d_, m_, v__ = adamw(args[n][0], total[n][0], args["m_" + n][0], args["v_" + n][0], "adamw_" + n)
        delta[n], new_m[n], new_v[n] = d_.reshape(s), m_.reshape(s), v__.reshape(s)

    for n in later:
        adam_big(n)
    total["w_in"] = reduce_scatter_end(w_in_state, ["w_in"], "rs1", new_v["w_down"])[0][None]
    for n, g_sum in zip(small, _unpack(all_reduce_end(small_state, new_v["w_down"]), small_shapes)):
        total[n] = lax.dynamic_slice_in_dim(g_sum, chip * 256, 256, axis=2) if n in lora_names else g_sum
    adam_big("w_in")
    adam_rows = -(-sum(int(np.prod(args[n].shape)) for n in small) // (8 * C)) * 8
    flat = [_pack([src[n].reshape(1, -1) for n in small], F32, adam_rows)
            for src in (args, total, {n: args["m_" + n] for n in small}, {n: args["v_" + n] for n in small})]
    outs = adamw(*flat, "adamw_small")
    for res, o in zip((delta, new_m, new_v), outs):
        res.update(zip(small, _unpack(o, [args[n].shape for n in small])))
    return (loss, dx[None], *[total[n] for n in names], *[delta[n] for n in names],
            *[new_m[n] for n in names], *[new_v[n] for n in names])
```

```python
import functools

import numpy as np
import jax
import jax.numpy as jnp
from jax import lax
from jax.experimental import pallas as pl
from jax.experimental.pallas import tpu as pltpu

F32 = jnp.float32
BF16 = jnp.bfloat16

D = 2048
C = 1024
HEADS = 16
HEAD = 64
LANES = 128
P_W = 5632
P_LORA = 5120
LORA_W = 512
D_IN = 5408
D_FF = 5632
CHUNK = 128
RMS_EPS = 1e-6
LN_EPS = 1e-5
GN_EPS = 64e-5
L2_EPS = 1e-12
VMEM_LIMIT = 56 * 1024 * 1024


def _pcall(body, **kw):
    return pl.pallas_call(body, **kw)


def _cparams(sem):
    return pltpu.CompilerParams(dimension_semantics=sem, vmem_limit_bytes=VMEM_LIMIT)


def _tile(n, most):
    t = most
    while t > 8 and n % t:
        t //= 2
    assert n % t == 0, (n, most)
    return t


MM_TILE = (2048, 512, 2048)
MM_VMEM = 40 * 1024 * 1024


def _div_tile(n, most, quantum=LANES):
    for t in range(min(n, most) // quantum * quantum, 0, -quantum):
        if n % t == 0:
            return t
    raise ValueError((n, most, quantum))


def _mm_tiles(m, n, k, a_bytes, b_bytes, r_bytes, o_bytes):
    tm, tn, tk = _div_tile(m, MM_TILE[0]), _div_tile(n, MM_TILE[1]), _div_tile(k, MM_TILE[2])

    def need(tm, tk):
        return 2 * (tm * tk * a_bytes + tk * tn * b_bytes + tm * tn * (r_bytes + o_bytes)) + tm * tn * 4

    while need(tm, tk) > MM_VMEM:
        if tk >= tm and tk > 512:
            tk = _div_tile(k, tk - LANES)
        else:
            tm = _div_tile(m, tm - LANES)
    return tm, tn, tk


def mm(a, b, *, ta=False, tb=False, res=None, out_dtype=F32, name, shards=None):
    if shards is not None:
        return _mm_shards(a, b, res, out_dtype, name, shards)
    m, k = (a.shape[1], a.shape[0]) if ta else a.shape
    n = b.shape[0] if tb else b.shape[1]
    assert (b.shape[1] if tb else b.shape[0]) == k
    tm, tn, tk = _mm_tiles(m, n, k, a.dtype.itemsize, b.dtype.itemsize, 0 if res is None else res.dtype.itemsize,
                           jnp.dtype(out_dtype).itemsize)
    nk = k // tk
    dims = (((0 if ta else 1,), (1 if tb else 0,)), ((), ()))
    a_spec = pl.BlockSpec((tk, tm), lambda i, j, l: (l, i)) if ta else pl.BlockSpec((tm, tk), lambda i, j, l: (i, l))
    b_spec = pl.BlockSpec((tn, tk), lambda i, j, l: (j, l)) if tb else pl.BlockSpec((tk, tn), lambda i, j, l: (l, j))
    o_spec = pl.BlockSpec((tm, tn), lambda i, j, l: (i, j))
    return _mm_call(a, b, res, dims, (m // tm, n // tn, nk), a_spec, b_spec, o_spec, o_spec, (tm, tn),
                    jax.ShapeDtypeStruct((m, n), out_dtype), name)


def _mm_shards(a, b, res, out_dtype, name, shards):
    if shards == "b":
        (m, k), ns = a.shape, b.shape[2]
        tm, tk = _div_tile(m, 1024), _div_tile(k, 2048)
        grid, dims, acc = (m // tm, 4, k // tk), (((1,), (0,)), ((), ())), (tm, ns)
        a_spec = pl.BlockSpec((tm, tk), lambda i, s, l: (i, l))
        b_spec = pl.BlockSpec((None, tk, ns), lambda i, s, l: (s, l, 0))
        o_spec = pl.BlockSpec((tm, ns), lambda i, s, l: (i, s))
        out = jax.ShapeDtypeStruct((m, 4 * ns), out_dtype)
    elif shards == "bt":
        m, (_, n, ns) = a.shape[0], b.shape
        tm, tn = _div_tile(m, 2048), _div_tile(n, 512)
        grid, dims, acc = (m // tm, n // tn, 4), (((1,), (1,)), ((), ())), (tm, tn)
        a_spec = pl.BlockSpec((tm, ns), lambda i, j, s: (i, s))
        b_spec = pl.BlockSpec((None, tn, ns), lambda i, j, s: (s, j, 0))
        o_spec = pl.BlockSpec((tm, tn), lambda i, j, s: (i, j))
        out = jax.ShapeDtypeStruct((m, n), out_dtype)
    else:
        (t, m), ns = a.shape, b.shape[1] // 4
        tm, tk = _div_tile(m, 1024), _div_tile(t, 2048)
        grid, dims, acc = (m // tm, 4, t // tk), (((0,), (0,)), ((), ())), (tm, ns)
        a_spec = pl.BlockSpec((tk, tm), lambda i, s, l: (l, i))
        b_spec = pl.BlockSpec((tk, ns), lambda i, s, l: (l, s))
        o_spec = pl.BlockSpec((None, tm, ns), lambda i, s, l: (s, i, 0))
        out = jax.ShapeDtypeStruct((4, m, ns), out_dtype)
    return _mm_call(a, b, res, dims, grid, a_spec, b_spec, o_spec, o_spec, acc, out, name)


def _mm_call(a, b, res, dims, grid, a_spec, b_spec, r_spec, o_spec, acc_shape, out, name):
    nk = grid[2]
    out_dtype = out.dtype

    def body(*refs):
        a_ref, b_ref = refs[:2]
        r_ref = None if res is None else refs[2]
        o_ref = refs[2 + (res is not None)]

        def finish(acc):
            if r_ref is not None:
                acc = acc + r_ref[...].astype(F32)
            o_ref[...] = acc.astype(out_dtype)

        prod = lax.dot_general(a_ref[...].astype(BF16), b_ref[...].astype(BF16), dims, preferred_element_type=F32)
        if nk == 1:
            finish(prod)
            return
        acc_ref = refs[-1]
        kk = pl.program_id(2)

        @pl.when(kk == 0)
        def _():
            acc_ref[...] = prod

        @pl.when(kk > 0)
        def _():
            acc_ref[...] += prod

        @pl.when(kk == nk - 1)
        def _():
            finish(acc_ref[...])

    in_specs = [a_spec, b_spec] + ([r_spec] if res is not None else [])
    args = (a, b) + ((res,) if res is not None else ())
    return _pcall(
        body, name=name, grid=grid, in_specs=in_specs, out_specs=o_spec, out_shape=out,
        scratch_shapes=[] if nk == 1 else [pltpu.VMEM(acc_shape, F32)],
        compiler_params=_cparams(("parallel", "parallel", "arbitrary")),
    )(*args)


class Tiled:
    def __init__(self, arr, width=None, col=0):
        self.arr, self.width, self.col = arr, (arr.shape[1] if width is None else width), col

    def spec(self, tm):
        col = self.col
        return pl.BlockSpec((tm, self.width), lambda i: (i, col))


def _full_spec(p):
    nd = p.ndim
    return pl.BlockSpec(p.shape, lambda i: (0,) * nd)


def row_fwd(name, fn, tiled, params, consts, outs, tm):
    t = tiled[0].arr.shape[0]
    n_in = len(tiled) + len(params) + len(consts)

    def body(*refs):
        res = fn(*[r[...] for r in refs[:n_in]])
        for o_ref, r in zip(refs[n_in:], res):
            o_ref[...] = r.astype(o_ref.dtype)

    return _pcall(
        body, name=name, grid=(t // tm,),
        in_specs=[x.spec(tm) for x in tiled] + [_full_spec(p) for p in params + consts],
        out_specs=[pl.BlockSpec((tm, w), lambda i: (i, 0)) for w, _ in outs],
        out_shape=[jax.ShapeDtypeStruct((t, w), dt) for w, dt in outs],
        compiler_params=_cparams(("parallel",)),
    )(*[x.arr for x in tiled], *params, *consts)


def row_bwd(name, fn, tiled, params, consts, cts, tm, tiled_out_dtypes=None, bf16_copy_of=()):
    t = tiled[0].arr.shape[0]
    nt, npar, ncon = len(tiled), len(params), len(consts)
    cts = [c if isinstance(c, tuple) else (c,) for c in cts]
    flat_cts = [x for c in cts for x in c]
    tiled_out_dtypes = tiled_out_dtypes or [F32] * nt
    copies = [tiled[j] for j in bf16_copy_of]

    def body(*refs):
        n_in = nt + npar + ncon
        ins = [r[...].astype(F32) for r in refs[:nt + npar]]
        con = [r[...] for r in refs[nt + npar:n_in]]
        ct_refs = list(refs[n_in:n_in + len(flat_cts)])
        o = refs[n_in + len(flat_cts):]
        ct = []
        for c in cts:
            parts = [ct_refs.pop(0)[...].astype(F32) for _ in c]
            ct.append(functools.reduce(lambda p, q: p + q, parts))
        _, vjp = jax.vjp(lambda *a: fn(*a, *con), *ins)
        g = vjp(tuple(ct))
        for j in range(nt):
            o[j][...] = g[j].astype(o[j].dtype)
        for n, j in enumerate(bf16_copy_of):
            o[nt + npar + n][...] = g[j].astype(BF16)
        first = pl.program_id(0) == 0

        @pl.when(first)
        def _():
            for j in range(npar):
                o[nt + j][...] = g[nt + j]

        @pl.when(jnp.logical_not(first))
        def _():
            for j in range(npar):
                o[nt + j][...] += g[nt + j]

    return _pcall(
        body, name=name, grid=(t // tm,),
        in_specs=[x.spec(tm) for x in tiled] + [_full_spec(p) for p in params + consts]
        + [x.spec(tm) for x in flat_cts],
        out_specs=[pl.BlockSpec((tm, x.width), lambda i: (i, 0)) for x in tiled] + [_full_spec(p) for p in params]
        + [pl.BlockSpec((tm, x.width), lambda i: (i, 0)) for x in copies],
        out_shape=[jax.ShapeDtypeStruct((t, x.width), dt) for x, dt in zip(tiled, tiled_out_dtypes)]
        + [jax.ShapeDtypeStruct(p.shape, F32) for p in params]
        + [jax.ShapeDtypeStruct((t, x.width), BF16) for x in copies],
        compiler_params=_cparams(("arbitrary",)),
    )(*[x.arr for x in tiled], *params, *consts, *[x.arr for x in flat_cts])


def _split_dot(x, w):
    hi = x.astype(BF16)
    lo = (x - hi.astype(F32)).astype(BF16)
    return jnp.dot(hi, w, preferred_element_type=F32) + jnp.dot(lo, w, preferred_element_type=F32)


@jax.custom_vjp
def headsum(x, j):
    return _split_dot(x, j)


def _headsum_fwd(x, j):
    return _split_dot(x, j), j


def _headsum_bwd(j, ct):
    return _split_dot(ct, j), jnp.zeros_like(j)


headsum.defvjp(_headsum_fwd, _headsum_bwd)


def _bdot(x, w):
    return jnp.dot(x.astype(BF16), w.astype(BF16), preferred_element_type=F32)


def _sigmoid(x):
    return 1.0 / (1.0 + jnp.exp(-x))


def _softplus(x):
    return jnp.maximum(x, 0.0) + jnp.log(1.0 + jnp.exp(-jnp.abs(x)))


def rms_fn(x, g):
    return (x * lax.rsqrt(jnp.mean(x * x, axis=-1, keepdims=True) + RMS_EPS) * g,)


def rms_res_fn(x, g):
    return rms_fn(x, g)[0], x


def pre_fn(rkv, lora, w0, w2p, a0, a2p, g2p, k_k, k_a, jm):
    r, k, v = rkv[:, :C], rkv[:, C:2 * C], rkv[:, 2 * C:]
    zwa, zg = lora[:, :LANES], lora[:, LANES:LANES + 256]
    w_log = -_softplus(-(w0 + _bdot(jnp.tanh(zwa), w2p))) - 0.5
    decay = jnp.exp(-jnp.exp(w_log))
    a = _sigmoid(a0 + _bdot(zwa, a2p))
    g = _bdot(_sigmoid(zg), g2p)
    kk = k * k_k
    kk = kk / jnp.maximum(jnp.sqrt(headsum(kk * kk, jm)), L2_EPS)
    k2 = k * (1.0 + (a - 1.0) * k_a)
    return r, decay, k2, v, -kk, kk * a, g


def post_fn(y, r, k2, v, g, gn_w, gn_b, r_k, jm):
    mu = headsum(y, jm) * (1.0 / HEAD)
    yc = y - mu
    var = headsum(yc * yc, jm) * (1.0 / HEAD)
    yn = yc * lax.rsqrt(var + GN_EPS) * gn_w + gn_b
    bonus = headsum(r * k2 * r_k, jm) * v
    return ((yn + bonus) * g,)


def _gelu(x):
    return 0.5 * x * (1.0 + lax.erf(x * np.float32(1.0 / np.sqrt(2.0))))


@jax.custom_vjp
def expand_groups(b_t, e, e_t):
    return _split_dot(b_t, e)


def _expand_groups_fwd(b_t, e, e_t):
    return _split_dot(b_t, e), (e, e_t)


def _expand_groups_bwd(res, ct):
    e, e_t = res
    return _split_dot(ct, e_t), jnp.zeros_like(e), jnp.zeros_like(e_t)


expand_groups.defvjp(_expand_groups_fwd, _expand_groups_bwd)


def sgu_fn(pu, pv, ln_g, ln_b, w_s, b_t, out_g, e, e_t):
    rows = pu.shape[0]
    b_exp = expand_groups(b_t, e, e_t)
    u, v = _gelu(pu), _gelu(pv)
    mu = jnp.mean(v, axis=-1, keepdims=True)
    vc = v - mu
    var = jnp.mean(vc * vc, axis=-1, keepdims=True)
    v = vc * lax.rsqrt(var + LN_EPS) * ln_g + ln_b
    tri = lax.broadcasted_iota(jnp.int32, (CHUNK, CHUNK), 0) >= lax.broadcasted_iota(jnp.int32, (CHUNK, CHUNK), 1)
    left = lax.broadcasted_iota(jnp.int32, (CHUNK, LANES), 1) < HEAD
    chunks = []
    for c0 in range(0, rows, CHUNK):
        cols = []
        for gp in range(HEADS // 2):
            vp = v[c0:c0 + CHUNK, gp * LANES:(gp + 1) * LANES]
            wa = jnp.where(tri, w_s[2 * gp], 0.0)
            wb = jnp.where(tri, w_s[2 * gp + 1], 0.0)
            cols.append(jnp.where(left, _bdot(wa, vp), _bdot(wb, vp)))
        chunks.append(jnp.concatenate(cols, axis=1) + b_exp)
    s = jnp.concatenate(chunks, axis=0) if len(chunks) > 1 else chunks[0]
    y = u * s
    return (y * lax.rsqrt(jnp.mean(y * y, axis=-1, keepdims=True) + RMS_EPS) * out_g,)


def swiglu_fn(gate, up):
    return (gate * _sigmoid(gate) * up,)


SCAN_CHUNK = 32
NB = C // LANES
STEP_GROUP = 4


def to_cols(a):
    t = a.shape[0]
    return a.reshape(t // 8, LANES, HEAD).transpose(0, 2, 1)


def from_cols(a):
    return a.transpose(0, 2, 1).reshape(a.shape[0] * 8, C)


def _scan_consts():
    i = np.arange(LANES)
    spread = (i[:, None] % 2) == (i[None, :] // HEAD)
    pick = (i[:, None] // HEAD) == (i[None, :] % 2)
    j2 = (i[:, None] // HEAD) == (i[None, :] // HEAD)
    sel = (i[None, :] // 2) == (np.arange(8 * NB * HEAD)[:, None] // HEAD)
    return (jnp.asarray(spread, BF16), jnp.asarray(pick, BF16), jnp.asarray(np.concatenate([j2, j2], 0), BF16),
            jnp.asarray(sel, BF16))


def _stack(blocks):
    return jnp.concatenate(blocks, axis=0) if len(blocks) > 1 else blocks[0]


def _unstack(x, n):
    return [x[i * HEAD:(i + 1) * HEAD] for i in range(n)]


def _headsums(blocks, j2k, group):
    res = []
    for g0 in range(0, len(blocks), group):
        x = _stack(blocks[g0:g0 + group])
        hi = x.astype(BF16)
        lo = (x - hi.astype(F32)).astype(BF16)
        out = jnp.dot(jnp.concatenate([hi, lo], axis=1), j2k, preferred_element_type=F32)
        res += _unstack(out, len(blocks[g0:g0 + group]))
    return res


def _headsums_out(blocks, pick):
    return _unstack(jnp.dot(_stack(blocks).astype(BF16), pick, preferred_element_type=F32), len(blocks))


def _expand8(tile, sel, spread):
    lhs = jnp.tile(tile.astype(BF16), (8 * NB, 1)) * sel
    return _unstack(jnp.dot(lhs, spread, preferred_element_type=F32), 8 * NB)


def _collapse(tile, blocks, first, lane_pair):
    for n, blk in enumerate(blocks):
        tile = jnp.where(jnp.tile(lane_pair == first + n, (HEAD // 8, 1)), blk, tile)
    return tile


def _row(tile, j, cb):
    return jnp.broadcast_to(tile[j:j + 1, cb * LANES:(cb + 1) * LANES], (HEAD, LANES))


def _scan_step(s_ref, rows, vexp, j2k, j, hist=None):
    w_t, k_t, a_t, b_t, r_t = rows
    s = [s_ref[:, cb * LANES:(cb + 1) * LANES] for cb in range(NB)]
    sab = _headsums([s[cb] * _row(a_t, j, cb) for cb in range(NB)], j2k, STEP_GROUP)
    out = []
    for cb in range(NB):
        sl = slice(cb * LANES, (cb + 1) * LANES)
        s_new = s[cb] * _row(w_t, j, cb) + sab[cb] * _row(b_t, j, cb) + vexp[j * NB + cb] * _row(k_t, j, cb)
        s_ref[:, sl] = s_new
        if hist is None:
            out.append(s_new * _row(r_t, j, cb))
        else:
            s_hist, sab_hist, idx = hist
            s_hist[idx + 1, :, sl] = s_new
            sab_hist[idx, :, sl] = sab[cb]
    return out


def wkv_fwd(r, w, k, v3, a, b):
    t = r.shape[0]
    nc = t // SCAN_CHUNK
    n8 = SCAN_CHUNK // 8

    def body(r_ref, w_ref, k_ref, a_ref, b_ref, v3_ref, spread_ref, pick_ref, j2k_ref, sel_ref, y3_ref, ck_ref,
             s_ref):
        @pl.when(pl.program_id(0) == 0)
        def _():
            s_ref[...] = jnp.zeros_like(s_ref)

        ck_ref[0] = s_ref[...]
        spread, pick, j2k, sel = spread_ref[...], pick_ref[...], j2k_ref[...], sel_ref[...]
        lane_pair = lax.broadcasted_iota(jnp.int32, (8, LANES), 1) // 2

        def t8_body(t8, carry):
            row0 = pl.multiple_of(t8 * 8, 8)
            rows = [ref[pl.ds(row0, 8), :] for ref in (w_ref, k_ref, a_ref, b_ref, r_ref)]
            vexp = _expand8(v3_ref[t8], sel, spread)
            y3 = jnp.zeros((HEAD, LANES), F32)
            for j in range(8):
                y3 = _collapse(y3, _headsums_out(_scan_step(s_ref, rows, vexp, j2k, j), pick), j * NB, lane_pair)
            y3_ref[t8] = y3
            return carry

        lax.fori_loop(0, n8, t8_body, 0)

    row_spec = pl.BlockSpec((SCAN_CHUNK, C), lambda i: (i, 0))
    col_spec = pl.BlockSpec((n8, HEAD, LANES), lambda i: (i, 0, 0))
    consts = _scan_consts()
    return _pcall(
        body, name="wkv_fwd", grid=(nc,),
        in_specs=[row_spec] * 5 + [col_spec] + [pl.BlockSpec(c.shape, lambda i: (0, 0)) for c in consts],
        out_specs=[col_spec, pl.BlockSpec((1, HEAD, C), lambda i: (i, 0, 0))],
        out_shape=[jax.ShapeDtypeStruct((t // 8, HEAD, LANES), F32), jax.ShapeDtypeStruct((nc, HEAD, C), F32)],
        scratch_shapes=[pltpu.VMEM((HEAD, C), F32)],
        compiler_params=_cparams(("arbitrary",)),
    )(r, w, k, a, b, v3, *consts)


def wkv_bwd(r, w, k, v3, a, b, ckpt, dy3):
    t = r.shape[0]
    nc = t // SCAN_CHUNK
    n8 = SCAN_CHUNK // 8

    def body(r_ref, w_ref, k_ref, a_ref, b_ref, v3_ref, dy3_ref, ck_ref, spread_ref, pick_ref, j2k_ref, sel_ref,
             dr_ref, dw_ref, dk_ref, da_ref, db_ref, dv3_ref, s_ref, g_ref, s_hist, sab_hist):
        @pl.when(pl.program_id(0) == 0)
        def _():
            g_ref[...] = jnp.zeros_like(g_ref)

        spread, pick, j2k, sel = spread_ref[...], pick_ref[...], j2k_ref[...], sel_ref[...]
        lane_pair = lax.broadcasted_iota(jnp.int32, (8, LANES), 1) // 2
        sub = lax.broadcasted_iota(jnp.int32, (8, LANES), 0)
        s_ref[...] = ck_ref[0]
        s_hist[0] = ck_ref[0]

        def redo(t8, carry):
            row0 = pl.multiple_of(t8 * 8, 8)
            rows = [ref[pl.ds(row0, 8), :] for ref in (w_ref, k_ref, a_ref, b_ref, r_ref)]
            vexp = _expand8(v3_ref[t8], sel, spread)
            for j in range(8):
                _scan_step(s_ref, rows, vexp, j2k, j, hist=(s_hist, sab_hist, t8 * 8 + j))
            return carry

        lax.fori_loop(0, n8, redo, 0)

        def back(q, carry):
            t8 = n8 - 1 - q
            row0 = pl.multiple_of(t8 * 8, 8)
            w_t, k_t, a_t, b_t, r_t = [ref[pl.ds(row0, 8), :] for ref in (w_ref, k_ref, a_ref, b_ref, r_ref)]
            vexp = _expand8(v3_ref[t8], sel, spread)
            dyexp = _expand8(dy3_ref[t8], sel, spread)
            dv3 = jnp.zeros((HEAD, LANES), F32)
            tiles = [[jnp.zeros((8, LANES), F32) for _ in range(NB)] for _ in range(5)]
            for j in range(7, -1, -1):
                idx = t8 * 8 + j
                g = [g_ref[:, cb * LANES:(cb + 1) * LANES] + dyexp[j * NB + cb] * _row(r_t, j, cb)
                     for cb in range(NB)]
                dsab = _headsums([g[cb] * _row(b_t, j, cb) for cb in range(NB)], j2k, STEP_GROUP)
                dv3 = _collapse(dv3, _headsums_out([g[cb] * _row(k_t, j, cb) for cb in range(NB)], pick), j * NB,
                                lane_pair)
                for cb in range(NB):
                    sl = slice(cb * LANES, (cb + 1) * LANES)
                    s_new = s_hist[idx + 1, :, sl]
                    s_old = s_hist[idx, :, sl]
                    sab = sab_hist[idx, :, sl]
                    sums = (s_new * dyexp[j * NB + cb], g[cb] * s_old, g[cb] * vexp[j * NB + cb],
                            s_old * dsab[cb], g[cb] * sab)
                    for n, prod in enumerate(sums):
                        rowsum = jnp.broadcast_to(jnp.sum(prod, axis=0, keepdims=True), (8, LANES))
                        tiles[n][cb] = jnp.where(sub == j, rowsum, tiles[n][cb])
                    g_ref[:, sl] = g[cb] * _row(w_t, j, cb) + dsab[cb] * _row(a_t, j, cb)
            dv3_ref[t8] = dv3
            for n, ref in enumerate((dr_ref, dw_ref, dk_ref, da_ref, db_ref)):
                for cb in range(NB):
                    ref[pl.ds(row0, 8), cb * LANES:(cb + 1) * LANES] = tiles[n][cb]
            return carry

        lax.fori_loop(0, n8, back, 0)

    row_spec = pl.BlockSpec((SCAN_CHUNK, C), lambda i: (nc - 1 - i, 0))
    col_spec = pl.BlockSpec((n8, HEAD, LANES), lambda i: (nc - 1 - i, 0, 0))
    consts = _scan_consts()
    return _pcall(
        body, name="wkv_bwd", grid=(nc,),
        in_specs=[row_spec] * 5 + [col_spec, col_spec, pl.BlockSpec((1, HEAD, C), lambda i: (nc - 1 - i, 0, 0))]
        + [pl.BlockSpec(c.shape, lambda i: (0, 0)) for c in consts],
        out_specs=[row_spec] * 5 + [col_spec],
        out_shape=[jax.ShapeDtypeStruct((t, C), F32)] * 5 + [jax.ShapeDtypeStruct((t // 8, HEAD, LANES), F32)],
        scratch_shapes=[pltpu.VMEM((HEAD, C), F32), pltpu.VMEM((HEAD, C), F32),
                        pltpu.VMEM((SCAN_CHUNK + 1, HEAD, C), F32), pltpu.VMEM((SCAN_CHUNK, HEAD, C), F32)],
        compiler_params=_cparams(("arbitrary",)),
    )(r, w, k, a, b, v3, dy3, ckpt, *consts)


def _prev_rows(cur, before, first_tile):
    last = jnp.where(first_tile, 0.0, before[7:8, :])
    row = lax.broadcasted_iota(jnp.int32, cur.shape, 0)
    return jnp.where(row == 0, last, pltpu.roll(cur, 1, 0))


def shift_fwd(p, mu, col, name):
    t, width = p.shape[0], mu.shape[1]
    tm = _tile(t, 256)

    def body(p_ref, before_ref, mu_ref, o_ref):
        cur = p_ref[...]
        prev = _prev_rows(cur, before_ref[...], pl.program_id(0) == 0)
        o_ref[...] = cur + (prev - cur) * mu_ref[...]

    return _pcall(
        body, name=name, grid=(t // tm,),
        in_specs=[pl.BlockSpec((tm, width), lambda i: (i, col)),
                  pl.BlockSpec((8, width), lambda i: (jnp.maximum(i * (tm // 8) - 1, 0), col)),
                  pl.BlockSpec((1, width), lambda i: (0, 0))],
        out_specs=pl.BlockSpec((tm, width), lambda i: (i, 0)),
        out_shape=jax.ShapeDtypeStruct((t, width), F32),
        compiler_params=_cparams(("parallel",)),
    )(p, p, mu)


def shift_bwd(dps, p, mu, col, name):
    t, width = dps.shape
    tm = _tile(t, 256)
    nt = t // tm

    def body(d_ref, after_ref, p_ref, before_ref, mu_ref, dp_ref, dmu_ref):
        i = pl.program_id(0)
        d, cur, mu_v = d_ref[...], p_ref[...], mu_ref[...]
        row = lax.broadcasted_iota(jnp.int32, d.shape, 0)
        first_after = jnp.where(i == nt - 1, 0.0, after_ref[0:1, :])
        nxt = jnp.where(row == tm - 1, first_after, pltpu.roll(d, tm - 1, 0))
        dp_ref[...] = (d * (1.0 - mu_v) + nxt * mu_v).astype(BF16)
        prev = _prev_rows(cur, before_ref[...], i == 0)
        part = jnp.sum(d * (prev - cur), axis=0, keepdims=True)

        @pl.when(i == 0)
        def _():
            dmu_ref[...] = part

        @pl.when(i > 0)
        def _():
            dmu_ref[...] += part

    return _pcall(
        body, name=name, grid=(nt,),
        in_specs=[pl.BlockSpec((tm, width), lambda i: (i, 0)),
                  pl.BlockSpec((8, width), lambda i: (jnp.minimum((i + 1) * (tm // 8), t // 8 - 1), 0)),
                  pl.BlockSpec((tm, width), lambda i: (i, col)),
                  pl.BlockSpec((8, width), lambda i: (jnp.maximum(i * (tm // 8) - 1, 0), col)),
                  pl.BlockSpec((1, width), lambda i: (0, 0))],
        out_specs=[pl.BlockSpec((tm, width), lambda i: (i, 0)), pl.BlockSpec((1, width), lambda i: (0, 0))],
        out_shape=[jax.ShapeDtypeStruct((t, width), BF16), jax.ShapeDtypeStruct((1, width), F32)],
        compiler_params=_cparams(("arbitrary",)),
    )(dps, dps, p, p, mu)


def loss_head(x3, tgt, g):
    t, d = x3.shape
    tm = _tile(t, 256)

    def body(x_ref, t_ref, g_ref, loss_ref, dx_ref, dg_ref, dxb_ref):
        (y,), vjp = jax.vjp(lambda a, b: rms_fn(a, b), x_ref[...], g_ref[...])
        diff = y - t_ref[...]
        part = 0.5 * jnp.sum(jnp.mean(diff * diff, axis=-1, keepdims=True), axis=0, keepdims=True)
        dx, dg = vjp((diff * (1.0 / d),))
        dx_ref[...] = dx
        dxb_ref[...] = dx.astype(BF16)
        part = jnp.broadcast_to(part, (1, LANES))
        first = pl.program_id(0) == 0

        @pl.when(first)
        def _():
            loss_ref[...] = part
            dg_ref[...] = dg

        @pl.when(jnp.logical_not(first))
        def _():
            loss_ref[...] += part
            dg_ref[...] += dg

    row = pl.BlockSpec((tm, d), lambda i: (i, 0))
    vec = pl.BlockSpec((1, d), lambda i: (0, 0))
    return _pcall(
        body, name="loss_head", grid=(t // tm,), in_specs=[row, row, vec],
        out_specs=[pl.BlockSpec((1, LANES), lambda i: (0, 0)), row, vec, row],
        out_shape=[jax.ShapeDtypeStruct((1, LANES), F32), jax.ShapeDtypeStruct((t, d), F32),
                   jax.ShapeDtypeStruct((1, d), F32), jax.ShapeDtypeStruct((t, d), BF16)],
        compiler_params=_cparams(("arbitrary",)),
    )(x3, tgt, g)


ADAM_LR, ADAM_B1, ADAM_B2, ADAM_EPS, ADAM_WD, ADAM_STEP = 0.001, 0.9, 0.999, 1e-08, 0.01, 10


def adamw(w, g, m, v, name):
    rows, width = w.shape
    tr = _tile(rows, 256)

    def body(w_ref, g_ref, m_ref, v_ref, d_ref, nm_ref, nv_ref):
        gv = g_ref[...]
        m_new = ADAM_B1 * m_ref[...] + (1.0 - ADAM_B1) * gv
        v_new = ADAM_B2 * v_ref[...] + (1.0 - ADAM_B2) * (gv * gv)
        m_hat = m_new / (1.0 - ADAM_B1 ** ADAM_STEP)
        v_hat = v_new / (1.0 - ADAM_B2 ** ADAM_STEP)
        d_ref[...] = -ADAM_LR * (m_hat / (jnp.sqrt(v_hat) + ADAM_EPS) + ADAM_WD * w_ref[...])
        nm_ref[...] = m_new
        nv_ref[...] = v_new

    spec = pl.BlockSpec((tr, width), lambda i: (i, 0))
    return _pcall(
        body, name=name, grid=(rows // tr,), in_specs=[spec] * 4, out_specs=[spec] * 3,
        out_shape=[jax.ShapeDtypeStruct((rows, width), F32)] * 3,
        compiler_params=_cparams(("parallel",)),
    )(w, g, m, v)


def sum_slots(q, name):
    n, rows, width = q.shape
    tr = _tile(rows, 512)

    def body(*refs):
        acc = refs[0][...].astype(F32)
        for r in refs[1:n]:
            acc = acc + r[...].astype(F32)
        refs[n][...] = acc

    specs = [pl.BlockSpec((None, tr, width), functools.partial(lambda s, i: (s, i, 0), s)) for s in range(n)]
    return _pcall(body, name=name, grid=(rows // tr,), in_specs=specs,
                  out_specs=pl.BlockSpec((tr, width), lambda i: (i, 0)),
                  out_shape=jax.ShapeDtypeStruct((rows, width), F32),
                  compiler_params=_cparams(("parallel",)))(*([q] * n))


MESH = pl.DeviceIdType.MESH
ANY_SPEC = pl.BlockSpec(memory_space=pl.ANY)


def _coords():
    return lax.axis_index("x"), lax.axis_index("y"), lax.axis_index("c")


def _other_chips(x, y):
    return [(1 - x, y), (x, 1 - y), (1 - x, 1 - y)]


def push(name, srcs, n_dst, plan_fn):
    n_arr = len(srcs)
    n_send = len(plan_fn(0, 0, 0))

    def body(*refs):
        src_refs, dst_refs = refs[:n_arr], refs[n_arr:2 * n_arr]
        send_sems, recv_sems = refs[2 * n_arr:]
        sends = plan_fn(*_coords())
        out = []
        for q, (src_ref, dst_ref) in enumerate(zip(src_refs, dst_refs)):
            out += [pltpu.make_async_remote_copy(src_ref.at[si], dst_ref.at[di], send_sems.at[q * n_send + k],
                                                 recv_sems.at[q * n_send + k], device_id=peer, device_id_type=MESH)
                    for k, (si, peer, di, _) in enumerate(sends)]
        for cp in out:
            cp.start()
        for q, (src_ref, dst_ref) in enumerate(zip(src_refs, dst_refs)):
            for k, (si, peer, _, ri) in enumerate(sends):
                pltpu.make_async_remote_copy(src_ref.at[si], dst_ref.at[ri], send_sems.at[q * n_send + k],
                                             recv_sems.at[q * n_send + k], device_id=peer,
                                             device_id_type=MESH).wait_recv()
        for cp in out:
            cp.wait_send()

    return _pcall(
        body, name=name, in_specs=[ANY_SPEC] * n_arr, out_specs=[ANY_SPEC] * n_arr,
        out_shape=[jax.ShapeDtypeStruct((n_dst,) + s.shape[1:], s.dtype) for s in srcs],
        scratch_shapes=[pltpu.SemaphoreType.DMA((n_arr * n_send,)), pltpu.SemaphoreType.DMA((n_arr * n_send,))],
    )(*srcs)


def plan_sibling_quarters(x, y, c):
    return [(2 * s + (1 - c), (x, y, 1 - c), s, s) for s in range(4)]


def plan_chips_by_shard(x, y, c):
    me = 2 * x + y
    return [(2 * px + py, (px, py, c), me, 2 * px + py) for px, py in _other_chips(x, y)]


def plan_chips_half(x, y, c):
    me = 2 * x + y
    return [(c, (px, py, c), me, 2 * px + py) for px, py in _other_chips(x, y)]


def swap_halves(name, bufs):
    n_arr = len(bufs)

    def body(*refs):
        buf_refs = refs[n_arr:2 * n_arr]
        send_sems, recv_sems = refs[2 * n_arr:]
        x, y, c = _coords()
        out = [pltpu.make_async_remote_copy(b.at[c], b.at[c], send_sems.at[q], recv_sems.at[q],
                                            device_id=(x, y, 1 - c), device_id_type=MESH)
               for q, b in enumerate(buf_refs)]
        for cp in out:
            cp.start()
        for q, b in enumerate(buf_refs):
            pltpu.make_async_remote_copy(b.at[1 - c], b.at[1 - c], send_sems.at[q], recv_sems.at[q],
                                         device_id=(x, y, 1 - c), device_id_type=MESH).wait_recv()
        for cp in out:
            cp.wait_send()

    return _pcall(
        body, name=name, in_specs=[ANY_SPEC] * n_arr, out_specs=[ANY_SPEC] * n_arr,
        out_shape=[jax.ShapeDtypeStruct(b.shape, b.dtype) for b in bufs],
        input_output_aliases={q: q for q in range(n_arr)},
        scratch_shapes=[pltpu.SemaphoreType.DMA((n_arr,)), pltpu.SemaphoreType.DMA((n_arr,))],
    )(*bufs)


HBM_SPEC = pl.BlockSpec(memory_space=pltpu.HBM)
SEM_SPEC = pl.BlockSpec(memory_space=pltpu.SEMAPHORE)
DATAFLOW = pltpu.SideEffectType.DATAFLOW_SIDE_EFFECTING


def _in_hbm(a):
    return pltpu.with_memory_space_constraint(a, pltpu.HBM)


def push_start(name, srcs, lands, plan_fn, after):
    n_arr = len(srcs)
    n_send = len(plan_fn(0, 0, 0))
    arrays = list(srcs) + ([] if lands is None else list(lands))
    n_all = len(arrays)

    def body(*refs):
        src_refs, land_refs = refs[:n_arr], refs[n_all - n_arr:n_all]
        send_sems, recv_sems = refs[n_all + 1:n_all + 3]
        token = refs[-1]
        sends = plan_fn(*_coords())
        for q, (src_ref, land_ref) in enumerate(zip(src_refs, land_refs)):
            for k, (si, peer, di, _) in enumerate(sends):
                pltpu.make_async_remote_copy(src_ref.at[si], land_ref.at[di], send_sems.at[q * n_send + k],
                                             recv_sems.at[q * n_send + k], device_id=peer,
                                             device_id_type=MESH).start()
        token[...] = jnp.zeros_like(token)

    sems = pltpu.SemaphoreType.DMA((n_arr * n_send,))
    out = _pcall(
        body, name=name,
        in_specs=[HBM_SPEC] * n_all + [ANY_SPEC],
        out_specs=[SEM_SPEC, SEM_SPEC] + [HBM_SPEC] * n_all + [pl.BlockSpec(memory_space=pltpu.VMEM)],
        out_shape=[sems, sems] + [pltpu.HBM(a.shape, a.dtype) for a in arrays]
        + [jax.ShapeDtypeStruct((8, LANES), F32)],
        input_output_aliases={q: 2 + q for q in range(n_all)},
        compiler_params=pltpu.CompilerParams(has_side_effects=DATAFLOW),
    )(*[_in_hbm(a) for a in arrays], after)
    return out[:-1], out[-1]


def push_wait(name, state, plan_fn, after, in_place=False, both=False):
    arrays = list(state[2:])
    n_all = len(arrays)
    n_arr = n_all if in_place else n_all // 2
    n_send = len(plan_fn(0, 0, 0))
    send_sems, recv_sems = state[:2]

    def body(*refs):
        src_refs, land_refs = refs[:n_arr], refs[n_all - n_arr:n_all]
        send_ref, recv_ref = refs[n_all:n_all + 2]
        sends = plan_fn(*_coords())
        for q, (src_ref, land_ref) in enumerate(zip(src_refs, land_refs)):
            for k, (si, peer, _, ri) in enumerate(sends):
                cp = pltpu.make_async_remote_copy(src_ref.at[si], land_ref.at[ri], send_ref.at[q * n_send + k],
                                                  recv_ref.at[q * n_send + k], device_id=peer, device_id_type=MESH)
                cp.wait_send()
                cp.wait_recv()

    out = _pcall(
        body, name=name,
        in_specs=[HBM_SPEC] * n_all + [SEM_SPEC, SEM_SPEC, ANY_SPEC],
        out_specs=[HBM_SPEC] * n_all,
        out_shape=[pltpu.HBM(a.shape, a.dtype) for a in arrays],
        input_output_aliases={q: q for q in range(n_all)},
        compiler_params=pltpu.CompilerParams(has_side_effects=DATAFLOW),
    )(*arrays, send_sems, recv_sems, after)
    return list(out) if both else list(out[n_all - n_arr:])


def plan_chips_gather(x, y, c):
    me = 2 * x + y
    return [(c, (px, py, c), 2 * me + c, 2 * (2 * px + py) + c) for px, py in _other_chips(x, y)]


def plan_forward(x, y, c):
    return [(2 * (2 * px + py) + c, (x, y, 1 - c), 2 * (2 * px + py) + c, 2 * (2 * px + py) + (1 - c))
            for px, py in _other_chips(x, y)]


def add_kept(g8, other, name):
    _, rows, cols = other.shape
    tr = _tile(rows, 512)

    def body(c_ref, a_ref, b_ref, o_ref):
        o_ref[...] = (a_ref[...].astype(F32) + b_ref[...].astype(F32)).astype(BF16)

    spec = pl.BlockSpec((None, tr, cols), lambda s, i, c_ref: (s, i, 0))
    grid_spec = pltpu.PrefetchScalarGridSpec(
        num_scalar_prefetch=1, grid=(4, rows // tr),
        in_specs=[pl.BlockSpec((None, tr, cols), lambda s, i, c_ref: (2 * s + c_ref[0], i, 0)), spec],
        out_specs=spec)
    return _pcall(body, name=name, grid_spec=grid_spec, out_shape=jax.ShapeDtypeStruct((4, rows, cols), BF16),
                  compiler_params=_cparams(("parallel", "parallel")))(
        lax.axis_index("c").astype(jnp.int32).reshape(1), g8, other)


def reduce_scatter_begin(grads, names, tag):
    g8 = [g.reshape(8, g.shape[1] // 2, g.shape[2]) for g in grads]
    return _chip_sums_start(g8, push(tag + "_sibling", g8, 4, plan_sibling_quarters), names, tag)


def _chip_sums_start(g8, from_sibling, names, tag):
    chip_sum = [add_kept(a, b, tag + "_add_" + n) for a, b, n in zip(g8, from_sibling, names)]
    return push_start(tag + "_start", chip_sum, chip_sum, plan_chips_by_shard, chip_sum[-1])


def sibling_exchange_begin(grads, tag):
    g8 = [g.reshape(8, g.shape[1] // 2, g.shape[2]) for g in grads]
    lands = [lax.empty((4,) + g.shape[1:], g.dtype) for g in g8]
    return push_start(tag + "_sib_start", g8, lands, plan_sibling_quarters, g8[-1])


def reduce_scatter_continue(state, names, tag, after):
    n = len(names)
    arrays = push_wait(tag + "_sib_wait", state, plan_sibling_quarters, after, both=True)
    return _chip_sums_start(arrays[:n], arrays[n:], names, tag)


def reduce_scatter_end(state, names, tag, after):
    by_chip = push_wait(tag + "_wait", state, plan_chips_by_shard, after)
    total_half = [sum_slots(q, tag + "_sum_" + n) for q, n in zip(by_chip, names)]
    both = swap_halves(tag + "_halves", [jnp.broadcast_to(t[None], (2,) + t.shape) for t in total_half])
    return [b.reshape(2 * b.shape[1], b.shape[2]) for b in both]


def _landing(mine):
    return [jnp.broadcast_to(m[None], (4,) + m.shape).reshape((8,) + m.shape[1:]) for m in mine]


def gather_begin(tag, mine, after, lands=None):
    return push_start(tag + "_start", mine, _landing(mine) if lands is None else lands, plan_chips_gather, after)


def gather_forward(tag, state, after):
    lands = push_wait(tag + "_wait", state, plan_chips_gather, after)
    return push_start(tag + "_fwd_start", lands, None, plan_forward, lands[0])


def gather_end(tag, state, after):
    bufs = push_wait(tag + "_fwd_wait", state, plan_forward, after, in_place=True)
    return [g.reshape(4, 2 * g.shape[1], g.shape[2]) for g in bufs]


def _pack(parts, dtype, rows):
    flat = jnp.concatenate([p.astype(dtype).reshape(-1) for p in parts])
    return jnp.pad(flat, (0, rows * C - flat.shape[0])).reshape(rows, C)


def _unpack(flat, shapes):
    flat = flat.reshape(-1)
    out, at = [], 0
    for s in shapes:
        n = int(np.prod(s))
        out.append(flat[at:at + n].reshape(s))
        at += n
    return out


def all_reduce_begin(vec):
    rows, width = vec.shape
    (pair,) = swap_halves("ar_sibling", [jnp.broadcast_to(vec[None], (2, rows, width))])
    chip_sum = sum_slots(pair, "ar_sum_sibling").reshape(2, rows // 2, width)
    mine = lax.dynamic_index_in_dim(chip_sum, lax.axis_index("c"), axis=0, keepdims=True)
    return push_start("ar_start", [chip_sum], [jnp.broadcast_to(mine, (4, rows // 2, width))], plan_chips_half,
                      chip_sum)


def all_reduce_end(state, after):
    (by_chip,) = push_wait("ar_wait", state, plan_chips_half, after)
    total_half = sum_slots(by_chip, "ar_sum_chips")
    (both,) = swap_halves("ar_halves", [jnp.broadcast_to(total_half[None], (2,) + total_half.shape)])
    return both.reshape(2 * total_half.shape[0], total_half.shape[1])


SMALL_ROWS = 576


def kernel(x, norm1_g, w_in, shift_mu, w0, w2, a0, a2, g2, k_k, k_a, r_k, gn_w, gn_b, sgu_ln_g, sgu_ln_b, sgu_w, sgu_b, sgu_out_g, w_out, norm2_g, w_gate, w_up, w_down, final_g, loss_target, m_norm1_g, m_w_in, m_shift_mu, m_w0, m_w2, m_a0, m_a2, m_g2, m_k_k, m_k_a, m_r_k, m_gn_w, m_gn_b, m_sgu_ln_g, m_sgu_ln_b, m_sgu_w, m_sgu_b, m_sgu_out_g, m_w_out, m_norm2_g, m_w_gate, m_w_up, m_w_down, m_final_g, v_norm1_g, v_w_in, v_shift_mu, v_w0, v_w2, v_a0, v_a2, v_g2, v_k_k, v_k_a, v_r_k, v_gn_w, v_gn_b, v_sgu_ln_g, v_sgu_ln_b, v_sgu_w, v_sgu_b, v_sgu_out_g, v_w_out, v_norm2_g, v_w_gate, v_w_up, v_w_down, v_final_g):
    args = dict(locals())
    names = ["norm1_g", "w_in", "shift_mu", "w0", "w2", "a0", "a2", "g2", "k_k", "k_a", "r_k", "gn_w", "gn_b",
             "sgu_ln_g", "sgu_ln_b", "sgu_w", "sgu_b", "sgu_out_g", "w_out", "norm2_g", "w_gate", "w_up", "w_down",
             "final_g"]
    big = ["w_in", "w_out", "w_gate", "w_up", "w_down"]
    xi, yi = lax.axis_index("x"), lax.axis_index("y")
    chip = 2 * xi + yi
    x2d, tgt = x[0], loss_target[0]
    t = x2d.shape[0]

    lora_names = ["w2", "a2", "g2"]
    first, later = ["w_in"] + lora_names, ["w_out", "w_gate", "w_up", "w_down"]

    def halves(n, token=None):
        m = args[n][0] if token is None else args[n][0] + token[0, 0]
        m = m.astype(BF16)
        return m.reshape(2, m.shape[0] // 2, m.shape[1])

    first_state, first_token = gather_begin("g0", [halves(n) for n in first], norm1_g)
    later_mine = [halves(n, first_token) for n in later]
    later_lands = _landing(later_mine)
    prepared = functools.reduce(lambda p, q: p + q, [a[0, 0, :1].astype(F32) for a in later_lands])
    first_state, first_token = gather_forward("g0", first_state, prepared)
    got = gather_end("g0", first_state, first_token)
    later_state, later_token = gather_begin("g1", later_mine, got[0], later_lands)
    norm1_g_fwd = norm1_g + later_token[0, 0]
    full = {n: blk.transpose(1, 0, 2).reshape(blk.shape[1], 4 * blk.shape[2]) for n, blk in zip(first, got)}
    w_in_f = full["w_in"]
    w_in_p = jnp.concatenate([w_in_f[:, :3 * C], w_in_f[:, 3360:], w_in_f[:, 3 * C:3360],
                              jnp.zeros((D, P_W - D_IN), BF16)], axis=1)
    mu_rkv = shift_mu[:, :3 * C]
    mu_lora = jnp.pad(shift_mu[:, 3 * C:], ((0, 0), (0, LORA_W - 288)))
    w2p = jnp.pad(full["w2"], ((0, 64), (0, 0)))
    a2p = jnp.pad(full["a2"], ((64, 0), (0, 0)))
    g2p = jnp.pad(full["g2"], ((0, 96), (0, 0)))
    ii = np.arange(C)
    jm = jnp.asarray((ii[:, None] // HEAD) == (ii[None, :] // HEAD), BF16)
    e_np = (np.arange(LANES)[:, None] == (ii[None, :] // HEAD))
    e_mat, e_t = jnp.asarray(e_np, BF16), jnp.asarray(e_np.T, BF16)
    b_t = jnp.pad(sgu_b[0].T, ((0, 0), (0, LANES - HEADS)))
    r_k_flat = r_k.reshape(1, C)
    tm_row = _tile(t, 256)
    tm_vjp = _tile(t, 128)

    (h1,) = row_fwd("norm1", rms_fn, [Tiled(x2d)], [norm1_g_fwd], [], [(D, BF16)], tm_row)
    p = mm(h1, w_in_p, name="proj_in")
    ps_rkv = shift_fwd(p, mu_rkv, 0, "shift_rkv")
    ps_lora = shift_fwd(p, mu_lora, P_LORA // LORA_W, "shift_lora")
    pre_par = [w0, w2p, a0, a2p, g2p, k_k, k_a]
    r_, w_, k2, v_, an, b_, g_ = row_fwd("pre", pre_fn, [Tiled(ps_rkv), Tiled(ps_lora)], pre_par, [jm],
                                         [(C, F32)] * 7, tm_vjp)
    v3 = to_cols(v_)
    y3, ckpt = wkv_fwd(r_, w_, k2, v3, an, b_)
    later_state, later_token = gather_forward("g1", later_state, y3)
    y_ = from_cols(y3 + later_token[0, 0])
    post_in = [Tiled(z) for z in (y_, r_, k2, v_, g_)]
    post_par = [gn_w, gn_b, r_k_flat]
    (y_rwkv,) = row_fwd("post", post_fn, post_in, post_par, [jm], [(C, BF16)], tm_vjp)
    sgu_in = [Tiled(p, C, 3), Tiled(p, C, 4)]
    sgu_par = [sgu_ln_g, sgu_ln_b, sgu_w[0], b_t, sgu_out_g]
    (y_sgu,) = row_fwd("sgu", sgu_fn, sgu_in, sgu_par, [e_mat, e_t], [(C, BF16)], tm_vjp)
    y_cat = jnp.concatenate([y_rwkv, y_sgu], axis=1)
    for n, blk in zip(later, gather_end("g1", later_state, y_cat)):
        full[n] = blk if n in ("w_gate", "w_up") else blk.reshape(4 * blk.shape[1], blk.shape[2])
    x_mid = mm(y_cat, full["w_out"], res=x2d, name="proj_out")
    (h2,) = row_fwd("norm2", rms_fn, [Tiled(x_mid)], [norm2_g], [], [(D, BF16)], tm_row)
    gate = mm(h2, full["w_gate"], shards="b", name="ffn_gate")
    up = mm(h2, full["w_up"], shards="b", name="ffn_up")
    (act,) = row_fwd("swiglu", swiglu_fn, [Tiled(gate), Tiled(up)], [], [], [(D_FF, BF16)], tm_row)
    x_out = mm(act, full["w_down"], res=x_mid, name="ffn_down")
    loss_part, dx_out, d_final_g, dx_out_b = loss_head(x_out, tgt, final_g.reshape(1, D))

    grads = {"final_g": d_final_g.reshape(D)}
    d_act = mm(dx_out_b, full["w_down"], tb=True, name="d_act")
    grads["w_down"] = mm(act, dx_out_b, ta=True, out_dtype=BF16, name="d_w_down")
    d_gate, d_up = row_bwd("swiglu_b", swiglu_fn, [Tiled(gate), Tiled(up)], [], [], [Tiled(d_act)], tm_row,
                           tiled_out_dtypes=[BF16, BF16])
    grads["w_gate"] = mm(h2, d_gate, shards="out", out_dtype=BF16, name="d_w_gate")
    grads["w_up"] = mm(h2, d_up, shards="out", out_dtype=BF16, name="d_w_up")
    d_h2 = mm(d_gate, full["w_gate"], shards="bt", name="d_h2_gate")
    d_h2 = mm(d_up, full["w_up"], shards="bt", res=d_h2, name="d_h2_up")
    dx_mid, grads["norm2_g"], dx_mid_b = row_bwd("norm2_b", rms_res_fn, [Tiled(x_mid)], [norm2_g], [],
                                                 [Tiled(d_h2), Tiled(dx_out)], tm_row, bf16_copy_of=(0,))
    d_ycat = mm(dx_mid_b, full["w_out"], tb=True, name="d_ycat")
    grads["w_out"] = mm(y_cat, dx_mid_b, ta=True, out_dtype=BF16, name="d_w_out")
    rs_state, rs_token = sibling_exchange_begin([grads[n].reshape((4,) + args[n].shape[1:]) for n in later], "rs2")
    d_pu, d_pv, grads["sgu_ln_g"], grads["sgu_ln_b"], d_sgu_w, d_b_t, grads["sgu_out_g"] = row_bwd(
        "sgu_b", sgu_fn, sgu_in, sgu_par, [e_mat, e_t], [Tiled(d_ycat, C, 1)], tm_vjp,
        tiled_out_dtypes=[BF16, BF16])
    grads["sgu_w"] = d_sgu_w[None]
    grads["sgu_b"] = d_b_t[:, :HEADS].T[None]
    post_par_b = [gn_w + rs_token[0, 0]] + post_par[1:]
    d_y, d_r1, d_k1, d_v1, d_g, grads["gn_w"], grads["gn_b"], d_r_k = row_bwd(
        "post_b", post_fn, post_in, post_par_b, [jm], [Tiled(d_ycat, C, 0)], tm_vjp)
    grads["r_k"] = d_r_k.reshape(r_k.shape)
    rs_state, rs_token = reduce_scatter_continue(rs_state, later, "rs2", d_y)
    d_r2, d_w, d_k2, d_a, d_b, d_v3 = wkv_bwd(r_, w_, k2, v3, an, b_, ckpt, to_cols(d_y + rs_token[0, 0]))
    d_v2 = from_cols(d_v3)
    pre_cts = [(Tiled(d_r1), Tiled(d_r2)), Tiled(d_w), (Tiled(d_k1), Tiled(d_k2)), (Tiled(d_v1), Tiled(d_v2)),
               Tiled(d_a), Tiled(d_b), Tiled(d_g)]
    d_ps_rkv, d_ps_lora, grads["w0"], d_w2p, grads["a0"], d_a2p, d_g2p, grads["k_k"], grads["k_a"] = row_bwd(
        "pre_b", pre_fn, [Tiled(ps_rkv), Tiled(ps_lora)], pre_par, [jm], pre_cts, tm_vjp)
    grads["w2"], grads["a2"], grads["g2"] = d_w2p[None, :64], d_a2p[None, 64:], d_g2p[None, :160]
    d_p_rkv, d_mu_rkv = shift_bwd(d_ps_rkv, p, mu_rkv, 0, "shift_rkv_b")
    d_p_lora, d_mu_lora = shift_bwd(d_ps_lora, p, mu_lora, P_LORA // LORA_W, "shift_lora_b")
    grads["shift_mu"] = jnp.concatenate([d_mu_rkv, d_mu_lora[:, :288]], axis=1)
    d_p = jnp.concatenate([d_p_rkv, d_pu, d_pv, d_p_lora], axis=1)
    d_h1 = mm(d_p, w_in_p, tb=True, name="d_h1")
    d_w_in_p = mm(h1, d_p, ta=True, out_dtype=BF16, name="d_w_in")
    d_w_in = jnp.concatenate([d_w_in_p[:, :3 * C], d_w_in_p[:, P_LORA:P_LORA + 288], d_w_in_p[:, 3 * C:P_LORA]],
                             axis=1)
    grads["w_in"] = d_w_in.reshape(D, 4, D_IN // 4).transpose(1, 0, 2)
    dx, grads["norm1_g"] = row_bwd("norm1_b", rms_res_fn, [Tiled(x2d)], [norm1_g], [],
                                   [Tiled(d_h1), Tiled(dx_mid)], tm_row)

    small = [n for n in names if n not in big]
    small_shapes = [(1, 64, C) if n in ("w2", "a2") else (1, 160, C) if n == "g2" else args[n].shape for n in small]
    vec = _pack([grads[n].reshape(1, -1) for n in small], F32, SMALL_ROWS)
    w_in_state, w_in_token = reduce_scatter_begin([grads["w_in"]], ["w_in"], "rs1")
    small_state, small_token = all_reduce_begin(vec)
    total = {n: g[None] for n, g in zip(later, reduce_scatter_end(rs_state, later, "rs2", w_in_token + small_token))}
    loss = lax.psum(loss_part[0, 0], ("x", "y", "c"))
    delta, new_m, new_v = {}, {}, {}

    def adam_big(n):
        s = args[n].shape
        d_, m_, v__ = adamw(args[n][0], total[n][0], args["m_" + n][0], args["v_" + n][0], "adamw_" + n)
        delta[n], new_m[n], new_v[n] = d_.reshape(s), m_.reshape(s), v__.reshape(s)

    for n in later:
        adam_big(n)
    total["w_in"] = reduce_scatter_end(w_in_state, ["w_in"], "rs1", new_v["w_down"])[0][None]
    for n, g_sum in zip(small, _unpack(all_reduce_end(small_state, new_v["w_down"]), small_shapes)):
        total[n] = lax.dynamic_slice_in_dim(g_sum, chip * 256, 256, axis=2) if n in lora_names else g_sum
    adam_big("w_in")
    adam_rows = -(-sum(int(np.prod(args[n].shape)) for n in small) // (8 * C)) * 8
    flat = [_pack([src[n].reshape(1, -1) for n in small], F32, adam_rows)
            for src in (args, total, {n: args["m_" + n] for n in small}, {n: args["v_" + n] for n in small})]
    outs = adamw(*flat, "adamw_small")
    for res, o in zip((delta, new_m, new_v), outs):
        res.update(zip(small, _unpack(o, [args[n].shape for n in small])))
    return (loss, dx[None], *[total[n] for n in names], *[delta[n] for n in names],
            *[new_m[n] for n in names], *[new_v[n] for n in names])
```

```python
import functools

import numpy as np
import jax
import jax.numpy as jnp
from jax import lax
from jax.experimental import pallas as pl
from jax.experimental.pallas import tpu as pltpu

F32 = jnp.float32
BF16 = jnp.bfloat16

D = 2048
C = 1024
HEADS = 16
HEAD = 64
LANES = 128
P_W = 5632
P_LORA = 5120
LORA_W = 512
D_IN = 5408
D_FF = 5632
CHUNK = 128
RMS_EPS = 1e-6
LN_EPS = 1e-5
GN_EPS = 64e-5
L2_EPS = 1e-12
VMEM_LIMIT = 56 * 1024 * 1024


def _pcall(body, **kw):
    return pl.pallas_call(body, **kw)


def _cparams(sem):
    return pltpu.CompilerParams(dimension_semantics=sem, vmem_limit_bytes=VMEM_LIMIT)


def _tile(n, most):
    t = most
    while t > 8 and n % t:
        t //= 2
    assert n % t == 0, (n, most)
    return t


MM_TILE = (2048, 512, 2048)
MM_VMEM = 40 * 1024 * 1024


def _div_tile(n, most, quantum=LANES):
    for t in range(min(n, most) // quantum * quantum, 0, -quantum):
        if n % t == 0:
            return t
    raise ValueError((n, most, quantum))


def _mm_tiles(m, n, k, a_bytes, b_bytes, r_bytes, o_bytes):
    tm, tn, tk = _div_tile(m, MM_TILE[0]), _div_tile(n, MM_TILE[1]), _div_tile(k, MM_TILE[2])

    def need(tm, tk):
        return 2 * (tm * tk * a_bytes + tk * tn * b_bytes + tm * tn * (r_bytes + o_bytes)) + tm * tn * 4

    while need(tm, tk) > MM_VMEM:
        if tk >= tm and tk > 512:
            tk = _div_tile(k, tk - LANES)
        else:
            tm = _div_tile(m, tm - LANES)
    return tm, tn, tk


def mm(a, b, *, ta=False, tb=False, res=None, out_dtype=F32, name, shards=None, post=None):
    if shards is not None:
        return _mm_shards(a, b, res, out_dtype, name, shards, post)
    m, k = (a.shape[1], a.shape[0]) if ta else a.shape
    n = b.shape[0] if tb else b.shape[1]
    assert (b.shape[1] if tb else b.shape[0]) == k
    if post is None:
        tm, tn, tk = _mm_tiles(m, n, k, a.dtype.itemsize, b.dtype.itemsize,
                               0 if res is None else res.dtype.itemsize, jnp.dtype(out_dtype).itemsize)
    else:
        tm, tn, tk = _div_tile(m, 1024), _div_tile(n, MM_TILE[1]), _div_tile(k, MM_TILE[2])
    nk = k // tk
    dims = (((0 if ta else 1,), (1 if tb else 0,)), ((), ()))
    a_spec = pl.BlockSpec((tk, tm), lambda i, j, l: (l, i)) if ta else pl.BlockSpec((tm, tk), lambda i, j, l: (i, l))
    b_spec = pl.BlockSpec((tn, tk), lambda i, j, l: (j, l)) if tb else pl.BlockSpec((tk, tn), lambda i, j, l: (l, j))
    o_spec = pl.BlockSpec((tm, tn), lambda i, j, l: (i, j))
    return _mm_call(a, b, res, dims, (m // tm, n // tn, nk), a_spec, b_spec, o_spec, o_spec, (tm, tn),
                    jax.ShapeDtypeStruct((m, n), out_dtype), name, post)


def _mm_shards(a, b, res, out_dtype, name, shards, post=None):
    assert post is None or shards == "b"
    if shards == "b":
        (m, k), ns = a.shape, b.shape[2]
        tm, tk = _div_tile(m, 1024 if post is None else 512), _div_tile(k, 2048)
        grid, dims, acc = (m // tm, 4, k // tk), (((1,), (0,)), ((), ())), (tm, ns)
        a_spec = pl.BlockSpec((tm, tk), lambda i, s, l: (i, l))
        b_spec = pl.BlockSpec((None, tk, ns), lambda i, s, l: (s, l, 0))
        o_spec = pl.BlockSpec((tm, ns), lambda i, s, l: (i, s))
        out = jax.ShapeDtypeStruct((m, 4 * ns), out_dtype)
    elif shards == "bt":
        m, (_, n, ns) = a.shape[0], b.shape
        tm, tn = _div_tile(m, 2048), _div_tile(n, 512)
        grid, dims, acc = (m // tm, n // tn, 4), (((1,), (1,)), ((), ())), (tm, tn)
        a_spec = pl.BlockSpec((tm, ns), lambda i, j, s: (i, s))
        b_spec = pl.BlockSpec((None, tn, ns), lambda i, j, s: (s, j, 0))
        o_spec = pl.BlockSpec((tm, tn), lambda i, j, s: (i, j))
        out = jax.ShapeDtypeStruct((m, n), out_dtype)
    else:
        (t, m), ns = a.shape, b.shape[1] // 4
        tm, tk = _div_tile(m, 1024), _div_tile(t, 2048)
        grid, dims, acc = (m // tm, 4, t // tk), (((0,), (0,)), ((), ())), (tm, ns)
        a_spec = pl.BlockSpec((tk, tm), lambda i, s, l: (l, i))
        b_spec = pl.BlockSpec((tk, ns), lambda i, s, l: (l, s))
        o_spec = pl.BlockSpec((None, tm, ns), lambda i, s, l: (s, i, 0))
        out = jax.ShapeDtypeStruct((4, m, ns), out_dtype)
    return _mm_call(a, b, res, dims, grid, a_spec, b_spec, o_spec, o_spec, acc, out, name, post)


def _mm_call(a, b, res, dims, grid, a_spec, b_spec, r_spec, o_spec, acc_shape, out, name, post=None):
    nk = grid[2]
    post_fn, post_in, post_dtypes = (None, [], [out.dtype]) if post is None else post
    n_extra = (res is not None) + len(post_in)

    def body(*refs):
        a_ref, b_ref = refs[:2]
        r_ref = None if res is None else refs[2]
        p_refs = refs[2 + (res is not None):2 + n_extra]
        o_refs = refs[2 + n_extra:2 + n_extra + len(post_dtypes)]

        def finish(acc):
            if r_ref is not None:
                acc = acc + r_ref[...].astype(F32)
            outs = (acc,) if post_fn is None else post_fn(acc, *[p[...] for p in p_refs])
            for o_ref, o in zip(o_refs, outs):
                o_ref[...] = o.astype(o_ref.dtype)

        prod = lax.dot_general(a_ref[...].astype(BF16), b_ref[...].astype(BF16), dims, preferred_element_type=F32)
        if nk == 1:
            finish(prod)
            return
        acc_ref = refs[-1]
        kk = pl.program_id(2)

        @pl.when(kk == 0)
        def _():
            acc_ref[...] = prod

        @pl.when(kk > 0)
        def _():
            acc_ref[...] += prod

        @pl.when(kk == nk - 1)
        def _():
            finish(acc_ref[...])

    in_specs = [a_spec, b_spec] + ([r_spec] if res is not None else []) + [o_spec] * len(post_in)
    args = (a, b) + ((res,) if res is not None else ()) + tuple(post_in)
    outs = _pcall(
        body, name=name, grid=grid, in_specs=in_specs, out_specs=[o_spec] * len(post_dtypes),
        out_shape=[jax.ShapeDtypeStruct(out.shape, dt) for dt in post_dtypes],
        scratch_shapes=[] if nk == 1 else [pltpu.VMEM(acc_shape, F32)],
        compiler_params=_cparams(("parallel", "parallel", "arbitrary")),
    )(*args)
    return outs[0] if post is None else outs


class Tiled:
    def __init__(self, arr, width=None, col=0):
        self.arr, self.width, self.col = arr, (arr.shape[1] if width is None else width), col

    def spec(self, tm):
        col = self.col
        return pl.BlockSpec((tm, self.width), lambda i: (i, col))


def _full_spec(p):
    nd = p.ndim
    return pl.BlockSpec(p.shape, lambda i: (0,) * nd)


def row_fwd(name, fn, tiled, params, consts, outs, tm):
    t = tiled[0].arr.shape[0]
    n_in = len(tiled) + len(params) + len(consts)

    def body(*refs):
        res = fn(*[r[...] for r in refs[:n_in]])
        for o_ref, r in zip(refs[n_in:], res):
            o_ref[...] = r.astype(o_ref.dtype)

    return _pcall(
        body, name=name, grid=(t // tm,),
        in_specs=[x.spec(tm) for x in tiled] + [_full_spec(p) for p in params + consts],
        out_specs=[pl.BlockSpec((tm, w), lambda i: (i, 0)) for w, _ in outs],
        out_shape=[jax.ShapeDtypeStruct((t, w), dt) for w, dt in outs],
        compiler_params=_cparams(("parallel",)),
    )(*[x.arr for x in tiled], *params, *consts)


def row_bwd(name, fn, tiled, params, consts, cts, tm, tiled_out_dtypes=None, bf16_copy_of=()):
    t = tiled[0].arr.shape[0]
    nt, npar, ncon = len(tiled), len(params), len(consts)
    cts = [c if isinstance(c, tuple) else (c,) for c in cts]
    flat_cts = [x for c in cts for x in c]
    tiled_out_dtypes = tiled_out_dtypes or [F32] * nt
    copies = [tiled[j] for j in bf16_copy_of]

    def body(*refs):
        n_in = nt + npar + ncon
        ins = [r[...].astype(F32) for r in refs[:nt + npar]]
        con = [r[...] for r in refs[nt + npar:n_in]]
        ct_refs = list(refs[n_in:n_in + len(flat_cts)])
        o = refs[n_in + len(flat_cts):]
        ct = []
        for c in cts:
            parts = [ct_refs.pop(0)[...].astype(F32) for _ in c]
            ct.append(functools.reduce(lambda p, q: p + q, parts))
        _, vjp = jax.vjp(lambda *a: fn(*a, *con), *ins)
        g = vjp(tuple(ct))
        for j in range(nt):
            o[j][...] = g[j].astype(o[j].dtype)
        for n, j in enumerate(bf16_copy_of):
            o[nt + npar + n][...] = g[j].astype(BF16)
        first = pl.program_id(0) == 0

        @pl.when(first)
        def _():
            for j in range(npar):
                o[nt + j][...] = g[nt + j]

        @pl.when(jnp.logical_not(first))
        def _():
            for j in range(npar):
                o[nt + j][...] += g[nt + j]

    return _pcall(
        body, name=name, grid=(t // tm,),
        in_specs=[x.spec(tm) for x in tiled] + [_full_spec(p) for p in params + consts]
        + [x.spec(tm) for x in flat_cts],
        out_specs=[pl.BlockSpec((tm, x.width), lambda i: (i, 0)) for x in tiled] + [_full_spec(p) for p in params]
        + [pl.BlockSpec((tm, x.width), lambda i: (i, 0)) for x in copies],
        out_shape=[jax.ShapeDtypeStruct((t, x.width), dt) for x, dt in zip(tiled, tiled_out_dtypes)]
        + [jax.ShapeDtypeStruct(p.shape, F32) for p in params]
        + [jax.ShapeDtypeStruct((t, x.width), BF16) for x in copies],
        compiler_params=_cparams(("arbitrary",)),
    )(*[x.arr for x in tiled], *params, *consts, *[x.arr for x in flat_cts])


def _split_dot(x, w):
    hi = x.astype(BF16)
    lo = (x - hi.astype(F32)).astype(BF16)
    return jnp.dot(hi, w, preferred_element_type=F32) + jnp.dot(lo, w, preferred_element_type=F32)


@jax.custom_vjp
def headsum(x, j):
    return _split_dot(x, j)


def _headsum_fwd(x, j):
    return _split_dot(x, j), j


def _headsum_bwd(j, ct):
    return _split_dot(ct, j), jnp.zeros_like(j)


headsum.defvjp(_headsum_fwd, _headsum_bwd)


def _bdot(x, w):
    return jnp.dot(x.astype(BF16), w.astype(BF16), preferred_element_type=F32)


def _sigmoid(x):
    return 1.0 / (1.0 + jnp.exp(-x))


def _softplus(x):
    return jnp.maximum(x, 0.0) + jnp.log(1.0 + jnp.exp(-jnp.abs(x)))


def rms_fn(x, g):
    return (x * lax.rsqrt(jnp.mean(x * x, axis=-1, keepdims=True) + RMS_EPS) * g,)


def rms_res_fn(x, g):
    return rms_fn(x, g)[0], x


def pre_fn(rkv, lora, w0, w2p, a0, a2p, g2p, k_k, k_a, jm):
    r, k, v = rkv[:, :C], rkv[:, C:2 * C], rkv[:, 2 * C:]
    zwa, zg = lora[:, :LANES], lora[:, LANES:LANES + 256]
    w_log = -_softplus(-(w0 + _bdot(jnp.tanh(zwa), w2p))) - 0.5
    decay = jnp.exp(-jnp.exp(w_log))
    a = _sigmoid(a0 + _bdot(zwa, a2p))
    g = _bdot(_sigmoid(zg), g2p)
    kk = k * k_k
    kk = kk / jnp.maximum(jnp.sqrt(headsum(kk * kk, jm)), L2_EPS)
    k2 = k * (1.0 + (a - 1.0) * k_a)
    return r, decay, k2, v, -kk, kk * a, g


def post_fn(y, r, k2, v, g, gn_w, gn_b, r_k, jm):
    mu = headsum(y, jm) * (1.0 / HEAD)
    yc = y - mu
    var = headsum(yc * yc, jm) * (1.0 / HEAD)
    yn = yc * lax.rsqrt(var + GN_EPS) * gn_w + gn_b
    bonus = headsum(r * k2 * r_k, jm) * v
    return ((yn + bonus) * g,)


def _gelu(x):
    return 0.5 * x * (1.0 + lax.erf(x * np.float32(1.0 / np.sqrt(2.0))))


@jax.custom_vjp
def expand_groups(b_t, e, e_t):
    return _split_dot(b_t, e)


def _expand_groups_fwd(b_t, e, e_t):
    return _split_dot(b_t, e), (e, e_t)


def _expand_groups_bwd(res, ct):
    e, e_t = res
    return _split_dot(ct, e_t), jnp.zeros_like(e), jnp.zeros_like(e_t)


expand_groups.defvjp(_expand_groups_fwd, _expand_groups_bwd)


def sgu_fn(pu, pv, ln_g, ln_b, w_s, b_t, out_g, e, e_t):
    rows = pu.shape[0]
    b_exp = expand_groups(b_t, e, e_t)
    u, v = _gelu(pu), _gelu(pv)
    mu = jnp.mean(v, axis=-1, keepdims=True)
    vc = v - mu
    var = jnp.mean(vc * vc, axis=-1, keepdims=True)
    v = vc * lax.rsqrt(var + LN_EPS) * ln_g + ln_b
    tri = lax.broadcasted_iota(jnp.int32, (CHUNK, CHUNK), 0) >= lax.broadcasted_iota(jnp.int32, (CHUNK, CHUNK), 1)
    left = lax.broadcasted_iota(jnp.int32, (CHUNK, LANES), 1) < HEAD
    chunks = []
    for c0 in range(0, rows, CHUNK):
        cols = []
        for gp in range(HEADS // 2):
            vp = v[c0:c0 + CHUNK, gp * LANES:(gp + 1) * LANES]
            wa = jnp.where(tri, w_s[2 * gp], 0.0)
            wb = jnp.where(tri, w_s[2 * gp + 1], 0.0)
            cols.append(jnp.where(left, _bdot(wa, vp), _bdot(wb, vp)))
        chunks.append(jnp.concatenate(cols, axis=1) + b_exp)
    s = jnp.concatenate(chunks, axis=0) if len(chunks) > 1 else chunks[0]
    y = u * s
    return (y * lax.rsqrt(jnp.mean(y * y, axis=-1, keepdims=True) + RMS_EPS) * out_g,)


def swiglu_fn(gate, up):
    return (gate * _sigmoid(gate) * up,)


SCAN_CHUNK = 32
NB = C // LANES
STEP_GROUP = 4


def to_cols(a):
    t = a.shape[0]
    return a.reshape(t // 8, LANES, HEAD).transpose(0, 2, 1)


def from_cols(a):
    return a.transpose(0, 2, 1).reshape(a.shape[0] * 8, C)


def _scan_consts():
    i = np.arange(LANES)
    spread = (i[:, None] % 2) == (i[None, :] // HEAD)
    pick = (i[:, None] // HEAD) == (i[None, :] % 2)
    j2 = (i[:, None] // HEAD) == (i[None, :] // HEAD)
    sel = (i[None, :] // 2) == (np.arange(8 * NB * HEAD)[:, None] // HEAD)
    return (jnp.asarray(spread, BF16), jnp.asarray(pick, BF16), jnp.asarray(np.concatenate([j2, j2], 0), BF16),
            jnp.asarray(sel, BF16))


def _stack(blocks):
    return jnp.concatenate(blocks, axis=0) if len(blocks) > 1 else blocks[0]


def _unstack(x, n):
    return [x[i * HEAD:(i + 1) * HEAD] for i in range(n)]


def _headsums(blocks, j2k, group):
    res = []
    for g0 in range(0, len(blocks), group):
        x = _stack(blocks[g0:g0 + group])
        hi = x.astype(BF16)
        lo = (x - hi.astype(F32)).astype(BF16)
        out = jnp.dot(jnp.concatenate([hi, lo], axis=1), j2k, preferred_element_type=F32)
        res += _unstack(out, len(blocks[g0:g0 + group]))
    return res


def _headsums_out(blocks, pick):
    return _unstack(jnp.dot(_stack(blocks).astype(BF16), pick, preferred_element_type=F32), len(blocks))


def _expand8(tile, sel, spread):
    lhs = jnp.tile(tile.astype(BF16), (8 * NB, 1)) * sel
    return _unstack(jnp.dot(lhs, spread, preferred_element_type=F32), 8 * NB)


def _collapse(tile, blocks, first, lane_pair):
    for n, blk in enumerate(blocks):
        tile = jnp.where(jnp.tile(lane_pair == first + n, (HEAD // 8, 1)), blk, tile)
    return tile


def _row(tile, j, cb):
    return jnp.broadcast_to(tile[j:j + 1, cb * LANES:(cb + 1) * LANES], (HEAD, LANES))


def _scan_step(s_ref, rows, vexp, j2k, j, hist=None):
    w_t, k_t, a_t, b_t, r_t = rows
    s = [s_ref[:, cb * LANES:(cb + 1) * LANES] for cb in range(NB)]
    sab = _headsums([s[cb] * _row(a_t, j, cb) for cb in range(NB)], j2k, STEP_GROUP)
    out = []
    for cb in range(NB):
        sl = slice(cb * LANES, (cb + 1) * LANES)
        s_new = s[cb] * _row(w_t, j, cb) + sab[cb] * _row(b_t, j, cb) + vexp[j * NB + cb] * _row(k_t, j, cb)
        s_ref[:, sl] = s_new
        if hist is None:
            out.append(s_new * _row(r_t, j, cb))
        else:
            s_hist, sab_hist, idx = hist
            s_hist[idx + 1, :, sl] = s_new
            sab_hist[idx, :, sl] = sab[cb]
    return out


def wkv_fwd(r, w, k, v3, a, b):
    t = r.shape[0]
    nc = t // SCAN_CHUNK
    n8 = SCAN_CHUNK // 8

    def body(r_ref, w_ref, k_ref, a_ref, b_ref, v3_ref, spread_ref, pick_ref, j2k_ref, sel_ref, y3_ref, ck_ref,
             s_ref):
        @pl.when(pl.program_id(0) == 0)
        def _():
            s_ref[...] = jnp.zeros_like(s_ref)

        ck_ref[0] = s_ref[...]
        spread, pick, j2k, sel = spread_ref[...], pick_ref[...], j2k_ref[...], sel_ref[...]
        lane_pair = lax.broadcasted_iota(jnp.int32, (8, LANES), 1) // 2

        def t8_body(t8, carry):
            row0 = pl.multiple_of(t8 * 8, 8)
            rows = [ref[pl.ds(row0, 8), :] for ref in (w_ref, k_ref, a_ref, b_ref, r_ref)]
            vexp = _expand8(v3_ref[t8], sel, spread)
            y3 = jnp.zeros((HEAD, LANES), F32)
            for j in range(8):
                y3 = _collapse(y3, _headsums_out(_scan_step(s_ref, rows, vexp, j2k, j), pick), j * NB, lane_pair)
            y3_ref[t8] = y3
            return carry

        lax.fori_loop(0, n8, t8_body, 0)

    row_spec = pl.BlockSpec((SCAN_CHUNK, C), lambda i: (i, 0))
    col_spec = pl.BlockSpec((n8, HEAD, LANES), lambda i: (i, 0, 0))
    consts = _scan_consts()
    return _pcall(
        body, name="wkv_fwd", grid=(nc,),
        in_specs=[row_spec] * 5 + [col_spec] + [pl.BlockSpec(c.shape, lambda i: (0, 0)) for c in consts],
        out_specs=[col_spec, pl.BlockSpec((1, HEAD, C), lambda i: (i, 0, 0))],
        out_shape=[jax.ShapeDtypeStruct((t // 8, HEAD, LANES), F32), jax.ShapeDtypeStruct((nc, HEAD, C), F32)],
        scratch_shapes=[pltpu.VMEM((HEAD, C), F32)],
        compiler_params=_cparams(("arbitrary",)),
    )(r, w, k, a, b, v3, *consts)


def wkv_bwd(r, w, k, v3, a, b, ckpt, dy3):
    t = r.shape[0]
    nc = t // SCAN_CHUNK
    n8 = SCAN_CHUNK // 8

    def body(r_ref, w_ref, k_ref, a_ref, b_ref, v3_ref, dy3_ref, ck_ref, spread_ref, pick_ref, j2k_ref, sel_ref,
             dr_ref, dw_ref, dk_ref, da_ref, db_ref, dv3_ref, s_ref, g_ref, s_hist, sab_hist):
        @pl.when(pl.program_id(0) == 0)
        def _():
            g_ref[...] = jnp.zeros_like(g_ref)

        spread, pick, j2k, sel = spread_ref[...], pick_ref[...], j2k_ref[...], sel_ref[...]
        lane_pair = lax.broadcasted_iota(jnp.int32, (8, LANES), 1) // 2
        sub = lax.broadcasted_iota(jnp.int32, (8, LANES), 0)
        s_ref[...] = ck_ref[0]
        s_hist[0] = ck_ref[0]

        def redo(t8, carry):
            row0 = pl.multiple_of(t8 * 8, 8)
            rows = [ref[pl.ds(row0, 8), :] for ref in (w_ref, k_ref, a_ref, b_ref, r_ref)]
            vexp = _expand8(v3_ref[t8], sel, spread)
            for j in range(8):
                _scan_step(s_ref, rows, vexp, j2k, j, hist=(s_hist, sab_hist, t8 * 8 + j))
            return carry

        lax.fori_loop(0, n8, redo, 0)

        def back(q, carry):
            t8 = n8 - 1 - q
            row0 = pl.multiple_of(t8 * 8, 8)
            w_t, k_t, a_t, b_t, r_t = [ref[pl.ds(row0, 8), :] for ref in (w_ref, k_ref, a_ref, b_ref, r_ref)]
            vexp = _expand8(v3_ref[t8], sel, spread)
            dyexp = _expand8(dy3_ref[t8], sel, spread)
            dv3 = jnp.zeros((HEAD, LANES), F32)
            tiles = [[jnp.zeros((8, LANES), F32) for _ in range(NB)] for _ in range(5)]
            for j in range(7, -1, -1):
                idx = t8 * 8 + j
                g = [g_ref[:, cb * LANES:(cb + 1) * LANES] + dyexp[j * NB + cb] * _row(r_t, j, cb)
                     for cb in range(NB)]
                dsab = _headsums([g[cb] * _row(b_t, j, cb) for cb in range(NB)], j2k, STEP_GROUP)
                dv3 = _collapse(dv3, _headsums_out([g[cb] * _row(k_t, j, cb) for cb in range(NB)], pick), j * NB,
                                lane_pair)
                for cb in range(NB):
                    sl = slice(cb * LANES, (cb + 1) * LANES)
                    s_new = s_hist[idx + 1, :, sl]
                    s_old = s_hist[idx, :, sl]
                    sab = sab_hist[idx, :, sl]
                    sums = (s_new * dyexp[j * NB + cb], g[cb] * s_old, g[cb] * vexp[j * NB + cb],
                            s_old * dsab[cb], g[cb] * sab)
                    for n, prod in enumerate(sums):
                        rowsum = jnp.broadcast_to(jnp.sum(prod, axis=0, keepdims=True), (8, LANES))
                        tiles[n][cb] = jnp.where(sub == j, rowsum, tiles[n][cb])
                    g_ref[:, sl] = g[cb] * _row(w_t, j, cb) + dsab[cb] * _row(a_t, j, cb)
            dv3_ref[t8] = dv3
            for n, ref in enumerate((dr_ref, dw_ref, dk_ref, da_ref, db_ref)):
                for cb in range(NB):
                    ref[pl.ds(row0, 8), cb * LANES:(cb + 1) * LANES] = tiles[n][cb]
            return carry

        lax.fori_loop(0, n8, back, 0)

    row_spec = pl.BlockSpec((SCAN_CHUNK, C), lambda i: (nc - 1 - i, 0))
    col_spec = pl.BlockSpec((n8, HEAD, LANES), lambda i: (nc - 1 - i, 0, 0))
    consts = _scan_consts()
    return _pcall(
        body, name="wkv_bwd", grid=(nc,),
        in_specs=[row_spec] * 5 + [col_spec, col_spec, pl.BlockSpec((1, HEAD, C), lambda i: (nc - 1 - i, 0, 0))]
        + [pl.BlockSpec(c.shape, lambda i: (0, 0)) for c in consts],
        out_specs=[row_spec] * 5 + [col_spec],
        out_shape=[jax.ShapeDtypeStruct((t, C), F32)] * 5 + [jax.ShapeDtypeStruct((t // 8, HEAD, LANES), F32)],
        scratch_shapes=[pltpu.VMEM((HEAD, C), F32), pltpu.VMEM((HEAD, C), F32),
                        pltpu.VMEM((SCAN_CHUNK + 1, HEAD, C), F32), pltpu.VMEM((SCAN_CHUNK, HEAD, C), F32)],
        compiler_params=_cparams(("arbitrary",)),
    )(r, w, k, a, b, v3, dy3, ckpt, *consts)


def _prev_rows(cur, before, first_tile):
    last = jnp.where(first_tile, 0.0, before[7:8, :])
    row = lax.broadcasted_iota(jnp.int32, cur.shape, 0)
    return jnp.where(row == 0, last, pltpu.roll(cur, 1, 0))


def shift_fwd(p, mu, col, name):
    t, width = p.shape[0], mu.shape[1]
    tm = _tile(t, 256)

    def body(p_ref, before_ref, mu_ref, o_ref):
        cur = p_ref[...]
        prev = _prev_rows(cur, before_ref[...], pl.program_id(0) == 0)
        o_ref[...] = cur + (prev - cur) * mu_ref[...]

    return _pcall(
        body, name=name, grid=(t // tm,),
        in_specs=[pl.BlockSpec((tm, width), lambda i: (i, col)),
                  pl.BlockSpec((8, width), lambda i: (jnp.maximum(i * (tm // 8) - 1, 0), col)),
                  pl.BlockSpec((1, width), lambda i: (0, 0))],
        out_specs=pl.BlockSpec((tm, width), lambda i: (i, 0)),
        out_shape=jax.ShapeDtypeStruct((t, width), F32),
        compiler_params=_cparams(("parallel",)),
    )(p, p, mu)


def shift_bwd(dps, p, mu, col, name):
    t, width = dps.shape
    tm = _tile(t, 256)
    nt = t // tm

    def body(d_ref, after_ref, p_ref, before_ref, mu_ref, dp_ref, dmu_ref):
        i = pl.program_id(0)
        d, cur, mu_v = d_ref[...], p_ref[...], mu_ref[...]
        row = lax.broadcasted_iota(jnp.int32, d.shape, 0)
        first_after = jnp.where(i == nt - 1, 0.0, after_ref[0:1, :])
        nxt = jnp.where(row == tm - 1, first_after, pltpu.roll(d, tm - 1, 0))
        dp_ref[...] = (d * (1.0 - mu_v) + nxt * mu_v).astype(BF16)
        prev = _prev_rows(cur, before_ref[...], i == 0)
        part = jnp.sum(d * (prev - cur), axis=0, keepdims=True)

        @pl.when(i == 0)
        def _():
            dmu_ref[...] = part

        @pl.when(i > 0)
        def _():
            dmu_ref[...] += part

    return _pcall(
        body, name=name, grid=(nt,),
        in_specs=[pl.BlockSpec((tm, width), lambda i: (i, 0)),
                  pl.BlockSpec((8, width), lambda i: (jnp.minimum((i + 1) * (tm // 8), t // 8 - 1), 0)),
                  pl.BlockSpec((tm, width), lambda i: (i, col)),
                  pl.BlockSpec((8, width), lambda i: (jnp.maximum(i * (tm // 8) - 1, 0), col)),
                  pl.BlockSpec((1, width), lambda i: (0, 0))],
        out_specs=[pl.BlockSpec((tm, width), lambda i: (i, 0)), pl.BlockSpec((1, width), lambda i: (0, 0))],
        out_shape=[jax.ShapeDtypeStruct((t, width), BF16), jax.ShapeDtypeStruct((1, width), F32)],
        compiler_params=_cparams(("arbitrary",)),
    )(dps, dps, p, p, mu)


def loss_head(x3, tgt, g):
    t, d = x3.shape
    tm = _tile(t, 256)

    def body(x_ref, t_ref, g_ref, loss_ref, dx_ref, dg_ref, dxb_ref):
        (y,), vjp = jax.vjp(lambda a, b: rms_fn(a, b), x_ref[...], g_ref[...])
        diff = y - t_ref[...]
        part = 0.5 * jnp.sum(jnp.mean(diff * diff, axis=-1, keepdims=True), axis=0, keepdims=True)
        dx, dg = vjp((diff * (1.0 / d),))
        dx_ref[...] = dx
        dxb_ref[...] = dx.astype(BF16)
        part = jnp.broadcast_to(part, (1, LANES))
        first = pl.program_id(0) == 0

        @pl.when(first)
        def _():
            loss_ref[...] = part
            dg_ref[...] = dg

        @pl.when(jnp.logical_not(first))
        def _():
            loss_ref[...] += part
            dg_ref[...] += dg

    row = pl.BlockSpec((tm, d), lambda i: (i, 0))
    vec = pl.BlockSpec((1, d), lambda i: (0, 0))
    return _pcall(
        body, name="loss_head", grid=(t // tm,), in_specs=[row, row, vec],
        out_specs=[pl.BlockSpec((1, LANES), lambda i: (0, 0)), row, vec, row],
        out_shape=[jax.ShapeDtypeStruct((1, LANES), F32), jax.ShapeDtypeStruct((t, d), F32),
                   jax.ShapeDtypeStruct((1, d), F32), jax.ShapeDtypeStruct((t, d), BF16)],
        compiler_params=_cparams(("arbitrary",)),
    )(x3, tgt, g)


ADAM_LR, ADAM_B1, ADAM_B2, ADAM_EPS, ADAM_WD, ADAM_STEP = 0.001, 0.9, 0.999, 1e-08, 0.01, 10


def adamw(w, g, m, v, name):
    rows, width = w.shape
    tr = _tile(rows, 256)

    def body(w_ref, g_ref, m_ref, v_ref, d_ref, nm_ref, nv_ref):
        gv = g_ref[...]
        m_new = ADAM_B1 * m_ref[...] + (1.0 - ADAM_B1) * gv
        v_new = ADAM_B2 * v_ref[...] + (1.0 - ADAM_B2) * (gv * gv)
        m_hat = m_new / (1.0 - ADAM_B1 ** ADAM_STEP)
        v_hat = v_new / (1.0 - ADAM_B2 ** ADAM_STEP)
        d_ref[...] = -ADAM_LR * (m_hat / (jnp.sqrt(v_hat) + ADAM_EPS) + ADAM_WD * w_ref[...])
        nm_ref[...] = m_new
        nv_ref[...] = v_new

    spec = pl.BlockSpec((tr, width), lambda i: (i, 0))
    return _pcall(
        body, name=name, grid=(rows // tr,), in_specs=[spec] * 4, out_specs=[spec] * 3,
        out_shape=[jax.ShapeDtypeStruct((rows, width), F32)] * 3,
        compiler_params=_cparams(("parallel",)),
    )(w, g, m, v)


def sum_slots(q, name, for_swap=False):
    n, rows, width = q.shape
    tr = _tile(rows, 512)

    def body(*refs):
        refs = refs[for_swap:]
        acc = refs[0][...].astype(F32)
        for r in refs[1:n]:
            acc = acc + r[...].astype(F32)
        refs[n][...] = acc

    if not for_swap:
        specs = [pl.BlockSpec((None, tr, width), functools.partial(lambda s, i: (s, i, 0), s)) for s in range(n)]
        return _pcall(body, name=name, grid=(rows // tr,), in_specs=specs,
                      out_specs=pl.BlockSpec((tr, width), lambda i: (i, 0)),
                      out_shape=jax.ShapeDtypeStruct((rows, width), F32),
                      compiler_params=_cparams(("parallel",)))(*([q] * n))
    specs = [pl.BlockSpec((None, tr, width), functools.partial(lambda s, i, c_ref: (s, i, 0), s)) for s in range(n)]
    grid_spec = pltpu.PrefetchScalarGridSpec(
        num_scalar_prefetch=1, grid=(rows // tr,), in_specs=specs,
        out_specs=pl.BlockSpec((None, tr, width), lambda i, c_ref: (c_ref[0], i, 0)))
    return _pcall(body, name=name, grid_spec=grid_spec, out_shape=jax.ShapeDtypeStruct((2, rows, width), F32),
                  compiler_params=_cparams(("parallel",)))(
        lax.axis_index("c").astype(jnp.int32).reshape(1), *([q] * n))


MESH = pl.DeviceIdType.MESH
ANY_SPEC = pl.BlockSpec(memory_space=pl.ANY)


def _coords():
    return lax.axis_index("x"), lax.axis_index("y"), lax.axis_index("c")


def _other_chips(x, y):
    return [(1 - x, y), (x, 1 - y), (1 - x, 1 - y)]


def push(name, srcs, n_dst, plan_fn):
    n_arr = len(srcs)
    n_send = len(plan_fn(0, 0, 0))

    def body(*refs):
        src_refs, dst_refs = refs[:n_arr], refs[n_arr:2 * n_arr]
        send_sems, recv_sems = refs[2 * n_arr:]
        sends = plan_fn(*_coords())
        out = []
        for q, (src_ref, dst_ref) in enumerate(zip(src_refs, dst_refs)):
            out += [pltpu.make_async_remote_copy(src_ref.at[si], dst_ref.at[di], send_sems.at[q * n_send + k],
                                                 recv_sems.at[q * n_send + k], device_id=peer, device_id_type=MESH)
                    for k, (si, peer, di, _) in enumerate(sends)]
        for cp in out:
            cp.start()
        for q, (src_ref, dst_ref) in enumerate(zip(src_refs, dst_refs)):
            for k, (si, peer, _, ri) in enumerate(sends):
                pltpu.make_async_remote_copy(src_ref.at[si], dst_ref.at[ri], send_sems.at[q * n_send + k],
                                             recv_sems.at[q * n_send + k], device_id=peer,
                                             device_id_type=MESH).wait_recv()
        for cp in out:
            cp.wait_send()

    return _pcall(
        body, name=name, in_specs=[ANY_SPEC] * n_arr, out_specs=[ANY_SPEC] * n_arr,
        out_shape=[jax.ShapeDtypeStruct((n_dst,) + s.shape[1:], s.dtype) for s in srcs],
        scratch_shapes=[pltpu.SemaphoreType.DMA((n_arr * n_send,)), pltpu.SemaphoreType.DMA((n_arr * n_send,))],
    )(*srcs)


def plan_sibling_quarters(x, y, c):
    return [(2 * s + (1 - c), (x, y, 1 - c), s, s) for s in range(4)]


def plan_chips_by_shard(x, y, c):
    me = 2 * x + y
    return [(2 * px + py, (px, py, c), me, 2 * px + py) for px, py in _other_chips(x, y)]


def plan_chips_half(x, y, c):
    me = 2 * x + y
    return [(c, (px, py, c), me, 2 * px + py) for px, py in _other_chips(x, y)]


def swap_halves(name, bufs):
    n_arr = len(bufs)

    def body(*refs):
        buf_refs = refs[n_arr:2 * n_arr]
        send_sems, recv_sems = refs[2 * n_arr:]
        x, y, c = _coords()
        out = [pltpu.make_async_remote_copy(b.at[c], b.at[c], send_sems.at[q], recv_sems.at[q],
                                            device_id=(x, y, 1 - c), device_id_type=MESH)
               for q, b in enumerate(buf_refs)]
        for cp in out:
            cp.start()
        for q, b in enumerate(buf_refs):
            pltpu.make_async_remote_copy(b.at[1 - c], b.at[1 - c], send_sems.at[q], recv_sems.at[q],
                                         device_id=(x, y, 1 - c), device_id_type=MESH).wait_recv()
        for cp in out:
            cp.wait_send()

    return _pcall(
        body, name=name, in_specs=[ANY_SPEC] * n_arr, out_specs=[ANY_SPEC] * n_arr,
        out_shape=[jax.ShapeDtypeStruct(b.shape, b.dtype) for b in bufs],
        input_output_aliases={q: q for q in range(n_arr)},
        scratch_shapes=[pltpu.SemaphoreType.DMA((n_arr,)), pltpu.SemaphoreType.DMA((n_arr,))],
    )(*bufs)


HBM_SPEC = pl.BlockSpec(memory_space=pltpu.HBM)
SEM_SPEC = pl.BlockSpec(memory_space=pltpu.SEMAPHORE)
DATAFLOW = pltpu.SideEffectType.DATAFLOW_SIDE_EFFECTING


def _in_hbm(a):
    return pltpu.with_memory_space_constraint(a, pltpu.HBM)


def push_start(name, srcs, lands, plan_fn, after):
    n_arr = len(srcs)
    n_send = len(plan_fn(0, 0, 0))
    arrays = list(srcs) + ([] if lands is None else list(lands))
    n_all = len(arrays)

    def body(*refs):
        src_refs, land_refs = refs[:n_arr], refs[n_all - n_arr:n_all]
        send_sems, recv_sems = refs[n_all + 1:n_all + 3]
        token = refs[-1]
        sends = plan_fn(*_coords())
        for q, (src_ref, land_ref) in enumerate(zip(src_refs, land_refs)):
            for k, (si, peer, di, _) in enumerate(sends):
                pltpu.make_async_remote_copy(src_ref.at[si], land_ref.at[di], send_sems.at[q * n_send + k],
                                             recv_sems.at[q * n_send + k], device_id=peer,
                                             device_id_type=MESH).start()
        token[...] = jnp.zeros_like(token)

    sems = pltpu.SemaphoreType.DMA((n_arr * n_send,))
    out = _pcall(
        body, name=name,
        in_specs=[HBM_SPEC] * n_all + [ANY_SPEC],
        out_specs=[SEM_SPEC, SEM_SPEC] + [HBM_SPEC] * n_all + [pl.BlockSpec(memory_space=pltpu.VMEM)],
        out_shape=[sems, sems] + [pltpu.HBM(a.shape, a.dtype) for a in arrays]
        + [jax.ShapeDtypeStruct((8, LANES), F32)],
        input_output_aliases={q: 2 + q for q in range(n_all)},
        compiler_params=pltpu.CompilerParams(has_side_effects=DATAFLOW),
    )(*[_in_hbm(a) for a in arrays], after)
    return out[:-1], out[-1]


def push_wait(name, state, plan_fn, after, in_place=False, both=False):
    arrays = list(state[2:])
    n_all = len(arrays)
    n_arr = n_all if in_place else n_all // 2
    n_send = len(plan_fn(0, 0, 0))
    send_sems, recv_sems = state[:2]

    def body(*refs):
        src_refs, land_refs = refs[:n_arr], refs[n_all - n_arr:n_all]
        send_ref, recv_ref = refs[n_all:n_all + 2]
        sends = plan_fn(*_coords())
        for q, (src_ref, land_ref) in enumerate(zip(src_refs, land_refs)):
            for k, (si, peer, _, ri) in enumerate(sends):
                cp = pltpu.make_async_remote_copy(src_ref.at[si], land_ref.at[ri], send_ref.at[q * n_send + k],
                                                  recv_ref.at[q * n_send + k], device_id=peer, device_id_type=MESH)
                cp.wait_send()
                cp.wait_recv()

    out = _pcall(
        body, name=name,
        in_specs=[HBM_SPEC] * n_all + [SEM_SPEC, SEM_SPEC, ANY_SPEC],
        out_specs=[HBM_SPEC] * n_all,
        out_shape=[pltpu.HBM(a.shape, a.dtype) for a in arrays],
        input_output_aliases={q: q for q in range(n_all)},
        compiler_params=pltpu.CompilerParams(has_side_effects=DATAFLOW),
    )(*arrays, send_sems, recv_sems, after)
    return list(out) if both else list(out[n_all - n_arr:])


def plan_chips_gather(x, y, c):
    me = 2 * x + y
    return [(c, (px, py, c), 2 * me + c, 2 * (2 * px + py) + c) for px, py in _other_chips(x, y)]


def plan_forward(x, y, c):
    return [(2 * (2 * px + py) + c, (x, y, 1 - c), 2 * (2 * px + py) + c, 2 * (2 * px + py) + (1 - c))
            for px, py in _other_chips(x, y)]


def add_kept(g8, other, name):
    _, rows, cols = other.shape
    tr = _tile(rows, 512)

    def body(c_ref, a_ref, b_ref, o_ref):
        o_ref[...] = (a_ref[...].astype(F32) + b_ref[...].astype(F32)).astype(BF16)

    spec = pl.BlockSpec((None, tr, cols), lambda s, i, c_ref: (s, i, 0))
    grid_spec = pltpu.PrefetchScalarGridSpec(
        num_scalar_prefetch=1, grid=(4, rows // tr),
        in_specs=[pl.BlockSpec((None, tr, cols), lambda s, i, c_ref: (2 * s + c_ref[0], i, 0)), spec],
        out_specs=spec)
    return _pcall(body, name=name, grid_spec=grid_spec, out_shape=jax.ShapeDtypeStruct((4, rows, cols), BF16),
                  compiler_params=_cparams(("parallel", "parallel")))(
        lax.axis_index("c").astype(jnp.int32).reshape(1), g8, other)


def reduce_scatter_begin(grads, names, tag):
    g8 = [g.reshape(8, g.shape[1] // 2, g.shape[2]) for g in grads]
    return _chip_sums_start(g8, push(tag + "_sibling", g8, 4, plan_sibling_quarters), names, tag)


def _chip_sums_start(g8, from_sibling, names, tag):
    chip_sum = [add_kept(a, b, tag + "_add_" + n) for a, b, n in zip(g8, from_sibling, names)]
    return push_start(tag + "_start", chip_sum, chip_sum, plan_chips_by_shard, chip_sum[-1])


def sibling_exchange_begin(grads, tag):
    g8 = [g.reshape(8, g.shape[1] // 2, g.shape[2]) for g in grads]
    lands = [lax.empty((4,) + g.shape[1:], g.dtype) for g in g8]
    return push_start(tag + "_sib_start", g8, lands, plan_sibling_quarters, g8[-1])


def reduce_scatter_continue(state, names, tag, after):
    n = len(names)
    arrays = push_wait(tag + "_sib_wait", state, plan_sibling_quarters, after, both=True)
    return _chip_sums_start(arrays[:n], arrays[n:], names, tag)


def reduce_scatter_end(state, names, tag, after):
    by_chip = push_wait(tag + "_wait", state, plan_chips_by_shard, after)
    both = swap_halves(tag + "_halves", [sum_slots(q, tag + "_sum_" + n, for_swap=True)
                                         for q, n in zip(by_chip, names)])
    return [b.reshape(2 * b.shape[1], b.shape[2]) for b in both]


def _landing(mine):
    return [jnp.broadcast_to(m[None], (4,) + m.shape).reshape((8,) + m.shape[1:]) for m in mine]


def gather_begin(tag, mine, after, lands=None):
    return push_start(tag + "_start", mine, _landing(mine) if lands is None else lands, plan_chips_gather, after)


def gather_forward(tag, state, after):
    lands = push_wait(tag + "_wait", state, plan_chips_gather, after)
    return push_start(tag + "_fwd_start", lands, None, plan_forward, lands[0])


def gather_end(tag, state, after):
    bufs = push_wait(tag + "_fwd_wait", state, plan_forward, after, in_place=True)
    return [g.reshape(4, 2 * g.shape[1], g.shape[2]) for g in bufs]


PACK_QUANTUM = 8 * C


def _pack(parts, dtype):
    pieces = []
    for p in parts:
        flat = p.astype(dtype).reshape(-1)
        pieces.append(jnp.pad(flat, (0, -flat.shape[0] % PACK_QUANTUM)).reshape(-1, C))
    rows = sum(p.shape[0] for p in pieces)
    if rows % 16:
        pieces.append(jnp.zeros((8, C), dtype))
    return jnp.concatenate(pieces, axis=0)


def _unpack(packed, shapes):
    out, at = [], 0
    for s in shapes:
        n = int(np.prod(s))
        rows = -(-n // PACK_QUANTUM) * 8
        out.append(packed[at:at + rows].reshape(-1)[:n].reshape(s))
        at += rows
    return out


def all_reduce_begin(vec):
    rows, width = vec.shape
    (pair,) = swap_halves("ar_sibling", [jnp.broadcast_to(vec[None], (2, rows, width))])
    chip_sum = sum_slots(pair, "ar_sum_sibling").reshape(2, rows // 2, width)
    mine = lax.dynamic_index_in_dim(chip_sum, lax.axis_index("c"), axis=0, keepdims=True)
    return push_start("ar_start", [chip_sum], [jnp.broadcast_to(mine, (4, rows // 2, width))], plan_chips_half,
                      chip_sum)


def all_reduce_end(state, after):
    (by_chip,) = push_wait("ar_wait", state, plan_chips_half, after)
    (both,) = swap_halves("ar_halves", [sum_slots(by_chip, "ar_sum_chips", for_swap=True)])
    return both.reshape(2 * both.shape[1], both.shape[2])


def kernel(x, norm1_g, w_in, shift_mu, w0, w2, a0, a2, g2, k_k, k_a, r_k, gn_w, gn_b, sgu_ln_g, sgu_ln_b, sgu_w, sgu_b, sgu_out_g, w_out, norm2_g, w_gate, w_up, w_down, final_g, loss_target, m_norm1_g, m_w_in, m_shift_mu, m_w0, m_w2, m_a0, m_a2, m_g2, m_k_k, m_k_a, m_r_k, m_gn_w, m_gn_b, m_sgu_ln_g, m_sgu_ln_b, m_sgu_w, m_sgu_b, m_sgu_out_g, m_w_out, m_norm2_g, m_w_gate, m_w_up, m_w_down, m_final_g, v_norm1_g, v_w_in, v_shift_mu, v_w0, v_w2, v_a0, v_a2, v_g2, v_k_k, v_k_a, v_r_k, v_gn_w, v_gn_b, v_sgu_ln_g, v_sgu_ln_b, v_sgu_w, v_sgu_b, v_sgu_out_g, v_w_out, v_norm2_g, v_w_gate, v_w_up, v_w_down, v_final_g):
    args = dict(locals())
    names = ["norm1_g", "w_in", "shift_mu", "w0", "w2", "a0", "a2", "g2", "k_k", "k_a", "r_k", "gn_w", "gn_b",
             "sgu_ln_g", "sgu_ln_b", "sgu_w", "sgu_b", "sgu_out_g", "w_out", "norm2_g", "w_gate", "w_up", "w_down",
             "final_g"]
    big = ["w_in", "w_out", "w_gate", "w_up", "w_down"]
    xi, yi = lax.axis_index("x"), lax.axis_index("y")
    chip = 2 * xi + yi
    x2d, tgt = x[0], loss_target[0]
    t = x2d.shape[0]

    lora_names = ["w2", "a2", "g2"]
    first, later = ["w_in"] + lora_names, ["w_out", "w_gate", "w_up", "w_down"]

    def halves(n, token=None):
        m = args[n][0] if token is None else args[n][0] + token[0, 0]
        m = m.astype(BF16)
        return m.reshape(2, m.shape[0] // 2, m.shape[1])

    first_state, first_token = gather_begin("g0", [halves(n) for n in first], norm1_g)
    later_mine = [halves(n, first_token) for n in later]
    later_lands = _landing(later_mine)
    prepared = functools.reduce(lambda p, q: p + q, [a[0, 0, :1].astype(F32) for a in later_lands])
    first_state, first_token = gather_forward("g0", first_state, prepared)
    got = gather_end("g0", first_state, first_token)
    later_state, later_token = gather_begin("g1", later_mine, got[0], later_lands)
    norm1_g_fwd = norm1_g + later_token[0, 0]
    full = {n: blk.transpose(1, 0, 2).reshape(blk.shape[1], 4 * blk.shape[2]) for n, blk in zip(first, got)}
    w_in_f = full["w_in"]
    w_in_p = jnp.concatenate([w_in_f[:, :3 * C], w_in_f[:, 3360:], w_in_f[:, 3 * C:3360],
                              jnp.zeros((D, P_W - D_IN), BF16)], axis=1)
    mu_rkv = shift_mu[:, :3 * C]
    mu_lora = jnp.pad(shift_mu[:, 3 * C:], ((0, 0), (0, LORA_W - 288)))
    w2p = jnp.pad(full["w2"], ((0, 64), (0, 0)))
    a2p = jnp.pad(full["a2"], ((64, 0), (0, 0)))
    g2p = jnp.pad(full["g2"], ((0, 96), (0, 0)))
    ii = np.arange(C)
    jm = jnp.asarray((ii[:, None] // HEAD) == (ii[None, :] // HEAD), BF16)
    e_np = (np.arange(LANES)[:, None] == (ii[None, :] // HEAD))
    e_mat, e_t = jnp.asarray(e_np, BF16), jnp.asarray(e_np.T, BF16)
    b_t = jnp.pad(sgu_b[0].T, ((0, 0), (0, LANES - HEADS)))
    r_k_flat = r_k.reshape(1, C)
    tm_row = _tile(t, 256)
    tm_vjp = _tile(t, 128)

    (h1,) = row_fwd("norm1", rms_fn, [Tiled(x2d)], [norm1_g_fwd], [], [(D, BF16)], tm_row)
    p = mm(h1, w_in_p, name="proj_in")
    ps_rkv = shift_fwd(p, mu_rkv, 0, "shift_rkv")
    ps_lora = shift_fwd(p, mu_lora, P_LORA // LORA_W, "shift_lora")
    pre_par = [w0, w2p, a0, a2p, g2p, k_k, k_a]
    r_, w_, k2, v_, an, b_, g_ = row_fwd("pre", pre_fn, [Tiled(ps_rkv), Tiled(ps_lora)], pre_par, [jm],
                                         [(C, F32)] * 7, tm_vjp)
    v3 = to_cols(v_)
    y3, ckpt = wkv_fwd(r_, w_, k2, v3, an, b_)
    later_state, later_token = gather_forward("g1", later_state, y3)
    y_ = from_cols(y3 + later_token[0, 0])
    post_in = [Tiled(z) for z in (y_, r_, k2, v_, g_)]
    post_par = [gn_w, gn_b, r_k_flat]
    (y_rwkv,) = row_fwd("post", post_fn, post_in, post_par, [jm], [(C, BF16)], tm_vjp)
    sgu_in = [Tiled(p, C, 3), Tiled(p, C, 4)]
    sgu_par = [sgu_ln_g, sgu_ln_b, sgu_w[0], b_t, sgu_out_g]
    (y_sgu,) = row_fwd("sgu", sgu_fn, sgu_in, sgu_par, [e_mat, e_t], [(C, BF16)], tm_vjp)
    y_cat = jnp.concatenate([y_rwkv, y_sgu], axis=1)
    for n, blk in zip(later, gather_end("g1", later_state, y_cat)):
        full[n] = blk if n in ("w_gate", "w_up") else blk.reshape(4 * blk.shape[1], blk.shape[2])
    x_mid = mm(y_cat, full["w_out"], res=x2d, name="proj_out")
    (h2,) = row_fwd("norm2", rms_fn, [Tiled(x_mid)], [norm2_g], [], [(D, BF16)], tm_row)
    gate = mm(h2, full["w_gate"], shards="b", name="ffn_gate")
    up, act = mm(h2, full["w_up"], shards="b", name="ffn_up",
                 post=(lambda up_tile, gate_tile: (up_tile, swiglu_fn(gate_tile, up_tile)[0]), [gate], [F32, BF16]))
    x_out = mm(act, full["w_down"], res=x_mid, name="ffn_down")
    loss_part, dx_out, d_final_g, dx_out_b = loss_head(x_out, tgt, final_g.reshape(1, D))

    grads = {"final_g": d_final_g.reshape(D)}
    d_gate, d_up = mm(dx_out_b, full["w_down"], tb=True, name="d_act",
                      post=(lambda d_act, g_tile, u_tile: jax.vjp(swiglu_fn, g_tile, u_tile)[1]((d_act,)), [gate, up],
                            [BF16, BF16]))
    grads["w_down"] = mm(act, dx_out_b, ta=True, out_dtype=BF16, name="d_w_down")
    grads["w_gate"] = mm(h2, d_gate, shards="out", out_dtype=BF16, name="d_w_gate")
    grads["w_up"] = mm(h2, d_up, shards="out", out_dtype=BF16, name="d_w_up")
    d_h2 = mm(d_gate, full["w_gate"], shards="bt", name="d_h2_gate")
    d_h2 = mm(d_up, full["w_up"], shards="bt", res=d_h2, name="d_h2_up")
    dx_mid, grads["norm2_g"], dx_mid_b = row_bwd("norm2_b", rms_res_fn, [Tiled(x_mid)], [norm2_g], [],
                                                 [Tiled(d_h2), Tiled(dx_out)], tm_row, bf16_copy_of=(0,))
    d_ycat = mm(dx_mid_b, full["w_out"], tb=True, name="d_ycat")
    grads["w_out"] = mm(y_cat, dx_mid_b, ta=True, out_dtype=BF16, name="d_w_out")
    rs_state, rs_token = sibling_exchange_begin([grads[n].reshape((4,) + args[n].shape[1:]) for n in later], "rs2")
    d_pu, d_pv, grads["sgu_ln_g"], grads["sgu_ln_b"], d_sgu_w, d_b_t, grads["sgu_out_g"] = row_bwd(
        "sgu_b", sgu_fn, sgu_in, sgu_par, [e_mat, e_t], [Tiled(d_ycat, C, 1)], tm_vjp,
        tiled_out_dtypes=[BF16, BF16])
    grads["sgu_w"] = d_sgu_w[None]
    grads["sgu_b"] = d_b_t[:, :HEADS].T[None]
    post_par_b = [gn_w + rs_token[0, 0]] + post_par[1:]
    d_y, d_r1, d_k1, d_v1, d_g, grads["gn_w"], grads["gn_b"], d_r_k = row_bwd(
        "post_b", post_fn, post_in, post_par_b, [jm], [Tiled(d_ycat, C, 0)], tm_vjp)
    grads["r_k"] = d_r_k.reshape(r_k.shape)
    rs_state, rs_token = reduce_scatter_continue(rs_state, later, "rs2", d_y)
    d_r2, d_w, d_k2, d_a, d_b, d_v3 = wkv_bwd(r_, w_, k2, v3, an, b_, ckpt, to_cols(d_y + rs_token[0, 0]))
    d_v2 = from_cols(d_v3)
    pre_cts = [(Tiled(d_r1), Tiled(d_r2)), Tiled(d_w), (Tiled(d_k1), Tiled(d_k2)), (Tiled(d_v1), Tiled(d_v2)),
               Tiled(d_a), Tiled(d_b), Tiled(d_g)]
    d_ps_rkv, d_ps_lora, grads["w0"], d_w2p, grads["a0"], d_a2p, d_g2p, grads["k_k"], grads["k_a"] = row_bwd(
        "pre_b", pre_fn, [Tiled(ps_rkv), Tiled(ps_lora)], pre_par, [jm], pre_cts, tm_vjp)
    grads["w2"], grads["a2"], grads["g2"] = d_w2p[None, :64], d_a2p[None, 64:], d_g2p[None, :160]
    d_p_rkv, d_mu_rkv = shift_bwd(d_ps_rkv, p, mu_rkv, 0, "shift_rkv_b")
    d_p_lora, d_mu_lora = shift_bwd(d_ps_lora, p, mu_lora, P_LORA // LORA_W, "shift_lora_b")
    grads["shift_mu"] = jnp.concatenate([d_mu_rkv, d_mu_lora[:, :288]], axis=1)
    d_p = jnp.concatenate([d_p_rkv, d_pu, d_pv, d_p_lora], axis=1)
    d_h1 = mm(d_p, w_in_p, tb=True, name="d_h1")
    d_w_in_p = mm(h1, d_p, ta=True, out_dtype=BF16, name="d_w_in")
    d_w_in = jnp.concatenate([d_w_in_p[:, :3 * C], d_w_in_p[:, P_LORA:P_LORA + 288], d_w_in_p[:, 3 * C:P_LORA]],
                             axis=1)
    grads["w_in"] = d_w_in.reshape(D, 4, D_IN // 4).transpose(1, 0, 2)
    dx, grads["norm1_g"] = row_bwd("norm1_b", rms_res_fn, [Tiled(x2d)], [norm1_g], [],
                                   [Tiled(d_h1), Tiled(dx_mid)], tm_row)

    small = [n for n in names if n not in big]
    small_shapes = [(1, 64, C) if n in ("w2", "a2") else (1, 160, C) if n == "g2" else args[n].shape for n in small]
    vec = _pack([grads[n] for n in small], F32)
    w_in_state, w_in_token = reduce_scatter_begin([grads["w_in"]], ["w_in"], "rs1")
    small_state, small_token = all_reduce_begin(vec)
    total = {n: g[None] for n, g in zip(later, reduce_scatter_end(rs_state, later, "rs2", w_in_token + small_token))}
    loss = lax.psum(loss_part[0, 0], ("x", "y", "c"))
    delta, new_m, new_v = {}, {}, {}

    def adam_big(n):
        s = args[n].shape
        d_, m_, v__ = adamw(args[n][0], total[n][0], args["m_" + n][0], args["v_" + n][0], "adamw_" + n)
        delta[n], new_m[n], new_v[n] = d_.reshape(s), m_.reshape(s), v__.reshape(s)

    for n in later:
        adam_big(n)
    total["w_in"] = reduce_scatter_end(w_in_state, ["w_in"], "rs1", new_v["w_down"])[0][None]
    for n, g_sum in zip(small, _unpack(all_reduce_end(small_state, new_v["w_down"]), small_shapes)):
        total[n] = lax.dynamic_slice_in_dim(g_sum, chip * 256, 256, axis=2) if n in lora_names else g_sum
    adam_big("w_in")
    flat = [_pack([src[n] for n in small], F32)
            for src in (args, total, {n: args["m_" + n] for n in small}, {n: args["v_" + n] for n in small})]
    outs = adamw(*flat, "adamw_small")
    for res, o in zip((delta, new_m, new_v), outs):
        res.update(zip(small, _unpack(o, [args[n].shape for n in small])))
    return (loss, dx[None], *[total[n] for n in names], *[delta[n] for n in names],
            *[new_m[n] for n in names], *[new_v[n] for n in names])
```

```python
import functools

import numpy as np
import jax
import jax.numpy as jnp
from jax import lax
from jax.experimental import pallas as pl
from jax.experimental.pallas import tpu as pltpu

F32 = jnp.float32
BF16 = jnp.bfloat16

D = 2048
C = 1024
HEADS = 16
HEAD = 64
LANES = 128
P_W = 5632
P_LORA = 5120
LORA_W = 512
D_IN = 5408
D_FF = 5632
CHUNK = 128
RMS_EPS = 1e-6
LN_EPS = 1e-5
GN_EPS = 64e-5
L2_EPS = 1e-12
VMEM_LIMIT = 56 * 1024 * 1024


def _pcall(body, **kw):
    return pl.pallas_call(body, **kw)


def _cparams(sem):
    return pltpu.CompilerParams(dimension_semantics=sem, vmem_limit_bytes=VMEM_LIMIT)


def _tile(n, most):
    t = most
    while t > 8 and n % t:
        t //= 2
    assert n % t == 0, (n, most)
    return t


MM_TILE = (2048, 512, 2048)
MM_VMEM = 40 * 1024 * 1024


def _div_tile(n, most, quantum=LANES):
    for t in range(min(n, most) // quantum * quantum, 0, -quantum):
        if n % t == 0:
            return t
    raise ValueError((n, most, quantum))


def _mm_tiles(m, n, k, a_bytes, b_bytes, r_bytes, o_bytes):
    tm, tn, tk = _div_tile(m, MM_TILE[0]), _div_tile(n, MM_TILE[1]), _div_tile(k, MM_TILE[2])

    def need(tm, tk):
        return 2 * (tm * tk * a_bytes + tk * tn * b_bytes + tm * tn * (r_bytes + o_bytes)) + tm * tn * 4

    while need(tm, tk) > MM_VMEM:
        if tk >= tm and tk > 512:
            tk = _div_tile(k, tk - LANES)
        else:
            tm = _div_tile(m, tm - LANES)
    return tm, tn, tk


def mm(a, b, *, ta=False, tb=False, res=None, out_dtype=F32, name, shards=None, post=None):
    if shards is not None:
        return _mm_shards(a, b, res, out_dtype, name, shards, post)
    m, k = (a.shape[1], a.shape[0]) if ta else a.shape
    n = b.shape[0] if tb else b.shape[1]
    assert (b.shape[1] if tb else b.shape[0]) == k
    if post is None:
        tm, tn, tk = _mm_tiles(m, n, k, a.dtype.itemsize, b.dtype.itemsize,
                               0 if res is None else res.dtype.itemsize, jnp.dtype(out_dtype).itemsize)
    else:
        tm, tn, tk = _div_tile(m, 1024), _div_tile(n, MM_TILE[1]), _div_tile(k, MM_TILE[2])
    nk = k // tk
    dims = (((0 if ta else 1,), (1 if tb else 0,)), ((), ()))
    a_spec = pl.BlockSpec((tk, tm), lambda i, j, l: (l, i)) if ta else pl.BlockSpec((tm, tk), lambda i, j, l: (i, l))
    b_spec = pl.BlockSpec((tn, tk), lambda i, j, l: (j, l)) if tb else pl.BlockSpec((tk, tn), lambda i, j, l: (l, j))
    o_spec = pl.BlockSpec((tm, tn), lambda i, j, l: (i, j))
    return _mm_call(a, b, res, dims, (m // tm, n // tn, nk), a_spec, b_spec, o_spec, o_spec, (tm, tn),
                    jax.ShapeDtypeStruct((m, n), out_dtype), name, post)


def _mm_shards(a, b, res, out_dtype, name, shards, post=None):
    assert post is None or shards == "b"
    if shards == "b":
        (m, k), ns = a.shape, b.shape[2]
        tm, tk = _div_tile(m, 1024 if post is None else 512), _div_tile(k, 2048)
        grid, dims, acc = (m // tm, 4, k // tk), (((1,), (0,)), ((), ())), (tm, ns)
        a_spec = pl.BlockSpec((tm, tk), lambda i, s, l: (i, l))
        b_spec = pl.BlockSpec((None, tk, ns), lambda i, s, l: (s, l, 0))
        o_spec = pl.BlockSpec((tm, ns), lambda i, s, l: (i, s))
        out = jax.ShapeDtypeStruct((m, 4 * ns), out_dtype)
    elif shards == "bt":
        m, (_, n, ns) = a.shape[0], b.shape
        tm, tn = _div_tile(m, 2048), _div_tile(n, 512)
        grid, dims, acc = (m // tm, n // tn, 4), (((1,), (1,)), ((), ())), (tm, tn)
        a_spec = pl.BlockSpec((tm, ns), lambda i, j, s: (i, s))
        b_spec = pl.BlockSpec((None, tn, ns), lambda i, j, s: (s, j, 0))
        o_spec = pl.BlockSpec((tm, tn), lambda i, j, s: (i, j))
        out = jax.ShapeDtypeStruct((m, n), out_dtype)
    else:
        (t, m), ns = a.shape, b.shape[1] // 4
        tm, tk = _div_tile(m, 1024), _div_tile(t, 2048)
        grid, dims, acc = (m // tm, 4, t // tk), (((0,), (0,)), ((), ())), (tm, ns)
        a_spec = pl.BlockSpec((tk, tm), lambda i, s, l: (l, i))
        b_spec = pl.BlockSpec((tk, ns), lambda i, s, l: (l, s))
        o_spec = pl.BlockSpec((None, tm, ns), lambda i, s, l: (s, i, 0))
        out = jax.ShapeDtypeStruct((4, m, ns), out_dtype)
    return _mm_call(a, b, res, dims, grid, a_spec, b_spec, o_spec, o_spec, acc, out, name, post)


def _mm_call(a, b, res, dims, grid, a_spec, b_spec, r_spec, o_spec, acc_shape, out, name, post=None):
    nk = grid[2]
    post_fn, post_in, post_dtypes = (None, [], [out.dtype]) if post is None else post
    n_extra = (res is not None) + len(post_in)

    def body(*refs):
        a_ref, b_ref = refs[:2]
        r_ref = None if res is None else refs[2]
        p_refs = refs[2 + (res is not None):2 + n_extra]
        o_refs = refs[2 + n_extra:2 + n_extra + len(post_dtypes)]

        def finish(acc):
            if r_ref is not None:
                acc = acc + r_ref[...].astype(F32)
            outs = (acc,) if post_fn is None else post_fn(acc, *[p[...] for p in p_refs])
            for o_ref, o in zip(o_refs, outs):
                o_ref[...] = o.astype(o_ref.dtype)

        prod = lax.dot_general(a_ref[...].astype(BF16), b_ref[...].astype(BF16), dims, preferred_element_type=F32)
        if nk == 1:
            finish(prod)
            return
        acc_ref = refs[-1]
        kk = pl.program_id(2)

        @pl.when(kk == 0)
        def _():
            acc_ref[...] = prod

        @pl.when(kk > 0)
        def _():
            acc_ref[...] += prod

        @pl.when(kk == nk - 1)
        def _():
            finish(acc_ref[...])

    in_specs = [a_spec, b_spec] + ([r_spec] if res is not None else []) + [o_spec] * len(post_in)
    args = (a, b) + ((res,) if res is not None else ()) + tuple(post_in)
    outs = _pcall(
        body, name=name, grid=grid, in_specs=in_specs, out_specs=[o_spec] * len(post_dtypes),
        out_shape=[jax.ShapeDtypeStruct(out.shape, dt) for dt in post_dtypes],
        scratch_shapes=[] if nk == 1 else [pltpu.VMEM(acc_shape, F32)],
        compiler_params=_cparams(("parallel", "parallel", "arbitrary")),
    )(*args)
    return outs[0] if post is None else outs


class Tiled:
    def __init__(self, arr, width=None, col=0):
        self.arr, self.width, self.col = arr, (arr.shape[1] if width is None else width), col

    def spec(self, tm):
        col = self.col
        return pl.BlockSpec((tm, self.width), lambda i: (i, col))


def _full_spec(p):
    nd = p.ndim
    return pl.BlockSpec(p.shape, lambda i: (0,) * nd)


def row_fwd(name, fn, tiled, params, consts, outs, tm):
    t = tiled[0].arr.shape[0]
    n_in = len(tiled) + len(params) + len(consts)

    def body(*refs):
        res = fn(*[r[...] for r in refs[:n_in]])
        for o_ref, r in zip(refs[n_in:], res):
            o_ref[...] = r.astype(o_ref.dtype)

    return _pcall(
        body, name=name, grid=(t // tm,),
        in_specs=[x.spec(tm) for x in tiled] + [_full_spec(p) for p in params + consts],
        out_specs=[pl.BlockSpec((tm, w), lambda i: (i, 0)) for w, _ in outs],
        out_shape=[jax.ShapeDtypeStruct((t, w), dt) for w, dt in outs],
        compiler_params=_cparams(("parallel",)),
    )(*[x.arr for x in tiled], *params, *consts)


def row_bwd(name, fn, tiled, params, consts, cts, tm, tiled_out_dtypes=None, bf16_copy_of=()):
    t = tiled[0].arr.shape[0]
    nt, npar, ncon = len(tiled), len(params), len(consts)
    cts = [c if isinstance(c, tuple) else (c,) for c in cts]
    flat_cts = [x for c in cts for x in c]
    tiled_out_dtypes = tiled_out_dtypes or [F32] * nt
    copies = [tiled[j] for j in bf16_copy_of]

    def body(*refs):
        n_in = nt + npar + ncon
        ins = [r[...].astype(F32) for r in refs[:nt + npar]]
        con = [r[...] for r in refs[nt + npar:n_in]]
        ct_refs = list(refs[n_in:n_in + len(flat_cts)])
        o = refs[n_in + len(flat_cts):]
        ct = []
        for c in cts:
            parts = [ct_refs.pop(0)[...].astype(F32) for _ in c]
            ct.append(functools.reduce(lambda p, q: p + q, parts))
        _, vjp = jax.vjp(lambda *a: fn(*a, *con), *ins)
        g = vjp(tuple(ct))
        for j in range(nt):
            o[j][...] = g[j].astype(o[j].dtype)
        for n, j in enumerate(bf16_copy_of):
            o[nt + npar + n][...] = g[j].astype(BF16)
        first = pl.program_id(0) == 0

        @pl.when(first)
        def _():
            for j in range(npar):
                o[nt + j][...] = g[nt + j]

        @pl.when(jnp.logical_not(first))
        def _():
            for j in range(npar):
                o[nt + j][...] += g[nt + j]

    return _pcall(
        body, name=name, grid=(t // tm,),
        in_specs=[x.spec(tm) for x in tiled] + [_full_spec(p) for p in params + consts]
        + [x.spec(tm) for x in flat_cts],
        out_specs=[pl.BlockSpec((tm, x.width), lambda i: (i, 0)) for x in tiled] + [_full_spec(p) for p in params]
        + [pl.BlockSpec((tm, x.width), lambda i: (i, 0)) for x in copies],
        out_shape=[jax.ShapeDtypeStruct((t, x.width), dt) for x, dt in zip(tiled, tiled_out_dtypes)]
        + [jax.ShapeDtypeStruct(p.shape, F32) for p in params]
        + [jax.ShapeDtypeStruct((t, x.width), BF16) for x in copies],
        compiler_params=_cparams(("arbitrary",)),
    )(*[x.arr for x in tiled], *params, *consts, *[x.arr for x in flat_cts])


def _split_dot(x, w):
    hi = x.astype(BF16)
    lo = (x - hi.astype(F32)).astype(BF16)
    return jnp.dot(hi, w, preferred_element_type=F32) + jnp.dot(lo, w, preferred_element_type=F32)


@jax.custom_vjp
def headsum(x, j):
    return _split_dot(x, j)


def _headsum_fwd(x, j):
    return _split_dot(x, j), j


def _headsum_bwd(j, ct):
    return _split_dot(ct, j), jnp.zeros_like(j)


headsum.defvjp(_headsum_fwd, _headsum_bwd)


def _bdot(x, w):
    return jnp.dot(x.astype(BF16), w.astype(BF16), preferred_element_type=F32)


def _sigmoid(x):
    return 1.0 / (1.0 + jnp.exp(-x))


def _softplus(x):
    return jnp.maximum(x, 0.0) + jnp.log(1.0 + jnp.exp(-jnp.abs(x)))


def rms_fn(x, g):
    return (x * lax.rsqrt(jnp.mean(x * x, axis=-1, keepdims=True) + RMS_EPS) * g,)


def rms_res_fn(x, g):
    return rms_fn(x, g)[0], x


def pre_fn(rkv, lora, w0, w2p, a0, a2p, g2p, k_k, k_a, jm):
    r, k, v = rkv[:, :C], rkv[:, C:2 * C], rkv[:, 2 * C:]
    zwa, zg = lora[:, :LANES], lora[:, LANES:LANES + 256]
    w_log = -_softplus(-(w0 + _bdot(jnp.tanh(zwa), w2p))) - 0.5
    decay = jnp.exp(-jnp.exp(w_log))
    a = _sigmoid(a0 + _bdot(zwa, a2p))
    g = _bdot(_sigmoid(zg), g2p)
    kk = k * k_k
    kk = kk / jnp.maximum(jnp.sqrt(headsum(kk * kk, jm)), L2_EPS)
    k2 = k * (1.0 + (a - 1.0) * k_a)
    return r, decay, k2, v, -kk, kk * a, g


def post_fn(y, r, k2, v, g, gn_w, gn_b, r_k, jm):
    mu = headsum(y, jm) * (1.0 / HEAD)
    yc = y - mu
    var = headsum(yc * yc, jm) * (1.0 / HEAD)
    yn = yc * lax.rsqrt(var + GN_EPS) * gn_w + gn_b
    bonus = headsum(r * k2 * r_k, jm) * v
    return ((yn + bonus) * g,)


def _gelu(x):
    return 0.5 * x * (1.0 + lax.erf(x * np.float32(1.0 / np.sqrt(2.0))))


@jax.custom_vjp
def expand_groups(b_t, e, e_t):
    return _split_dot(b_t, e)


def _expand_groups_fwd(b_t, e, e_t):
    return _split_dot(b_t, e), (e, e_t)


def _expand_groups_bwd(res, ct):
    e, e_t = res
    return _split_dot(ct, e_t), jnp.zeros_like(e), jnp.zeros_like(e_t)


expand_groups.defvjp(_expand_groups_fwd, _expand_groups_bwd)


def sgu_fn(pu, pv, ln_g, ln_b, w_s, b_t, out_g, e, e_t):
    rows = pu.shape[0]
    b_exp = expand_groups(b_t, e, e_t)
    u, v = _gelu(pu), _gelu(pv)
    mu = jnp.mean(v, axis=-1, keepdims=True)
    vc = v - mu
    var = jnp.mean(vc * vc, axis=-1, keepdims=True)
    v = vc * lax.rsqrt(var + LN_EPS) * ln_g + ln_b
    tri = lax.broadcasted_iota(jnp.int32, (CHUNK, CHUNK), 0) >= lax.broadcasted_iota(jnp.int32, (CHUNK, CHUNK), 1)
    left = lax.broadcasted_iota(jnp.int32, (CHUNK, LANES), 1) < HEAD
    chunks = []
    for c0 in range(0, rows, CHUNK):
        cols = []
        for gp in range(HEADS // 2):
            vp = v[c0:c0 + CHUNK, gp * LANES:(gp + 1) * LANES]
            wa = jnp.where(tri, w_s[2 * gp], 0.0)
            wb = jnp.where(tri, w_s[2 * gp + 1], 0.0)
            cols.append(jnp.where(left, _bdot(wa, vp), _bdot(wb, vp)))
        chunks.append(jnp.concatenate(cols, axis=1) + b_exp)
    s = jnp.concatenate(chunks, axis=0) if len(chunks) > 1 else chunks[0]
    y = u * s
    return (y * lax.rsqrt(jnp.mean(y * y, axis=-1, keepdims=True) + RMS_EPS) * out_g,)


def swiglu_fn(gate, up):
    return (gate * _sigmoid(gate) * up,)


SCAN_CHUNK = 32
NB = C // LANES
STEP_GROUP = 4


def to_cols(a):
    t = a.shape[0]
    return a.reshape(t // 8, LANES, HEAD).transpose(0, 2, 1)


def from_cols(a):
    return a.transpose(0, 2, 1).reshape(a.shape[0] * 8, C)


def _scan_consts():
    i = np.arange(LANES)
    spread = (i[:, None] % 2) == (i[None, :] // HEAD)
    pick = (i[:, None] // HEAD) == (i[None, :] % 2)
    j2 = (i[:, None] // HEAD) == (i[None, :] // HEAD)
    sel = (i[None, :] // 2) == (np.arange(8 * NB * HEAD)[:, None] // HEAD)
    return (jnp.asarray(spread, BF16), jnp.asarray(pick, BF16), jnp.asarray(np.concatenate([j2, j2], 0), BF16),
            jnp.asarray(sel, BF16))


def _stack(blocks):
    return jnp.concatenate(blocks, axis=0) if len(blocks) > 1 else blocks[0]


def _unstack(x, n):
    return [x[i * HEAD:(i + 1) * HEAD] for i in range(n)]


def _headsums(blocks, j2k, group):
    res = []
    for g0 in range(0, len(blocks), group):
        x = _stack(blocks[g0:g0 + group])
        hi = x.astype(BF16)
        lo = (x - hi.astype(F32)).astype(BF16)
        out = jnp.dot(jnp.concatenate([hi, lo], axis=1), j2k, preferred_element_type=F32)
        res += _unstack(out, len(blocks[g0:g0 + group]))
    return res


def _headsums_out(blocks, pick):
    return _unstack(jnp.dot(_stack(blocks).astype(BF16), pick, preferred_element_type=F32), len(blocks))


def _expand8(tile, sel, spread):
    lhs = jnp.tile(tile.astype(BF16), (8 * NB, 1)) * sel
    return _unstack(jnp.dot(lhs, spread, preferred_element_type=F32), 8 * NB)


def _collapse(tile, blocks, first, lane_pair):
    for n, blk in enumerate(blocks):
        tile = jnp.where(jnp.tile(lane_pair == first + n, (HEAD // 8, 1)), blk, tile)
    return tile


def _row(tile, j, cb):
    return jnp.broadcast_to(tile[j:j + 1, cb * LANES:(cb + 1) * LANES], (HEAD, LANES))


def _scan_step(s_ref, rows, vexp, j2k, j, hist=None):
    w_t, k_t, a_t, b_t, r_t = rows
    s = [s_ref[:, cb * LANES:(cb + 1) * LANES] for cb in range(NB)]
    sab = _headsums([s[cb] * _row(a_t, j, cb) for cb in range(NB)], j2k, STEP_GROUP)
    out = []
    for cb in range(NB):
        sl = slice(cb * LANES, (cb + 1) * LANES)
        s_new = s[cb] * _row(w_t, j, cb) + sab[cb] * _row(b_t, j, cb) + vexp[j * NB + cb] * _row(k_t, j, cb)
        s_ref[:, sl] = s_new
        if hist is None:
            out.append(s_new * _row(r_t, j, cb))
        else:
            s_hist, sab_hist, idx = hist
            s_hist[idx + 1, :, sl] = s_new
            sab_hist[idx, :, sl] = sab[cb]
    return out


def wkv_fwd(r, w, k, v3, a, b):
    t = r.shape[0]
    nc = t // SCAN_CHUNK
    n8 = SCAN_CHUNK // 8

    def body(r_ref, w_ref, k_ref, a_ref, b_ref, v3_ref, spread_ref, pick_ref, j2k_ref, sel_ref, y3_ref, ck_ref,
             s_ref):
        @pl.when(pl.program_id(0) == 0)
        def _():
            s_ref[...] = jnp.zeros_like(s_ref)

        ck_ref[0] = s_ref[...]
        spread, pick, j2k, sel = spread_ref[...], pick_ref[...], j2k_ref[...], sel_ref[...]
        lane_pair = lax.broadcasted_iota(jnp.int32, (8, LANES), 1) // 2

        def t8_body(t8, carry):
            row0 = pl.multiple_of(t8 * 8, 8)
            rows = [ref[pl.ds(row0, 8), :] for ref in (w_ref, k_ref, a_ref, b_ref, r_ref)]
            vexp = _expand8(v3_ref[t8], sel, spread)
            y3 = jnp.zeros((HEAD, LANES), F32)
            for j in range(8):
                y3 = _collapse(y3, _headsums_out(_scan_step(s_ref, rows, vexp, j2k, j), pick), j * NB, lane_pair)
            y3_ref[t8] = y3
            return carry

        lax.fori_loop(0, n8, t8_body, 0)

    row_spec = pl.BlockSpec((SCAN_CHUNK, C), lambda i: (i, 0))
    col_spec = pl.BlockSpec((n8, HEAD, LANES), lambda i: (i, 0, 0))
    consts = _scan_consts()
    return _pcall(
        body, name="wkv_fwd", grid=(nc,),
        in_specs=[row_spec] * 5 + [col_spec] + [pl.BlockSpec(c.shape, lambda i: (0, 0)) for c in consts],
        out_specs=[col_spec, pl.BlockSpec((1, HEAD, C), lambda i: (i, 0, 0))],
        out_shape=[jax.ShapeDtypeStruct((t // 8, HEAD, LANES), F32), jax.ShapeDtypeStruct((nc, HEAD, C), F32)],
        scratch_shapes=[pltpu.VMEM((HEAD, C), F32)],
        compiler_params=_cparams(("arbitrary",)),
    )(r, w, k, a, b, v3, *consts)


def wkv_bwd(r, w, k, v3, a, b, ckpt, dy3):
    t = r.shape[0]
    nc = t // SCAN_CHUNK
    n8 = SCAN_CHUNK // 8

    def body(r_ref, w_ref, k_ref, a_ref, b_ref, v3_ref, dy3_ref, ck_ref, spread_ref, pick_ref, j2k_ref, sel_ref,
             dr_ref, dw_ref, dk_ref, da_ref, db_ref, dv3_ref, s_ref, g_ref, s_hist, sab_hist):
        @pl.when(pl.program_id(0) == 0)
        def _():
            g_ref[...] = jnp.zeros_like(g_ref)

        spread, pick, j2k, sel = spread_ref[...], pick_ref[...], j2k_ref[...], sel_ref[...]
        lane_pair = lax.broadcasted_iota(jnp.int32, (8, LANES), 1) // 2
        sub = lax.broadcasted_iota(jnp.int32, (8, LANES), 0)
        s_ref[...] = ck_ref[0]
        s_hist[0] = ck_ref[0]

        def redo(t8, carry):
            row0 = pl.multiple_of(t8 * 8, 8)
            rows = [ref[pl.ds(row0, 8), :] for ref in (w_ref, k_ref, a_ref, b_ref, r_ref)]
            vexp = _expand8(v3_ref[t8], sel, spread)
            for j in range(8):
                _scan_step(s_ref, rows, vexp, j2k, j, hist=(s_hist, sab_hist, t8 * 8 + j))
            return carry

        lax.fori_loop(0, n8, redo, 0)

        def back(q, carry):
            t8 = n8 - 1 - q
            row0 = pl.multiple_of(t8 * 8, 8)
            w_t, k_t, a_t, b_t, r_t = [ref[pl.ds(row0, 8), :] for ref in (w_ref, k_ref, a_ref, b_ref, r_ref)]
            vexp = _expand8(v3_ref[t8], sel, spread)
            dyexp = _expand8(dy3_ref[t8], sel, spread)
            dv3 = jnp.zeros((HEAD, LANES), F32)
            tiles = [[jnp.zeros((8, LANES), F32) for _ in range(NB)] for _ in range(5)]
            for j in range(7, -1, -1):
                idx = t8 * 8 + j
                g = [g_ref[:, cb * LANES:(cb + 1) * LANES] + dyexp[j * NB + cb] * _row(r_t, j, cb)
                     for cb in range(NB)]
                dsab = _headsums([g[cb] * _row(b_t, j, cb) for cb in range(NB)], j2k, STEP_GROUP)
                dv3 = _collapse(dv3, _headsums_out([g[cb] * _row(k_t, j, cb) for cb in range(NB)], pick), j * NB,
                                lane_pair)
                for cb in range(NB):
                    sl = slice(cb * LANES, (cb + 1) * LANES)
                    s_new = s_hist[idx + 1, :, sl]
                    s_old = s_hist[idx, :, sl]
                    sab = sab_hist[idx, :, sl]
                    sums = (s_new * dyexp[j * NB + cb], g[cb] * s_old, g[cb] * vexp[j * NB + cb],
                            s_old * dsab[cb], g[cb] * sab)
                    for n, prod in enumerate(sums):
                        rowsum = jnp.broadcast_to(jnp.sum(prod, axis=0, keepdims=True), (8, LANES))
                        tiles[n][cb] = jnp.where(sub == j, rowsum, tiles[n][cb])
                    g_ref[:, sl] = g[cb] * _row(w_t, j, cb) + dsab[cb] * _row(a_t, j, cb)
            dv3_ref[t8] = dv3
            for n, ref in enumerate((dr_ref, dw_ref, dk_ref, da_ref, db_ref)):
                for cb in range(NB):
                    ref[pl.ds(row0, 8), cb * LANES:(cb + 1) * LANES] = tiles[n][cb]
            return carry

        lax.fori_loop(0, n8, back, 0)

    row_spec = pl.BlockSpec((SCAN_CHUNK, C), lambda i: (nc - 1 - i, 0))
    col_spec = pl.BlockSpec((n8, HEAD, LANES), lambda i: (nc - 1 - i, 0, 0))
    consts = _scan_consts()
    return _pcall(
        body, name="wkv_bwd", grid=(nc,),
        in_specs=[row_spec] * 5 + [col_spec, col_spec, pl.BlockSpec((1, HEAD, C), lambda i: (nc - 1 - i, 0, 0))]
        + [pl.BlockSpec(c.shape, lambda i: (0, 0)) for c in consts],
        out_specs=[row_spec] * 5 + [col_spec],
        out_shape=[jax.ShapeDtypeStruct((t, C), F32)] * 5 + [jax.ShapeDtypeStruct((t // 8, HEAD, LANES), F32)],
        scratch_shapes=[pltpu.VMEM((HEAD, C), F32), pltpu.VMEM((HEAD, C), F32),
                        pltpu.VMEM((SCAN_CHUNK + 1, HEAD, C), F32), pltpu.VMEM((SCAN_CHUNK, HEAD, C), F32)],
        compiler_params=_cparams(("arbitrary",)),
    )(r, w, k, a, b, v3, dy3, ckpt, *consts)


def _prev_rows(cur, before, first_tile):
    last = jnp.where(first_tile, 0.0, before[7:8, :])
    row = lax.broadcasted_iota(jnp.int32, cur.shape, 0)
    return jnp.where(row == 0, last, pltpu.roll(cur, 1, 0))


def shift_fwd(p, mu, col, name):
    t, width = p.shape[0], mu.shape[1]
    tm = _tile(t, 256)

    def body(p_ref, before_ref, mu_ref, o_ref):
        cur = p_ref[...]
        prev = _prev_rows(cur, before_ref[...], pl.program_id(0) == 0)
        o_ref[...] = cur + (prev - cur) * mu_ref[...]

    return _pcall(
        body, name=name, grid=(t // tm,),
        in_specs=[pl.BlockSpec((tm, width), lambda i: (i, col)),
                  pl.BlockSpec((8, width), lambda i: (jnp.maximum(i * (tm // 8) - 1, 0), col)),
                  pl.BlockSpec((1, width), lambda i: (0, 0))],
        out_specs=pl.BlockSpec((tm, width), lambda i: (i, 0)),
        out_shape=jax.ShapeDtypeStruct((t, width), F32),
        compiler_params=_cparams(("parallel",)),
    )(p, p, mu)


def shift_bwd(dps, p, mu, col, name):
    t, width = dps.shape
    tm = _tile(t, 256)
    nt = t // tm

    def body(d_ref, after_ref, p_ref, before_ref, mu_ref, dp_ref, dmu_ref):
        i = pl.program_id(0)
        d, cur, mu_v = d_ref[...], p_ref[...], mu_ref[...]
        row = lax.broadcasted_iota(jnp.int32, d.shape, 0)
        first_after = jnp.where(i == nt - 1, 0.0, after_ref[0:1, :])
        nxt = jnp.where(row == tm - 1, first_after, pltpu.roll(d, tm - 1, 0))
        dp_ref[...] = (d * (1.0 - mu_v) + nxt * mu_v).astype(BF16)
        prev = _prev_rows(cur, before_ref[...], i == 0)
        part = jnp.sum(d * (prev - cur), axis=0, keepdims=True)

        @pl.when(i == 0)
        def _():
            dmu_ref[...] = part

        @pl.when(i > 0)
        def _():
            dmu_ref[...] += part

    return _pcall(
        body, name=name, grid=(nt,),
        in_specs=[pl.BlockSpec((tm, width), lambda i: (i, 0)),
                  pl.BlockSpec((8, width), lambda i: (jnp.minimum((i + 1) * (tm // 8), t // 8 - 1), 0)),
                  pl.BlockSpec((tm, width), lambda i: (i, col)),
                  pl.BlockSpec((8, width), lambda i: (jnp.maximum(i * (tm // 8) - 1, 0), col)),
                  pl.BlockSpec((1, width), lambda i: (0, 0))],
        out_specs=[pl.BlockSpec((tm, width), lambda i: (i, 0)), pl.BlockSpec((1, width), lambda i: (0, 0))],
        out_shape=[jax.ShapeDtypeStruct((t, width), BF16), jax.ShapeDtypeStruct((1, width), F32)],
        compiler_params=_cparams(("arbitrary",)),
    )(dps, dps, p, p, mu)


def loss_head(x3, tgt, g):
    t, d = x3.shape
    tm = _tile(t, 256)

    def body(x_ref, t_ref, g_ref, loss_ref, dx_ref, dg_ref, dxb_ref):
        (y,), vjp = jax.vjp(lambda a, b: rms_fn(a, b), x_ref[...], g_ref[...])
        diff = y - t_ref[...]
        part = 0.5 * jnp.sum(jnp.mean(diff * diff, axis=-1, keepdims=True), axis=0, keepdims=True)
        dx, dg = vjp((diff * (1.0 / d),))
        dx_ref[...] = dx
        dxb_ref[...] = dx.astype(BF16)
        part = jnp.broadcast_to(part, (1, LANES))
        first = pl.program_id(0) == 0

        @pl.when(first)
        def _():
            loss_ref[...] = part
            dg_ref[...] = dg

        @pl.when(jnp.logical_not(first))
        def _():
            loss_ref[...] += part
            dg_ref[...] += dg

    row = pl.BlockSpec((tm, d), lambda i: (i, 0))
    vec = pl.BlockSpec((1, d), lambda i: (0, 0))
    return _pcall(
        body, name="loss_head", grid=(t // tm,), in_specs=[row, row, vec],
        out_specs=[pl.BlockSpec((1, LANES), lambda i: (0, 0)), row, vec, row],
        out_shape=[jax.ShapeDtypeStruct((1, LANES), F32), jax.ShapeDtypeStruct((t, d), F32),
                   jax.ShapeDtypeStruct((1, d), F32), jax.ShapeDtypeStruct((t, d), BF16)],
        compiler_params=_cparams(("arbitrary",)),
    )(x3, tgt, g)


ADAM_LR, ADAM_B1, ADAM_B2, ADAM_EPS, ADAM_WD, ADAM_STEP = 0.001, 0.9, 0.999, 1e-08, 0.01, 10


def adamw(w, g, m, v, name):
    rows, width = w.shape
    tr = _tile(rows, 256)

    def body(w_ref, g_ref, m_ref, v_ref, d_ref, nm_ref, nv_ref):
        gv = g_ref[...]
        m_new = ADAM_B1 * m_ref[...] + (1.0 - ADAM_B1) * gv
        v_new = ADAM_B2 * v_ref[...] + (1.0 - ADAM_B2) * (gv * gv)
        m_hat = m_new / (1.0 - ADAM_B1 ** ADAM_STEP)
        v_hat = v_new / (1.0 - ADAM_B2 ** ADAM_STEP)
        d_ref[...] = -ADAM_LR * (m_hat / (jnp.sqrt(v_hat) + ADAM_EPS) + ADAM_WD * w_ref[...])
        nm_ref[...] = m_new
        nv_ref[...] = v_new

    spec = pl.BlockSpec((tr, width), lambda i: (i, 0))
    return _pcall(
        body, name=name, grid=(rows // tr,), in_specs=[spec] * 4, out_specs=[spec] * 3,
        out_shape=[jax.ShapeDtypeStruct((rows, width), F32)] * 3,
        compiler_params=_cparams(("parallel",)),
    )(w, g, m, v)


def sum_slots(q, name, for_swap=False):
    n, rows, width = q.shape
    tr = _tile(rows, 512)

    def body(*refs):
        refs = refs[for_swap:]
        acc = refs[0][...].astype(F32)
        for r in refs[1:n]:
            acc = acc + r[...].astype(F32)
        refs[n][...] = acc

    if not for_swap:
        specs = [pl.BlockSpec((None, tr, width), functools.partial(lambda s, i: (s, i, 0), s)) for s in range(n)]
        return _pcall(body, name=name, grid=(rows // tr,), in_specs=specs,
                      out_specs=pl.BlockSpec((tr, width), lambda i: (i, 0)),
                      out_shape=jax.ShapeDtypeStruct((rows, width), F32),
                      compiler_params=_cparams(("parallel",)))(*([q] * n))
    specs = [pl.BlockSpec((None, tr, width), functools.partial(lambda s, i, c_ref: (s, i, 0), s)) for s in range(n)]
    grid_spec = pltpu.PrefetchScalarGridSpec(
        num_scalar_prefetch=1, grid=(rows // tr,), in_specs=specs,
        out_specs=pl.BlockSpec((None, tr, width), lambda i, c_ref: (c_ref[0], i, 0)))
    return _pcall(body, name=name, grid_spec=grid_spec, out_shape=jax.ShapeDtypeStruct((2, rows, width), F32),
                  compiler_params=_cparams(("parallel",)))(
        lax.axis_index("c").astype(jnp.int32).reshape(1), *([q] * n))


MESH = pl.DeviceIdType.MESH
ANY_SPEC = pl.BlockSpec(memory_space=pl.ANY)


def _coords():
    return lax.axis_index("x"), lax.axis_index("y"), lax.axis_index("c")


def _other_chips(x, y):
    return [(1 - x, y), (x, 1 - y), (1 - x, 1 - y)]


def push(name, srcs, n_dst, plan_fn):
    n_arr = len(srcs)
    n_send = len(plan_fn(0, 0, 0))

    def body(*refs):
        src_refs, dst_refs = refs[:n_arr], refs[n_arr:2 * n_arr]
        send_sems, recv_sems = refs[2 * n_arr:]
        sends = plan_fn(*_coords())
        out = []
        for q, (src_ref, dst_ref) in enumerate(zip(src_refs, dst_refs)):
            out += [pltpu.make_async_remote_copy(src_ref.at[si], dst_ref.at[di], send_sems.at[q * n_send + k],
                                                 recv_sems.at[q * n_send + k], device_id=peer, device_id_type=MESH)
                    for k, (si, peer, di, _) in enumerate(sends)]
        for cp in out:
            cp.start()
        for q, (src_ref, dst_ref) in enumerate(zip(src_refs, dst_refs)):
            for k, (si, peer, _, ri) in enumerate(sends):
                pltpu.make_async_remote_copy(src_ref.at[si], dst_ref.at[ri], send_sems.at[q * n_send + k],
                                             recv_sems.at[q * n_send + k], device_id=peer,
                                             device_id_type=MESH).wait_recv()
        for cp in out:
            cp.wait_send()

    return _pcall(
        body, name=name, in_specs=[ANY_SPEC] * n_arr, out_specs=[ANY_SPEC] * n_arr,
        out_shape=[jax.ShapeDtypeStruct((n_dst,) + s.shape[1:], s.dtype) for s in srcs],
        scratch_shapes=[pltpu.SemaphoreType.DMA((n_arr * n_send,)), pltpu.SemaphoreType.DMA((n_arr * n_send,))],
    )(*srcs)


def plan_sibling_quarters(x, y, c):
    return [(2 * s + (1 - c), (x, y, 1 - c), s, s) for s in range(4)]


def plan_chips_by_shard(x, y, c):
    me = 2 * x + y
    return [(2 * px + py, (px, py, c), me, 2 * px + py) for px, py in _other_chips(x, y)]


def plan_chips_half(x, y, c):
    me = 2 * x + y
    return [(c, (px, py, c), me, 2 * px + py) for px, py in _other_chips(x, y)]


def swap_halves(name, bufs):
    n_arr = len(bufs)

    def body(*refs):
        buf_refs = refs[n_arr:2 * n_arr]
        send_sems, recv_sems = refs[2 * n_arr:]
        x, y, c = _coords()
        out = [pltpu.make_async_remote_copy(b.at[c], b.at[c], send_sems.at[q], recv_sems.at[q],
                                            device_id=(x, y, 1 - c), device_id_type=MESH)
               for q, b in enumerate(buf_refs)]
        for cp in out:
            cp.start()
        for q, b in enumerate(buf_refs):
            pltpu.make_async_remote_copy(b.at[1 - c], b.at[1 - c], send_sems.at[q], recv_sems.at[q],
                                         device_id=(x, y, 1 - c), device_id_type=MESH).wait_recv()
        for cp in out:
            cp.wait_send()

    return _pcall(
        body, name=name, in_specs=[ANY_SPEC] * n_arr, out_specs=[ANY_SPEC] * n_arr,
        out_shape=[jax.ShapeDtypeStruct(b.shape, b.dtype) for b in bufs],
        input_output_aliases={q: q for q in range(n_arr)},
        scratch_shapes=[pltpu.SemaphoreType.DMA((n_arr,)), pltpu.SemaphoreType.DMA((n_arr,))],
    )(*bufs)


HBM_SPEC = pl.BlockSpec(memory_space=pltpu.HBM)
SEM_SPEC = pl.BlockSpec(memory_space=pltpu.SEMAPHORE)
DATAFLOW = pltpu.SideEffectType.DATAFLOW_SIDE_EFFECTING


def _in_hbm(a):
    return pltpu.with_memory_space_constraint(a, pltpu.HBM)


def push_start(name, srcs, lands, plan_fn, after):
    n_arr = len(srcs)
    n_send = len(plan_fn(0, 0, 0))
    arrays = list(srcs) + ([] if lands is None else list(lands))
    if after is None:
        after = jnp.zeros((8, LANES), F32)
    n_all = len(arrays)

    def body(*refs):
        src_refs, land_refs = refs[:n_arr], refs[n_all - n_arr:n_all]
        send_sems, recv_sems = refs[n_all + 1:n_all + 3]
        token = refs[-1]
        sends = plan_fn(*_coords())
        for q, (src_ref, land_ref) in enumerate(zip(src_refs, land_refs)):
            for k, (si, peer, di, _) in enumerate(sends):
                pltpu.make_async_remote_copy(src_ref.at[si], land_ref.at[di], send_sems.at[q * n_send + k],
                                             recv_sems.at[q * n_send + k], device_id=peer,
                                             device_id_type=MESH).start()
        token[...] = jnp.zeros_like(token)

    sems = pltpu.SemaphoreType.DMA((n_arr * n_send,))
    out = _pcall(
        body, name=name,
        in_specs=[HBM_SPEC] * n_all + [ANY_SPEC],
        out_specs=[SEM_SPEC, SEM_SPEC] + [HBM_SPEC] * n_all + [pl.BlockSpec(memory_space=pltpu.VMEM)],
        out_shape=[sems, sems] + [pltpu.HBM(a.shape, a.dtype) for a in arrays]
        + [jax.ShapeDtypeStruct((8, LANES), F32)],
        input_output_aliases={q: 2 + q for q in range(n_all)},
        compiler_params=pltpu.CompilerParams(has_side_effects=DATAFLOW),
    )(*[_in_hbm(a) for a in arrays], after)
    return out[:-1], out[-1]


def push_wait(name, state, plan_fn, after, in_place=False, both=False):
    arrays = list(state[2:])
    n_all = len(arrays)
    n_arr = n_all if in_place else n_all // 2
    n_send = len(plan_fn(0, 0, 0))
    send_sems, recv_sems = state[:2]

    def body(*refs):
        src_refs, land_refs = refs[:n_arr], refs[n_all - n_arr:n_all]
        send_ref, recv_ref = refs[n_all:n_all + 2]
        sends = plan_fn(*_coords())
        for q, (src_ref, land_ref) in enumerate(zip(src_refs, land_refs)):
            for k, (si, peer, _, ri) in enumerate(sends):
                cp = pltpu.make_async_remote_copy(src_ref.at[si], land_ref.at[ri], send_ref.at[q * n_send + k],
                                                  recv_ref.at[q * n_send + k], device_id=peer, device_id_type=MESH)
                cp.wait_send()
                cp.wait_recv()

    out = _pcall(
        body, name=name,
        in_specs=[HBM_SPEC] * n_all + [SEM_SPEC, SEM_SPEC, ANY_SPEC],
        out_specs=[HBM_SPEC] * n_all,
        out_shape=[pltpu.HBM(a.shape, a.dtype) for a in arrays],
        input_output_aliases={q: q for q in range(n_all)},
        compiler_params=pltpu.CompilerParams(has_side_effects=DATAFLOW),
    )(*arrays, send_sems, recv_sems, after)
    return list(out) if both else list(out[n_all - n_arr:])


def plan_chips_gather(x, y, c):
    me = 2 * x + y
    return [(c, (px, py, c), 2 * me + c, 2 * (2 * px + py) + c) for px, py in _other_chips(x, y)]


def plan_forward(x, y, c):
    return [(2 * (2 * px + py) + c, (x, y, 1 - c), 2 * (2 * px + py) + c, 2 * (2 * px + py) + (1 - c))
            for px, py in _other_chips(x, y)]


def add_kept(g8, other, name):
    _, rows, cols = other.shape
    tr = _tile(rows, 512)

    def body(c_ref, a_ref, b_ref, o_ref):
        o_ref[...] = (a_ref[...].astype(F32) + b_ref[...].astype(F32)).astype(BF16)

    spec = pl.BlockSpec((None, tr, cols), lambda s, i, c_ref: (s, i, 0))
    grid_spec = pltpu.PrefetchScalarGridSpec(
        num_scalar_prefetch=1, grid=(4, rows // tr),
        in_specs=[pl.BlockSpec((None, tr, cols), lambda s, i, c_ref: (2 * s + c_ref[0], i, 0)), spec],
        out_specs=spec)
    return _pcall(body, name=name, grid_spec=grid_spec, out_shape=jax.ShapeDtypeStruct((4, rows, cols), BF16),
                  compiler_params=_cparams(("parallel", "parallel")))(
        lax.axis_index("c").astype(jnp.int32).reshape(1), g8, other)


def reduce_scatter_begin(grads, names, tag):
    g8 = [g.reshape(8, g.shape[1] // 2, g.shape[2]) for g in grads]
    return _chip_sums_start(g8, push(tag + "_sibling", g8, 4, plan_sibling_quarters), names, tag)


def _chip_sums_start(g8, from_sibling, names, tag):
    chip_sum = [add_kept(a, b, tag + "_add_" + n) for a, b, n in zip(g8, from_sibling, names)]
    return push_start(tag + "_start", chip_sum, chip_sum, plan_chips_by_shard, None)


def sibling_exchange_begin(grads, tag):
    g8 = [g.reshape(8, g.shape[1] // 2, g.shape[2]) for g in grads]
    lands = [lax.empty((4,) + g.shape[1:], g.dtype) for g in g8]
    return push_start(tag + "_sib_start", g8, lands, plan_sibling_quarters, None)


def reduce_scatter_continue(state, names, tag, after):
    n = len(names)
    arrays = push_wait(tag + "_sib_wait", state, plan_sibling_quarters, after, both=True)
    return _chip_sums_start(arrays[:n], arrays[n:], names, tag)


def reduce_scatter_end(state, names, tag, after):
    by_chip = push_wait(tag + "_wait", state, plan_chips_by_shard, after)
    both = swap_halves(tag + "_halves", [sum_slots(q, tag + "_sum_" + n, for_swap=True)
                                         for q, n in zip(by_chip, names)])
    return [b.reshape(2 * b.shape[1], b.shape[2]) for b in both]


def _landing(mine):
    return [jnp.broadcast_to(m[None], (4,) + m.shape).reshape((8,) + m.shape[1:]) for m in mine]


def gather_begin(tag, mine, after, lands=None):
    return push_start(tag + "_start", mine, _landing(mine) if lands is None else lands, plan_chips_gather, after)


def gather_forward(tag, state, after):
    lands = push_wait(tag + "_wait", state, plan_chips_gather, after)
    return push_start(tag + "_fwd_start", lands, None, plan_forward, None)


def gather_end(tag, state, after):
    bufs = push_wait(tag + "_fwd_wait", state, plan_forward, after, in_place=True)
    return [g.reshape(4, 2 * g.shape[1], g.shape[2]) for g in bufs]


PACK_QUANTUM = 8 * C


def _pack(parts, dtype):
    pieces = []
    for p in parts:
        flat = p.astype(dtype).reshape(-1)
        pieces.append(jnp.pad(flat, (0, -flat.shape[0] % PACK_QUANTUM)).reshape(-1, C))
    rows = sum(p.shape[0] for p in pieces)
    if rows % 16:
        pieces.append(jnp.zeros((8, C), dtype))
    return jnp.concatenate(pieces, axis=0)


def _unpack(packed, shapes):
    out, at = [], 0
    for s in shapes:
        n = int(np.prod(s))
        rows = -(-n // PACK_QUANTUM) * 8
        out.append(packed[at:at + rows].reshape(-1)[:n].reshape(s))
        at += rows
    return out


def all_reduce_begin(vec):
    rows, width = vec.shape
    (pair,) = swap_halves("ar_sibling", [jnp.broadcast_to(vec[None], (2, rows, width))])
    chip_sum = sum_slots(pair, "ar_sum_sibling").reshape(2, rows // 2, width)
    mine = lax.dynamic_index_in_dim(chip_sum, lax.axis_index("c"), axis=0, keepdims=True)
    return push_start("ar_start", [chip_sum], [jnp.broadcast_to(mine, (4, rows // 2, width))], plan_chips_half,
                      None)


def all_reduce_end(state, after):
    (by_chip,) = push_wait("ar_wait", state, plan_chips_half, after)
    (both,) = swap_halves("ar_halves", [sum_slots(by_chip, "ar_sum_chips", for_swap=True)])
    return both.reshape(2 * both.shape[1], both.shape[2])


def kernel(x, norm1_g, w_in, shift_mu, w0, w2, a0, a2, g2, k_k, k_a, r_k, gn_w, gn_b, sgu_ln_g, sgu_ln_b, sgu_w, sgu_b, sgu_out_g, w_out, norm2_g, w_gate, w_up, w_down, final_g, loss_target, m_norm1_g, m_w_in, m_shift_mu, m_w0, m_w2, m_a0, m_a2, m_g2, m_k_k, m_k_a, m_r_k, m_gn_w, m_gn_b, m_sgu_ln_g, m_sgu_ln_b, m_sgu_w, m_sgu_b, m_sgu_out_g, m_w_out, m_norm2_g, m_w_gate, m_w_up, m_w_down, m_final_g, v_norm1_g, v_w_in, v_shift_mu, v_w0, v_w2, v_a0, v_a2, v_g2, v_k_k, v_k_a, v_r_k, v_gn_w, v_gn_b, v_sgu_ln_g, v_sgu_ln_b, v_sgu_w, v_sgu_b, v_sgu_out_g, v_w_out, v_norm2_g, v_w_gate, v_w_up, v_w_down, v_final_g):
    args = dict(locals())
    names = ["norm1_g", "w_in", "shift_mu", "w0", "w2", "a0", "a2", "g2", "k_k", "k_a", "r_k", "gn_w", "gn_b",
             "sgu_ln_g", "sgu_ln_b", "sgu_w", "sgu_b", "sgu_out_g", "w_out", "norm2_g", "w_gate", "w_up", "w_down",
             "final_g"]
    big = ["w_in", "w_out", "w_gate", "w_up", "w_down"]
    xi, yi = lax.axis_index("x"), lax.axis_index("y")
    chip = 2 * xi + yi
    x2d, tgt = x[0], loss_target[0]
    t = x2d.shape[0]

    lora_names = ["w2", "a2", "g2"]
    first, later = ["w_in"] + lora_names, ["w_out", "w_gate", "w_up", "w_down"]

    def halves(n, token=None):
        m = args[n][0] if token is None else args[n][0] + token[0, 0]
        m = m.astype(BF16)
        return m.reshape(2, m.shape[0] // 2, m.shape[1])

    first_state, first_token = gather_begin("g0", [halves(n) for n in first], norm1_g)
    later_mine = [halves(n, first_token) for n in later]
    later_lands = _landing(later_mine)
    (h1,) = row_fwd("norm1", rms_fn, [Tiled(x2d)], [norm1_g + first_token[0, 0]], [], [(D, BF16)], _tile(t, 256))
    prepared = functools.reduce(lambda p, q: p + q, [a[0, 0, :1].astype(F32) for a in later_lands])
    first_state, first_token = gather_forward("g0", first_state, prepared + h1[0, :1].astype(F32))
    got = gather_end("g0", first_state, first_token)
    later_state, later_token = gather_begin("g1", later_mine, got[0], later_lands)
    full = {n: blk.transpose(1, 0, 2).reshape(blk.shape[1], 4 * blk.shape[2]) for n, blk in zip(first, got)}
    w_in_f = full["w_in"]
    w_in_p = jnp.concatenate([w_in_f[:, :3 * C], w_in_f[:, 3360:], w_in_f[:, 3 * C:3360],
                              jnp.zeros((D, P_W - D_IN), BF16)], axis=1)
    mu_rkv = shift_mu[:, :3 * C] + later_token[0, 0]
    mu_lora = jnp.pad(shift_mu[:, 3 * C:], ((0, 0), (0, LORA_W - 288)))
    w2p = jnp.pad(full["w2"], ((0, 64), (0, 0)))
    a2p = jnp.pad(full["a2"], ((64, 0), (0, 0)))
    g2p = jnp.pad(full["g2"], ((0, 96), (0, 0)))
    ii = np.arange(C)
    jm = jnp.asarray((ii[:, None] // HEAD) == (ii[None, :] // HEAD), BF16)
    e_np = (np.arange(LANES)[:, None] == (ii[None, :] // HEAD))
    e_mat, e_t = jnp.asarray(e_np, BF16), jnp.asarray(e_np.T, BF16)
    b_t = jnp.pad(sgu_b[0].T, ((0, 0), (0, LANES - HEADS)))
    r_k_flat = r_k.reshape(1, C)
    tm_row = _tile(t, 256)
    tm_vjp = _tile(t, 128)

    p = mm(h1, w_in_p, name="proj_in")
    ps_rkv = shift_fwd(p, mu_rkv, 0, "shift_rkv")
    ps_lora = shift_fwd(p, mu_lora, P_LORA // LORA_W, "shift_lora")
    pre_par = [w0, w2p, a0, a2p, g2p, k_k, k_a]
    r_, w_, k2, v_, an, b_, g_ = row_fwd("pre", pre_fn, [Tiled(ps_rkv), Tiled(ps_lora)], pre_par, [jm],
                                         [(C, F32)] * 7, tm_vjp)
    v3 = to_cols(v_)
    y3, ckpt = wkv_fwd(r_, w_, k2, v3, an, b_)
    later_state, later_token = gather_forward("g1", later_state, y3)
    y_ = from_cols(y3 + later_token[0, 0])
    post_in = [Tiled(z) for z in (y_, r_, k2, v_, g_)]
    post_par = [gn_w, gn_b, r_k_flat]
    (y_rwkv,) = row_fwd("post", post_fn, post_in, post_par, [jm], [(C, BF16)], tm_vjp)
    sgu_in = [Tiled(p, C, 3), Tiled(p, C, 4)]
    sgu_par = [sgu_ln_g, sgu_ln_b, sgu_w[0], b_t, sgu_out_g]
    (y_sgu,) = row_fwd("sgu", sgu_fn, sgu_in, sgu_par, [e_mat, e_t], [(C, BF16)], tm_vjp)
    y_cat = jnp.concatenate([y_rwkv, y_sgu], axis=1)
    for n, blk in zip(later, gather_end("g1", later_state, y_cat)):
        full[n] = blk if n in ("w_gate", "w_up") else blk.reshape(4 * blk.shape[1], blk.shape[2])
    x_mid = mm(y_cat, full["w_out"], res=x2d, name="proj_out")
    (h2,) = row_fwd("norm2", rms_fn, [Tiled(x_mid)], [norm2_g], [], [(D, BF16)], tm_row)
    gate = mm(h2, full["w_gate"], shards="b", name="ffn_gate")
    up, act = mm(h2, full["w_up"], shards="b", name="ffn_up",
                 post=(lambda up_tile, gate_tile: (up_tile, swiglu_fn(gate_tile, up_tile)[0]), [gate], [F32, BF16]))
    x_out = mm(act, full["w_down"], res=x_mid, name="ffn_down")
    loss_part, dx_out, d_final_g, dx_out_b = loss_head(x_out, tgt, final_g.reshape(1, D))

    grads = {"final_g": d_final_g.reshape(D)}
    d_gate, d_up = mm(dx_out_b, full["w_down"], tb=True, name="d_act",
                      post=(lambda d_act, g_tile, u_tile: jax.vjp(swiglu_fn, g_tile, u_tile)[1]((d_act,)), [gate, up],
                            [BF16, BF16]))
    grads["w_down"] = mm(act, dx_out_b, ta=True, out_dtype=BF16, name="d_w_down")
    grads["w_gate"] = mm(h2, d_gate, shards="out", out_dtype=BF16, name="d_w_gate")
    grads["w_up"] = mm(h2, d_up, shards="out", out_dtype=BF16, name="d_w_up")
    d_h2 = mm(d_gate, full["w_gate"], shards="bt", name="d_h2_gate")
    d_h2 = mm(d_up, full["w_up"], shards="bt", res=d_h2, name="d_h2_up")
    dx_mid, grads["norm2_g"], dx_mid_b = row_bwd("norm2_b", rms_res_fn, [Tiled(x_mid)], [norm2_g], [],
                                                 [Tiled(d_h2), Tiled(dx_out)], tm_row, bf16_copy_of=(0,))
    d_ycat = mm(dx_mid_b, full["w_out"], tb=True, name="d_ycat")
    grads["w_out"] = mm(y_cat, dx_mid_b, ta=True, out_dtype=BF16, name="d_w_out")
    rs_state, rs_token = sibling_exchange_begin([grads[n].reshape((4,) + args[n].shape[1:]) for n in later], "rs2")
    d_pu, d_pv, grads["sgu_ln_g"], grads["sgu_ln_b"], d_sgu_w, d_b_t, grads["sgu_out_g"] = row_bwd(
        "sgu_b", sgu_fn, sgu_in, sgu_par, [e_mat, e_t], [Tiled(d_ycat, C, 1)], tm_vjp,
        tiled_out_dtypes=[BF16, BF16])
    grads["sgu_w"] = d_sgu_w[None]
    grads["sgu_b"] = d_b_t[:, :HEADS].T[None]
    post_par_b = [gn_w + rs_token[0, 0]] + post_par[1:]
    d_y, d_r1, d_k1, d_v1, d_g, grads["gn_w"], grads["gn_b"], d_r_k = row_bwd(
        "post_b", post_fn, post_in, post_par_b, [jm], [Tiled(d_ycat, C, 0)], tm_vjp)
    grads["r_k"] = d_r_k.reshape(r_k.shape)
    rs_state, rs_token = reduce_scatter_continue(rs_state, later, "rs2", d_y)
    d_r2, d_w, d_k2, d_a, d_b, d_v3 = wkv_bwd(r_, w_, k2, v3, an, b_, ckpt, to_cols(d_y + rs_token[0, 0]))
    d_v2 = from_cols(d_v3)
    pre_cts = [(Tiled(d_r1), Tiled(d_r2)), Tiled(d_w), (Tiled(d_k1), Tiled(d_k2)), (Tiled(d_v1), Tiled(d_v2)),
               Tiled(d_a), Tiled(d_b), Tiled(d_g)]
    d_ps_rkv, d_ps_lora, grads["w0"], d_w2p, grads["a0"], d_a2p, d_g2p, grads["k_k"], grads["k_a"] = row_bwd(
        "pre_b", pre_fn, [Tiled(ps_rkv), Tiled(ps_lora)], pre_par, [jm], pre_cts, tm_vjp)
    grads["w2"], grads["a2"], grads["g2"] = d_w2p[None, :64], d_a2p[None, 64:], d_g2p[None, :160]
    d_p_rkv, d_mu_rkv = shift_bwd(d_ps_rkv, p, mu_rkv, 0, "shift_rkv_b")
    d_p_lora, d_mu_lora = shift_bwd(d_ps_lora, p, mu_lora, P_LORA // LORA_W, "shift_lora_b")
    grads["shift_mu"] = jnp.concatenate([d_mu_rkv, d_mu_lora[:, :288]], axis=1)
    d_p = jnp.concatenate([d_p_rkv, d_pu, d_pv, d_p_lora], axis=1)
    d_h1 = mm(d_p, w_in_p, tb=True, name="d_h1")
    d_w_in_p = mm(h1, d_p, ta=True, out_dtype=BF16, name="d_w_in")
    d_w_in = jnp.concatenate([d_w_in_p[:, :3 * C], d_w_in_p[:, P_LORA:P_LORA + 288], d_w_in_p[:, 3 * C:P_LORA]],
                             axis=1)
    grads["w_in"] = d_w_in.reshape(D, 4, D_IN // 4).transpose(1, 0, 2)
    dx, grads["norm1_g"] = row_bwd("norm1_b", rms_res_fn, [Tiled(x2d)], [norm1_g], [],
                                   [Tiled(d_h1), Tiled(dx_mid)], tm_row)

    small = [n for n in names if n not in big]
    small_shapes = [(1, 64, C) if n in ("w2", "a2") else (1, 160, C) if n == "g2" else args[n].shape for n in small]
    vec = _pack([grads[n] for n in small], F32)
    w_in_state, w_in_token = reduce_scatter_begin([grads["w_in"]], ["w_in"], "rs1")
    small_state, small_token = all_reduce_begin(vec)
    total = {n: g[None] for n, g in zip(later, reduce_scatter_end(rs_state, later, "rs2", w_in_token + small_token))}
    loss = lax.psum(loss_part[0, 0], ("x", "y", "c"))
    delta, new_m, new_v = {}, {}, {}

    def adam_big(n):
        s = args[n].shape
        d_, m_, v__ = adamw(args[n][0], total[n][0], args["m_" + n][0], args["v_" + n][0], "adamw_" + n)
        delta[n], new_m[n], new_v[n] = d_.reshape(s), m_.reshape(s), v__.reshape(s)

    for n in later:
        adam_big(n)
    total["w_in"] = reduce_scatter_end(w_in_state, ["w_in"], "rs1", new_v["w_down"])[0][None]
    for n, g_sum in zip(small, _unpack(all_reduce_end(small_state, new_v["w_down"]), small_shapes)):
        total[n] = lax.dynamic_slice_in_dim(g_sum, chip * 256, 256, axis=2) if n in lora_names else g_sum
    adam_big("w_in")
    flat = [_pack([src[n] for n in small], F32)
            for src in (args, total, {n: args["m_" + n] for n in small}, {n: args["v_" + n] for n in small})]
    outs = adamw(*flat, "adamw_small")
    for res, o in zip((delta, new_m, new_v), outs):
        res.update(zip(small, _unpack(o, [args[n].shape for n in small])))
    return (loss, dx[None], *[total[n] for n in names], *[delta[n] for n in names],
            *[new_m[n] for n in names], *[new_v[n] for n in names])
```

```python
import functools

import numpy as np
import jax
import jax.numpy as jnp
from jax import lax
from jax.experimental import pallas as pl
from jax.experimental.pallas import tpu as pltpu

F32 = jnp.float32
BF16 = jnp.bfloat16

D = 2048
C = 1024
HEADS = 16
HEAD = 64
LANES = 128
P_W = 5632
P_LORA = 5120
LORA_W = 512
D_IN = 5408
D_FF = 5632
CHUNK = 128
RMS_EPS = 1e-6
LN_EPS = 1e-5
GN_EPS = 64e-5
L2_EPS = 1e-12
VMEM_LIMIT = 56 * 1024 * 1024


def _pcall(body, **kw):
    return pl.pallas_call(body, **kw)


def _cparams(sem):
    return pltpu.CompilerParams(dimension_semantics=sem, vmem_limit_bytes=VMEM_LIMIT)


def _tile(n, most):
    t = most
    while t > 8 and n % t:
        t //= 2
    assert n % t == 0, (n, most)
    return t


MM_TILE = (2048, 512, 2048)
MM_VMEM = 40 * 1024 * 1024


def _div_tile(n, most, quantum=LANES):
    for t in range(min(n, most) // quantum * quantum, 0, -quantum):
        if n % t == 0:
            return t
    raise ValueError((n, most, quantum))


def _mm_tiles(m, n, k, a_bytes, b_bytes, r_bytes, o_bytes):
    tm, tn, tk = _div_tile(m, MM_TILE[0]), _div_tile(n, MM_TILE[1]), _div_tile(k, MM_TILE[2])

    def need(tm, tk):
        return 2 * (tm * tk * a_bytes + tk * tn * b_bytes + tm * tn * (r_bytes + o_bytes)) + tm * tn * 4

    while need(tm, tk) > MM_VMEM:
        if tk >= tm and tk > 512:
            tk = _div_tile(k, tk - LANES)
        else:
            tm = _div_tile(m, tm - LANES)
    return tm, tn, tk


def mm(a, b, *, ta=False, tb=False, res=None, out_dtype=F32, name, shards=None, post=None):
    if shards is not None:
        return _mm_shards(a, b, res, out_dtype, name, shards, post)
    m, k = (a.shape[1], a.shape[0]) if ta else a.shape
    n = b.shape[0] if tb else b.shape[1]
    assert (b.shape[1] if tb else b.shape[0]) == k
    if post is None:
        tm, tn, tk = _mm_tiles(m, n, k, a.dtype.itemsize, b.dtype.itemsize,
                               0 if res is None else res.dtype.itemsize, jnp.dtype(out_dtype).itemsize)
    else:
        tm, tn, tk = _div_tile(m, 1024), _div_tile(n, MM_TILE[1]), _div_tile(k, MM_TILE[2])
    nk = k // tk
    dims = (((0 if ta else 1,), (1 if tb else 0,)), ((), ()))
    a_spec = pl.BlockSpec((tk, tm), lambda i, j, l: (l, i)) if ta else pl.BlockSpec((tm, tk), lambda i, j, l: (i, l))
    b_spec = pl.BlockSpec((tn, tk), lambda i, j, l: (j, l)) if tb else pl.BlockSpec((tk, tn), lambda i, j, l: (l, j))
    o_spec = pl.BlockSpec((tm, tn), lambda i, j, l: (i, j))
    return _mm_call(a, b, res, dims, (m // tm, n // tn, nk), a_spec, b_spec, o_spec, o_spec, (tm, tn),
                    jax.ShapeDtypeStruct((m, n), out_dtype), name, post)


def _mm_shards(a, b, res, out_dtype, name, shards, post=None):
    assert post is None or shards == "b"
    if shards == "b":
        (m, k), ns = a.shape, b.shape[2]
        tm, tk = _div_tile(m, 1024 if post is None else 512), _div_tile(k, 2048)
        grid, dims, acc = (m // tm, 4, k // tk), (((1,), (0,)), ((), ())), (tm, ns)
        a_spec = pl.BlockSpec((tm, tk), lambda i, s, l: (i, l))
        b_spec = pl.BlockSpec((None, tk, ns), lambda i, s, l: (s, l, 0))
        o_spec = pl.BlockSpec((tm, ns), lambda i, s, l: (i, s))
        out = jax.ShapeDtypeStruct((m, 4 * ns), out_dtype)
    elif shards == "bt":
        m, (_, n, ns) = a.shape[0], b.shape
        tm, tn = _div_tile(m, 2048), _div_tile(n, 512)
        grid, dims, acc = (m // tm, n // tn, 4), (((1,), (1,)), ((), ())), (tm, tn)
        a_spec = pl.BlockSpec((tm, ns), lambda i, j, s: (i, s))
        b_spec = pl.BlockSpec((None, tn, ns), lambda i, j, s: (s, j, 0))
        o_spec = pl.BlockSpec((tm, tn), lambda i, j, s: (i, j))
        out = jax.ShapeDtypeStruct((m, n), out_dtype)
    else:
        (t, m), ns = a.shape, b.shape[1] // 4
        tm, tk = _div_tile(m, 1024), _div_tile(t, 2048)
        grid, dims, acc = (m // tm, 4, t // tk), (((0,), (0,)), ((), ())), (tm, ns)
        a_spec = pl.BlockSpec((tk, tm), lambda i, s, l: (l, i))
        b_spec = pl.BlockSpec((tk, ns), lambda i, s, l: (l, s))
        o_spec = pl.BlockSpec((None, tm, ns), lambda i, s, l: (s, i, 0))
        out = jax.ShapeDtypeStruct((4, m, ns), out_dtype)
    return _mm_call(a, b, res, dims, grid, a_spec, b_spec, o_spec, o_spec, acc, out, name, post)


def _mm_call(a, b, res, dims, grid, a_spec, b_spec, r_spec, o_spec, acc_shape, out, name, post=None):
    nk = grid[2]
    post_fn, post_in, post_dtypes = (None, [], [out.dtype]) if post is None else post
    n_extra = (res is not None) + len(post_in)

    def body(*refs):
        a_ref, b_ref = refs[:2]
        r_ref = None if res is None else refs[2]
        p_refs = refs[2 + (res is not None):2 + n_extra]
        o_refs = refs[2 + n_extra:2 + n_extra + len(post_dtypes)]

        def finish(acc):
            if r_ref is not None:
                acc = acc + r_ref[...].astype(F32)
            outs = (acc,) if post_fn is None else post_fn(acc, *[p[...] for p in p_refs])
            for o_ref, o in zip(o_refs, outs):
                o_ref[...] = o.astype(o_ref.dtype)

        prod = lax.dot_general(a_ref[...].astype(BF16), b_ref[...].astype(BF16), dims, preferred_element_type=F32)
        if nk == 1:
            finish(prod)
            return
        acc_ref = refs[-1]
        kk = pl.program_id(2)

        @pl.when(kk == 0)
        def _():
            acc_ref[...] = prod

        @pl.when(kk > 0)
        def _():
            acc_ref[...] += prod

        @pl.when(kk == nk - 1)
        def _():
            finish(acc_ref[...])

    in_specs = [a_spec, b_spec] + ([r_spec] if res is not None else []) + [o_spec] * len(post_in)
    args = (a, b) + ((res,) if res is not None else ()) + tuple(post_in)
    outs = _pcall(
        body, name=name, grid=grid, in_specs=in_specs, out_specs=[o_spec] * len(post_dtypes),
        out_shape=[jax.ShapeDtypeStruct(out.shape, dt) for dt in post_dtypes],
        scratch_shapes=[] if nk == 1 else [pltpu.VMEM(acc_shape, F32)],
        compiler_params=_cparams(("parallel", "parallel", "arbitrary")),
    )(*args)
    return outs[0] if post is None else outs


class Tiled:
    def __init__(self, arr, width=None, col=0):
        self.arr, self.width, self.col = arr, (arr.shape[1] if width is None else width), col

    def spec(self, tm):
        col = self.col
        return pl.BlockSpec((tm, self.width), lambda i: (i, col))


def _full_spec(p):
    nd = p.ndim
    return pl.BlockSpec(p.shape, lambda i: (0,) * nd)


def row_fwd(name, fn, tiled, params, consts, outs, tm):
    t = tiled[0].arr.shape[0]
    n_in = len(tiled) + len(params) + len(consts)

    def body(*refs):
        res = fn(*[r[...] for r in refs[:n_in]])
        for o_ref, r in zip(refs[n_in:], res):
            o_ref[...] = r.astype(o_ref.dtype)

    return _pcall(
        body, name=name, grid=(t // tm,),
        in_specs=[x.spec(tm) for x in tiled] + [_full_spec(p) for p in params + consts],
        out_specs=[pl.BlockSpec((tm, w), lambda i: (i, 0)) for w, _ in outs],
        out_shape=[jax.ShapeDtypeStruct((t, w), dt) for w, dt in outs],
        compiler_params=_cparams(("parallel",)),
    )(*[x.arr for x in tiled], *params, *consts)


def row_bwd(name, fn, tiled, params, consts, cts, tm, tiled_out_dtypes=None, bf16_copy_of=()):
    t = tiled[0].arr.shape[0]
    nt, npar, ncon = len(tiled), len(params), len(consts)
    cts = [c if isinstance(c, tuple) else (c,) for c in cts]
    flat_cts = [x for c in cts for x in c]
    tiled_out_dtypes = tiled_out_dtypes or [F32] * nt
    copies = [tiled[j] for j in bf16_copy_of]

    def body(*refs):
        n_in = nt + npar + ncon
        ins = [r[...].astype(F32) for r in refs[:nt + npar]]
        con = [r[...] for r in refs[nt + npar:n_in]]
        ct_refs = list(refs[n_in:n_in + len(flat_cts)])
        o = refs[n_in + len(flat_cts):]
        ct = []
        for c in cts:
            parts = [ct_refs.pop(0)[...].astype(F32) for _ in c]
            ct.append(functools.reduce(lambda p, q: p + q, parts))
        _, vjp = jax.vjp(lambda *a: fn(*a, *con), *ins)
        g = vjp(tuple(ct))
        for j in range(nt):
            o[j][...] = g[j].astype(o[j].dtype)
        for n, j in enumerate(bf16_copy_of):
            o[nt + npar + n][...] = g[j].astype(BF16)
        first = pl.program_id(0) == 0

        @pl.when(first)
        def _():
            for j in range(npar):
                o[nt + j][...] = g[nt + j]

        @pl.when(jnp.logical_not(first))
        def _():
            for j in range(npar):
                o[nt + j][...] += g[nt + j]

    return _pcall(
        body, name=name, grid=(t // tm,),
        in_specs=[x.spec(tm) for x in tiled] + [_full_spec(p) for p in params + consts]
        + [x.spec(tm) for x in flat_cts],
        out_specs=[pl.BlockSpec((tm, x.width), lambda i: (i, 0)) for x in tiled] + [_full_spec(p) for p in params]
        + [pl.BlockSpec((tm, x.width), lambda i: (i, 0)) for x in copies],
        out_shape=[jax.ShapeDtypeStruct((t, x.width), dt) for x, dt in zip(tiled, tiled_out_dtypes)]
        + [jax.ShapeDtypeStruct(p.shape, F32) for p in params]
        + [jax.ShapeDtypeStruct((t, x.width), BF16) for x in copies],
        compiler_params=_cparams(("arbitrary",)),
    )(*[x.arr for x in tiled], *params, *consts, *[x.arr for x in flat_cts])


def _split_dot(x, w):
    hi = x.astype(BF16)
    lo = (x - hi.astype(F32)).astype(BF16)
    return jnp.dot(hi, w, preferred_element_type=F32) + jnp.dot(lo, w, preferred_element_type=F32)


@jax.custom_vjp
def headsum(x, j):
    return _split_dot(x, j)


def _headsum_fwd(x, j):
    return _split_dot(x, j), j


def _headsum_bwd(j, ct):
    return _split_dot(ct, j), jnp.zeros_like(j)


headsum.defvjp(_headsum_fwd, _headsum_bwd)


def _bdot(x, w):
    return jnp.dot(x.astype(BF16), w.astype(BF16), preferred_element_type=F32)


def _sigmoid(x):
    return 1.0 / (1.0 + jnp.exp(-x))


def _softplus(x):
    return jnp.maximum(x, 0.0) + jnp.log(1.0 + jnp.exp(-jnp.abs(x)))


def rms_fn(x, g):
    return (x * lax.rsqrt(jnp.mean(x * x, axis=-1, keepdims=True) + RMS_EPS) * g,)


def rms_res_fn(x, g):
    return rms_fn(x, g)[0], x


def pre_fn(rkv, lora, w0, w2p, a0, a2p, g2p, k_k, k_a, jm):
    r, k, v = rkv[:, :C], rkv[:, C:2 * C], rkv[:, 2 * C:]
    zwa, zg = lora[:, :LANES], lora[:, LANES:LANES + 256]
    w_log = -_softplus(-(w0 + _bdot(jnp.tanh(zwa), w2p))) - 0.5
    decay = jnp.exp(-jnp.exp(w_log))
    a = _sigmoid(a0 + _bdot(zwa, a2p))
    g = _bdot(_sigmoid(zg), g2p)
    kk = k * k_k
    kk = kk / jnp.maximum(jnp.sqrt(headsum(kk * kk, jm)), L2_EPS)
    k2 = k * (1.0 + (a - 1.0) * k_a)
    return r, decay, k2, v, -kk, kk * a, g


def post_fn(y, r, k2, v, g, gn_w, gn_b, r_k, jm):
    mu = headsum(y, jm) * (1.0 / HEAD)
    yc = y - mu
    var = headsum(yc * yc, jm) * (1.0 / HEAD)
    yn = yc * lax.rsqrt(var + GN_EPS) * gn_w + gn_b
    bonus = headsum(r * k2 * r_k, jm) * v
    return ((yn + bonus) * g,)


def _gelu(x):
    return 0.5 * x * (1.0 + lax.erf(x * np.float32(1.0 / np.sqrt(2.0))))


@jax.custom_vjp
def expand_groups(b_t, e, e_t):
    return _split_dot(b_t, e)


def _expand_groups_fwd(b_t, e, e_t):
    return _split_dot(b_t, e), (e, e_t)


def _expand_groups_bwd(res, ct):
    e, e_t = res
    return _split_dot(ct, e_t), jnp.zeros_like(e), jnp.zeros_like(e_t)


expand_groups.defvjp(_expand_groups_fwd, _expand_groups_bwd)


def sgu_fn(pu, pv, ln_g, ln_b, w_s, b_t, out_g, e, e_t):
    rows = pu.shape[0]
    b_exp = expand_groups(b_t, e, e_t)
    u, v = _gelu(pu), _gelu(pv)
    mu = jnp.mean(v, axis=-1, keepdims=True)
    vc = v - mu
    var = jnp.mean(vc * vc, axis=-1, keepdims=True)
    v = vc * lax.rsqrt(var + LN_EPS) * ln_g + ln_b
    tri = lax.broadcasted_iota(jnp.int32, (CHUNK, CHUNK), 0) >= lax.broadcasted_iota(jnp.int32, (CHUNK, CHUNK), 1)
    left = lax.broadcasted_iota(jnp.int32, (CHUNK, LANES), 1) < HEAD
    chunks = []
    for c0 in range(0, rows, CHUNK):
        cols = []
        for gp in range(HEADS // 2):
            vp = v[c0:c0 + CHUNK, gp * LANES:(gp + 1) * LANES]
            wa = jnp.where(tri, w_s[2 * gp], 0.0)
            wb = jnp.where(tri, w_s[2 * gp + 1], 0.0)
            cols.append(jnp.where(left, _bdot(wa, vp), _bdot(wb, vp)))
        chunks.append(jnp.concatenate(cols, axis=1) + b_exp)
    s = jnp.concatenate(chunks, axis=0) if len(chunks) > 1 else chunks[0]
    y = u * s
    return (y * lax.rsqrt(jnp.mean(y * y, axis=-1, keepdims=True) + RMS_EPS) * out_g,)


def swiglu_fn(gate, up):
    return (gate * _sigmoid(gate) * up,)


SCAN_CHUNK = 32
NB = C // LANES
FWD_GROUP = 2
BWD_GROUP = 8


def to_cols(a):
    t = a.shape[0]
    return a.reshape(t // 8, LANES, HEAD).transpose(0, 2, 1)


def from_cols(a):
    return a.transpose(0, 2, 1).reshape(a.shape[0] * 8, C)


def _scan_consts():
    i = np.arange(LANES)
    spread = (i[:, None] % 2) == (i[None, :] // HEAD)
    pick = (i[:, None] // HEAD) == (i[None, :] % 2)
    j2 = (i[:, None] // HEAD) == (i[None, :] // HEAD)
    sel = (i[None, :] // 2) == (np.arange(8 * NB * HEAD)[:, None] // HEAD)
    return (jnp.asarray(spread, BF16), jnp.asarray(pick, BF16), jnp.asarray(np.concatenate([j2, j2], 0), BF16),
            jnp.asarray(sel, BF16))


def _stack(blocks):
    return jnp.concatenate(blocks, axis=0) if len(blocks) > 1 else blocks[0]


def _unstack(x, n):
    return [x[i * HEAD:(i + 1) * HEAD] for i in range(n)]


def _headsums(blocks, j2k, group):
    res = []
    for g0 in range(0, len(blocks), group):
        x = _stack(blocks[g0:g0 + group])
        hi = x.astype(BF16)
        lo = (x - hi.astype(F32)).astype(BF16)
        out = jnp.dot(jnp.concatenate([hi, lo], axis=1), j2k, preferred_element_type=F32)
        res += _unstack(out, len(blocks[g0:g0 + group]))
    return res


def _headsums_out(blocks, pick):
    return _unstack(jnp.dot(_stack(blocks).astype(BF16), pick, preferred_element_type=F32), len(blocks))


def _expand8(tile, sel, spread):
    lhs = jnp.tile(tile.astype(BF16), (8 * NB, 1)) * sel
    return _unstack(jnp.dot(lhs, spread, preferred_element_type=F32), 8 * NB)


def _collapse(tile, blocks, first, lane_pair):
    for n, blk in enumerate(blocks):
        tile = jnp.where(jnp.tile(lane_pair == first + n, (HEAD // 8, 1)), blk, tile)
    return tile


def _row(tile, j, cb):
    return jnp.broadcast_to(tile[j:j + 1, cb * LANES:(cb + 1) * LANES], (HEAD, LANES))


def _scan_step(s_ref, rows, vexp, j2k, j, hist=None):
    w_t, k_t, a_t, b_t, r_t = rows
    s = [s_ref[:, cb * LANES:(cb + 1) * LANES] for cb in range(NB)]
    sab = _headsums([s[cb] * _row(a_t, j, cb) for cb in range(NB)], j2k, FWD_GROUP if hist is None else BWD_GROUP)
    out = []
    for cb in range(NB):
        sl = slice(cb * LANES, (cb + 1) * LANES)
        s_new = s[cb] * _row(w_t, j, cb) + sab[cb] * _row(b_t, j, cb) + vexp[j * NB + cb] * _row(k_t, j, cb)
        s_ref[:, sl] = s_new
        if hist is None:
            out.append(s_new * _row(r_t, j, cb))
        else:
            s_hist, sab_hist, idx = hist
            s_hist[idx + 1, :, sl] = s_new
            sab_hist[idx, :, sl] = sab[cb]
    return out


def wkv_fwd(r, w, k, v3, a, b):
    t = r.shape[0]
    nc = t // SCAN_CHUNK
    n8 = SCAN_CHUNK // 8

    def body(r_ref, w_ref, k_ref, a_ref, b_ref, v3_ref, spread_ref, pick_ref, j2k_ref, sel_ref, y3_ref, ck_ref,
             s_ref):
        @pl.when(pl.program_id(0) == 0)
        def _():
            s_ref[...] = jnp.zeros_like(s_ref)

        ck_ref[0] = s_ref[...]
        spread, pick, j2k, sel = spread_ref[...], pick_ref[...], j2k_ref[...], sel_ref[...]
        lane_pair = lax.broadcasted_iota(jnp.int32, (8, LANES), 1) // 2

        def t8_body(t8, carry):
            row0 = pl.multiple_of(t8 * 8, 8)
            rows = [ref[pl.ds(row0, 8), :] for ref in (w_ref, k_ref, a_ref, b_ref, r_ref)]
            vexp = _expand8(v3_ref[t8], sel, spread)
            y3 = jnp.zeros((HEAD, LANES), F32)
            for j in range(8):
                y3 = _collapse(y3, _headsums_out(_scan_step(s_ref, rows, vexp, j2k, j), pick), j * NB, lane_pair)
            y3_ref[t8] = y3
            return carry

        lax.fori_loop(0, n8, t8_body, 0)

    row_spec = pl.BlockSpec((SCAN_CHUNK, C), lambda i: (i, 0))
    col_spec = pl.BlockSpec((n8, HEAD, LANES), lambda i: (i, 0, 0))
    consts = _scan_consts()
    return _pcall(
        body, name="wkv_fwd", grid=(nc,),
        in_specs=[row_spec] * 5 + [col_spec] + [pl.BlockSpec(c.shape, lambda i: (0, 0)) for c in consts],
        out_specs=[col_spec, pl.BlockSpec((1, HEAD, C), lambda i: (i, 0, 0))],
        out_shape=[jax.ShapeDtypeStruct((t // 8, HEAD, LANES), F32), jax.ShapeDtypeStruct((nc, HEAD, C), F32)],
        scratch_shapes=[pltpu.VMEM((HEAD, C), F32)],
        compiler_params=_cparams(("arbitrary",)),
    )(r, w, k, a, b, v3, *consts)


def wkv_bwd(r, w, k, v3, a, b, ckpt, dy3):
    t = r.shape[0]
    nc = t // SCAN_CHUNK
    n8 = SCAN_CHUNK // 8

    def body(r_ref, w_ref, k_ref, a_ref, b_ref, v3_ref, dy3_ref, ck_ref, spread_ref, pick_ref, j2k_ref, sel_ref,
             dr_ref, dw_ref, dk_ref, da_ref, db_ref, dv3_ref, s_ref, g_ref, s_hist, sab_hist):
        @pl.when(pl.program_id(0) == 0)
        def _():
            g_ref[...] = jnp.zeros_like(g_ref)

        spread, pick, j2k, sel = spread_ref[...], pick_ref[...], j2k_ref[...], sel_ref[...]
        lane_pair = lax.broadcasted_iota(jnp.int32, (8, LANES), 1) // 2
        sub = lax.broadcasted_iota(jnp.int32, (8, LANES), 0)
        s_ref[...] = ck_ref[0]
        s_hist[0] = ck_ref[0]

        def redo(t8, carry):
            row0 = pl.multiple_of(t8 * 8, 8)
            rows = [ref[pl.ds(row0, 8), :] for ref in (w_ref, k_ref, a_ref, b_ref, r_ref)]
            vexp = _expand8(v3_ref[t8], sel, spread)
            for j in range(8):
                _scan_step(s_ref, rows, vexp, j2k, j, hist=(s_hist, sab_hist, t8 * 8 + j))
            return carry

        lax.fori_loop(0, n8, redo, 0)

        def back(q, carry):
            t8 = n8 - 1 - q
            row0 = pl.multiple_of(t8 * 8, 8)
            w_t, k_t, a_t, b_t, r_t = [ref[pl.ds(row0, 8), :] for ref in (w_ref, k_ref, a_ref, b_ref, r_ref)]
            vexp = _expand8(v3_ref[t8], sel, spread)
            dyexp = _expand8(dy3_ref[t8], sel, spread)
            dv3 = jnp.zeros((HEAD, LANES), F32)
            tiles = [[jnp.zeros((8, LANES), F32) for _ in range(NB)] for _ in range(5)]
            for j in range(7, -1, -1):
                idx = t8 * 8 + j
                g = [g_ref[:, cb * LANES:(cb + 1) * LANES] + dyexp[j * NB + cb] * _row(r_t, j, cb)
                     for cb in range(NB)]
                dsab = _headsums([g[cb] * _row(b_t, j, cb) for cb in range(NB)], j2k, BWD_GROUP)
                dv3 = _collapse(dv3, _headsums_out([g[cb] * _row(k_t, j, cb) for cb in range(NB)], pick), j * NB,
                                lane_pair)
                for cb in range(NB):
                    sl = slice(cb * LANES, (cb + 1) * LANES)
                    s_new = s_hist[idx + 1, :, sl]
                    s_old = s_hist[idx, :, sl]
                    sab = sab_hist[idx, :, sl]
                    sums = (s_new * dyexp[j * NB + cb], g[cb] * s_old, g[cb] * vexp[j * NB + cb],
                            s_old * dsab[cb], g[cb] * sab)
                    for n, prod in enumerate(sums):
                        rowsum = jnp.broadcast_to(jnp.sum(prod, axis=0, keepdims=True), (8, LANES))
                        tiles[n][cb] = jnp.where(sub == j, rowsum, tiles[n][cb])
                    g_ref[:, sl] = g[cb] * _row(w_t, j, cb) + dsab[cb] * _row(a_t, j, cb)
            dv3_ref[t8] = dv3
            for n, ref in enumerate((dr_ref, dw_ref, dk_ref, da_ref, db_ref)):
                for cb in range(NB):
                    ref[pl.ds(row0, 8), cb * LANES:(cb + 1) * LANES] = tiles[n][cb]
            return carry

        lax.fori_loop(0, n8, back, 0)

    row_spec = pl.BlockSpec((SCAN_CHUNK, C), lambda i: (nc - 1 - i, 0))
    col_spec = pl.BlockSpec((n8, HEAD, LANES), lambda i: (nc - 1 - i, 0, 0))
    consts = _scan_consts()
    return _pcall(
        body, name="wkv_bwd", grid=(nc,),
        in_specs=[row_spec] * 5 + [col_spec, col_spec, pl.BlockSpec((1, HEAD, C), lambda i: (nc - 1 - i, 0, 0))]
        + [pl.BlockSpec(c.shape, lambda i: (0, 0)) for c in consts],
        out_specs=[row_spec] * 5 + [col_spec],
        out_shape=[jax.ShapeDtypeStruct((t, C), F32)] * 5 + [jax.ShapeDtypeStruct((t // 8, HEAD, LANES), F32)],
        scratch_shapes=[pltpu.VMEM((HEAD, C), F32), pltpu.VMEM((HEAD, C), F32),
                        pltpu.VMEM((SCAN_CHUNK + 1, HEAD, C), F32), pltpu.VMEM((SCAN_CHUNK, HEAD, C), F32)],
        compiler_params=_cparams(("arbitrary",)),
    )(r, w, k, a, b, v3, dy3, ckpt, *consts)


def _prev_rows(cur, before, first_tile):
    last = jnp.where(first_tile, 0.0, before[7:8, :])
    row = lax.broadcasted_iota(jnp.int32, cur.shape, 0)
    return jnp.where(row == 0, last, pltpu.roll(cur, 1, 0))


def shift_fwd(p, mu, col, name):
    t, width = p.shape[0], mu.shape[1]
    tm = _tile(t, 256)

    def body(p_ref, before_ref, mu_ref, o_ref):
        cur = p_ref[...]
        prev = _prev_rows(cur, before_ref[...], pl.program_id(0) == 0)
        o_ref[...] = cur + (prev - cur) * mu_ref[...]

    return _pcall(
        body, name=name, grid=(t // tm,),
        in_specs=[pl.BlockSpec((tm, width), lambda i: (i, col)),
                  pl.BlockSpec((8, width), lambda i: (jnp.maximum(i * (tm // 8) - 1, 0), col)),
                  pl.BlockSpec((1, width), lambda i: (0, 0))],
        out_specs=pl.BlockSpec((tm, width), lambda i: (i, 0)),
        out_shape=jax.ShapeDtypeStruct((t, width), F32),
        compiler_params=_cparams(("parallel",)),
    )(p, p, mu)


def shift_bwd(dps, p, mu, col, name):
    t, width = dps.shape
    tm = _tile(t, 256)
    nt = t // tm

    def body(d_ref, after_ref, p_ref, before_ref, mu_ref, dp_ref, dmu_ref):
        i = pl.program_id(0)
        d, cur, mu_v = d_ref[...], p_ref[...], mu_ref[...]
        row = lax.broadcasted_iota(jnp.int32, d.shape, 0)
        first_after = jnp.where(i == nt - 1, 0.0, after_ref[0:1, :])
        nxt = jnp.where(row == tm - 1, first_after, pltpu.roll(d, tm - 1, 0))
        dp_ref[...] = (d * (1.0 - mu_v) + nxt * mu_v).astype(BF16)
        prev = _prev_rows(cur, before_ref[...], i == 0)
        part = jnp.sum(d * (prev - cur), axis=0, keepdims=True)

        @pl.when(i == 0)
        def _():
            dmu_ref[...] = part

        @pl.when(i > 0)
        def _():
            dmu_ref[...] += part

    return _pcall(
        body, name=name, grid=(nt,),
        in_specs=[pl.BlockSpec((tm, width), lambda i: (i, 0)),
                  pl.BlockSpec((8, width), lambda i: (jnp.minimum((i + 1) * (tm // 8), t // 8 - 1), 0)),
                  pl.BlockSpec((tm, width), lambda i: (i, col)),
                  pl.BlockSpec((8, width), lambda i: (jnp.maximum(i * (tm // 8) - 1, 0), col)),
                  pl.BlockSpec((1, width), lambda i: (0, 0))],
        out_specs=[pl.BlockSpec((tm, width), lambda i: (i, 0)), pl.BlockSpec((1, width), lambda i: (0, 0))],
        out_shape=[jax.ShapeDtypeStruct((t, width), BF16), jax.ShapeDtypeStruct((1, width), F32)],
        compiler_params=_cparams(("arbitrary",)),
    )(dps, dps, p, p, mu)


def loss_head(x3, tgt, g):
    t, d = x3.shape
    tm = _tile(t, 256)

    def body(x_ref, t_ref, g_ref, loss_ref, dx_ref, dg_ref, dxb_ref):
        (y,), vjp = jax.vjp(lambda a, b: rms_fn(a, b), x_ref[...], g_ref[...])
        diff = y - t_ref[...]
        part = 0.5 * jnp.sum(jnp.mean(diff * diff, axis=-1, keepdims=True), axis=0, keepdims=True)
        dx, dg = vjp((diff * (1.0 / d),))
        dx_ref[...] = dx
        dxb_ref[...] = dx.astype(BF16)
        part = jnp.broadcast_to(part, (1, LANES))
        first = pl.program_id(0) == 0

        @pl.when(first)
        def _():
            loss_ref[...] = part
            dg_ref[...] = dg

        @pl.when(jnp.logical_not(first))
        def _():
            loss_ref[...] += part
            dg_ref[...] += dg

    row = pl.BlockSpec((tm, d), lambda i: (i, 0))
    vec = pl.BlockSpec((1, d), lambda i: (0, 0))
    return _pcall(
        body, name="loss_head", grid=(t // tm,), in_specs=[row, row, vec],
        out_specs=[pl.BlockSpec((1, LANES), lambda i: (0, 0)), row, vec, row],
        out_shape=[jax.ShapeDtypeStruct((1, LANES), F32), jax.ShapeDtypeStruct((t, d), F32),
                   jax.ShapeDtypeStruct((1, d), F32), jax.ShapeDtypeStruct((t, d), BF16)],
        compiler_params=_cparams(("arbitrary",)),
    )(x3, tgt, g)


ADAM_LR, ADAM_B1, ADAM_B2, ADAM_EPS, ADAM_WD, ADAM_STEP = 0.001, 0.9, 0.999, 1e-08, 0.01, 10


def adamw(w, g, m, v, name):
    rows, width = w.shape
    tr = _tile(rows, 256)

    def body(w_ref, g_ref, m_ref, v_ref, d_ref, nm_ref, nv_ref):
        gv = g_ref[...]
        m_new = ADAM_B1 * m_ref[...] + (1.0 - ADAM_B1) * gv
        v_new = ADAM_B2 * v_ref[...] + (1.0 - ADAM_B2) * (gv * gv)
        m_hat = m_new / (1.0 - ADAM_B1 ** ADAM_STEP)
        v_hat = v_new / (1.0 - ADAM_B2 ** ADAM_STEP)
        d_ref[...] = -ADAM_LR * (m_hat / (jnp.sqrt(v_hat) + ADAM_EPS) + ADAM_WD * w_ref[...])
        nm_ref[...] = m_new
        nv_ref[...] = v_new

    spec = pl.BlockSpec((tr, width), lambda i: (i, 0))
    return _pcall(
        body, name=name, grid=(rows // tr,), in_specs=[spec] * 4, out_specs=[spec] * 3,
        out_shape=[jax.ShapeDtypeStruct((rows, width), F32)] * 3,
        compiler_params=_cparams(("parallel",)),
    )(w, g, m, v)


def sum_slots(q, name, for_swap=False):
    n, rows, width = q.shape
    tr = _tile(rows, 512)

    def body(*refs):
        refs = refs[for_swap:]
        acc = refs[0][...].astype(F32)
        for r in refs[1:n]:
            acc = acc + r[...].astype(F32)
        refs[n][...] = acc

    if not for_swap:
        specs = [pl.BlockSpec((None, tr, width), functools.partial(lambda s, i: (s, i, 0), s)) for s in range(n)]
        return _pcall(body, name=name, grid=(rows // tr,), in_specs=specs,
                      out_specs=pl.BlockSpec((tr, width), lambda i: (i, 0)),
                      out_shape=jax.ShapeDtypeStruct((rows, width), F32),
                      compiler_params=_cparams(("parallel",)))(*([q] * n))
    specs = [pl.BlockSpec((None, tr, width), functools.partial(lambda s, i, c_ref: (s, i, 0), s)) for s in range(n)]
    grid_spec = pltpu.PrefetchScalarGridSpec(
        num_scalar_prefetch=1, grid=(rows // tr,), in_specs=specs,
        out_specs=pl.BlockSpec((None, tr, width), lambda i, c_ref: (c_ref[0], i, 0)))
    return _pcall(body, name=name, grid_spec=grid_spec, out_shape=jax.ShapeDtypeStruct((2, rows, width), F32),
                  compiler_params=_cparams(("parallel",)))(
        lax.axis_index("c").astype(jnp.int32).reshape(1), *([q] * n))


MESH = pl.DeviceIdType.MESH
ANY_SPEC = pl.BlockSpec(memory_space=pl.ANY)


def _coords():
    return lax.axis_index("x"), lax.axis_index("y"), lax.axis_index("c")


def _other_chips(x, y):
    return [(1 - x, y), (x, 1 - y), (1 - x, 1 - y)]


def push(name, srcs, n_dst, plan_fn):
    n_arr = len(srcs)
    n_send = len(plan_fn(0, 0, 0))

    def body(*refs):
        src_refs, dst_refs = refs[:n_arr], refs[n_arr:2 * n_arr]
        send_sems, recv_sems = refs[2 * n_arr:]
        sends = plan_fn(*_coords())
        out = []
        for q, (src_ref, dst_ref) in enumerate(zip(src_refs, dst_refs)):
            out += [pltpu.make_async_remote_copy(src_ref.at[si], dst_ref.at[di], send_sems.at[q * n_send + k],
                                                 recv_sems.at[q * n_send + k], device_id=peer, device_id_type=MESH)
                    for k, (si, peer, di, _) in enumerate(sends)]
        for cp in out:
            cp.start()
        for q, (src_ref, dst_ref) in enumerate(zip(src_refs, dst_refs)):
            for k, (si, peer, _, ri) in enumerate(sends):
                pltpu.make_async_remote_copy(src_ref.at[si], dst_ref.at[ri], send_sems.at[q * n_send + k],
                                             recv_sems.at[q * n_send + k], device_id=peer,
                                             device_id_type=MESH).wait_recv()
        for cp in out:
            cp.wait_send()

    return _pcall(
        body, name=name, in_specs=[ANY_SPEC] * n_arr, out_specs=[ANY_SPEC] * n_arr,
        out_shape=[jax.ShapeDtypeStruct((n_dst,) + s.shape[1:], s.dtype) for s in srcs],
        scratch_shapes=[pltpu.SemaphoreType.DMA((n_arr * n_send,)), pltpu.SemaphoreType.DMA((n_arr * n_send,))],
    )(*srcs)


def plan_sibling_quarters(x, y, c):
    return [(2 * s + (1 - c), (x, y, 1 - c), s, s) for s in range(4)]


def plan_chips_by_shard(x, y, c):
    me = 2 * x + y
    return [(2 * px + py, (px, py, c), me, 2 * px + py) for px, py in _other_chips(x, y)]


def plan_chips_half(x, y, c):
    me = 2 * x + y
    return [(c, (px, py, c), me, 2 * px + py) for px, py in _other_chips(x, y)]


def swap_halves(name, bufs):
    n_arr = len(bufs)

    def body(*refs):
        buf_refs = refs[n_arr:2 * n_arr]
        send_sems, recv_sems = refs[2 * n_arr:]
        x, y, c = _coords()
        out = [pltpu.make_async_remote_copy(b.at[c], b.at[c], send_sems.at[q], recv_sems.at[q],
                                            device_id=(x, y, 1 - c), device_id_type=MESH)
               for q, b in enumerate(buf_refs)]
        for cp in out:
            cp.start()
        for q, b in enumerate(buf_refs):
            pltpu.make_async_remote_copy(b.at[1 - c], b.at[1 - c], send_sems.at[q], recv_sems.at[q],
                                         device_id=(x, y, 1 - c), device_id_type=MESH).wait_recv()
        for cp in out:
            cp.wait_send()

    return _pcall(
        body, name=name, in_specs=[ANY_SPEC] * n_arr, out_specs=[ANY_SPEC] * n_arr,
        out_shape=[jax.ShapeDtypeStruct(b.shape, b.dtype) for b in bufs],
        input_output_aliases={q: q for q in range(n_arr)},
        scratch_shapes=[pltpu.SemaphoreType.DMA((n_arr,)), pltpu.SemaphoreType.DMA((n_arr,))],
    )(*bufs)


HBM_SPEC = pl.BlockSpec(memory_space=pltpu.HBM)
SEM_SPEC = pl.BlockSpec(memory_space=pltpu.SEMAPHORE)
DATAFLOW = pltpu.SideEffectType.DATAFLOW_SIDE_EFFECTING


def _in_hbm(a):
    return pltpu.with_memory_space_constraint(a, pltpu.HBM)


def push_start(name, srcs, lands, plan_fn, after):
    n_arr = len(srcs)
    n_send = len(plan_fn(0, 0, 0))
    arrays = list(srcs) + ([] if lands is None else list(lands))
    if after is None:
        after = jnp.zeros((8, LANES), F32)
    n_all = len(arrays)

    def body(*refs):
        src_refs, land_refs = refs[:n_arr], refs[n_all - n_arr:n_all]
        send_sems, recv_sems = refs[n_all + 1:n_all + 3]
        token = refs[-1]
        sends = plan_fn(*_coords())
        for q, (src_ref, land_ref) in enumerate(zip(src_refs, land_refs)):
            for k, (si, peer, di, _) in enumerate(sends):
                pltpu.make_async_remote_copy(src_ref.at[si], land_ref.at[di], send_sems.at[q * n_send + k],
                                             recv_sems.at[q * n_send + k], device_id=peer,
                                             device_id_type=MESH).start()
        token[...] = jnp.zeros_like(token)

    sems = pltpu.SemaphoreType.DMA((n_arr * n_send,))
    out = _pcall(
        body, name=name,
        in_specs=[HBM_SPEC] * n_all + [ANY_SPEC],
        out_specs=[SEM_SPEC, SEM_SPEC] + [HBM_SPEC] * n_all + [pl.BlockSpec(memory_space=pltpu.VMEM)],
        out_shape=[sems, sems] + [pltpu.HBM(a.shape, a.dtype) for a in arrays]
        + [jax.ShapeDtypeStruct((8, LANES), F32)],
        input_output_aliases={q: 2 + q for q in range(n_all)},
        compiler_params=pltpu.CompilerParams(has_side_effects=DATAFLOW),
    )(*[_in_hbm(a) for a in arrays], after)
    return out[:-1], out[-1]


def push_wait(name, state, plan_fn, after, in_place=False, both=False):
    arrays = list(state[2:])
    n_all = len(arrays)
    n_arr = n_all if in_place else n_all // 2
    n_send = len(plan_fn(0, 0, 0))
    send_sems, recv_sems = state[:2]

    def body(*refs):
        src_refs, land_refs = refs[:n_arr], refs[n_all - n_arr:n_all]
        send_ref, recv_ref = refs[n_all:n_all + 2]
        sends = plan_fn(*_coords())
        for q, (src_ref, land_ref) in enumerate(zip(src_refs, land_refs)):
            for k, (si, peer, _, ri) in enumerate(sends):
                cp = pltpu.make_async_remote_copy(src_ref.at[si], land_ref.at[ri], send_ref.at[q * n_send + k],
                                                  recv_ref.at[q * n_send + k], device_id=peer, device_id_type=MESH)
                cp.wait_send()
                cp.wait_recv()

    out = _pcall(
        body, name=name,
        in_specs=[HBM_SPEC] * n_all + [SEM_SPEC, SEM_SPEC, ANY_SPEC],
        out_specs=[HBM_SPEC] * n_all,
        out_shape=[pltpu.HBM(a.shape, a.dtype) for a in arrays],
        input_output_aliases={q: q for q in range(n_all)},
        compiler_params=pltpu.CompilerParams(has_side_effects=DATAFLOW),
    )(*arrays, send_sems, recv_sems, after)
    return list(out) if both else list(out[n_all - n_arr:])


def plan_chips_gather(x, y, c):
    me = 2 * x + y
    return [(c, (px, py, c), 2 * me + c, 2 * (2 * px + py) + c) for px, py in _other_chips(x, y)]


def plan_forward(x, y, c):
    return [(2 * (2 * px + py) + c, (x, y, 1 - c), 2 * (2 * px + py) + c, 2 * (2 * px + py) + (1 - c))
            for px, py in _other_chips(x, y)]


def add_kept(g8, other, name):
    _, rows, cols = other.shape
    tr = _tile(rows, 512)

    def body(c_ref, a_ref, b_ref, o_ref):
        o_ref[...] = (a_ref[...].astype(F32) + b_ref[...].astype(F32)).astype(BF16)

    spec = pl.BlockSpec((None, tr, cols), lambda s, i, c_ref: (s, i, 0))
    grid_spec = pltpu.PrefetchScalarGridSpec(
        num_scalar_prefetch=1, grid=(4, rows // tr),
        in_specs=[pl.BlockSpec((None, tr, cols), lambda s, i, c_ref: (2 * s + c_ref[0], i, 0)), spec],
        out_specs=spec)
    return _pcall(body, name=name, grid_spec=grid_spec, out_shape=jax.ShapeDtypeStruct((4, rows, cols), BF16),
                  compiler_params=_cparams(("parallel", "parallel")))(
        lax.axis_index("c").astype(jnp.int32).reshape(1), g8, other)


def reduce_scatter_begin(grads, names, tag):
    g8 = [g.reshape(8, g.shape[1] // 2, g.shape[2]) for g in grads]
    return _chip_sums_start(g8, push(tag + "_sibling", g8, 4, plan_sibling_quarters), names, tag)


def _chip_sums_start(g8, from_sibling, names, tag):
    chip_sum = [add_kept(a, b, tag + "_add_" + n) for a, b, n in zip(g8, from_sibling, names)]
    return push_start(tag + "_start", chip_sum, chip_sum, plan_chips_by_shard, None)


def sibling_exchange_begin(grads, tag):
    g8 = [g.reshape(8, g.shape[1] // 2, g.shape[2]) for g in grads]
    lands = [lax.empty((4,) + g.shape[1:], g.dtype) for g in g8]
    return push_start(tag + "_sib_start", g8, lands, plan_sibling_quarters, None)


def reduce_scatter_continue(state, names, tag, after):
    n = len(names)
    arrays = push_wait(tag + "_sib_wait", state, plan_sibling_quarters, after, both=True)
    return _chip_sums_start(arrays[:n], arrays[n:], names, tag)


def reduce_scatter_end(state, names, tag, after):
    by_chip = push_wait(tag + "_wait", state, plan_chips_by_shard, after)
    both = swap_halves(tag + "_halves", [sum_slots(q, tag + "_sum_" + n, for_swap=True)
                                         for q, n in zip(by_chip, names)])
    return [b.reshape(2 * b.shape[1], b.shape[2]) for b in both]


def _landing(mine):
    return [jnp.broadcast_to(m[None], (4,) + m.shape).reshape((8,) + m.shape[1:]) for m in mine]


def gather_begin(tag, mine, after, lands=None):
    return push_start(tag + "_start", mine, _landing(mine) if lands is None else lands, plan_chips_gather, after)


def gather_forward(tag, state, after):
    lands = push_wait(tag + "_wait", state, plan_chips_gather, after)
    return push_start(tag + "_fwd_start", lands, None, plan_forward, None)


def gather_end(tag, state, after):
    bufs = push_wait(tag + "_fwd_wait", state, plan_forward, after, in_place=True)
    return [g.reshape(4, 2 * g.shape[1], g.shape[2]) for g in bufs]


PACK_QUANTUM = 8 * C


def _pack(parts, dtype):
    pieces = []
    for p in parts:
        flat = p.astype(dtype).reshape(-1)
        pieces.append(jnp.pad(flat, (0, -flat.shape[0] % PACK_QUANTUM)).reshape(-1, C))
    rows = sum(p.shape[0] for p in pieces)
    if rows % 16:
        pieces.append(jnp.zeros((8, C), dtype))
    return jnp.concatenate(pieces, axis=0)


def _unpack(packed, shapes):
    out, at = [], 0
    for s in shapes:
        n = int(np.prod(s))
        rows = -(-n // PACK_QUANTUM) * 8
        out.append(packed[at:at + rows].reshape(-1)[:n].reshape(s))
        at += rows
    return out


def all_reduce_begin(vec):
    rows, width = vec.shape
    (pair,) = swap_halves("ar_sibling", [jnp.broadcast_to(vec[None], (2, rows, width))])
    chip_sum = sum_slots(pair, "ar_sum_sibling").reshape(2, rows // 2, width)
    mine = lax.dynamic_index_in_dim(chip_sum, lax.axis_index("c"), axis=0, keepdims=True)
    return push_start("ar_start", [chip_sum], [jnp.broadcast_to(mine, (4, rows // 2, width))], plan_chips_half,
                      None)


def all_reduce_end(state, after):
    (by_chip,) = push_wait("ar_wait", state, plan_chips_half, after)
    (both,) = swap_halves("ar_halves", [sum_slots(by_chip, "ar_sum_chips", for_swap=True)])
    return both.reshape(2 * both.shape[1], both.shape[2])


def kernel(x, norm1_g, w_in, shift_mu, w0, w2, a0, a2, g2, k_k, k_a, r_k, gn_w, gn_b, sgu_ln_g, sgu_ln_b, sgu_w, sgu_b, sgu_out_g, w_out, norm2_g, w_gate, w_up, w_down, final_g, loss_target, m_norm1_g, m_w_in, m_shift_mu, m_w0, m_w2, m_a0, m_a2, m_g2, m_k_k, m_k_a, m_r_k, m_gn_w, m_gn_b, m_sgu_ln_g, m_sgu_ln_b, m_sgu_w, m_sgu_b, m_sgu_out_g, m_w_out, m_norm2_g, m_w_gate, m_w_up, m_w_down, m_final_g, v_norm1_g, v_w_in, v_shift_mu, v_w0, v_w2, v_a0, v_a2, v_g2, v_k_k, v_k_a, v_r_k, v_gn_w, v_gn_b, v_sgu_ln_g, v_sgu_ln_b, v_sgu_w, v_sgu_b, v_sgu_out_g, v_w_out, v_norm2_g, v_w_gate, v_w_up, v_w_down, v_final_g):
    args = dict(locals())
    names = ["norm1_g", "w_in", "shift_mu", "w0", "w2", "a0", "a2", "g2", "k_k", "k_a", "r_k", "gn_w", "gn_b",
             "sgu_ln_g", "sgu_ln_b", "sgu_w", "sgu_b", "sgu_out_g", "w_out", "norm2_g", "w_gate", "w_up", "w_down",
             "final_g"]
    big = ["w_in", "w_out", "w_gate", "w_up", "w_down"]
    xi, yi = lax.axis_index("x"), lax.axis_index("y")
    chip = 2 * xi + yi
    x2d, tgt = x[0], loss_target[0]
    t = x2d.shape[0]

    lora_names = ["w2", "a2", "g2"]
    first, later = ["w_in"] + lora_names, ["w_out", "w_gate", "w_up", "w_down"]

    def halves(n, token=None):
        m = args[n][0] if token is None else args[n][0] + token[0, 0]
        m = m.astype(BF16)
        return m.reshape(2, m.shape[0] // 2, m.shape[1])

    first_state, first_token = gather_begin("g0", [halves(n) for n in first], norm1_g)
    later_mine = [halves(n, first_token) for n in later]
    later_lands = _landing(later_mine)
    (h1,) = row_fwd("norm1", rms_fn, [Tiled(x2d)], [norm1_g + first_token[0, 0]], [], [(D, BF16)], _tile(t, 256))
    prepared = functools.reduce(lambda p, q: p + q, [a[0, 0, :1].astype(F32) for a in later_lands])
    first_state, first_token = gather_forward("g0", first_state, prepared + h1[0, :1].astype(F32))
    got = gather_end("g0", first_state, first_token)
    later_state, later_token = gather_begin("g1", later_mine, got[0], later_lands)
    full = {n: blk.transpose(1, 0, 2).reshape(blk.shape[1], 4 * blk.shape[2]) for n, blk in zip(first, got)}
    w_in_f = full["w_in"]
    w_in_p = jnp.concatenate([w_in_f[:, :3 * C], w_in_f[:, 3360:], w_in_f[:, 3 * C:3360],
                              jnp.zeros((D, P_W - D_IN), BF16)], axis=1)
    mu_rkv = shift_mu[:, :3 * C] + later_token[0, 0]
    mu_lora = jnp.pad(shift_mu[:, 3 * C:], ((0, 0), (0, LORA_W - 288)))
    w2p = jnp.pad(full["w2"], ((0, 64), (0, 0)))
    a2p = jnp.pad(full["a2"], ((64, 0), (0, 0)))
    g2p = jnp.pad(full["g2"], ((0, 96), (0, 0)))
    ii = np.arange(C)
    jm = jnp.asarray((ii[:, None] // HEAD) == (ii[None, :] // HEAD), BF16)
    e_np = (np.arange(LANES)[:, None] == (ii[None, :] // HEAD))
    e_mat, e_t = jnp.asarray(e_np, BF16), jnp.asarray(e_np.T, BF16)
    b_t = jnp.pad(sgu_b[0].T, ((0, 0), (0, LANES - HEADS)))
    r_k_flat = r_k.reshape(1, C)
    tm_row = _tile(t, 256)
    tm_vjp = _tile(t, 128)

    p = mm(h1, w_in_p, name="proj_in")
    ps_rkv = shift_fwd(p, mu_rkv, 0, "shift_rkv")
    ps_lora = shift_fwd(p, mu_lora, P_LORA // LORA_W, "shift_lora")
    pre_par = [w0, w2p, a0, a2p, g2p, k_k, k_a]
    r_, w_, k2, v_, an, b_, g_ = row_fwd("pre", pre_fn, [Tiled(ps_rkv), Tiled(ps_lora)], pre_par, [jm],
                                         [(C, F32)] * 7, tm_vjp)
    v3 = to_cols(v_)
    y3, ckpt = wkv_fwd(r_, w_, k2, v3, an, b_)
    later_state, later_token = gather_forward("g1", later_state, y3)
    y_ = from_cols(y3 + later_token[0, 0])
    post_in = [Tiled(z) for z in (y_, r_, k2, v_, g_)]
    post_par = [gn_w, gn_b, r_k_flat]
    (y_rwkv,) = row_fwd("post", post_fn, post_in, post_par, [jm], [(C, BF16)], tm_vjp)
    sgu_in = [Tiled(p, C, 3), Tiled(p, C, 4)]
    sgu_par = [sgu_ln_g, sgu_ln_b, sgu_w[0], b_t, sgu_out_g]
    (y_sgu,) = row_fwd("sgu", sgu_fn, sgu_in, sgu_par, [e_mat, e_t], [(C, BF16)], tm_vjp)
    y_cat = jnp.concatenate([y_rwkv, y_sgu], axis=1)
    for n, blk in zip(later, gather_end("g1", later_state, y_cat)):
        full[n] = blk if n in ("w_gate", "w_up") else blk.reshape(4 * blk.shape[1], blk.shape[2])
    x_mid = mm(y_cat, full["w_out"], res=x2d, name="proj_out")
    (h2,) = row_fwd("norm2", rms_fn, [Tiled(x_mid)], [norm2_g], [], [(D, BF16)], tm_row)
    gate = mm(h2, full["w_gate"], shards="b", name="ffn_gate")
    up, act = mm(h2, full["w_up"], shards="b", name="ffn_up",
                 post=(lambda up_tile, gate_tile: (up_tile, swiglu_fn(gate_tile, up_tile)[0]), [gate], [F32, BF16]))
    x_out = mm(act, full["w_down"], res=x_mid, name="ffn_down")
    loss_part, dx_out, d_final_g, dx_out_b = loss_head(x_out, tgt, final_g.reshape(1, D))

    grads = {"final_g": d_final_g.reshape(D)}
    d_gate, d_up = mm(dx_out_b, full["w_down"], tb=True, name="d_act",
                      post=(lambda d_act, g_tile, u_tile: jax.vjp(swiglu_fn, g_tile, u_tile)[1]((d_act,)), [gate, up],
                            [BF16, BF16]))
    grads["w_down"] = mm(act, dx_out_b, ta=True, out_dtype=BF16, name="d_w_down")
    grads["w_gate"] = mm(h2, d_gate, shards="out", out_dtype=BF16, name="d_w_gate")
    grads["w_up"] = mm(h2, d_up, shards="out", out_dtype=BF16, name="d_w_up")
    d_h2 = mm(d_gate, full["w_gate"], shards="bt", name="d_h2_gate")
    d_h2 = mm(d_up, full["w_up"], shards="bt", res=d_h2, name="d_h2_up")
    dx_mid, grads["norm2_g"], dx_mid_b = row_bwd("norm2_b", rms_res_fn, [Tiled(x_mid)], [norm2_g], [],
                                                 [Tiled(d_h2), Tiled(dx_out)], tm_row, bf16_copy_of=(0,))
    d_ycat = mm(dx_mid_b, full["w_out"], tb=True, name="d_ycat")
    grads["w_out"] = mm(y_cat, dx_mid_b, ta=True, out_dtype=BF16, name="d_w_out")
    rs_state, rs_token = sibling_exchange_begin([grads[n].reshape((4,) + args[n].shape[1:]) for n in later], "rs2")
    d_pu, d_pv, grads["sgu_ln_g"], grads["sgu_ln_b"], d_sgu_w, d_b_t, grads["sgu_out_g"] = row_bwd(
        "sgu_b", sgu_fn, sgu_in, sgu_par, [e_mat, e_t], [Tiled(d_ycat, C, 1)], tm_vjp,
        tiled_out_dtypes=[BF16, BF16])
    grads["sgu_w"] = d_sgu_w[None]
    grads["sgu_b"] = d_b_t[:, :HEADS].T[None]
    post_par_b = [gn_w + rs_token[0, 0]] + post_par[1:]
    d_y, d_r1, d_k1, d_v1, d_g, grads["gn_w"], grads["gn_b"], d_r_k = row_bwd(
        "post_b", post_fn, post_in, post_par_b, [jm], [Tiled(d_ycat, C, 0)], tm_vjp)
    grads["r_k"] = d_r_k.reshape(r_k.shape)
    rs_state, rs_token = reduce_scatter_continue(rs_state, later, "rs2", d_y)
    d_r2, d_w, d_k2, d_a, d_b, d_v3 = wkv_bwd(r_, w_, k2, v3, an, b_, ckpt, to_cols(d_y + rs_token[0, 0]))
    d_v2 = from_cols(d_v3)
    pre_cts = [(Tiled(d_r1), Tiled(d_r2)), Tiled(d_w), (Tiled(d_k1), Tiled(d_k2)), (Tiled(d_v1), Tiled(d_v2)),
               Tiled(d_a), Tiled(d_b), Tiled(d_g)]
    d_ps_rkv, d_ps_lora, grads["w0"], d_w2p, grads["a0"], d_a2p, d_g2p, grads["k_k"], grads["k_a"] = row_bwd(
        "pre_b", pre_fn, [Tiled(ps_rkv), Tiled(ps_lora)], pre_par, [jm], pre_cts, tm_vjp)
    grads["w2"], grads["a2"], grads["g2"] = d_w2p[None, :64], d_a2p[None, 64:], d_g2p[None, :160]
    d_p_rkv, d_mu_rkv = shift_bwd(d_ps_rkv, p, mu_rkv, 0, "shift_rkv_b")
    d_p_lora, d_mu_lora = shift_bwd(d_ps_lora, p, mu_lora, P_LORA // LORA_W, "shift_lora_b")
    grads["shift_mu"] = jnp.concatenate([d_mu_rkv, d_mu_lora[:, :288]], axis=1)
    d_p = jnp.concatenate([d_p_rkv, d_pu, d_pv, d_p_lora], axis=1)
    d_h1 = mm(d_p, w_in_p, tb=True, name="d_h1")
    d_w_in_p = mm(h1, d_p, ta=True, out_dtype=BF16, name="d_w_in")
    d_w_in = jnp.concatenate([d_w_in_p[:, :3 * C], d_w_in_p[:, P_LORA:P_LORA + 288], d_w_in_p[:, 3 * C:P_LORA]],
                             axis=1)
    grads["w_in"] = d_w_in.reshape(D, 4, D_IN // 4).transpose(1, 0, 2)
    dx, grads["norm1_g"] = row_bwd("norm1_b", rms_res_fn, [Tiled(x2d)], [norm1_g], [],
                                   [Tiled(d_h1), Tiled(dx_mid)], tm_row)

    small = [n for n in names if n not in big]
    small_shapes = [(1, 64, C) if n in ("w2", "a2") else (1, 160, C) if n == "g2" else args[n].shape for n in small]
    vec = _pack([grads[n] for n in small], F32)
    w_in_state, w_in_token = reduce_scatter_begin([grads["w_in"]], ["w_in"], "rs1")
    small_state, small_token = all_reduce_begin(vec)
    total = {n: g[None] for n, g in zip(later, reduce_scatter_end(rs_state, later, "rs2", w_in_token + small_token))}
    loss = lax.psum(loss_part[0, 0], ("x", "y", "c"))
    delta, new_m, new_v = {}, {}, {}

    def adam_big(n):
        s = args[n].shape
        d_, m_, v__ = adamw(args[n][0], total[n][0], args["m_" + n][0], args["v_" + n][0], "adamw_" + n)
        delta[n], new_m[n], new_v[n] = d_.reshape(s), m_.reshape(s), v__.reshape(s)

    for n in later:
        adam_big(n)
    total["w_in"] = reduce_scatter_end(w_in_state, ["w_in"], "rs1", new_v["w_down"])[0][None]
    for n, g_sum in zip(small, _unpack(all_reduce_end(small_state, new_v["w_down"]), small_shapes)):
        total[n] = lax.dynamic_slice_in_dim(g_sum, chip * 256, 256, axis=2) if n in lora_names else g_sum
    adam_big("w_in")
    flat = [_pack([src[n] for n in small], F32)
            for src in (args, total, {n: args["m_" + n] for n in small}, {n: args["v_" + n] for n in small})]
    outs = adamw(*flat, "adamw_small")
    for res, o in zip((delta, new_m, new_v), outs):
        res.update(zip(small, _unpack(o, [args[n].shape for n in small])))
    return (loss, dx[None], *[total[n] for n in names], *[delta[n] for n in names],
            *[new_m[n] for n in names], *[new_v[n] for n in names])
```

```python
import functools

import numpy as np
import jax
import jax.numpy as jnp
from jax import lax
from jax.experimental import pallas as pl
from jax.experimental.pallas import tpu as pltpu

F32 = jnp.float32
BF16 = jnp.bfloat16

D = 2048
C = 1024
HEADS = 16
HEAD = 64
LANES = 128
P_W = 5632
P_LORA = 5120
LORA_W = 512
D_IN = 5408
CHUNK = 128
RMS_EPS = 1e-6
LN_EPS = 1e-5
GN_EPS = 64e-5
L2_EPS = 1e-12
VMEM_LIMIT = 56 * 1024 * 1024


def _pcall(body, **kw):
    return pl.pallas_call(body, **kw)


def _cparams(sem):
    return pltpu.CompilerParams(dimension_semantics=sem, vmem_limit_bytes=VMEM_LIMIT)


def _tile(n, most):
    t = most
    while t > 8 and n % t:
        t //= 2
    assert n % t == 0, (n, most)
    return t


MM_TILE = (2048, 512, 2048)
MM_VMEM = 40 * 1024 * 1024


def _div_tile(n, most, quantum=LANES):
    for t in range(min(n, most) // quantum * quantum, 0, -quantum):
        if n % t == 0:
            return t
    raise ValueError((n, most, quantum))


def _mm_tiles(m, n, k, a_bytes, b_bytes, r_bytes, o_bytes):
    tm, tn, tk = _div_tile(m, MM_TILE[0]), _div_tile(n, MM_TILE[1]), _div_tile(k, MM_TILE[2])

    def need(tm, tk):
        return 2 * (tm * tk * a_bytes + tk * tn * b_bytes + tm * tn * (r_bytes + o_bytes)) + tm * tn * 4

    while need(tm, tk) > MM_VMEM:
        if tk >= tm and tk > 512:
            tk = _div_tile(k, tk - LANES)
        else:
            tm = _div_tile(m, tm - LANES)
    return tm, tn, tk


def mm(a, b, *, ta=False, tb=False, res=None, out_dtype=F32, name, shards=None, post=None):
    if shards is not None:
        return _mm_shards(a, b, res, out_dtype, name, shards, post)
    m, k = (a.shape[1], a.shape[0]) if ta else a.shape
    n = b.shape[0] if tb else b.shape[1]
    assert (b.shape[1] if tb else b.shape[0]) == k
    if post is None:
        tm, tn, tk = _mm_tiles(m, n, k, a.dtype.itemsize, b.dtype.itemsize,
                               0 if res is None else res.dtype.itemsize, jnp.dtype(out_dtype).itemsize)
    else:
        tm, tn, tk = _div_tile(m, 1024), _div_tile(n, MM_TILE[1]), _div_tile(k, MM_TILE[2])
    nk = k // tk
    dims = (((0 if ta else 1,), (1 if tb else 0,)), ((), ()))
    a_spec = pl.BlockSpec((tk, tm), lambda i, j, l: (l, i)) if ta else pl.BlockSpec((tm, tk), lambda i, j, l: (i, l))
    b_spec = pl.BlockSpec((tn, tk), lambda i, j, l: (j, l)) if tb else pl.BlockSpec((tk, tn), lambda i, j, l: (l, j))
    o_spec = pl.BlockSpec((tm, tn), lambda i, j, l: (i, j))
    return _mm_call(a, b, res, dims, (m // tm, n // tn, nk), a_spec, b_spec, o_spec, o_spec, (tm, tn),
                    jax.ShapeDtypeStruct((m, n), out_dtype), name, post)


def _mm_shards(a, b, res, out_dtype, name, shards, post=None):
    assert post is None or shards == "b"
    if shards == "b":
        (m, k), ns = a.shape, b.shape[2]
        tm, tk = _div_tile(m, 1024 if post is None else 512), _div_tile(k, 2048)
        grid, dims, acc = (m // tm, 4, k // tk), (((1,), (0,)), ((), ())), (tm, ns)
        a_spec = pl.BlockSpec((tm, tk), lambda i, s, l: (i, l))
        b_spec = pl.BlockSpec((None, tk, ns), lambda i, s, l: (s, l, 0))
        o_spec = pl.BlockSpec((tm, ns), lambda i, s, l: (i, s))
        out = jax.ShapeDtypeStruct((m, 4 * ns), out_dtype)
    elif shards == "bt":
        m, (_, n, ns) = a.shape[0], b.shape
        tm, tn = _div_tile(m, 2048), _div_tile(n, 512)
        grid, dims, acc = (m // tm, n // tn, 4), (((1,), (1,)), ((), ())), (tm, tn)
        a_spec = pl.BlockSpec((tm, ns), lambda i, j, s: (i, s))
        b_spec = pl.BlockSpec((None, tn, ns), lambda i, j, s: (s, j, 0))
        o_spec = pl.BlockSpec((tm, tn), lambda i, j, s: (i, j))
        out = jax.ShapeDtypeStruct((m, n), out_dtype)
    else:
        (t, m), ns = a.shape, b.shape[1] // 4
        tm, tk = _div_tile(m, 1024), _div_tile(t, 2048)
        grid, dims, acc = (m // tm, 4, t // tk), (((0,), (0,)), ((), ())), (tm, ns)
        a_spec = pl.BlockSpec((tk, tm), lambda i, s, l: (l, i))
        b_spec = pl.BlockSpec((tk, ns), lambda i, s, l: (l, s))
        o_spec = pl.BlockSpec((None, tm, ns), lambda i, s, l: (s, i, 0))
        out = jax.ShapeDtypeStruct((4, m, ns), out_dtype)
    return _mm_call(a, b, res, dims, grid, a_spec, b_spec, o_spec, o_spec, acc, out, name, post)


def _mm_call(a, b, res, dims, grid, a_spec, b_spec, r_spec, o_spec, acc_shape, out, name, post=None):
    nk = grid[2]
    post_fn, post_in, post_dtypes = (None, [], [out.dtype]) if post is None else post
    n_extra = (res is not None) + len(post_in)

    def body(*refs):
        a_ref, b_ref = refs[:2]
        r_ref = None if res is None else refs[2]
        p_refs = refs[2 + (res is not None):2 + n_extra]
        o_refs = refs[2 + n_extra:2 + n_extra + len(post_dtypes)]

        def finish(acc):
            if r_ref is not None:
                acc = acc + r_ref[...].astype(F32)
            outs = (acc,) if post_fn is None else post_fn(acc, *[p[...] for p in p_refs])
            for o_ref, o in zip(o_refs, outs):
                o_ref[...] = o.astype(o_ref.dtype)

        prod = lax.dot_general(a_ref[...].astype(BF16), b_ref[...].astype(BF16), dims, preferred_element_type=F32)
        if nk == 1:
            finish(prod)
            return
        acc_ref = refs[-1]
        kk = pl.program_id(2)

        @pl.when(kk == 0)
        def _():
            acc_ref[...] = prod

        @pl.when(kk > 0)
        def _():
            acc_ref[...] += prod

        @pl.when(kk == nk - 1)
        def _():
            finish(acc_ref[...])

    in_specs = [a_spec, b_spec] + ([r_spec] if res is not None else []) + [o_spec] * len(post_in)
    args = (a, b) + ((res,) if res is not None else ()) + tuple(post_in)
    outs = _pcall(
        body, name=name, grid=grid, in_specs=in_specs, out_specs=[o_spec] * len(post_dtypes),
        out_shape=[jax.ShapeDtypeStruct(out.shape, dt) for dt in post_dtypes],
        scratch_shapes=[] if nk == 1 else [pltpu.VMEM(acc_shape, F32)],
        compiler_params=_cparams(("parallel", "parallel", "arbitrary")),
    )(*args)
    return outs[0] if post is None else outs


class Tiled:
    def __init__(self, arr, width=None, col=0):
        self.arr, self.width, self.col = arr, (arr.shape[1] if width is None else width), col

    def spec(self, tm):
        col = self.col
        return pl.BlockSpec((tm, self.width), lambda i: (i, col))


def _full_spec(p):
    nd = p.ndim
    return pl.BlockSpec(p.shape, lambda i: (0,) * nd)


def row_fwd(name, fn, tiled, params, consts, outs, tm):
    t = tiled[0].arr.shape[0]
    n_in = len(tiled) + len(params) + len(consts)

    def body(*refs):
        res = fn(*[r[...] for r in refs[:n_in]])
        for o_ref, r in zip(refs[n_in:], res):
            o_ref[...] = r.astype(o_ref.dtype)

    return _pcall(
        body, name=name, grid=(t // tm,),
        in_specs=[x.spec(tm) for x in tiled] + [_full_spec(p) for p in params + consts],
        out_specs=[pl.BlockSpec((tm, w), lambda i: (i, 0)) for w, _ in outs],
        out_shape=[jax.ShapeDtypeStruct((t, w), dt) for w, dt in outs],
        compiler_params=_cparams(("parallel",)),
    )(*[x.arr for x in tiled], *params, *consts)


def row_bwd(name, fn, tiled, params, consts, cts, tm, tiled_out_dtypes=None, bf16_copy_of=()):
    t = tiled[0].arr.shape[0]
    nt, npar, ncon = len(tiled), len(params), len(consts)
    cts = [c if isinstance(c, tuple) else (c,) for c in cts]
    flat_cts = [x for c in cts for x in c]
    tiled_out_dtypes = tiled_out_dtypes or [F32] * nt
    copies = [tiled[j] for j in bf16_copy_of]

    def body(*refs):
        n_in = nt + npar + ncon
        ins = [r[...].astype(F32) for r in refs[:nt + npar]]
        con = [r[...] for r in refs[nt + npar:n_in]]
        ct_refs = list(refs[n_in:n_in + len(flat_cts)])
        o = refs[n_in + len(flat_cts):]
        ct = []
        for c in cts:
            parts = [ct_refs.pop(0)[...].astype(F32) for _ in c]
            ct.append(functools.reduce(lambda p, q: p + q, parts))
        _, vjp = jax.vjp(lambda *a: fn(*a, *con), *ins)
        g = vjp(tuple(ct))
        for j in range(nt):
            o[j][...] = g[j].astype(o[j].dtype)
        for n, j in enumerate(bf16_copy_of):
            o[nt + npar + n][...] = g[j].astype(BF16)
        first = pl.program_id(0) == 0

        @pl.when(first)
        def _():
            for j in range(npar):
                o[nt + j][...] = g[nt + j]

        @pl.when(jnp.logical_not(first))
        def _():
            for j in range(npar):
                o[nt + j][...] += g[nt + j]

    return _pcall(
        body, name=name, grid=(t // tm,),
        in_specs=[x.spec(tm) for x in tiled] + [_full_spec(p) for p in params + consts]
        + [x.spec(tm) for x in flat_cts],
        out_specs=[pl.BlockSpec((tm, x.width), lambda i: (i, 0)) for x in tiled] + [_full_spec(p) for p in params]
        + [pl.BlockSpec((tm, x.width), lambda i: (i, 0)) for x in copies],
        out_shape=[jax.ShapeDtypeStruct((t, x.width), dt) for x, dt in zip(tiled, tiled_out_dtypes)]
        + [jax.ShapeDtypeStruct(p.shape, F32) for p in params]
        + [jax.ShapeDtypeStruct((t, x.width), BF16) for x in copies],
        compiler_params=_cparams(("arbitrary",)),
    )(*[x.arr for x in tiled], *params, *consts, *[x.arr for x in flat_cts])


def _split_dot(x, w):
    hi = x.astype(BF16)
    lo = (x - hi.astype(F32)).astype(BF16)
    return jnp.dot(hi, w, preferred_element_type=F32) + jnp.dot(lo, w, preferred_element_type=F32)


def _headsum_blocks(x, j2k):
    out = []
    for c0 in range(0, x.shape[1], LANES):
        blk = x[:, c0:c0 + LANES]
        hi = blk.astype(BF16)
        lo = (blk - hi.astype(F32)).astype(BF16)
        out.append(jnp.dot(jnp.concatenate([hi, lo], axis=1), j2k, preferred_element_type=F32))
    return jnp.concatenate(out, axis=1)


@jax.custom_vjp
def headsum(x, j2k):
    return _headsum_blocks(x, j2k)


def _headsum_fwd(x, j2k):
    return _headsum_blocks(x, j2k), j2k


def _headsum_bwd(j2k, ct):
    return _headsum_blocks(ct, j2k), jnp.zeros_like(j2k)


headsum.defvjp(_headsum_fwd, _headsum_bwd)


def _bdot(x, w):
    return jnp.dot(x.astype(BF16), w.astype(BF16), preferred_element_type=F32)


def _sigmoid(x):
    return jax.nn.sigmoid(x)


def _softplus(x):
    return jnp.maximum(x, 0.0) + jnp.log(1.0 + jnp.exp(-jnp.abs(x)))


def rms_fn(x, g):
    return (x * lax.rsqrt(jnp.mean(x * x, axis=-1, keepdims=True) + RMS_EPS) * g,)


def rms_res_fn(x, g):
    return rms_fn(x, g)[0], x


def pre_fn(rkv, lora, w0, w2p, a0, a2p, g2p, k_k, k_a, jm):
    r, k, v = rkv[:, :C], rkv[:, C:2 * C], rkv[:, 2 * C:]
    zwa, zg = lora[:, :LANES], lora[:, LANES:LANES + 256]
    w_log = -_softplus(-(w0 + _bdot(jnp.tanh(zwa), w2p))) - 0.5
    decay = jnp.exp(-jnp.exp(w_log))
    a = _sigmoid(a0 + _bdot(zwa, a2p))
    g = _bdot(_sigmoid(zg), g2p)
    kk = k * k_k
    kk = kk / jnp.maximum(jnp.sqrt(headsum(kk * kk, jm)), L2_EPS)
    k2 = k * (1.0 + (a - 1.0) * k_a)
    return r, decay, k2, v, -kk, kk * a, g


def post_fn(y, r, k2, v, g, gn_w, gn_b, r_k, jm):
    mu = headsum(y, jm) * (1.0 / HEAD)
    yc = y - mu
    var = headsum(yc * yc, jm) * (1.0 / HEAD)
    yn = yc * lax.rsqrt(var + GN_EPS) * gn_w + gn_b
    bonus = headsum(r * k2 * r_k, jm) * v
    return ((yn + bonus) * g,)


def _gelu(x):
    return 0.5 * x * (1.0 + lax.erf(x * np.float32(1.0 / np.sqrt(2.0))))


@jax.custom_vjp
def expand_groups(b_t, e, e_t):
    return _split_dot(b_t, e)


def _expand_groups_fwd(b_t, e, e_t):
    return _split_dot(b_t, e), (e, e_t)


def _expand_groups_bwd(res, ct):
    e, e_t = res
    return _split_dot(ct, e_t), jnp.zeros_like(e), jnp.zeros_like(e_t)


expand_groups.defvjp(_expand_groups_fwd, _expand_groups_bwd)


def sgu_fn(pu, pv, ln_g, ln_b, w_s, b_t, out_g, e, e_t):
    rows = pu.shape[0]
    b_exp = expand_groups(b_t, e, e_t)
    u, v = _gelu(pu), _gelu(pv)
    mu = jnp.mean(v, axis=-1, keepdims=True)
    vc = v - mu
    var = jnp.mean(vc * vc, axis=-1, keepdims=True)
    v = vc * lax.rsqrt(var + LN_EPS) * ln_g + ln_b
    tri = lax.broadcasted_iota(jnp.int32, (CHUNK, CHUNK), 0) >= lax.broadcasted_iota(jnp.int32, (CHUNK, CHUNK), 1)
    left = lax.broadcasted_iota(jnp.int32, (CHUNK, LANES), 1) < HEAD
    chunks = []
    for c0 in range(0, rows, CHUNK):
        cols = []
        for gp in range(HEADS // 2):
            vp = v[c0:c0 + CHUNK, gp * LANES:(gp + 1) * LANES]
            wa = jnp.where(tri, w_s[2 * gp], 0.0)
            wb = jnp.where(tri, w_s[2 * gp + 1], 0.0)
            cols.append(jnp.where(left, _bdot(wa, vp), _bdot(wb, vp)))
        chunks.append(jnp.concatenate(cols, axis=1) + b_exp)
    s = jnp.concatenate(chunks, axis=0) if len(chunks) > 1 else chunks[0]
    y = u * s
    return (y * lax.rsqrt(jnp.mean(y * y, axis=-1, keepdims=True) + RMS_EPS) * out_g,)


def swiglu_fn(gate, up):
    return (gate * _sigmoid(gate) * up,)


SCAN_CHUNK = 32
NB = C // LANES
FWD_GROUP = 2
BWD_GROUP = 8


def to_cols(a):
    t = a.shape[0]
    return a.reshape(t // 8, LANES, HEAD).transpose(0, 2, 1)


def from_cols(a):
    return a.transpose(0, 2, 1).reshape(a.shape[0] * 8, C)


def _scan_consts():
    i = np.arange(LANES)
    spread = (i[:, None] % 2) == (i[None, :] // HEAD)
    pick = (i[:, None] // HEAD) == (i[None, :] % 2)
    j2 = (i[:, None] // HEAD) == (i[None, :] // HEAD)
    sel = (i[None, :] // 2) == (np.arange(8 * NB * HEAD)[:, None] // HEAD)
    return (jnp.asarray(spread, BF16), jnp.asarray(pick, BF16), jnp.asarray(np.concatenate([j2, j2], 0), BF16),
            jnp.asarray(sel, BF16))


def _stack(blocks):
    return jnp.concatenate(blocks, axis=0) if len(blocks) > 1 else blocks[0]


def _unstack(x, n):
    return [x[i * HEAD:(i + 1) * HEAD] for i in range(n)]


def _headsums(blocks, j2k, group):
    res = []
    for g0 in range(0, len(blocks), group):
        x = _stack(blocks[g0:g0 + group])
        hi = x.astype(BF16)
        lo = (x - hi.astype(F32)).astype(BF16)
        out = jnp.dot(jnp.concatenate([hi, lo], axis=1), j2k, preferred_element_type=F32)
        res += _unstack(out, len(blocks[g0:g0 + group]))
    return res


def _headsums_out(blocks, pick):
    return _unstack(jnp.dot(_stack(blocks).astype(BF16), pick, preferred_element_type=F32), len(blocks))


def _expand8(tile, sel, spread):
    lhs = jnp.tile(tile.astype(BF16), (8 * NB, 1)) * sel
    return _unstack(jnp.dot(lhs, spread, preferred_element_type=F32), 8 * NB)


def _collapse(tile, blocks, first, lane_pair):
    for n, blk in enumerate(blocks):
        tile = jnp.where(jnp.tile(lane_pair == first + n, (HEAD // 8, 1)), blk, tile)
    return tile


def _row(tile, j, cb):
    return jnp.broadcast_to(tile[j:j + 1, cb * LANES:(cb + 1) * LANES], (HEAD, LANES))


def _scan_step(s_ref, rows, vexp, j2k, j, hist=None):
    w_t, k_t, a_t, b_t, r_t = rows
    s = [s_ref[:, cb * LANES:(cb + 1) * LANES] for cb in range(NB)]
    sab = _headsums([s[cb] * _row(a_t, j, cb) for cb in range(NB)], j2k, FWD_GROUP if hist is None else BWD_GROUP)
    out = []
    for cb in range(NB):
        sl = slice(cb * LANES, (cb + 1) * LANES)
        s_new = s[cb] * _row(w_t, j, cb) + sab[cb] * _row(b_t, j, cb) + vexp[j * NB + cb] * _row(k_t, j, cb)
        s_ref[:, sl] = s_new
        if hist is None:
            out.append(s_new * _row(r_t, j, cb))
        else:
            s_hist, sab_hist, idx = hist
            s_hist[idx + 1, :, sl] = s_new
            sab_hist[idx, :, sl] = sab[cb]
    return out


def wkv_fwd(r, w, k, v3, a, b):
    t = r.shape[0]
    nc = t // SCAN_CHUNK
    n8 = SCAN_CHUNK // 8

    def body(r_ref, w_ref, k_ref, a_ref, b_ref, v3_ref, spread_ref, pick_ref, j2k_ref, sel_ref, y3_ref, ck_ref,
             s_ref):
        @pl.when(pl.program_id(0) == 0)
        def _():
            s_ref[...] = jnp.zeros_like(s_ref)

        ck_ref[0] = s_ref[...]
        spread, pick, j2k, sel = spread_ref[...], pick_ref[...], j2k_ref[...], sel_ref[...]
        lane_pair = lax.broadcasted_iota(jnp.int32, (8, LANES), 1) // 2

        def t8_body(t8, carry):
            row0 = pl.multiple_of(t8 * 8, 8)
            rows = [ref[pl.ds(row0, 8), :] for ref in (w_ref, k_ref, a_ref, b_ref, r_ref)]
            vexp = _expand8(v3_ref[t8], sel, spread)
            y3 = jnp.zeros((HEAD, LANES), F32)
            for j in range(8):
                y3 = _collapse(y3, _headsums_out(_scan_step(s_ref, rows, vexp, j2k, j), pick), j * NB, lane_pair)
            y3_ref[t8] = y3
            return carry

        lax.fori_loop(0, n8, t8_body, 0)

    row_spec = pl.BlockSpec((SCAN_CHUNK, C), lambda i: (i, 0))
    col_spec = pl.BlockSpec((n8, HEAD, LANES), lambda i: (i, 0, 0))
    consts = _scan_consts()
    return _pcall(
        body, name="wkv_fwd", grid=(nc,),
        in_specs=[row_spec] * 5 + [col_spec] + [pl.BlockSpec(c.shape, lambda i: (0, 0)) for c in consts],
        out_specs=[col_spec, pl.BlockSpec((1, HEAD, C), lambda i: (i, 0, 0))],
        out_shape=[jax.ShapeDtypeStruct((t // 8, HEAD, LANES), F32), jax.ShapeDtypeStruct((nc, HEAD, C), F32)],
        scratch_shapes=[pltpu.VMEM((HEAD, C), F32)],
        compiler_params=_cparams(("arbitrary",)),
    )(r, w, k, a, b, v3, *consts)


def wkv_bwd(r, w, k, v3, a, b, ckpt, dy3):
    t = r.shape[0]
    nc = t // SCAN_CHUNK
    n8 = SCAN_CHUNK // 8

    def body(r_ref, w_ref, k_ref, a_ref, b_ref, v3_ref, dy3_ref, ck_ref, spread_ref, pick_ref, j2k_ref, sel_ref,
             dr_ref, dw_ref, dk_ref, da_ref, db_ref, dv3_ref, s_ref, g_ref, s_hist, sab_hist):
        @pl.when(pl.program_id(0) == 0)
        def _():
            g_ref[...] = jnp.zeros_like(g_ref)

        spread, pick, j2k, sel = spread_ref[...], pick_ref[...], j2k_ref[...], sel_ref[...]
        lane_pair = lax.broadcasted_iota(jnp.int32, (8, LANES), 1) // 2
        sub = lax.broadcasted_iota(jnp.int32, (8, LANES), 0)
        s_ref[...] = ck_ref[0]
        s_hist[0] = ck_ref[0]

        def redo(t8, carry):
            row0 = pl.multiple_of(t8 * 8, 8)
            rows = [ref[pl.ds(row0, 8), :] for ref in (w_ref, k_ref, a_ref, b_ref, r_ref)]
            vexp = _expand8(v3_ref[t8], sel, spread)
            for j in range(8):
                _scan_step(s_ref, rows, vexp, j2k, j, hist=(s_hist, sab_hist, t8 * 8 + j))
            return carry

        lax.fori_loop(0, n8, redo, 0)

        def back(q, carry):
            t8 = n8 - 1 - q
            row0 = pl.multiple_of(t8 * 8, 8)
            w_t, k_t, a_t, b_t, r_t = [ref[pl.ds(row0, 8), :] for ref in (w_ref, k_ref, a_ref, b_ref, r_ref)]
            vexp = _expand8(v3_ref[t8], sel, spread)
            dyexp = _expand8(dy3_ref[t8], sel, spread)
            dv3 = jnp.zeros((HEAD, LANES), F32)
            tiles = [[jnp.zeros((8, LANES), F32) for _ in range(NB)] for _ in range(5)]
            for j in range(7, -1, -1):
                idx = t8 * 8 + j
                g = [g_ref[:, cb * LANES:(cb + 1) * LANES] + dyexp[j * NB + cb] * _row(r_t, j, cb)
                     for cb in range(NB)]
                dsab = _headsums([g[cb] * _row(b_t, j, cb) for cb in range(NB)], j2k, BWD_GROUP)
                dv3 = _collapse(dv3, _headsums_out([g[cb] * _row(k_t, j, cb) for cb in range(NB)], pick), j * NB,
                                lane_pair)
                for cb in range(NB):
                    sl = slice(cb * LANES, (cb + 1) * LANES)
                    s_new = s_hist[idx + 1, :, sl]
                    s_old = s_hist[idx, :, sl]
                    sab = sab_hist[idx, :, sl]
                    sums = (s_new * dyexp[j * NB + cb], g[cb] * s_old, g[cb] * vexp[j * NB + cb],
                            s_old * dsab[cb], g[cb] * sab)
                    for n, prod in enumerate(sums):
                        rowsum = jnp.broadcast_to(jnp.sum(prod, axis=0, keepdims=True), (8, LANES))
                        tiles[n][cb] = jnp.where(sub == j, rowsum, tiles[n][cb])
                    g_ref[:, sl] = g[cb] * _row(w_t, j, cb) + dsab[cb] * _row(a_t, j, cb)
            dv3_ref[t8] = dv3
            for n, ref in enumerate((dr_ref, dw_ref, dk_ref, da_ref, db_ref)):
                for cb in range(NB):
                    ref[pl.ds(row0, 8), cb * LANES:(cb + 1) * LANES] = tiles[n][cb]
            return carry

        lax.fori_loop(0, n8, back, 0)

    row_spec = pl.BlockSpec((SCAN_CHUNK, C), lambda i: (nc - 1 - i, 0))
    col_spec = pl.BlockSpec((n8, HEAD, LANES), lambda i: (nc - 1 - i, 0, 0))
    consts = _scan_consts()
    return _pcall(
        body, name="wkv_bwd", grid=(nc,),
        in_specs=[row_spec] * 5 + [col_spec, col_spec, pl.BlockSpec((1, HEAD, C), lambda i: (nc - 1 - i, 0, 0))]
        + [pl.BlockSpec(c.shape, lambda i: (0, 0)) for c in consts],
        out_specs=[row_spec] * 5 + [col_spec],
        out_shape=[jax.ShapeDtypeStruct((t, C), F32)] * 5 + [jax.ShapeDtypeStruct((t // 8, HEAD, LANES), F32)],
        scratch_shapes=[pltpu.VMEM((HEAD, C), F32), pltpu.VMEM((HEAD, C), F32),
                        pltpu.VMEM((SCAN_CHUNK + 1, HEAD, C), F32), pltpu.VMEM((SCAN_CHUNK, HEAD, C), F32)],
        compiler_params=_cparams(("arbitrary",)),
    )(r, w, k, a, b, v3, dy3, ckpt, *consts)


def _prev_rows(cur, before, first_tile):
    last = jnp.where(first_tile, 0.0, before[7:8, :])
    row = lax.broadcasted_iota(jnp.int32, cur.shape, 0)
    return jnp.where(row == 0, last, pltpu.roll(cur, 1, 0))


def shift_fwd(p, mu, col, name):
    t, width = p.shape[0], mu.shape[1]
    tm = _tile(t, 256)

    def body(p_ref, before_ref, mu_ref, o_ref):
        cur = p_ref[...]
        prev = _prev_rows(cur, before_ref[...], pl.program_id(0) == 0)
        o_ref[...] = cur + (prev - cur) * mu_ref[...]

    return _pcall(
        body, name=name, grid=(t // tm,),
        in_specs=[pl.BlockSpec((tm, width), lambda i: (i, col)),
                  pl.BlockSpec((8, width), lambda i: (jnp.maximum(i * (tm // 8) - 1, 0), col)),
                  pl.BlockSpec((1, width), lambda i: (0, 0))],
        out_specs=pl.BlockSpec((tm, width), lambda i: (i, 0)),
        out_shape=jax.ShapeDtypeStruct((t, width), F32),
        compiler_params=_cparams(("parallel",)),
    )(p, p, mu)


def shift_bwd(dps, p, mu, col, name):
    t, width = dps.shape
    tm = _tile(t, 256)
    nt = t // tm

    def body(d_ref, after_ref, p_ref, before_ref, mu_ref, dp_ref, dmu_ref):
        i = pl.program_id(0)
        d, cur, mu_v = d_ref[...], p_ref[...], mu_ref[...]
        row = lax.broadcasted_iota(jnp.int32, d.shape, 0)
        first_after = jnp.where(i == nt - 1, 0.0, after_ref[0:1, :])
        nxt = jnp.where(row == tm - 1, first_after, pltpu.roll(d, tm - 1, 0))
        dp_ref[...] = (d * (1.0 - mu_v) + nxt * mu_v).astype(BF16)
        prev = _prev_rows(cur, before_ref[...], i == 0)
        part = jnp.sum(d * (prev - cur), axis=0, keepdims=True)

        @pl.when(i == 0)
        def _():
            dmu_ref[...] = part

        @pl.when(i > 0)
        def _():
            dmu_ref[...] += part

    return _pcall(
        body, name=name, grid=(nt,),
        in_specs=[pl.BlockSpec((tm, width), lambda i: (i, 0)),
                  pl.BlockSpec((8, width), lambda i: (jnp.minimum((i + 1) * (tm // 8), t // 8 - 1), 0)),
                  pl.BlockSpec((tm, width), lambda i: (i, col)),
                  pl.BlockSpec((8, width), lambda i: (jnp.maximum(i * (tm // 8) - 1, 0), col)),
                  pl.BlockSpec((1, width), lambda i: (0, 0))],
        out_specs=[pl.BlockSpec((tm, width), lambda i: (i, 0)), pl.BlockSpec((1, width), lambda i: (0, 0))],
        out_shape=[jax.ShapeDtypeStruct((t, width), BF16), jax.ShapeDtypeStruct((1, width), F32)],
        compiler_params=_cparams(("arbitrary",)),
    )(dps, dps, p, p, mu)


def loss_head(x3, tgt, g):
    t, d = x3.shape
    tm = _tile(t, 256)

    def body(x_ref, t_ref, g_ref, loss_ref, dx_ref, dg_ref, dxb_ref):
        (y,), vjp = jax.vjp(lambda a, b: rms_fn(a, b), x_ref[...], g_ref[...])
        diff = y - t_ref[...]
        part = 0.5 * jnp.sum(jnp.mean(diff * diff, axis=-1, keepdims=True), axis=0, keepdims=True)
        dx, dg = vjp((diff * (1.0 / d),))
        dx_ref[...] = dx
        dxb_ref[...] = dx.astype(BF16)
        part = jnp.broadcast_to(part, (1, LANES))
        first = pl.program_id(0) == 0

        @pl.when(first)
        def _():
            loss_ref[...] = part
            dg_ref[...] = dg

        @pl.when(jnp.logical_not(first))
        def _():
            loss_ref[...] += part
            dg_ref[...] += dg

    row = pl.BlockSpec((tm, d), lambda i: (i, 0))
    vec = pl.BlockSpec((1, d), lambda i: (0, 0))
    return _pcall(
        body, name="loss_head", grid=(t // tm,), in_specs=[row, row, vec],
        out_specs=[pl.BlockSpec((1, LANES), lambda i: (0, 0)), row, vec, row],
        out_shape=[jax.ShapeDtypeStruct((1, LANES), F32), jax.ShapeDtypeStruct((t, d), F32),
                   jax.ShapeDtypeStruct((1, d), F32), jax.ShapeDtypeStruct((t, d), BF16)],
        compiler_params=_cparams(("arbitrary",)),
    )(x3, tgt, g)


ADAM_LR, ADAM_B1, ADAM_B2, ADAM_EPS, ADAM_WD, ADAM_STEP = 0.001, 0.9, 0.999, 1e-08, 0.01, 10


def adamw(w, g, m, v, name):
    rows, width = w.shape
    tr = _tile(rows, 256)

    def body(w_ref, g_ref, m_ref, v_ref, d_ref, nm_ref, nv_ref):
        gv = g_ref[...]
        m_new = ADAM_B1 * m_ref[...] + (1.0 - ADAM_B1) * gv
        v_new = ADAM_B2 * v_ref[...] + (1.0 - ADAM_B2) * (gv * gv)
        m_hat = m_new / (1.0 - ADAM_B1 ** ADAM_STEP)
        v_hat = v_new / (1.0 - ADAM_B2 ** ADAM_STEP)
        d_ref[...] = -ADAM_LR * (m_hat / (jnp.sqrt(v_hat) + ADAM_EPS) + ADAM_WD * w_ref[...])
        nm_ref[...] = m_new
        nv_ref[...] = v_new

    spec = pl.BlockSpec((tr, width), lambda i: (i, 0))
    return _pcall(
        body, name=name, grid=(rows // tr,), in_specs=[spec] * 4, out_specs=[spec] * 3,
        out_shape=[jax.ShapeDtypeStruct((rows, width), F32)] * 3,
        compiler_params=_cparams(("parallel",)),
    )(w, g, m, v)


def sum_slots(q, name, for_swap=False):
    n, rows, width = q.shape
    tr = _tile(rows, 512)

    def body(*refs):
        refs = refs[for_swap:]
        acc = refs[0][...].astype(F32)
        for r in refs[1:n]:
            acc = acc + r[...].astype(F32)
        refs[n][...] = acc

    if not for_swap:
        specs = [pl.BlockSpec((None, tr, width), functools.partial(lambda s, i: (s, i, 0), s)) for s in range(n)]
        return _pcall(body, name=name, grid=(rows // tr,), in_specs=specs,
                      out_specs=pl.BlockSpec((tr, width), lambda i: (i, 0)),
                      out_shape=jax.ShapeDtypeStruct((rows, width), F32),
                      compiler_params=_cparams(("parallel",)))(*([q] * n))
    specs = [pl.BlockSpec((None, tr, width), functools.partial(lambda s, i, c_ref: (s, i, 0), s)) for s in range(n)]
    grid_spec = pltpu.PrefetchScalarGridSpec(
        num_scalar_prefetch=1, grid=(rows // tr,), in_specs=specs,
        out_specs=pl.BlockSpec((None, tr, width), lambda i, c_ref: (c_ref[0], i, 0)))
    return _pcall(body, name=name, grid_spec=grid_spec, out_shape=jax.ShapeDtypeStruct((2, rows, width), F32),
                  compiler_params=_cparams(("parallel",)))(
        lax.axis_index("c").astype(jnp.int32).reshape(1), *([q] * n))


MESH = pl.DeviceIdType.MESH
ANY_SPEC = pl.BlockSpec(memory_space=pl.ANY)


def _coords():
    return lax.axis_index("x"), lax.axis_index("y"), lax.axis_index("c")


def _other_chips(x, y):
    return [(1 - x, y), (x, 1 - y), (1 - x, 1 - y)]


def push(name, srcs, n_dst, plan_fn):
    n_arr = len(srcs)
    n_send = len(plan_fn(0, 0, 0))

    def body(*refs):
        src_refs, dst_refs = refs[:n_arr], refs[n_arr:2 * n_arr]
        send_sems, recv_sems = refs[2 * n_arr:]
        sends = plan_fn(*_coords())
        out = []
        for q, (src_ref, dst_ref) in enumerate(zip(src_refs, dst_refs)):
            out += [pltpu.make_async_remote_copy(src_ref.at[si], dst_ref.at[di], send_sems.at[q * n_send + k],
                                                 recv_sems.at[q * n_send + k], device_id=peer, device_id_type=MESH)
                    for k, (si, peer, di, _) in enumerate(sends)]
        for cp in out:
            cp.start()
        for q, (src_ref, dst_ref) in enumerate(zip(src_refs, dst_refs)):
            for k, (si, peer, _, ri) in enumerate(sends):
                pltpu.make_async_remote_copy(src_ref.at[si], dst_ref.at[ri], send_sems.at[q * n_send + k],
                                             recv_sems.at[q * n_send + k], device_id=peer,
                                             device_id_type=MESH).wait_recv()
        for cp in out:
            cp.wait_send()

    return _pcall(
        body, name=name, in_specs=[ANY_SPEC] * n_arr, out_specs=[ANY_SPEC] * n_arr,
        out_shape=[jax.ShapeDtypeStruct((n_dst,) + s.shape[1:], s.dtype) for s in srcs],
        scratch_shapes=[pltpu.SemaphoreType.DMA((n_arr * n_send,)), pltpu.SemaphoreType.DMA((n_arr * n_send,))],
    )(*srcs)


def plan_sibling_quarters(x, y, c):
    return [(2 * s + (1 - c), (x, y, 1 - c), s, s) for s in range(4)]


def plan_chips_by_shard(x, y, c):
    me = 2 * x + y
    return [(2 * px + py, (px, py, c), me, 2 * px + py) for px, py in _other_chips(x, y)]


def plan_chips_half(x, y, c):
    me = 2 * x + y
    return [(c, (px, py, c), me, 2 * px + py) for px, py in _other_chips(x, y)]


def swap_halves(name, bufs):
    n_arr = len(bufs)

    def body(*refs):
        buf_refs = refs[n_arr:2 * n_arr]
        send_sems, recv_sems = refs[2 * n_arr:]
        x, y, c = _coords()
        out = [pltpu.make_async_remote_copy(b.at[c], b.at[c], send_sems.at[q], recv_sems.at[q],
                                            device_id=(x, y, 1 - c), device_id_type=MESH)
               for q, b in enumerate(buf_refs)]
        for cp in out:
            cp.start()
        for q, b in enumerate(buf_refs):
            pltpu.make_async_remote_copy(b.at[1 - c], b.at[1 - c], send_sems.at[q], recv_sems.at[q],
                                         device_id=(x, y, 1 - c), device_id_type=MESH).wait_recv()
        for cp in out:
            cp.wait_send()

    return _pcall(
        body, name=name, in_specs=[ANY_SPEC] * n_arr, out_specs=[ANY_SPEC] * n_arr,
        out_shape=[jax.ShapeDtypeStruct(b.shape, b.dtype) for b in bufs],
        input_output_aliases={q: q for q in range(n_arr)},
        scratch_shapes=[pltpu.SemaphoreType.DMA((n_arr,)), pltpu.SemaphoreType.DMA((n_arr,))],
    )(*bufs)


HBM_SPEC = pl.BlockSpec(memory_space=pltpu.HBM)
SEM_SPEC = pl.BlockSpec(memory_space=pltpu.SEMAPHORE)
DATAFLOW = pltpu.SideEffectType.DATAFLOW_SIDE_EFFECTING


def _in_hbm(a):
    return pltpu.with_memory_space_constraint(a, pltpu.HBM)


def push_start(name, srcs, lands, plan_fn, after):
    n_arr = len(srcs)
    n_send = len(plan_fn(0, 0, 0))
    arrays = list(srcs) + ([] if lands is None else list(lands))
    if after is None:
        after = jnp.zeros((8, LANES), F32)
    n_all = len(arrays)

    def body(*refs):
        src_refs, land_refs = refs[:n_arr], refs[n_all - n_arr:n_all]
        send_sems, recv_sems = refs[n_all + 1:n_all + 3]
        token = refs[-1]
        sends = plan_fn(*_coords())
        for q, (src_ref, land_ref) in enumerate(zip(src_refs, land_refs)):
            for k, (si, peer, di, _) in enumerate(sends):
                pltpu.make_async_remote_copy(src_ref.at[si], land_ref.at[di], send_sems.at[q * n_send + k],
                                             recv_sems.at[q * n_send + k], device_id=peer,
                                             device_id_type=MESH).start()
        token[...] = jnp.zeros_like(token)

    sems = pltpu.SemaphoreType.DMA((n_arr * n_send,))
    out = _pcall(
        body, name=name,
        in_specs=[HBM_SPEC] * n_all + [ANY_SPEC],
        out_specs=[SEM_SPEC, SEM_SPEC] + [HBM_SPEC] * n_all + [pl.BlockSpec(memory_space=pltpu.VMEM)],
        out_shape=[sems, sems] + [pltpu.HBM(a.shape, a.dtype) for a in arrays]
        + [jax.ShapeDtypeStruct((8, LANES), F32)],
        input_output_aliases={q: 2 + q for q in range(n_all)},
        compiler_params=pltpu.CompilerParams(has_side_effects=DATAFLOW),
    )(*[_in_hbm(a) for a in arrays], after)
    return out[:-1], out[-1]


def push_wait(name, state, plan_fn, after, in_place=False, both=False):
    arrays = list(state[2:])
    n_all = len(arrays)
    n_arr = n_all if in_place else n_all // 2
    n_send = len(plan_fn(0, 0, 0))
    send_sems, recv_sems = state[:2]

    def body(*refs):
        src_refs, land_refs = refs[:n_arr], refs[n_all - n_arr:n_all]
        send_ref, recv_ref = refs[n_all:n_all + 2]
        sends = plan_fn(*_coords())
        for q, (src_ref, land_ref) in enumerate(zip(src_refs, land_refs)):
            for k, (si, peer, _, ri) in enumerate(sends):
                cp = pltpu.make_async_remote_copy(src_ref.at[si], land_ref.at[ri], send_ref.at[q * n_send + k],
                                                  recv_ref.at[q * n_send + k], device_id=peer, device_id_type=MESH)
                cp.wait_send()
                cp.wait_recv()

    out = _pcall(
        body, name=name,
        in_specs=[HBM_SPEC] * n_all + [SEM_SPEC, SEM_SPEC, ANY_SPEC],
        out_specs=[HBM_SPEC] * n_all,
        out_shape=[pltpu.HBM(a.shape, a.dtype) for a in arrays],
        input_output_aliases={q: q for q in range(n_all)},
        compiler_params=pltpu.CompilerParams(has_side_effects=DATAFLOW),
    )(*arrays, send_sems, recv_sems, after)
    return list(out) if both else list(out[n_all - n_arr:])


def plan_chips_gather(x, y, c):
    me = 2 * x + y
    return [(c, (px, py, c), 2 * me + c, 2 * (2 * px + py) + c) for px, py in _other_chips(x, y)]


def plan_forward(x, y, c):
    return [(2 * (2 * px + py) + c, (x, y, 1 - c), 2 * (2 * px + py) + c, 2 * (2 * px + py) + (1 - c))
            for px, py in _other_chips(x, y)]


def add_kept(g8, other, name):
    _, rows, cols = other.shape
    tr = _tile(rows, 512)

    def body(c_ref, a_ref, b_ref, o_ref):
        o_ref[...] = (a_ref[...].astype(F32) + b_ref[...].astype(F32)).astype(BF16)

    spec = pl.BlockSpec((None, tr, cols), lambda s, i, c_ref: (s, i, 0))
    grid_spec = pltpu.PrefetchScalarGridSpec(
        num_scalar_prefetch=1, grid=(4, rows // tr),
        in_specs=[pl.BlockSpec((None, tr, cols), lambda s, i, c_ref: (2 * s + c_ref[0], i, 0)), spec],
        out_specs=spec)
    return _pcall(body, name=name, grid_spec=grid_spec, out_shape=jax.ShapeDtypeStruct((4, rows, cols), BF16),
                  compiler_params=_cparams(("parallel", "parallel")))(
        lax.axis_index("c").astype(jnp.int32).reshape(1), g8, other)


def reduce_scatter_begin(grads, names, tag):
    g8 = [g.reshape(8, g.shape[1] // 2, g.shape[2]) for g in grads]
    return _chip_sums_start(g8, push(tag + "_sibling", g8, 4, plan_sibling_quarters), names, tag)


def _chip_sums_start(g8, from_sibling, names, tag):
    chip_sum = [add_kept(a, b, tag + "_add_" + n) for a, b, n in zip(g8, from_sibling, names)]
    return push_start(tag + "_start", chip_sum, chip_sum, plan_chips_by_shard, None)


def sibling_exchange_begin(grads, tag):
    g8 = [g.reshape(8, g.shape[1] // 2, g.shape[2]) for g in grads]
    lands = [lax.empty((4,) + g.shape[1:], g.dtype) for g in g8]
    return push_start(tag + "_sib_start", g8, lands, plan_sibling_quarters, None)


def reduce_scatter_continue(state, names, tag, after):
    n = len(names)
    arrays = push_wait(tag + "_sib_wait", state, plan_sibling_quarters, after, both=True)
    return _chip_sums_start(arrays[:n], arrays[n:], names, tag)


def reduce_scatter_end(state, names, tag, after):
    by_chip = push_wait(tag + "_wait", state, plan_chips_by_shard, after)
    both = swap_halves(tag + "_halves", [sum_slots(q, tag + "_sum_" + n, for_swap=True)
                                         for q, n in zip(by_chip, names)])
    return [b.reshape(2 * b.shape[1], b.shape[2]) for b in both]


def _landing(mine):
    return [jnp.broadcast_to(m[None], (4,) + m.shape).reshape((8,) + m.shape[1:]) for m in mine]


def gather_begin(tag, mine, after, lands=None):
    return push_start(tag + "_start", mine, _landing(mine) if lands is None else lands, plan_chips_gather, after)


def gather_forward(tag, state, after):
    lands = push_wait(tag + "_wait", state, plan_chips_gather, after)
    return push_start(tag + "_fwd_start", lands, None, plan_forward, None)


def gather_end(tag, state, after):
    bufs = push_wait(tag + "_fwd_wait", state, plan_forward, after, in_place=True)
    return [g.reshape(4, 2 * g.shape[1], g.shape[2]) for g in bufs]


PACK_QUANTUM = 8 * C


def _pack(parts, dtype):
    pieces = []
    for p in parts:
        flat = p.astype(dtype).reshape(-1)
        pieces.append(jnp.pad(flat, (0, -flat.shape[0] % PACK_QUANTUM)).reshape(-1, C))
    rows = sum(p.shape[0] for p in pieces)
    if rows % 16:
        pieces.append(jnp.zeros((8, C), dtype))
    return jnp.concatenate(pieces, axis=0)


def _unpack(packed, shapes):
    out, at = [], 0
    for s in shapes:
        n = int(np.prod(s))
        rows = -(-n // PACK_QUANTUM) * 8
        out.append(packed[at:at + rows].reshape(-1)[:n].reshape(s))
        at += rows
    return out


def all_reduce_begin(vec):
    rows, width = vec.shape
    (pair,) = swap_halves("ar_sibling", [jnp.broadcast_to(vec[None], (2, rows, width))])
    chip_sum = sum_slots(pair, "ar_sum_sibling").reshape(2, rows // 2, width)
    mine = lax.dynamic_index_in_dim(chip_sum, lax.axis_index("c"), axis=0, keepdims=True)
    return push_start("ar_start", [chip_sum], [jnp.broadcast_to(mine, (4, rows // 2, width))], plan_chips_half,
                      None)


def all_reduce_end(state, after):
    (by_chip,) = push_wait("ar_wait", state, plan_chips_half, after)
    (both,) = swap_halves("ar_halves", [sum_slots(by_chip, "ar_sum_chips", for_swap=True)])
    return both.reshape(2 * both.shape[1], both.shape[2])


def kernel(x, norm1_g, w_in, shift_mu, w0, w2, a0, a2, g2, k_k, k_a, r_k, gn_w, gn_b, sgu_ln_g, sgu_ln_b, sgu_w, sgu_b, sgu_out_g, w_out, norm2_g, w_gate, w_up, w_down, final_g, loss_target, m_norm1_g, m_w_in, m_shift_mu, m_w0, m_w2, m_a0, m_a2, m_g2, m_k_k, m_k_a, m_r_k, m_gn_w, m_gn_b, m_sgu_ln_g, m_sgu_ln_b, m_sgu_w, m_sgu_b, m_sgu_out_g, m_w_out, m_norm2_g, m_w_gate, m_w_up, m_w_down, m_final_g, v_norm1_g, v_w_in, v_shift_mu, v_w0, v_w2, v_a0, v_a2, v_g2, v_k_k, v_k_a, v_r_k, v_gn_w, v_gn_b, v_sgu_ln_g, v_sgu_ln_b, v_sgu_w, v_sgu_b, v_sgu_out_g, v_w_out, v_norm2_g, v_w_gate, v_w_up, v_w_down, v_final_g):
    args = dict(locals())
    names = ["norm1_g", "w_in", "shift_mu", "w0", "w2", "a0", "a2", "g2", "k_k", "k_a", "r_k", "gn_w", "gn_b",
             "sgu_ln_g", "sgu_ln_b", "sgu_w", "sgu_b", "sgu_out_g", "w_out", "norm2_g", "w_gate", "w_up", "w_down",
             "final_g"]
    big = ["w_in", "w_out", "w_gate", "w_up", "w_down"]
    xi, yi = lax.axis_index("x"), lax.axis_index("y")
    chip = 2 * xi + yi
    x2d, tgt = x[0], loss_target[0]
    t = x2d.shape[0]

    lora_names = ["w2", "a2", "g2"]
    first, later = ["w_in"] + lora_names, ["w_out", "w_gate", "w_up", "w_down"]

    def halves(n, token=None):
        m = args[n][0] if token is None else args[n][0] + token[0, 0]
        m = m.astype(BF16)
        return m.reshape(2, m.shape[0] // 2, m.shape[1])

    first_state, first_token = gather_begin("g0", [halves(n) for n in first], norm1_g)
    later_mine = [halves(n, first_token) for n in later]
    later_lands = _landing(later_mine)
    (h1,) = row_fwd("norm1", rms_fn, [Tiled(x2d)], [norm1_g + first_token[0, 0]], [], [(D, BF16)], _tile(t, 256))
    prepared = functools.reduce(lambda p, q: p + q, [a[0, 0, :1].astype(F32) for a in later_lands])
    first_state, first_token = gather_forward("g0", first_state, prepared + h1[0, :1].astype(F32))
    got = gather_end("g0", first_state, first_token)
    later_state, later_token = gather_begin("g1", later_mine, got[0], later_lands)
    full = {n: blk.transpose(1, 0, 2).reshape(blk.shape[1], 4 * blk.shape[2]) for n, blk in zip(first, got)}
    w_in_f = full["w_in"]
    w_in_p = jnp.concatenate([w_in_f[:, :3 * C], w_in_f[:, 3360:], w_in_f[:, 3 * C:3360],
                              jnp.zeros((D, P_W - D_IN), BF16)], axis=1)
    mu_rkv = shift_mu[:, :3 * C] + later_token[0, 0]
    mu_lora = jnp.pad(shift_mu[:, 3 * C:], ((0, 0), (0, LORA_W - 288)))
    w2p = jnp.pad(full["w2"], ((0, 64), (0, 0)))
    a2p = jnp.pad(full["a2"], ((64, 0), (0, 0)))
    g2p = jnp.pad(full["g2"], ((0, 96), (0, 0)))
    ii = np.arange(C)
    jm = _scan_consts()[2]
    e_np = (np.arange(LANES)[:, None] == (ii[None, :] // HEAD))
    e_mat, e_t = jnp.asarray(e_np, BF16), jnp.asarray(e_np.T, BF16)
    b_t = jnp.pad(sgu_b[0].T, ((0, 0), (0, LANES - HEADS)))
    r_k_flat = r_k.reshape(1, C)
    tm_row = _tile(t, 256)
    tm_vjp = _tile(t, 128)

    p = mm(h1, w_in_p, name="proj_in")
    ps_rkv = shift_fwd(p, mu_rkv, 0, "shift_rkv")
    ps_lora = shift_fwd(p, mu_lora, P_LORA // LORA_W, "shift_lora")
    pre_par = [w0, w2p, a0, a2p, g2p, k_k, k_a]
    r_, w_, k2, v_, an, b_, g_ = row_fwd("pre", pre_fn, [Tiled(ps_rkv), Tiled(ps_lora)], pre_par, [jm],
                                         [(C, F32)] * 7, tm_vjp)
    v3 = to_cols(v_)
    y3, ckpt = wkv_fwd(r_, w_, k2, v3, an, b_)
    later_state, later_token = gather_forward("g1", later_state, y3)
    y_ = from_cols(y3 + later_token[0, 0])
    post_in = [Tiled(z) for z in (y_, r_, k2, v_, g_)]
    post_par = [gn_w, gn_b, r_k_flat]
    (y_rwkv,) = row_fwd("post", post_fn, post_in, post_par, [jm], [(C, BF16)], tm_vjp)
    sgu_in = [Tiled(p, C, 3), Tiled(p, C, 4)]
    sgu_par = [sgu_ln_g, sgu_ln_b, sgu_w[0], b_t, sgu_out_g]
    (y_sgu,) = row_fwd("sgu", sgu_fn, sgu_in, sgu_par, [e_mat, e_t], [(C, BF16)], tm_vjp)
    y_cat = jnp.concatenate([y_rwkv, y_sgu], axis=1)
    for n, blk in zip(later, gather_end("g1", later_state, y_cat)):
        full[n] = blk if n in ("w_gate", "w_up") else blk.reshape(4 * blk.shape[1], blk.shape[2])
    x_mid = mm(y_cat, full["w_out"], res=x2d, name="proj_out")
    (h2,) = row_fwd("norm2", rms_fn, [Tiled(x_mid)], [norm2_g], [], [(D, BF16)], tm_row)
    gate = mm(h2, full["w_gate"], shards="b", name="ffn_gate")
    up, act = mm(h2, full["w_up"], shards="b", name="ffn_up",
                 post=(lambda up_tile, gate_tile: (up_tile, swiglu_fn(gate_tile, up_tile)[0]), [gate], [F32, BF16]))
    x_out = mm(act, full["w_down"], res=x_mid, name="ffn_down")
    loss_part, dx_out, d_final_g, dx_out_b = loss_head(x_out, tgt, final_g.reshape(1, D))

    grads = {"final_g": d_final_g.reshape(D)}
    d_gate, d_up = mm(dx_out_b, full["w_down"], tb=True, name="d_act",
                      post=(lambda d_act, g_tile, u_tile: jax.vjp(swiglu_fn, g_tile, u_tile)[1]((d_act,)), [gate, up],
                            [BF16, BF16]))
    grads["w_down"] = mm(act, dx_out_b, ta=True, out_dtype=BF16, name="d_w_down")
    grads["w_gate"] = mm(h2, d_gate, shards="out", out_dtype=BF16, name="d_w_gate")
    grads["w_up"] = mm(h2, d_up, shards="out", out_dtype=BF16, name="d_w_up")
    d_h2 = mm(d_gate, full["w_gate"], shards="bt", name="d_h2_gate")
    d_h2 = mm(d_up, full["w_up"], shards="bt", res=d_h2, name="d_h2_up")
    dx_mid, grads["norm2_g"], dx_mid_b = row_bwd("norm2_b", rms_res_fn, [Tiled(x_mid)], [norm2_g], [],
                                                 [Tiled(d_h2), Tiled(dx_out)], tm_row, bf16_copy_of=(0,))
    d_ycat = mm(dx_mid_b, full["w_out"], tb=True, name="d_ycat")
    grads["w_out"] = mm(y_cat, dx_mid_b, ta=True, out_dtype=BF16, name="d_w_out")
    rs_state, rs_token = sibling_exchange_begin([grads[n].reshape((4,) + args[n].shape[1:]) for n in later], "rs2")
    d_pu, d_pv, grads["sgu_ln_g"], grads["sgu_ln_b"], d_sgu_w, d_b_t, grads["sgu_out_g"] = row_bwd(
        "sgu_b", sgu_fn, sgu_in, sgu_par, [e_mat, e_t], [Tiled(d_ycat, C, 1)], tm_vjp,
        tiled_out_dtypes=[BF16, BF16])
    grads["sgu_w"] = d_sgu_w[None]
    grads["sgu_b"] = d_b_t[:, :HEADS].T[None]
    post_par_b = [gn_w + rs_token[0, 0]] + post_par[1:]
    d_y, d_r1, d_k1, d_v1, d_g, grads["gn_w"], grads["gn_b"], d_r_k = row_bwd(
        "post_b", post_fn, post_in, post_par_b, [jm], [Tiled(d_ycat, C, 0)], tm_vjp)
    grads["r_k"] = d_r_k.reshape(r_k.shape)
    rs_state, rs_token = reduce_scatter_continue(rs_state, later, "rs2", d_y)
    d_r2, d_w, d_k2, d_a, d_b, d_v3 = wkv_bwd(r_, w_, k2, v3, an, b_, ckpt, to_cols(d_y + rs_token[0, 0]))
    d_v2 = from_cols(d_v3)
    pre_cts = [(Tiled(d_r1), Tiled(d_r2)), Tiled(d_w), (Tiled(d_k1), Tiled(d_k2)), (Tiled(d_v1), Tiled(d_v2)),
               Tiled(d_a), Tiled(d_b), Tiled(d_g)]
    d_ps_rkv, d_ps_lora, grads["w0"], d_w2p, grads["a0"], d_a2p, d_g2p, grads["k_k"], grads["k_a"] = row_bwd(
        "pre_b", pre_fn, [Tiled(ps_rkv), Tiled(ps_lora)], pre_par, [jm], pre_cts, tm_vjp)
    grads["w2"], grads["a2"], grads["g2"] = d_w2p[None, :64], d_a2p[None, 64:], d_g2p[None, :160]
    d_p_rkv, d_mu_rkv = shift_bwd(d_ps_rkv, p, mu_rkv, 0, "shift_rkv_b")
    d_p_lora, d_mu_lora = shift_bwd(d_ps_lora, p, mu_lora, P_LORA // LORA_W, "shift_lora_b")
    grads["shift_mu"] = jnp.concatenate([d_mu_rkv, d_mu_lora[:, :288]], axis=1)
    d_p = jnp.concatenate([d_p_rkv, d_pu, d_pv, d_p_lora], axis=1)
    d_h1 = mm(d_p, w_in_p, tb=True, name="d_h1")
    d_w_in_p = mm(h1, d_p, ta=True, out_dtype=BF16, name="d_w_in")
    d_w_in = jnp.concatenate([d_w_in_p[:, :3 * C], d_w_in_p[:, P_LORA:P_LORA + 288], d_w_in_p[:, 3 * C:P_LORA]],
                             axis=1)
    grads["w_in"] = d_w_in.reshape(D, 4, D_IN // 4).transpose(1, 0, 2)
    dx, grads["norm1_g"] = row_bwd("norm1_b", rms_res_fn, [Tiled(x2d)], [norm1_g], [],
                                   [Tiled(d_h1), Tiled(dx_mid)], tm_row)

    small = [n for n in names if n not in big]
    small_shapes = [(1, 64, C) if n in ("w2", "a2") else (1, 160, C) if n == "g2" else args[n].shape for n in small]
    vec = _pack([grads[n] for n in small], F32)
    w_in_state, w_in_token = reduce_scatter_begin([grads["w_in"]], ["w_in"], "rs1")
    small_state, small_token = all_reduce_begin(vec)
    total = {n: g[None] for n, g in zip(later, reduce_scatter_end(rs_state, later, "rs2", w_in_token + small_token))}
    loss = lax.psum(loss_part[0, 0], ("x", "y", "c"))
    delta, new_m, new_v = {}, {}, {}

    def adam_big(n):
        s = args[n].shape
        d_, m_, v__ = adamw(args[n][0], total[n][0], args["m_" + n][0], args["v_" + n][0], "adamw_" + n)
        delta[n], new_m[n], new_v[n] = d_.reshape(s), m_.reshape(s), v__.reshape(s)

    for n in later:
        adam_big(n)
    total["w_in"] = reduce_scatter_end(w_in_state, ["w_in"], "rs1", new_v["w_down"])[0][None]
    for n, g_sum in zip(small, _unpack(all_reduce_end(small_state, new_v["w_down"]), small_shapes)):
        total[n] = lax.dynamic_slice_in_dim(g_sum, chip * 256, 256, axis=2) if n in lora_names else g_sum
    adam_big("w_in")
    flat = [_pack([src[n] for n in small], F32)
            for src in (args, total, {n: args["m_" + n] for n in small}, {n: args["v_" + n] for n in small})]
    outs = adamw(*flat, "adamw_small")
    for res, o in zip((delta, new_m, new_v), outs):
        res.update(zip(small, _unpack(o, [args[n].shape for n in small])))
    return (loss, dx[None], *[total[n] for n in names], *[delta[n] for n in names],
            *[new_m[n] for n in names], *[new_v[n] for n in names])
```

```python
import functools

import numpy as np
import jax
import jax.numpy as jnp
from jax import lax
from jax.experimental import pallas as pl
from jax.experimental.pallas import tpu as pltpu

F32 = jnp.float32
BF16 = jnp.bfloat16

D = 2048
C = 1024
HEADS = 16
HEAD = 64
LANES = 128
P_W = 5632
P_LORA = 5120
LORA_W = 512
D_IN = 5408
CHUNK = 128
RMS_EPS = 1e-6
LN_EPS = 1e-5
GN_EPS = 64e-5
L2_EPS = 1e-12
VMEM_LIMIT = 56 * 1024 * 1024


def _pcall(body, **kw):
    return pl.pallas_call(body, **kw)


def _cparams(sem):
    return pltpu.CompilerParams(dimension_semantics=sem, vmem_limit_bytes=VMEM_LIMIT)


def _tile(n, most):
    t = most
    while t > 8 and n % t:
        t //= 2
    assert n % t == 0, (n, most)
    return t


MM_TILE = (2048, 512, 2048)
MM_VMEM = 40 * 1024 * 1024


def _div_tile(n, most, quantum=LANES):
    for t in range(min(n, most) // quantum * quantum, 0, -quantum):
        if n % t == 0:
            return t
    raise ValueError((n, most, quantum))


def _mm_tiles(m, n, k, a_bytes, b_bytes, r_bytes, o_bytes):
    tm, tn, tk = _div_tile(m, MM_TILE[0]), _div_tile(n, MM_TILE[1]), _div_tile(k, MM_TILE[2])

    def need(tm, tk):
        return 2 * (tm * tk * a_bytes + tk * tn * b_bytes + tm * tn * (r_bytes + o_bytes)) + tm * tn * 4

    while need(tm, tk) > MM_VMEM:
        if tk >= tm and tk > 512:
            tk = _div_tile(k, tk - LANES)
        else:
            tm = _div_tile(m, tm - LANES)
    return tm, tn, tk


def mm(a, b, *, ta=False, tb=False, res=None, out_dtype=F32, name, shards=None, post=None):
    if shards is not None:
        return _mm_shards(a, b, res, out_dtype, name, shards, post)
    m, k = (a.shape[1], a.shape[0]) if ta else a.shape
    n = b.shape[0] if tb else b.shape[1]
    assert (b.shape[1] if tb else b.shape[0]) == k
    if post is None:
        tm, tn, tk = _mm_tiles(m, n, k, a.dtype.itemsize, b.dtype.itemsize,
                               0 if res is None else res.dtype.itemsize, jnp.dtype(out_dtype).itemsize)
    else:
        tm, tn, tk = _div_tile(m, 1024), _div_tile(n, MM_TILE[1]), _div_tile(k, MM_TILE[2])
    nk = k // tk
    dims = (((0 if ta else 1,), (1 if tb else 0,)), ((), ()))
    a_spec = pl.BlockSpec((tk, tm), lambda i, j, l: (l, i)) if ta else pl.BlockSpec((tm, tk), lambda i, j, l: (i, l))
    b_spec = pl.BlockSpec((tn, tk), lambda i, j, l: (j, l)) if tb else pl.BlockSpec((tk, tn), lambda i, j, l: (l, j))
    o_spec = pl.BlockSpec((tm, tn), lambda i, j, l: (i, j))
    return _mm_call(a, b, res, dims, (m // tm, n // tn, nk), a_spec, b_spec, o_spec, o_spec, (tm, tn),
                    jax.ShapeDtypeStruct((m, n), out_dtype), name, post)


def _mm_shards(a, b, res, out_dtype, name, shards, post=None):
    assert post is None or shards == "b"
    if shards == "b":
        (m, k), ns = a.shape, b.shape[2]
        tm, tk = _div_tile(m, 1024 if post is None else 512), _div_tile(k, 2048)
        grid, dims, acc = (m // tm, 4, k // tk), (((1,), (0,)), ((), ())), (tm, ns)
        a_spec = pl.BlockSpec((tm, tk), lambda i, s, l: (i, l))
        b_spec = pl.BlockSpec((None, tk, ns), lambda i, s, l: (s, l, 0))
        o_spec = pl.BlockSpec((tm, ns), lambda i, s, l: (i, s))
        out = jax.ShapeDtypeStruct((m, 4 * ns), out_dtype)
    elif shards == "bt":
        m, (_, n, ns) = a.shape[0], b.shape
        tm, tn = _div_tile(m, 2048), _div_tile(n, 512)
        grid, dims, acc = (m // tm, n // tn, 4), (((1,), (1,)), ((), ())), (tm, tn)
        a_spec = pl.BlockSpec((tm, ns), lambda i, j, s: (i, s))
        b_spec = pl.BlockSpec((None, tn, ns), lambda i, j, s: (s, j, 0))
        o_spec = pl.BlockSpec((tm, tn), lambda i, j, s: (i, j))
        out = jax.ShapeDtypeStruct((m, n), out_dtype)
    else:
        (t, m), ns = a.shape, b.shape[1] // 4
        tm, tk = _div_tile(m, 1024), _div_tile(t, 2048)
        grid, dims, acc = (m // tm, 4, t // tk), (((0,), (0,)), ((), ())), (tm, ns)
        a_spec = pl.BlockSpec((tk, tm), lambda i, s, l: (l, i))
        b_spec = pl.BlockSpec((tk, ns), lambda i, s, l: (l, s))
        o_spec = pl.BlockSpec((None, tm, ns), lambda i, s, l: (s, i, 0))
        out = jax.ShapeDtypeStruct((4, m, ns), out_dtype)
    return _mm_call(a, b, res, dims, grid, a_spec, b_spec, o_spec, o_spec, acc, out, name, post)


def _mm_call(a, b, res, dims, grid, a_spec, b_spec, r_spec, o_spec, acc_shape, out, name, post=None):
    nk = grid[2]
    post_fn, post_in, post_dtypes = (None, [], [out.dtype]) if post is None else post
    n_extra = (res is not None) + len(post_in)

    def body(*refs):
        a_ref, b_ref = refs[:2]
        r_ref = None if res is None else refs[2]
        p_refs = refs[2 + (res is not None):2 + n_extra]
        o_refs = refs[2 + n_extra:2 + n_extra + len(post_dtypes)]

        def finish(acc):
            if r_ref is not None:
                acc = acc + r_ref[...].astype(F32)
            outs = (acc,) if post_fn is None else post_fn(acc, *[p[...] for p in p_refs])
            for o_ref, o in zip(o_refs, outs):
                o_ref[...] = o.astype(o_ref.dtype)

        prod = lax.dot_general(a_ref[...].astype(BF16), b_ref[...].astype(BF16), dims, preferred_element_type=F32)
        if nk == 1:
            finish(prod)
            return
        acc_ref = refs[-1]
        kk = pl.program_id(2)

        @pl.when(kk == 0)
        def _():
            acc_ref[...] = prod

        @pl.when(kk > 0)
        def _():
            acc_ref[...] += prod

        @pl.when(kk == nk - 1)
        def _():
            finish(acc_ref[...])

    in_specs = [a_spec, b_spec] + ([r_spec] if res is not None else []) + [o_spec] * len(post_in)
    args = (a, b) + ((res,) if res is not None else ()) + tuple(post_in)
    outs = _pcall(
        body, name=name, grid=grid, in_specs=in_specs, out_specs=[o_spec] * len(post_dtypes),
        out_shape=[jax.ShapeDtypeStruct(out.shape, dt) for dt in post_dtypes],
        scratch_shapes=[] if nk == 1 else [pltpu.VMEM(acc_shape, F32)],
        compiler_params=_cparams(("parallel", "parallel", "arbitrary")),
    )(*args)
    return outs[0] if post is None else outs


class Tiled:
    def __init__(self, arr, width=None, col=0):
        self.arr, self.width, self.col = arr, (arr.shape[1] if width is None else width), col

    def spec(self, tm):
        col = self.col
        return pl.BlockSpec((tm, self.width), lambda i: (i, col))


def _full_spec(p):
    nd = p.ndim
    return pl.BlockSpec(p.shape, lambda i: (0,) * nd)


def row_fwd(name, fn, tiled, params, consts, outs, tm):
    t = tiled[0].arr.shape[0]
    n_in = len(tiled) + len(params) + len(consts)

    def body(*refs):
        res = fn(*[r[...] for r in refs[:n_in]])
        for o_ref, r in zip(refs[n_in:], res):
            o_ref[...] = r.astype(o_ref.dtype)

    return _pcall(
        body, name=name, grid=(t // tm,),
        in_specs=[x.spec(tm) for x in tiled] + [_full_spec(p) for p in params + consts],
        out_specs=[pl.BlockSpec((tm, w), lambda i: (i, 0)) for w, _ in outs],
        out_shape=[jax.ShapeDtypeStruct((t, w), dt) for w, dt in outs],
        compiler_params=_cparams(("parallel",)),
    )(*[x.arr for x in tiled], *params, *consts)


def row_bwd(name, fn, tiled, params, consts, cts, tm, tiled_out_dtypes=None, bf16_copy_of=()):
    t = tiled[0].arr.shape[0]
    nt, npar, ncon = len(tiled), len(params), len(consts)
    cts = [c if isinstance(c, tuple) else (c,) for c in cts]
    flat_cts = [x for c in cts for x in c]
    tiled_out_dtypes = tiled_out_dtypes or [F32] * nt
    copies = [tiled[j] for j in bf16_copy_of]

    def body(*refs):
        n_in = nt + npar + ncon
        ins = [r[...].astype(F32) for r in refs[:nt + npar]]
        con = [r[...] for r in refs[nt + npar:n_in]]
        ct_refs = list(refs[n_in:n_in + len(flat_cts)])
        o = refs[n_in + len(flat_cts):]
        ct = []
        for c in cts:
            parts = [ct_refs.pop(0)[...].astype(F32) for _ in c]
            ct.append(functools.reduce(lambda p, q: p + q, parts))
        _, vjp = jax.vjp(lambda *a: fn(*a, *con), *ins)
        g = vjp(tuple(ct))
        for j in range(nt):
            o[j][...] = g[j].astype(o[j].dtype)
        for n, j in enumerate(bf16_copy_of):
            o[nt + npar + n][...] = g[j].astype(BF16)
        first = pl.program_id(0) == 0

        @pl.when(first)
        def _():
            for j in range(npar):
                o[nt + j][...] = g[nt + j]

        @pl.when(jnp.logical_not(first))
        def _():
            for j in range(npar):
                o[nt + j][...] += g[nt + j]

    return _pcall(
        body, name=name, grid=(t // tm,),
        in_specs=[x.spec(tm) for x in tiled] + [_full_spec(p) for p in params + consts]
        + [x.spec(tm) for x in flat_cts],
        out_specs=[pl.BlockSpec((tm, x.width), lambda i: (i, 0)) for x in tiled] + [_full_spec(p) for p in params]
        + [pl.BlockSpec((tm, x.width), lambda i: (i, 0)) for x in copies],
        out_shape=[jax.ShapeDtypeStruct((t, x.width), dt) for x, dt in zip(tiled, tiled_out_dtypes)]
        + [jax.ShapeDtypeStruct(p.shape, F32) for p in params]
        + [jax.ShapeDtypeStruct((t, x.width), BF16) for x in copies],
        compiler_params=_cparams(("arbitrary",)),
    )(*[x.arr for x in tiled], *params, *consts, *[x.arr for x in flat_cts])


def _split_dot(x, w):
    hi = x.astype(BF16)
    lo = (x - hi.astype(F32)).astype(BF16)
    return jnp.dot(hi, w, preferred_element_type=F32) + jnp.dot(lo, w, preferred_element_type=F32)


def _headsum_blocks(x, j2k):
    out = []
    for c0 in range(0, x.shape[1], LANES):
        blk = x[:, c0:c0 + LANES]
        hi = blk.astype(BF16)
        lo = (blk - hi.astype(F32)).astype(BF16)
        out.append(jnp.dot(jnp.concatenate([hi, lo], axis=1), j2k, preferred_element_type=F32))
    return jnp.concatenate(out, axis=1)


@jax.custom_vjp
def headsum(x, j2k):
    return _headsum_blocks(x, j2k)


def _headsum_fwd(x, j2k):
    return _headsum_blocks(x, j2k), j2k


def _headsum_bwd(j2k, ct):
    return _headsum_blocks(ct, j2k), jnp.zeros_like(j2k)


headsum.defvjp(_headsum_fwd, _headsum_bwd)


def _bdot(x, w):
    return jnp.dot(x.astype(BF16), w.astype(BF16), preferred_element_type=F32)


def _sigmoid(x):
    return jax.nn.sigmoid(x)


def _softplus(x):
    return jnp.maximum(x, 0.0) + jnp.log(1.0 + jnp.exp(-jnp.abs(x)))


def rms_fn(x, g):
    return (x * lax.rsqrt(jnp.mean(x * x, axis=-1, keepdims=True) + RMS_EPS) * g,)


def rms_res_fn(x, g):
    return rms_fn(x, g)[0], x


def pre_fn(rkv, lora, w0, w2p, a0, a2p, g2p, k_k, k_a, jm):
    r, k, v = rkv[:, :C], rkv[:, C:2 * C], rkv[:, 2 * C:]
    zwa, zg = lora[:, :LANES], lora[:, LANES:LANES + 256]
    w_log = -_softplus(-(w0 + _bdot(jnp.tanh(zwa), w2p))) - 0.5
    decay = jnp.exp(-jnp.exp(w_log))
    a = _sigmoid(a0 + _bdot(zwa, a2p))
    g = _bdot(_sigmoid(zg), g2p)
    kk = k * k_k
    kk = kk / jnp.maximum(jnp.sqrt(headsum(kk * kk, jm)), L2_EPS)
    k2 = k * (1.0 + (a - 1.0) * k_a)
    return r, decay, k2, v, -kk, kk * a, g


def post_fn(y, r, k2, v, g, gn_w, gn_b, r_k, jm):
    mu = headsum(y, jm) * (1.0 / HEAD)
    yc = y - mu
    var = headsum(yc * yc, jm) * (1.0 / HEAD)
    yn = yc * lax.rsqrt(var + GN_EPS) * gn_w + gn_b
    bonus = headsum(r * k2 * r_k, jm) * v
    return ((yn + bonus) * g,)


def _gelu(x):
    return 0.5 * x * (1.0 + lax.erf(x * np.float32(1.0 / np.sqrt(2.0))))


@jax.custom_vjp
def expand_groups(b_t, e, e_t):
    return _split_dot(b_t, e)


def _expand_groups_fwd(b_t, e, e_t):
    return _split_dot(b_t, e), (e, e_t)


def _expand_groups_bwd(res, ct):
    e, e_t = res
    return _split_dot(ct, e_t), jnp.zeros_like(e), jnp.zeros_like(e_t)


expand_groups.defvjp(_expand_groups_fwd, _expand_groups_bwd)


def sgu_fn(pu, pv, ln_g, ln_b, w_s, b_t, out_g, e, e_t):
    rows = pu.shape[0]
    b_exp = expand_groups(b_t, e, e_t)
    u, v = _gelu(pu), _gelu(pv)
    mu = jnp.mean(v, axis=-1, keepdims=True)
    vc = v - mu
    var = jnp.mean(vc * vc, axis=-1, keepdims=True)
    v = vc * lax.rsqrt(var + LN_EPS) * ln_g + ln_b
    tri = lax.broadcasted_iota(jnp.int32, (CHUNK, CHUNK), 0) >= lax.broadcasted_iota(jnp.int32, (CHUNK, CHUNK), 1)
    left = lax.broadcasted_iota(jnp.int32, (CHUNK, LANES), 1) < HEAD
    chunks = []
    for c0 in range(0, rows, CHUNK):
        cols = []
        for gp in range(HEADS // 2):
            vp = v[c0:c0 + CHUNK, gp * LANES:(gp + 1) * LANES]
            wa = jnp.where(tri, w_s[2 * gp], 0.0)
            wb = jnp.where(tri, w_s[2 * gp + 1], 0.0)
            cols.append(jnp.where(left, _bdot(wa, vp), _bdot(wb, vp)))
        chunks.append(jnp.concatenate(cols, axis=1) + b_exp)
    s = jnp.concatenate(chunks, axis=0) if len(chunks) > 1 else chunks[0]
    y = u * s
    return (y * lax.rsqrt(jnp.mean(y * y, axis=-1, keepdims=True) + RMS_EPS) * out_g,)


def swiglu_fn(gate, up):
    return (gate * _sigmoid(gate) * up,)


SCAN_CHUNK = 32
NB = C // LANES
FWD_GROUP = 2
BWD_GROUP = 8


def to_cols(a):
    t = a.shape[0]
    return a.reshape(t // 8, LANES, HEAD).transpose(0, 2, 1)


def from_cols(a):
    return a.transpose(0, 2, 1).reshape(a.shape[0] * 8, C)


def _scan_consts():
    i = np.arange(LANES)
    spread = (i[:, None] % 2) == (i[None, :] // HEAD)
    pick = (i[:, None] // HEAD) == (i[None, :] % 2)
    j2 = (i[:, None] // HEAD) == (i[None, :] // HEAD)
    sel = (i[None, :] // 2) == (np.arange(8 * NB * HEAD)[:, None] // HEAD)
    return (jnp.asarray(spread, BF16), jnp.asarray(pick, BF16), jnp.asarray(np.concatenate([j2, j2], 0), BF16),
            jnp.asarray(sel, BF16))


def _stack(blocks):
    return jnp.concatenate(blocks, axis=0) if len(blocks) > 1 else blocks[0]


def _unstack(x, n):
    return [x[i * HEAD:(i + 1) * HEAD] for i in range(n)]


def _headsums(blocks, j2k, group):
    res = []
    for g0 in range(0, len(blocks), group):
        x = _stack(blocks[g0:g0 + group])
        hi = x.astype(BF16)
        lo = (x - hi.astype(F32)).astype(BF16)
        out = jnp.dot(jnp.concatenate([hi, lo], axis=1), j2k, preferred_element_type=F32)
        res += _unstack(out, len(blocks[g0:g0 + group]))
    return res


def _headsums_out(blocks, pick):
    return _unstack(jnp.dot(_stack(blocks).astype(BF16), pick, preferred_element_type=F32), len(blocks))


def _expand8(tile, sel, spread):
    lhs = jnp.tile(tile.astype(BF16), (8 * NB, 1)) * sel
    return _unstack(jnp.dot(lhs, spread, preferred_element_type=F32), 8 * NB)


def _collapse(tile, blocks, first, lane_pair):
    for n, blk in enumerate(blocks):
        tile = jnp.where(jnp.tile(lane_pair == first + n, (HEAD // 8, 1)), blk, tile)
    return tile


def _row(tile, j, cb):
    return jnp.broadcast_to(tile[j:j + 1, cb * LANES:(cb + 1) * LANES], (HEAD, LANES))


def _scan_step(s_ref, rows, vexp, j2k, j, hist=None):
    w_t, k_t, a_t, b_t, r_t = rows
    s = [s_ref[:, cb * LANES:(cb + 1) * LANES] for cb in range(NB)]
    sab = _headsums([s[cb] * _row(a_t, j, cb) for cb in range(NB)], j2k, FWD_GROUP if hist is None else BWD_GROUP)
    out = []
    for cb in range(NB):
        sl = slice(cb * LANES, (cb + 1) * LANES)
        s_new = s[cb] * _row(w_t, j, cb) + sab[cb] * _row(b_t, j, cb) + vexp[j * NB + cb] * _row(k_t, j, cb)
        s_ref[:, sl] = s_new
        if hist is None:
            out.append(s_new * _row(r_t, j, cb))
        else:
            s_hist, sab_hist, idx = hist
            s_hist[idx + 1, :, sl] = s_new
            sab_hist[idx, :, sl] = sab[cb]
    return out


def wkv_fwd(r, w, k, v3, a, b):
    t = r.shape[0]
    nc = t // SCAN_CHUNK
    n8 = SCAN_CHUNK // 8

    def body(r_ref, w_ref, k_ref, a_ref, b_ref, v3_ref, spread_ref, pick_ref, j2k_ref, sel_ref, y3_ref, ck_ref,
             s_ref):
        @pl.when(pl.program_id(0) == 0)
        def _():
            s_ref[...] = jnp.zeros_like(s_ref)

        ck_ref[0] = s_ref[...]
        spread, pick, j2k, sel = spread_ref[...], pick_ref[...], j2k_ref[...], sel_ref[...]
        lane_pair = lax.broadcasted_iota(jnp.int32, (8, LANES), 1) // 2

        def t8_body(t8, carry):
            row0 = pl.multiple_of(t8 * 8, 8)
            rows = [ref[pl.ds(row0, 8), :] for ref in (w_ref, k_ref, a_ref, b_ref, r_ref)]
            vexp = _expand8(v3_ref[t8], sel, spread)
            y3 = jnp.zeros((HEAD, LANES), F32)
            for j in range(8):
                y3 = _collapse(y3, _headsums_out(_scan_step(s_ref, rows, vexp, j2k, j), pick), j * NB, lane_pair)
            y3_ref[t8] = y3
            return carry

        lax.fori_loop(0, n8, t8_body, 0)

    row_spec = pl.BlockSpec((SCAN_CHUNK, C), lambda i: (i, 0))
    col_spec = pl.BlockSpec((n8, HEAD, LANES), lambda i: (i, 0, 0))
    consts = _scan_consts()
    return _pcall(
        body, name="wkv_fwd", grid=(nc,),
        in_specs=[row_spec] * 5 + [col_spec] + [pl.BlockSpec(c.shape, lambda i: (0, 0)) for c in consts],
        out_specs=[col_spec, pl.BlockSpec((1, HEAD, C), lambda i: (i, 0, 0))],
        out_shape=[jax.ShapeDtypeStruct((t // 8, HEAD, LANES), F32), jax.ShapeDtypeStruct((nc, HEAD, C), F32)],
        scratch_shapes=[pltpu.VMEM((HEAD, C), F32)],
        compiler_params=_cparams(("arbitrary",)),
    )(r, w, k, a, b, v3, *consts)


def wkv_bwd(r, w, k, v3, a, b, ckpt, dy3):
    t = r.shape[0]
    nc = t // SCAN_CHUNK
    n8 = SCAN_CHUNK // 8

    def body(r_ref, w_ref, k_ref, a_ref, b_ref, v3_ref, dy3_ref, ck_ref, spread_ref, pick_ref, j2k_ref, sel_ref,
             dr_ref, dw_ref, dk_ref, da_ref, db_ref, dv3_ref, s_ref, g_ref, s_hist, sab_hist):
        @pl.when(pl.program_id(0) == 0)
        def _():
            g_ref[...] = jnp.zeros_like(g_ref)

        spread, pick, j2k, sel = spread_ref[...], pick_ref[...], j2k_ref[...], sel_ref[...]
        lane_pair = lax.broadcasted_iota(jnp.int32, (8, LANES), 1) // 2
        sub = lax.broadcasted_iota(jnp.int32, (8, LANES), 0)
        s_ref[...] = ck_ref[0]
        s_hist[0] = ck_ref[0]

        def redo(t8, carry):
            row0 = pl.multiple_of(t8 * 8, 8)
            rows = [ref[pl.ds(row0, 8), :] for ref in (w_ref, k_ref, a_ref, b_ref, r_ref)]
            vexp = _expand8(v3_ref[t8], sel, spread)
            for j in range(8):
                _scan_step(s_ref, rows, vexp, j2k, j, hist=(s_hist, sab_hist, t8 * 8 + j))
            return carry

        lax.fori_loop(0, n8, redo, 0)

        def back(q, carry):
            t8 = n8 - 1 - q
            row0 = pl.multiple_of(t8 * 8, 8)
            w_t, k_t, a_t, b_t, r_t = [ref[pl.ds(row0, 8), :] for ref in (w_ref, k_ref, a_ref, b_ref, r_ref)]
            vexp = _expand8(v3_ref[t8], sel, spread)
            dyexp = _expand8(dy3_ref[t8], sel, spread)
            dv3 = jnp.zeros((HEAD, LANES), F32)
            tiles = [[jnp.zeros((8, LANES), F32) for _ in range(NB)] for _ in range(5)]
            for j in range(7, -1, -1):
                idx = t8 * 8 + j
                g = [g_ref[:, cb * LANES:(cb + 1) * LANES] + dyexp[j * NB + cb] * _row(r_t, j, cb)
                     for cb in range(NB)]
                dsab = _headsums([g[cb] * _row(b_t, j, cb) for cb in range(NB)], j2k, BWD_GROUP)
                dv3 = _collapse(dv3, _headsums_out([g[cb] * _row(k_t, j, cb) for cb in range(NB)], pick), j * NB,
                                lane_pair)
                for cb in range(NB):
                    sl = slice(cb * LANES, (cb + 1) * LANES)
                    s_new = s_hist[idx + 1, :, sl]
                    s_old = s_hist[idx, :, sl]
                    sab = sab_hist[idx, :, sl]
                    sums = (s_new * dyexp[j * NB + cb], g[cb] * s_old, g[cb] * vexp[j * NB + cb],
                            s_old * dsab[cb], g[cb] * sab)
                    for n, prod in enumerate(sums):
                        rowsum = jnp.broadcast_to(jnp.sum(prod, axis=0, keepdims=True), (8, LANES))
                        tiles[n][cb] = jnp.where(sub == j, rowsum, tiles[n][cb])
                    g_ref[:, sl] = g[cb] * _row(w_t, j, cb) + dsab[cb] * _row(a_t, j, cb)
            dv3_ref[t8] = dv3
            for n, ref in enumerate((dr_ref, dw_ref, dk_ref, da_ref, db_ref)):
                for cb in range(NB):
                    ref[pl.ds(row0, 8), cb * LANES:(cb + 1) * LANES] = tiles[n][cb]
            return carry

        lax.fori_loop(0, n8, back, 0)

    row_spec = pl.BlockSpec((SCAN_CHUNK, C), lambda i: (nc - 1 - i, 0))
    col_spec = pl.BlockSpec((n8, HEAD, LANES), lambda i: (nc - 1 - i, 0, 0))
    consts = _scan_consts()
    return _pcall(
        body, name="wkv_bwd", grid=(nc,),
        in_specs=[row_spec] * 5 + [col_spec, col_spec, pl.BlockSpec((1, HEAD, C), lambda i: (nc - 1 - i, 0, 0))]
        + [pl.BlockSpec(c.shape, lambda i: (0, 0)) for c in consts],
        out_specs=[row_spec] * 5 + [col_spec],
        out_shape=[jax.ShapeDtypeStruct((t, C), F32)] * 5 + [jax.ShapeDtypeStruct((t // 8, HEAD, LANES), F32)],
        scratch_shapes=[pltpu.VMEM((HEAD, C), F32), pltpu.VMEM((HEAD, C), F32),
                        pltpu.VMEM((SCAN_CHUNK + 1, HEAD, C), F32), pltpu.VMEM((SCAN_CHUNK, HEAD, C), F32)],
        compiler_params=_cparams(("arbitrary",)),
    )(r, w, k, a, b, v3, dy3, ckpt, *consts)


def _prev_rows(cur, before, first_tile):
    last = jnp.where(first_tile, 0.0, before[7:8, :])
    row = lax.broadcasted_iota(jnp.int32, cur.shape, 0)
    return jnp.where(row == 0, last, pltpu.roll(cur, 1, 0))


def shift_fwd(p, mu, col, name):
    t, width = p.shape[0], mu.shape[1]
    tm = _tile(t, 256)

    def body(p_ref, before_ref, mu_ref, o_ref):
        cur = p_ref[...]
        prev = _prev_rows(cur, before_ref[...], pl.program_id(0) == 0)
        o_ref[...] = cur + (prev - cur) * mu_ref[...]

    return _pcall(
        body, name=name, grid=(t // tm,),
        in_specs=[pl.BlockSpec((tm, width), lambda i: (i, col)),
                  pl.BlockSpec((8, width), lambda i: (jnp.maximum(i * (tm // 8) - 1, 0), col)),
                  pl.BlockSpec((1, width), lambda i: (0, 0))],
        out_specs=pl.BlockSpec((tm, width), lambda i: (i, 0)),
        out_shape=jax.ShapeDtypeStruct((t, width), F32),
        compiler_params=_cparams(("parallel",)),
    )(p, p, mu)


def shift_bwd(dps, p, mu, col, name):
    t, width = dps.shape
    tm = _tile(t, 256)
    nt = t // tm

    def body(d_ref, after_ref, p_ref, before_ref, mu_ref, dp_ref, dmu_ref):
        i = pl.program_id(0)
        d, cur, mu_v = d_ref[...], p_ref[...], mu_ref[...]
        row = lax.broadcasted_iota(jnp.int32, d.shape, 0)
        first_after = jnp.where(i == nt - 1, 0.0, after_ref[0:1, :])
        nxt = jnp.where(row == tm - 1, first_after, pltpu.roll(d, tm - 1, 0))
        dp_ref[...] = (d * (1.0 - mu_v) + nxt * mu_v).astype(BF16)
        prev = _prev_rows(cur, before_ref[...], i == 0)
        part = jnp.sum(d * (prev - cur), axis=0, keepdims=True)

        @pl.when(i == 0)
        def _():
            dmu_ref[...] = part

        @pl.when(i > 0)
        def _():
            dmu_ref[...] += part

    return _pcall(
        body, name=name, grid=(nt,),
        in_specs=[pl.BlockSpec((tm, width), lambda i: (i, 0)),
                  pl.BlockSpec((8, width), lambda i: (jnp.minimum((i + 1) * (tm // 8), t // 8 - 1), 0)),
                  pl.BlockSpec((tm, width), lambda i: (i, col)),
                  pl.BlockSpec((8, width), lambda i: (jnp.maximum(i * (tm // 8) - 1, 0), col)),
                  pl.BlockSpec((1, width), lambda i: (0, 0))],
        out_specs=[pl.BlockSpec((tm, width), lambda i: (i, 0)), pl.BlockSpec((1, width), lambda i: (0, 0))],
        out_shape=[jax.ShapeDtypeStruct((t, width), BF16), jax.ShapeDtypeStruct((1, width), F32)],
        compiler_params=_cparams(("arbitrary",)),
    )(dps, dps, p, p, mu)


def loss_head(x3, tgt, g):
    t, d = x3.shape
    tm = _tile(t, 256)

    def body(x_ref, t_ref, g_ref, loss_ref, dx_ref, dg_ref, dxb_ref):
        (y,), vjp = jax.vjp(lambda a, b: rms_fn(a, b), x_ref[...], g_ref[...])
        diff = y - t_ref[...]
        part = 0.5 * jnp.sum(jnp.mean(diff * diff, axis=-1, keepdims=True), axis=0, keepdims=True)
        dx, dg = vjp((diff * (1.0 / d),))
        dx_ref[...] = dx
        dxb_ref[...] = dx.astype(BF16)
        part = jnp.broadcast_to(part, (1, LANES))
        first = pl.program_id(0) == 0

        @pl.when(first)
        def _():
            loss_ref[...] = part
            dg_ref[...] = dg

        @pl.when(jnp.logical_not(first))
        def _():
            loss_ref[...] += part
            dg_ref[...] += dg

    row = pl.BlockSpec((tm, d), lambda i: (i, 0))
    vec = pl.BlockSpec((1, d), lambda i: (0, 0))
    return _pcall(
        body, name="loss_head", grid=(t // tm,), in_specs=[row, row, vec],
        out_specs=[pl.BlockSpec((1, LANES), lambda i: (0, 0)), row, vec, row],
        out_shape=[jax.ShapeDtypeStruct((1, LANES), F32), jax.ShapeDtypeStruct((t, d), F32),
                   jax.ShapeDtypeStruct((1, d), F32), jax.ShapeDtypeStruct((t, d), BF16)],
        compiler_params=_cparams(("arbitrary",)),
    )(x3, tgt, g)


ADAM_LR, ADAM_B1, ADAM_B2, ADAM_EPS, ADAM_WD, ADAM_STEP = 0.001, 0.9, 0.999, 1e-08, 0.01, 10


def adamw(w, g, m, v, name):
    rows, width = w.shape
    tr = _tile(rows, 256)

    def body(w_ref, g_ref, m_ref, v_ref, d_ref, nm_ref, nv_ref):
        gv = g_ref[...]
        m_new = ADAM_B1 * m_ref[...] + (1.0 - ADAM_B1) * gv
        v_new = ADAM_B2 * v_ref[...] + (1.0 - ADAM_B2) * (gv * gv)
        m_hat = m_new / (1.0 - ADAM_B1 ** ADAM_STEP)
        v_hat = v_new / (1.0 - ADAM_B2 ** ADAM_STEP)
        d_ref[...] = -ADAM_LR * (m_hat / (jnp.sqrt(v_hat) + ADAM_EPS) + ADAM_WD * w_ref[...])
        nm_ref[...] = m_new
        nv_ref[...] = v_new

    spec = pl.BlockSpec((tr, width), lambda i: (i, 0))
    return _pcall(
        body, name=name, grid=(rows // tr,), in_specs=[spec] * 4, out_specs=[spec] * 3,
        out_shape=[jax.ShapeDtypeStruct((rows, width), F32)] * 3,
        compiler_params=_cparams(("parallel",)),
    )(w, g, m, v)


def sum_slots(q, name, for_swap=False):
    n, rows, width = q.shape
    tr = _tile(rows, 512)

    def body(*refs):
        refs = refs[for_swap:]
        acc = refs[0][...].astype(F32)
        for r in refs[1:n]:
            acc = acc + r[...].astype(F32)
        refs[n][...] = acc

    if not for_swap:
        specs = [pl.BlockSpec((None, tr, width), functools.partial(lambda s, i: (s, i, 0), s)) for s in range(n)]
        return _pcall(body, name=name, grid=(rows // tr,), in_specs=specs,
                      out_specs=pl.BlockSpec((tr, width), lambda i: (i, 0)),
                      out_shape=jax.ShapeDtypeStruct((rows, width), F32),
                      compiler_params=_cparams(("parallel",)))(*([q] * n))
    specs = [pl.BlockSpec((None, tr, width), functools.partial(lambda s, i, c_ref: (s, i, 0), s)) for s in range(n)]
    grid_spec = pltpu.PrefetchScalarGridSpec(
        num_scalar_prefetch=1, grid=(rows // tr,), in_specs=specs,
        out_specs=pl.BlockSpec((None, tr, width), lambda i, c_ref: (c_ref[0], i, 0)))
    return _pcall(body, name=name, grid_spec=grid_spec, out_shape=jax.ShapeDtypeStruct((2, rows, width), F32),
                  compiler_params=_cparams(("parallel",)))(
        lax.axis_index("c").astype(jnp.int32).reshape(1), *([q] * n))


MESH = pl.DeviceIdType.MESH
ANY_SPEC = pl.BlockSpec(memory_space=pl.ANY)


def _coords():
    return lax.axis_index("x"), lax.axis_index("y"), lax.axis_index("c")


def _other_chips(x, y):
    return [(1 - x, y), (x, 1 - y), (1 - x, 1 - y)]


def push(name, srcs, n_dst, plan_fn):
    n_arr = len(srcs)
    n_send = len(plan_fn(0, 0, 0))

    def body(*refs):
        src_refs, dst_refs = refs[:n_arr], refs[n_arr:2 * n_arr]
        send_sems, recv_sems = refs[2 * n_arr:]
        sends = plan_fn(*_coords())
        out = []
        for q, (src_ref, dst_ref) in enumerate(zip(src_refs, dst_refs)):
            out += [pltpu.make_async_remote_copy(src_ref.at[si], dst_ref.at[di], send_sems.at[q * n_send + k],
                                                 recv_sems.at[q * n_send + k], device_id=peer, device_id_type=MESH)
                    for k, (si, peer, di, _) in enumerate(sends)]
        for cp in out:
            cp.start()
        for q, (src_ref, dst_ref) in enumerate(zip(src_refs, dst_refs)):
            for k, (si, peer, _, ri) in enumerate(sends):
                pltpu.make_async_remote_copy(src_ref.at[si], dst_ref.at[ri], send_sems.at[q * n_send + k],
                                             recv_sems.at[q * n_send + k], device_id=peer,
                                             device_id_type=MESH).wait_recv()
        for cp in out:
            cp.wait_send()

    return _pcall(
        body, name=name, in_specs=[ANY_SPEC] * n_arr, out_specs=[ANY_SPEC] * n_arr,
        out_shape=[jax.ShapeDtypeStruct((n_dst,) + s.shape[1:], s.dtype) for s in srcs],
        scratch_shapes=[pltpu.SemaphoreType.DMA((n_arr * n_send,)), pltpu.SemaphoreType.DMA((n_arr * n_send,))],
    )(*srcs)


def plan_sibling_quarters(x, y, c):
    return [(2 * s + (1 - c), (x, y, 1 - c), s, s) for s in range(4)]


def plan_chips_by_shard(x, y, c):
    me = 2 * x + y
    return [(2 * px + py, (px, py, c), me, 2 * px + py) for px, py in _other_chips(x, y)]


def plan_chips_half(x, y, c):
    me = 2 * x + y
    return [(c, (px, py, c), me, 2 * px + py) for px, py in _other_chips(x, y)]


def swap_halves(name, bufs):
    n_arr = len(bufs)

    def body(*refs):
        buf_refs = refs[n_arr:2 * n_arr]
        send_sems, recv_sems = refs[2 * n_arr:]
        x, y, c = _coords()
        out = [pltpu.make_async_remote_copy(b.at[c], b.at[c], send_sems.at[q], recv_sems.at[q],
                                            device_id=(x, y, 1 - c), device_id_type=MESH)
               for q, b in enumerate(buf_refs)]
        for cp in out:
            cp.start()
        for q, b in enumerate(buf_refs):
            pltpu.make_async_remote_copy(b.at[1 - c], b.at[1 - c], send_sems.at[q], recv_sems.at[q],
                                         device_id=(x, y, 1 - c), device_id_type=MESH).wait_recv()
        for cp in out:
            cp.wait_send()

    return _pcall(
        body, name=name, in_specs=[ANY_SPEC] * n_arr, out_specs=[ANY_SPEC] * n_arr,
        out_shape=[jax.ShapeDtypeStruct(b.shape, b.dtype) for b in bufs],
        input_output_aliases={q: q for q in range(n_arr)},
        scratch_shapes=[pltpu.SemaphoreType.DMA((n_arr,)), pltpu.SemaphoreType.DMA((n_arr,))],
    )(*bufs)


HBM_SPEC = pl.BlockSpec(memory_space=pltpu.HBM)
SEM_SPEC = pl.BlockSpec(memory_space=pltpu.SEMAPHORE)
DATAFLOW = pltpu.SideEffectType.DATAFLOW_SIDE_EFFECTING


def _in_hbm(a):
    return pltpu.with_memory_space_constraint(a, pltpu.HBM)


def push_start(name, srcs, lands, plan_fn, after):
    n_arr = len(srcs)
    n_send = len(plan_fn(0, 0, 0))
    arrays = list(srcs) + ([] if lands is None else list(lands))
    if after is None:
        after = jnp.zeros((8, LANES), F32)
    n_all = len(arrays)

    def body(*refs):
        src_refs, land_refs = refs[:n_arr], refs[n_all - n_arr:n_all]
        send_sems, recv_sems = refs[n_all + 1:n_all + 3]
        token = refs[-1]
        sends = plan_fn(*_coords())
        for q, (src_ref, land_ref) in enumerate(zip(src_refs, land_refs)):
            for k, (si, peer, di, _) in enumerate(sends):
                pltpu.make_async_remote_copy(src_ref.at[si], land_ref.at[di], send_sems.at[q * n_send + k],
                                             recv_sems.at[q * n_send + k], device_id=peer,
                                             device_id_type=MESH).start()
        token[...] = jnp.zeros_like(token)

    sems = pltpu.SemaphoreType.DMA((n_arr * n_send,))
    out = _pcall(
        body, name=name,
        in_specs=[HBM_SPEC] * n_all + [ANY_SPEC],
        out_specs=[SEM_SPEC, SEM_SPEC] + [HBM_SPEC] * n_all + [pl.BlockSpec(memory_space=pltpu.VMEM)],
        out_shape=[sems, sems] + [pltpu.HBM(a.shape, a.dtype) for a in arrays]
        + [jax.ShapeDtypeStruct((8, LANES), F32)],
        input_output_aliases={q: 2 + q for q in range(n_all)},
        compiler_params=pltpu.CompilerParams(has_side_effects=DATAFLOW),
    )(*[_in_hbm(a) for a in arrays], after)
    return out[:-1], out[-1]


def push_wait(name, state, plan_fn, after, in_place=False, both=False):
    arrays = list(state[2:])
    n_all = len(arrays)
    n_arr = n_all if in_place else n_all // 2
    n_send = len(plan_fn(0, 0, 0))
    send_sems, recv_sems = state[:2]

    def body(*refs):
        src_refs, land_refs = refs[:n_arr], refs[n_all - n_arr:n_all]
        send_ref, recv_ref = refs[n_all:n_all + 2]
        sends = plan_fn(*_coords())
        for q, (src_ref, land_ref) in enumerate(zip(src_refs, land_refs)):
            for k, (si, peer, _, ri) in enumerate(sends):
                cp = pltpu.make_async_remote_copy(src_ref.at[si], land_ref.at[ri], send_ref.at[q * n_send + k],
                                                  recv_ref.at[q * n_send + k], device_id=peer, device_id_type=MESH)
                cp.wait_send()
                cp.wait_recv()

    out = _pcall(
        body, name=name,
        in_specs=[HBM_SPEC] * n_all + [SEM_SPEC, SEM_SPEC, ANY_SPEC],
        out_specs=[HBM_SPEC] * n_all,
        out_shape=[pltpu.HBM(a.shape, a.dtype) for a in arrays],
        input_output_aliases={q: q for q in range(n_all)},
        compiler_params=pltpu.CompilerParams(has_side_effects=DATAFLOW),
    )(*arrays, send_sems, recv_sems, after)
    return list(out) if both else list(out[n_all - n_arr:])


def plan_chips_gather(x, y, c):
    me = 2 * x + y
    return [(c, (px, py, c), 2 * me + c, 2 * (2 * px + py) + c) for px, py in _other_chips(x, y)]


def plan_forward(x, y, c):
    return [(2 * (2 * px + py) + c, (x, y, 1 - c), 2 * (2 * px + py) + c, 2 * (2 * px + py) + (1 - c))
            for px, py in _other_chips(x, y)]


def add_kept(g8, other, name):
    _, rows, cols = other.shape
    tr = _tile(rows, 512)

    def body(c_ref, a_ref, b_ref, o_ref):
        o_ref[...] = (a_ref[...].astype(F32) + b_ref[...].astype(F32)).astype(BF16)

    spec = pl.BlockSpec((None, tr, cols), lambda s, i, c_ref: (s, i, 0))
    grid_spec = pltpu.PrefetchScalarGridSpec(
        num_scalar_prefetch=1, grid=(4, rows // tr),
        in_specs=[pl.BlockSpec((None, tr, cols), lambda s, i, c_ref: (2 * s + c_ref[0], i, 0)), spec],
        out_specs=spec)
    return _pcall(body, name=name, grid_spec=grid_spec, out_shape=jax.ShapeDtypeStruct((4, rows, cols), BF16),
                  compiler_params=_cparams(("parallel", "parallel")))(
        lax.axis_index("c").astype(jnp.int32).reshape(1), g8, other)


def reduce_scatter_begin(grads, names, tag):
    g8 = [g.reshape(8, g.shape[1] // 2, g.shape[2]) for g in grads]
    return _chip_sums_start(g8, push(tag + "_sibling", g8, 4, plan_sibling_quarters), names, tag)


def _chip_sums_start(g8, from_sibling, names, tag):
    chip_sum = [add_kept(a, b, tag + "_add_" + n) for a, b, n in zip(g8, from_sibling, names)]
    return push_start(tag + "_start", chip_sum, chip_sum, plan_chips_by_shard, None)


def sibling_exchange_begin(grads, tag):
    g8 = [g.reshape(8, g.shape[1] // 2, g.shape[2]) for g in grads]
    lands = [lax.empty((4,) + g.shape[1:], g.dtype) for g in g8]
    return push_start(tag + "_sib_start", g8, lands, plan_sibling_quarters, None)


def reduce_scatter_continue(state, names, tag, after):
    n = len(names)
    arrays = push_wait(tag + "_sib_wait", state, plan_sibling_quarters, after, both=True)
    return _chip_sums_start(arrays[:n], arrays[n:], names, tag)


def reduce_scatter_end(state, names, tag, after):
    by_chip = push_wait(tag + "_wait", state, plan_chips_by_shard, after)
    both = swap_halves(tag + "_halves", [sum_slots(q, tag + "_sum_" + n, for_swap=True)
                                         for q, n in zip(by_chip, names)])
    return [b.reshape(2 * b.shape[1], b.shape[2]) for b in both]


def _landing(mine):
    return [jnp.broadcast_to(m[None], (4,) + m.shape).reshape((8,) + m.shape[1:]) for m in mine]


def gather_begin(tag, mine, after, lands=None):
    return push_start(tag + "_start", mine, _landing(mine) if lands is None else lands, plan_chips_gather, after)


def gather_forward(tag, state, after):
    lands = push_wait(tag + "_wait", state, plan_chips_gather, after)
    return push_start(tag + "_fwd_start", lands, None, plan_forward, None)


def gather_end(tag, state, after):
    bufs = push_wait(tag + "_fwd_wait", state, plan_forward, after, in_place=True)
    return [g.reshape(4, 2 * g.shape[1], g.shape[2]) for g in bufs]


PACK_QUANTUM = 8 * C


def _pack(parts, dtype):
    pieces = []
    for p in parts:
        flat = p.astype(dtype).reshape(-1)
        pieces.append(jnp.pad(flat, (0, -flat.shape[0] % PACK_QUANTUM)).reshape(-1, C))
    rows = sum(p.shape[0] for p in pieces)
    if rows % 16:
        pieces.append(jnp.zeros((8, C), dtype))
    return jnp.concatenate(pieces, axis=0)


def _unpack(packed, shapes):
    out, at = [], 0
    for s in shapes:
        n = int(np.prod(s))
        rows = -(-n // PACK_QUANTUM) * 8
        out.append(packed[at:at + rows].reshape(-1)[:n].reshape(s))
        at += rows
    return out


def all_reduce_begin(vec):
    rows, width = vec.shape
    (pair,) = swap_halves("ar_sibling", [jnp.broadcast_to(vec[None], (2, rows, width))])
    chip_sum = sum_slots(pair, "ar_sum_sibling").reshape(2, rows // 2, width)
    mine = lax.dynamic_index_in_dim(chip_sum, lax.axis_index("c"), axis=0, keepdims=True)
    return push_start("ar_start", [chip_sum], [jnp.broadcast_to(mine, (4, rows // 2, width))], plan_chips_half,
                      None)


def all_reduce_end(state, after):
    (by_chip,) = push_wait("ar_wait", state, plan_chips_half, after)
    (both,) = swap_halves("ar_halves", [sum_slots(by_chip, "ar_sum_chips", for_swap=True)])
    return both.reshape(2 * both.shape[1], both.shape[2])


def kernel(x, norm1_g, w_in, shift_mu, w0, w2, a0, a2, g2, k_k, k_a, r_k, gn_w, gn_b, sgu_ln_g, sgu_ln_b, sgu_w, sgu_b, sgu_out_g, w_out, norm2_g, w_gate, w_up, w_down, final_g, loss_target, m_norm1_g, m_w_in, m_shift_mu, m_w0, m_w2, m_a0, m_a2, m_g2, m_k_k, m_k_a, m_r_k, m_gn_w, m_gn_b, m_sgu_ln_g, m_sgu_ln_b, m_sgu_w, m_sgu_b, m_sgu_out_g, m_w_out, m_norm2_g, m_w_gate, m_w_up, m_w_down, m_final_g, v_norm1_g, v_w_in, v_shift_mu, v_w0, v_w2, v_a0, v_a2, v_g2, v_k_k, v_k_a, v_r_k, v_gn_w, v_gn_b, v_sgu_ln_g, v_sgu_ln_b, v_sgu_w, v_sgu_b, v_sgu_out_g, v_w_out, v_norm2_g, v_w_gate, v_w_up, v_w_down, v_final_g):
    args = dict(locals())
    names = ["norm1_g", "w_in", "shift_mu", "w0", "w2", "a0", "a2", "g2", "k_k", "k_a", "r_k", "gn_w", "gn_b",
             "sgu_ln_g", "sgu_ln_b", "sgu_w", "sgu_b", "sgu_out_g", "w_out", "norm2_g", "w_gate", "w_up", "w_down",
             "final_g"]
    big = ["w_in", "w_out", "w_gate", "w_up", "w_down"]
    xi, yi = lax.axis_index("x"), lax.axis_index("y")
    chip = 2 * xi + yi
    x2d, tgt = x[0], loss_target[0]
    t = x2d.shape[0]

    lora_names = ["w2", "a2", "g2"]
    first, later = ["w_in"] + lora_names, ["w_out", "w_gate", "w_up", "w_down"]

    def halves(n, token=None):
        m = args[n][0] if token is None else args[n][0] + token[0, 0]
        m = m.astype(BF16)
        return m.reshape(2, m.shape[0] // 2, m.shape[1])

    first_state, first_token = gather_begin("g0", [halves(n) for n in first], norm1_g)
    later_mine = [halves(n, first_token) for n in later]
    later_lands = _landing(later_mine)
    (h1,) = row_fwd("norm1", rms_fn, [Tiled(x2d)], [norm1_g + first_token[0, 0]], [], [(D, BF16)], _tile(t, 256))
    prepared = functools.reduce(lambda p, q: p + q, [a[0, 0, :1].astype(F32) for a in later_lands])
    first_state, first_token = gather_forward("g0", first_state, prepared + h1[0, :1].astype(F32))
    got = gather_end("g0", first_state, first_token)
    later_state, later_token = gather_begin("g1", later_mine, got[0], later_lands)
    full = {n: blk.transpose(1, 0, 2).reshape(blk.shape[1], 4 * blk.shape[2]) for n, blk in zip(first, got)}
    w_in_f = full["w_in"]
    w_in_p = jnp.concatenate([w_in_f[:, :3 * C], w_in_f[:, 3360:], w_in_f[:, 3 * C:3360],
                              jnp.zeros((D, P_W - D_IN), BF16)], axis=1)
    mu_rkv = shift_mu[:, :3 * C] + later_token[0, 0]
    mu_lora = jnp.pad(shift_mu[:, 3 * C:], ((0, 0), (0, LORA_W - 288)))
    w2p = jnp.pad(full["w2"], ((0, 64), (0, 0)))
    a2p = jnp.pad(full["a2"], ((64, 0), (0, 0)))
    g2p = jnp.pad(full["g2"], ((0, 96), (0, 0)))
    ii = np.arange(C)
    jm = _scan_consts()[2]
    e_np = (np.arange(LANES)[:, None] == (ii[None, :] // HEAD))
    e_mat, e_t = jnp.asarray(e_np, BF16), jnp.asarray(e_np.T, BF16)
    b_t = jnp.pad(sgu_b[0].T, ((0, 0), (0, LANES - HEADS)))
    r_k_flat = r_k.reshape(1, C)
    tm_row = _tile(t, 256)
    tm_sgu_b = _tile(t, 128)

    p = mm(h1, w_in_p, name="proj_in")
    ps_rkv = shift_fwd(p, mu_rkv, 0, "shift_rkv")
    ps_lora = shift_fwd(p, mu_lora, P_LORA // LORA_W, "shift_lora")
    pre_par = [w0, w2p, a0, a2p, g2p, k_k, k_a]
    r_, w_, k2, v_, an, b_, g_ = row_fwd("pre", pre_fn, [Tiled(ps_rkv), Tiled(ps_lora)], pre_par, [jm],
                                         [(C, F32)] * 7, tm_row)
    v3 = to_cols(v_)
    y3, ckpt = wkv_fwd(r_, w_, k2, v3, an, b_)
    later_state, later_token = gather_forward("g1", later_state, y3)
    y_ = from_cols(y3 + later_token[0, 0])
    post_in = [Tiled(z) for z in (y_, r_, k2, v_, g_)]
    post_par = [gn_w, gn_b, r_k_flat]
    (y_rwkv,) = row_fwd("post", post_fn, post_in, post_par, [jm], [(C, BF16)], tm_row)
    sgu_in = [Tiled(p, C, 3), Tiled(p, C, 4)]
    sgu_par = [sgu_ln_g, sgu_ln_b, sgu_w[0], b_t, sgu_out_g]
    (y_sgu,) = row_fwd("sgu", sgu_fn, sgu_in, sgu_par, [e_mat, e_t], [(C, BF16)], tm_row)
    y_cat = jnp.concatenate([y_rwkv, y_sgu], axis=1)
    for n, blk in zip(later, gather_end("g1", later_state, y_cat)):
        full[n] = blk if n in ("w_gate", "w_up") else blk.reshape(4 * blk.shape[1], blk.shape[2])
    x_mid = mm(y_cat, full["w_out"], res=x2d, name="proj_out")
    (h2,) = row_fwd("norm2", rms_fn, [Tiled(x_mid)], [norm2_g], [], [(D, BF16)], tm_row)
    gate = mm(h2, full["w_gate"], shards="b", name="ffn_gate")
    up, act = mm(h2, full["w_up"], shards="b", name="ffn_up",
                 post=(lambda up_tile, gate_tile: (up_tile, swiglu_fn(gate_tile, up_tile)[0]), [gate], [F32, BF16]))
    x_out = mm(act, full["w_down"], res=x_mid, name="ffn_down")
    loss_part, dx_out, d_final_g, dx_out_b = loss_head(x_out, tgt, final_g.reshape(1, D))

    grads = {"final_g": d_final_g.reshape(D)}
    d_gate, d_up = mm(dx_out_b, full["w_down"], tb=True, name="d_act",
                      post=(lambda d_act, g_tile, u_tile: jax.vjp(swiglu_fn, g_tile, u_tile)[1]((d_act,)), [gate, up],
                            [BF16, BF16]))
    grads["w_down"] = mm(act, dx_out_b, ta=True, out_dtype=BF16, name="d_w_down")
    grads["w_gate"] = mm(h2, d_gate, shards="out", out_dtype=BF16, name="d_w_gate")
    grads["w_up"] = mm(h2, d_up, shards="out", out_dtype=BF16, name="d_w_up")
    d_h2 = mm(d_gate, full["w_gate"], shards="bt", name="d_h2_gate")
    d_h2 = mm(d_up, full["w_up"], shards="bt", res=d_h2, name="d_h2_up")
    dx_mid, grads["norm2_g"], dx_mid_b = row_bwd("norm2_b", rms_res_fn, [Tiled(x_mid)], [norm2_g], [],
                                                 [Tiled(d_h2), Tiled(dx_out)], tm_row, bf16_copy_of=(0,))
    d_ycat = mm(dx_mid_b, full["w_out"], tb=True, name="d_ycat")
    grads["w_out"] = mm(y_cat, dx_mid_b, ta=True, out_dtype=BF16, name="d_w_out")
    rs_state, rs_token = sibling_exchange_begin([grads[n].reshape((4,) + args[n].shape[1:]) for n in later], "rs2")
    d_pu, d_pv, grads["sgu_ln_g"], grads["sgu_ln_b"], d_sgu_w, d_b_t, grads["sgu_out_g"] = row_bwd(
        "sgu_b", sgu_fn, sgu_in, sgu_par, [e_mat, e_t], [Tiled(d_ycat, C, 1)], tm_sgu_b,
        tiled_out_dtypes=[BF16, BF16])
    grads["sgu_w"] = d_sgu_w[None]
    grads["sgu_b"] = d_b_t[:, :HEADS].T[None]
    post_par_b = [gn_w + rs_token[0, 0]] + post_par[1:]
    d_y, d_r1, d_k1, d_v1, d_g, grads["gn_w"], grads["gn_b"], d_r_k = row_bwd(
        "post_b", post_fn, post_in, post_par_b, [jm], [Tiled(d_ycat, C, 0)], tm_row)
    grads["r_k"] = d_r_k.reshape(r_k.shape)
    rs_state, rs_token = reduce_scatter_continue(rs_state, later, "rs2", d_y)
    d_r2, d_w, d_k2, d_a, d_b, d_v3 = wkv_bwd(r_, w_, k2, v3, an, b_, ckpt, to_cols(d_y + rs_token[0, 0]))
    d_v2 = from_cols(d_v3)
    pre_cts = [(Tiled(d_r1), Tiled(d_r2)), Tiled(d_w), (Tiled(d_k1), Tiled(d_k2)), (Tiled(d_v1), Tiled(d_v2)),
               Tiled(d_a), Tiled(d_b), Tiled(d_g)]
    d_ps_rkv, d_ps_lora, grads["w0"], d_w2p, grads["a0"], d_a2p, d_g2p, grads["k_k"], grads["k_a"] = row_bwd(
        "pre_b", pre_fn, [Tiled(ps_rkv), Tiled(ps_lora)], pre_par, [jm], pre_cts, tm_row)
    grads["w2"], grads["a2"], grads["g2"] = d_w2p[None, :64], d_a2p[None, 64:], d_g2p[None, :160]
    d_p_rkv, d_mu_rkv = shift_bwd(d_ps_rkv, p, mu_rkv, 0, "shift_rkv_b")
    d_p_lora, d_mu_lora = shift_bwd(d_ps_lora, p, mu_lora, P_LORA // LORA_W, "shift_lora_b")
    grads["shift_mu"] = jnp.concatenate([d_mu_rkv, d_mu_lora[:, :288]], axis=1)
    d_p = jnp.concatenate([d_p_rkv, d_pu, d_pv, d_p_lora], axis=1)
    d_h1 = mm(d_p, w_in_p, tb=True, name="d_h1")
    d_w_in_p = mm(h1, d_p, ta=True, out_dtype=BF16, name="d_w_in")
    d_w_in = jnp.concatenate([d_w_in_p[:, :3 * C], d_w_in_p[:, P_LORA:P_LORA + 288], d_w_in_p[:, 3 * C:P_LORA]],
                             axis=1)
    grads["w_in"] = d_w_in.reshape(D, 4, D_IN // 4).transpose(1, 0, 2)
    dx, grads["norm1_g"] = row_bwd("norm1_b", rms_res_fn, [Tiled(x2d)], [norm1_g], [],
                                   [Tiled(d_h1), Tiled(dx_mid)], tm_row)

    small = [n for n in names if n not in big]
    small_shapes = [(1, 64, C) if n in ("w2", "a2") else (1, 160, C) if n == "g2" else args[n].shape for n in small]
    vec = _pack([grads[n] for n in small], F32)
    w_in_state, w_in_token = reduce_scatter_begin([grads["w_in"]], ["w_in"], "rs1")
    small_state, small_token = all_reduce_begin(vec)
    total = {n: g[None] for n, g in zip(later, reduce_scatter_end(rs_state, later, "rs2", w_in_token + small_token))}
    loss = lax.psum(loss_part[0, 0], ("x", "y", "c"))
    delta, new_m, new_v = {}, {}, {}

    def adam_big(n):
        s = args[n].shape
        d_, m_, v__ = adamw(args[n][0], total[n][0], args["m_" + n][0], args["v_" + n][0], "adamw_" + n)
        delta[n], new_m[n], new_v[n] = d_.reshape(s), m_.reshape(s), v__.reshape(s)

    for n in later:
        adam_big(n)
    total["w_in"] = reduce_scatter_end(w_in_state, ["w_in"], "rs1", new_v["w_down"])[0][None]
    for n, g_sum in zip(small, _unpack(all_reduce_end(small_state, new_v["w_down"]), small_shapes)):
        total[n] = lax.dynamic_slice_in_dim(g_sum, chip * 256, 256, axis=2) if n in lora_names else g_sum
    adam_big("w_in")
    flat = [_pack([src[n] for n in small], F32)
            for src in (args, total, {n: args["m_" + n] for n in small}, {n: args["v_" + n] for n in small})]
    outs = adamw(*flat, "adamw_small")
    for res, o in zip((delta, new_m, new_v), outs):
        res.update(zip(small, _unpack(o, [args[n].shape for n in small])))
    return (loss, dx[None], *[total[n] for n in names], *[delta[n] for n in names],
            *[new_m[n] for n in names], *[new_v[n] for n in names])
```

```python
import functools

import numpy as np
import jax
import jax.numpy as jnp
from jax import lax
from jax.experimental import pallas as pl
from jax.experimental.pallas import tpu as pltpu

F32 = jnp.float32
BF16 = jnp.bfloat16

D = 2048
C = 1024
HEADS = 16
HEAD = 64
LANES = 128
P_W = 5632
P_LORA = 5120
LORA_W = 512
D_IN = 5408
CUT_A = 3 * C - 2 * (D_IN // 4)
CUT_B = CUT_A + 288
CHUNK = 128
RMS_EPS = 1e-6
LN_EPS = 1e-5
GN_EPS = 64e-5
L2_EPS = 1e-12
VMEM_LIMIT = 56 * 1024 * 1024


def _pcall(body, **kw):
    return pl.pallas_call(body, **kw)


def _cparams(sem):
    return pltpu.CompilerParams(dimension_semantics=sem, vmem_limit_bytes=VMEM_LIMIT)


def _tile(n, most):
    t = most
    while t > 8 and n % t:
        t //= 2
    assert n % t == 0, (n, most)
    return t


MM_TILE = (2048, 512, 2048)
MM_VMEM = 40 * 1024 * 1024


def _div_tile(n, most, quantum=LANES):
    for t in range(min(n, most) // quantum * quantum, 0, -quantum):
        if n % t == 0:
            return t
    raise ValueError((n, most, quantum))


def _mm_tiles(m, n, k, a_bytes, b_bytes, r_bytes, o_bytes):
    tm, tn, tk = _div_tile(m, MM_TILE[0]), _div_tile(n, MM_TILE[1]), _div_tile(k, MM_TILE[2])

    def need(tm, tk):
        return 2 * (tm * tk * a_bytes + tk * tn * b_bytes + tm * tn * (r_bytes + o_bytes)) + tm * tn * 4

    while need(tm, tk) > MM_VMEM:
        if tk >= tm and tk > 512:
            tk = _div_tile(k, tk - LANES)
        else:
            tm = _div_tile(m, tm - LANES)
    return tm, tn, tk


def mm(a, b, *, ta=False, tb=False, res=None, out_dtype=F32, name, shards=None, post=None):
    if shards is not None:
        return _mm_shards(a, b, res, out_dtype, name, shards, post)
    m, k = (a.shape[1], a.shape[0]) if ta else a.shape
    n = b.shape[0] if tb else b.shape[1]
    assert (b.shape[1] if tb else b.shape[0]) == k
    if post is None:
        tm, tn, tk = _mm_tiles(m, n, k, a.dtype.itemsize, b.dtype.itemsize,
                               0 if res is None else res.dtype.itemsize, jnp.dtype(out_dtype).itemsize)
    else:
        tm, tn, tk = _div_tile(m, 1024), _div_tile(n, MM_TILE[1]), _div_tile(k, MM_TILE[2])
    nk = k // tk
    dims = (((0 if ta else 1,), (1 if tb else 0,)), ((), ()))
    a_spec = pl.BlockSpec((tk, tm), lambda i, j, l: (l, i)) if ta else pl.BlockSpec((tm, tk), lambda i, j, l: (i, l))
    b_spec = pl.BlockSpec((tn, tk), lambda i, j, l: (j, l)) if tb else pl.BlockSpec((tk, tn), lambda i, j, l: (l, j))
    o_spec = pl.BlockSpec((tm, tn), lambda i, j, l: (i, j))
    return _mm_call(a, b, res, dims, (m // tm, n // tn, nk), a_spec, b_spec, o_spec, o_spec, (tm, tn),
                    jax.ShapeDtypeStruct((m, n), out_dtype), name, post)


def _mm_shards(a, b, res, out_dtype, name, shards, post=None):
    assert post is None or shards == "b"
    if shards == "b":
        (m, k), ns = a.shape, b.shape[2]
        tm, tk = _div_tile(m, 1024 if post is None else 512), _div_tile(k, 2048)
        grid, dims, acc = (m // tm, 4, k // tk), (((1,), (0,)), ((), ())), (tm, ns)
        a_spec = pl.BlockSpec((tm, tk), lambda i, s, l: (i, l))
        b_spec = pl.BlockSpec((None, tk, ns), lambda i, s, l: (s, l, 0))
        o_spec = pl.BlockSpec((tm, ns), lambda i, s, l: (i, s))
        out = jax.ShapeDtypeStruct((m, 4 * ns), out_dtype)
    elif shards == "bt":
        m, (_, n, ns) = a.shape[0], b.shape
        tm, tn = _div_tile(m, 2048), _div_tile(n, 512)
        grid, dims, acc = (m // tm, n // tn, 4), (((1,), (1,)), ((), ())), (tm, tn)
        a_spec = pl.BlockSpec((tm, ns), lambda i, j, s: (i, s))
        b_spec = pl.BlockSpec((None, tn, ns), lambda i, j, s: (s, j, 0))
        o_spec = pl.BlockSpec((tm, tn), lambda i, j, s: (i, j))
        out = jax.ShapeDtypeStruct((m, n), out_dtype)
    else:
        (t, m), ns = a.shape, b.shape[1] // 4
        tm, tk = _div_tile(m, 1024), _div_tile(t, 2048)
        grid, dims, acc = (m // tm, 4, t // tk), (((0,), (0,)), ((), ())), (tm, ns)
        a_spec = pl.BlockSpec((tk, tm), lambda i, s, l: (l, i))
        b_spec = pl.BlockSpec((tk, ns), lambda i, s, l: (l, s))
        o_spec = pl.BlockSpec((None, tm, ns), lambda i, s, l: (s, i, 0))
        out = jax.ShapeDtypeStruct((4, m, ns), out_dtype)
    return _mm_call(a, b, res, dims, grid, a_spec, b_spec, o_spec, o_spec, acc, out, name, post)


def _mm_call(a, b, res, dims, grid, a_spec, b_spec, r_spec, o_spec, acc_shape, out, name, post=None):
    nk = grid[2]
    post_fn, post_in, post_dtypes = (None, [], [out.dtype]) if post is None else post
    n_extra = (res is not None) + len(post_in)

    def body(*refs):
        a_ref, b_ref = refs[:2]
        r_ref = None if res is None else refs[2]
        p_refs = refs[2 + (res is not None):2 + n_extra]
        o_refs = refs[2 + n_extra:2 + n_extra + len(post_dtypes)]

        def finish(acc):
            if r_ref is not None:
                acc = acc + r_ref[...].astype(F32)
            outs = (acc,) if post_fn is None else post_fn(acc, *[p[...] for p in p_refs])
            for o_ref, o in zip(o_refs, outs):
                o_ref[...] = o.astype(o_ref.dtype)

        prod = lax.dot_general(a_ref[...].astype(BF16), b_ref[...].astype(BF16), dims, preferred_element_type=F32)
        if nk == 1:
            finish(prod)
            return
        acc_ref = refs[-1]
        kk = pl.program_id(2)

        @pl.when(kk == 0)
        def _():
            acc_ref[...] = prod

        @pl.when(kk > 0)
        def _():
            acc_ref[...] += prod

        @pl.when(kk == nk - 1)
        def _():
            finish(acc_ref[...])

    in_specs = [a_spec, b_spec] + ([r_spec] if res is not None else []) + [o_spec] * len(post_in)
    args = (a, b) + ((res,) if res is not None else ()) + tuple(post_in)
    outs = _pcall(
        body, name=name, grid=grid, in_specs=in_specs, out_specs=[o_spec] * len(post_dtypes),
        out_shape=[jax.ShapeDtypeStruct(out.shape, dt) for dt in post_dtypes],
        scratch_shapes=[] if nk == 1 else [pltpu.VMEM(acc_shape, F32)],
        compiler_params=_cparams(("parallel", "parallel", "arbitrary")),
    )(*args)
    return outs[0] if post is None else outs


class Tiled:
    def __init__(self, arr, width=None, col=0):
        self.arr, self.width, self.col = arr, (arr.shape[1] if width is None else width), col

    def spec(self, tm):
        col = self.col
        return pl.BlockSpec((tm, self.width), lambda i: (i, col))


def _full_spec(p):
    nd = p.ndim
    return pl.BlockSpec(p.shape, lambda i: (0,) * nd)


def row_fwd(name, fn, tiled, params, consts, outs, tm):
    t = tiled[0].arr.shape[0]
    n_in = len(tiled) + len(params) + len(consts)

    def body(*refs):
        res = fn(*[r[...] for r in refs[:n_in]])
        for o_ref, r in zip(refs[n_in:], res):
            o_ref[...] = r.astype(o_ref.dtype)

    return _pcall(
        body, name=name, grid=(t // tm,),
        in_specs=[x.spec(tm) for x in tiled] + [_full_spec(p) for p in params + consts],
        out_specs=[pl.BlockSpec((tm, w), lambda i: (i, 0)) for w, _ in outs],
        out_shape=[jax.ShapeDtypeStruct((t, w), dt) for w, dt in outs],
        compiler_params=_cparams(("parallel",)),
    )(*[x.arr for x in tiled], *params, *consts)


def row_bwd(name, fn, tiled, params, consts, cts, tm, tiled_out_dtypes=None, bf16_copy_of=()):
    t = tiled[0].arr.shape[0]
    nt, npar, ncon = len(tiled), len(params), len(consts)
    cts = [c if isinstance(c, tuple) else (c,) for c in cts]
    flat_cts = [x for c in cts for x in c]
    tiled_out_dtypes = tiled_out_dtypes or [F32] * nt
    copies = [tiled[j] for j in bf16_copy_of]

    def body(*refs):
        n_in = nt + npar + ncon
        ins = [r[...].astype(F32) for r in refs[:nt + npar]]
        con = [r[...] for r in refs[nt + npar:n_in]]
        ct_refs = list(refs[n_in:n_in + len(flat_cts)])
        o = refs[n_in + len(flat_cts):]
        ct = []
        for c in cts:
            parts = [ct_refs.pop(0)[...].astype(F32) for _ in c]
            ct.append(functools.reduce(lambda p, q: p + q, parts))
        _, vjp = jax.vjp(lambda *a: fn(*a, *con), *ins)
        g = vjp(tuple(ct))
        for j in range(nt):
            o[j][...] = g[j].astype(o[j].dtype)
        for n, j in enumerate(bf16_copy_of):
            o[nt + npar + n][...] = g[j].astype(BF16)
        first = pl.program_id(0) == 0

        @pl.when(first)
        def _():
            for j in range(npar):
                o[nt + j][...] = g[nt + j]

        @pl.when(jnp.logical_not(first))
        def _():
            for j in range(npar):
                o[nt + j][...] += g[nt + j]

    return _pcall(
        body, name=name, grid=(t // tm,),
        in_specs=[x.spec(tm) for x in tiled] + [_full_spec(p) for p in params + consts]
        + [x.spec(tm) for x in flat_cts],
        out_specs=[pl.BlockSpec((tm, x.width), lambda i: (i, 0)) for x in tiled] + [_full_spec(p) for p in params]
        + [pl.BlockSpec((tm, x.width), lambda i: (i, 0)) for x in copies],
        out_shape=[jax.ShapeDtypeStruct((t, x.width), dt) for x, dt in zip(tiled, tiled_out_dtypes)]
        + [jax.ShapeDtypeStruct(p.shape, F32) for p in params]
        + [jax.ShapeDtypeStruct((t, x.width), BF16) for x in copies],
        compiler_params=_cparams(("arbitrary",)),
    )(*[x.arr for x in tiled], *params, *consts, *[x.arr for x in flat_cts])


def _split_dot(x, w):
    hi = x.astype(BF16)
    lo = (x - hi.astype(F32)).astype(BF16)
    return jnp.dot(hi, w, preferred_element_type=F32) + jnp.dot(lo, w, preferred_element_type=F32)


def _headsum_blocks(x, j2k):
    out = []
    for c0 in range(0, x.shape[1], LANES):
        blk = x[:, c0:c0 + LANES]
        hi = blk.astype(BF16)
        lo = (blk - hi.astype(F32)).astype(BF16)
        out.append(jnp.dot(jnp.concatenate([hi, lo], axis=1), j2k, preferred_element_type=F32))
    return jnp.concatenate(out, axis=1)


@jax.custom_vjp
def headsum(x, j2k):
    return _headsum_blocks(x, j2k)


def _headsum_fwd(x, j2k):
    return _headsum_blocks(x, j2k), j2k


def _headsum_bwd(j2k, ct):
    return _headsum_blocks(ct, j2k), jnp.zeros_like(j2k)


headsum.defvjp(_headsum_fwd, _headsum_bwd)


def _bdot(x, w):
    return jnp.dot(x.astype(BF16), w.astype(BF16), preferred_element_type=F32)


def _sigmoid(x):
    return jax.nn.sigmoid(x)


def _softplus(x):
    return jnp.maximum(x, 0.0) + jnp.log(1.0 + jnp.exp(-jnp.abs(x)))


def rms_fn(x, g):
    return (x * lax.rsqrt(jnp.mean(x * x, axis=-1, keepdims=True) + RMS_EPS) * g,)


def rms_res_fn(x, g):
    return rms_fn(x, g)[0], x


def pre_fn(rkv, lora, w0, w2p, a0, a2p, g2p, k_k, k_a, jm):
    r, k, v = rkv[:, :C], rkv[:, C:2 * C], rkv[:, 2 * C:]
    zwa, zg = lora[:, :LANES], lora[:, LANES:LANES + 256]
    w_log = -_softplus(-(w0 + _bdot(jnp.tanh(zwa), w2p))) - 0.5
    decay = jnp.exp(-jnp.exp(w_log))
    a = _sigmoid(a0 + _bdot(zwa, a2p))
    g = _bdot(_sigmoid(zg), g2p)
    kk = k * k_k
    kk = kk / jnp.maximum(jnp.sqrt(headsum(kk * kk, jm)), L2_EPS)
    k2 = k * (1.0 + (a - 1.0) * k_a)
    return r, decay, k2, v, -kk, kk * a, g


def post_fn(y, r, k2, v, g, gn_w, gn_b, r_k, jm):
    mu = headsum(y, jm) * (1.0 / HEAD)
    yc = y - mu
    var = headsum(yc * yc, jm) * (1.0 / HEAD)
    yn = yc * lax.rsqrt(var + GN_EPS) * gn_w + gn_b
    bonus = headsum(r * k2 * r_k, jm) * v
    return ((yn + bonus) * g,)


def _gelu(x):
    return 0.5 * x * (1.0 + lax.erf(x * np.float32(1.0 / np.sqrt(2.0))))


@jax.custom_vjp
def expand_groups(b_t, e, e_t):
    return _split_dot(b_t, e)


def _expand_groups_fwd(b_t, e, e_t):
    return _split_dot(b_t, e), (e, e_t)


def _expand_groups_bwd(res, ct):
    e, e_t = res
    return _split_dot(ct, e_t), jnp.zeros_like(e), jnp.zeros_like(e_t)


expand_groups.defvjp(_expand_groups_fwd, _expand_groups_bwd)


def sgu_fn(pu, pv, ln_g, ln_b, w_s, b_t, out_g, e, e_t):
    rows = pu.shape[0]
    b_exp = expand_groups(b_t, e, e_t)
    u, v = _gelu(pu), _gelu(pv)
    mu = jnp.mean(v, axis=-1, keepdims=True)
    vc = v - mu
    var = jnp.mean(vc * vc, axis=-1, keepdims=True)
    v = vc * lax.rsqrt(var + LN_EPS) * ln_g + ln_b
    tri = lax.broadcasted_iota(jnp.int32, (CHUNK, CHUNK), 0) >= lax.broadcasted_iota(jnp.int32, (CHUNK, CHUNK), 1)
    left = lax.broadcasted_iota(jnp.int32, (CHUNK, LANES), 1) < HEAD
    chunks = []
    for c0 in range(0, rows, CHUNK):
        cols = []
        for gp in range(HEADS // 2):
            vp = v[c0:c0 + CHUNK, gp * LANES:(gp + 1) * LANES]
            wa = jnp.where(tri, w_s[2 * gp], 0.0)
            wb = jnp.where(tri, w_s[2 * gp + 1], 0.0)
            cols.append(jnp.where(left, _bdot(wa, vp), _bdot(wb, vp)))
        chunks.append(jnp.concatenate(cols, axis=1) + b_exp)
    s = jnp.concatenate(chunks, axis=0) if len(chunks) > 1 else chunks[0]
    y = u * s
    return (y * lax.rsqrt(jnp.mean(y * y, axis=-1, keepdims=True) + RMS_EPS) * out_g,)


def swiglu_fn(gate, up):
    return (gate * _sigmoid(gate) * up,)


SCAN_CHUNK = 32
NB = C // LANES
FWD_GROUP = 2
BWD_GROUP = 8


def to_cols(a):
    t = a.shape[0]
    return a.reshape(t // 8, LANES, HEAD).transpose(0, 2, 1)


def from_cols(a):
    return a.transpose(0, 2, 1).reshape(a.shape[0] * 8, C)


def _scan_consts():
    i = np.arange(LANES)
    spread = (i[:, None] % 2) == (i[None, :] // HEAD)
    pick = (i[:, None] // HEAD) == (i[None, :] % 2)
    j2 = (i[:, None] // HEAD) == (i[None, :] // HEAD)
    sel = (i[None, :] // 2) == (np.arange(8 * NB * HEAD)[:, None] // HEAD)
    return (jnp.asarray(spread, BF16), jnp.asarray(pick, BF16), jnp.asarray(np.concatenate([j2, j2], 0), BF16),
            jnp.asarray(sel, BF16))


def _stack(blocks):
    return jnp.concatenate(blocks, axis=0) if len(blocks) > 1 else blocks[0]


def _unstack(x, n):
    return [x[i * HEAD:(i + 1) * HEAD] for i in range(n)]


def _headsums(blocks, j2k, group):
    res = []
    for g0 in range(0, len(blocks), group):
        x = _stack(blocks[g0:g0 + group])
        hi = x.astype(BF16)
        lo = (x - hi.astype(F32)).astype(BF16)
        out = jnp.dot(jnp.concatenate([hi, lo], axis=1), j2k, preferred_element_type=F32)
        res += _unstack(out, len(blocks[g0:g0 + group]))
    return res


def _headsums_out(blocks, pick):
    return _unstack(jnp.dot(_stack(blocks).astype(BF16), pick, preferred_element_type=F32), len(blocks))


def _expand8(tile, sel, spread):
    lhs = jnp.tile(tile.astype(BF16), (8 * NB, 1)) * sel
    return _unstack(jnp.dot(lhs, spread, preferred_element_type=F32), 8 * NB)


def _collapse(tile, blocks, first, lane_pair):
    for n, blk in enumerate(blocks):
        tile = jnp.where(jnp.tile(lane_pair == first + n, (HEAD // 8, 1)), blk, tile)
    return tile


def _row(tile, j, cb):
    return jnp.broadcast_to(tile[j:j + 1, cb * LANES:(cb + 1) * LANES], (HEAD, LANES))


def _scan_step(s_ref, rows, vexp, j2k, j, hist=None):
    w_t, k_t, a_t, b_t, r_t = rows
    s = [s_ref[:, cb * LANES:(cb + 1) * LANES] for cb in range(NB)]
    sab = _headsums([s[cb] * _row(a_t, j, cb) for cb in range(NB)], j2k, FWD_GROUP if hist is None else BWD_GROUP)
    out = []
    for cb in range(NB):
        sl = slice(cb * LANES, (cb + 1) * LANES)
        s_new = s[cb] * _row(w_t, j, cb) + sab[cb] * _row(b_t, j, cb) + vexp[j * NB + cb] * _row(k_t, j, cb)
        s_ref[:, sl] = s_new
        if hist is None:
            out.append(s_new * _row(r_t, j, cb))
        else:
            s_hist, sab_hist, idx = hist
            s_hist[idx + 1, :, sl] = s_new
            sab_hist[idx, :, sl] = sab[cb]
    return out


def wkv_fwd(r, w, k, v3, a, b):
    t = r.shape[0]
    nc = t // SCAN_CHUNK
    n8 = SCAN_CHUNK // 8

    def body(r_ref, w_ref, k_ref, a_ref, b_ref, v3_ref, spread_ref, pick_ref, j2k_ref, sel_ref, y3_ref, ck_ref,
             s_ref):
        @pl.when(pl.program_id(0) == 0)
        def _():
            s_ref[...] = jnp.zeros_like(s_ref)

        ck_ref[0] = s_ref[...]
        spread, pick, j2k, sel = spread_ref[...], pick_ref[...], j2k_ref[...], sel_ref[...]
        lane_pair = lax.broadcasted_iota(jnp.int32, (8, LANES), 1) // 2

        def t8_body(t8, carry):
            row0 = pl.multiple_of(t8 * 8, 8)
            rows = [ref[pl.ds(row0, 8), :] for ref in (w_ref, k_ref, a_ref, b_ref, r_ref)]
            vexp = _expand8(v3_ref[t8], sel, spread)
            y3 = jnp.zeros((HEAD, LANES), F32)
            for j in range(8):
                y3 = _collapse(y3, _headsums_out(_scan_step(s_ref, rows, vexp, j2k, j), pick), j * NB, lane_pair)
            y3_ref[t8] = y3
            return carry

        lax.fori_loop(0, n8, t8_body, 0)

    row_spec = pl.BlockSpec((SCAN_CHUNK, C), lambda i: (i, 0))
    col_spec = pl.BlockSpec((n8, HEAD, LANES), lambda i: (i, 0, 0))
    consts = _scan_consts()
    return _pcall(
        body, name="wkv_fwd", grid=(nc,),
        in_specs=[row_spec] * 5 + [col_spec] + [pl.BlockSpec(c.shape, lambda i: (0, 0)) for c in consts],
        out_specs=[col_spec, pl.BlockSpec((1, HEAD, C), lambda i: (i, 0, 0))],
        out_shape=[jax.ShapeDtypeStruct((t // 8, HEAD, LANES), F32), jax.ShapeDtypeStruct((nc, HEAD, C), F32)],
        scratch_shapes=[pltpu.VMEM((HEAD, C), F32)],
        compiler_params=_cparams(("arbitrary",)),
    )(r, w, k, a, b, v3, *consts)


def wkv_bwd(r, w, k, v3, a, b, ckpt, dy3):
    t = r.shape[0]
    nc = t // SCAN_CHUNK
    n8 = SCAN_CHUNK // 8

    def body(r_ref, w_ref, k_ref, a_ref, b_ref, v3_ref, dy3_ref, ck_ref, spread_ref, pick_ref, j2k_ref, sel_ref,
             dr_ref, dw_ref, dk_ref, da_ref, db_ref, dv3_ref, s_ref, g_ref, s_hist, sab_hist):
        @pl.when(pl.program_id(0) == 0)
        def _():
            g_ref[...] = jnp.zeros_like(g_ref)

        spread, pick, j2k, sel = spread_ref[...], pick_ref[...], j2k_ref[...], sel_ref[...]
        lane_pair = lax.broadcasted_iota(jnp.int32, (8, LANES), 1) // 2
        sub = lax.broadcasted_iota(jnp.int32, (8, LANES), 0)
        s_ref[...] = ck_ref[0]
        s_hist[0] = ck_ref[0]

        def redo(t8, carry):
            row0 = pl.multiple_of(t8 * 8, 8)
            rows = [ref[pl.ds(row0, 8), :] for ref in (w_ref, k_ref, a_ref, b_ref, r_ref)]
            vexp = _expand8(v3_ref[t8], sel, spread)
            for j in range(8):
                _scan_step(s_ref, rows, vexp, j2k, j, hist=(s_hist, sab_hist, t8 * 8 + j))
            return carry

        lax.fori_loop(0, n8, redo, 0)

        def back(q, carry):
            t8 = n8 - 1 - q
            row0 = pl.multiple_of(t8 * 8, 8)
            w_t, k_t, a_t, b_t, r_t = [ref[pl.ds(row0, 8), :] for ref in (w_ref, k_ref, a_ref, b_ref, r_ref)]
            vexp = _expand8(v3_ref[t8], sel, spread)
            dyexp = _expand8(dy3_ref[t8], sel, spread)
            dv3 = jnp.zeros((HEAD, LANES), F32)
            tiles = [[jnp.zeros((8, LANES), F32) for _ in range(NB)] for _ in range(5)]
            for j in range(7, -1, -1):
                idx = t8 * 8 + j
                g = [g_ref[:, cb * LANES:(cb + 1) * LANES] + dyexp[j * NB + cb] * _row(r_t, j, cb)
                     for cb in range(NB)]
                dsab = _headsums([g[cb] * _row(b_t, j, cb) for cb in range(NB)], j2k, BWD_GROUP)
                dv3 = _collapse(dv3, _headsums_out([g[cb] * _row(k_t, j, cb) for cb in range(NB)], pick), j * NB,
                                lane_pair)
                for cb in range(NB):
                    sl = slice(cb * LANES, (cb + 1) * LANES)
                    s_new = s_hist[idx + 1, :, sl]
                    s_old = s_hist[idx, :, sl]
                    sab = sab_hist[idx, :, sl]
                    sums = (s_new * dyexp[j * NB + cb], g[cb] * s_old, g[cb] * vexp[j * NB + cb],
                            s_old * dsab[cb], g[cb] * sab)
                    for n, prod in enumerate(sums):
                        rowsum = jnp.broadcast_to(jnp.sum(prod, axis=0, keepdims=True), (8, LANES))
                        tiles[n][cb] = jnp.where(sub == j, rowsum, tiles[n][cb])
                    g_ref[:, sl] = g[cb] * _row(w_t, j, cb) + dsab[cb] * _row(a_t, j, cb)
            dv3_ref[t8] = dv3
            for n, ref in enumerate((dr_ref, dw_ref, dk_ref, da_ref, db_ref)):
                for cb in range(NB):
                    ref[pl.ds(row0, 8), cb * LANES:(cb + 1) * LANES] = tiles[n][cb]
            return carry

        lax.fori_loop(0, n8, back, 0)

    row_spec = pl.BlockSpec((SCAN_CHUNK, C), lambda i: (nc - 1 - i, 0))
    col_spec = pl.BlockSpec((n8, HEAD, LANES), lambda i: (nc - 1 - i, 0, 0))
    consts = _scan_consts()
    return _pcall(
        body, name="wkv_bwd", grid=(nc,),
        in_specs=[row_spec] * 5 + [col_spec, col_spec, pl.BlockSpec((1, HEAD, C), lambda i: (nc - 1 - i, 0, 0))]
        + [pl.BlockSpec(c.shape, lambda i: (0, 0)) for c in consts],
        out_specs=[row_spec] * 5 + [col_spec],
        out_shape=[jax.ShapeDtypeStruct((t, C), F32)] * 5 + [jax.ShapeDtypeStruct((t // 8, HEAD, LANES), F32)],
        scratch_shapes=[pltpu.VMEM((HEAD, C), F32), pltpu.VMEM((HEAD, C), F32),
                        pltpu.VMEM((SCAN_CHUNK + 1, HEAD, C), F32), pltpu.VMEM((SCAN_CHUNK, HEAD, C), F32)],
        compiler_params=_cparams(("arbitrary",)),
    )(r, w, k, a, b, v3, dy3, ckpt, *consts)


def _prev_rows(cur, before, first_tile):
    last = jnp.where(first_tile, 0.0, before[7:8, :])
    row = lax.broadcasted_iota(jnp.int32, cur.shape, 0)
    return jnp.where(row == 0, last, pltpu.roll(cur, 1, 0))


def shift_fwd(p, mu, col, name):
    t, width = p.shape[0], mu.shape[1]
    tm = _tile(t, 256)

    def body(p_ref, before_ref, mu_ref, o_ref):
        cur = p_ref[...]
        prev = _prev_rows(cur, before_ref[...], pl.program_id(0) == 0)
        o_ref[...] = cur + (prev - cur) * mu_ref[...]

    return _pcall(
        body, name=name, grid=(t // tm,),
        in_specs=[pl.BlockSpec((tm, width), lambda i: (i, col)),
                  pl.BlockSpec((8, width), lambda i: (jnp.maximum(i * (tm // 8) - 1, 0), col)),
                  pl.BlockSpec((1, width), lambda i: (0, 0))],
        out_specs=pl.BlockSpec((tm, width), lambda i: (i, 0)),
        out_shape=jax.ShapeDtypeStruct((t, width), F32),
        compiler_params=_cparams(("parallel",)),
    )(p, p, mu)


def shift_bwd(dps, p, mu, col, name):
    t, width = dps.shape
    tm = _tile(t, 256)
    nt = t // tm

    def body(d_ref, after_ref, p_ref, before_ref, mu_ref, dp_ref, dmu_ref):
        i = pl.program_id(0)
        d, cur, mu_v = d_ref[...], p_ref[...], mu_ref[...]
        row = lax.broadcasted_iota(jnp.int32, d.shape, 0)
        first_after = jnp.where(i == nt - 1, 0.0, after_ref[0:1, :])
        nxt = jnp.where(row == tm - 1, first_after, pltpu.roll(d, tm - 1, 0))
        dp_ref[...] = (d * (1.0 - mu_v) + nxt * mu_v).astype(BF16)
        prev = _prev_rows(cur, before_ref[...], i == 0)
        part = jnp.sum(d * (prev - cur), axis=0, keepdims=True)

        @pl.when(i == 0)
        def _():
            dmu_ref[...] = part

        @pl.when(i > 0)
        def _():
            dmu_ref[...] += part

    return _pcall(
        body, name=name, grid=(nt,),
        in_specs=[pl.BlockSpec((tm, width), lambda i: (i, 0)),
                  pl.BlockSpec((8, width), lambda i: (jnp.minimum((i + 1) * (tm // 8), t // 8 - 1), 0)),
                  pl.BlockSpec((tm, width), lambda i: (i, col)),
                  pl.BlockSpec((8, width), lambda i: (jnp.maximum(i * (tm // 8) - 1, 0), col)),
                  pl.BlockSpec((1, width), lambda i: (0, 0))],
        out_specs=[pl.BlockSpec((tm, width), lambda i: (i, 0)), pl.BlockSpec((1, width), lambda i: (0, 0))],
        out_shape=[jax.ShapeDtypeStruct((t, width), BF16), jax.ShapeDtypeStruct((1, width), F32)],
        compiler_params=_cparams(("arbitrary",)),
    )(dps, dps, p, p, mu)


def loss_head(x3, tgt, g):
    t, d = x3.shape
    tm = _tile(t, 256)

    def body(x_ref, t_ref, g_ref, loss_ref, dx_ref, dg_ref, dxb_ref):
        (y,), vjp = jax.vjp(lambda a, b: rms_fn(a, b), x_ref[...], g_ref[...])
        diff = y - t_ref[...]
        part = 0.5 * jnp.sum(jnp.mean(diff * diff, axis=-1, keepdims=True), axis=0, keepdims=True)
        dx, dg = vjp((diff * (1.0 / d),))
        dx_ref[...] = dx
        dxb_ref[...] = dx.astype(BF16)
        part = jnp.broadcast_to(part, (1, LANES))
        first = pl.program_id(0) == 0

        @pl.when(first)
        def _():
            loss_ref[...] = part
            dg_ref[...] = dg

        @pl.when(jnp.logical_not(first))
        def _():
            loss_ref[...] += part
            dg_ref[...] += dg

    row = pl.BlockSpec((tm, d), lambda i: (i, 0))
    vec = pl.BlockSpec((1, d), lambda i: (0, 0))
    return _pcall(
        body, name="loss_head", grid=(t // tm,), in_specs=[row, row, vec],
        out_specs=[pl.BlockSpec((1, LANES), lambda i: (0, 0)), row, vec, row],
        out_shape=[jax.ShapeDtypeStruct((1, LANES), F32), jax.ShapeDtypeStruct((t, d), F32),
                   jax.ShapeDtypeStruct((1, d), F32), jax.ShapeDtypeStruct((t, d), BF16)],
        compiler_params=_cparams(("arbitrary",)),
    )(x3, tgt, g)


ADAM_LR, ADAM_B1, ADAM_B2, ADAM_EPS, ADAM_WD, ADAM_STEP = 0.001, 0.9, 0.999, 1e-08, 0.01, 10


def adamw(w, g, m, v, name):
    rows, width = w.shape
    tr = _tile(rows, 256)

    def body(w_ref, g_ref, m_ref, v_ref, d_ref, nm_ref, nv_ref):
        gv = g_ref[...]
        m_new = ADAM_B1 * m_ref[...] + (1.0 - ADAM_B1) * gv
        v_new = ADAM_B2 * v_ref[...] + (1.0 - ADAM_B2) * (gv * gv)
        m_hat = m_new / (1.0 - ADAM_B1 ** ADAM_STEP)
        v_hat = v_new / (1.0 - ADAM_B2 ** ADAM_STEP)
        d_ref[...] = -ADAM_LR * (m_hat / (jnp.sqrt(v_hat) + ADAM_EPS) + ADAM_WD * w_ref[...])
        nm_ref[...] = m_new
        nv_ref[...] = v_new

    spec = pl.BlockSpec((tr, width), lambda i: (i, 0))
    return _pcall(
        body, name=name, grid=(rows // tr,), in_specs=[spec] * 4, out_specs=[spec] * 3,
        out_shape=[jax.ShapeDtypeStruct((rows, width), F32)] * 3,
        compiler_params=_cparams(("parallel",)),
    )(w, g, m, v)


def sum_slots(q, name, for_swap=False):
    n, rows, width = q.shape
    tr = _tile(rows, 512)

    def body(*refs):
        refs = refs[for_swap:]
        acc = refs[0][...].astype(F32)
        for r in refs[1:n]:
            acc = acc + r[...].astype(F32)
        refs[n][...] = acc

    if not for_swap:
        specs = [pl.BlockSpec((None, tr, width), functools.partial(lambda s, i: (s, i, 0), s)) for s in range(n)]
        return _pcall(body, name=name, grid=(rows // tr,), in_specs=specs,
                      out_specs=pl.BlockSpec((tr, width), lambda i: (i, 0)),
                      out_shape=jax.ShapeDtypeStruct((rows, width), F32),
                      compiler_params=_cparams(("parallel",)))(*([q] * n))
    specs = [pl.BlockSpec((None, tr, width), functools.partial(lambda s, i, c_ref: (s, i, 0), s)) for s in range(n)]
    grid_spec = pltpu.PrefetchScalarGridSpec(
        num_scalar_prefetch=1, grid=(rows // tr,), in_specs=specs,
        out_specs=pl.BlockSpec((None, tr, width), lambda i, c_ref: (c_ref[0], i, 0)))
    return _pcall(body, name=name, grid_spec=grid_spec, out_shape=jax.ShapeDtypeStruct((2, rows, width), F32),
                  compiler_params=_cparams(("parallel",)))(
        lax.axis_index("c").astype(jnp.int32).reshape(1), *([q] * n))


MESH = pl.DeviceIdType.MESH
ANY_SPEC = pl.BlockSpec(memory_space=pl.ANY)


def _coords():
    return lax.axis_index("x"), lax.axis_index("y"), lax.axis_index("c")


def _other_chips(x, y):
    return [(1 - x, y), (x, 1 - y), (1 - x, 1 - y)]


def push(name, srcs, n_dst, plan_fn):
    n_arr = len(srcs)
    n_send = len(plan_fn(0, 0, 0))

    def body(*refs):
        src_refs, dst_refs = refs[:n_arr], refs[n_arr:2 * n_arr]
        send_sems, recv_sems = refs[2 * n_arr:]
        sends = plan_fn(*_coords())
        out = []
        for q, (src_ref, dst_ref) in enumerate(zip(src_refs, dst_refs)):
            out += [pltpu.make_async_remote_copy(src_ref.at[si], dst_ref.at[di], send_sems.at[q * n_send + k],
                                                 recv_sems.at[q * n_send + k], device_id=peer, device_id_type=MESH)
                    for k, (si, peer, di, _) in enumerate(sends)]
        for cp in out:
            cp.start()
        for q, (src_ref, dst_ref) in enumerate(zip(src_refs, dst_refs)):
            for k, (si, peer, _, ri) in enumerate(sends):
                pltpu.make_async_remote_copy(src_ref.at[si], dst_ref.at[ri], send_sems.at[q * n_send + k],
                                             recv_sems.at[q * n_send + k], device_id=peer,
                                             device_id_type=MESH).wait_recv()
        for cp in out:
            cp.wait_send()

    return _pcall(
        body, name=name, in_specs=[ANY_SPEC] * n_arr, out_specs=[ANY_SPEC] * n_arr,
        out_shape=[jax.ShapeDtypeStruct((n_dst,) + s.shape[1:], s.dtype) for s in srcs],
        scratch_shapes=[pltpu.SemaphoreType.DMA((n_arr * n_send,)), pltpu.SemaphoreType.DMA((n_arr * n_send,))],
    )(*srcs)


def plan_sibling_quarters(x, y, c):
    return [(2 * s + (1 - c), (x, y, 1 - c), s, s) for s in range(4)]


def plan_chips_by_shard(x, y, c):
    me = 2 * x + y
    return [(2 * px + py, (px, py, c), me, 2 * px + py) for px, py in _other_chips(x, y)]


def plan_chips_half(x, y, c):
    me = 2 * x + y
    return [(c, (px, py, c), me, 2 * px + py) for px, py in _other_chips(x, y)]


def swap_halves(name, bufs):
    n_arr = len(bufs)

    def body(*refs):
        buf_refs = refs[n_arr:2 * n_arr]
        send_sems, recv_sems = refs[2 * n_arr:]
        x, y, c = _coords()
        out = [pltpu.make_async_remote_copy(b.at[c], b.at[c], send_sems.at[q], recv_sems.at[q],
                                            device_id=(x, y, 1 - c), device_id_type=MESH)
               for q, b in enumerate(buf_refs)]
        for cp in out:
            cp.start()
        for q, b in enumerate(buf_refs):
            pltpu.make_async_remote_copy(b.at[1 - c], b.at[1 - c], send_sems.at[q], recv_sems.at[q],
                                         device_id=(x, y, 1 - c), device_id_type=MESH).wait_recv()
        for cp in out:
            cp.wait_send()

    return _pcall(
        body, name=name, in_specs=[ANY_SPEC] * n_arr, out_specs=[ANY_SPEC] * n_arr,
        out_shape=[jax.ShapeDtypeStruct(b.shape, b.dtype) for b in bufs],
        input_output_aliases={q: q for q in range(n_arr)},
        scratch_shapes=[pltpu.SemaphoreType.DMA((n_arr,)), pltpu.SemaphoreType.DMA((n_arr,))],
    )(*bufs)


HBM_SPEC = pl.BlockSpec(memory_space=pltpu.HBM)
SEM_SPEC = pl.BlockSpec(memory_space=pltpu.SEMAPHORE)
DATAFLOW = pltpu.SideEffectType.DATAFLOW_SIDE_EFFECTING


def _in_hbm(a):
    return pltpu.with_memory_space_constraint(a, pltpu.HBM)


def push_start(name, srcs, lands, plan_fn, after):
    n_arr = len(srcs)
    n_send = len(plan_fn(0, 0, 0))
    arrays = list(srcs) + ([] if lands is None else list(lands))
    if after is None:
        after = jnp.zeros((8, LANES), F32)
    n_all = len(arrays)

    def body(*refs):
        src_refs, land_refs = refs[:n_arr], refs[n_all - n_arr:n_all]
        send_sems, recv_sems = refs[n_all + 1:n_all + 3]
        token = refs[-1]
        sends = plan_fn(*_coords())
        for q, (src_ref, land_ref) in enumerate(zip(src_refs, land_refs)):
            for k, (si, peer, di, _) in enumerate(sends):
                pltpu.make_async_remote_copy(src_ref.at[si], land_ref.at[di], send_sems.at[q * n_send + k],
                                             recv_sems.at[q * n_send + k], device_id=peer,
                                             device_id_type=MESH).start()
        token[...] = jnp.zeros_like(token)

    sems = pltpu.SemaphoreType.DMA((n_arr * n_send,))
    out = _pcall(
        body, name=name,
        in_specs=[HBM_SPEC] * n_all + [ANY_SPEC],
        out_specs=[SEM_SPEC, SEM_SPEC] + [HBM_SPEC] * n_all + [pl.BlockSpec(memory_space=pltpu.VMEM)],
        out_shape=[sems, sems] + [pltpu.HBM(a.shape, a.dtype) for a in arrays]
        + [jax.ShapeDtypeStruct((8, LANES), F32)],
        input_output_aliases={q: 2 + q for q in range(n_all)},
        compiler_params=pltpu.CompilerParams(has_side_effects=DATAFLOW),
    )(*[_in_hbm(a) for a in arrays], after)
    return out[:-1], out[-1]


def push_wait(name, state, plan_fn, after, in_place=False, both=False):
    arrays = list(state[2:])
    n_all = len(arrays)
    n_arr = n_all if in_place else n_all // 2
    n_send = len(plan_fn(0, 0, 0))
    send_sems, recv_sems = state[:2]

    def body(*refs):
        src_refs, land_refs = refs[:n_arr], refs[n_all - n_arr:n_all]
        send_ref, recv_ref = refs[n_all:n_all + 2]
        sends = plan_fn(*_coords())
        for q, (src_ref, land_ref) in enumerate(zip(src_refs, land_refs)):
            for k, (si, peer, _, ri) in enumerate(sends):
                cp = pltpu.make_async_remote_copy(src_ref.at[si], land_ref.at[ri], send_ref.at[q * n_send + k],
                                                  recv_ref.at[q * n_send + k], device_id=peer, device_id_type=MESH)
                cp.wait_send()
                cp.wait_recv()

    out = _pcall(
        body, name=name,
        in_specs=[HBM_SPEC] * n_all + [SEM_SPEC, SEM_SPEC, ANY_SPEC],
        out_specs=[HBM_SPEC] * n_all,
        out_shape=[pltpu.HBM(a.shape, a.dtype) for a in arrays],
        input_output_aliases={q: q for q in range(n_all)},
        compiler_params=pltpu.CompilerParams(has_side_effects=DATAFLOW),
    )(*arrays, send_sems, recv_sems, after)
    return list(out) if both else list(out[n_all - n_arr:])


def plan_chips_gather(x, y, c):
    me = 2 * x + y
    return [(c, (px, py, c), 2 * me + c, 2 * (2 * px + py) + c) for px, py in _other_chips(x, y)]


def plan_forward(x, y, c):
    return [(2 * (2 * px + py) + c, (x, y, 1 - c), 2 * (2 * px + py) + c, 2 * (2 * px + py) + (1 - c))
            for px, py in _other_chips(x, y)]


def add_kept(g8, other, name):
    _, rows, cols = other.shape
    tr = _tile(rows, 512)

    def body(c_ref, a_ref, b_ref, o_ref):
        o_ref[...] = (a_ref[...].astype(F32) + b_ref[...].astype(F32)).astype(BF16)

    spec = pl.BlockSpec((None, tr, cols), lambda s, i, c_ref: (s, i, 0))
    grid_spec = pltpu.PrefetchScalarGridSpec(
        num_scalar_prefetch=1, grid=(4, rows // tr),
        in_specs=[pl.BlockSpec((None, tr, cols), lambda s, i, c_ref: (2 * s + c_ref[0], i, 0)), spec],
        out_specs=spec)
    return _pcall(body, name=name, grid_spec=grid_spec, out_shape=jax.ShapeDtypeStruct((4, rows, cols), BF16),
                  compiler_params=_cparams(("parallel", "parallel")))(
        lax.axis_index("c").astype(jnp.int32).reshape(1), g8, other)


def reduce_scatter_begin(grads, names, tag):
    g8 = [g.reshape(8, g.shape[1] // 2, g.shape[2]) for g in grads]
    return _chip_sums_start(g8, push(tag + "_sibling", g8, 4, plan_sibling_quarters), names, tag)


def _chip_sums_start(g8, from_sibling, names, tag):
    chip_sum = [add_kept(a, b, tag + "_add_" + n) for a, b, n in zip(g8, from_sibling, names)]
    return push_start(tag + "_start", chip_sum, chip_sum, plan_chips_by_shard, None)


def sibling_exchange_begin(grads, tag):
    g8 = [g.reshape(8, g.shape[1] // 2, g.shape[2]) for g in grads]
    lands = [lax.empty((4,) + g.shape[1:], g.dtype) for g in g8]
    return push_start(tag + "_sib_start", g8, lands, plan_sibling_quarters, None)


def reduce_scatter_continue(state, names, tag, after):
    n = len(names)
    arrays = push_wait(tag + "_sib_wait", state, plan_sibling_quarters, after, both=True)
    return _chip_sums_start(arrays[:n], arrays[n:], names, tag)


def reduce_scatter_end(state, names, tag, after):
    by_chip = push_wait(tag + "_wait", state, plan_chips_by_shard, after)
    both = swap_halves(tag + "_halves", [sum_slots(q, tag + "_sum_" + n, for_swap=True)
                                         for q, n in zip(by_chip, names)])
    return [b.reshape(2 * b.shape[1], b.shape[2]) for b in both]


def _landing(mine):
    return [jnp.broadcast_to(m[None], (4,) + m.shape).reshape((8,) + m.shape[1:]) for m in mine]


def gather_begin(tag, mine, after, lands=None):
    return push_start(tag + "_start", mine, _landing(mine) if lands is None else lands, plan_chips_gather, after)


def gather_forward(tag, state, after):
    lands = push_wait(tag + "_wait", state, plan_chips_gather, after)
    return push_start(tag + "_fwd_start", lands, None, plan_forward, None)


def gather_end(tag, state, after):
    bufs = push_wait(tag + "_fwd_wait", state, plan_forward, after, in_place=True)
    return [g.reshape(4, 2 * g.shape[1], g.shape[2]) for g in bufs]


PACK_QUANTUM = 8 * C


def _pack(parts, dtype):
    pieces = []
    for p in parts:
        flat = p.astype(dtype).reshape(-1)
        pieces.append(jnp.pad(flat, (0, -flat.shape[0] % PACK_QUANTUM)).reshape(-1, C))
    rows = sum(p.shape[0] for p in pieces)
    if rows % 16:
        pieces.append(jnp.zeros((8, C), dtype))
    return jnp.concatenate(pieces, axis=0)


def _unpack(packed, shapes):
    out, at = [], 0
    for s in shapes:
        n = int(np.prod(s))
        rows = -(-n // PACK_QUANTUM) * 8
        out.append(packed[at:at + rows].reshape(-1)[:n].reshape(s))
        at += rows
    return out


def all_reduce_begin(vec):
    rows, width = vec.shape
    (pair,) = swap_halves("ar_sibling", [jnp.broadcast_to(vec[None], (2, rows, width))])
    chip_sum = sum_slots(pair, "ar_sum_sibling").reshape(2, rows // 2, width)
    mine = lax.dynamic_index_in_dim(chip_sum, lax.axis_index("c"), axis=0, keepdims=True)
    return push_start("ar_start", [chip_sum], [jnp.broadcast_to(mine, (4, rows // 2, width))], plan_chips_half,
                      None)


def all_reduce_end(state, after):
    (by_chip,) = push_wait("ar_wait", state, plan_chips_half, after)
    (both,) = swap_halves("ar_halves", [sum_slots(by_chip, "ar_sum_chips", for_swap=True)])
    return both.reshape(2 * both.shape[1], both.shape[2])


def kernel(x, norm1_g, w_in, shift_mu, w0, w2, a0, a2, g2, k_k, k_a, r_k, gn_w, gn_b, sgu_ln_g, sgu_ln_b, sgu_w, sgu_b, sgu_out_g, w_out, norm2_g, w_gate, w_up, w_down, final_g, loss_target, m_norm1_g, m_w_in, m_shift_mu, m_w0, m_w2, m_a0, m_a2, m_g2, m_k_k, m_k_a, m_r_k, m_gn_w, m_gn_b, m_sgu_ln_g, m_sgu_ln_b, m_sgu_w, m_sgu_b, m_sgu_out_g, m_w_out, m_norm2_g, m_w_gate, m_w_up, m_w_down, m_final_g, v_norm1_g, v_w_in, v_shift_mu, v_w0, v_w2, v_a0, v_a2, v_g2, v_k_k, v_k_a, v_r_k, v_gn_w, v_gn_b, v_sgu_ln_g, v_sgu_ln_b, v_sgu_w, v_sgu_b, v_sgu_out_g, v_w_out, v_norm2_g, v_w_gate, v_w_up, v_w_down, v_final_g):
    args = dict(locals())
    names = ["norm1_g", "w_in", "shift_mu", "w0", "w2", "a0", "a2", "g2", "k_k", "k_a", "r_k", "gn_w", "gn_b",
             "sgu_ln_g", "sgu_ln_b", "sgu_w", "sgu_b", "sgu_out_g", "w_out", "norm2_g", "w_gate", "w_up", "w_down",
             "final_g"]
    big = ["w_in", "w_out", "w_gate", "w_up", "w_down"]
    xi, yi = lax.axis_index("x"), lax.axis_index("y")
    chip = 2 * xi + yi
    x2d, tgt = x[0], loss_target[0]
    t = x2d.shape[0]

    lora_names = ["w2", "a2", "g2"]
    first, later = ["w_in"] + lora_names, ["w_out", "w_gate", "w_up", "w_down"]

    def halves(n, token=None):
        m = args[n][0] if token is None else args[n][0] + token[0, 0]
        m = m.astype(BF16)
        return m.reshape(2, m.shape[0] // 2, m.shape[1])

    first_state, first_token = gather_begin("g0", [halves(n) for n in first], norm1_g)
    later_mine = [halves(n, first_token) for n in later]
    later_lands = _landing(later_mine)
    (h1,) = row_fwd("norm1", rms_fn, [Tiled(x2d)], [norm1_g + first_token[0, 0]], [], [(D, BF16)], _tile(t, 256))
    prepared = functools.reduce(lambda p, q: p + q, [a[0, 0, :1].astype(F32) for a in later_lands])
    first_state, first_token = gather_forward("g0", first_state, prepared + h1[0, :1].astype(F32))
    got = gather_end("g0", first_state, first_token)
    later_state, later_token = gather_begin("g1", later_mine, got[0], later_lands)
    full = {n: blk.transpose(1, 0, 2).reshape(blk.shape[1], 4 * blk.shape[2]) for n, blk in zip(first[1:], got[1:])}
    s0, s1, s2, s3 = got[0]
    w_in_p = jnp.concatenate([s0, s1, s2[:, :CUT_A], s2[:, CUT_B:], s3, s2[:, CUT_A:CUT_B],
                              jnp.zeros((D, P_W - D_IN), BF16)], axis=1)
    mu_rkv = shift_mu[:, :3 * C] + later_token[0, 0]
    mu_lora = jnp.pad(shift_mu[:, 3 * C:], ((0, 0), (0, LORA_W - 288)))
    w2p = jnp.pad(full["w2"], ((0, 64), (0, 0)))
    a2p = jnp.pad(full["a2"], ((64, 0), (0, 0)))
    g2p = jnp.pad(full["g2"], ((0, 96), (0, 0)))
    ii = np.arange(C)
    jm = _scan_consts()[2]
    e_np = (np.arange(LANES)[:, None] == (ii[None, :] // HEAD))
    e_mat, e_t = jnp.asarray(e_np, BF16), jnp.asarray(e_np.T, BF16)
    b_t = jnp.pad(sgu_b[0].T, ((0, 0), (0, LANES - HEADS)))
    r_k_flat = r_k.reshape(1, C)
    tm_row = _tile(t, 256)
    tm_sgu_b = _tile(t, 128)

    p = mm(h1, w_in_p, name="proj_in")
    ps_rkv = shift_fwd(p, mu_rkv, 0, "shift_rkv")
    ps_lora = shift_fwd(p, mu_lora, P_LORA // LORA_W, "shift_lora")
    pre_par = [w0, w2p, a0, a2p, g2p, k_k, k_a]
    r_, w_, k2, v_, an, b_, g_ = row_fwd("pre", pre_fn, [Tiled(ps_rkv), Tiled(ps_lora)], pre_par, [jm],
                                         [(C, F32)] * 7, tm_row)
    v3 = to_cols(v_)
    y3, ckpt = wkv_fwd(r_, w_, k2, v3, an, b_)
    later_state, later_token = gather_forward("g1", later_state, y3)
    y_ = from_cols(y3 + later_token[0, 0])
    post_in = [Tiled(z) for z in (y_, r_, k2, v_, g_)]
    post_par = [gn_w, gn_b, r_k_flat]
    (y_rwkv,) = row_fwd("post", post_fn, post_in, post_par, [jm], [(C, BF16)], tm_row)
    sgu_in = [Tiled(p, C, 3), Tiled(p, C, 4)]
    sgu_par = [sgu_ln_g, sgu_ln_b, sgu_w[0], b_t, sgu_out_g]
    (y_sgu,) = row_fwd("sgu", sgu_fn, sgu_in, sgu_par, [e_mat, e_t], [(C, BF16)], tm_row)
    y_cat = jnp.concatenate([y_rwkv, y_sgu], axis=1)
    for n, blk in zip(later, gather_end("g1", later_state, y_cat)):
        full[n] = blk if n in ("w_gate", "w_up") else blk.reshape(4 * blk.shape[1], blk.shape[2])
    x_mid = mm(y_cat, full["w_out"], res=x2d, name="proj_out")
    (h2,) = row_fwd("norm2", rms_fn, [Tiled(x_mid)], [norm2_g], [], [(D, BF16)], tm_row)
    gate = mm(h2, full["w_gate"], shards="b", name="ffn_gate")
    up, act = mm(h2, full["w_up"], shards="b", name="ffn_up",
                 post=(lambda up_tile, gate_tile: (up_tile, swiglu_fn(gate_tile, up_tile)[0]), [gate], [F32, BF16]))
    x_out = mm(act, full["w_down"], res=x_mid, name="ffn_down")
    loss_part, dx_out, d_final_g, dx_out_b = loss_head(x_out, tgt, final_g.reshape(1, D))

    grads = {"final_g": d_final_g.reshape(D)}
    d_gate, d_up = mm(dx_out_b, full["w_down"], tb=True, name="d_act",
                      post=(lambda d_act, g_tile, u_tile: jax.vjp(swiglu_fn, g_tile, u_tile)[1]((d_act,)), [gate, up],
                            [BF16, BF16]))
    grads["w_down"] = mm(act, dx_out_b, ta=True, out_dtype=BF16, name="d_w_down")
    grads["w_gate"] = mm(h2, d_gate, shards="out", out_dtype=BF16, name="d_w_gate")
    grads["w_up"] = mm(h2, d_up, shards="out", out_dtype=BF16, name="d_w_up")
    d_h2 = mm(d_gate, full["w_gate"], shards="bt", name="d_h2_gate")
    d_h2 = mm(d_up, full["w_up"], shards="bt", res=d_h2, name="d_h2_up")
    dx_mid, grads["norm2_g"], dx_mid_b = row_bwd("norm2_b", rms_res_fn, [Tiled(x_mid)], [norm2_g], [],
                                                 [Tiled(d_h2), Tiled(dx_out)], tm_row, bf16_copy_of=(0,))
    d_ycat = mm(dx_mid_b, full["w_out"], tb=True, name="d_ycat")
    grads["w_out"] = mm(y_cat, dx_mid_b, ta=True, out_dtype=BF16, name="d_w_out")
    rs_state, rs_token = sibling_exchange_begin([grads[n].reshape((4,) + args[n].shape[1:]) for n in later], "rs2")
    d_pu, d_pv, grads["sgu_ln_g"], grads["sgu_ln_b"], d_sgu_w, d_b_t, grads["sgu_out_g"] = row_bwd(
        "sgu_b", sgu_fn, sgu_in, sgu_par, [e_mat, e_t], [Tiled(d_ycat, C, 1)], tm_sgu_b,
        tiled_out_dtypes=[BF16, BF16])
    grads["sgu_w"] = d_sgu_w[None]
    grads["sgu_b"] = d_b_t[:, :HEADS].T[None]
    post_par_b = [gn_w + rs_token[0, 0]] + post_par[1:]
    d_y, d_r1, d_k1, d_v1, d_g, grads["gn_w"], grads["gn_b"], d_r_k = row_bwd(
        "post_b", post_fn, post_in, post_par_b, [jm], [Tiled(d_ycat, C, 0)], tm_row)
    grads["r_k"] = d_r_k.reshape(r_k.shape)
    rs_state, rs_token = reduce_scatter_continue(rs_state, later, "rs2", d_y)
    d_r2, d_w, d_k2, d_a, d_b, d_v3 = wkv_bwd(r_, w_, k2, v3, an, b_, ckpt, to_cols(d_y + rs_token[0, 0]))
    d_v2 = from_cols(d_v3)
    pre_cts = [(Tiled(d_r1), Tiled(d_r2)), Tiled(d_w), (Tiled(d_k1), Tiled(d_k2)), (Tiled(d_v1), Tiled(d_v2)),
               Tiled(d_a), Tiled(d_b), Tiled(d_g)]
    d_ps_rkv, d_ps_lora, grads["w0"], d_w2p, grads["a0"], d_a2p, d_g2p, grads["k_k"], grads["k_a"] = row_bwd(
        "pre_b", pre_fn, [Tiled(ps_rkv), Tiled(ps_lora)], pre_par, [jm], pre_cts, tm_row)
    grads["w2"], grads["a2"], grads["g2"] = d_w2p[None, :64], d_a2p[None, 64:], d_g2p[None, :160]
    d_p_rkv, d_mu_rkv = shift_bwd(d_ps_rkv, p, mu_rkv, 0, "shift_rkv_b")
    d_p_lora, d_mu_lora = shift_bwd(d_ps_lora, p, mu_lora, P_LORA // LORA_W, "shift_lora_b")
    grads["shift_mu"] = jnp.concatenate([d_mu_rkv, d_mu_lora[:, :288]], axis=1)
    d_p = jnp.concatenate([d_p_rkv, d_pu, d_pv, d_p_lora], axis=1)
    d_h1 = mm(d_p, w_in_p, tb=True, name="d_h1")
    d_w_in_p = mm(h1, d_p, ta=True, out_dtype=BF16, name="d_w_in")
    ws = D_IN // 4
    shard2 = jnp.concatenate([d_w_in_p[:, 2 * ws:3 * C], d_w_in_p[:, P_LORA:P_LORA + 288],
                              d_w_in_p[:, 3 * C:3 * C + ws - CUT_B]], axis=1)
    grads["w_in"] = jnp.stack([d_w_in_p[:, :ws], d_w_in_p[:, ws:2 * ws], shard2,
                               d_w_in_p[:, 3 * C + ws - CUT_B:P_LORA]])
    dx, grads["norm1_g"] = row_bwd("norm1_b", rms_res_fn, [Tiled(x2d)], [norm1_g], [],
                                   [Tiled(d_h1), Tiled(dx_mid)], tm_row)

    small = [n for n in names if n not in big]
    small_shapes = [(1, 64, C) if n in ("w2", "a2") else (1, 160, C) if n == "g2" else args[n].shape for n in small]
    vec = _pack([grads[n] for n in small], F32)
    w_in_state, w_in_token = reduce_scatter_begin([grads["w_in"]], ["w_in"], "rs1")
    small_state, small_token = all_reduce_begin(vec)
    total = {n: g[None] for n, g in zip(later, reduce_scatter_end(rs_state, later, "rs2", w_in_token + small_token))}
    loss = lax.psum(loss_part[0, 0], ("x", "y", "c"))
    delta, new_m, new_v = {}, {}, {}

    def adam_big(n):
        s = args[n].shape
        d_, m_, v__ = adamw(args[n][0], total[n][0], args["m_" + n][0], args["v_" + n][0], "adamw_" + n)
        delta[n], new_m[n], new_v[n] = d_.reshape(s), m_.reshape(s), v__.reshape(s)

    for n in later:
        adam_big(n)
    total["w_in"] = reduce_scatter_end(w_in_state, ["w_in"], "rs1", new_v["w_down"])[0][None]
    for n, g_sum in zip(small, _unpack(all_reduce_end(small_state, new_v["w_down"]), small_shapes)):
        total[n] = lax.dynamic_slice_in_dim(g_sum, chip * 256, 256, axis=2) if n in lora_names else g_sum
    adam_big("w_in")
    flat = [_pack([src[n] for n in small], F32)
            for src in (args, total, {n: args["m_" + n] for n in small}, {n: args["v_" + n] for n in small})]
    outs = adamw(*flat, "adamw_small")
    for res, o in zip((delta, new_m, new_v), outs):
        res.update(zip(small, _unpack(o, [args[n].shape for n in small])))
    return (loss, dx[None], *[total[n] for n in names], *[delta[n] for n in names],
            *[new_m[n] for n in names], *[new_v[n] for n in names])
```

```python
import functools

import numpy as np
import jax
import jax.numpy as jnp
from jax import lax
from jax.experimental import pallas as pl
from jax.experimental.pallas import tpu as pltpu

F32 = jnp.float32
BF16 = jnp.bfloat16

D = 2048
C = 1024
HEADS = 16
HEAD = 64
LANES = 128
P_W = 5632
P_LORA = 5120
LORA_W = 512
D_IN = 5408
CUT_A = 3 * C - 2 * (D_IN // 4)
CUT_B = CUT_A + 288
CHUNK = 128
RMS_EPS = 1e-6
LN_EPS = 1e-5
GN_EPS = 64e-5
L2_EPS = 1e-12
VMEM_LIMIT = 56 * 1024 * 1024


def _pcall(body, **kw):
    return pl.pallas_call(body, **kw)


def _cparams(sem):
    return pltpu.CompilerParams(dimension_semantics=sem, vmem_limit_bytes=VMEM_LIMIT)


def _tile(n, most):
    t = most
    while t > 8 and n % t:
        t //= 2
    assert n % t == 0, (n, most)
    return t


MM_TILE = (2048, 512, 2048)
MM_VMEM = 40 * 1024 * 1024


def _div_tile(n, most, quantum=LANES):
    for t in range(min(n, most) // quantum * quantum, 0, -quantum):
        if n % t == 0:
            return t
    raise ValueError((n, most, quantum))


def _mm_tiles(m, n, k, a_bytes, b_bytes, r_bytes, o_bytes):
    tm, tn, tk = _div_tile(m, MM_TILE[0]), _div_tile(n, MM_TILE[1]), _div_tile(k, MM_TILE[2])

    def need(tm, tk):
        return 2 * (tm * tk * a_bytes + tk * tn * b_bytes + tm * tn * (r_bytes + o_bytes)) + tm * tn * 4

    while need(tm, tk) > MM_VMEM:
        if tk >= tm and tk > 512:
            tk = _div_tile(k, tk - LANES)
        else:
            tm = _div_tile(m, tm - LANES)
    return tm, tn, tk


def mm(a, b, *, ta=False, tb=False, res=None, out_dtype=F32, name, shards=None, post=None):
    if shards is not None:
        return _mm_shards(a, b, res, out_dtype, name, shards, post)
    m, k = (a.shape[1], a.shape[0]) if ta else a.shape
    n = b.shape[0] if tb else b.shape[1]
    assert (b.shape[1] if tb else b.shape[0]) == k
    if post is None:
        tm, tn, tk = _mm_tiles(m, n, k, a.dtype.itemsize, b.dtype.itemsize,
                               0 if res is None else res.dtype.itemsize, jnp.dtype(out_dtype).itemsize)
    else:
        tm, tn, tk = _div_tile(m, 1024), _div_tile(n, MM_TILE[1]), _div_tile(k, MM_TILE[2])
    nk = k // tk
    dims = (((0 if ta else 1,), (1 if tb else 0,)), ((), ()))
    a_spec = pl.BlockSpec((tk, tm), lambda i, j, l: (l, i)) if ta else pl.BlockSpec((tm, tk), lambda i, j, l: (i, l))
    b_spec = pl.BlockSpec((tn, tk), lambda i, j, l: (j, l)) if tb else pl.BlockSpec((tk, tn), lambda i, j, l: (l, j))
    o_spec = pl.BlockSpec((tm, tn), lambda i, j, l: (i, j))
    return _mm_call(a, b, res, dims, (m // tm, n // tn, nk), a_spec, b_spec, o_spec, o_spec, (tm, tn),
                    jax.ShapeDtypeStruct((m, n), out_dtype), name, post)


def _mm_shards(a, b, res, out_dtype, name, shards, post=None):
    assert post is None or shards == "b"
    if shards == "b":
        (m, k), ns = a.shape, b.shape[2]
        tm, tk = _div_tile(m, 1024 if post is None else 512), _div_tile(k, 2048)
        grid, dims, acc = (m // tm, 4, k // tk), (((1,), (0,)), ((), ())), (tm, ns)
        a_spec = pl.BlockSpec((tm, tk), lambda i, s, l: (i, l))
        b_spec = pl.BlockSpec((None, tk, ns), lambda i, s, l: (s, l, 0))
        o_spec = pl.BlockSpec((tm, ns), lambda i, s, l: (i, s))
        out = jax.ShapeDtypeStruct((m, 4 * ns), out_dtype)
    elif shards == "bt":
        m, (_, n, ns) = a.shape[0], b.shape
        tm, tn = _div_tile(m, 2048), _div_tile(n, 512)
        grid, dims, acc = (m // tm, n // tn, 4), (((1,), (1,)), ((), ())), (tm, tn)
        a_spec = pl.BlockSpec((tm, ns), lambda i, j, s: (i, s))
        b_spec = pl.BlockSpec((None, tn, ns), lambda i, j, s: (s, j, 0))
        o_spec = pl.BlockSpec((tm, tn), lambda i, j, s: (i, j))
        out = jax.ShapeDtypeStruct((m, n), out_dtype)
    else:
        (t, m), ns = a.shape, b.shape[1] // 4
        tm, tk = _div_tile(m, 1024), _div_tile(t, 2048)
        grid, dims, acc = (m // tm, 4, t // tk), (((0,), (0,)), ((), ())), (tm, ns)
        a_spec = pl.BlockSpec((tk, tm), lambda i, s, l: (l, i))
        b_spec = pl.BlockSpec((tk, ns), lambda i, s, l: (l, s))
        o_spec = pl.BlockSpec((None, tm, ns), lambda i, s, l: (s, i, 0))
        out = jax.ShapeDtypeStruct((4, m, ns), out_dtype)
    return _mm_call(a, b, res, dims, grid, a_spec, b_spec, o_spec, o_spec, acc, out, name, post)


def _mm_call(a, b, res, dims, grid, a_spec, b_spec, r_spec, o_spec, acc_shape, out, name, post=None):
    nk = grid[2]
    post_fn, post_in, post_dtypes = (None, [], [out.dtype]) if post is None else post
    n_extra = (res is not None) + len(post_in)

    def body(*refs):
        a_ref, b_ref = refs[:2]
        r_ref = None if res is None else refs[2]
        p_refs = refs[2 + (res is not None):2 + n_extra]
        o_refs = refs[2 + n_extra:2 + n_extra + len(post_dtypes)]

        def finish(acc):
            if r_ref is not None:
                acc = acc + r_ref[...].astype(F32)
            outs = (acc,) if post_fn is None else post_fn(acc, *[p[...] for p in p_refs])
            for o_ref, o in zip(o_refs, outs):
                o_ref[...] = o.astype(o_ref.dtype)

        prod = lax.dot_general(a_ref[...].astype(BF16), b_ref[...].astype(BF16), dims, preferred_element_type=F32)
        if nk == 1:
            finish(prod)
            return
        acc_ref = refs[-1]
        kk = pl.program_id(2)

        @pl.when(kk == 0)
        def _():
            acc_ref[...] = prod

        @pl.when(kk > 0)
        def _():
            acc_ref[...] += prod

        @pl.when(kk == nk - 1)
        def _():
            finish(acc_ref[...])

    in_specs = [a_spec, b_spec] + ([r_spec] if res is not None else []) + [o_spec] * len(post_in)
    args = (a, b) + ((res,) if res is not None else ()) + tuple(post_in)
    outs = _pcall(
        body, name=name, grid=grid, in_specs=in_specs, out_specs=[o_spec] * len(post_dtypes),
        out_shape=[jax.ShapeDtypeStruct(out.shape, dt) for dt in post_dtypes],
        scratch_shapes=[] if nk == 1 else [pltpu.VMEM(acc_shape, F32)],
        compiler_params=_cparams(("parallel", "parallel", "arbitrary")),
    )(*args)
    return outs[0] if post is None else outs


class Tiled:
    def __init__(self, arr, width=None, col=0):
        self.arr, self.width, self.col = arr, (arr.shape[1] if width is None else width), col

    def spec(self, tm):
        col = self.col
        return pl.BlockSpec((tm, self.width), lambda i: (i, col))


def _full_spec(p):
    nd = p.ndim
    return pl.BlockSpec(p.shape, lambda i: (0,) * nd)


def row_fwd(name, fn, tiled, params, consts, outs, tm):
    t = tiled[0].arr.shape[0]
    n_in = len(tiled) + len(params) + len(consts)

    def body(*refs):
        res = fn(*[r[...] for r in refs[:n_in]])
        for o_ref, r in zip(refs[n_in:], res):
            o_ref[...] = r.astype(o_ref.dtype)

    return _pcall(
        body, name=name, grid=(t // tm,),
        in_specs=[x.spec(tm) for x in tiled] + [_full_spec(p) for p in params + consts],
        out_specs=[pl.BlockSpec((tm, w), lambda i: (i, 0)) for w, _ in outs],
        out_shape=[jax.ShapeDtypeStruct((t, w), dt) for w, dt in outs],
        compiler_params=_cparams(("parallel",)),
    )(*[x.arr for x in tiled], *params, *consts)


def row_bwd(name, fn, tiled, params, consts, cts, tm, tiled_out_dtypes=None, bf16_copy_of=()):
    t = tiled[0].arr.shape[0]
    nt, npar, ncon = len(tiled), len(params), len(consts)
    cts = [c if isinstance(c, tuple) else (c,) for c in cts]
    flat_cts = [x for c in cts for x in c]
    tiled_out_dtypes = tiled_out_dtypes or [F32] * nt
    copies = [tiled[j] for j in bf16_copy_of]

    def body(*refs):
        n_in = nt + npar + ncon
        ins = [r[...].astype(F32) for r in refs[:nt + npar]]
        con = [r[...] for r in refs[nt + npar:n_in]]
        ct_refs = list(refs[n_in:n_in + len(flat_cts)])
        o = refs[n_in + len(flat_cts):]
        ct = []
        for c in cts:
            parts = [ct_refs.pop(0)[...].astype(F32) for _ in c]
            ct.append(functools.reduce(lambda p, q: p + q, parts))
        _, vjp = jax.vjp(lambda *a: fn(*a, *con), *ins)
        g = vjp(tuple(ct))
        for j in range(nt):
            o[j][...] = g[j].astype(o[j].dtype)
        for n, j in enumerate(bf16_copy_of):
            o[nt + npar + n][...] = g[j].astype(BF16)
        first = pl.program_id(0) == 0

        @pl.when(first)
        def _():
            for j in range(npar):
                o[nt + j][...] = g[nt + j]

        @pl.when(jnp.logical_not(first))
        def _():
            for j in range(npar):
                o[nt + j][...] += g[nt + j]

    return _pcall(
        body, name=name, grid=(t // tm,),
        in_specs=[x.spec(tm) for x in tiled] + [_full_spec(p) for p in params + consts]
        + [x.spec(tm) for x in flat_cts],
        out_specs=[pl.BlockSpec((tm, x.width), lambda i: (i, 0)) for x in tiled] + [_full_spec(p) for p in params]
        + [pl.BlockSpec((tm, x.width), lambda i: (i, 0)) for x in copies],
        out_shape=[jax.ShapeDtypeStruct((t, x.width), dt) for x, dt in zip(tiled, tiled_out_dtypes)]
        + [jax.ShapeDtypeStruct(p.shape, F32) for p in params]
        + [jax.ShapeDtypeStruct((t, x.width), BF16) for x in copies],
        compiler_params=_cparams(("arbitrary",)),
    )(*[x.arr for x in tiled], *params, *consts, *[x.arr for x in flat_cts])


def _split_dot(x, w):
    hi = x.astype(BF16)
    lo = (x - hi.astype(F32)).astype(BF16)
    return jnp.dot(hi, w, preferred_element_type=F32) + jnp.dot(lo, w, preferred_element_type=F32)


def _headsum_blocks(x, j2k):
    out = []
    for c0 in range(0, x.shape[1], LANES):
        blk = x[:, c0:c0 + LANES]
        hi = blk.astype(BF16)
        lo = (blk - hi.astype(F32)).astype(BF16)
        out.append(jnp.dot(jnp.concatenate([hi, lo], axis=1), j2k, preferred_element_type=F32))
    return jnp.concatenate(out, axis=1)


@jax.custom_vjp
def headsum(x, j2k):
    return _headsum_blocks(x, j2k)


def _headsum_fwd(x, j2k):
    return _headsum_blocks(x, j2k), j2k


def _headsum_bwd(j2k, ct):
    return _headsum_blocks(ct, j2k), jnp.zeros_like(j2k)


headsum.defvjp(_headsum_fwd, _headsum_bwd)


def _bdot(x, w):
    return jnp.dot(x.astype(BF16), w.astype(BF16), preferred_element_type=F32)


def _sigmoid(x):
    return jax.nn.sigmoid(x)


def _softplus(x):
    return jnp.maximum(x, 0.0) + jnp.log(1.0 + jnp.exp(-jnp.abs(x)))


def rms_fn(x, g):
    return (x * lax.rsqrt(jnp.mean(x * x, axis=-1, keepdims=True) + RMS_EPS) * g,)


def rms_res_fn(x, g):
    return rms_fn(x, g)[0], x


def pre_fn(rkv, lora, w0, w2p, a0, a2p, g2p, k_k, k_a, jm):
    r, k, v = rkv[:, :C], rkv[:, C:2 * C], rkv[:, 2 * C:]
    zwa, zg = lora[:, :LANES], lora[:, LANES:LANES + 256]
    w_log = -_softplus(-(w0 + _bdot(jnp.tanh(zwa), w2p))) - 0.5
    decay = jnp.exp(-jnp.exp(w_log))
    a = _sigmoid(a0 + _bdot(zwa, a2p))
    g = _bdot(_sigmoid(zg), g2p)
    kk = k * k_k
    kk = kk / jnp.maximum(jnp.sqrt(headsum(kk * kk, jm)), L2_EPS)
    k2 = k * (1.0 + (a - 1.0) * k_a)
    return r, decay, k2, v, -kk, kk * a, g


def post_fn(y, r, k2, v, g, gn_w, gn_b, r_k, jm):
    mu = headsum(y, jm) * (1.0 / HEAD)
    yc = y - mu
    var = headsum(yc * yc, jm) * (1.0 / HEAD)
    yn = yc * lax.rsqrt(var + GN_EPS) * gn_w + gn_b
    bonus = headsum(r * k2 * r_k, jm) * v
    return ((yn + bonus) * g,)


def _gelu(x):
    return 0.5 * x * (1.0 + lax.erf(x * np.float32(1.0 / np.sqrt(2.0))))


@jax.custom_vjp
def expand_groups(b_t, e, e_t):
    return _split_dot(b_t, e)


def _expand_groups_fwd(b_t, e, e_t):
    return _split_dot(b_t, e), (e, e_t)


def _expand_groups_bwd(res, ct):
    e, e_t = res
    return _split_dot(ct, e_t), jnp.zeros_like(e), jnp.zeros_like(e_t)


expand_groups.defvjp(_expand_groups_fwd, _expand_groups_bwd)


def sgu_fn(pu, pv, ln_g, ln_b, w_s, b_t, out_g, e, e_t):
    rows = pu.shape[0]
    b_exp = expand_groups(b_t, e, e_t)
    u, v = _gelu(pu), _gelu(pv)
    mu = jnp.mean(v, axis=-1, keepdims=True)
    vc = v - mu
    var = jnp.mean(vc * vc, axis=-1, keepdims=True)
    v = vc * lax.rsqrt(var + LN_EPS) * ln_g + ln_b
    tri = lax.broadcasted_iota(jnp.int32, (CHUNK, CHUNK), 0) >= lax.broadcasted_iota(jnp.int32, (CHUNK, CHUNK), 1)
    left = lax.broadcasted_iota(jnp.int32, (CHUNK, LANES), 1) < HEAD
    chunks = []
    for c0 in range(0, rows, CHUNK):
        cols = []
        for gp in range(HEADS // 2):
            vp = v[c0:c0 + CHUNK, gp * LANES:(gp + 1) * LANES]
            wa = jnp.where(tri, w_s[2 * gp], 0.0)
            wb = jnp.where(tri, w_s[2 * gp + 1], 0.0)
            cols.append(jnp.where(left, _bdot(wa, vp), _bdot(wb, vp)))
        chunks.append(jnp.concatenate(cols, axis=1) + b_exp)
    s = jnp.concatenate(chunks, axis=0) if len(chunks) > 1 else chunks[0]
    y = u * s
    return (y * lax.rsqrt(jnp.mean(y * y, axis=-1, keepdims=True) + RMS_EPS) * out_g,)


def swiglu_fn(gate, up):
    return (gate * _sigmoid(gate) * up,)


SCAN_CHUNK = 32
NB = C // LANES
FWD_GROUP = 2
BWD_GROUP = 8


def to_cols(a):
    t = a.shape[0]
    return a.reshape(t // 8, LANES, HEAD).transpose(0, 2, 1)


def from_cols(a):
    return a.transpose(0, 2, 1).reshape(a.shape[0] * 8, C)


def _scan_consts():
    i = np.arange(LANES)
    spread = (i[:, None] % 2) == (i[None, :] // HEAD)
    pick = (i[:, None] // HEAD) == (i[None, :] % 2)
    j2 = (i[:, None] // HEAD) == (i[None, :] // HEAD)
    sel = (i[None, :] // 2) == (np.arange(8 * NB * HEAD)[:, None] // HEAD)
    return (jnp.asarray(spread, BF16), jnp.asarray(pick, BF16), jnp.asarray(np.concatenate([j2, j2], 0), BF16),
            jnp.asarray(sel, BF16))


def _stack(blocks):
    return jnp.concatenate(blocks, axis=0) if len(blocks) > 1 else blocks[0]


def _unstack(x, n):
    return [x[i * HEAD:(i + 1) * HEAD] for i in range(n)]


def _headsums(blocks, j2k, group):
    res = []
    for g0 in range(0, len(blocks), group):
        x = _stack(blocks[g0:g0 + group])
        hi = x.astype(BF16)
        lo = (x - hi.astype(F32)).astype(BF16)
        out = jnp.dot(jnp.concatenate([hi, lo], axis=1), j2k, preferred_element_type=F32)
        res += _unstack(out, len(blocks[g0:g0 + group]))
    return res


def _headsums_out(blocks, pick):
    return _unstack(jnp.dot(_stack(blocks).astype(BF16), pick, preferred_element_type=F32), len(blocks))


def _expand8(tile, sel, spread):
    lhs = jnp.tile(tile.astype(BF16), (8 * NB, 1)) * sel
    return _unstack(jnp.dot(lhs, spread, preferred_element_type=F32), 8 * NB)


def _collapse(tile, blocks, first, lane_pair):
    for n, blk in enumerate(blocks):
        tile = jnp.where(jnp.tile(lane_pair == first + n, (HEAD // 8, 1)), blk, tile)
    return tile


def _row(tile, j, cb):
    return jnp.broadcast_to(tile[j:j + 1, cb * LANES:(cb + 1) * LANES], (HEAD, LANES))


def _scan_step(s_ref, rows, vexp, j2k, j, hist=None):
    w_t, k_t, a_t, b_t, r_t = rows
    s = [s_ref[:, cb * LANES:(cb + 1) * LANES] for cb in range(NB)]
    sab = _headsums([s[cb] * _row(a_t, j, cb) for cb in range(NB)], j2k, FWD_GROUP if hist is None else BWD_GROUP)
    out = []
    for cb in range(NB):
        sl = slice(cb * LANES, (cb + 1) * LANES)
        s_new = s[cb] * _row(w_t, j, cb) + sab[cb] * _row(b_t, j, cb) + vexp[j * NB + cb] * _row(k_t, j, cb)
        s_ref[:, sl] = s_new
        if hist is None:
            out.append(s_new * _row(r_t, j, cb))
        else:
            s_hist, sab_hist, idx = hist
            s_hist[idx + 1, :, sl] = s_new
            sab_hist[idx, :, sl] = sab[cb]
    return out


def wkv_fwd(r, w, k, v3, a, b):
    t = r.shape[0]
    nc = t // SCAN_CHUNK
    n8 = SCAN_CHUNK // 8

    def body(r_ref, w_ref, k_ref, a_ref, b_ref, v3_ref, spread_ref, pick_ref, j2k_ref, sel_ref, y3_ref, ck_ref,
             s_ref):
        @pl.when(pl.program_id(0) == 0)
        def _():
            s_ref[...] = jnp.zeros_like(s_ref)

        ck_ref[0] = s_ref[...]
        spread, pick, j2k, sel = spread_ref[...], pick_ref[...], j2k_ref[...], sel_ref[...]
        lane_pair = lax.broadcasted_iota(jnp.int32, (8, LANES), 1) // 2

        def t8_body(t8, carry):
            row0 = pl.multiple_of(t8 * 8, 8)
            rows = [ref[pl.ds(row0, 8), :] for ref in (w_ref, k_ref, a_ref, b_ref, r_ref)]
            vexp = _expand8(v3_ref[t8], sel, spread)
            y3 = jnp.zeros((HEAD, LANES), F32)
            for j in range(8):
                y3 = _collapse(y3, _headsums_out(_scan_step(s_ref, rows, vexp, j2k, j), pick), j * NB, lane_pair)
            y3_ref[t8] = y3
            return carry

        lax.fori_loop(0, n8, t8_body, 0)

    row_spec = pl.BlockSpec((SCAN_CHUNK, C), lambda i: (i, 0))
    col_spec = pl.BlockSpec((n8, HEAD, LANES), lambda i: (i, 0, 0))
    consts = _scan_consts()
    return _pcall(
        body, name="wkv_fwd", grid=(nc,),
        in_specs=[row_spec] * 5 + [col_spec] + [pl.BlockSpec(c.shape, lambda i: (0, 0)) for c in consts],
        out_specs=[col_spec, pl.BlockSpec((1, HEAD, C), lambda i: (i, 0, 0))],
        out_shape=[jax.ShapeDtypeStruct((t // 8, HEAD, LANES), F32), jax.ShapeDtypeStruct((nc, HEAD, C), F32)],
        scratch_shapes=[pltpu.VMEM((HEAD, C), F32)],
        compiler_params=_cparams(("arbitrary",)),
    )(r, w, k, a, b, v3, *consts)


def wkv_bwd(r, w, k, v3, a, b, ckpt, dy3):
    t = r.shape[0]
    nc = t // SCAN_CHUNK
    n8 = SCAN_CHUNK // 8

    def body(r_ref, w_ref, k_ref, a_ref, b_ref, v3_ref, dy3_ref, ck_ref, spread_ref, pick_ref, j2k_ref, sel_ref,
             dr_ref, dw_ref, dk_ref, da_ref, db_ref, dv3_ref, s_ref, g_ref, s_hist, sab_hist):
        @pl.when(pl.program_id(0) == 0)
        def _():
            g_ref[...] = jnp.zeros_like(g_ref)

        spread, pick, j2k, sel = spread_ref[...], pick_ref[...], j2k_ref[...], sel_ref[...]
        lane_pair = lax.broadcasted_iota(jnp.int32, (8, LANES), 1) // 2
        sub = lax.broadcasted_iota(jnp.int32, (8, LANES), 0)
        s_ref[...] = ck_ref[0]
        s_hist[0] = ck_ref[0]

        def redo(t8, carry):
            row0 = pl.multiple_of(t8 * 8, 8)
            rows = [ref[pl.ds(row0, 8), :] for ref in (w_ref, k_ref, a_ref, b_ref, r_ref)]
            vexp = _expand8(v3_ref[t8], sel, spread)
            for j in range(8):
                _scan_step(s_ref, rows, vexp, j2k, j, hist=(s_hist, sab_hist, t8 * 8 + j))
            return carry

        lax.fori_loop(0, n8, redo, 0)

        def back(q, carry):
            t8 = n8 - 1 - q
            row0 = pl.multiple_of(t8 * 8, 8)
            w_t, k_t, a_t, b_t, r_t = [ref[pl.ds(row0, 8), :] for ref in (w_ref, k_ref, a_ref, b_ref, r_ref)]
            vexp = _expand8(v3_ref[t8], sel, spread)
            dyexp = _expand8(dy3_ref[t8], sel, spread)
            dv3 = jnp.zeros((HEAD, LANES), F32)
            tiles = [[jnp.zeros((8, LANES), F32) for _ in range(NB)] for _ in range(5)]
            for j in range(7, -1, -1):
                idx = t8 * 8 + j
                g = [g_ref[:, cb * LANES:(cb + 1) * LANES] + dyexp[j * NB + cb] * _row(r_t, j, cb)
                     for cb in range(NB)]
                dsab = _headsums([g[cb] * _row(b_t, j, cb) for cb in range(NB)], j2k, BWD_GROUP)
                dv3 = _collapse(dv3, _headsums_out([g[cb] * _row(k_t, j, cb) for cb in range(NB)], pick), j * NB,
                                lane_pair)
                for cb in range(NB):
                    sl = slice(cb * LANES, (cb + 1) * LANES)
                    s_new = s_hist[idx + 1, :, sl]
                    s_old = s_hist[idx, :, sl]
                    sab = sab_hist[idx, :, sl]
                    sums = (s_new * dyexp[j * NB + cb], g[cb] * s_old, g[cb] * vexp[j * NB + cb],
                            s_old * dsab[cb], g[cb] * sab)
                    for n, prod in enumerate(sums):
                        rowsum = jnp.broadcast_to(jnp.sum(prod, axis=0, keepdims=True), (8, LANES))
                        tiles[n][cb] = jnp.where(sub == j, rowsum, tiles[n][cb])
                    g_ref[:, sl] = g[cb] * _row(w_t, j, cb) + dsab[cb] * _row(a_t, j, cb)
            dv3_ref[t8] = dv3
            for n, ref in enumerate((dr_ref, dw_ref, dk_ref, da_ref, db_ref)):
                for cb in range(NB):
                    ref[pl.ds(row0, 8), cb * LANES:(cb + 1) * LANES] = tiles[n][cb]
            return carry

        lax.fori_loop(0, n8, back, 0)

    row_spec = pl.BlockSpec((SCAN_CHUNK, C), lambda i: (nc - 1 - i, 0))
    col_spec = pl.BlockSpec((n8, HEAD, LANES), lambda i: (nc - 1 - i, 0, 0))
    consts = _scan_consts()
    return _pcall(
        body, name="wkv_bwd", grid=(nc,),
        in_specs=[row_spec] * 5 + [col_spec, col_spec, pl.BlockSpec((1, HEAD, C), lambda i: (nc - 1 - i, 0, 0))]
        + [pl.BlockSpec(c.shape, lambda i: (0, 0)) for c in consts],
        out_specs=[row_spec] * 5 + [col_spec],
        out_shape=[jax.ShapeDtypeStruct((t, C), F32)] * 5 + [jax.ShapeDtypeStruct((t // 8, HEAD, LANES), F32)],
        scratch_shapes=[pltpu.VMEM((HEAD, C), F32), pltpu.VMEM((HEAD, C), F32),
                        pltpu.VMEM((SCAN_CHUNK + 1, HEAD, C), F32), pltpu.VMEM((SCAN_CHUNK, HEAD, C), F32)],
        compiler_params=_cparams(("arbitrary",)),
    )(r, w, k, a, b, v3, dy3, ckpt, *consts)


def _prev_rows(cur, before, first_tile):
    last = jnp.where(first_tile, 0.0, before[7:8, :])
    row = lax.broadcasted_iota(jnp.int32, cur.shape, 0)
    return jnp.where(row == 0, last, pltpu.roll(cur, 1, 0))


def shift_fwd(p, mu, col, name):
    t, width = p.shape[0], mu.shape[1]
    tm = _tile(t, 256)

    def body(p_ref, before_ref, mu_ref, o_ref):
        cur = p_ref[...]
        prev = _prev_rows(cur, before_ref[...], pl.program_id(0) == 0)
        o_ref[...] = cur + (prev - cur) * mu_ref[...]

    return _pcall(
        body, name=name, grid=(t // tm,),
        in_specs=[pl.BlockSpec((tm, width), lambda i: (i, col)),
                  pl.BlockSpec((8, width), lambda i: (jnp.maximum(i * (tm // 8) - 1, 0), col)),
                  pl.BlockSpec((1, width), lambda i: (0, 0))],
        out_specs=pl.BlockSpec((tm, width), lambda i: (i, 0)),
        out_shape=jax.ShapeDtypeStruct((t, width), F32),
        compiler_params=_cparams(("parallel",)),
    )(p, p, mu)


def shift_bwd(dps, p, mu, col, name):
    t, width = dps.shape
    tm = _tile(t, 256)
    nt = t // tm

    def body(d_ref, after_ref, p_ref, before_ref, mu_ref, dp_ref, dmu_ref):
        i = pl.program_id(0)
        d, cur, mu_v = d_ref[...], p_ref[...], mu_ref[...]
        row = lax.broadcasted_iota(jnp.int32, d.shape, 0)
        first_after = jnp.where(i == nt - 1, 0.0, after_ref[0:1, :])
        nxt = jnp.where(row == tm - 1, first_after, pltpu.roll(d, tm - 1, 0))
        dp_ref[...] = (d * (1.0 - mu_v) + nxt * mu_v).astype(BF16)
        prev = _prev_rows(cur, before_ref[...], i == 0)
        part = jnp.sum(d * (prev - cur), axis=0, keepdims=True)

        @pl.when(i == 0)
        def _():
            dmu_ref[...] = part

        @pl.when(i > 0)
        def _():
            dmu_ref[...] += part

    return _pcall(
        body, name=name, grid=(nt,),
        in_specs=[pl.BlockSpec((tm, width), lambda i: (i, 0)),
                  pl.BlockSpec((8, width), lambda i: (jnp.minimum((i + 1) * (tm // 8), t // 8 - 1), 0)),
                  pl.BlockSpec((tm, width), lambda i: (i, col)),
                  pl.BlockSpec((8, width), lambda i: (jnp.maximum(i * (tm // 8) - 1, 0), col)),
                  pl.BlockSpec((1, width), lambda i: (0, 0))],
        out_specs=[pl.BlockSpec((tm, width), lambda i: (i, 0)), pl.BlockSpec((1, width), lambda i: (0, 0))],
        out_shape=[jax.ShapeDtypeStruct((t, width), BF16), jax.ShapeDtypeStruct((1, width), F32)],
        compiler_params=_cparams(("arbitrary",)),
    )(dps, dps, p, p, mu)


def loss_head(x3, tgt, g):
    t, d = x3.shape
    tm = _tile(t, 256)

    def body(x_ref, t_ref, g_ref, loss_ref, dx_ref, dg_ref, dxb_ref):
        (y,), vjp = jax.vjp(lambda a, b: rms_fn(a, b), x_ref[...], g_ref[...])
        diff = y - t_ref[...]
        part = 0.5 * jnp.sum(jnp.mean(diff * diff, axis=-1, keepdims=True), axis=0, keepdims=True)
        dx, dg = vjp((diff * (1.0 / d),))
        dx_ref[...] = dx
        dxb_ref[...] = dx.astype(BF16)
        part = jnp.broadcast_to(part, (1, LANES))
        first = pl.program_id(0) == 0

        @pl.when(first)
        def _():
            loss_ref[...] = part
            dg_ref[...] = dg

        @pl.when(jnp.logical_not(first))
        def _():
            loss_ref[...] += part
            dg_ref[...] += dg

    row = pl.BlockSpec((tm, d), lambda i: (i, 0))
    vec = pl.BlockSpec((1, d), lambda i: (0, 0))
    return _pcall(
        body, name="loss_head", grid=(t // tm,), in_specs=[row, row, vec],
        out_specs=[pl.BlockSpec((1, LANES), lambda i: (0, 0)), row, vec, row],
        out_shape=[jax.ShapeDtypeStruct((1, LANES), F32), jax.ShapeDtypeStruct((t, d), F32),
                   jax.ShapeDtypeStruct((1, d), F32), jax.ShapeDtypeStruct((t, d), BF16)],
        compiler_params=_cparams(("arbitrary",)),
    )(x3, tgt, g)


ADAM_LR, ADAM_B1, ADAM_B2, ADAM_EPS, ADAM_WD, ADAM_STEP = 0.001, 0.9, 0.999, 1e-08, 0.01, 10


def adamw(w, g, m, v, name):
    rows, width = w.shape
    tr = _div_tile(rows, 256, 8)

    def body(w_ref, g_ref, m_ref, v_ref, d_ref, nm_ref, nv_ref):
        gv = g_ref[...]
        m_new = ADAM_B1 * m_ref[...] + (1.0 - ADAM_B1) * gv
        v_new = ADAM_B2 * v_ref[...] + (1.0 - ADAM_B2) * (gv * gv)
        m_hat = m_new / (1.0 - ADAM_B1 ** ADAM_STEP)
        v_hat = v_new / (1.0 - ADAM_B2 ** ADAM_STEP)
        d_ref[...] = -ADAM_LR * (m_hat / (jnp.sqrt(v_hat) + ADAM_EPS) + ADAM_WD * w_ref[...])
        nm_ref[...] = m_new
        nv_ref[...] = v_new

    spec = pl.BlockSpec((tr, width), lambda i: (i, 0))
    return _pcall(
        body, name=name, grid=(rows // tr,), in_specs=[spec] * 4, out_specs=[spec] * 3,
        out_shape=[jax.ShapeDtypeStruct((rows, width), F32)] * 3,
        compiler_params=_cparams(("parallel",)),
    )(w, g, m, v)


def sum_slots(q, name, for_swap=False):
    n, rows, width = q.shape
    tr = _tile(rows, 512)

    def body(*refs):
        refs = refs[for_swap:]
        acc = refs[0][...].astype(F32)
        for r in refs[1:n]:
            acc = acc + r[...].astype(F32)
        refs[n][...] = acc

    if not for_swap:
        specs = [pl.BlockSpec((None, tr, width), functools.partial(lambda s, i: (s, i, 0), s)) for s in range(n)]
        return _pcall(body, name=name, grid=(rows // tr,), in_specs=specs,
                      out_specs=pl.BlockSpec((tr, width), lambda i: (i, 0)),
                      out_shape=jax.ShapeDtypeStruct((rows, width), F32),
                      compiler_params=_cparams(("parallel",)))(*([q] * n))
    specs = [pl.BlockSpec((None, tr, width), functools.partial(lambda s, i, c_ref: (s, i, 0), s)) for s in range(n)]
    grid_spec = pltpu.PrefetchScalarGridSpec(
        num_scalar_prefetch=1, grid=(rows // tr,), in_specs=specs,
        out_specs=pl.BlockSpec((None, tr, width), lambda i, c_ref: (c_ref[0], i, 0)))
    return _pcall(body, name=name, grid_spec=grid_spec, out_shape=jax.ShapeDtypeStruct((2, rows, width), F32),
                  compiler_params=_cparams(("parallel",)))(
        lax.axis_index("c").astype(jnp.int32).reshape(1), *([q] * n))


MESH = pl.DeviceIdType.MESH
ANY_SPEC = pl.BlockSpec(memory_space=pl.ANY)


def _coords():
    return lax.axis_index("x"), lax.axis_index("y"), lax.axis_index("c")


def _other_chips(x, y):
    return [(1 - x, y), (x, 1 - y), (1 - x, 1 - y)]


def push(name, srcs, n_dst, plan_fn):
    n_arr = len(srcs)
    n_send = len(plan_fn(0, 0, 0))

    def body(*refs):
        src_refs, dst_refs = refs[:n_arr], refs[n_arr:2 * n_arr]
        send_sems, recv_sems = refs[2 * n_arr:]
        sends = plan_fn(*_coords())
        out = []
        for q, (src_ref, dst_ref) in enumerate(zip(src_refs, dst_refs)):
            out += [pltpu.make_async_remote_copy(src_ref.at[si], dst_ref.at[di], send_sems.at[q * n_send + k],
                                                 recv_sems.at[q * n_send + k], device_id=peer, device_id_type=MESH)
                    for k, (si, peer, di, _) in enumerate(sends)]
        for cp in out:
            cp.start()
        for q, (src_ref, dst_ref) in enumerate(zip(src_refs, dst_refs)):
            for k, (si, peer, _, ri) in enumerate(sends):
                pltpu.make_async_remote_copy(src_ref.at[si], dst_ref.at[ri], send_sems.at[q * n_send + k],
                                             recv_sems.at[q * n_send + k], device_id=peer,
                                             device_id_type=MESH).wait_recv()
        for cp in out:
            cp.wait_send()

    return _pcall(
        body, name=name, in_specs=[ANY_SPEC] * n_arr, out_specs=[ANY_SPEC] * n_arr,
        out_shape=[jax.ShapeDtypeStruct((n_dst,) + s.shape[1:], s.dtype) for s in srcs],
        scratch_shapes=[pltpu.SemaphoreType.DMA((n_arr * n_send,)), pltpu.SemaphoreType.DMA((n_arr * n_send,))],
    )(*srcs)


def plan_sibling_quarters(x, y, c):
    return [(2 * s + (1 - c), (x, y, 1 - c), s, s) for s in range(4)]


def plan_chips_by_shard(x, y, c):
    me = 2 * x + y
    return [(2 * px + py, (px, py, c), me, 2 * px + py) for px, py in _other_chips(x, y)]


def plan_chips_half(x, y, c):
    me = 2 * x + y
    return [(c, (px, py, c), me, 2 * px + py) for px, py in _other_chips(x, y)]


def swap_halves(name, bufs):
    n_arr = len(bufs)

    def body(*refs):
        buf_refs = refs[n_arr:2 * n_arr]
        send_sems, recv_sems = refs[2 * n_arr:]
        x, y, c = _coords()
        out = [pltpu.make_async_remote_copy(b.at[c], b.at[c], send_sems.at[q], recv_sems.at[q],
                                            device_id=(x, y, 1 - c), device_id_type=MESH)
               for q, b in enumerate(buf_refs)]
        for cp in out:
            cp.start()
        for q, b in enumerate(buf_refs):
            pltpu.make_async_remote_copy(b.at[1 - c], b.at[1 - c], send_sems.at[q], recv_sems.at[q],
                                         device_id=(x, y, 1 - c), device_id_type=MESH).wait_recv()
        for cp in out:
            cp.wait_send()

    return _pcall(
        body, name=name, in_specs=[ANY_SPEC] * n_arr, out_specs=[ANY_SPEC] * n_arr,
        out_shape=[jax.ShapeDtypeStruct(b.shape, b.dtype) for b in bufs],
        input_output_aliases={q: q for q in range(n_arr)},
        scratch_shapes=[pltpu.SemaphoreType.DMA((n_arr,)), pltpu.SemaphoreType.DMA((n_arr,))],
    )(*bufs)


HBM_SPEC = pl.BlockSpec(memory_space=pltpu.HBM)
SEM_SPEC = pl.BlockSpec(memory_space=pltpu.SEMAPHORE)
DATAFLOW = pltpu.SideEffectType.DATAFLOW_SIDE_EFFECTING


def _in_hbm(a):
    return pltpu.with_memory_space_constraint(a, pltpu.HBM)


def push_start(name, srcs, lands, plan_fn, after):
    n_arr = len(srcs)
    n_send = len(plan_fn(0, 0, 0))
    arrays = list(srcs) + ([] if lands is None else list(lands))
    if after is None:
        after = jnp.zeros((8, LANES), F32)
    n_all = len(arrays)

    def body(*refs):
        src_refs, land_refs = refs[:n_arr], refs[n_all - n_arr:n_all]
        send_sems, recv_sems = refs[n_all + 1:n_all + 3]
        token = refs[-1]
        sends = plan_fn(*_coords())
        for q, (src_ref, land_ref) in enumerate(zip(src_refs, land_refs)):
            for k, (si, peer, di, _) in enumerate(sends):
                pltpu.make_async_remote_copy(src_ref.at[si], land_ref.at[di], send_sems.at[q * n_send + k],
                                             recv_sems.at[q * n_send + k], device_id=peer,
                                             device_id_type=MESH).start()
        token[...] = jnp.zeros_like(token)

    sems = pltpu.SemaphoreType.DMA((n_arr * n_send,))
    out = _pcall(
        body, name=name,
        in_specs=[HBM_SPEC] * n_all + [ANY_SPEC],
        out_specs=[SEM_SPEC, SEM_SPEC] + [HBM_SPEC] * n_all + [pl.BlockSpec(memory_space=pltpu.VMEM)],
        out_shape=[sems, sems] + [pltpu.HBM(a.shape, a.dtype) for a in arrays]
        + [jax.ShapeDtypeStruct((8, LANES), F32)],
        input_output_aliases={q: 2 + q for q in range(n_all)},
        compiler_params=pltpu.CompilerParams(has_side_effects=DATAFLOW),
    )(*[_in_hbm(a) for a in arrays], after)
    return out[:-1], out[-1]


def push_wait(name, state, plan_fn, after, in_place=False, both=False):
    arrays = list(state[2:])
    n_all = len(arrays)
    n_arr = n_all if in_place else n_all // 2
    n_send = len(plan_fn(0, 0, 0))
    send_sems, recv_sems = state[:2]

    def body(*refs):
        src_refs, land_refs = refs[:n_arr], refs[n_all - n_arr:n_all]
        send_ref, recv_ref = refs[n_all:n_all + 2]
        sends = plan_fn(*_coords())
        for q, (src_ref, land_ref) in enumerate(zip(src_refs, land_refs)):
            for k, (si, peer, _, ri) in enumerate(sends):
                cp = pltpu.make_async_remote_copy(src_ref.at[si], land_ref.at[ri], send_ref.at[q * n_send + k],
                                                  recv_ref.at[q * n_send + k], device_id=peer, device_id_type=MESH)
                cp.wait_send()
                cp.wait_recv()

    out = _pcall(
        body, name=name,
        in_specs=[HBM_SPEC] * n_all + [SEM_SPEC, SEM_SPEC, ANY_SPEC],
        out_specs=[HBM_SPEC] * n_all,
        out_shape=[pltpu.HBM(a.shape, a.dtype) for a in arrays],
        input_output_aliases={q: q for q in range(n_all)},
        compiler_params=pltpu.CompilerParams(has_side_effects=DATAFLOW),
    )(*arrays, send_sems, recv_sems, after)
    return list(out) if both else list(out[n_all - n_arr:])


def plan_chips_gather(x, y, c):
    me = 2 * x + y
    return [(c, (px, py, c), 2 * me + c, 2 * (2 * px + py) + c) for px, py in _other_chips(x, y)]


def plan_forward(x, y, c):
    return [(2 * (2 * px + py) + c, (x, y, 1 - c), 2 * (2 * px + py) + c, 2 * (2 * px + py) + (1 - c))
            for px, py in _other_chips(x, y)]


def add_kept(g8, other, name):
    _, rows, cols = other.shape
    tr = _tile(rows, 512)

    def body(c_ref, a_ref, b_ref, o_ref):
        o_ref[...] = (a_ref[...].astype(F32) + b_ref[...].astype(F32)).astype(BF16)

    spec = pl.BlockSpec((None, tr, cols), lambda s, i, c_ref: (s, i, 0))
    grid_spec = pltpu.PrefetchScalarGridSpec(
        num_scalar_prefetch=1, grid=(4, rows // tr),
        in_specs=[pl.BlockSpec((None, tr, cols), lambda s, i, c_ref: (2 * s + c_ref[0], i, 0)), spec],
        out_specs=spec)
    return _pcall(body, name=name, grid_spec=grid_spec, out_shape=jax.ShapeDtypeStruct((4, rows, cols), BF16),
                  compiler_params=_cparams(("parallel", "parallel")))(
        lax.axis_index("c").astype(jnp.int32).reshape(1), g8, other)


def reduce_scatter_begin(grads, names, tag):
    g8 = [g.reshape(8, g.shape[1] // 2, g.shape[2]) for g in grads]
    return _chip_sums_start(g8, push(tag + "_sibling", g8, 4, plan_sibling_quarters), names, tag)


def _chip_sums_start(g8, from_sibling, names, tag):
    chip_sum = [add_kept(a, b, tag + "_add_" + n) for a, b, n in zip(g8, from_sibling, names)]
    return push_start(tag + "_start", chip_sum, chip_sum, plan_chips_by_shard, None)


def sibling_exchange_begin(grads, tag):
    g8 = [g.reshape(8, g.shape[1] // 2, g.shape[2]) for g in grads]
    lands = [lax.empty((4,) + g.shape[1:], g.dtype) for g in g8]
    return push_start(tag + "_sib_start", g8, lands, plan_sibling_quarters, None)


def reduce_scatter_continue(state, names, tag, after):
    n = len(names)
    arrays = push_wait(tag + "_sib_wait", state, plan_sibling_quarters, after, both=True)
    return _chip_sums_start(arrays[:n], arrays[n:], names, tag)


def reduce_scatter_end(state, names, tag, after):
    by_chip = push_wait(tag + "_wait", state, plan_chips_by_shard, after)
    both = swap_halves(tag + "_halves", [sum_slots(q, tag + "_sum_" + n, for_swap=True)
                                         for q, n in zip(by_chip, names)])
    return [b.reshape(2 * b.shape[1], b.shape[2]) for b in both]


def _landing(mine):
    return [jnp.broadcast_to(m[None], (4,) + m.shape).reshape((8,) + m.shape[1:]) for m in mine]


def gather_begin(tag, mine, after, lands=None):
    return push_start(tag + "_start", mine, _landing(mine) if lands is None else lands, plan_chips_gather, after)


def gather_forward(tag, state, after):
    lands = push_wait(tag + "_wait", state, plan_chips_gather, after)
    return push_start(tag + "_fwd_start", lands, None, plan_forward, None)


def gather_end(tag, state, after):
    bufs = push_wait(tag + "_fwd_wait", state, plan_forward, after, in_place=True)
    return [g.reshape(4, 2 * g.shape[1], g.shape[2]) for g in bufs]


PACK_QUANTUM = 8 * C


def _pack(parts, dtype):
    pieces = []
    for p in parts:
        flat = p.astype(dtype).reshape(-1)
        pieces.append(jnp.pad(flat, (0, -flat.shape[0] % PACK_QUANTUM)).reshape(-1, C))
    rows = sum(p.shape[0] for p in pieces)
    if rows % 16:
        pieces.append(jnp.zeros((8, C), dtype))
    return jnp.concatenate(pieces, axis=0)


def _unpack(packed, shapes):
    out, at = [], 0
    for s in shapes:
        n = int(np.prod(s))
        rows = -(-n // PACK_QUANTUM) * 8
        out.append(packed[at:at + rows].reshape(-1)[:n].reshape(s))
        at += rows
    return out


def all_reduce_begin(vec):
    rows, width = vec.shape
    (pair,) = swap_halves("ar_sibling", [jnp.broadcast_to(vec[None], (2, rows, width))])
    chip_sum = sum_slots(pair, "ar_sum_sibling").reshape(2, rows // 2, width)
    mine = lax.dynamic_index_in_dim(chip_sum, lax.axis_index("c"), axis=0, keepdims=True)
    return push_start("ar_start", [chip_sum], [jnp.broadcast_to(mine, (4, rows // 2, width))], plan_chips_half,
                      None)


def all_reduce_end(state, after):
    (by_chip,) = push_wait("ar_wait", state, plan_chips_half, after)
    (both,) = swap_halves("ar_halves", [sum_slots(by_chip, "ar_sum_chips", for_swap=True)])
    return both.reshape(2 * both.shape[1], both.shape[2])


def kernel(x, norm1_g, w_in, shift_mu, w0, w2, a0, a2, g2, k_k, k_a, r_k, gn_w, gn_b, sgu_ln_g, sgu_ln_b, sgu_w, sgu_b, sgu_out_g, w_out, norm2_g, w_gate, w_up, w_down, final_g, loss_target, m_norm1_g, m_w_in, m_shift_mu, m_w0, m_w2, m_a0, m_a2, m_g2, m_k_k, m_k_a, m_r_k, m_gn_w, m_gn_b, m_sgu_ln_g, m_sgu_ln_b, m_sgu_w, m_sgu_b, m_sgu_out_g, m_w_out, m_norm2_g, m_w_gate, m_w_up, m_w_down, m_final_g, v_norm1_g, v_w_in, v_shift_mu, v_w0, v_w2, v_a0, v_a2, v_g2, v_k_k, v_k_a, v_r_k, v_gn_w, v_gn_b, v_sgu_ln_g, v_sgu_ln_b, v_sgu_w, v_sgu_b, v_sgu_out_g, v_w_out, v_norm2_g, v_w_gate, v_w_up, v_w_down, v_final_g):
    args = dict(locals())
    names = ["norm1_g", "w_in", "shift_mu", "w0", "w2", "a0", "a2", "g2", "k_k", "k_a", "r_k", "gn_w", "gn_b",
             "sgu_ln_g", "sgu_ln_b", "sgu_w", "sgu_b", "sgu_out_g", "w_out", "norm2_g", "w_gate", "w_up", "w_down",
             "final_g"]
    big = ["w_in", "w_out", "w_gate", "w_up", "w_down"]
    xi, yi = lax.axis_index("x"), lax.axis_index("y")
    chip = 2 * xi + yi
    x2d, tgt = x[0], loss_target[0]
    t = x2d.shape[0]

    lora_names = ["w2", "a2", "g2"]
    first, later = ["w_in"] + lora_names, ["w_out", "w_gate", "w_up", "w_down"]

    def halves(n, token=None):
        m = args[n][0] if token is None else args[n][0] + token[0, 0]
        m = m.astype(BF16)
        return m.reshape(2, m.shape[0] // 2, m.shape[1])

    first_state, first_token = gather_begin("g0", [halves(n) for n in first], norm1_g)
    later_mine = [halves(n, first_token) for n in later]
    later_lands = _landing(later_mine)
    (h1,) = row_fwd("norm1", rms_fn, [Tiled(x2d)], [norm1_g + first_token[0, 0]], [], [(D, BF16)], _tile(t, 256))
    prepared = functools.reduce(lambda p, q: p + q, [a[0, 0, :1].astype(F32) for a in later_lands])
    first_state, first_token = gather_forward("g0", first_state, prepared + h1[0, :1].astype(F32))
    got = gather_end("g0", first_state, first_token)
    later_state, later_token = gather_begin("g1", later_mine, got[0], later_lands)
    full = {n: blk.transpose(1, 0, 2).reshape(blk.shape[1], 4 * blk.shape[2]) for n, blk in zip(first[1:], got[1:])}
    s0, s1, s2, s3 = got[0]
    w_in_p = jnp.concatenate([s0, s1, s2[:, :CUT_A], s2[:, CUT_B:], s3, s2[:, CUT_A:CUT_B],
                              jnp.zeros((D, P_W - D_IN), BF16)], axis=1)
    mu_rkv = shift_mu[:, :3 * C] + later_token[0, 0]
    mu_lora = jnp.pad(shift_mu[:, 3 * C:], ((0, 0), (0, LORA_W - 288)))
    w2p = jnp.pad(full["w2"], ((0, 64), (0, 0)))
    a2p = jnp.pad(full["a2"], ((64, 0), (0, 0)))
    g2p = jnp.pad(full["g2"], ((0, 96), (0, 0)))
    ii = np.arange(C)
    jm = _scan_consts()[2]
    e_np = (np.arange(LANES)[:, None] == (ii[None, :] // HEAD))
    e_mat, e_t = jnp.asarray(e_np, BF16), jnp.asarray(e_np.T, BF16)
    b_t = jnp.pad(sgu_b[0].T, ((0, 0), (0, LANES - HEADS)))
    r_k_flat = r_k.reshape(1, C)
    tm_row = _tile(t, 256)
    tm_sgu_b = _tile(t, 128)

    p = mm(h1, w_in_p, name="proj_in")
    ps_rkv = shift_fwd(p, mu_rkv, 0, "shift_rkv")
    ps_lora = shift_fwd(p, mu_lora, P_LORA // LORA_W, "shift_lora")
    pre_par = [w0, w2p, a0, a2p, g2p, k_k, k_a]
    r_, w_, k2, v_, an, b_, g_ = row_fwd("pre", pre_fn, [Tiled(ps_rkv), Tiled(ps_lora)], pre_par, [jm],
                                         [(C, F32)] * 7, tm_row)
    v3 = to_cols(v_)
    y3, ckpt = wkv_fwd(r_, w_, k2, v3, an, b_)
    later_state, later_token = gather_forward("g1", later_state, y3)
    y_ = from_cols(y3 + later_token[0, 0])
    post_in = [Tiled(z) for z in (y_, r_, k2, v_, g_)]
    post_par = [gn_w, gn_b, r_k_flat]
    (y_rwkv,) = row_fwd("post", post_fn, post_in, post_par, [jm], [(C, BF16)], tm_row)
    sgu_in = [Tiled(p, C, 3), Tiled(p, C, 4)]
    sgu_par = [sgu_ln_g, sgu_ln_b, sgu_w[0], b_t, sgu_out_g]
    (y_sgu,) = row_fwd("sgu", sgu_fn, sgu_in, sgu_par, [e_mat, e_t], [(C, BF16)], tm_row)
    y_cat = jnp.concatenate([y_rwkv, y_sgu], axis=1)
    for n, blk in zip(later, gather_end("g1", later_state, y_cat)):
        full[n] = blk if n in ("w_gate", "w_up") else blk.reshape(4 * blk.shape[1], blk.shape[2])
    x_mid = mm(y_cat, full["w_out"], res=x2d, name="proj_out")
    (h2,) = row_fwd("norm2", rms_fn, [Tiled(x_mid)], [norm2_g], [], [(D, BF16)], tm_row)
    gate = mm(h2, full["w_gate"], shards="b", name="ffn_gate")
    up, act = mm(h2, full["w_up"], shards="b", name="ffn_up",
                 post=(lambda up_tile, gate_tile: (up_tile, swiglu_fn(gate_tile, up_tile)[0]), [gate], [F32, BF16]))
    x_out = mm(act, full["w_down"], res=x_mid, name="ffn_down")
    loss_part, dx_out, d_final_g, dx_out_b = loss_head(x_out, tgt, final_g.reshape(1, D))

    grads = {"final_g": d_final_g.reshape(D)}
    d_gate, d_up = mm(dx_out_b, full["w_down"], tb=True, name="d_act",
                      post=(lambda d_act, g_tile, u_tile: jax.vjp(swiglu_fn, g_tile, u_tile)[1]((d_act,)), [gate, up],
                            [BF16, BF16]))
    grads["w_down"] = mm(act, dx_out_b, ta=True, out_dtype=BF16, name="d_w_down")
    grads["w_gate"] = mm(h2, d_gate, shards="out", out_dtype=BF16, name="d_w_gate")
    grads["w_up"] = mm(h2, d_up, shards="out", out_dtype=BF16, name="d_w_up")
    d_h2 = mm(d_gate, full["w_gate"], shards="bt", name="d_h2_gate")
    d_h2 = mm(d_up, full["w_up"], shards="bt", res=d_h2, name="d_h2_up")
    dx_mid, grads["norm2_g"], dx_mid_b = row_bwd("norm2_b", rms_res_fn, [Tiled(x_mid)], [norm2_g], [],
                                                 [Tiled(d_h2), Tiled(dx_out)], tm_row, bf16_copy_of=(0,))
    d_ycat = mm(dx_mid_b, full["w_out"], tb=True, name="d_ycat")
    grads["w_out"] = mm(y_cat, dx_mid_b, ta=True, out_dtype=BF16, name="d_w_out")
    rs_state, rs_token = sibling_exchange_begin([grads[n].reshape((4,) + args[n].shape[1:]) for n in later], "rs2")
    d_pu, d_pv, grads["sgu_ln_g"], grads["sgu_ln_b"], d_sgu_w, d_b_t, grads["sgu_out_g"] = row_bwd(
        "sgu_b", sgu_fn, sgu_in, sgu_par, [e_mat, e_t], [Tiled(d_ycat, C, 1)], tm_sgu_b,
        tiled_out_dtypes=[BF16, BF16])
    grads["sgu_w"] = d_sgu_w[None]
    grads["sgu_b"] = d_b_t[:, :HEADS].T[None]
    post_par_b = [gn_w + rs_token[0, 0]] + post_par[1:]
    d_y, d_r1, d_k1, d_v1, d_g, grads["gn_w"], grads["gn_b"], d_r_k = row_bwd(
        "post_b", post_fn, post_in, post_par_b, [jm], [Tiled(d_ycat, C, 0)], tm_row)
    grads["r_k"] = d_r_k.reshape(r_k.shape)
    rs_state, rs_token = reduce_scatter_continue(rs_state, later, "rs2", d_y)
    d_r2, d_w, d_k2, d_a, d_b, d_v3 = wkv_bwd(r_, w_, k2, v3, an, b_, ckpt, to_cols(d_y + rs_token[0, 0]))
    d_v2 = from_cols(d_v3)
    pre_cts = [(Tiled(d_r1), Tiled(d_r2)), Tiled(d_w), (Tiled(d_k1), Tiled(d_k2)), (Tiled(d_v1), Tiled(d_v2)),
               Tiled(d_a), Tiled(d_b), Tiled(d_g)]
    d_ps_rkv, d_ps_lora, grads["w0"], d_w2p, grads["a0"], d_a2p, d_g2p, grads["k_k"], grads["k_a"] = row_bwd(
        "pre_b", pre_fn, [Tiled(ps_rkv), Tiled(ps_lora)], pre_par, [jm], pre_cts, tm_row)
    grads["w2"], grads["a2"], grads["g2"] = d_w2p[None, :64], d_a2p[None, 64:], d_g2p[None, :160]
    d_p_rkv, d_mu_rkv = shift_bwd(d_ps_rkv, p, mu_rkv, 0, "shift_rkv_b")
    d_p_lora, d_mu_lora = shift_bwd(d_ps_lora, p, mu_lora, P_LORA // LORA_W, "shift_lora_b")
    grads["shift_mu"] = jnp.concatenate([d_mu_rkv, d_mu_lora[:, :288]], axis=1)
    d_p = jnp.concatenate([d_p_rkv, d_pu, d_pv, d_p_lora], axis=1)
    d_h1 = mm(d_p, w_in_p, tb=True, name="d_h1")
    d_w_in_p = mm(h1, d_p, ta=True, out_dtype=BF16, name="d_w_in")
    ws = D_IN // 4
    shard2 = jnp.concatenate([d_w_in_p[:, 2 * ws:3 * C], d_w_in_p[:, P_LORA:P_LORA + 288],
                              d_w_in_p[:, 3 * C:3 * C + ws - CUT_B]], axis=1)
    grads["w_in"] = jnp.stack([d_w_in_p[:, :ws], d_w_in_p[:, ws:2 * ws], shard2,
                               d_w_in_p[:, 3 * C + ws - CUT_B:P_LORA]])
    dx, grads["norm1_g"] = row_bwd("norm1_b", rms_res_fn, [Tiled(x2d)], [norm1_g], [],
                                   [Tiled(d_h1), Tiled(dx_mid)], tm_row)

    small = [n for n in names if n not in big]
    small_shapes = [(1, 64, C) if n in ("w2", "a2") else (1, 160, C) if n == "g2" else args[n].shape for n in small]
    vec = _pack([grads[n] for n in small], F32)
    w_in_state, w_in_token = reduce_scatter_begin([grads["w_in"]], ["w_in"], "rs1")
    small_state, small_token = all_reduce_begin(vec)
    total = {n: g[None] for n, g in zip(later, reduce_scatter_end(rs_state, later, "rs2", w_in_token + small_token))}
    loss = lax.psum(loss_part[0, 0], ("x", "y", "c"))
    delta, new_m, new_v = {}, {}, {}

    def adam_big(n):
        s = args[n].shape
        view = (lambda a: a[0].T) if n == "w_in" else (lambda a: a[0])
        back = (lambda a: a.T.reshape(s)) if n == "w_in" else (lambda a: a.reshape(s))
        d_, m_, v__ = adamw(view(args[n]), view(total[n]), view(args["m_" + n]), view(args["v_" + n]), "adamw_" + n)
        delta[n], new_m[n], new_v[n] = back(d_), back(m_), back(v__)

    for n in later:
        adam_big(n)
    total["w_in"] = reduce_scatter_end(w_in_state, ["w_in"], "rs1", new_v["w_down"])[0][None]
    for n, g_sum in zip(small, _unpack(all_reduce_end(small_state, new_v["w_down"]), small_shapes)):
        total[n] = lax.dynamic_slice_in_dim(g_sum, chip * 256, 256, axis=2) if n in lora_names else g_sum
    adam_big("w_in")
    flat = [_pack([src[n] for n in small], F32)
            for src in (args, total, {n: args["m_" + n] for n in small}, {n: args["v_" + n] for n in small})]
    outs = adamw(*flat, "adamw_small")
    for res, o in zip((delta, new_m, new_v), outs):
        res.update(zip(small, _unpack(o, [args[n].shape for n in small])))
    return (loss, dx[None], *[total[n] for n in names], *[delta[n] for n in names],
            *[new_m[n] for n in names], *[new_v[n] for n in names])
```

```python
import functools

import numpy as np
import jax
import jax.numpy as jnp
from jax import lax
from jax.experimental import pallas as pl
from jax.experimental.pallas import tpu as pltpu

F32 = jnp.float32
BF16 = jnp.bfloat16

D = 2048
C = 1024
HEADS = 16
HEAD = 64
LANES = 128
P_W = 5632
P_LORA = 5120
LORA_W = 512
D_IN = 5408
CUT_A = 3 * C - 2 * (D_IN // 4)
CUT_B = CUT_A + 288
CHUNK = 128
RMS_EPS = 1e-6
LN_EPS = 1e-5
GN_EPS = 64e-5
L2_EPS = 1e-12
VMEM_LIMIT = 56 * 1024 * 1024


def _pcall(body, **kw):
    return pl.pallas_call(body, **kw)


def _cparams(sem):
    return pltpu.CompilerParams(dimension_semantics=sem, vmem_limit_bytes=VMEM_LIMIT)


def _tile(n, most):
    t = most
    while t > 8 and n % t:
        t //= 2
    assert n % t == 0, (n, most)
    return t


MM_TILE = (2048, 512, 2048)
MM_VMEM = 40 * 1024 * 1024


def _div_tile(n, most, quantum=LANES):
    for t in range(min(n, most) // quantum * quantum, 0, -quantum):
        if n % t == 0:
            return t
    raise ValueError((n, most, quantum))


def _mm_tiles(m, n, k, a_bytes, b_bytes, r_bytes, o_bytes):
    tm, tn, tk = _div_tile(m, MM_TILE[0]), _div_tile(n, MM_TILE[1]), _div_tile(k, MM_TILE[2])

    def need(tm, tk):
        return 2 * (tm * tk * a_bytes + tk * tn * b_bytes + tm * tn * (r_bytes + o_bytes)) + tm * tn * 4

    while need(tm, tk) > MM_VMEM:
        if tk >= tm and tk > 512:
            tk = _div_tile(k, tk - LANES)
        else:
            tm = _div_tile(m, tm - LANES)
    return tm, tn, tk


def mm(a, b, *, ta=False, tb=False, res=None, out_dtype=F32, name, shards=None, post=None):
    if shards is not None:
        return _mm_shards(a, b, res, out_dtype, name, shards, post)
    m, k = (a.shape[1], a.shape[0]) if ta else a.shape
    n = b.shape[0] if tb else b.shape[1]
    assert (b.shape[1] if tb else b.shape[0]) == k
    if post is None:
        tm, tn, tk = _mm_tiles(m, n, k, a.dtype.itemsize, b.dtype.itemsize,
                               0 if res is None else res.dtype.itemsize, jnp.dtype(out_dtype).itemsize)
    else:
        tm, tn, tk = _div_tile(m, 1024), _div_tile(n, MM_TILE[1]), _div_tile(k, MM_TILE[2])
    nk = k // tk
    dims = (((0 if ta else 1,), (1 if tb else 0,)), ((), ()))
    a_spec = pl.BlockSpec((tk, tm), lambda i, j, l: (l, i)) if ta else pl.BlockSpec((tm, tk), lambda i, j, l: (i, l))
    b_spec = pl.BlockSpec((tn, tk), lambda i, j, l: (j, l)) if tb else pl.BlockSpec((tk, tn), lambda i, j, l: (l, j))
    o_spec = pl.BlockSpec((tm, tn), lambda i, j, l: (i, j))
    return _mm_call(a, b, res, dims, (m // tm, n // tn, nk), a_spec, b_spec, o_spec, o_spec, (tm, tn),
                    jax.ShapeDtypeStruct((m, n), out_dtype), name, post)


def _mm_shards(a, b, res, out_dtype, name, shards, post=None):
    assert post is None or shards == "b"
    if shards == "b":
        (m, k), ns = a.shape, b.shape[2]
        tm, tk = _div_tile(m, 1024 if post is None else 512), _div_tile(k, 2048)
        grid, dims, acc = (m // tm, 4, k // tk), (((1,), (0,)), ((), ())), (tm, ns)
        a_spec = pl.BlockSpec((tm, tk), lambda i, s, l: (i, l))
        b_spec = pl.BlockSpec((None, tk, ns), lambda i, s, l: (s, l, 0))
        o_spec = pl.BlockSpec((tm, ns), lambda i, s, l: (i, s))
        out = jax.ShapeDtypeStruct((m, 4 * ns), out_dtype)
    elif shards == "bt":
        m, (_, n, ns) = a.shape[0], b.shape
        tm, tn = _div_tile(m, 2048), _div_tile(n, 512)
        grid, dims, acc = (m // tm, n // tn, 4), (((1,), (1,)), ((), ())), (tm, tn)
        a_spec = pl.BlockSpec((tm, ns), lambda i, j, s: (i, s))
        b_spec = pl.BlockSpec((None, tn, ns), lambda i, j, s: (s, j, 0))
        o_spec = pl.BlockSpec((tm, tn), lambda i, j, s: (i, j))
        out = jax.ShapeDtypeStruct((m, n), out_dtype)
    else:
        (t, m), ns = a.shape, b.shape[1] // 4
        tm, tk = _div_tile(m, 1024), _div_tile(t, 2048)
        grid, dims, acc = (m // tm, 4, t // tk), (((0,), (0,)), ((), ())), (tm, ns)
        a_spec = pl.BlockSpec((tk, tm), lambda i, s, l: (l, i))
        b_spec = pl.BlockSpec((tk, ns), lambda i, s, l: (l, s))
        o_spec = pl.BlockSpec((None, tm, ns), lambda i, s, l: (s, i, 0))
        out = jax.ShapeDtypeStruct((4, m, ns), out_dtype)
    return _mm_call(a, b, res, dims, grid, a_spec, b_spec, o_spec, o_spec, acc, out, name, post)


def _mm_call(a, b, res, dims, grid, a_spec, b_spec, r_spec, o_spec, acc_shape, out, name, post=None):
    nk = grid[2]
    post_fn, post_in, post_dtypes = (None, [], [out.dtype]) if post is None else post
    n_extra = (res is not None) + len(post_in)

    def body(*refs):
        a_ref, b_ref = refs[:2]
        r_ref = None if res is None else refs[2]
        p_refs = refs[2 + (res is not None):2 + n_extra]
        o_refs = refs[2 + n_extra:2 + n_extra + len(post_dtypes)]

        def finish(acc):
            if r_ref is not None:
                acc = acc + r_ref[...].astype(F32)
            outs = (acc,) if post_fn is None else post_fn(acc, *[p[...] for p in p_refs])
            for o_ref, o in zip(o_refs, outs):
                o_ref[...] = o.astype(o_ref.dtype)

        prod = lax.dot_general(a_ref[...].astype(BF16), b_ref[...].astype(BF16), dims, preferred_element_type=F32)
        if nk == 1:
            finish(prod)
            return
        acc_ref = refs[-1]
        kk = pl.program_id(2)

        @pl.when(kk == 0)
        def _():
            acc_ref[...] = prod

        @pl.when(kk > 0)
        def _():
            acc_ref[...] += prod

        @pl.when(kk == nk - 1)
        def _():
            finish(acc_ref[...])

    in_specs = [a_spec, b_spec] + ([r_spec] if res is not None else []) + [o_spec] * len(post_in)
    args = (a, b) + ((res,) if res is not None else ()) + tuple(post_in)
    outs = _pcall(
        body, name=name, grid=grid, in_specs=in_specs, out_specs=[o_spec] * len(post_dtypes),
        out_shape=[jax.ShapeDtypeStruct(out.shape, dt) for dt in post_dtypes],
        scratch_shapes=[] if nk == 1 else [pltpu.VMEM(acc_shape, F32)],
        compiler_params=_cparams(("parallel", "parallel", "arbitrary")),
    )(*args)
    return outs[0] if post is None else outs


class Tiled:
    def __init__(self, arr, width=None, col=0):
        self.arr, self.width, self.col = arr, (arr.shape[1] if width is None else width), col

    def spec(self, tm):
        col = self.col
        return pl.BlockSpec((tm, self.width), lambda i: (i, col))


def _full_spec(p):
    nd = p.ndim
    return pl.BlockSpec(p.shape, lambda i: (0,) * nd)


def row_fwd(name, fn, tiled, params, consts, outs, tm):
    t = tiled[0].arr.shape[0]
    n_in = len(tiled) + len(params) + len(consts)

    def body(*refs):
        res = fn(*[r[...] for r in refs[:n_in]])
        for o_ref, r in zip(refs[n_in:], res):
            o_ref[...] = r.astype(o_ref.dtype)

    return _pcall(
        body, name=name, grid=(t // tm,),
        in_specs=[x.spec(tm) for x in tiled] + [_full_spec(p) for p in params + consts],
        out_specs=[pl.BlockSpec((tm, w), lambda i: (i, 0)) for w, _ in outs],
        out_shape=[jax.ShapeDtypeStruct((t, w), dt) for w, dt in outs],
        compiler_params=_cparams(("parallel",)),
    )(*[x.arr for x in tiled], *params, *consts)


def row_bwd(name, fn, tiled, params, consts, cts, tm, tiled_out_dtypes=None, bf16_copy_of=()):
    t = tiled[0].arr.shape[0]
    nt, npar, ncon = len(tiled), len(params), len(consts)
    cts = [c if isinstance(c, tuple) else (c,) for c in cts]
    flat_cts = [x for c in cts for x in c]
    tiled_out_dtypes = tiled_out_dtypes or [F32] * nt
    copies = [tiled[j] for j in bf16_copy_of]

    def body(*refs):
        n_in = nt + npar + ncon
        ins = [r[...].astype(F32) for r in refs[:nt + npar]]
        con = [r[...] for r in refs[nt + npar:n_in]]
        ct_refs = list(refs[n_in:n_in + len(flat_cts)])
        o = refs[n_in + len(flat_cts):]
        ct = []
        for c in cts:
            parts = [ct_refs.pop(0)[...].astype(F32) for _ in c]
            ct.append(functools.reduce(lambda p, q: p + q, parts))
        _, vjp = jax.vjp(lambda *a: fn(*a, *con), *ins)
        g = vjp(tuple(ct))
        for j in range(nt):
            o[j][...] = g[j].astype(o[j].dtype)
        for n, j in enumerate(bf16_copy_of):
            o[nt + npar + n][...] = g[j].astype(BF16)
        first = pl.program_id(0) == 0

        @pl.when(first)
        def _():
            for j in range(npar):
                o[nt + j][...] = g[nt + j]

        @pl.when(jnp.logical_not(first))
        def _():
            for j in range(npar):
                o[nt + j][...] += g[nt + j]

    return _pcall(
        body, name=name, grid=(t // tm,),
        in_specs=[x.spec(tm) for x in tiled] + [_full_spec(p) for p in params + consts]
        + [x.spec(tm) for x in flat_cts],
        out_specs=[pl.BlockSpec((tm, x.width), lambda i: (i, 0)) for x in tiled] + [_full_spec(p) for p in params]
        + [pl.BlockSpec((tm, x.width), lambda i: (i, 0)) for x in copies],
        out_shape=[jax.ShapeDtypeStruct((t, x.width), dt) for x, dt in zip(tiled, tiled_out_dtypes)]
        + [jax.ShapeDtypeStruct(p.shape, F32) for p in params]
        + [jax.ShapeDtypeStruct((t, x.width), BF16) for x in copies],
        compiler_params=_cparams(("arbitrary",)),
    )(*[x.arr for x in tiled], *params, *consts, *[x.arr for x in flat_cts])


def _split_dot(x, w):
    hi = x.astype(BF16)
    lo = (x - hi.astype(F32)).astype(BF16)
    return jnp.dot(hi, w, preferred_element_type=F32) + jnp.dot(lo, w, preferred_element_type=F32)


def _headsum_blocks(x, j2k):
    out = []
    for c0 in range(0, x.shape[1], LANES):
        blk = x[:, c0:c0 + LANES]
        hi = blk.astype(BF16)
        lo = (blk - hi.astype(F32)).astype(BF16)
        out.append(jnp.dot(jnp.concatenate([hi, lo], axis=1), j2k, preferred_element_type=F32))
    return jnp.concatenate(out, axis=1)


@jax.custom_vjp
def headsum(x, j2k):
    return _headsum_blocks(x, j2k)


def _headsum_fwd(x, j2k):
    return _headsum_blocks(x, j2k), j2k


def _headsum_bwd(j2k, ct):
    return _headsum_blocks(ct, j2k), jnp.zeros_like(j2k)


headsum.defvjp(_headsum_fwd, _headsum_bwd)


def _bdot(x, w):
    return jnp.dot(x.astype(BF16), w.astype(BF16), preferred_element_type=F32)


def _sigmoid(x):
    return jax.nn.sigmoid(x)


def _softplus(x):
    return jnp.maximum(x, 0.0) + jnp.log(1.0 + jnp.exp(-jnp.abs(x)))


def rms_fn(x, g):
    return (x * lax.rsqrt(jnp.mean(x * x, axis=-1, keepdims=True) + RMS_EPS) * g,)


def rms_res_fn(x, g):
    return rms_fn(x, g)[0], x


def pre_fn(rkv, lora, w0, w2p, a0, a2p, g2p, k_k, k_a, jm):
    r, k, v = rkv[:, :C], rkv[:, C:2 * C], rkv[:, 2 * C:]
    zwa, zg = lora[:, :LANES], lora[:, LANES:LANES + 256]
    w_log = -_softplus(-(w0 + _bdot(jnp.tanh(zwa), w2p))) - 0.5
    decay = jnp.exp(-jnp.exp(w_log))
    a = _sigmoid(a0 + _bdot(zwa, a2p))
    g = _bdot(_sigmoid(zg), g2p)
    kk = k * k_k
    kk = kk / jnp.maximum(jnp.sqrt(headsum(kk * kk, jm)), L2_EPS)
    k2 = k * (1.0 + (a - 1.0) * k_a)
    return r, decay, k2, v, -kk, kk * a, g


def post_fn(y, r, k2, v, g, gn_w, gn_b, r_k, jm):
    mu = headsum(y, jm) * (1.0 / HEAD)
    yc = y - mu
    var = headsum(yc * yc, jm) * (1.0 / HEAD)
    yn = yc * lax.rsqrt(var + GN_EPS) * gn_w + gn_b
    bonus = headsum(r * k2 * r_k, jm) * v
    return ((yn + bonus) * g,)


def _gelu(x):
    return 0.5 * x * (1.0 + lax.erf(x * np.float32(1.0 / np.sqrt(2.0))))


@jax.custom_vjp
def expand_groups(b_t, e, e_t):
    return _split_dot(b_t, e)


def _expand_groups_fwd(b_t, e, e_t):
    return _split_dot(b_t, e), (e, e_t)


def _expand_groups_bwd(res, ct):
    e, e_t = res
    return _split_dot(ct, e_t), jnp.zeros_like(e), jnp.zeros_like(e_t)


expand_groups.defvjp(_expand_groups_fwd, _expand_groups_bwd)


def sgu_fn(pu, pv, ln_g, ln_b, w_s, b_t, out_g, e, e_t):
    rows = pu.shape[0]
    b_exp = expand_groups(b_t, e, e_t)
    u, v = _gelu(pu), _gelu(pv)
    mu = jnp.mean(v, axis=-1, keepdims=True)
    vc = v - mu
    var = jnp.mean(vc * vc, axis=-1, keepdims=True)
    v = vc * lax.rsqrt(var + LN_EPS) * ln_g + ln_b
    tri = lax.broadcasted_iota(jnp.int32, (CHUNK, CHUNK), 0) >= lax.broadcasted_iota(jnp.int32, (CHUNK, CHUNK), 1)
    left = lax.broadcasted_iota(jnp.int32, (CHUNK, LANES), 1) < HEAD
    chunks = []
    for c0 in range(0, rows, CHUNK):
        cols = []
        for gp in range(HEADS // 2):
            vp = v[c0:c0 + CHUNK, gp * LANES:(gp + 1) * LANES]
            wa = jnp.where(tri, w_s[2 * gp], 0.0)
            wb = jnp.where(tri, w_s[2 * gp + 1], 0.0)
            cols.append(jnp.where(left, _bdot(wa, vp), _bdot(wb, vp)))
        chunks.append(jnp.concatenate(cols, axis=1) + b_exp)
    s = jnp.concatenate(chunks, axis=0) if len(chunks) > 1 else chunks[0]
    y = u * s
    return (y * lax.rsqrt(jnp.mean(y * y, axis=-1, keepdims=True) + RMS_EPS) * out_g,)


def swiglu_fn(gate, up):
    return (gate * _sigmoid(gate) * up,)


SCAN_CHUNK = 32
NB = C // LANES
FWD_GROUP = 2
BWD_GROUP = 8


def to_cols(a):
    t = a.shape[0]
    return a.reshape(t // 8, LANES, HEAD).transpose(0, 2, 1)


def from_cols(a):
    return a.transpose(0, 2, 1).reshape(a.shape[0] * 8, C)


def _scan_consts():
    i = np.arange(LANES)
    spread = (i[:, None] % 2) == (i[None, :] // HEAD)
    pick = (i[:, None] // HEAD) == (i[None, :] % 2)
    j2 = (i[:, None] // HEAD) == (i[None, :] // HEAD)
    sel = (i[None, :] // 2) == (np.arange(8 * NB * HEAD)[:, None] // HEAD)
    return (jnp.asarray(spread, BF16), jnp.asarray(pick, BF16), jnp.asarray(np.concatenate([j2, j2], 0), BF16),
            jnp.asarray(sel, BF16))


def _stack(blocks):
    return jnp.concatenate(blocks, axis=0) if len(blocks) > 1 else blocks[0]


def _unstack(x, n):
    return [x[i * HEAD:(i + 1) * HEAD] for i in range(n)]


def _headsums(blocks, j2k, group):
    res = []
    for g0 in range(0, len(blocks), group):
        x = _stack(blocks[g0:g0 + group])
        hi = x.astype(BF16)
        lo = (x - hi.astype(F32)).astype(BF16)
        out = jnp.dot(jnp.concatenate([hi, lo], axis=1), j2k, preferred_element_type=F32)
        res += _unstack(out, len(blocks[g0:g0 + group]))
    return res


def _headsums_out(blocks, pick):
    return _unstack(jnp.dot(_stack(blocks).astype(BF16), pick, preferred_element_type=F32), len(blocks))


def _expand8(tile, sel, spread):
    lhs = jnp.tile(tile.astype(BF16), (8 * NB, 1)) * sel
    return _unstack(jnp.dot(lhs, spread, preferred_element_type=F32), 8 * NB)


def _collapse(tile, blocks, first, lane_pair):
    for n, blk in enumerate(blocks):
        tile = jnp.where(jnp.tile(lane_pair == first + n, (HEAD // 8, 1)), blk, tile)
    return tile


def _row(tile, j, cb):
    return jnp.broadcast_to(tile[j:j + 1, cb * LANES:(cb + 1) * LANES], (HEAD, LANES))


def _scan_step(s_ref, rows, vexp, j2k, j, hist=None):
    w_t, k_t, a_t, b_t, r_t = rows
    s = [s_ref[:, cb * LANES:(cb + 1) * LANES] for cb in range(NB)]
    sab = _headsums([s[cb] * _row(a_t, j, cb) for cb in range(NB)], j2k, FWD_GROUP if hist is None else BWD_GROUP)
    out = []
    for cb in range(NB):
        sl = slice(cb * LANES, (cb + 1) * LANES)
        s_new = s[cb] * _row(w_t, j, cb) + sab[cb] * _row(b_t, j, cb) + vexp[j * NB + cb] * _row(k_t, j, cb)
        s_ref[:, sl] = s_new
        if hist is None:
            out.append(s_new * _row(r_t, j, cb))
        else:
            s_hist, sab_hist, idx = hist
            s_hist[idx + 1, :, sl] = s_new
            sab_hist[idx, :, sl] = sab[cb]
    return out


def wkv_fwd(r, w, k, v3, a, b):
    t = r.shape[0]
    nc = t // SCAN_CHUNK
    n8 = SCAN_CHUNK // 8

    def body(r_ref, w_ref, k_ref, a_ref, b_ref, v3_ref, spread_ref, pick_ref, j2k_ref, sel_ref, y3_ref, ck_ref,
             s_ref):
        @pl.when(pl.program_id(0) == 0)
        def _():
            s_ref[...] = jnp.zeros_like(s_ref)

        ck_ref[0] = s_ref[...]
        spread, pick, j2k, sel = spread_ref[...], pick_ref[...], j2k_ref[...], sel_ref[...]
        lane_pair = lax.broadcasted_iota(jnp.int32, (8, LANES), 1) // 2

        def t8_body(t8, carry):
            row0 = pl.multiple_of(t8 * 8, 8)
            rows = [ref[pl.ds(row0, 8), :] for ref in (w_ref, k_ref, a_ref, b_ref, r_ref)]
            vexp = _expand8(v3_ref[t8], sel, spread)
            y3 = jnp.zeros((HEAD, LANES), F32)
            for j in range(8):
                y3 = _collapse(y3, _headsums_out(_scan_step(s_ref, rows, vexp, j2k, j), pick), j * NB, lane_pair)
            y3_ref[t8] = y3
            return carry

        lax.fori_loop(0, n8, t8_body, 0)

    row_spec = pl.BlockSpec((SCAN_CHUNK, C), lambda i: (i, 0))
    col_spec = pl.BlockSpec((n8, HEAD, LANES), lambda i: (i, 0, 0))
    consts = _scan_consts()
    return _pcall(
        body, name="wkv_fwd", grid=(nc,),
        in_specs=[row_spec] * 5 + [col_spec] + [pl.BlockSpec(c.shape, lambda i: (0, 0)) for c in consts],
        out_specs=[col_spec, pl.BlockSpec((1, HEAD, C), lambda i: (i, 0, 0))],
        out_shape=[jax.ShapeDtypeStruct((t // 8, HEAD, LANES), F32), jax.ShapeDtypeStruct((nc, HEAD, C), F32)],
        scratch_shapes=[pltpu.VMEM((HEAD, C), F32)],
        compiler_params=_cparams(("arbitrary",)),
    )(r, w, k, a, b, v3, *consts)


def wkv_bwd(r, w, k, v3, a, b, ckpt, dy3):
    t = r.shape[0]
    nc = t // SCAN_CHUNK
    n8 = SCAN_CHUNK // 8

    def body(r_ref, w_ref, k_ref, a_ref, b_ref, v3_ref, dy3_ref, ck_ref, spread_ref, pick_ref, j2k_ref, sel_ref,
             dr_ref, dw_ref, dk_ref, da_ref, db_ref, dv3_ref, s_ref, g_ref, s_hist, sab_hist):
        @pl.when(pl.program_id(0) == 0)
        def _():
            g_ref[...] = jnp.zeros_like(g_ref)

        spread, pick, j2k, sel = spread_ref[...], pick_ref[...], j2k_ref[...], sel_ref[...]
        lane_pair = lax.broadcasted_iota(jnp.int32, (8, LANES), 1) // 2
        sub = lax.broadcasted_iota(jnp.int32, (8, LANES), 0)
        s_ref[...] = ck_ref[0]
        s_hist[0] = ck_ref[0]

        def redo(t8, carry):
            row0 = pl.multiple_of(t8 * 8, 8)
            rows = [ref[pl.ds(row0, 8), :] for ref in (w_ref, k_ref, a_ref, b_ref, r_ref)]
            vexp = _expand8(v3_ref[t8], sel, spread)
            for j in range(8):
                _scan_step(s_ref, rows, vexp, j2k, j, hist=(s_hist, sab_hist, t8 * 8 + j))
            return carry

        lax.fori_loop(0, n8, redo, 0)

        def back(q, carry):
            t8 = n8 - 1 - q
            row0 = pl.multiple_of(t8 * 8, 8)
            w_t, k_t, a_t, b_t, r_t = [ref[pl.ds(row0, 8), :] for ref in (w_ref, k_ref, a_ref, b_ref, r_ref)]
            vexp = _expand8(v3_ref[t8], sel, spread)
            dyexp = _expand8(dy3_ref[t8], sel, spread)
            dv3 = jnp.zeros((HEAD, LANES), F32)
            tiles = [[jnp.zeros((8, LANES), F32) for _ in range(NB)] for _ in range(5)]
            for j in range(7, -1, -1):
                idx = t8 * 8 + j
                g = [g_ref[:, cb * LANES:(cb + 1) * LANES] + dyexp[j * NB + cb] * _row(r_t, j, cb)
                     for cb in range(NB)]
                dsab = _headsums([g[cb] * _row(b_t, j, cb) for cb in range(NB)], j2k, BWD_GROUP)
                dv3 = _collapse(dv3, _headsums_out([g[cb] * _row(k_t, j, cb) for cb in range(NB)], pick), j * NB,
                                lane_pair)
                for cb in range(NB):
                    sl = slice(cb * LANES, (cb + 1) * LANES)
                    s_new = s_hist[idx + 1, :, sl]
                    s_old = s_hist[idx, :, sl]
                    sab = sab_hist[idx, :, sl]
                    sums = (s_new * dyexp[j * NB + cb], g[cb] * s_old, g[cb] * vexp[j * NB + cb],
                            s_old * dsab[cb], g[cb] * sab)
                    for n, prod in enumerate(sums):
                        rowsum = jnp.broadcast_to(jnp.sum(prod, axis=0, keepdims=True), (8, LANES))
                        tiles[n][cb] = jnp.where(sub == j, rowsum, tiles[n][cb])
                    g_ref[:, sl] = g[cb] * _row(w_t, j, cb) + dsab[cb] * _row(a_t, j, cb)
            dv3_ref[t8] = dv3
            for n, ref in enumerate((dr_ref, dw_ref, dk_ref, da_ref, db_ref)):
                for cb in range(NB):
                    ref[pl.ds(row0, 8), cb * LANES:(cb + 1) * LANES] = tiles[n][cb]
            return carry

        lax.fori_loop(0, n8, back, 0)

    row_spec = pl.BlockSpec((SCAN_CHUNK, C), lambda i: (nc - 1 - i, 0))
    col_spec = pl.BlockSpec((n8, HEAD, LANES), lambda i: (nc - 1 - i, 0, 0))
    consts = _scan_consts()
    return _pcall(
        body, name="wkv_bwd", grid=(nc,),
        in_specs=[row_spec] * 5 + [col_spec, col_spec, pl.BlockSpec((1, HEAD, C), lambda i: (nc - 1 - i, 0, 0))]
        + [pl.BlockSpec(c.shape, lambda i: (0, 0)) for c in consts],
        out_specs=[row_spec] * 5 + [col_spec],
        out_shape=[jax.ShapeDtypeStruct((t, C), F32)] * 5 + [jax.ShapeDtypeStruct((t // 8, HEAD, LANES), F32)],
        scratch_shapes=[pltpu.VMEM((HEAD, C), F32), pltpu.VMEM((HEAD, C), F32),
                        pltpu.VMEM((SCAN_CHUNK + 1, HEAD, C), F32), pltpu.VMEM((SCAN_CHUNK, HEAD, C), F32)],
        compiler_params=_cparams(("arbitrary",)),
    )(r, w, k, a, b, v3, dy3, ckpt, *consts)


def _prev_rows(cur, before, first_tile):
    last = jnp.where(first_tile, 0.0, before[7:8, :])
    row = lax.broadcasted_iota(jnp.int32, cur.shape, 0)
    return jnp.where(row == 0, last, pltpu.roll(cur, 1, 0))


def shift_fwd(p, mu, col, name):
    t, width = p.shape[0], mu.shape[1]
    tm = _tile(t, 256)

    def body(p_ref, before_ref, mu_ref, o_ref):
        cur = p_ref[...]
        prev = _prev_rows(cur, before_ref[...], pl.program_id(0) == 0)
        o_ref[...] = cur + (prev - cur) * mu_ref[...]

    return _pcall(
        body, name=name, grid=(t // tm,),
        in_specs=[pl.BlockSpec((tm, width), lambda i: (i, col)),
                  pl.BlockSpec((8, width), lambda i: (jnp.maximum(i * (tm // 8) - 1, 0), col)),
                  pl.BlockSpec((1, width), lambda i: (0, 0))],
        out_specs=pl.BlockSpec((tm, width), lambda i: (i, 0)),
        out_shape=jax.ShapeDtypeStruct((t, width), F32),
        compiler_params=_cparams(("parallel",)),
    )(p, p, mu)


def shift_bwd(dps, p, mu, col, name):
    t, width = dps.shape
    tm = _tile(t, 256)
    nt = t // tm

    def body(d_ref, after_ref, p_ref, before_ref, mu_ref, dp_ref, dmu_ref):
        i = pl.program_id(0)
        d, cur, mu_v = d_ref[...], p_ref[...], mu_ref[...]
        row = lax.broadcasted_iota(jnp.int32, d.shape, 0)
        first_after = jnp.where(i == nt - 1, 0.0, after_ref[0:1, :])
        nxt = jnp.where(row == tm - 1, first_after, pltpu.roll(d, tm - 1, 0))
        dp_ref[...] = (d * (1.0 - mu_v) + nxt * mu_v).astype(BF16)
        prev = _prev_rows(cur, before_ref[...], i == 0)
        part = jnp.sum(d * (prev - cur), axis=0, keepdims=True)

        @pl.when(i == 0)
        def _():
            dmu_ref[...] = part

        @pl.when(i > 0)
        def _():
            dmu_ref[...] += part

    return _pcall(
        body, name=name, grid=(nt,),
        in_specs=[pl.BlockSpec((tm, width), lambda i: (i, 0)),
                  pl.BlockSpec((8, width), lambda i: (jnp.minimum((i + 1) * (tm // 8), t // 8 - 1), 0)),
                  pl.BlockSpec((tm, width), lambda i: (i, col)),
                  pl.BlockSpec((8, width), lambda i: (jnp.maximum(i * (tm // 8) - 1, 0), col)),
                  pl.BlockSpec((1, width), lambda i: (0, 0))],
        out_specs=[pl.BlockSpec((tm, width), lambda i: (i, 0)), pl.BlockSpec((1, width), lambda i: (0, 0))],
        out_shape=[jax.ShapeDtypeStruct((t, width), BF16), jax.ShapeDtypeStruct((1, width), F32)],
        compiler_params=_cparams(("arbitrary",)),
    )(dps, dps, p, p, mu)


def loss_head(x3, tgt, g):
    t, d = x3.shape
    tm = _tile(t, 256)

    def body(x_ref, t_ref, g_ref, loss_ref, dx_ref, dg_ref, dxb_ref):
        (y,), vjp = jax.vjp(lambda a, b: rms_fn(a, b), x_ref[...], g_ref[...])
        diff = y - t_ref[...]
        part = 0.5 * jnp.sum(jnp.mean(diff * diff, axis=-1, keepdims=True), axis=0, keepdims=True)
        dx, dg = vjp((diff * (1.0 / d),))
        dx_ref[...] = dx
        dxb_ref[...] = dx.astype(BF16)
        part = jnp.broadcast_to(part, (1, LANES))
        first = pl.program_id(0) == 0

        @pl.when(first)
        def _():
            loss_ref[...] = part
            dg_ref[...] = dg

        @pl.when(jnp.logical_not(first))
        def _():
            loss_ref[...] += part
            dg_ref[...] += dg

    row = pl.BlockSpec((tm, d), lambda i: (i, 0))
    vec = pl.BlockSpec((1, d), lambda i: (0, 0))
    return _pcall(
        body, name="loss_head", grid=(t // tm,), in_specs=[row, row, vec],
        out_specs=[pl.BlockSpec((1, LANES), lambda i: (0, 0)), row, vec, row],
        out_shape=[jax.ShapeDtypeStruct((1, LANES), F32), jax.ShapeDtypeStruct((t, d), F32),
                   jax.ShapeDtypeStruct((1, d), F32), jax.ShapeDtypeStruct((t, d), BF16)],
        compiler_params=_cparams(("arbitrary",)),
    )(x3, tgt, g)


ADAM_LR, ADAM_B1, ADAM_B2, ADAM_EPS, ADAM_WD, ADAM_STEP = 0.001, 0.9, 0.999, 1e-08, 0.01, 10


def adamw(w, g, m, v, name):
    rows, width = w.shape
    tr = _div_tile(rows, 256, 8)

    def body(w_ref, g_ref, m_ref, v_ref, d_ref, nm_ref, nv_ref):
        gv = g_ref[...]
        m_new = ADAM_B1 * m_ref[...] + (1.0 - ADAM_B1) * gv
        v_new = ADAM_B2 * v_ref[...] + (1.0 - ADAM_B2) * (gv * gv)
        m_hat = m_new / (1.0 - ADAM_B1 ** ADAM_STEP)
        v_hat = v_new / (1.0 - ADAM_B2 ** ADAM_STEP)
        d_ref[...] = -ADAM_LR * (m_hat / (jnp.sqrt(v_hat) + ADAM_EPS) + ADAM_WD * w_ref[...])
        nm_ref[...] = m_new
        nv_ref[...] = v_new

    spec = pl.BlockSpec((tr, width), lambda i: (i, 0))
    return _pcall(
        body, name=name, grid=(rows // tr,), in_specs=[spec] * 4, out_specs=[spec] * 3,
        out_shape=[jax.ShapeDtypeStruct((rows, width), F32)] * 3,
        compiler_params=_cparams(("parallel",)),
    )(w, g, m, v)


def sum_slots(q, name, for_swap=False):
    n, rows, width = q.shape
    tr = _tile(rows, 512)

    def body(*refs):
        refs = refs[for_swap:]
        acc = refs[0][...].astype(F32)
        for r in refs[1:n]:
            acc = acc + r[...].astype(F32)
        refs[n][...] = acc

    if not for_swap:
        specs = [pl.BlockSpec((None, tr, width), functools.partial(lambda s, i: (s, i, 0), s)) for s in range(n)]
        return _pcall(body, name=name, grid=(rows // tr,), in_specs=specs,
                      out_specs=pl.BlockSpec((tr, width), lambda i: (i, 0)),
                      out_shape=jax.ShapeDtypeStruct((rows, width), F32),
                      compiler_params=_cparams(("parallel",)))(*([q] * n))
    specs = [pl.BlockSpec((None, tr, width), functools.partial(lambda s, i, c_ref: (s, i, 0), s)) for s in range(n)]
    grid_spec = pltpu.PrefetchScalarGridSpec(
        num_scalar_prefetch=1, grid=(rows // tr,), in_specs=specs,
        out_specs=pl.BlockSpec((None, tr, width), lambda i, c_ref: (c_ref[0], i, 0)))
    return _pcall(body, name=name, grid_spec=grid_spec, out_shape=jax.ShapeDtypeStruct((2, rows, width), F32),
                  compiler_params=_cparams(("parallel",)))(
        lax.axis_index("c").astype(jnp.int32).reshape(1), *([q] * n))


MESH = pl.DeviceIdType.MESH
ANY_SPEC = pl.BlockSpec(memory_space=pl.ANY)


def _coords():
    return lax.axis_index("x"), lax.axis_index("y"), lax.axis_index("c")


def _other_chips(x, y):
    return [(1 - x, y), (x, 1 - y), (1 - x, 1 - y)]


def push(name, srcs, n_dst, plan_fn):
    n_arr = len(srcs)
    n_send = len(plan_fn(0, 0, 0))

    def body(*refs):
        src_refs, dst_refs = refs[:n_arr], refs[n_arr:2 * n_arr]
        send_sems, recv_sems = refs[2 * n_arr:]
        sends = plan_fn(*_coords())
        out = []
        for q, (src_ref, dst_ref) in enumerate(zip(src_refs, dst_refs)):
            out += [pltpu.make_async_remote_copy(src_ref.at[si], dst_ref.at[di], send_sems.at[q * n_send + k],
                                                 recv_sems.at[q * n_send + k], device_id=peer, device_id_type=MESH)
                    for k, (si, peer, di, _) in enumerate(sends)]
        for cp in out:
            cp.start()
        for q, (src_ref, dst_ref) in enumerate(zip(src_refs, dst_refs)):
            for k, (si, peer, _, ri) in enumerate(sends):
                pltpu.make_async_remote_copy(src_ref.at[si], dst_ref.at[ri], send_sems.at[q * n_send + k],
                                             recv_sems.at[q * n_send + k], device_id=peer,
                                             device_id_type=MESH).wait_recv()
        for cp in out:
            cp.wait_send()

    return _pcall(
        body, name=name, in_specs=[ANY_SPEC] * n_arr, out_specs=[ANY_SPEC] * n_arr,
        out_shape=[jax.ShapeDtypeStruct((n_dst,) + s.shape[1:], s.dtype) for s in srcs],
        scratch_shapes=[pltpu.SemaphoreType.DMA((n_arr * n_send,)), pltpu.SemaphoreType.DMA((n_arr * n_send,))],
    )(*srcs)


def plan_sibling_quarters(x, y, c):
    return [(2 * s + (1 - c), (x, y, 1 - c), s, s) for s in range(4)]


def plan_chips_by_shard(x, y, c):
    me = 2 * x + y
    return [(2 * px + py, (px, py, c), me, 2 * px + py) for px, py in _other_chips(x, y)]


def plan_chips_half(x, y, c):
    me = 2 * x + y
    return [(c, (px, py, c), me, 2 * px + py) for px, py in _other_chips(x, y)]


def swap_halves(name, bufs):
    n_arr = len(bufs)

    def body(*refs):
        buf_refs = refs[n_arr:2 * n_arr]
        send_sems, recv_sems = refs[2 * n_arr:]
        x, y, c = _coords()
        out = [pltpu.make_async_remote_copy(b.at[c], b.at[c], send_sems.at[q], recv_sems.at[q],
                                            device_id=(x, y, 1 - c), device_id_type=MESH)
               for q, b in enumerate(buf_refs)]
        for cp in out:
            cp.start()
        for q, b in enumerate(buf_refs):
            pltpu.make_async_remote_copy(b.at[1 - c], b.at[1 - c], send_sems.at[q], recv_sems.at[q],
                                         device_id=(x, y, 1 - c), device_id_type=MESH).wait_recv()
        for cp in out:
            cp.wait_send()

    return _pcall(
        body, name=name, in_specs=[ANY_SPEC] * n_arr, out_specs=[ANY_SPEC] * n_arr,
        out_shape=[jax.ShapeDtypeStruct(b.shape, b.dtype) for b in bufs],
        input_output_aliases={q: q for q in range(n_arr)},
        scratch_shapes=[pltpu.SemaphoreType.DMA((n_arr,)), pltpu.SemaphoreType.DMA((n_arr,))],
    )(*bufs)


HBM_SPEC = pl.BlockSpec(memory_space=pltpu.HBM)
SEM_SPEC = pl.BlockSpec(memory_space=pltpu.SEMAPHORE)
DATAFLOW = pltpu.SideEffectType.DATAFLOW_SIDE_EFFECTING


def _in_hbm(a):
    return pltpu.with_memory_space_constraint(a, pltpu.HBM)


def push_start(name, srcs, lands, plan_fn, after):
    n_arr = len(srcs)
    n_send = len(plan_fn(0, 0, 0))
    arrays = list(srcs) + ([] if lands is None else list(lands))
    if after is None:
        after = jnp.zeros((8, LANES), F32)
    n_all = len(arrays)

    def body(*refs):
        src_refs, land_refs = refs[:n_arr], refs[n_all - n_arr:n_all]
        send_sems, recv_sems = refs[n_all + 1:n_all + 3]
        token = refs[-1]
        sends = plan_fn(*_coords())
        for q, (src_ref, land_ref) in enumerate(zip(src_refs, land_refs)):
            for k, (si, peer, di, _) in enumerate(sends):
                pltpu.make_async_remote_copy(src_ref.at[si], land_ref.at[di], send_sems.at[q * n_send + k],
                                             recv_sems.at[q * n_send + k], device_id=peer,
                                             device_id_type=MESH).start()
        token[...] = jnp.zeros_like(token)

    sems = pltpu.SemaphoreType.DMA((n_arr * n_send,))
    out = _pcall(
        body, name=name,
        in_specs=[HBM_SPEC] * n_all + [ANY_SPEC],
        out_specs=[SEM_SPEC, SEM_SPEC] + [HBM_SPEC] * n_all + [pl.BlockSpec(memory_space=pltpu.VMEM)],
        out_shape=[sems, sems] + [pltpu.HBM(a.shape, a.dtype) for a in arrays]
        + [jax.ShapeDtypeStruct((8, LANES), F32)],
        input_output_aliases={q: 2 + q for q in range(n_all)},
        compiler_params=pltpu.CompilerParams(has_side_effects=DATAFLOW),
    )(*[_in_hbm(a) for a in arrays], after)
    return out[:-1], out[-1]


def push_wait(name, state, plan_fn, after, in_place=False, both=False):
    arrays = list(state[2:])
    n_all = len(arrays)
    n_arr = n_all if in_place else n_all // 2
    n_send = len(plan_fn(0, 0, 0))
    send_sems, recv_sems = state[:2]

    def body(*refs):
        src_refs, land_refs = refs[:n_arr], refs[n_all - n_arr:n_all]
        send_ref, recv_ref = refs[n_all:n_all + 2]
        sends = plan_fn(*_coords())
        for q, (src_ref, land_ref) in enumerate(zip(src_refs, land_refs)):
            for k, (si, peer, _, ri) in enumerate(sends):
                cp = pltpu.make_async_remote_copy(src_ref.at[si], land_ref.at[ri], send_ref.at[q * n_send + k],
                                                  recv_ref.at[q * n_send + k], device_id=peer, device_id_type=MESH)
                cp.wait_send()
                cp.wait_recv()

    out = _pcall(
        body, name=name,
        in_specs=[HBM_SPEC] * n_all + [SEM_SPEC, SEM_SPEC, ANY_SPEC],
        out_specs=[HBM_SPEC] * n_all,
        out_shape=[pltpu.HBM(a.shape, a.dtype) for a in arrays],
        input_output_aliases={q: q for q in range(n_all)},
        compiler_params=pltpu.CompilerParams(has_side_effects=DATAFLOW),
    )(*arrays, send_sems, recv_sems, after)
    return list(out) if both else list(out[n_all - n_arr:])


def plan_chips_gather(x, y, c):
    me = 2 * x + y
    return [(c, (px, py, c), 2 * me + c, 2 * (2 * px + py) + c) for px, py in _other_chips(x, y)]


def plan_forward(x, y, c):
    return [(2 * (2 * px + py) + c, (x, y, 1 - c), 2 * (2 * px + py) + c, 2 * (2 * px + py) + (1 - c))
            for px, py in _other_chips(x, y)]


def add_kept(g8, other, name):
    _, rows, cols = other.shape
    tr = _tile(rows, 512)

    def body(c_ref, a_ref, b_ref, o_ref):
        o_ref[...] = (a_ref[...].astype(F32) + b_ref[...].astype(F32)).astype(BF16)

    spec = pl.BlockSpec((None, tr, cols), lambda s, i, c_ref: (s, i, 0))
    grid_spec = pltpu.PrefetchScalarGridSpec(
        num_scalar_prefetch=1, grid=(4, rows // tr),
        in_specs=[pl.BlockSpec((None, tr, cols), lambda s, i, c_ref: (2 * s + c_ref[0], i, 0)), spec],
        out_specs=spec)
    return _pcall(body, name=name, grid_spec=grid_spec, out_shape=jax.ShapeDtypeStruct((4, rows, cols), BF16),
                  compiler_params=_cparams(("parallel", "parallel")))(
        lax.axis_index("c").astype(jnp.int32).reshape(1), g8, other)


def reduce_scatter_begin(grads, names, tag):
    g8 = [g.reshape(8, g.shape[1] // 2, g.shape[2]) for g in grads]
    return _chip_sums_start(g8, push(tag + "_sibling", g8, 4, plan_sibling_quarters), names, tag)


def _chip_sums_start(g8, from_sibling, names, tag):
    chip_sum = [add_kept(a, b, tag + "_add_" + n) for a, b, n in zip(g8, from_sibling, names)]
    return push_start(tag + "_start", chip_sum, chip_sum, plan_chips_by_shard, None)


def sibling_exchange_begin(grads, tag):
    g8 = [g.reshape(8, g.shape[1] // 2, g.shape[2]) for g in grads]
    lands = [lax.empty((4,) + g.shape[1:], g.dtype) for g in g8]
    return push_start(tag + "_sib_start", g8, lands, plan_sibling_quarters, None)


def reduce_scatter_continue(state, names, tag, after):
    n = len(names)
    arrays = push_wait(tag + "_sib_wait", state, plan_sibling_quarters, after, both=True)
    return _chip_sums_start(arrays[:n], arrays[n:], names, tag)


def reduce_scatter_end(state, names, tag, after):
    by_chip = push_wait(tag + "_wait", state, plan_chips_by_shard, after)
    both = swap_halves(tag + "_halves", [sum_slots(q, tag + "_sum_" + n, for_swap=True)
                                         for q, n in zip(by_chip, names)])
    return [b.reshape(2 * b.shape[1], b.shape[2]) for b in both]


def _landing(mine):
    return [jnp.broadcast_to(m[None], (4,) + m.shape).reshape((8,) + m.shape[1:]) for m in mine]


def gather_begin(tag, mine, after, lands=None):
    return push_start(tag + "_start", mine, _landing(mine) if lands is None else lands, plan_chips_gather, after)


def gather_forward(tag, state, after):
    lands = push_wait(tag + "_wait", state, plan_chips_gather, after)
    return push_start(tag + "_fwd_start", lands, None, plan_forward, None)


def gather_end(tag, state, after):
    bufs = push_wait(tag + "_fwd_wait", state, plan_forward, after, in_place=True)
    return [g.reshape(4, 2 * g.shape[1], g.shape[2]) for g in bufs]


PACK_QUANTUM = 8 * C


def _pack(parts, dtype):
    pieces = []
    for p in parts:
        flat = p.astype(dtype).reshape(-1)
        pieces.append(jnp.pad(flat, (0, -flat.shape[0] % PACK_QUANTUM)).reshape(-1, C))
    rows = sum(p.shape[0] for p in pieces)
    if rows % 16:
        pieces.append(jnp.zeros((8, C), dtype))
    return jnp.concatenate(pieces, axis=0)


def _unpack(packed, shapes):
    out, at = [], 0
    for s in shapes:
        n = int(np.prod(s))
        rows = -(-n // PACK_QUANTUM) * 8
        out.append(packed[at:at + rows].reshape(-1)[:n].reshape(s))
        at += rows
    return out


def all_reduce_begin(vec):
    rows, width = vec.shape
    (pair,) = swap_halves("ar_sibling", [jnp.broadcast_to(vec[None], (2, rows, width))])
    chip_sum = sum_slots(pair, "ar_sum_sibling").reshape(2, rows // 2, width)
    mine = lax.dynamic_index_in_dim(chip_sum, lax.axis_index("c"), axis=0, keepdims=True)
    return push_start("ar_start", [chip_sum], [jnp.broadcast_to(mine, (4, rows // 2, width))], plan_chips_half,
                      None)


def all_reduce_end(state, after):
    (by_chip,) = push_wait("ar_wait", state, plan_chips_half, after)
    (both,) = swap_halves("ar_halves", [sum_slots(by_chip, "ar_sum_chips", for_swap=True)])
    return both.reshape(2 * both.shape[1], both.shape[2])


def kernel(x, norm1_g, w_in, shift_mu, w0, w2, a0, a2, g2, k_k, k_a, r_k, gn_w, gn_b, sgu_ln_g, sgu_ln_b, sgu_w, sgu_b, sgu_out_g, w_out, norm2_g, w_gate, w_up, w_down, final_g, loss_target, m_norm1_g, m_w_in, m_shift_mu, m_w0, m_w2, m_a0, m_a2, m_g2, m_k_k, m_k_a, m_r_k, m_gn_w, m_gn_b, m_sgu_ln_g, m_sgu_ln_b, m_sgu_w, m_sgu_b, m_sgu_out_g, m_w_out, m_norm2_g, m_w_gate, m_w_up, m_w_down, m_final_g, v_norm1_g, v_w_in, v_shift_mu, v_w0, v_w2, v_a0, v_a2, v_g2, v_k_k, v_k_a, v_r_k, v_gn_w, v_gn_b, v_sgu_ln_g, v_sgu_ln_b, v_sgu_w, v_sgu_b, v_sgu_out_g, v_w_out, v_norm2_g, v_w_gate, v_w_up, v_w_down, v_final_g):
    args = dict(locals())
    names = ["norm1_g", "w_in", "shift_mu", "w0", "w2", "a0", "a2", "g2", "k_k", "k_a", "r_k", "gn_w", "gn_b",
             "sgu_ln_g", "sgu_ln_b", "sgu_w", "sgu_b", "sgu_out_g", "w_out", "norm2_g", "w_gate", "w_up", "w_down",
             "final_g"]
    big = ["w_in", "w_out", "w_gate", "w_up", "w_down"]
    xi, yi = lax.axis_index("x"), lax.axis_index("y")
    chip = 2 * xi + yi
    x2d, tgt = x[0], loss_target[0]
    t = x2d.shape[0]

    lora_names = ["w2", "a2", "g2"]
    first, later = ["w_in"] + lora_names, ["w_out", "w_gate", "w_up", "w_down"]

    def halves(n, token=None):
        m = args[n][0] if token is None else args[n][0] + token[0, 0]
        m = m.astype(BF16)
        return m.reshape(2, m.shape[0] // 2, m.shape[1])

    first_state, first_token = gather_begin("g0", [halves(n) for n in first], norm1_g)
    later_mine = [halves(n, first_token) for n in later]
    later_lands = _landing(later_mine)
    (h1,) = row_fwd("norm1", rms_fn, [Tiled(x2d)], [norm1_g + first_token[0, 0]], [], [(D, BF16)], _tile(t, 256))
    prepared = functools.reduce(lambda p, q: p + q, [a[0, 0, :1].astype(F32) for a in later_lands])
    first_state, first_token = gather_forward("g0", first_state, prepared + h1[0, :1].astype(F32))
    got = gather_end("g0", first_state, first_token)
    later_state, later_token = gather_begin("g1", later_mine, got[0], later_lands)
    full = {n: blk.transpose(1, 0, 2).reshape(blk.shape[1], 4 * blk.shape[2]) for n, blk in zip(first[1:], got[1:])}
    s0, s1, s2, s3 = got[0]
    w_in_p = jnp.concatenate([s0, s1, s2[:, :CUT_A], s2[:, CUT_B:], s3, s2[:, CUT_A:CUT_B],
                              jnp.zeros((D, P_W - D_IN), BF16)], axis=1)
    mu_rkv = shift_mu[:, :3 * C] + later_token[0, 0]
    mu_lora = jnp.pad(shift_mu[:, 3 * C:], ((0, 0), (0, LORA_W - 288)))
    w2p = jnp.pad(full["w2"], ((0, 64), (0, 0)))
    a2p = jnp.pad(full["a2"], ((64, 0), (0, 0)))
    g2p = jnp.pad(full["g2"], ((0, 96), (0, 0)))
    ii = np.arange(C)
    jm = _scan_consts()[2]
    e_np = (np.arange(LANES)[:, None] == (ii[None, :] // HEAD))
    e_mat, e_t = jnp.asarray(e_np, BF16), jnp.asarray(e_np.T, BF16)
    b_t = jnp.pad(sgu_b[0].T, ((0, 0), (0, LANES - HEADS)))
    r_k_flat = r_k.reshape(1, C)
    tm_row = _tile(t, 256)
    tm_sgu_b = _tile(t, 128)

    p = mm(h1, w_in_p, name="proj_in")
    ps_rkv = shift_fwd(p, mu_rkv, 0, "shift_rkv")
    ps_lora = shift_fwd(p, mu_lora, P_LORA // LORA_W, "shift_lora")
    pre_par = [w0, w2p, a0, a2p, g2p, k_k, k_a]
    r_, w_, k2, v_, an, b_, g_ = row_fwd("pre", pre_fn, [Tiled(ps_rkv), Tiled(ps_lora)], pre_par, [jm],
                                         [(C, F32)] * 7, tm_row)
    v3 = to_cols(v_)
    y3, ckpt = wkv_fwd(r_, w_, k2, v3, an, b_)
    later_state, later_token = gather_forward("g1", later_state, y3)
    y_ = from_cols(y3 + later_token[0, 0])
    post_in = [Tiled(z) for z in (y_, r_, k2, v_, g_)]
    post_par = [gn_w, gn_b, r_k_flat]
    (y_rwkv,) = row_fwd("post", post_fn, post_in, post_par, [jm], [(C, BF16)], tm_row)
    sgu_in = [Tiled(p, C, 3), Tiled(p, C, 4)]
    sgu_par = [sgu_ln_g, sgu_ln_b, sgu_w[0], b_t, sgu_out_g]
    (y_sgu,) = row_fwd("sgu", sgu_fn, sgu_in, sgu_par, [e_mat, e_t], [(C, BF16)], tm_row)
    y_cat = jnp.concatenate([y_rwkv, y_sgu], axis=1)
    for n, blk in zip(later, gather_end("g1", later_state, y_cat)):
        full[n] = blk if n in ("w_gate", "w_up") else blk.reshape(4 * blk.shape[1], blk.shape[2])
    x_mid = mm(y_cat, full["w_out"], res=x2d, name="proj_out")
    (h2,) = row_fwd("norm2", rms_fn, [Tiled(x_mid)], [norm2_g], [], [(D, BF16)], tm_row)
    gate = mm(h2, full["w_gate"], shards="b", name="ffn_gate")
    up, act = mm(h2, full["w_up"], shards="b", name="ffn_up",
                 post=(lambda up_tile, gate_tile: (up_tile, swiglu_fn(gate_tile, up_tile)[0]), [gate], [F32, BF16]))
    x_out = mm(act, full["w_down"], res=x_mid, name="ffn_down")
    loss_part, dx_out, d_final_g, dx_out_b = loss_head(x_out, tgt, final_g.reshape(1, D))

    grads = {"final_g": d_final_g.reshape(D)}
    d_gate, d_up = mm(dx_out_b, full["w_down"], tb=True, name="d_act",
                      post=(lambda d_act, g_tile, u_tile: jax.vjp(swiglu_fn, g_tile, u_tile)[1]((d_act,)), [gate, up],
                            [BF16, BF16]))
    grads["w_down"] = mm(act, dx_out_b, ta=True, out_dtype=BF16, name="d_w_down")
    grads["w_gate"] = mm(h2, d_gate, shards="out", out_dtype=BF16, name="d_w_gate")
    grads["w_up"] = mm(h2, d_up, shards="out", out_dtype=BF16, name="d_w_up")
    d_h2 = mm(d_gate, full["w_gate"], shards="bt", name="d_h2_gate")
    d_h2 = mm(d_up, full["w_up"], shards="bt", res=d_h2, name="d_h2_up")
    dx_mid, grads["norm2_g"], dx_mid_b = row_bwd("norm2_b", rms_res_fn, [Tiled(x_mid)], [norm2_g], [],
                                                 [Tiled(d_h2), Tiled(dx_out)], tm_row, bf16_copy_of=(0,))
    d_ycat = mm(dx_mid_b, full["w_out"], tb=True, name="d_ycat")
    grads["w_out"] = mm(y_cat, dx_mid_b, ta=True, out_dtype=BF16, name="d_w_out")
    rs_state, rs_token = sibling_exchange_begin([grads[n].reshape((4,) + args[n].shape[1:]) for n in later], "rs2")
    d_pu, d_pv, grads["sgu_ln_g"], grads["sgu_ln_b"], d_sgu_w, d_b_t, grads["sgu_out_g"] = row_bwd(
        "sgu_b", sgu_fn, sgu_in, sgu_par[:-1] + [sgu_out_g + rs_token[0, 0]], [e_mat, e_t], [Tiled(d_ycat, C, 1)],
        tm_sgu_b, tiled_out_dtypes=[BF16, BF16])
    grads["sgu_w"] = d_sgu_w[None]
    grads["sgu_b"] = d_b_t[:, :HEADS].T[None]
    post_par_b = [gn_w + rs_token[0, 0]] + post_par[1:]
    d_y, d_r1, d_k1, d_v1, d_g, grads["gn_w"], grads["gn_b"], d_r_k = row_bwd(
        "post_b", post_fn, post_in, post_par_b, [jm], [Tiled(d_ycat, C, 0)], tm_row)
    grads["r_k"] = d_r_k.reshape(r_k.shape)
    rs_state, rs_token = reduce_scatter_continue(rs_state, later, "rs2", d_y[0, :1] + d_pu[0, :1].astype(F32))
    d_r2, d_w, d_k2, d_a, d_b, d_v3 = wkv_bwd(r_, w_, k2, v3, an, b_, ckpt, to_cols(d_y + rs_token[0, 0]))
    d_v2 = from_cols(d_v3)
    pre_cts = [(Tiled(d_r1), Tiled(d_r2)), Tiled(d_w), (Tiled(d_k1), Tiled(d_k2)), (Tiled(d_v1), Tiled(d_v2)),
               Tiled(d_a), Tiled(d_b), Tiled(d_g)]
    d_ps_rkv, d_ps_lora, grads["w0"], d_w2p, grads["a0"], d_a2p, d_g2p, grads["k_k"], grads["k_a"] = row_bwd(
        "pre_b", pre_fn, [Tiled(ps_rkv), Tiled(ps_lora)], pre_par, [jm], pre_cts, tm_row)
    grads["w2"], grads["a2"], grads["g2"] = d_w2p[None, :64], d_a2p[None, 64:], d_g2p[None, :160]
    d_p_rkv, d_mu_rkv = shift_bwd(d_ps_rkv, p, mu_rkv, 0, "shift_rkv_b")
    d_p_lora, d_mu_lora = shift_bwd(d_ps_lora, p, mu_lora, P_LORA // LORA_W, "shift_lora_b")
    grads["shift_mu"] = jnp.concatenate([d_mu_rkv, d_mu_lora[:, :288]], axis=1)
    d_p = jnp.concatenate([d_p_rkv, d_pu, d_pv, d_p_lora], axis=1)
    d_h1 = mm(d_p, w_in_p, tb=True, name="d_h1")
    d_w_in_p = mm(h1, d_p, ta=True, out_dtype=BF16, name="d_w_in")
    ws = D_IN // 4
    shard2 = jnp.concatenate([d_w_in_p[:, 2 * ws:3 * C], d_w_in_p[:, P_LORA:P_LORA + 288],
                              d_w_in_p[:, 3 * C:3 * C + ws - CUT_B]], axis=1)
    grads["w_in"] = jnp.stack([d_w_in_p[:, :ws], d_w_in_p[:, ws:2 * ws], shard2,
                               d_w_in_p[:, 3 * C + ws - CUT_B:P_LORA]])
    dx, grads["norm1_g"] = row_bwd("norm1_b", rms_res_fn, [Tiled(x2d)], [norm1_g], [],
                                   [Tiled(d_h1), Tiled(dx_mid)], tm_row)

    small = [n for n in names if n not in big]
    small_shapes = [(1, 64, C) if n in ("w2", "a2") else (1, 160, C) if n == "g2" else args[n].shape for n in small]
    vec = _pack([grads[n] for n in small], F32)
    w_in_state, w_in_token = reduce_scatter_begin([grads["w_in"]], ["w_in"], "rs1")
    small_state, small_token = all_reduce_begin(vec)
    total = {n: g[None] for n, g in zip(later, reduce_scatter_end(rs_state, later, "rs2", w_in_token + small_token))}
    loss = lax.psum(loss_part[0, 0], ("x", "y", "c"))
    delta, new_m, new_v = {}, {}, {}

    def adam_big(n):
        s = args[n].shape
        view = (lambda a: a[0].T) if n == "w_in" else (lambda a: a[0])
        back = (lambda a: a.T.reshape(s)) if n == "w_in" else (lambda a: a.reshape(s))
        d_, m_, v__ = adamw(view(args[n]), view(total[n]), view(args["m_" + n]), view(args["v_" + n]), "adamw_" + n)
        delta[n], new_m[n], new_v[n] = back(d_), back(m_), back(v__)

    for n in later:
        adam_big(n)
    total["w_in"] = reduce_scatter_end(w_in_state, ["w_in"], "rs1", new_v["w_down"])[0][None]
    for n, g_sum in zip(small, _unpack(all_reduce_end(small_state, new_v["w_down"]), small_shapes)):
        total[n] = lax.dynamic_slice_in_dim(g_sum, chip * 256, 256, axis=2) if n in lora_names else g_sum
    adam_big("w_in")
    flat = [_pack([src[n] for n in small], F32)
            for src in (args, total, {n: args["m_" + n] for n in small}, {n: args["v_" + n] for n in small})]
    outs = adamw(*flat, "adamw_small")
    for res, o in zip((delta, new_m, new_v), outs):
        res.update(zip(small, _unpack(o, [args[n].shape for n in small])))
    return (loss, dx[None], *[total[n] for n in names], *[delta[n] for n in names],
            *[new_m[n] for n in names], *[new_v[n] for n in names])
```

```python
import functools

import numpy as np
import jax
import jax.numpy as jnp
from jax import lax
from jax.experimental import pallas as pl
from jax.experimental.pallas import tpu as pltpu

F32 = jnp.float32
BF16 = jnp.bfloat16

D = 2048
C = 1024
HEADS = 16
HEAD = 64
LANES = 128
P_W = 5632
P_LORA = 5120
LORA_W = 512
D_IN = 5408
CUT_A = 3 * C - 2 * (D_IN // 4)
CUT_B = CUT_A + 288
CHUNK = 128
RMS_EPS = 1e-6
LN_EPS = 1e-5
GN_EPS = 64e-5
L2_EPS = 1e-12
VMEM_LIMIT = 56 * 1024 * 1024


def _pcall(body, **kw):
    return pl.pallas_call(body, **kw)


def _cparams(sem):
    return pltpu.CompilerParams(dimension_semantics=sem, vmem_limit_bytes=VMEM_LIMIT)


def _tile(n, most):
    t = most
    while t > 8 and n % t:
        t //= 2
    assert n % t == 0, (n, most)
    return t


MM_TILE = (2048, 512, 2048)
MM_VMEM = 40 * 1024 * 1024


def _div_tile(n, most, quantum=LANES):
    for t in range(min(n, most) // quantum * quantum, 0, -quantum):
        if n % t == 0:
            return t
    raise ValueError((n, most, quantum))


def _mm_tiles(m, n, k, a_bytes, b_bytes, r_bytes, o_bytes):
    tm, tn, tk = _div_tile(m, MM_TILE[0]), _div_tile(n, MM_TILE[1]), _div_tile(k, MM_TILE[2])

    def need(tm, tk):
        return 2 * (tm * tk * a_bytes + tk * tn * b_bytes + tm * tn * (r_bytes + o_bytes)) + tm * tn * 4

    while need(tm, tk) > MM_VMEM:
        if tk >= tm and tk > 512:
            tk = _div_tile(k, tk - LANES)
        else:
            tm = _div_tile(m, tm - LANES)
    return tm, tn, tk


def mm(a, b, *, ta=False, tb=False, res=None, out_dtype=F32, name, shards=None, post=None):
    if shards is not None:
        return _mm_shards(a, b, res, out_dtype, name, shards, post)
    m, k = (a.shape[1], a.shape[0]) if ta else a.shape
    n = b.shape[0] if tb else b.shape[1]
    assert (b.shape[1] if tb else b.shape[0]) == k
    if post is None:
        tm, tn, tk = _mm_tiles(m, n, k, a.dtype.itemsize, b.dtype.itemsize,
                               0 if res is None else res.dtype.itemsize, jnp.dtype(out_dtype).itemsize)
    else:
        tm, tn, tk = _div_tile(m, 1024), _div_tile(n, MM_TILE[1]), _div_tile(k, MM_TILE[2])
    nk = k // tk
    dims = (((0 if ta else 1,), (1 if tb else 0,)), ((), ()))
    a_spec = pl.BlockSpec((tk, tm), lambda i, j, l: (l, i)) if ta else pl.BlockSpec((tm, tk), lambda i, j, l: (i, l))
    b_spec = pl.BlockSpec((tn, tk), lambda i, j, l: (j, l)) if tb else pl.BlockSpec((tk, tn), lambda i, j, l: (l, j))
    o_spec = pl.BlockSpec((tm, tn), lambda i, j, l: (i, j))
    return _mm_call(a, b, res, dims, (m // tm, n // tn, nk), a_spec, b_spec, o_spec, o_spec, (tm, tn),
                    jax.ShapeDtypeStruct((m, n), out_dtype), name, post)


def _mm_shards(a, b, res, out_dtype, name, shards, post=None):
    assert post is None or shards == "b"
    if shards == "b":
        (m, k), ns = a.shape, b.shape[2]
        tm, tk = _div_tile(m, 1024 if post is None else 512), _div_tile(k, 2048)
        grid, dims, acc = (m // tm, 4, k // tk), (((1,), (0,)), ((), ())), (tm, ns)
        a_spec = pl.BlockSpec((tm, tk), lambda i, s, l: (i, l))
        b_spec = pl.BlockSpec((None, tk, ns), lambda i, s, l: (s, l, 0))
        o_spec = pl.BlockSpec((tm, ns), lambda i, s, l: (i, s))
        out = jax.ShapeDtypeStruct((m, 4 * ns), out_dtype)
    elif shards == "bt":
        m, (_, n, ns) = a.shape[0], b.shape
        tm, tn = _div_tile(m, 2048), _div_tile(n, 512)
        grid, dims, acc = (m // tm, n // tn, 4), (((1,), (1,)), ((), ())), (tm, tn)
        a_spec = pl.BlockSpec((tm, ns), lambda i, j, s: (i, s))
        b_spec = pl.BlockSpec((None, tn, ns), lambda i, j, s: (s, j, 0))
        o_spec = pl.BlockSpec((tm, tn), lambda i, j, s: (i, j))
        out = jax.ShapeDtypeStruct((m, n), out_dtype)
    else:
        (t, m), ns = a.shape, b.shape[1] // 4
        tm, tk = _div_tile(m, 1024), _div_tile(t, 2048)
        grid, dims, acc = (m // tm, 4, t // tk), (((0,), (0,)), ((), ())), (tm, ns)
        a_spec = pl.BlockSpec((tk, tm), lambda i, s, l: (l, i))
        b_spec = pl.BlockSpec((tk, ns), lambda i, s, l: (l, s))
        o_spec = pl.BlockSpec((None, tm, ns), lambda i, s, l: (s, i, 0))
        out = jax.ShapeDtypeStruct((4, m, ns), out_dtype)
    return _mm_call(a, b, res, dims, grid, a_spec, b_spec, o_spec, o_spec, acc, out, name, post)


def _mm_call(a, b, res, dims, grid, a_spec, b_spec, r_spec, o_spec, acc_shape, out, name, post=None):
    nk = grid[2]
    post_fn, post_in, post_dtypes = (None, [], [out.dtype]) if post is None else post
    n_extra = (res is not None) + len(post_in)

    def body(*refs):
        a_ref, b_ref = refs[:2]
        r_ref = None if res is None else refs[2]
        p_refs = refs[2 + (res is not None):2 + n_extra]
        o_refs = refs[2 + n_extra:2 + n_extra + len(post_dtypes)]

        def finish(acc):
            if r_ref is not None:
                acc = acc + r_ref[...].astype(F32)
            outs = (acc,) if post_fn is None else post_fn(acc, *[p[...] for p in p_refs])
            for o_ref, o in zip(o_refs, outs):
                o_ref[...] = o.astype(o_ref.dtype)

        prod = lax.dot_general(a_ref[...].astype(BF16), b_ref[...].astype(BF16), dims, preferred_element_type=F32)
        if nk == 1:
            finish(prod)
            return
        acc_ref = refs[-1]
        kk = pl.program_id(2)

        @pl.when(kk == 0)
        def _():
            acc_ref[...] = prod

        @pl.when(kk > 0)
        def _():
            acc_ref[...] += prod

        @pl.when(kk == nk - 1)
        def _():
            finish(acc_ref[...])

    in_specs = [a_spec, b_spec] + ([r_spec] if res is not None else []) + [o_spec] * len(post_in)
    args = (a, b) + ((res,) if res is not None else ()) + tuple(post_in)
    outs = _pcall(
        body, name=name, grid=grid, in_specs=in_specs, out_specs=[o_spec] * len(post_dtypes),
        out_shape=[jax.ShapeDtypeStruct(out.shape, dt) for dt in post_dtypes],
        scratch_shapes=[] if nk == 1 else [pltpu.VMEM(acc_shape, F32)],
        compiler_params=_cparams(("parallel", "parallel", "arbitrary")),
    )(*args)
    return outs[0] if post is None else outs


class Tiled:
    def __init__(self, arr, width=None, col=0):
        self.arr, self.width, self.col = arr, (arr.shape[1] if width is None else width), col

    def spec(self, tm):
        col = self.col
        return pl.BlockSpec((tm, self.width), lambda i: (i, col))


def _full_spec(p):
    nd = p.ndim
    return pl.BlockSpec(p.shape, lambda i: (0,) * nd)


def row_fwd(name, fn, tiled, params, consts, outs, tm):
    t = tiled[0].arr.shape[0]
    n_in = len(tiled) + len(params) + len(consts)

    def body(*refs):
        res = fn(*[r[...] for r in refs[:n_in]])
        for o_ref, r in zip(refs[n_in:], res):
            o_ref[...] = r.astype(o_ref.dtype)

    return _pcall(
        body, name=name, grid=(t // tm,),
        in_specs=[x.spec(tm) for x in tiled] + [_full_spec(p) for p in params + consts],
        out_specs=[pl.BlockSpec((tm, w), lambda i: (i, 0)) for w, _ in outs],
        out_shape=[jax.ShapeDtypeStruct((t, w), dt) for w, dt in outs],
        compiler_params=_cparams(("parallel",)),
    )(*[x.arr for x in tiled], *params, *consts)


def row_bwd(name, fn, tiled, params, consts, cts, tm, tiled_out_dtypes=None, bf16_copy_of=()):
    t = tiled[0].arr.shape[0]
    nt, npar, ncon = len(tiled), len(params), len(consts)
    cts = [c if isinstance(c, tuple) else (c,) for c in cts]
    flat_cts = [x for c in cts for x in c]
    tiled_out_dtypes = tiled_out_dtypes or [F32] * nt
    copies = [tiled[j] for j in bf16_copy_of]

    def body(*refs):
        n_in = nt + npar + ncon
        ins = [r[...].astype(F32) for r in refs[:nt + npar]]
        con = [r[...] for r in refs[nt + npar:n_in]]
        ct_refs = list(refs[n_in:n_in + len(flat_cts)])
        o = refs[n_in + len(flat_cts):]
        ct = []
        for c in cts:
            parts = [ct_refs.pop(0)[...].astype(F32) for _ in c]
            ct.append(functools.reduce(lambda p, q: p + q, parts))
        _, vjp = jax.vjp(lambda *a: fn(*a, *con), *ins)
        g = vjp(tuple(ct))
        for j in range(nt):
            o[j][...] = g[j].astype(o[j].dtype)
        for n, j in enumerate(bf16_copy_of):
            o[nt + npar + n][...] = g[j].astype(BF16)
        first = pl.program_id(0) == 0

        @pl.when(first)
        def _():
            for j in range(npar):
                o[nt + j][...] = g[nt + j]

        @pl.when(jnp.logical_not(first))
        def _():
            for j in range(npar):
                o[nt + j][...] += g[nt + j]

    return _pcall(
        body, name=name, grid=(t // tm,),
        in_specs=[x.spec(tm) for x in tiled] + [_full_spec(p) for p in params + consts]
        + [x.spec(tm) for x in flat_cts],
        out_specs=[pl.BlockSpec((tm, x.width), lambda i: (i, 0)) for x in tiled] + [_full_spec(p) for p in params]
        + [pl.BlockSpec((tm, x.width), lambda i: (i, 0)) for x in copies],
        out_shape=[jax.ShapeDtypeStruct((t, x.width), dt) for x, dt in zip(tiled, tiled_out_dtypes)]
        + [jax.ShapeDtypeStruct(p.shape, F32) for p in params]
        + [jax.ShapeDtypeStruct((t, x.width), BF16) for x in copies],
        compiler_params=_cparams(("arbitrary",)),
    )(*[x.arr for x in tiled], *params, *consts, *[x.arr for x in flat_cts])


def _split_dot(x, w):
    hi = x.astype(BF16)
    lo = (x - hi.astype(F32)).astype(BF16)
    return jnp.dot(hi, w, preferred_element_type=F32) + jnp.dot(lo, w, preferred_element_type=F32)


def _headsum_blocks(x, j2k):
    out = []
    for c0 in range(0, x.shape[1], LANES):
        blk = x[:, c0:c0 + LANES]
        hi = blk.astype(BF16)
        lo = (blk - hi.astype(F32)).astype(BF16)
        out.append(jnp.dot(jnp.concatenate([hi, lo], axis=1), j2k, preferred_element_type=F32))
    return jnp.concatenate(out, axis=1)


@jax.custom_vjp
def headsum(x, j2k):
    return _headsum_blocks(x, j2k)


def _headsum_fwd(x, j2k):
    return _headsum_blocks(x, j2k), j2k


def _headsum_bwd(j2k, ct):
    return _headsum_blocks(ct, j2k), jnp.zeros_like(j2k)


headsum.defvjp(_headsum_fwd, _headsum_bwd)


def _bdot(x, w):
    return jnp.dot(x.astype(BF16), w.astype(BF16), preferred_element_type=F32)


def _sigmoid(x):
    return jax.nn.sigmoid(x)


def _softplus(x):
    return jnp.maximum(x, 0.0) + jnp.log(1.0 + jnp.exp(-jnp.abs(x)))


def rms_fn(x, g):
    return (x * lax.rsqrt(jnp.mean(x * x, axis=-1, keepdims=True) + RMS_EPS) * g,)


def rms_res_fn(x, g):
    return rms_fn(x, g)[0], x


def pre_fn(rkv, lora, w0, w2p, a0, a2p, g2p, k_k, k_a, jm):
    r, k, v = rkv[:, :C], rkv[:, C:2 * C], rkv[:, 2 * C:]
    zwa, zg = lora[:, :LANES], lora[:, LANES:LANES + 256]
    w_log = -_softplus(-(w0 + _bdot(jnp.tanh(zwa), w2p))) - 0.5
    decay = jnp.exp(-jnp.exp(w_log))
    a = _sigmoid(a0 + _bdot(zwa, a2p))
    g = _bdot(_sigmoid(zg), g2p)
    kk = k * k_k
    kk = kk / jnp.maximum(jnp.sqrt(headsum(kk * kk, jm)), L2_EPS)
    k2 = k * (1.0 + (a - 1.0) * k_a)
    return r, decay, k2, v, -kk, kk * a, g


def post_fn(y, r, k2, v, g, gn_w, gn_b, r_k, jm):
    mu = headsum(y, jm) * (1.0 / HEAD)
    yc = y - mu
    var = headsum(yc * yc, jm) * (1.0 / HEAD)
    yn = yc * lax.rsqrt(var + GN_EPS) * gn_w + gn_b
    bonus = headsum(r * k2 * r_k, jm) * v
    return ((yn + bonus) * g,)


def _gelu(x):
    return 0.5 * x * (1.0 + lax.erf(x * np.float32(1.0 / np.sqrt(2.0))))


@jax.custom_vjp
def expand_groups(b_t, e, e_t):
    return _split_dot(b_t, e)


def _expand_groups_fwd(b_t, e, e_t):
    return _split_dot(b_t, e), (e, e_t)


def _expand_groups_bwd(res, ct):
    e, e_t = res
    return _split_dot(ct, e_t), jnp.zeros_like(e), jnp.zeros_like(e_t)


expand_groups.defvjp(_expand_groups_fwd, _expand_groups_bwd)


def sgu_fn(pu, pv, ln_g, ln_b, w_s, b_t, out_g, e, e_t):
    rows = pu.shape[0]
    b_exp = expand_groups(b_t, e, e_t)
    u, v = _gelu(pu), _gelu(pv)
    mu = jnp.mean(v, axis=-1, keepdims=True)
    vc = v - mu
    var = jnp.mean(vc * vc, axis=-1, keepdims=True)
    v = vc * lax.rsqrt(var + LN_EPS) * ln_g + ln_b
    tri = lax.broadcasted_iota(jnp.int32, (CHUNK, CHUNK), 0) >= lax.broadcasted_iota(jnp.int32, (CHUNK, CHUNK), 1)
    left = lax.broadcasted_iota(jnp.int32, (CHUNK, LANES), 1) < HEAD
    chunks = []
    for c0 in range(0, rows, CHUNK):
        cols = []
        for gp in range(HEADS // 2):
            vp = v[c0:c0 + CHUNK, gp * LANES:(gp + 1) * LANES]
            wa = jnp.where(tri, w_s[2 * gp], 0.0)
            wb = jnp.where(tri, w_s[2 * gp + 1], 0.0)
            cols.append(jnp.where(left, _bdot(wa, vp), _bdot(wb, vp)))
        chunks.append(jnp.concatenate(cols, axis=1) + b_exp)
    s = jnp.concatenate(chunks, axis=0) if len(chunks) > 1 else chunks[0]
    y = u * s
    return (y * lax.rsqrt(jnp.mean(y * y, axis=-1, keepdims=True) + RMS_EPS) * out_g,)


def swiglu_fn(gate, up):
    return (gate * _sigmoid(gate) * up,)


SCAN_CHUNK = 32
NB = C // LANES
FWD_GROUP = 2
BWD_GROUP = 8


def to_cols(a):
    t = a.shape[0]
    return a.reshape(t // 8, LANES, HEAD).transpose(0, 2, 1)


def from_cols(a):
    return a.transpose(0, 2, 1).reshape(a.shape[0] * 8, C)


def _scan_consts():
    i = np.arange(LANES)
    spread = (i[:, None] % 2) == (i[None, :] // HEAD)
    pick = (i[:, None] // HEAD) == (i[None, :] % 2)
    j2 = (i[:, None] // HEAD) == (i[None, :] // HEAD)
    sel = (i[None, :] // 2) == (np.arange(8 * NB * HEAD)[:, None] // HEAD)
    return (jnp.asarray(spread, BF16), jnp.asarray(pick, BF16), jnp.asarray(np.concatenate([j2, j2], 0), BF16),
            jnp.asarray(sel, BF16))


def _stack(blocks):
    return jnp.concatenate(blocks, axis=0) if len(blocks) > 1 else blocks[0]


def _unstack(x, n):
    return [x[i * HEAD:(i + 1) * HEAD] for i in range(n)]


def _headsums(blocks, j2k, group):
    res = []
    for g0 in range(0, len(blocks), group):
        x = _stack(blocks[g0:g0 + group])
        hi = x.astype(BF16)
        lo = (x - hi.astype(F32)).astype(BF16)
        out = jnp.dot(jnp.concatenate([hi, lo], axis=1), j2k, preferred_element_type=F32)
        res += _unstack(out, len(blocks[g0:g0 + group]))
    return res


def _headsums_out(blocks, pick):
    return _unstack(jnp.dot(_stack(blocks).astype(BF16), pick, preferred_element_type=F32), len(blocks))


def _expand8(tile, sel, spread):
    lhs = jnp.tile(tile.astype(BF16), (8 * NB, 1)) * sel
    return _unstack(jnp.dot(lhs, spread, preferred_element_type=F32), 8 * NB)


def _collapse(tile, blocks, first, lane_pair):
    for n, blk in enumerate(blocks):
        tile = jnp.where(jnp.tile(lane_pair == first + n, (HEAD // 8, 1)), blk, tile)
    return tile


def _row(tile, j, cb):
    return jnp.broadcast_to(tile[j:j + 1, cb * LANES:(cb + 1) * LANES], (HEAD, LANES))


def _scan_step(s_ref, rows, vexp, j2k, j, hist=None):
    w_t, k_t, a_t, b_t, r_t = rows
    s = [s_ref[:, cb * LANES:(cb + 1) * LANES] for cb in range(NB)]
    sab = _headsums([s[cb] * _row(a_t, j, cb) for cb in range(NB)], j2k, FWD_GROUP if hist is None else BWD_GROUP)
    out = []
    for cb in range(NB):
        sl = slice(cb * LANES, (cb + 1) * LANES)
        s_new = s[cb] * _row(w_t, j, cb) + sab[cb] * _row(b_t, j, cb) + vexp[j * NB + cb] * _row(k_t, j, cb)
        s_ref[:, sl] = s_new
        if hist is None:
            out.append(s_new * _row(r_t, j, cb))
        else:
            s_hist, sab_hist, idx = hist
            s_hist[idx + 1, :, sl] = s_new
            sab_hist[idx, :, sl] = sab[cb]
    return out


def wkv_fwd(r, w, k, v3, a, b):
    t = r.shape[0]
    nc = t // SCAN_CHUNK
    n8 = SCAN_CHUNK // 8

    def body(r_ref, w_ref, k_ref, a_ref, b_ref, v3_ref, spread_ref, pick_ref, j2k_ref, sel_ref, y3_ref, ck_ref,
             s_ref):
        @pl.when(pl.program_id(0) == 0)
        def _():
            s_ref[...] = jnp.zeros_like(s_ref)

        ck_ref[0] = s_ref[...]
        spread, pick, j2k, sel = spread_ref[...], pick_ref[...], j2k_ref[...], sel_ref[...]
        lane_pair = lax.broadcasted_iota(jnp.int32, (8, LANES), 1) // 2

        def t8_body(t8, carry):
            row0 = pl.multiple_of(t8 * 8, 8)
            rows = [ref[pl.ds(row0, 8), :] for ref in (w_ref, k_ref, a_ref, b_ref, r_ref)]
            vexp = _expand8(v3_ref[t8], sel, spread)
            y3 = jnp.zeros((HEAD, LANES), F32)
            for j in range(8):
                y3 = _collapse(y3, _headsums_out(_scan_step(s_ref, rows, vexp, j2k, j), pick), j * NB, lane_pair)
            y3_ref[t8] = y3
            return carry

        lax.fori_loop(0, n8, t8_body, 0)

    row_spec = pl.BlockSpec((SCAN_CHUNK, C), lambda i: (i, 0))
    col_spec = pl.BlockSpec((n8, HEAD, LANES), lambda i: (i, 0, 0))
    consts = _scan_consts()
    return _pcall(
        body, name="wkv_fwd", grid=(nc,),
        in_specs=[row_spec] * 5 + [col_spec] + [pl.BlockSpec(c.shape, lambda i: (0, 0)) for c in consts],
        out_specs=[col_spec, pl.BlockSpec((1, HEAD, C), lambda i: (i, 0, 0))],
        out_shape=[jax.ShapeDtypeStruct((t // 8, HEAD, LANES), F32), jax.ShapeDtypeStruct((nc, HEAD, C), F32)],
        scratch_shapes=[pltpu.VMEM((HEAD, C), F32)],
        compiler_params=_cparams(("arbitrary",)),
    )(r, w, k, a, b, v3, *consts)


def wkv_bwd(r, w, k, v3, a, b, ckpt, dy3):
    t = r.shape[0]
    nc = t // SCAN_CHUNK
    n8 = SCAN_CHUNK // 8

    def body(r_ref, w_ref, k_ref, a_ref, b_ref, v3_ref, dy3_ref, ck_ref, spread_ref, pick_ref, j2k_ref, sel_ref,
             dr_ref, dw_ref, dk_ref, da_ref, db_ref, dv3_ref, s_ref, g_ref, s_hist, sab_hist):
        @pl.when(pl.program_id(0) == 0)
        def _():
            g_ref[...] = jnp.zeros_like(g_ref)

        spread, pick, j2k, sel = spread_ref[...], pick_ref[...], j2k_ref[...], sel_ref[...]
        lane_pair = lax.broadcasted_iota(jnp.int32, (8, LANES), 1) // 2
        sub = lax.broadcasted_iota(jnp.int32, (8, LANES), 0)
        s_ref[...] = ck_ref[0]
        s_hist[0] = ck_ref[0]

        def redo(t8, carry):
            row0 = pl.multiple_of(t8 * 8, 8)
            rows = [ref[pl.ds(row0, 8), :] for ref in (w_ref, k_ref, a_ref, b_ref, r_ref)]
            vexp = _expand8(v3_ref[t8], sel, spread)
            for j in range(8):
                _scan_step(s_ref, rows, vexp, j2k, j, hist=(s_hist, sab_hist, t8 * 8 + j))
            return carry

        lax.fori_loop(0, n8, redo, 0)

        def back(q, carry):
            t8 = n8 - 1 - q
            row0 = pl.multiple_of(t8 * 8, 8)
            w_t, k_t, a_t, b_t, r_t = [ref[pl.ds(row0, 8), :] for ref in (w_ref, k_ref, a_ref, b_ref, r_ref)]
            vexp = _expand8(v3_ref[t8], sel, spread)
            dyexp = _expand8(dy3_ref[t8], sel, spread)
            dv3 = jnp.zeros((HEAD, LANES), F32)
            tiles = [[jnp.zeros((8, LANES), F32) for _ in range(NB)] for _ in range(5)]
            for j in range(7, -1, -1):
                idx = t8 * 8 + j
                g = [g_ref[:, cb * LANES:(cb + 1) * LANES] + dyexp[j * NB + cb] * _row(r_t, j, cb)
                     for cb in range(NB)]
                dsab = _headsums([g[cb] * _row(b_t, j, cb) for cb in range(NB)], j2k, BWD_GROUP)
                dv3 = _collapse(dv3, _headsums_out([g[cb] * _row(k_t, j, cb) for cb in range(NB)], pick), j * NB,
                                lane_pair)
                for cb in range(NB):
                    sl = slice(cb * LANES, (cb + 1) * LANES)
                    s_new = s_hist[idx + 1, :, sl]
                    s_old = s_hist[idx, :, sl]
                    sab = sab_hist[idx, :, sl]
                    sums = (s_new * dyexp[j * NB + cb], g[cb] * s_old, g[cb] * vexp[j * NB + cb],
                            s_old * dsab[cb], g[cb] * sab)
                    for n, prod in enumerate(sums):
                        rowsum = jnp.broadcast_to(jnp.sum(prod, axis=0, keepdims=True), (8, LANES))
                        tiles[n][cb] = jnp.where(sub == j, rowsum, tiles[n][cb])
                    g_ref[:, sl] = g[cb] * _row(w_t, j, cb) + dsab[cb] * _row(a_t, j, cb)
            dv3_ref[t8] = dv3
            for n, ref in enumerate((dr_ref, dw_ref, dk_ref, da_ref, db_ref)):
                for cb in range(NB):
                    ref[pl.ds(row0, 8), cb * LANES:(cb + 1) * LANES] = tiles[n][cb]
            return carry

        lax.fori_loop(0, n8, back, 0)

    row_spec = pl.BlockSpec((SCAN_CHUNK, C), lambda i: (nc - 1 - i, 0))
    col_spec = pl.BlockSpec((n8, HEAD, LANES), lambda i: (nc - 1 - i, 0, 0))
    consts = _scan_consts()
    return _pcall(
        body, name="wkv_bwd", grid=(nc,),
        in_specs=[row_spec] * 5 + [col_spec, col_spec, pl.BlockSpec((1, HEAD, C), lambda i: (nc - 1 - i, 0, 0))]
        + [pl.BlockSpec(c.shape, lambda i: (0, 0)) for c in consts],
        out_specs=[row_spec] * 5 + [col_spec],
        out_shape=[jax.ShapeDtypeStruct((t, C), F32)] * 5 + [jax.ShapeDtypeStruct((t // 8, HEAD, LANES), F32)],
        scratch_shapes=[pltpu.VMEM((HEAD, C), F32), pltpu.VMEM((HEAD, C), F32),
                        pltpu.VMEM((SCAN_CHUNK + 1, HEAD, C), F32), pltpu.VMEM((SCAN_CHUNK, HEAD, C), F32)],
        compiler_params=_cparams(("arbitrary",)),
    )(r, w, k, a, b, v3, dy3, ckpt, *consts)


def _prev_rows(cur, before, first_tile):
    last = jnp.where(first_tile, 0.0, before[7:8, :])
    row = lax.broadcasted_iota(jnp.int32, cur.shape, 0)
    return jnp.where(row == 0, last, pltpu.roll(cur, 1, 0))


def shift_fwd(p, mu, col, name):
    t, width = p.shape[0], mu.shape[1]
    tm = _tile(t, 256)

    def body(p_ref, before_ref, mu_ref, o_ref):
        cur = p_ref[...]
        prev = _prev_rows(cur, before_ref[...], pl.program_id(0) == 0)
        o_ref[...] = cur + (prev - cur) * mu_ref[...]

    return _pcall(
        body, name=name, grid=(t // tm,),
        in_specs=[pl.BlockSpec((tm, width), lambda i: (i, col)),
                  pl.BlockSpec((8, width), lambda i: (jnp.maximum(i * (tm // 8) - 1, 0), col)),
                  pl.BlockSpec((1, width), lambda i: (0, 0))],
        out_specs=pl.BlockSpec((tm, width), lambda i: (i, 0)),
        out_shape=jax.ShapeDtypeStruct((t, width), F32),
        compiler_params=_cparams(("parallel",)),
    )(p, p, mu)


def shift_bwd(dps, p, mu, col, name):
    t, width = dps.shape
    tm = _tile(t, 256)
    nt = t // tm

    def body(d_ref, after_ref, p_ref, before_ref, mu_ref, dp_ref, dmu_ref):
        i = pl.program_id(0)
        d, cur, mu_v = d_ref[...], p_ref[...], mu_ref[...]
        row = lax.broadcasted_iota(jnp.int32, d.shape, 0)
        first_after = jnp.where(i == nt - 1, 0.0, after_ref[0:1, :])
        nxt = jnp.where(row == tm - 1, first_after, pltpu.roll(d, tm - 1, 0))
        dp_ref[...] = (d * (1.0 - mu_v) + nxt * mu_v).astype(BF16)
        prev = _prev_rows(cur, before_ref[...], i == 0)
        part = jnp.sum(d * (prev - cur), axis=0, keepdims=True)

        @pl.when(i == 0)
        def _():
            dmu_ref[...] = part

        @pl.when(i > 0)
        def _():
            dmu_ref[...] += part

    return _pcall(
        body, name=name, grid=(nt,),
        in_specs=[pl.BlockSpec((tm, width), lambda i: (i, 0)),
                  pl.BlockSpec((8, width), lambda i: (jnp.minimum((i + 1) * (tm // 8), t // 8 - 1), 0)),
                  pl.BlockSpec((tm, width), lambda i: (i, col)),
                  pl.BlockSpec((8, width), lambda i: (jnp.maximum(i * (tm // 8) - 1, 0), col)),
                  pl.BlockSpec((1, width), lambda i: (0, 0))],
        out_specs=[pl.BlockSpec((tm, width), lambda i: (i, 0)), pl.BlockSpec((1, width), lambda i: (0, 0))],
        out_shape=[jax.ShapeDtypeStruct((t, width), BF16), jax.ShapeDtypeStruct((1, width), F32)],
        compiler_params=_cparams(("arbitrary",)),
    )(dps, dps, p, p, mu)


def loss_head(x3, tgt, g):
    t, d = x3.shape
    tm = _tile(t, 256)

    def body(x_ref, t_ref, g_ref, loss_ref, dx_ref, dg_ref, dxb_ref):
        (y,), vjp = jax.vjp(lambda a, b: rms_fn(a, b), x_ref[...], g_ref[...])
        diff = y - t_ref[...]
        part = 0.5 * jnp.sum(jnp.mean(diff * diff, axis=-1, keepdims=True), axis=0, keepdims=True)
        dx, dg = vjp((diff * (1.0 / d),))
        dx_ref[...] = dx
        dxb_ref[...] = dx.astype(BF16)
        part = jnp.broadcast_to(part, (1, LANES))
        first = pl.program_id(0) == 0

        @pl.when(first)
        def _():
            loss_ref[...] = part
            dg_ref[...] = dg

        @pl.when(jnp.logical_not(first))
        def _():
            loss_ref[...] += part
            dg_ref[...] += dg

    row = pl.BlockSpec((tm, d), lambda i: (i, 0))
    vec = pl.BlockSpec((1, d), lambda i: (0, 0))
    return _pcall(
        body, name="loss_head", grid=(t // tm,), in_specs=[row, row, vec],
        out_specs=[pl.BlockSpec((1, LANES), lambda i: (0, 0)), row, vec, row],
        out_shape=[jax.ShapeDtypeStruct((1, LANES), F32), jax.ShapeDtypeStruct((t, d), F32),
                   jax.ShapeDtypeStruct((1, d), F32), jax.ShapeDtypeStruct((t, d), BF16)],
        compiler_params=_cparams(("arbitrary",)),
    )(x3, tgt, g)


ADAM_LR, ADAM_B1, ADAM_B2, ADAM_EPS, ADAM_WD, ADAM_STEP = 0.001, 0.9, 0.999, 1e-08, 0.01, 10


def adamw(w, g, m, v, name):
    rows, width = w.shape
    tr = _div_tile(rows, 256, 8)

    def body(w_ref, g_ref, m_ref, v_ref, d_ref, nm_ref, nv_ref):
        gv = g_ref[...]
        m_new = ADAM_B1 * m_ref[...] + (1.0 - ADAM_B1) * gv
        v_new = ADAM_B2 * v_ref[...] + (1.0 - ADAM_B2) * (gv * gv)
        m_hat = m_new / (1.0 - ADAM_B1 ** ADAM_STEP)
        v_hat = v_new / (1.0 - ADAM_B2 ** ADAM_STEP)
        d_ref[...] = -ADAM_LR * (m_hat / (jnp.sqrt(v_hat) + ADAM_EPS) + ADAM_WD * w_ref[...])
        nm_ref[...] = m_new
        nv_ref[...] = v_new

    spec = pl.BlockSpec((tr, width), lambda i: (i, 0))
    return _pcall(
        body, name=name, grid=(rows // tr,), in_specs=[spec] * 4, out_specs=[spec] * 3,
        out_shape=[jax.ShapeDtypeStruct((rows, width), F32)] * 3,
        compiler_params=_cparams(("parallel",)),
    )(w, g, m, v)


def sum_slots(q, name, for_swap=False):
    n, rows, width = q.shape
    tr = _div_tile(rows, 512, 16)

    def body(*refs):
        refs = refs[for_swap:]
        acc = refs[0][...].astype(F32)
        for r in refs[1:n]:
            acc = acc + r[...].astype(F32)
        refs[n][...] = acc

    if not for_swap:
        specs = [pl.BlockSpec((None, tr, width), functools.partial(lambda s, i: (s, i, 0), s)) for s in range(n)]
        return _pcall(body, name=name, grid=(rows // tr,), in_specs=specs,
                      out_specs=pl.BlockSpec((tr, width), lambda i: (i, 0)),
                      out_shape=jax.ShapeDtypeStruct((rows, width), F32),
                      compiler_params=_cparams(("parallel",)))(*([q] * n))
    specs = [pl.BlockSpec((None, tr, width), functools.partial(lambda s, i, c_ref: (s, i, 0), s)) for s in range(n)]
    grid_spec = pltpu.PrefetchScalarGridSpec(
        num_scalar_prefetch=1, grid=(rows // tr,), in_specs=specs,
        out_specs=pl.BlockSpec((None, tr, width), lambda i, c_ref: (c_ref[0], i, 0)))
    return _pcall(body, name=name, grid_spec=grid_spec, out_shape=jax.ShapeDtypeStruct((2, rows, width), F32),
                  compiler_params=_cparams(("parallel",)))(
        lax.axis_index("c").astype(jnp.int32).reshape(1), *([q] * n))


MESH = pl.DeviceIdType.MESH
ANY_SPEC = pl.BlockSpec(memory_space=pl.ANY)


def _coords():
    return lax.axis_index("x"), lax.axis_index("y"), lax.axis_index("c")


def _other_chips(x, y):
    return [(1 - x, y), (x, 1 - y), (1 - x, 1 - y)]


def push(name, srcs, n_dst, plan_fn):
    n_arr = len(srcs)
    n_send = len(plan_fn(0, 0, 0))

    def body(*refs):
        src_refs, dst_refs = refs[:n_arr], refs[n_arr:2 * n_arr]
        send_sems, recv_sems = refs[2 * n_arr:]
        sends = plan_fn(*_coords())
        out = []
        for q, (src_ref, dst_ref) in enumerate(zip(src_refs, dst_refs)):
            out += [pltpu.make_async_remote_copy(src_ref.at[si], dst_ref.at[di], send_sems.at[q * n_send + k],
                                                 recv_sems.at[q * n_send + k], device_id=peer, device_id_type=MESH)
                    for k, (si, peer, di, _) in enumerate(sends)]
        for cp in out:
            cp.start()
        for q, (src_ref, dst_ref) in enumerate(zip(src_refs, dst_refs)):
            for k, (si, peer, _, ri) in enumerate(sends):
                pltpu.make_async_remote_copy(src_ref.at[si], dst_ref.at[ri], send_sems.at[q * n_send + k],
                                             recv_sems.at[q * n_send + k], device_id=peer,
                                             device_id_type=MESH).wait_recv()
        for cp in out:
            cp.wait_send()

    return _pcall(
        body, name=name, in_specs=[ANY_SPEC] * n_arr, out_specs=[ANY_SPEC] * n_arr,
        out_shape=[jax.ShapeDtypeStruct((n_dst,) + s.shape[1:], s.dtype) for s in srcs],
        scratch_shapes=[pltpu.SemaphoreType.DMA((n_arr * n_send,)), pltpu.SemaphoreType.DMA((n_arr * n_send,))],
    )(*srcs)


def plan_sibling_quarters(x, y, c):
    return [(2 * s + (1 - c), (x, y, 1 - c), s, s) for s in range(4)]


def plan_chips_by_shard(x, y, c):
    me = 2 * x + y
    return [(2 * px + py, (px, py, c), me, 2 * px + py) for px, py in _other_chips(x, y)]


def plan_chips_half(x, y, c):
    me = 2 * x + y
    return [(c, (px, py, c), me, 2 * px + py) for px, py in _other_chips(x, y)]


def swap_halves(name, bufs):
    n_arr = len(bufs)

    def body(*refs):
        buf_refs = refs[n_arr:2 * n_arr]
        send_sems, recv_sems = refs[2 * n_arr:]
        x, y, c = _coords()
        out = [pltpu.make_async_remote_copy(b.at[c], b.at[c], send_sems.at[q], recv_sems.at[q],
                                            device_id=(x, y, 1 - c), device_id_type=MESH)
               for q, b in enumerate(buf_refs)]
        for cp in out:
            cp.start()
        for q, b in enumerate(buf_refs):
            pltpu.make_async_remote_copy(b.at[1 - c], b.at[1 - c], send_sems.at[q], recv_sems.at[q],
                                         device_id=(x, y, 1 - c), device_id_type=MESH).wait_recv()
        for cp in out:
            cp.wait_send()

    return _pcall(
        body, name=name, in_specs=[ANY_SPEC] * n_arr, out_specs=[ANY_SPEC] * n_arr,
        out_shape=[jax.ShapeDtypeStruct(b.shape, b.dtype) for b in bufs],
        input_output_aliases={q: q for q in range(n_arr)},
        scratch_shapes=[pltpu.SemaphoreType.DMA((n_arr,)), pltpu.SemaphoreType.DMA((n_arr,))],
    )(*bufs)


HBM_SPEC = pl.BlockSpec(memory_space=pltpu.HBM)
SEM_SPEC = pl.BlockSpec(memory_space=pltpu.SEMAPHORE)
DATAFLOW = pltpu.SideEffectType.DATAFLOW_SIDE_EFFECTING


def _in_hbm(a):
    return pltpu.with_memory_space_constraint(a, pltpu.HBM)


def push_start(name, srcs, lands, plan_fn, after):
    n_arr = len(srcs)
    n_send = len(plan_fn(0, 0, 0))
    arrays = list(srcs) + ([] if lands is None else list(lands))
    if after is None:
        after = jnp.zeros((8, LANES), F32)
    n_all = len(arrays)

    def body(*refs):
        src_refs, land_refs = refs[:n_arr], refs[n_all - n_arr:n_all]
        send_sems, recv_sems = refs[n_all + 1:n_all + 3]
        token = refs[-1]
        sends = plan_fn(*_coords())
        for q, (src_ref, land_ref) in enumerate(zip(src_refs, land_refs)):
            for k, (si, peer, di, _) in enumerate(sends):
                pltpu.make_async_remote_copy(src_ref.at[si], land_ref.at[di], send_sems.at[q * n_send + k],
                                             recv_sems.at[q * n_send + k], device_id=peer,
                                             device_id_type=MESH).start()
        token[...] = jnp.zeros_like(token)

    sems = pltpu.SemaphoreType.DMA((n_arr * n_send,))
    out = _pcall(
        body, name=name,
        in_specs=[HBM_SPEC] * n_all + [ANY_SPEC],
        out_specs=[SEM_SPEC, SEM_SPEC] + [HBM_SPEC] * n_all + [pl.BlockSpec(memory_space=pltpu.VMEM)],
        out_shape=[sems, sems] + [pltpu.HBM(a.shape, a.dtype) for a in arrays]
        + [jax.ShapeDtypeStruct((8, LANES), F32)],
        input_output_aliases={q: 2 + q for q in range(n_all)},
        compiler_params=pltpu.CompilerParams(has_side_effects=DATAFLOW),
    )(*[_in_hbm(a) for a in arrays], after)
    return out[:-1], out[-1]


def push_wait(name, state, plan_fn, after, in_place=False, both=False):
    arrays = list(state[2:])
    n_all = len(arrays)
    n_arr = n_all if in_place else n_all // 2
    n_send = len(plan_fn(0, 0, 0))
    send_sems, recv_sems = state[:2]

    def body(*refs):
        src_refs, land_refs = refs[:n_arr], refs[n_all - n_arr:n_all]
        send_ref, recv_ref = refs[n_all:n_all + 2]
        sends = plan_fn(*_coords())
        for q, (src_ref, land_ref) in enumerate(zip(src_refs, land_refs)):
            for k, (si, peer, _, ri) in enumerate(sends):
                cp = pltpu.make_async_remote_copy(src_ref.at[si], land_ref.at[ri], send_ref.at[q * n_send + k],
                                                  recv_ref.at[q * n_send + k], device_id=peer, device_id_type=MESH)
                cp.wait_send()
                cp.wait_recv()

    out = _pcall(
        body, name=name,
        in_specs=[HBM_SPEC] * n_all + [SEM_SPEC, SEM_SPEC, ANY_SPEC],
        out_specs=[HBM_SPEC] * n_all,
        out_shape=[pltpu.HBM(a.shape, a.dtype) for a in arrays],
        input_output_aliases={q: q for q in range(n_all)},
        compiler_params=pltpu.CompilerParams(has_side_effects=DATAFLOW),
    )(*arrays, send_sems, recv_sems, after)
    return list(out) if both else list(out[n_all - n_arr:])


def plan_chips_gather(x, y, c):
    me = 2 * x + y
    return [(c, (px, py, c), 2 * me + c, 2 * (2 * px + py) + c) for px, py in _other_chips(x, y)]


def plan_forward(x, y, c):
    return [(2 * (2 * px + py) + c, (x, y, 1 - c), 2 * (2 * px + py) + c, 2 * (2 * px + py) + (1 - c))
            for px, py in _other_chips(x, y)]


def add_kept(g8, other, name):
    _, rows, cols = other.shape
    tr = _div_tile(rows, 512, 16)

    def body(c_ref, a_ref, b_ref, o_ref):
        o_ref[...] = (a_ref[...].astype(F32) + b_ref[...].astype(F32)).astype(BF16)

    spec = pl.BlockSpec((None, tr, cols), lambda s, i, c_ref: (s, i, 0))
    grid_spec = pltpu.PrefetchScalarGridSpec(
        num_scalar_prefetch=1, grid=(4, rows // tr),
        in_specs=[pl.BlockSpec((None, tr, cols), lambda s, i, c_ref: (2 * s + c_ref[0], i, 0)), spec],
        out_specs=spec)
    return _pcall(body, name=name, grid_spec=grid_spec, out_shape=jax.ShapeDtypeStruct((4, rows, cols), BF16),
                  compiler_params=_cparams(("parallel", "parallel")))(
        lax.axis_index("c").astype(jnp.int32).reshape(1), g8, other)


def reduce_scatter_begin(grads, names, tag):
    g8 = [g.reshape(8, g.shape[1] // 2, g.shape[2]) for g in grads]
    return _chip_sums_start(g8, push(tag + "_sibling", g8, 4, plan_sibling_quarters), names, tag)


def _chip_sums_start(g8, from_sibling, names, tag):
    chip_sum = [add_kept(a, b, tag + "_add_" + n) for a, b, n in zip(g8, from_sibling, names)]
    return push_start(tag + "_start", chip_sum, chip_sum, plan_chips_by_shard, None)


def sibling_exchange_begin(grads, tag):
    g8 = [g.reshape(8, g.shape[1] // 2, g.shape[2]) for g in grads]
    lands = [lax.empty((4,) + g.shape[1:], g.dtype) for g in g8]
    return push_start(tag + "_sib_start", g8, lands, plan_sibling_quarters, None)


def reduce_scatter_continue(state, names, tag, after):
    n = len(names)
    arrays = push_wait(tag + "_sib_wait", state, plan_sibling_quarters, after, both=True)
    return _chip_sums_start(arrays[:n], arrays[n:], names, tag)


def reduce_scatter_end(state, names, tag, after):
    by_chip = push_wait(tag + "_wait", state, plan_chips_by_shard, after)
    both = swap_halves(tag + "_halves", [sum_slots(q, tag + "_sum_" + n, for_swap=True)
                                         for q, n in zip(by_chip, names)])
    return [b.reshape(2 * b.shape[1], b.shape[2]) for b in both]


def _landing(mine):
    return [jnp.broadcast_to(m[None], (4,) + m.shape).reshape((8,) + m.shape[1:]) for m in mine]


def gather_begin(tag, mine, after, lands=None):
    return push_start(tag + "_start", mine, _landing(mine) if lands is None else lands, plan_chips_gather, after)


def gather_forward(tag, state, after):
    lands = push_wait(tag + "_wait", state, plan_chips_gather, after)
    return push_start(tag + "_fwd_start", lands, None, plan_forward, None)


def gather_end(tag, state, after):
    bufs = push_wait(tag + "_fwd_wait", state, plan_forward, after, in_place=True)
    return [g.reshape(4, 2 * g.shape[1], g.shape[2]) for g in bufs]


PACK_QUANTUM = 8 * C


def _pack(parts, dtype):
    pieces = []
    for p in parts:
        flat = p.astype(dtype).reshape(-1)
        pieces.append(jnp.pad(flat, (0, -flat.shape[0] % PACK_QUANTUM)).reshape(-1, C))
    rows = sum(p.shape[0] for p in pieces)
    if rows % 16:
        pieces.append(jnp.zeros((8, C), dtype))
    return jnp.concatenate(pieces, axis=0)


def _unpack(packed, shapes):
    out, at = [], 0
    for s in shapes:
        n = int(np.prod(s))
        rows = -(-n // PACK_QUANTUM) * 8
        out.append(packed[at:at + rows].reshape(-1)[:n].reshape(s))
        at += rows
    return out


def all_reduce_begin(vec):
    rows, width = vec.shape
    (pair,) = swap_halves("ar_sibling", [jnp.broadcast_to(vec[None], (2, rows, width))])
    chip_sum = sum_slots(pair, "ar_sum_sibling").reshape(2, rows // 2, width)
    mine = lax.dynamic_index_in_dim(chip_sum, lax.axis_index("c"), axis=0, keepdims=True)
    return push_start("ar_start", [chip_sum], [jnp.broadcast_to(mine, (4, rows // 2, width))], plan_chips_half,
                      None)


def all_reduce_end(state, after):
    (by_chip,) = push_wait("ar_wait", state, plan_chips_half, after)
    (both,) = swap_halves("ar_halves", [sum_slots(by_chip, "ar_sum_chips", for_swap=True)])
    return both.reshape(2 * both.shape[1], both.shape[2])


def kernel(x, norm1_g, w_in, shift_mu, w0, w2, a0, a2, g2, k_k, k_a, r_k, gn_w, gn_b, sgu_ln_g, sgu_ln_b, sgu_w, sgu_b, sgu_out_g, w_out, norm2_g, w_gate, w_up, w_down, final_g, loss_target, m_norm1_g, m_w_in, m_shift_mu, m_w0, m_w2, m_a0, m_a2, m_g2, m_k_k, m_k_a, m_r_k, m_gn_w, m_gn_b, m_sgu_ln_g, m_sgu_ln_b, m_sgu_w, m_sgu_b, m_sgu_out_g, m_w_out, m_norm2_g, m_w_gate, m_w_up, m_w_down, m_final_g, v_norm1_g, v_w_in, v_shift_mu, v_w0, v_w2, v_a0, v_a2, v_g2, v_k_k, v_k_a, v_r_k, v_gn_w, v_gn_b, v_sgu_ln_g, v_sgu_ln_b, v_sgu_w, v_sgu_b, v_sgu_out_g, v_w_out, v_norm2_g, v_w_gate, v_w_up, v_w_down, v_final_g):
    args = dict(locals())
    names = ["norm1_g", "w_in", "shift_mu", "w0", "w2", "a0", "a2", "g2", "k_k", "k_a", "r_k", "gn_w", "gn_b",
             "sgu_ln_g", "sgu_ln_b", "sgu_w", "sgu_b", "sgu_out_g", "w_out", "norm2_g", "w_gate", "w_up", "w_down",
             "final_g"]
    big = ["w_in", "w_out", "w_gate", "w_up", "w_down"]
    xi, yi = lax.axis_index("x"), lax.axis_index("y")
    chip = 2 * xi + yi
    x2d, tgt = x[0], loss_target[0]
    t = x2d.shape[0]

    lora_names = ["w2", "a2", "g2"]
    first, later = ["w_in"] + lora_names, ["w_out", "w_gate", "w_up", "w_down"]

    def halves(n, token=None):
        m = args[n][0] if token is None else args[n][0] + token[0, 0]
        m = m.astype(BF16)
        return m.reshape(2, m.shape[0] // 2, m.shape[1])

    first_state, first_token = gather_begin("g0", [halves(n) for n in first], norm1_g)
    later_mine = [halves(n, first_token) for n in later]
    later_lands = _landing(later_mine)
    (h1,) = row_fwd("norm1", rms_fn, [Tiled(x2d)], [norm1_g + first_token[0, 0]], [], [(D, BF16)], _tile(t, 256))
    prepared = functools.reduce(lambda p, q: p + q, [a[0, 0, :1].astype(F32) for a in later_lands])
    first_state, first_token = gather_forward("g0", first_state, prepared + h1[0, :1].astype(F32))
    got = gather_end("g0", first_state, first_token)
    later_state, later_token = gather_begin("g1", later_mine, got[0], later_lands)
    full = {n: blk.transpose(1, 0, 2).reshape(blk.shape[1], 4 * blk.shape[2]) for n, blk in zip(first[1:], got[1:])}
    s0, s1, s2, s3 = got[0]
    w_in_p = jnp.concatenate([s0, s1, s2[:, :CUT_A], s2[:, CUT_B:], s3, s2[:, CUT_A:CUT_B],
                              jnp.zeros((D, P_W - D_IN), BF16)], axis=1)
    mu_rkv = shift_mu[:, :3 * C] + later_token[0, 0]
    mu_lora = jnp.pad(shift_mu[:, 3 * C:], ((0, 0), (0, LORA_W - 288)))
    w2p = jnp.pad(full["w2"], ((0, 64), (0, 0)))
    a2p = jnp.pad(full["a2"], ((64, 0), (0, 0)))
    g2p = jnp.pad(full["g2"], ((0, 96), (0, 0)))
    ii = np.arange(C)
    jm = _scan_consts()[2]
    e_np = (np.arange(LANES)[:, None] == (ii[None, :] // HEAD))
    e_mat, e_t = jnp.asarray(e_np, BF16), jnp.asarray(e_np.T, BF16)
    b_t = jnp.pad(sgu_b[0].T, ((0, 0), (0, LANES - HEADS)))
    r_k_flat = r_k.reshape(1, C)
    tm_row = _tile(t, 256)
    tm_sgu_b = _tile(t, 128)

    p = mm(h1, w_in_p, name="proj_in")
    ps_rkv = shift_fwd(p, mu_rkv, 0, "shift_rkv")
    ps_lora = shift_fwd(p, mu_lora, P_LORA // LORA_W, "shift_lora")
    pre_par = [w0, w2p, a0, a2p, g2p, k_k, k_a]
    r_, w_, k2, v_, an, b_, g_ = row_fwd("pre", pre_fn, [Tiled(ps_rkv), Tiled(ps_lora)], pre_par, [jm],
                                         [(C, F32)] * 7, tm_row)
    v3 = to_cols(v_)
    y3, ckpt = wkv_fwd(r_, w_, k2, v3, an, b_)
    later_state, later_token = gather_forward("g1", later_state, y3)
    y_ = from_cols(y3 + later_token[0, 0])
    post_in = [Tiled(z) for z in (y_, r_, k2, v_, g_)]
    post_par = [gn_w, gn_b, r_k_flat]
    (y_rwkv,) = row_fwd("post", post_fn, post_in, post_par, [jm], [(C, BF16)], tm_row)
    sgu_in = [Tiled(p, C, 3), Tiled(p, C, 4)]
    sgu_par = [sgu_ln_g, sgu_ln_b, sgu_w[0], b_t, sgu_out_g]
    (y_sgu,) = row_fwd("sgu", sgu_fn, sgu_in, sgu_par, [e_mat, e_t], [(C, BF16)], tm_row)
    y_cat = jnp.concatenate([y_rwkv, y_sgu], axis=1)
    for n, blk in zip(later, gather_end("g1", later_state, y_cat)):
        full[n] = blk if n in ("w_gate", "w_up") else blk.reshape(4 * blk.shape[1], blk.shape[2])
    x_mid = mm(y_cat, full["w_out"], res=x2d, name="proj_out")
    (h2,) = row_fwd("norm2", rms_fn, [Tiled(x_mid)], [norm2_g], [], [(D, BF16)], tm_row)
    gate = mm(h2, full["w_gate"], shards="b", name="ffn_gate")
    up, act = mm(h2, full["w_up"], shards="b", name="ffn_up",
                 post=(lambda up_tile, gate_tile: (up_tile, swiglu_fn(gate_tile, up_tile)[0]), [gate], [F32, BF16]))
    x_out = mm(act, full["w_down"], res=x_mid, name="ffn_down")
    loss_part, dx_out, d_final_g, dx_out_b = loss_head(x_out, tgt, final_g.reshape(1, D))

    grads = {"final_g": d_final_g.reshape(D)}
    d_gate, d_up = mm(dx_out_b, full["w_down"], tb=True, name="d_act",
                      post=(lambda d_act, g_tile, u_tile: jax.vjp(swiglu_fn, g_tile, u_tile)[1]((d_act,)), [gate, up],
                            [BF16, BF16]))
    grads["w_down"] = mm(act, dx_out_b, ta=True, out_dtype=BF16, name="d_w_down")
    grads["w_gate"] = mm(h2, d_gate, shards="out", out_dtype=BF16, name="d_w_gate")
    grads["w_up"] = mm(h2, d_up, shards="out", out_dtype=BF16, name="d_w_up")
    d_h2 = mm(d_gate, full["w_gate"], shards="bt", name="d_h2_gate")
    d_h2 = mm(d_up, full["w_up"], shards="bt", res=d_h2, name="d_h2_up")
    dx_mid, grads["norm2_g"], dx_mid_b = row_bwd("norm2_b", rms_res_fn, [Tiled(x_mid)], [norm2_g], [],
                                                 [Tiled(d_h2), Tiled(dx_out)], tm_row, bf16_copy_of=(0,))
    d_ycat = mm(dx_mid_b, full["w_out"], tb=True, name="d_ycat")
    grads["w_out"] = mm(y_cat, dx_mid_b, ta=True, out_dtype=BF16, name="d_w_out")
    rs_state, rs_token = sibling_exchange_begin([grads[n].reshape((4,) + args[n].shape[1:]) for n in later], "rs2")
    d_pu, d_pv, grads["sgu_ln_g"], grads["sgu_ln_b"], d_sgu_w, d_b_t, grads["sgu_out_g"] = row_bwd(
        "sgu_b", sgu_fn, sgu_in, sgu_par[:-1] + [sgu_out_g + rs_token[0, 0]], [e_mat, e_t], [Tiled(d_ycat, C, 1)],
        tm_sgu_b, tiled_out_dtypes=[BF16, BF16])
    grads["sgu_w"] = d_sgu_w[None]
    grads["sgu_b"] = d_b_t[:, :HEADS].T[None]
    post_par_b = [gn_w + rs_token[0, 0]] + post_par[1:]
    d_y, d_r1, d_k1, d_v1, d_g, grads["gn_w"], grads["gn_b"], d_r_k = row_bwd(
        "post_b", post_fn, post_in, post_par_b, [jm], [Tiled(d_ycat, C, 0)], tm_row)
    grads["r_k"] = d_r_k.reshape(r_k.shape)
    rs_state, rs_token = reduce_scatter_continue(rs_state, later, "rs2", d_y[0, :1] + d_pu[0, :1].astype(F32))
    d_r2, d_w, d_k2, d_a, d_b, d_v3 = wkv_bwd(r_, w_, k2, v3, an, b_, ckpt, to_cols(d_y + rs_token[0, 0]))
    d_v2 = from_cols(d_v3)
    pre_cts = [(Tiled(d_r1), Tiled(d_r2)), Tiled(d_w), (Tiled(d_k1), Tiled(d_k2)), (Tiled(d_v1), Tiled(d_v2)),
               Tiled(d_a), Tiled(d_b), Tiled(d_g)]
    d_ps_rkv, d_ps_lora, grads["w0"], d_w2p, grads["a0"], d_a2p, d_g2p, grads["k_k"], grads["k_a"] = row_bwd(
        "pre_b", pre_fn, [Tiled(ps_rkv), Tiled(ps_lora)], pre_par, [jm], pre_cts, tm_row)
    grads["w2"], grads["a2"], grads["g2"] = d_w2p[None, :64], d_a2p[None, 64:], d_g2p[None, :160]
    d_p_rkv, d_mu_rkv = shift_bwd(d_ps_rkv, p, mu_rkv, 0, "shift_rkv_b")
    d_p_lora, d_mu_lora = shift_bwd(d_ps_lora, p, mu_lora, P_LORA // LORA_W, "shift_lora_b")
    grads["shift_mu"] = jnp.concatenate([d_mu_rkv, d_mu_lora[:, :288]], axis=1)
    d_p = jnp.concatenate([d_p_rkv, d_pu, d_pv, d_p_lora], axis=1)
    d_h1 = mm(d_p, w_in_p, tb=True, name="d_h1")
    d_w_in_p = mm(h1, d_p, ta=True, out_dtype=BF16, name="d_w_in")
    ws = D_IN // 4
    shard2 = jnp.concatenate([d_w_in_p[:, 2 * ws:3 * C], d_w_in_p[:, P_LORA:P_LORA + 288],
                              d_w_in_p[:, 3 * C:3 * C + ws - CUT_B]], axis=1)
    grads["w_in"] = jnp.stack([d_w_in_p[:, :ws], d_w_in_p[:, ws:2 * ws], shard2,
                               d_w_in_p[:, 3 * C + ws - CUT_B:P_LORA]])
    dx, grads["norm1_g"] = row_bwd("norm1_b", rms_res_fn, [Tiled(x2d)], [norm1_g], [],
                                   [Tiled(d_h1), Tiled(dx_mid)], tm_row)

    small = [n for n in names if n not in big]
    small_shapes = [(1, 64, C) if n in ("w2", "a2") else (1, 160, C) if n == "g2" else args[n].shape for n in small]
    vec = _pack([grads[n] for n in small], F32)
    w_in_state, w_in_token = reduce_scatter_begin([grads["w_in"]], ["w_in"], "rs1")
    small_state, small_token = all_reduce_begin(vec)
    total = {n: g[None] for n, g in zip(later, reduce_scatter_end(rs_state, later, "rs2", w_in_token + small_token))}
    loss = lax.psum(loss_part[0, 0], ("x", "y", "c"))
    delta, new_m, new_v = {}, {}, {}

    def adam_big(n):
        s = args[n].shape
        view = (lambda a: a[0].T) if n == "w_in" else (lambda a: a[0])
        back = (lambda a: a.T.reshape(s)) if n == "w_in" else (lambda a: a.reshape(s))
        d_, m_, v__ = adamw(view(args[n]), view(total[n]), view(args["m_" + n]), view(args["v_" + n]), "adamw_" + n)
        delta[n], new_m[n], new_v[n] = back(d_), back(m_), back(v__)

    for n in later:
        adam_big(n)
    total["w_in"] = reduce_scatter_end(w_in_state, ["w_in"], "rs1", new_v["w_down"])[0][None]
    for n, g_sum in zip(small, _unpack(all_reduce_end(small_state, new_v["w_down"]), small_shapes)):
        total[n] = lax.dynamic_slice_in_dim(g_sum, chip * 256, 256, axis=2) if n in lora_names else g_sum
    adam_big("w_in")
    flat = [_pack([src[n] for n in small], F32)
            for src in (args, total, {n: args["m_" + n] for n in small}, {n: args["v_" + n] for n in small})]
    outs = adamw(*flat, "adamw_small")
    for res, o in zip((delta, new_m, new_v), outs):
        res.update(zip(small, _unpack(o, [args[n].shape for n in small])))
    return (loss, dx[None], *[total[n] for n in names], *[delta[n] for n in names],
            *[new_m[n] for n in names], *[new_v[n] for n in names])
```

```python
import functools

import numpy as np
import jax
import jax.numpy as jnp
from jax import lax
from jax.experimental import pallas as pl
from jax.experimental.pallas import tpu as pltpu

F32 = jnp.float32
BF16 = jnp.bfloat16

D = 2048
C = 1024
HEADS = 16
HEAD = 64
LANES = 128
P_W = 5632
P_LORA = 5120
LORA_W = 512
D_IN = 5408
CUT_A = 3 * C - 2 * (D_IN // 4)
CUT_B = CUT_A + 288
CHUNK = 128
RMS_EPS = 1e-6
LN_EPS = 1e-5
GN_EPS = 64e-5
L2_EPS = 1e-12
VMEM_LIMIT = 56 * 1024 * 1024


def _pcall(body, **kw):
    return pl.pallas_call(body, **kw)


def _cparams(sem):
    return pltpu.CompilerParams(dimension_semantics=sem, vmem_limit_bytes=VMEM_LIMIT)


def _tile(n, most):
    t = most
    while t > 8 and n % t:
        t //= 2
    assert n % t == 0, (n, most)
    return t


MM_TILE = (2048, 512, 2048)
MM_VMEM = 40 * 1024 * 1024


def _div_tile(n, most, quantum=LANES):
    for t in range(min(n, most) // quantum * quantum, 0, -quantum):
        if n % t == 0:
            return t
    raise ValueError((n, most, quantum))


def _mm_tiles(m, n, k, a_bytes, b_bytes, r_bytes, o_bytes):
    tm, tn, tk = _div_tile(m, MM_TILE[0]), _div_tile(n, MM_TILE[1]), _div_tile(k, MM_TILE[2])

    def need(tm, tk):
        return 2 * (tm * tk * a_bytes + tk * tn * b_bytes + tm * tn * (r_bytes + o_bytes)) + tm * tn * 4

    while need(tm, tk) > MM_VMEM:
        if tk >= tm and tk > 512:
            tk = _div_tile(k, tk - LANES)
        else:
            tm = _div_tile(m, tm - LANES)
    return tm, tn, tk


def mm(a, b, *, ta=False, tb=False, res=None, out_dtype=F32, name, shards=None, post=None):
    if shards is not None:
        return _mm_shards(a, b, res, out_dtype, name, shards, post)
    m, k = (a.shape[1], a.shape[0]) if ta else a.shape
    n = b.shape[0] if tb else b.shape[1]
    assert (b.shape[1] if tb else b.shape[0]) == k
    if post is None:
        tm, tn, tk = _mm_tiles(m, n, k, a.dtype.itemsize, b.dtype.itemsize,
                               0 if res is None else res.dtype.itemsize, jnp.dtype(out_dtype).itemsize)
    else:
        tm, tn, tk = _div_tile(m, 1024), _div_tile(n, MM_TILE[1]), _div_tile(k, MM_TILE[2])
    nk = k // tk
    dims = (((0 if ta else 1,), (1 if tb else 0,)), ((), ()))
    a_spec = pl.BlockSpec((tk, tm), lambda i, j, l: (l, i)) if ta else pl.BlockSpec((tm, tk), lambda i, j, l: (i, l))
    b_spec = pl.BlockSpec((tn, tk), lambda i, j, l: (j, l)) if tb else pl.BlockSpec((tk, tn), lambda i, j, l: (l, j))
    o_spec = pl.BlockSpec((tm, tn), lambda i, j, l: (i, j))
    return _mm_call(a, b, res, dims, (m // tm, n // tn, nk), a_spec, b_spec, o_spec, o_spec, (tm, tn),
                    jax.ShapeDtypeStruct((m, n), out_dtype), name, post)


def _mm_shards(a, b, res, out_dtype, name, shards, post=None):
    assert post is None or shards == "b"
    if shards == "b":
        (m, k), ns = a.shape, b.shape[2]
        tm, tk = _div_tile(m, 1024 if post is None else 512), _div_tile(k, 2048)
        grid, dims, acc = (m // tm, 4, k // tk), (((1,), (0,)), ((), ())), (tm, ns)
        a_spec = pl.BlockSpec((tm, tk), lambda i, s, l: (i, l))
        b_spec = pl.BlockSpec((None, tk, ns), lambda i, s, l: (s, l, 0))
        o_spec = pl.BlockSpec((tm, ns), lambda i, s, l: (i, s))
        out = jax.ShapeDtypeStruct((m, 4 * ns), out_dtype)
    elif shards == "bt":
        m, (_, n, ns) = a.shape[0], b.shape
        tm, tn = _div_tile(m, 2048), _div_tile(n, 512)
        grid, dims, acc = (m // tm, n // tn, 4), (((1,), (1,)), ((), ())), (tm, tn)
        a_spec = pl.BlockSpec((tm, ns), lambda i, j, s: (i, s))
        b_spec = pl.BlockSpec((None, tn, ns), lambda i, j, s: (s, j, 0))
        o_spec = pl.BlockSpec((tm, tn), lambda i, j, s: (i, j))
        out = jax.ShapeDtypeStruct((m, n), out_dtype)
    else:
        (t, m), ns = a.shape, b.shape[1] // 4
        tm, tk = _div_tile(m, 1024), _div_tile(t, 2048)
        grid, dims, acc = (m // tm, 4, t // tk), (((0,), (0,)), ((), ())), (tm, ns)
        a_spec = pl.BlockSpec((tk, tm), lambda i, s, l: (l, i))
        b_spec = pl.BlockSpec((tk, ns), lambda i, s, l: (l, s))
        o_spec = pl.BlockSpec((None, tm, ns), lambda i, s, l: (s, i, 0))
        out = jax.ShapeDtypeStruct((4, m, ns), out_dtype)
    return _mm_call(a, b, res, dims, grid, a_spec, b_spec, o_spec, o_spec, acc, out, name, post)


def _mm_call(a, b, res, dims, grid, a_spec, b_spec, r_spec, o_spec, acc_shape, out, name, post=None):
    nk = grid[2]
    post_fn, post_in, post_dtypes = (None, [], [out.dtype]) if post is None else post
    n_extra = (res is not None) + len(post_in)

    def body(*refs):
        a_ref, b_ref = refs[:2]
        r_ref = None if res is None else refs[2]
        p_refs = refs[2 + (res is not None):2 + n_extra]
        o_refs = refs[2 + n_extra:2 + n_extra + len(post_dtypes)]

        def finish(acc):
            if r_ref is not None:
                acc = acc + r_ref[...].astype(F32)
            outs = (acc,) if post_fn is None else post_fn(acc, *[p[...] for p in p_refs])
            for o_ref, o in zip(o_refs, outs):
                o_ref[...] = o.astype(o_ref.dtype)

        prod = lax.dot_general(a_ref[...].astype(BF16), b_ref[...].astype(BF16), dims, preferred_element_type=F32)
        if nk == 1:
            finish(prod)
            return
        acc_ref = refs[-1]
        kk = pl.program_id(2)

        @pl.when(kk == 0)
        def _():
            acc_ref[...] = prod

        @pl.when(kk > 0)
        def _():
            acc_ref[...] += prod

        @pl.when(kk == nk - 1)
        def _():
            finish(acc_ref[...])

    in_specs = [a_spec, b_spec] + ([r_spec] if res is not None else []) + [o_spec] * len(post_in)
    args = (a, b) + ((res,) if res is not None else ()) + tuple(post_in)
    outs = _pcall(
        body, name=name, grid=grid, in_specs=in_specs, out_specs=[o_spec] * len(post_dtypes),
        out_shape=[jax.ShapeDtypeStruct(out.shape, dt) for dt in post_dtypes],
        scratch_shapes=[] if nk == 1 else [pltpu.VMEM(acc_shape, F32)],
        compiler_params=_cparams(("parallel", "parallel", "arbitrary")),
    )(*args)
    return outs[0] if post is None else outs


class Tiled:
    def __init__(self, arr, width=None, col=0):
        self.arr, self.width, self.col = arr, (arr.shape[1] if width is None else width), col

    def spec(self, tm):
        col = self.col
        return pl.BlockSpec((tm, self.width), lambda i: (i, col))


def _full_spec(p):
    nd = p.ndim
    return pl.BlockSpec(p.shape, lambda i: (0,) * nd)


def row_fwd(name, fn, tiled, params, consts, outs, tm):
    t = tiled[0].arr.shape[0]
    n_in = len(tiled) + len(params) + len(consts)

    def body(*refs):
        res = fn(*[r[...] for r in refs[:n_in]])
        for o_ref, r in zip(refs[n_in:], res):
            o_ref[...] = r.astype(o_ref.dtype)

    return _pcall(
        body, name=name, grid=(t // tm,),
        in_specs=[x.spec(tm) for x in tiled] + [_full_spec(p) for p in params + consts],
        out_specs=[pl.BlockSpec((tm, w), lambda i: (i, 0)) for w, _ in outs],
        out_shape=[jax.ShapeDtypeStruct((t, w), dt) for w, dt in outs],
        compiler_params=_cparams(("parallel",)),
    )(*[x.arr for x in tiled], *params, *consts)


def row_bwd(name, fn, tiled, params, consts, cts, tm, tiled_out_dtypes=None, bf16_copy_of=()):
    t = tiled[0].arr.shape[0]
    nt, npar, ncon = len(tiled), len(params), len(consts)
    cts = [c if isinstance(c, tuple) else (c,) for c in cts]
    flat_cts = [x for c in cts for x in c]
    tiled_out_dtypes = tiled_out_dtypes or [F32] * nt
    copies = [tiled[j] for j in bf16_copy_of]

    def body(*refs):
        n_in = nt + npar + ncon
        ins = [r[...].astype(F32) for r in refs[:nt + npar]]
        con = [r[...] for r in refs[nt + npar:n_in]]
        ct_refs = list(refs[n_in:n_in + len(flat_cts)])
        o = refs[n_in + len(flat_cts):]
        ct = []
        for c in cts:
            parts = [ct_refs.pop(0)[...].astype(F32) for _ in c]
            ct.append(functools.reduce(lambda p, q: p + q, parts))
        _, vjp = jax.vjp(lambda *a: fn(*a, *con), *ins)
        g = vjp(tuple(ct))
        for j in range(nt):
            o[j][...] = g[j].astype(o[j].dtype)
        for n, j in enumerate(bf16_copy_of):
            o[nt + npar + n][...] = g[j].astype(BF16)
        first = pl.program_id(0) == 0

        @pl.when(first)
        def _():
            for j in range(npar):
                o[nt + j][...] = g[nt + j]

        @pl.when(jnp.logical_not(first))
        def _():
            for j in range(npar):
                o[nt + j][...] += g[nt + j]

    return _pcall(
        body, name=name, grid=(t // tm,),
        in_specs=[x.spec(tm) for x in tiled] + [_full_spec(p) for p in params + consts]
        + [x.spec(tm) for x in flat_cts],
        out_specs=[pl.BlockSpec((tm, x.width), lambda i: (i, 0)) for x in tiled] + [_full_spec(p) for p in params]
        + [pl.BlockSpec((tm, x.width), lambda i: (i, 0)) for x in copies],
        out_shape=[jax.ShapeDtypeStruct((t, x.width), dt) for x, dt in zip(tiled, tiled_out_dtypes)]
        + [jax.ShapeDtypeStruct(p.shape, F32) for p in params]
        + [jax.ShapeDtypeStruct((t, x.width), BF16) for x in copies],
        compiler_params=_cparams(("arbitrary",)),
    )(*[x.arr for x in tiled], *params, *consts, *[x.arr for x in flat_cts])


def _split_dot(x, w):
    hi = x.astype(BF16)
    lo = (x - hi.astype(F32)).astype(BF16)
    return jnp.dot(hi, w, preferred_element_type=F32) + jnp.dot(lo, w, preferred_element_type=F32)


def _headsum_blocks(x, j2k):
    out = []
    for c0 in range(0, x.shape[1], LANES):
        blk = x[:, c0:c0 + LANES]
        hi = blk.astype(BF16)
        lo = (blk - hi.astype(F32)).astype(BF16)
        out.append(jnp.dot(jnp.concatenate([hi, lo], axis=1), j2k, preferred_element_type=F32))
    return jnp.concatenate(out, axis=1)


@jax.custom_vjp
def headsum(x, j2k):
    return _headsum_blocks(x, j2k)


def _headsum_fwd(x, j2k):
    return _headsum_blocks(x, j2k), j2k


def _headsum_bwd(j2k, ct):
    return _headsum_blocks(ct, j2k), jnp.zeros_like(j2k)


headsum.defvjp(_headsum_fwd, _headsum_bwd)


def _bdot(x, w):
    return jnp.dot(x.astype(BF16), w.astype(BF16), preferred_element_type=F32)


def _sigmoid(x):
    return jax.nn.sigmoid(x)


def _softplus(x):
    return jnp.maximum(x, 0.0) + jnp.log(1.0 + jnp.exp(-jnp.abs(x)))


def rms_fn(x, g):
    return (x * lax.rsqrt(jnp.mean(x * x, axis=-1, keepdims=True) + RMS_EPS) * g,)


def rms_res_fn(x, g):
    return rms_fn(x, g)[0], x


def pre_fn(rkv, lora, w0, w2p, a0, a2p, g2p, k_k, k_a, jm):
    r, k, v = rkv[:, :C], rkv[:, C:2 * C], rkv[:, 2 * C:]
    zwa, zg = lora[:, :LANES], lora[:, LANES:LANES + 256]
    w_log = -_softplus(-(w0 + _bdot(jnp.tanh(zwa), w2p))) - 0.5
    decay = jnp.exp(-jnp.exp(w_log))
    a = _sigmoid(a0 + _bdot(zwa, a2p))
    g = _bdot(_sigmoid(zg), g2p)
    kk = k * k_k
    kk = kk / jnp.maximum(jnp.sqrt(headsum(kk * kk, jm)), L2_EPS)
    k2 = k * (1.0 + (a - 1.0) * k_a)
    return r, decay, k2, v, -kk, kk * a, g


def post_fn(y, r, k2, v, g, gn_w, gn_b, r_k, jm):
    mu = headsum(y, jm) * (1.0 / HEAD)
    yc = y - mu
    var = headsum(yc * yc, jm) * (1.0 / HEAD)
    yn = yc * lax.rsqrt(var + GN_EPS) * gn_w + gn_b
    bonus = headsum(r * k2 * r_k, jm) * v
    return ((yn + bonus) * g,)


def _gelu(x):
    return 0.5 * x * (1.0 + lax.erf(x * np.float32(1.0 / np.sqrt(2.0))))


@jax.custom_vjp
def expand_groups(b_t, e, e_t):
    return _split_dot(b_t, e)


def _expand_groups_fwd(b_t, e, e_t):
    return _split_dot(b_t, e), (e, e_t)


def _expand_groups_bwd(res, ct):
    e, e_t = res
    return _split_dot(ct, e_t), jnp.zeros_like(e), jnp.zeros_like(e_t)


expand_groups.defvjp(_expand_groups_fwd, _expand_groups_bwd)


def sgu_fn(pu, pv, ln_g, ln_b, w_s, b_t, out_g, e, e_t):
    rows = pu.shape[0]
    b_exp = expand_groups(b_t, e, e_t)
    u, v = _gelu(pu), _gelu(pv)
    mu = jnp.mean(v, axis=-1, keepdims=True)
    vc = v - mu
    var = jnp.mean(vc * vc, axis=-1, keepdims=True)
    v = vc * lax.rsqrt(var + LN_EPS) * ln_g + ln_b
    tri = lax.broadcasted_iota(jnp.int32, (CHUNK, CHUNK), 0) >= lax.broadcasted_iota(jnp.int32, (CHUNK, CHUNK), 1)
    left = lax.broadcasted_iota(jnp.int32, (CHUNK, LANES), 1) < HEAD
    chunks = []
    for c0 in range(0, rows, CHUNK):
        cols = []
        for gp in range(HEADS // 2):
            vp = v[c0:c0 + CHUNK, gp * LANES:(gp + 1) * LANES]
            wa = jnp.where(tri, w_s[2 * gp], 0.0)
            wb = jnp.where(tri, w_s[2 * gp + 1], 0.0)
            cols.append(jnp.where(left, _bdot(wa, vp), _bdot(wb, vp)))
        chunks.append(jnp.concatenate(cols, axis=1) + b_exp)
    s = jnp.concatenate(chunks, axis=0) if len(chunks) > 1 else chunks[0]
    y = u * s
    return (y * lax.rsqrt(jnp.mean(y * y, axis=-1, keepdims=True) + RMS_EPS) * out_g,)


def swiglu_fn(gate, up):
    return (gate * _sigmoid(gate) * up,)


SCAN_CHUNK = 32
NB = C // LANES
FWD_GROUP = 2
BWD_GROUP = 8


def to_cols(a):
    t = a.shape[0]
    return a.reshape(t // 8, LANES, HEAD).transpose(0, 2, 1)


def from_cols(a):
    return a.transpose(0, 2, 1).reshape(a.shape[0] * 8, C)


def _scan_consts():
    i = np.arange(LANES)
    spread = (i[:, None] % 2) == (i[None, :] // HEAD)
    pick = (i[:, None] // HEAD) == (i[None, :] % 2)
    j2 = (i[:, None] // HEAD) == (i[None, :] // HEAD)
    sel = (i[None, :] // 2) == (np.arange(8 * NB * HEAD)[:, None] // HEAD)
    return (jnp.asarray(spread, BF16), jnp.asarray(pick, BF16), jnp.asarray(np.concatenate([j2, j2], 0), BF16),
            jnp.asarray(sel, BF16))


def _stack(blocks):
    return jnp.concatenate(blocks, axis=0) if len(blocks) > 1 else blocks[0]


def _unstack(x, n):
    return [x[i * HEAD:(i + 1) * HEAD] for i in range(n)]


def _headsums(blocks, j2k, group):
    res = []
    for g0 in range(0, len(blocks), group):
        x = _stack(blocks[g0:g0 + group])
        hi = x.astype(BF16)
        lo = (x - hi.astype(F32)).astype(BF16)
        out = jnp.dot(jnp.concatenate([hi, lo], axis=1), j2k, preferred_element_type=F32)
        res += _unstack(out, len(blocks[g0:g0 + group]))
    return res


def _headsums_out(blocks, pick):
    return _unstack(jnp.dot(_stack(blocks).astype(BF16), pick, preferred_element_type=F32), len(blocks))


def _expand8(tile, sel, spread):
    lhs = jnp.tile(tile.astype(BF16), (8 * NB, 1)) * sel
    return _unstack(jnp.dot(lhs, spread, preferred_element_type=F32), 8 * NB)


def _collapse(tile, blocks, first, lane_pair):
    for n, blk in enumerate(blocks):
        tile = jnp.where(jnp.tile(lane_pair == first + n, (HEAD // 8, 1)), blk, tile)
    return tile


def _row(tile, j, cb):
    return jnp.broadcast_to(tile[j:j + 1, cb * LANES:(cb + 1) * LANES], (HEAD, LANES))


def _scan_step(s_ref, rows, vexp, j2k, j, hist=None):
    w_t, k_t, a_t, b_t, r_t = rows
    s = [s_ref[:, cb * LANES:(cb + 1) * LANES] for cb in range(NB)]
    sab = _headsums([s[cb] * _row(a_t, j, cb) for cb in range(NB)], j2k, FWD_GROUP if hist is None else BWD_GROUP)
    out = []
    for cb in range(NB):
        sl = slice(cb * LANES, (cb + 1) * LANES)
        s_new = s[cb] * _row(w_t, j, cb) + sab[cb] * _row(b_t, j, cb) + vexp[j * NB + cb] * _row(k_t, j, cb)
        s_ref[:, sl] = s_new
        if hist is None:
            out.append(s_new * _row(r_t, j, cb))
        else:
            s_hist, sab_hist, idx = hist
            s_hist[idx + 1, :, sl] = s_new
            sab_hist[idx, :, sl] = sab[cb]
    return out


def wkv_fwd(r, w, k, v3, a, b):
    t = r.shape[0]
    nc = t // SCAN_CHUNK
    n8 = SCAN_CHUNK // 8

    def body(r_ref, w_ref, k_ref, a_ref, b_ref, v3_ref, spread_ref, pick_ref, j2k_ref, sel_ref, y3_ref, ck_ref,
             s_ref):
        @pl.when(pl.program_id(0) == 0)
        def _():
            s_ref[...] = jnp.zeros_like(s_ref)

        ck_ref[0] = s_ref[...]
        spread, pick, j2k, sel = spread_ref[...], pick_ref[...], j2k_ref[...], sel_ref[...]
        lane_pair = lax.broadcasted_iota(jnp.int32, (8, LANES), 1) // 2

        def t8_body(t8, carry):
            row0 = pl.multiple_of(t8 * 8, 8)
            rows = [ref[pl.ds(row0, 8), :] for ref in (w_ref, k_ref, a_ref, b_ref, r_ref)]
            vexp = _expand8(v3_ref[t8], sel, spread)
            y3 = jnp.zeros((HEAD, LANES), F32)
            for j in range(8):
                y3 = _collapse(y3, _headsums_out(_scan_step(s_ref, rows, vexp, j2k, j), pick), j * NB, lane_pair)
            y3_ref[t8] = y3
            return carry

        lax.fori_loop(0, n8, t8_body, 0)

    row_spec = pl.BlockSpec((SCAN_CHUNK, C), lambda i: (i, 0))
    col_spec = pl.BlockSpec((n8, HEAD, LANES), lambda i: (i, 0, 0))
    consts = _scan_consts()
    return _pcall(
        body, name="wkv_fwd", grid=(nc,),
        in_specs=[row_spec] * 5 + [col_spec] + [pl.BlockSpec(c.shape, lambda i: (0, 0)) for c in consts],
        out_specs=[col_spec, pl.BlockSpec((1, HEAD, C), lambda i: (i, 0, 0))],
        out_shape=[jax.ShapeDtypeStruct((t // 8, HEAD, LANES), F32), jax.ShapeDtypeStruct((nc, HEAD, C), F32)],
        scratch_shapes=[pltpu.VMEM((HEAD, C), F32)],
        compiler_params=_cparams(("arbitrary",)),
    )(r, w, k, a, b, v3, *consts)


def wkv_bwd(r, w, k, v3, a, b, ckpt, dy3):
    t = r.shape[0]
    nc = t // SCAN_CHUNK
    n8 = SCAN_CHUNK // 8

    def body(r_ref, w_ref, k_ref, a_ref, b_ref, v3_ref, dy3_ref, ck_ref, spread_ref, pick_ref, j2k_ref, sel_ref,
             dr_ref, dw_ref, dk_ref, da_ref, db_ref, dv3_ref, s_ref, g_ref, s_hist, sab_hist):
        @pl.when(pl.program_id(0) == 0)
        def _():
            g_ref[...] = jnp.zeros_like(g_ref)

        spread, pick, j2k, sel = spread_ref[...], pick_ref[...], j2k_ref[...], sel_ref[...]
        lane_pair = lax.broadcasted_iota(jnp.int32, (8, LANES), 1) // 2
        sub = lax.broadcasted_iota(jnp.int32, (8, LANES), 0)
        s_ref[...] = ck_ref[0]
        s_hist[0] = ck_ref[0]

        def redo(t8, carry):
            row0 = pl.multiple_of(t8 * 8, 8)
            rows = [ref[pl.ds(row0, 8), :] for ref in (w_ref, k_ref, a_ref, b_ref, r_ref)]
            vexp = _expand8(v3_ref[t8], sel, spread)
            for j in range(8):
                _scan_step(s_ref, rows, vexp, j2k, j, hist=(s_hist, sab_hist, t8 * 8 + j))
            return carry

        lax.fori_loop(0, n8, redo, 0)

        def back(q, carry):
            t8 = n8 - 1 - q
            row0 = pl.multiple_of(t8 * 8, 8)
            w_t, k_t, a_t, b_t, r_t = [ref[pl.ds(row0, 8), :] for ref in (w_ref, k_ref, a_ref, b_ref, r_ref)]
            vexp = _expand8(v3_ref[t8], sel, spread)
            dyexp = _expand8(dy3_ref[t8], sel, spread)
            dv3 = jnp.zeros((HEAD, LANES), F32)
            tiles = [[jnp.zeros((8, LANES), F32) for _ in range(NB)] for _ in range(5)]
            for j in range(7, -1, -1):
                idx = t8 * 8 + j
                g = [g_ref[:, cb * LANES:(cb + 1) * LANES] + dyexp[j * NB + cb] * _row(r_t, j, cb)
                     for cb in range(NB)]
                dsab = _headsums([g[cb] * _row(b_t, j, cb) for cb in range(NB)], j2k, BWD_GROUP)
                dv3 = _collapse(dv3, _headsums_out([g[cb] * _row(k_t, j, cb) for cb in range(NB)], pick), j * NB,
                                lane_pair)
                for cb in range(NB):
                    sl = slice(cb * LANES, (cb + 1) * LANES)
                    s_new = s_hist[idx + 1, :, sl]
                    s_old = s_hist[idx, :, sl]
                    sab = sab_hist[idx, :, sl]
                    sums = (s_new * dyexp[j * NB + cb], g[cb] * s_old, g[cb] * vexp[j * NB + cb],
                            s_old * dsab[cb], g[cb] * sab)
                    for n, prod in enumerate(sums):
                        rowsum = jnp.broadcast_to(jnp.sum(prod, axis=0, keepdims=True), (8, LANES))
                        tiles[n][cb] = jnp.where(sub == j, rowsum, tiles[n][cb])
                    g_ref[:, sl] = g[cb] * _row(w_t, j, cb) + dsab[cb] * _row(a_t, j, cb)
            dv3_ref[t8] = dv3
            for n, ref in enumerate((dr_ref, dw_ref, dk_ref, da_ref, db_ref)):
                for cb in range(NB):
                    ref[pl.ds(row0, 8), cb * LANES:(cb + 1) * LANES] = tiles[n][cb]
            return carry

        lax.fori_loop(0, n8, back, 0)

    row_spec = pl.BlockSpec((SCAN_CHUNK, C), lambda i: (nc - 1 - i, 0))
    col_spec = pl.BlockSpec((n8, HEAD, LANES), lambda i: (nc - 1 - i, 0, 0))
    consts = _scan_consts()
    return _pcall(
        body, name="wkv_bwd", grid=(nc,),
        in_specs=[row_spec] * 5 + [col_spec, col_spec, pl.BlockSpec((1, HEAD, C), lambda i: (nc - 1 - i, 0, 0))]
        + [pl.BlockSpec(c.shape, lambda i: (0, 0)) for c in consts],
        out_specs=[row_spec] * 5 + [col_spec],
        out_shape=[jax.ShapeDtypeStruct((t, C), F32)] * 5 + [jax.ShapeDtypeStruct((t // 8, HEAD, LANES), F32)],
        scratch_shapes=[pltpu.VMEM((HEAD, C), F32), pltpu.VMEM((HEAD, C), F32),
                        pltpu.VMEM((SCAN_CHUNK + 1, HEAD, C), F32), pltpu.VMEM((SCAN_CHUNK, HEAD, C), F32)],
        compiler_params=_cparams(("arbitrary",)),
    )(r, w, k, a, b, v3, dy3, ckpt, *consts)


def _prev_rows(cur, before, first_tile):
    last = jnp.where(first_tile, 0.0, before[7:8, :])
    row = lax.broadcasted_iota(jnp.int32, cur.shape, 0)
    return jnp.where(row == 0, last, pltpu.roll(cur, 1, 0))


def shift_fwd(p, mu, col, name):
    t, width = p.shape[0], mu.shape[1]
    tm = _tile(t, 256)

    def body(p_ref, before_ref, mu_ref, o_ref):
        cur = p_ref[...]
        prev = _prev_rows(cur, before_ref[...], pl.program_id(0) == 0)
        o_ref[...] = cur + (prev - cur) * mu_ref[...]

    return _pcall(
        body, name=name, grid=(t // tm,),
        in_specs=[pl.BlockSpec((tm, width), lambda i: (i, col)),
                  pl.BlockSpec((8, width), lambda i: (jnp.maximum(i * (tm // 8) - 1, 0), col)),
                  pl.BlockSpec((1, width), lambda i: (0, 0))],
        out_specs=pl.BlockSpec((tm, width), lambda i: (i, 0)),
        out_shape=jax.ShapeDtypeStruct((t, width), F32),
        compiler_params=_cparams(("parallel",)),
    )(p, p, mu)


def shift_bwd(dps, p, mu, col, name):
    t, width = dps.shape
    tm = _tile(t, 256)
    nt = t // tm

    def body(d_ref, after_ref, p_ref, before_ref, mu_ref, dp_ref, dmu_ref):
        i = pl.program_id(0)
        d, cur, mu_v = d_ref[...], p_ref[...], mu_ref[...]
        row = lax.broadcasted_iota(jnp.int32, d.shape, 0)
        first_after = jnp.where(i == nt - 1, 0.0, after_ref[0:1, :])
        nxt = jnp.where(row == tm - 1, first_after, pltpu.roll(d, tm - 1, 0))
        dp_ref[...] = (d * (1.0 - mu_v) + nxt * mu_v).astype(BF16)
        prev = _prev_rows(cur, before_ref[...], i == 0)
        part = jnp.sum(d * (prev - cur), axis=0, keepdims=True)

        @pl.when(i == 0)
        def _():
            dmu_ref[...] = part

        @pl.when(i > 0)
        def _():
            dmu_ref[...] += part

    return _pcall(
        body, name=name, grid=(nt,),
        in_specs=[pl.BlockSpec((tm, width), lambda i: (i, 0)),
                  pl.BlockSpec((8, width), lambda i: (jnp.minimum((i + 1) * (tm // 8), t // 8 - 1), 0)),
                  pl.BlockSpec((tm, width), lambda i: (i, col)),
                  pl.BlockSpec((8, width), lambda i: (jnp.maximum(i * (tm // 8) - 1, 0), col)),
                  pl.BlockSpec((1, width), lambda i: (0, 0))],
        out_specs=[pl.BlockSpec((tm, width), lambda i: (i, 0)), pl.BlockSpec((1, width), lambda i: (0, 0))],
        out_shape=[jax.ShapeDtypeStruct((t, width), BF16), jax.ShapeDtypeStruct((1, width), F32)],
        compiler_params=_cparams(("arbitrary",)),
    )(dps, dps, p, p, mu)


def loss_head(x3, tgt, g):
    t, d = x3.shape
    tm = _tile(t, 256)

    def body(x_ref, t_ref, g_ref, loss_ref, dx_ref, dg_ref, dxb_ref):
        (y,), vjp = jax.vjp(lambda a, b: rms_fn(a, b), x_ref[...], g_ref[...])
        diff = y - t_ref[...]
        part = 0.5 * jnp.sum(jnp.mean(diff * diff, axis=-1, keepdims=True), axis=0, keepdims=True)
        dx, dg = vjp((diff * (1.0 / d),))
        dx_ref[...] = dx
        dxb_ref[...] = dx.astype(BF16)
        part = jnp.broadcast_to(part, (1, LANES))
        first = pl.program_id(0) == 0

        @pl.when(first)
        def _():
            loss_ref[...] = part
            dg_ref[...] = dg

        @pl.when(jnp.logical_not(first))
        def _():
            loss_ref[...] += part
            dg_ref[...] += dg

    row = pl.BlockSpec((tm, d), lambda i: (i, 0))
    vec = pl.BlockSpec((1, d), lambda i: (0, 0))
    return _pcall(
        body, name="loss_head", grid=(t // tm,), in_specs=[row, row, vec],
        out_specs=[pl.BlockSpec((1, LANES), lambda i: (0, 0)), row, vec, row],
        out_shape=[jax.ShapeDtypeStruct((1, LANES), F32), jax.ShapeDtypeStruct((t, d), F32),
                   jax.ShapeDtypeStruct((1, d), F32), jax.ShapeDtypeStruct((t, d), BF16)],
        compiler_params=_cparams(("arbitrary",)),
    )(x3, tgt, g)


ADAM_LR, ADAM_B1, ADAM_B2, ADAM_EPS, ADAM_WD, ADAM_STEP = 0.001, 0.9, 0.999, 1e-08, 0.01, 10


def adamw(w, g, m, v, name):
    rows, width = w.shape
    tr = _div_tile(rows, 256, 8)

    def body(w_ref, g_ref, m_ref, v_ref, d_ref, nm_ref, nv_ref):
        gv = g_ref[...]
        m_new = ADAM_B1 * m_ref[...] + (1.0 - ADAM_B1) * gv
        v_new = ADAM_B2 * v_ref[...] + (1.0 - ADAM_B2) * (gv * gv)
        m_hat = m_new / (1.0 - ADAM_B1 ** ADAM_STEP)
        v_hat = v_new / (1.0 - ADAM_B2 ** ADAM_STEP)
        d_ref[...] = -ADAM_LR * (m_hat / (jnp.sqrt(v_hat) + ADAM_EPS) + ADAM_WD * w_ref[...])
        nm_ref[...] = m_new
        nv_ref[...] = v_new

    spec = pl.BlockSpec((tr, width), lambda i: (i, 0))
    return _pcall(
        body, name=name, grid=(rows // tr,), in_specs=[spec] * 4, out_specs=[spec] * 3,
        out_shape=[jax.ShapeDtypeStruct((rows, width), F32)] * 3,
        compiler_params=_cparams(("parallel",)),
    )(w, g, m, v)


def sum_slots(q, name, for_swap=False):
    n, rows, width = q.shape
    tr = _div_tile(rows, 512, 32 // q.dtype.itemsize)

    def body(*refs):
        refs = refs[for_swap:]
        acc = refs[0][...].astype(F32)
        for r in refs[1:n]:
            acc = acc + r[...].astype(F32)
        refs[n][...] = acc

    if not for_swap:
        specs = [pl.BlockSpec((None, tr, width), functools.partial(lambda s, i: (s, i, 0), s)) for s in range(n)]
        return _pcall(body, name=name, grid=(rows // tr,), in_specs=specs,
                      out_specs=pl.BlockSpec((tr, width), lambda i: (i, 0)),
                      out_shape=jax.ShapeDtypeStruct((rows, width), F32),
                      compiler_params=_cparams(("parallel",)))(*([q] * n))
    specs = [pl.BlockSpec((None, tr, width), functools.partial(lambda s, i, c_ref: (s, i, 0), s)) for s in range(n)]
    grid_spec = pltpu.PrefetchScalarGridSpec(
        num_scalar_prefetch=1, grid=(rows // tr,), in_specs=specs,
        out_specs=pl.BlockSpec((None, tr, width), lambda i, c_ref: (c_ref[0], i, 0)))
    return _pcall(body, name=name, grid_spec=grid_spec, out_shape=jax.ShapeDtypeStruct((2, rows, width), F32),
                  compiler_params=_cparams(("parallel",)))(
        lax.axis_index("c").astype(jnp.int32).reshape(1), *([q] * n))


MESH = pl.DeviceIdType.MESH
ANY_SPEC = pl.BlockSpec(memory_space=pl.ANY)


def _coords():
    return lax.axis_index("x"), lax.axis_index("y"), lax.axis_index("c")


def _other_chips(x, y):
    return [(1 - x, y), (x, 1 - y), (1 - x, 1 - y)]


def push(name, srcs, n_dst, plan_fn):
    n_arr = len(srcs)
    n_send = len(plan_fn(0, 0, 0))

    def body(*refs):
        src_refs, dst_refs = refs[:n_arr], refs[n_arr:2 * n_arr]
        send_sems, recv_sems = refs[2 * n_arr:]
        sends = plan_fn(*_coords())
        out = []
        for q, (src_ref, dst_ref) in enumerate(zip(src_refs, dst_refs)):
            out += [pltpu.make_async_remote_copy(src_ref.at[si], dst_ref.at[di], send_sems.at[q * n_send + k],
                                                 recv_sems.at[q * n_send + k], device_id=peer, device_id_type=MESH)
                    for k, (si, peer, di, _) in enumerate(sends)]
        for cp in out:
            cp.start()
        for q, (src_ref, dst_ref) in enumerate(zip(src_refs, dst_refs)):
            for k, (si, peer, _, ri) in enumerate(sends):
                pltpu.make_async_remote_copy(src_ref.at[si], dst_ref.at[ri], send_sems.at[q * n_send + k],
                                             recv_sems.at[q * n_send + k], device_id=peer,
                                             device_id_type=MESH).wait_recv()
        for cp in out:
            cp.wait_send()

    return _pcall(
        body, name=name, in_specs=[ANY_SPEC] * n_arr, out_specs=[ANY_SPEC] * n_arr,
        out_shape=[jax.ShapeDtypeStruct((n_dst,) + s.shape[1:], s.dtype) for s in srcs],
        scratch_shapes=[pltpu.SemaphoreType.DMA((n_arr * n_send,)), pltpu.SemaphoreType.DMA((n_arr * n_send,))],
    )(*srcs)


def plan_sibling_quarters(x, y, c):
    return [(2 * s + (1 - c), (x, y, 1 - c), s, s) for s in range(4)]


def plan_chips_by_shard(x, y, c):
    me = 2 * x + y
    return [(2 * px + py, (px, py, c), me, 2 * px + py) for px, py in _other_chips(x, y)]


def plan_chips_half(x, y, c):
    me = 2 * x + y
    return [(c, (px, py, c), me, 2 * px + py) for px, py in _other_chips(x, y)]


def swap_halves(name, bufs):
    n_arr = len(bufs)

    def body(*refs):
        buf_refs = refs[n_arr:2 * n_arr]
        send_sems, recv_sems = refs[2 * n_arr:]
        x, y, c = _coords()
        out = [pltpu.make_async_remote_copy(b.at[c], b.at[c], send_sems.at[q], recv_sems.at[q],
                                            device_id=(x, y, 1 - c), device_id_type=MESH)
               for q, b in enumerate(buf_refs)]
        for cp in out:
            cp.start()
        for q, b in enumerate(buf_refs):
            pltpu.make_async_remote_copy(b.at[1 - c], b.at[1 - c], send_sems.at[q], recv_sems.at[q],
                                         device_id=(x, y, 1 - c), device_id_type=MESH).wait_recv()
        for cp in out:
            cp.wait_send()

    return _pcall(
        body, name=name, in_specs=[ANY_SPEC] * n_arr, out_specs=[ANY_SPEC] * n_arr,
        out_shape=[jax.ShapeDtypeStruct(b.shape, b.dtype) for b in bufs],
        input_output_aliases={q: q for q in range(n_arr)},
        scratch_shapes=[pltpu.SemaphoreType.DMA((n_arr,)), pltpu.SemaphoreType.DMA((n_arr,))],
    )(*bufs)


HBM_SPEC = pl.BlockSpec(memory_space=pltpu.HBM)
SEM_SPEC = pl.BlockSpec(memory_space=pltpu.SEMAPHORE)
DATAFLOW = pltpu.SideEffectType.DATAFLOW_SIDE_EFFECTING


def _in_hbm(a):
    return pltpu.with_memory_space_constraint(a, pltpu.HBM)


def push_start(name, srcs, lands, plan_fn, after):
    n_arr = len(srcs)
    n_send = len(plan_fn(0, 0, 0))
    arrays = list(srcs) + ([] if lands is None else list(lands))
    if after is None:
        after = jnp.zeros((8, LANES), F32)
    n_all = len(arrays)

    def body(*refs):
        src_refs, land_refs = refs[:n_arr], refs[n_all - n_arr:n_all]
        send_sems, recv_sems = refs[n_all + 1:n_all + 3]
        token = refs[-1]
        sends = plan_fn(*_coords())
        for q, (src_ref, land_ref) in enumerate(zip(src_refs, land_refs)):
            for k, (si, peer, di, _) in enumerate(sends):
                pltpu.make_async_remote_copy(src_ref.at[si], land_ref.at[di], send_sems.at[q * n_send + k],
                                             recv_sems.at[q * n_send + k], device_id=peer,
                                             device_id_type=MESH).start()
        token[...] = jnp.zeros_like(token)

    sems = pltpu.SemaphoreType.DMA((n_arr * n_send,))
    out = _pcall(
        body, name=name,
        in_specs=[HBM_SPEC] * n_all + [ANY_SPEC],
        out_specs=[SEM_SPEC, SEM_SPEC] + [HBM_SPEC] * n_all + [pl.BlockSpec(memory_space=pltpu.VMEM)],
        out_shape=[sems, sems] + [pltpu.HBM(a.shape, a.dtype) for a in arrays]
        + [jax.ShapeDtypeStruct((8, LANES), F32)],
        input_output_aliases={q: 2 + q for q in range(n_all)},
        compiler_params=pltpu.CompilerParams(has_side_effects=DATAFLOW),
    )(*[_in_hbm(a) for a in arrays], after)
    return out[:-1], out[-1]


def push_wait(name, state, plan_fn, after, in_place=False, both=False):
    arrays = list(state[2:])
    n_all = len(arrays)
    n_arr = n_all if in_place else n_all // 2
    n_send = len(plan_fn(0, 0, 0))
    send_sems, recv_sems = state[:2]

    def body(*refs):
        src_refs, land_refs = refs[:n_arr], refs[n_all - n_arr:n_all]
        send_ref, recv_ref = refs[n_all:n_all + 2]
        sends = plan_fn(*_coords())
        for q, (src_ref, land_ref) in enumerate(zip(src_refs, land_refs)):
            for k, (si, peer, _, ri) in enumerate(sends):
                cp = pltpu.make_async_remote_copy(src_ref.at[si], land_ref.at[ri], send_ref.at[q * n_send + k],
                                                  recv_ref.at[q * n_send + k], device_id=peer, device_id_type=MESH)
                cp.wait_send()
                cp.wait_recv()

    out = _pcall(
        body, name=name,
        in_specs=[HBM_SPEC] * n_all + [SEM_SPEC, SEM_SPEC, ANY_SPEC],
        out_specs=[HBM_SPEC] * n_all,
        out_shape=[pltpu.HBM(a.shape, a.dtype) for a in arrays],
        input_output_aliases={q: q for q in range(n_all)},
        compiler_params=pltpu.CompilerParams(has_side_effects=DATAFLOW),
    )(*arrays, send_sems, recv_sems, after)
    return list(out) if both else list(out[n_all - n_arr:])


def plan_chips_gather(x, y, c):
    me = 2 * x + y
    return [(c, (px, py, c), 2 * me + c, 2 * (2 * px + py) + c) for px, py in _other_chips(x, y)]


def plan_forward(x, y, c):
    return [(2 * (2 * px + py) + c, (x, y, 1 - c), 2 * (2 * px + py) + c, 2 * (2 * px + py) + (1 - c))
            for px, py in _other_chips(x, y)]


def add_kept(g8, other, name):
    _, rows, cols = other.shape
    tr = _div_tile(rows, 512, 16)

    def body(c_ref, a_ref, b_ref, o_ref):
        o_ref[...] = (a_ref[...].astype(F32) + b_ref[...].astype(F32)).astype(BF16)

    spec = pl.BlockSpec((None, tr, cols), lambda s, i, c_ref: (s, i, 0))
    grid_spec = pltpu.PrefetchScalarGridSpec(
        num_scalar_prefetch=1, grid=(4, rows // tr),
        in_specs=[pl.BlockSpec((None, tr, cols), lambda s, i, c_ref: (2 * s + c_ref[0], i, 0)), spec],
        out_specs=spec)
    return _pcall(body, name=name, grid_spec=grid_spec, out_shape=jax.ShapeDtypeStruct((4, rows, cols), BF16),
                  compiler_params=_cparams(("parallel", "parallel")))(
        lax.axis_index("c").astype(jnp.int32).reshape(1), g8, other)


def reduce_scatter_begin(grads, names, tag):
    g8 = [g.reshape(8, g.shape[1] // 2, g.shape[2]) for g in grads]
    return _chip_sums_start(g8, push(tag + "_sibling", g8, 4, plan_sibling_quarters), names, tag)


def _chip_sums_start(g8, from_sibling, names, tag):
    chip_sum = [add_kept(a, b, tag + "_add_" + n) for a, b, n in zip(g8, from_sibling, names)]
    return push_start(tag + "_start", chip_sum, chip_sum, plan_chips_by_shard, None)


def sibling_exchange_begin(grads, tag):
    g8 = [g.reshape(8, g.shape[1] // 2, g.shape[2]) for g in grads]
    lands = [lax.empty((4,) + g.shape[1:], g.dtype) for g in g8]
    return push_start(tag + "_sib_start", g8, lands, plan_sibling_quarters, None)


def reduce_scatter_continue(state, names, tag, after):
    n = len(names)
    arrays = push_wait(tag + "_sib_wait", state, plan_sibling_quarters, after, both=True)
    return _chip_sums_start(arrays[:n], arrays[n:], names, tag)


def reduce_scatter_end(state, names, tag, after):
    by_chip = push_wait(tag + "_wait", state, plan_chips_by_shard, after)
    both = swap_halves(tag + "_halves", [sum_slots(q, tag + "_sum_" + n, for_swap=True)
                                         for q, n in zip(by_chip, names)])
    return [b.reshape(2 * b.shape[1], b.shape[2]) for b in both]


def _landing(mine):
    return [jnp.broadcast_to(m[None], (4,) + m.shape).reshape((8,) + m.shape[1:]) for m in mine]


def gather_begin(tag, mine, after, lands=None):
    return push_start(tag + "_start", mine, _landing(mine) if lands is None else lands, plan_chips_gather, after)


def gather_forward(tag, state, after):
    lands = push_wait(tag + "_wait", state, plan_chips_gather, after)
    return push_start(tag + "_fwd_start", lands, None, plan_forward, None)


def gather_end(tag, state, after):
    bufs = push_wait(tag + "_fwd_wait", state, plan_forward, after, in_place=True)
    return [g.reshape(4, 2 * g.shape[1], g.shape[2]) for g in bufs]


PACK_QUANTUM = 8 * C


def _pack(parts, dtype):
    pieces = []
    for p in parts:
        flat = p.astype(dtype).reshape(-1)
        pieces.append(jnp.pad(flat, (0, -flat.shape[0] % PACK_QUANTUM)).reshape(-1, C))
    rows = sum(p.shape[0] for p in pieces)
    if rows % 16:
        pieces.append(jnp.zeros((8, C), dtype))
    return jnp.concatenate(pieces, axis=0)


def _unpack(packed, shapes):
    out, at = [], 0
    for s in shapes:
        n = int(np.prod(s))
        rows = -(-n // PACK_QUANTUM) * 8
        out.append(packed[at:at + rows].reshape(-1)[:n].reshape(s))
        at += rows
    return out


def all_reduce_begin(vec):
    rows, width = vec.shape
    (pair,) = swap_halves("ar_sibling", [jnp.broadcast_to(vec[None], (2, rows, width))])
    chip_sum = sum_slots(pair, "ar_sum_sibling").reshape(2, rows // 2, width)
    mine = lax.dynamic_index_in_dim(chip_sum, lax.axis_index("c"), axis=0, keepdims=True)
    return push_start("ar_start", [chip_sum], [jnp.broadcast_to(mine, (4, rows // 2, width))], plan_chips_half,
                      None)


def all_reduce_end(state, after):
    (by_chip,) = push_wait("ar_wait", state, plan_chips_half, after)
    (both,) = swap_halves("ar_halves", [sum_slots(by_chip, "ar_sum_chips", for_swap=True)])
    return both.reshape(2 * both.shape[1], both.shape[2])


def kernel(x, norm1_g, w_in, shift_mu, w0, w2, a0, a2, g2, k_k, k_a, r_k, gn_w, gn_b, sgu_ln_g, sgu_ln_b, sgu_w, sgu_b, sgu_out_g, w_out, norm2_g, w_gate, w_up, w_down, final_g, loss_target, m_norm1_g, m_w_in, m_shift_mu, m_w0, m_w2, m_a0, m_a2, m_g2, m_k_k, m_k_a, m_r_k, m_gn_w, m_gn_b, m_sgu_ln_g, m_sgu_ln_b, m_sgu_w, m_sgu_b, m_sgu_out_g, m_w_out, m_norm2_g, m_w_gate, m_w_up, m_w_down, m_final_g, v_norm1_g, v_w_in, v_shift_mu, v_w0, v_w2, v_a0, v_a2, v_g2, v_k_k, v_k_a, v_r_k, v_gn_w, v_gn_b, v_sgu_ln_g, v_sgu_ln_b, v_sgu_w, v_sgu_b, v_sgu_out_g, v_w_out, v_norm2_g, v_w_gate, v_w_up, v_w_down, v_final_g):
    args = dict(locals())
    names = ["norm1_g", "w_in", "shift_mu", "w0", "w2", "a0", "a2", "g2", "k_k", "k_a", "r_k", "gn_w", "gn_b",
             "sgu_ln_g", "sgu_ln_b", "sgu_w", "sgu_b", "sgu_out_g", "w_out", "norm2_g", "w_gate", "w_up", "w_down",
             "final_g"]
    big = ["w_in", "w_out", "w_gate", "w_up", "w_down"]
    xi, yi = lax.axis_index("x"), lax.axis_index("y")
    chip = 2 * xi + yi
    x2d, tgt = x[0], loss_target[0]
    t = x2d.shape[0]

    lora_names = ["w2", "a2", "g2"]
    first, later = ["w_in"] + lora_names, ["w_out", "w_gate", "w_up", "w_down"]

    def halves(n, token=None):
        m = args[n][0] if token is None else args[n][0] + token[0, 0]
        m = m.astype(BF16)
        return m.reshape(2, m.shape[0] // 2, m.shape[1])

    first_state, first_token = gather_begin("g0", [halves(n) for n in first], norm1_g)
    later_mine = [halves(n, first_token) for n in later]
    later_lands = _landing(later_mine)
    (h1,) = row_fwd("norm1", rms_fn, [Tiled(x2d)], [norm1_g + first_token[0, 0]], [], [(D, BF16)], _tile(t, 256))
    prepared = functools.reduce(lambda p, q: p + q, [a[0, 0, :1].astype(F32) for a in later_lands])
    first_state, first_token = gather_forward("g0", first_state, prepared + h1[0, :1].astype(F32))
    got = gather_end("g0", first_state, first_token)
    later_state, later_token = gather_begin("g1", later_mine, got[0], later_lands)
    full = {n: blk.transpose(1, 0, 2).reshape(blk.shape[1], 4 * blk.shape[2]) for n, blk in zip(first[1:], got[1:])}
    s0, s1, s2, s3 = got[0]
    w_in_p = jnp.concatenate([s0, s1, s2[:, :CUT_A], s2[:, CUT_B:], s3, s2[:, CUT_A:CUT_B],
                              jnp.zeros((D, P_W - D_IN), BF16)], axis=1)
    mu_rkv = shift_mu[:, :3 * C] + later_token[0, 0]
    mu_lora = jnp.pad(shift_mu[:, 3 * C:], ((0, 0), (0, LORA_W - 288)))
    w2p = jnp.pad(full["w2"], ((0, 64), (0, 0)))
    a2p = jnp.pad(full["a2"], ((64, 0), (0, 0)))
    g2p = jnp.pad(full["g2"], ((0, 96), (0, 0)))
    ii = np.arange(C)
    jm = _scan_consts()[2]
    e_np = (np.arange(LANES)[:, None] == (ii[None, :] // HEAD))
    e_mat, e_t = jnp.asarray(e_np, BF16), jnp.asarray(e_np.T, BF16)
    b_t = jnp.pad(sgu_b[0].T, ((0, 0), (0, LANES - HEADS)))
    r_k_flat = r_k.reshape(1, C)
    tm_row = _tile(t, 256)
    tm_sgu_b = _tile(t, 128)

    p = mm(h1, w_in_p, name="proj_in")
    ps_rkv = shift_fwd(p, mu_rkv, 0, "shift_rkv")
    ps_lora = shift_fwd(p, mu_lora, P_LORA // LORA_W, "shift_lora")
    pre_par = [w0, w2p, a0, a2p, g2p, k_k, k_a]
    r_, w_, k2, v_, an, b_, g_ = row_fwd("pre", pre_fn, [Tiled(ps_rkv), Tiled(ps_lora)], pre_par, [jm],
                                         [(C, F32)] * 7, tm_row)
    v3 = to_cols(v_)
    y3, ckpt = wkv_fwd(r_, w_, k2, v3, an, b_)
    later_state, later_token = gather_forward("g1", later_state, y3)
    y_ = from_cols(y3 + later_token[0, 0])
    post_in = [Tiled(z) for z in (y_, r_, k2, v_, g_)]
    post_par = [gn_w, gn_b, r_k_flat]
    (y_rwkv,) = row_fwd("post", post_fn, post_in, post_par, [jm], [(C, BF16)], tm_row)
    sgu_in = [Tiled(p, C, 3), Tiled(p, C, 4)]
    sgu_par = [sgu_ln_g, sgu_ln_b, sgu_w[0], b_t, sgu_out_g]
    (y_sgu,) = row_fwd("sgu", sgu_fn, sgu_in, sgu_par, [e_mat, e_t], [(C, BF16)], tm_row)
    y_cat = jnp.concatenate([y_rwkv, y_sgu], axis=1)
    for n, blk in zip(later, gather_end("g1", later_state, y_cat)):
        full[n] = blk if n in ("w_gate", "w_up") else blk.reshape(4 * blk.shape[1], blk.shape[2])
    x_mid = mm(y_cat, full["w_out"], res=x2d, name="proj_out")
    (h2,) = row_fwd("norm2", rms_fn, [Tiled(x_mid)], [norm2_g], [], [(D, BF16)], tm_row)
    gate = mm(h2, full["w_gate"], shards="b", name="ffn_gate")
    up, act = mm(h2, full["w_up"], shards="b", name="ffn_up",
                 post=(lambda up_tile, gate_tile: (up_tile, swiglu_fn(gate_tile, up_tile)[0]), [gate], [F32, BF16]))
    x_out = mm(act, full["w_down"], res=x_mid, name="ffn_down")
    loss_part, dx_out, d_final_g, dx_out_b = loss_head(x_out, tgt, final_g.reshape(1, D))

    grads = {"final_g": d_final_g.reshape(D)}
    d_gate, d_up = mm(dx_out_b, full["w_down"], tb=True, name="d_act",
                      post=(lambda d_act, g_tile, u_tile: jax.vjp(swiglu_fn, g_tile, u_tile)[1]((d_act,)), [gate, up],
                            [BF16, BF16]))
    grads["w_down"] = mm(act, dx_out_b, ta=True, out_dtype=BF16, name="d_w_down")
    grads["w_gate"] = mm(h2, d_gate, shards="out", out_dtype=BF16, name="d_w_gate")
    grads["w_up"] = mm(h2, d_up, shards="out", out_dtype=BF16, name="d_w_up")
    d_h2 = mm(d_gate, full["w_gate"], shards="bt", name="d_h2_gate")
    d_h2 = mm(d_up, full["w_up"], shards="bt", res=d_h2, name="d_h2_up")
    dx_mid, grads["norm2_g"], dx_mid_b = row_bwd("norm2_b", rms_res_fn, [Tiled(x_mid)], [norm2_g], [],
                                                 [Tiled(d_h2), Tiled(dx_out)], tm_row, bf16_copy_of=(0,))
    d_ycat = mm(dx_mid_b, full["w_out"], tb=True, name="d_ycat")
    grads["w_out"] = mm(y_cat, dx_mid_b, ta=True, out_dtype=BF16, name="d_w_out")
    rs_state, rs_token = sibling_exchange_begin([grads[n].reshape((4,) + args[n].shape[1:]) for n in later], "rs2")
    d_pu, d_pv, grads["sgu_ln_g"], grads["sgu_ln_b"], d_sgu_w, d_b_t, grads["sgu_out_g"] = row_bwd(
        "sgu_b", sgu_fn, sgu_in, sgu_par[:-1] + [sgu_out_g + rs_token[0, 0]], [e_mat, e_t], [Tiled(d_ycat, C, 1)],
        tm_sgu_b, tiled_out_dtypes=[BF16, BF16])
    grads["sgu_w"] = d_sgu_w[None]
    grads["sgu_b"] = d_b_t[:, :HEADS].T[None]
    post_par_b = [gn_w + rs_token[0, 0]] + post_par[1:]
    d_y, d_r1, d_k1, d_v1, d_g, grads["gn_w"], grads["gn_b"], d_r_k = row_bwd(
        "post_b", post_fn, post_in, post_par_b, [jm], [Tiled(d_ycat, C, 0)], tm_row)
    grads["r_k"] = d_r_k.reshape(r_k.shape)
    rs_state, rs_token = reduce_scatter_continue(rs_state, later, "rs2", d_y[0, :1] + d_pu[0, :1].astype(F32))
    d_r2, d_w, d_k2, d_a, d_b, d_v3 = wkv_bwd(r_, w_, k2, v3, an, b_, ckpt, to_cols(d_y + rs_token[0, 0]))
    d_v2 = from_cols(d_v3)
    pre_cts = [(Tiled(d_r1), Tiled(d_r2)), Tiled(d_w), (Tiled(d_k1), Tiled(d_k2)), (Tiled(d_v1), Tiled(d_v2)),
               Tiled(d_a), Tiled(d_b), Tiled(d_g)]
    d_ps_rkv, d_ps_lora, grads["w0"], d_w2p, grads["a0"], d_a2p, d_g2p, grads["k_k"], grads["k_a"] = row_bwd(
        "pre_b", pre_fn, [Tiled(ps_rkv), Tiled(ps_lora)], pre_par, [jm], pre_cts, tm_row)
    grads["w2"], grads["a2"], grads["g2"] = d_w2p[None, :64], d_a2p[None, 64:], d_g2p[None, :160]
    d_p_rkv, d_mu_rkv = shift_bwd(d_ps_rkv, p, mu_rkv, 0, "shift_rkv_b")
    d_p_lora, d_mu_lora = shift_bwd(d_ps_lora, p, mu_lora, P_LORA // LORA_W, "shift_lora_b")
    grads["shift_mu"] = jnp.concatenate([d_mu_rkv, d_mu_lora[:, :288]], axis=1)
    d_p = jnp.concatenate([d_p_rkv, d_pu, d_pv, d_p_lora], axis=1)
    d_h1 = mm(d_p, w_in_p, tb=True, name="d_h1")
    d_w_in_p = mm(h1, d_p, ta=True, out_dtype=BF16, name="d_w_in")
    ws = D_IN // 4
    shard2 = jnp.concatenate([d_w_in_p[:, 2 * ws:3 * C], d_w_in_p[:, P_LORA:P_LORA + 288],
                              d_w_in_p[:, 3 * C:3 * C + ws - CUT_B]], axis=1)
    grads["w_in"] = jnp.stack([d_w_in_p[:, :ws], d_w_in_p[:, ws:2 * ws], shard2,
                               d_w_in_p[:, 3 * C + ws - CUT_B:P_LORA]])
    dx, grads["norm1_g"] = row_bwd("norm1_b", rms_res_fn, [Tiled(x2d)], [norm1_g], [],
                                   [Tiled(d_h1), Tiled(dx_mid)], tm_row)

    small = [n for n in names if n not in big]
    small_shapes = [(1, 64, C) if n in ("w2", "a2") else (1, 160, C) if n == "g2" else args[n].shape for n in small]
    vec = _pack([grads[n] for n in small] + [loss_part[:, :1]], F32)
    w_in_state, w_in_token = reduce_scatter_begin([grads["w_in"]], ["w_in"], "rs1")
    small_state, small_token = all_reduce_begin(vec)
    total = {n: g[None] for n, g in zip(later, reduce_scatter_end(rs_state, later, "rs2", w_in_token + small_token))}
    delta, new_m, new_v = {}, {}, {}

    def adam_big(n):
        s = args[n].shape
        view = (lambda a: a[0].T) if n == "w_in" else (lambda a: a[0])
        back = (lambda a: a.T.reshape(s)) if n == "w_in" else (lambda a: a.reshape(s))
        d_, m_, v__ = adamw(view(args[n]), view(total[n]), view(args["m_" + n]), view(args["v_" + n]), "adamw_" + n)
        delta[n], new_m[n], new_v[n] = back(d_), back(m_), back(v__)

    for n in later:
        adam_big(n)
    total["w_in"] = reduce_scatter_end(w_in_state, ["w_in"], "rs1", new_v["w_down"])[0][None]
    *small_sums, loss = _unpack(all_reduce_end(small_state, new_v["w_down"]), small_shapes + [()])
    for n, g_sum in zip(small, small_sums):
        total[n] = lax.dynamic_slice_in_dim(g_sum, chip * 256, 256, axis=2) if n in lora_names else g_sum
    adam_big("w_in")
    flat = [_pack([src[n] for n in small], F32)
            for src in (args, total, {n: args["m_" + n] for n in small}, {n: args["v_" + n] for n in small})]
    outs = adamw(*flat, "adamw_small")
    for res, o in zip((delta, new_m, new_v), outs):
        res.update(zip(small, _unpack(o, [args[n].shape for n in small])))
    return (loss, dx[None], *[total[n] for n in names], *[delta[n] for n in names],
            *[new_m[n] for n in names], *[new_v[n] for n in names])
```

```python
import functools

import numpy as np
import jax
import jax.numpy as jnp
from jax import lax
from jax.experimental import pallas as pl
from jax.experimental.pallas import tpu as pltpu

F32 = jnp.float32
BF16 = jnp.bfloat16

D = 2048
C = 1024
HEADS = 16
HEAD = 64
LANES = 128
P_W = 5632
P_LORA = 5120
LORA_W = 512
D_IN = 5408
CUT_A = 3 * C - 2 * (D_IN // 4)
CUT_B = CUT_A + 288
CHUNK = 128
RMS_EPS = 1e-6
LN_EPS = 1e-5
GN_EPS = 64e-5
L2_EPS = 1e-12
VMEM_LIMIT = 56 * 1024 * 1024


def _pcall(body, **kw):
    return pl.pallas_call(body, **kw)


def _cparams(sem):
    return pltpu.CompilerParams(dimension_semantics=sem, vmem_limit_bytes=VMEM_LIMIT)


def _tile(n, most):
    t = most
    while t > 8 and n % t:
        t //= 2
    assert n % t == 0, (n, most)
    return t


MM_TILE = (2048, 512, 2048)
MM_VMEM = 44 * 1024 * 1024


def _div_tile(n, most, quantum=LANES):
    for t in range(min(n, most) // quantum * quantum, 0, -quantum):
        if n % t == 0:
            return t
    raise ValueError((n, most, quantum))


def _mm_tiles(m, n, k, a_bytes, b_bytes, r_bytes, o_bytes):
    tm, tn, tk = _div_tile(m, MM_TILE[0]), _div_tile(n, MM_TILE[1]), _div_tile(k, MM_TILE[2])

    def need(tm, tk):
        return 2 * (tm * tk * a_bytes + tk * tn * b_bytes + tm * tn * (r_bytes + o_bytes)) + tm * tn * 4

    while need(tm, tk) > MM_VMEM:
        if tk >= tm and tk > 512:
            tk = _div_tile(k, tk - LANES)
        else:
            tm = _div_tile(m, tm - LANES)
    return tm, tn, tk


def mm(a, b, *, ta=False, tb=False, res=None, out_dtype=F32, name, shards=None, post=None):
    if shards is not None:
        return _mm_shards(a, b, res, out_dtype, name, shards, post)
    m, k = (a.shape[1], a.shape[0]) if ta else a.shape
    n = b.shape[0] if tb else b.shape[1]
    assert (b.shape[1] if tb else b.shape[0]) == k
    if post is None:
        tm, tn, tk = _mm_tiles(m, n, k, a.dtype.itemsize, b.dtype.itemsize,
                               0 if res is None else res.dtype.itemsize, jnp.dtype(out_dtype).itemsize)
    else:
        tm, tn, tk = _div_tile(m, 1024), _div_tile(n, MM_TILE[1]), _div_tile(k, MM_TILE[2])
    nk = k // tk
    dims = (((0 if ta else 1,), (1 if tb else 0,)), ((), ()))
    a_spec = pl.BlockSpec((tk, tm), lambda i, j, l: (l, i)) if ta else pl.BlockSpec((tm, tk), lambda i, j, l: (i, l))
    b_spec = pl.BlockSpec((tn, tk), lambda i, j, l: (j, l)) if tb else pl.BlockSpec((tk, tn), lambda i, j, l: (l, j))
    o_spec = pl.BlockSpec((tm, tn), lambda i, j, l: (i, j))
    return _mm_call(a, b, res, dims, (m // tm, n // tn, nk), a_spec, b_spec, o_spec, o_spec, (tm, tn),
                    jax.ShapeDtypeStruct((m, n), out_dtype), name, post)


def _mm_shards(a, b, res, out_dtype, name, shards, post=None):
    assert post is None or shards == "b"
    if shards == "b":
        (m, k), ns = a.shape, b.shape[2]
        tm, tk = _div_tile(m, 1024 if post is None else 512), _div_tile(k, 2048)
        grid, dims, acc = (m // tm, 4, k // tk), (((1,), (0,)), ((), ())), (tm, ns)
        a_spec = pl.BlockSpec((tm, tk), lambda i, s, l: (i, l))
        b_spec = pl.BlockSpec((None, tk, ns), lambda i, s, l: (s, l, 0))
        o_spec = pl.BlockSpec((tm, ns), lambda i, s, l: (i, s))
        out = jax.ShapeDtypeStruct((m, 4 * ns), out_dtype)
    elif shards == "bt":
        m, (_, n, ns) = a.shape[0], b.shape
        tm, tn = _div_tile(m, 2048), _div_tile(n, 512)
        grid, dims, acc = (m // tm, n // tn, 4), (((1,), (1,)), ((), ())), (tm, tn)
        a_spec = pl.BlockSpec((tm, ns), lambda i, j, s: (i, s))
        b_spec = pl.BlockSpec((None, tn, ns), lambda i, j, s: (s, j, 0))
        o_spec = pl.BlockSpec((tm, tn), lambda i, j, s: (i, j))
        out = jax.ShapeDtypeStruct((m, n), out_dtype)
    else:
        (t, m), ns = a.shape, b.shape[1] // 4
        tm, tk = _div_tile(m, 1024), _div_tile(t, 2048)
        grid, dims, acc = (m // tm, 4, t // tk), (((0,), (0,)), ((), ())), (tm, ns)
        a_spec = pl.BlockSpec((tk, tm), lambda i, s, l: (l, i))
        b_spec = pl.BlockSpec((tk, ns), lambda i, s, l: (l, s))
        o_spec = pl.BlockSpec((None, tm, ns), lambda i, s, l: (s, i, 0))
        out = jax.ShapeDtypeStruct((4, m, ns), out_dtype)
    return _mm_call(a, b, res, dims, grid, a_spec, b_spec, o_spec, o_spec, acc, out, name, post)


def _mm_call(a, b, res, dims, grid, a_spec, b_spec, r_spec, o_spec, acc_shape, out, name, post=None):
    nk = grid[2]
    post_fn, post_in, post_dtypes = (None, [], [out.dtype]) if post is None else post
    n_extra = (res is not None) + len(post_in)

    def body(*refs):
        a_ref, b_ref = refs[:2]
        r_ref = None if res is None else refs[2]
        p_refs = refs[2 + (res is not None):2 + n_extra]
        o_refs = refs[2 + n_extra:2 + n_extra + len(post_dtypes)]

        def finish(acc):
            if r_ref is not None:
                acc = acc + r_ref[...].astype(F32)
            outs = (acc,) if post_fn is None else post_fn(acc, *[p[...] for p in p_refs])
            for o_ref, o in zip(o_refs, outs):
                o_ref[...] = o.astype(o_ref.dtype)

        prod = lax.dot_general(a_ref[...].astype(BF16), b_ref[...].astype(BF16), dims, preferred_element_type=F32)
        if nk == 1:
            finish(prod)
            return
        acc_ref = refs[-1]
        kk = pl.program_id(2)

        @pl.when(kk == 0)
        def _():
            acc_ref[...] = prod

        @pl.when(kk > 0)
        def _():
            acc_ref[...] += prod

        @pl.when(kk == nk - 1)
        def _():
            finish(acc_ref[...])

    in_specs = [a_spec, b_spec] + ([r_spec] if res is not None else []) + [o_spec] * len(post_in)
    args = (a, b) + ((res,) if res is not None else ()) + tuple(post_in)
    outs = _pcall(
        body, name=name, grid=grid, in_specs=in_specs, out_specs=[o_spec] * len(post_dtypes),
        out_shape=[jax.ShapeDtypeStruct(out.shape, dt) for dt in post_dtypes],
        scratch_shapes=[] if nk == 1 else [pltpu.VMEM(acc_shape, F32)],
        compiler_params=_cparams(("parallel", "parallel", "arbitrary")),
    )(*args)
    return outs[0] if post is None else outs


class Tiled:
    def __init__(self, arr, width=None, col=0):
        self.arr, self.width, self.col = arr, (arr.shape[1] if width is None else width), col

    def spec(self, tm):
        col = self.col
        return pl.BlockSpec((tm, self.width), lambda i: (i, col))


def _full_spec(p):
    nd = p.ndim
    return pl.BlockSpec(p.shape, lambda i: (0,) * nd)


def row_fwd(name, fn, tiled, params, consts, outs, tm):
    t = tiled[0].arr.shape[0]
    n_in = len(tiled) + len(params) + len(consts)

    def body(*refs):
        res = fn(*[r[...] for r in refs[:n_in]])
        for o_ref, r in zip(refs[n_in:], res):
            o_ref[...] = r.astype(o_ref.dtype)

    return _pcall(
        body, name=name, grid=(t // tm,),
        in_specs=[x.spec(tm) for x in tiled] + [_full_spec(p) for p in params + consts],
        out_specs=[pl.BlockSpec((tm, w), lambda i: (i, 0)) for w, _ in outs],
        out_shape=[jax.ShapeDtypeStruct((t, w), dt) for w, dt in outs],
        compiler_params=_cparams(("parallel",)),
    )(*[x.arr for x in tiled], *params, *consts)


def row_bwd(name, fn, tiled, params, consts, cts, tm, tiled_out_dtypes=None, bf16_copy_of=()):
    t = tiled[0].arr.shape[0]
    nt, npar, ncon = len(tiled), len(params), len(consts)
    cts = [c if isinstance(c, tuple) else (c,) for c in cts]
    flat_cts = [x for c in cts for x in c]
    tiled_out_dtypes = tiled_out_dtypes or [F32] * nt
    copies = [tiled[j] for j in bf16_copy_of]

    def body(*refs):
        n_in = nt + npar + ncon
        ins = [r[...].astype(F32) for r in refs[:nt + npar]]
        con = [r[...] for r in refs[nt + npar:n_in]]
        ct_refs = list(refs[n_in:n_in + len(flat_cts)])
        o = refs[n_in + len(flat_cts):]
        ct = []
        for c in cts:
            parts = [ct_refs.pop(0)[...].astype(F32) for _ in c]
            ct.append(functools.reduce(lambda p, q: p + q, parts))
        _, vjp = jax.vjp(lambda *a: fn(*a, *con), *ins)
        g = vjp(tuple(ct))
        for j in range(nt):
            o[j][...] = g[j].astype(o[j].dtype)
        for n, j in enumerate(bf16_copy_of):
            o[nt + npar + n][...] = g[j].astype(BF16)
        first = pl.program_id(0) == 0

        @pl.when(first)
        def _():
            for j in range(npar):
                o[nt + j][...] = g[nt + j]

        @pl.when(jnp.logical_not(first))
        def _():
            for j in range(npar):
                o[nt + j][...] += g[nt + j]

    return _pcall(
        body, name=name, grid=(t // tm,),
        in_specs=[x.spec(tm) for x in tiled] + [_full_spec(p) for p in params + consts]
        + [x.spec(tm) for x in flat_cts],
        out_specs=[pl.BlockSpec((tm, x.width), lambda i: (i, 0)) for x in tiled] + [_full_spec(p) for p in params]
        + [pl.BlockSpec((tm, x.width), lambda i: (i, 0)) for x in copies],
        out_shape=[jax.ShapeDtypeStruct((t, x.width), dt) for x, dt in zip(tiled, tiled_out_dtypes)]
        + [jax.ShapeDtypeStruct(p.shape, F32) for p in params]
        + [jax.ShapeDtypeStruct((t, x.width), BF16) for x in copies],
        compiler_params=_cparams(("arbitrary",)),
    )(*[x.arr for x in tiled], *params, *consts, *[x.arr for x in flat_cts])


def _split_dot(x, w):
    hi = x.astype(BF16)
    lo = (x - hi.astype(F32)).astype(BF16)
    return jnp.dot(hi, w, preferred_element_type=F32) + jnp.dot(lo, w, preferred_element_type=F32)


def _headsum_blocks(x, j2k):
    out = []
    for c0 in range(0, x.shape[1], LANES):
        blk = x[:, c0:c0 + LANES]
        hi = blk.astype(BF16)
        lo = (blk - hi.astype(F32)).astype(BF16)
        out.append(jnp.dot(jnp.concatenate([hi, lo], axis=1), j2k, preferred_element_type=F32))
    return jnp.concatenate(out, axis=1)


@jax.custom_vjp
def headsum(x, j2k):
    return _headsum_blocks(x, j2k)


def _headsum_fwd(x, j2k):
    return _headsum_blocks(x, j2k), j2k


def _headsum_bwd(j2k, ct):
    return _headsum_blocks(ct, j2k), jnp.zeros_like(j2k)


headsum.defvjp(_headsum_fwd, _headsum_bwd)


def _bdot(x, w):
    return jnp.dot(x.astype(BF16), w.astype(BF16), preferred_element_type=F32)


def _sigmoid(x):
    return jax.nn.sigmoid(x)


def _softplus(x):
    return jnp.maximum(x, 0.0) + jnp.log(1.0 + jnp.exp(-jnp.abs(x)))


def rms_fn(x, g):
    return (x * lax.rsqrt(jnp.mean(x * x, axis=-1, keepdims=True) + RMS_EPS) * g,)


def rms_res_fn(x, g):
    return rms_fn(x, g)[0], x


def pre_fn(rkv, lora, w0, w2p, a0, a2p, g2p, k_k, k_a, jm):
    r, k, v = rkv[:, :C], rkv[:, C:2 * C], rkv[:, 2 * C:]
    zwa, zg = lora[:, :LANES], lora[:, LANES:LANES + 256]
    w_log = -_softplus(-(w0 + _bdot(jnp.tanh(zwa), w2p))) - 0.5
    decay = jnp.exp(-jnp.exp(w_log))
    a = _sigmoid(a0 + _bdot(zwa, a2p))
    g = _bdot(_sigmoid(zg), g2p)
    kk = k * k_k
    kk = kk / jnp.maximum(jnp.sqrt(headsum(kk * kk, jm)), L2_EPS)
    k2 = k * (1.0 + (a - 1.0) * k_a)
    return r, decay, k2, v, -kk, kk * a, g


def post_fn(y, r, k2, v, g, gn_w, gn_b, r_k, jm):
    mu = headsum(y, jm) * (1.0 / HEAD)
    yc = y - mu
    var = headsum(yc * yc, jm) * (1.0 / HEAD)
    yn = yc * lax.rsqrt(var + GN_EPS) * gn_w + gn_b
    bonus = headsum(r * k2 * r_k, jm) * v
    return ((yn + bonus) * g,)


def _gelu(x):
    return 0.5 * x * (1.0 + lax.erf(x * np.float32(1.0 / np.sqrt(2.0))))


@jax.custom_vjp
def expand_groups(b_t, e, e_t):
    return _split_dot(b_t, e)


def _expand_groups_fwd(b_t, e, e_t):
    return _split_dot(b_t, e), (e, e_t)


def _expand_groups_bwd(res, ct):
    e, e_t = res
    return _split_dot(ct, e_t), jnp.zeros_like(e), jnp.zeros_like(e_t)


expand_groups.defvjp(_expand_groups_fwd, _expand_groups_bwd)


def sgu_fn(pu, pv, ln_g, ln_b, w_s, b_t, out_g, e, e_t):
    rows = pu.shape[0]
    b_exp = expand_groups(b_t, e, e_t)
    u, v = _gelu(pu), _gelu(pv)
    mu = jnp.mean(v, axis=-1, keepdims=True)
    vc = v - mu
    var = jnp.mean(vc * vc, axis=-1, keepdims=True)
    v = vc * lax.rsqrt(var + LN_EPS) * ln_g + ln_b
    tri = lax.broadcasted_iota(jnp.int32, (CHUNK, CHUNK), 0) >= lax.broadcasted_iota(jnp.int32, (CHUNK, CHUNK), 1)
    left = lax.broadcasted_iota(jnp.int32, (CHUNK, LANES), 1) < HEAD
    chunks = []
    for c0 in range(0, rows, CHUNK):
        cols = []
        for gp in range(HEADS // 2):
            vp = v[c0:c0 + CHUNK, gp * LANES:(gp + 1) * LANES]
            wa = jnp.where(tri, w_s[2 * gp], 0.0)
            wb = jnp.where(tri, w_s[2 * gp + 1], 0.0)
            cols.append(jnp.where(left, _bdot(wa, vp), _bdot(wb, vp)))
        chunks.append(jnp.concatenate(cols, axis=1) + b_exp)
    s = jnp.concatenate(chunks, axis=0) if len(chunks) > 1 else chunks[0]
    y = u * s
    return (y * lax.rsqrt(jnp.mean(y * y, axis=-1, keepdims=True) + RMS_EPS) * out_g,)


def swiglu_fn(gate, up):
    return (gate * _sigmoid(gate) * up,)


SCAN_CHUNK = 32
NB = C // LANES
FWD_GROUP = 2
BWD_GROUP = 8


def to_cols(a):
    t = a.shape[0]
    return a.reshape(t // 8, LANES, HEAD).transpose(0, 2, 1)


def from_cols(a):
    return a.transpose(0, 2, 1).reshape(a.shape[0] * 8, C)


def _scan_consts():
    i = np.arange(LANES)
    spread = (i[:, None] % 2) == (i[None, :] // HEAD)
    pick = (i[:, None] // HEAD) == (i[None, :] % 2)
    j2 = (i[:, None] // HEAD) == (i[None, :] // HEAD)
    sel = (i[None, :] // 2) == (np.arange(8 * NB * HEAD)[:, None] // HEAD)
    return (jnp.asarray(spread, BF16), jnp.asarray(pick, BF16), jnp.asarray(np.concatenate([j2, j2], 0), BF16),
            jnp.asarray(sel, BF16))


def _stack(blocks):
    return jnp.concatenate(blocks, axis=0) if len(blocks) > 1 else blocks[0]


def _unstack(x, n):
    return [x[i * HEAD:(i + 1) * HEAD] for i in range(n)]


def _headsums(blocks, j2k, group):
    res = []
    for g0 in range(0, len(blocks), group):
        x = _stack(blocks[g0:g0 + group])
        hi = x.astype(BF16)
        lo = (x - hi.astype(F32)).astype(BF16)
        out = jnp.dot(jnp.concatenate([hi, lo], axis=1), j2k, preferred_element_type=F32)
        res += _unstack(out, len(blocks[g0:g0 + group]))
    return res


def _headsums_out(blocks, pick):
    return _unstack(jnp.dot(_stack(blocks).astype(BF16), pick, preferred_element_type=F32), len(blocks))


def _expand8(tile, sel, spread):
    lhs = jnp.tile(tile.astype(BF16), (8 * NB, 1)) * sel
    return _unstack(jnp.dot(lhs, spread, preferred_element_type=F32), 8 * NB)


def _collapse(tile, blocks, first, lane_pair):
    for n, blk in enumerate(blocks):
        tile = jnp.where(jnp.tile(lane_pair == first + n, (HEAD // 8, 1)), blk, tile)
    return tile


def _row(tile, j, cb):
    return jnp.broadcast_to(tile[j:j + 1, cb * LANES:(cb + 1) * LANES], (HEAD, LANES))


def _scan_step(s_ref, rows, vexp, j2k, j, hist=None):
    w_t, k_t, a_t, b_t, r_t = rows
    s = [s_ref[:, cb * LANES:(cb + 1) * LANES] for cb in range(NB)]
    sab = _headsums([s[cb] * _row(a_t, j, cb) for cb in range(NB)], j2k, FWD_GROUP if hist is None else BWD_GROUP)
    out = []
    for cb in range(NB):
        sl = slice(cb * LANES, (cb + 1) * LANES)
        s_new = s[cb] * _row(w_t, j, cb) + sab[cb] * _row(b_t, j, cb) + vexp[j * NB + cb] * _row(k_t, j, cb)
        s_ref[:, sl] = s_new
        if hist is None:
            out.append(s_new * _row(r_t, j, cb))
        else:
            s_hist, sab_hist, idx = hist
            s_hist[idx + 1, :, sl] = s_new
            sab_hist[idx, :, sl] = sab[cb]
    return out


def wkv_fwd(r, w, k, v3, a, b):
    t = r.shape[0]
    nc = t // SCAN_CHUNK
    n8 = SCAN_CHUNK // 8

    def body(r_ref, w_ref, k_ref, a_ref, b_ref, v3_ref, spread_ref, pick_ref, j2k_ref, sel_ref, y3_ref, ck_ref,
             s_ref):
        @pl.when(pl.program_id(0) == 0)
        def _():
            s_ref[...] = jnp.zeros_like(s_ref)

        ck_ref[0] = s_ref[...]
        spread, pick, j2k, sel = spread_ref[...], pick_ref[...], j2k_ref[...], sel_ref[...]
        lane_pair = lax.broadcasted_iota(jnp.int32, (8, LANES), 1) // 2

        def t8_body(t8, carry):
            row0 = pl.multiple_of(t8 * 8, 8)
            rows = [ref[pl.ds(row0, 8), :] for ref in (w_ref, k_ref, a_ref, b_ref, r_ref)]
            vexp = _expand8(v3_ref[t8], sel, spread)
            y3 = jnp.zeros((HEAD, LANES), F32)
            for j in range(8):
                y3 = _collapse(y3, _headsums_out(_scan_step(s_ref, rows, vexp, j2k, j), pick), j * NB, lane_pair)
            y3_ref[t8] = y3
            return carry

        lax.fori_loop(0, n8, t8_body, 0)

    row_spec = pl.BlockSpec((SCAN_CHUNK, C), lambda i: (i, 0))
    col_spec = pl.BlockSpec((n8, HEAD, LANES), lambda i: (i, 0, 0))
    consts = _scan_consts()
    return _pcall(
        body, name="wkv_fwd", grid=(nc,),
        in_specs=[row_spec] * 5 + [col_spec] + [pl.BlockSpec(c.shape, lambda i: (0, 0)) for c in consts],
        out_specs=[col_spec, pl.BlockSpec((1, HEAD, C), lambda i: (i, 0, 0))],
        out_shape=[jax.ShapeDtypeStruct((t // 8, HEAD, LANES), F32), jax.ShapeDtypeStruct((nc, HEAD, C), F32)],
        scratch_shapes=[pltpu.VMEM((HEAD, C), F32)],
        compiler_params=_cparams(("arbitrary",)),
    )(r, w, k, a, b, v3, *consts)


def wkv_bwd(r, w, k, v3, a, b, ckpt, dy3, token):
    t = r.shape[0]
    nc = t // SCAN_CHUNK
    n8 = SCAN_CHUNK // 8

    def body(r_ref, w_ref, k_ref, a_ref, b_ref, v3_ref, dy3_ref, ck_ref, spread_ref, pick_ref, j2k_ref, sel_ref,
             dr_ref, dw_ref, dk_ref, da_ref, db_ref, dv3_ref, s_ref, g_ref, s_hist, sab_hist):
        @pl.when(pl.program_id(0) == 0)
        def _():
            g_ref[...] = jnp.zeros_like(g_ref)

        spread, pick, j2k, sel = spread_ref[...], pick_ref[...], j2k_ref[...], sel_ref[...]
        lane_pair = lax.broadcasted_iota(jnp.int32, (8, LANES), 1) // 2
        sub = lax.broadcasted_iota(jnp.int32, (8, LANES), 0)
        s_ref[...] = ck_ref[0]
        s_hist[0] = ck_ref[0]

        def redo(t8, carry):
            row0 = pl.multiple_of(t8 * 8, 8)
            rows = [ref[pl.ds(row0, 8), :] for ref in (w_ref, k_ref, a_ref, b_ref, r_ref)]
            vexp = _expand8(v3_ref[t8], sel, spread)
            for j in range(8):
                _scan_step(s_ref, rows, vexp, j2k, j, hist=(s_hist, sab_hist, t8 * 8 + j))
            return carry

        lax.fori_loop(0, n8, redo, 0)

        def back(q, carry):
            t8 = n8 - 1 - q
            row0 = pl.multiple_of(t8 * 8, 8)
            w_t, k_t, a_t, b_t, r_t = [ref[pl.ds(row0, 8), :] for ref in (w_ref, k_ref, a_ref, b_ref, r_ref)]
            vexp = _expand8(v3_ref[t8], sel, spread)
            dyexp = _expand8(dy3_ref[t8], sel, spread)
            dv3 = jnp.zeros((HEAD, LANES), F32)
            tiles = [[jnp.zeros((8, LANES), F32) for _ in range(NB)] for _ in range(5)]
            for j in range(7, -1, -1):
                idx = t8 * 8 + j
                g = [g_ref[:, cb * LANES:(cb + 1) * LANES] + dyexp[j * NB + cb] * _row(r_t, j, cb)
                     for cb in range(NB)]
                dsab = _headsums([g[cb] * _row(b_t, j, cb) for cb in range(NB)], j2k, BWD_GROUP)
                dv3 = _collapse(dv3, _headsums_out([g[cb] * _row(k_t, j, cb) for cb in range(NB)], pick), j * NB,
                                lane_pair)
                for cb in range(NB):
                    sl = slice(cb * LANES, (cb + 1) * LANES)
                    s_new = s_hist[idx + 1, :, sl]
                    s_old = s_hist[idx, :, sl]
                    sab = sab_hist[idx, :, sl]
                    sums = (s_new * dyexp[j * NB + cb], g[cb] * s_old, g[cb] * vexp[j * NB + cb],
                            s_old * dsab[cb], g[cb] * sab)
                    for n, prod in enumerate(sums):
                        rowsum = jnp.broadcast_to(jnp.sum(prod, axis=0, keepdims=True), (8, LANES))
                        tiles[n][cb] = jnp.where(sub == j, rowsum, tiles[n][cb])
                    g_ref[:, sl] = g[cb] * _row(w_t, j, cb) + dsab[cb] * _row(a_t, j, cb)
            dv3_ref[t8] = dv3
            for n, ref in enumerate((dr_ref, dw_ref, dk_ref, da_ref, db_ref)):
                for cb in range(NB):
                    ref[pl.ds(row0, 8), cb * LANES:(cb + 1) * LANES] = tiles[n][cb]
            return carry

        lax.fori_loop(0, n8, back, 0)

    row_spec = pl.BlockSpec((SCAN_CHUNK, C), lambda i: (nc - 1 - i, 0))
    col_spec = pl.BlockSpec((n8, HEAD, LANES), lambda i: (nc - 1 - i, 0, 0))
    consts = list(_scan_consts())
    consts[0] = consts[0] + token[0, 0].astype(BF16)
    return _pcall(
        body, name="wkv_bwd", grid=(nc,),
        in_specs=[row_spec] * 5 + [col_spec, col_spec, pl.BlockSpec((1, HEAD, C), lambda i: (nc - 1 - i, 0, 0))]
        + [pl.BlockSpec(c.shape, lambda i: (0, 0)) for c in consts],
        out_specs=[row_spec] * 5 + [col_spec],
        out_shape=[jax.ShapeDtypeStruct((t, C), F32)] * 5 + [jax.ShapeDtypeStruct((t // 8, HEAD, LANES), F32)],
        scratch_shapes=[pltpu.VMEM((HEAD, C), F32), pltpu.VMEM((HEAD, C), F32),
                        pltpu.VMEM((SCAN_CHUNK + 1, HEAD, C), F32), pltpu.VMEM((SCAN_CHUNK, HEAD, C), F32)],
        compiler_params=_cparams(("arbitrary",)),
    )(r, w, k, a, b, v3, dy3, ckpt, *consts)


def _prev_rows(cur, before, first_tile):
    last = jnp.where(first_tile, 0.0, before[7:8, :])
    row = lax.broadcasted_iota(jnp.int32, cur.shape, 0)
    return jnp.where(row == 0, last, pltpu.roll(cur, 1, 0))


def shift_fwd(p, mu, col, name):
    t, width = p.shape[0], mu.shape[1]
    tm = _tile(t, 256)

    def body(p_ref, before_ref, mu_ref, o_ref):
        cur = p_ref[...]
        prev = _prev_rows(cur, before_ref[...], pl.program_id(0) == 0)
        o_ref[...] = cur + (prev - cur) * mu_ref[...]

    return _pcall(
        body, name=name, grid=(t // tm,),
        in_specs=[pl.BlockSpec((tm, width), lambda i: (i, col)),
                  pl.BlockSpec((8, width), lambda i: (jnp.maximum(i * (tm // 8) - 1, 0), col)),
                  pl.BlockSpec((1, width), lambda i: (0, 0))],
        out_specs=pl.BlockSpec((tm, width), lambda i: (i, 0)),
        out_shape=jax.ShapeDtypeStruct((t, width), F32),
        compiler_params=_cparams(("parallel",)),
    )(p, p, mu)


def shift_bwd(dps, p, mu, col, name):
    t, width = dps.shape
    tm = _tile(t, 256)
    nt = t // tm

    def body(d_ref, after_ref, p_ref, before_ref, mu_ref, dp_ref, dmu_ref):
        i = pl.program_id(0)
        d, cur, mu_v = d_ref[...], p_ref[...], mu_ref[...]
        row = lax.broadcasted_iota(jnp.int32, d.shape, 0)
        first_after = jnp.where(i == nt - 1, 0.0, after_ref[0:1, :])
        nxt = jnp.where(row == tm - 1, first_after, pltpu.roll(d, tm - 1, 0))
        dp_ref[...] = (d * (1.0 - mu_v) + nxt * mu_v).astype(BF16)
        prev = _prev_rows(cur, before_ref[...], i == 0)
        part = jnp.sum(d * (prev - cur), axis=0, keepdims=True)

        @pl.when(i == 0)
        def _():
            dmu_ref[...] = part

        @pl.when(i > 0)
        def _():
            dmu_ref[...] += part

    return _pcall(
        body, name=name, grid=(nt,),
        in_specs=[pl.BlockSpec((tm, width), lambda i: (i, 0)),
                  pl.BlockSpec((8, width), lambda i: (jnp.minimum((i + 1) * (tm // 8), t // 8 - 1), 0)),
                  pl.BlockSpec((tm, width), lambda i: (i, col)),
                  pl.BlockSpec((8, width), lambda i: (jnp.maximum(i * (tm // 8) - 1, 0), col)),
                  pl.BlockSpec((1, width), lambda i: (0, 0))],
        out_specs=[pl.BlockSpec((tm, width), lambda i: (i, 0)), pl.BlockSpec((1, width), lambda i: (0, 0))],
        out_shape=[jax.ShapeDtypeStruct((t, width), BF16), jax.ShapeDtypeStruct((1, width), F32)],
        compiler_params=_cparams(("arbitrary",)),
    )(dps, dps, p, p, mu)


def loss_head(x3, tgt, g):
    t, d = x3.shape
    tm = _tile(t, 256)

    def body(x_ref, t_ref, g_ref, loss_ref, dx_ref, dg_ref, dxb_ref):
        (y,), vjp = jax.vjp(lambda a, b: rms_fn(a, b), x_ref[...], g_ref[...])
        diff = y - t_ref[...]
        part = 0.5 * jnp.sum(jnp.mean(diff * diff, axis=-1, keepdims=True), axis=0, keepdims=True)
        dx, dg = vjp((diff * (1.0 / d),))
        dx_ref[...] = dx
        dxb_ref[...] = dx.astype(BF16)
        part = jnp.broadcast_to(part, (1, LANES))
        first = pl.program_id(0) == 0

        @pl.when(first)
        def _():
            loss_ref[...] = part
            dg_ref[...] = dg

        @pl.when(jnp.logical_not(first))
        def _():
            loss_ref[...] += part
            dg_ref[...] += dg

    row = pl.BlockSpec((tm, d), lambda i: (i, 0))
    vec = pl.BlockSpec((1, d), lambda i: (0, 0))
    return _pcall(
        body, name="loss_head", grid=(t // tm,), in_specs=[row, row, vec],
        out_specs=[pl.BlockSpec((1, LANES), lambda i: (0, 0)), row, vec, row],
        out_shape=[jax.ShapeDtypeStruct((1, LANES), F32), jax.ShapeDtypeStruct((t, d), F32),
                   jax.ShapeDtypeStruct((1, d), F32), jax.ShapeDtypeStruct((t, d), BF16)],
        compiler_params=_cparams(("arbitrary",)),
    )(x3, tgt, g)


ADAM_LR, ADAM_B1, ADAM_B2, ADAM_EPS, ADAM_WD, ADAM_STEP = 0.001, 0.9, 0.999, 1e-08, 0.01, 10


def adamw(w, g, m, v, name):
    rows, width = w.shape
    tr = _div_tile(rows, 256, 8)

    def body(w_ref, g_ref, m_ref, v_ref, d_ref, nm_ref, nv_ref):
        gv = g_ref[...]
        m_new = ADAM_B1 * m_ref[...] + (1.0 - ADAM_B1) * gv
        v_new = ADAM_B2 * v_ref[...] + (1.0 - ADAM_B2) * (gv * gv)
        m_hat = m_new / (1.0 - ADAM_B1 ** ADAM_STEP)
        v_hat = v_new / (1.0 - ADAM_B2 ** ADAM_STEP)
        d_ref[...] = -ADAM_LR * (m_hat / (jnp.sqrt(v_hat) + ADAM_EPS) + ADAM_WD * w_ref[...])
        nm_ref[...] = m_new
        nv_ref[...] = v_new

    spec = pl.BlockSpec((tr, width), lambda i: (i, 0))
    return _pcall(
        body, name=name, grid=(rows // tr,), in_specs=[spec] * 4, out_specs=[spec] * 3,
        out_shape=[jax.ShapeDtypeStruct((rows, width), F32)] * 3,
        compiler_params=_cparams(("parallel",)),
    )(w, g, m, v)


def sum_slots(q, name, for_swap=False):
    n, rows, width = q.shape
    tr = _div_tile(rows, 512, 32 // q.dtype.itemsize)

    def body(*refs):
        refs = refs[for_swap:]
        acc = refs[0][...].astype(F32)
        for r in refs[1:n]:
            acc = acc + r[...].astype(F32)
        refs[n][...] = acc

    if not for_swap:
        specs = [pl.BlockSpec((None, tr, width), functools.partial(lambda s, i: (s, i, 0), s)) for s in range(n)]
        return _pcall(body, name=name, grid=(rows // tr,), in_specs=specs,
                      out_specs=pl.BlockSpec((tr, width), lambda i: (i, 0)),
                      out_shape=jax.ShapeDtypeStruct((rows, width), F32),
                      compiler_params=_cparams(("parallel",)))(*([q] * n))
    specs = [pl.BlockSpec((None, tr, width), functools.partial(lambda s, i, c_ref: (s, i, 0), s)) for s in range(n)]
    grid_spec = pltpu.PrefetchScalarGridSpec(
        num_scalar_prefetch=1, grid=(rows // tr,), in_specs=specs,
        out_specs=pl.BlockSpec((None, tr, width), lambda i, c_ref: (c_ref[0], i, 0)))
    return _pcall(body, name=name, grid_spec=grid_spec, out_shape=jax.ShapeDtypeStruct((2, rows, width), F32),
                  compiler_params=_cparams(("parallel",)))(
        lax.axis_index("c").astype(jnp.int32).reshape(1), *([q] * n))


MESH = pl.DeviceIdType.MESH
ANY_SPEC = pl.BlockSpec(memory_space=pl.ANY)


def _coords():
    return lax.axis_index("x"), lax.axis_index("y"), lax.axis_index("c")


def _other_chips(x, y):
    return [(1 - x, y), (x, 1 - y), (1 - x, 1 - y)]


def push(name, srcs, n_dst, plan_fn):
    n_arr = len(srcs)
    n_send = len(plan_fn(0, 0, 0))

    def body(*refs):
        src_refs, dst_refs = refs[:n_arr], refs[n_arr:2 * n_arr]
        send_sems, recv_sems = refs[2 * n_arr:]
        sends = plan_fn(*_coords())
        out = []
        for q, (src_ref, dst_ref) in enumerate(zip(src_refs, dst_refs)):
            out += [pltpu.make_async_remote_copy(src_ref.at[si], dst_ref.at[di], send_sems.at[q * n_send + k],
                                                 recv_sems.at[q * n_send + k], device_id=peer, device_id_type=MESH)
                    for k, (si, peer, di, _) in enumerate(sends)]
        for cp in out:
            cp.start()
        for q, (src_ref, dst_ref) in enumerate(zip(src_refs, dst_refs)):
            for k, (si, peer, _, ri) in enumerate(sends):
                pltpu.make_async_remote_copy(src_ref.at[si], dst_ref.at[ri], send_sems.at[q * n_send + k],
                                             recv_sems.at[q * n_send + k], device_id=peer,
                                             device_id_type=MESH).wait_recv()
        for cp in out:
            cp.wait_send()

    return _pcall(
        body, name=name, in_specs=[ANY_SPEC] * n_arr, out_specs=[ANY_SPEC] * n_arr,
        out_shape=[jax.ShapeDtypeStruct((n_dst,) + s.shape[1:], s.dtype) for s in srcs],
        scratch_shapes=[pltpu.SemaphoreType.DMA((n_arr * n_send,)), pltpu.SemaphoreType.DMA((n_arr * n_send,))],
    )(*srcs)


def plan_sibling_quarters(x, y, c):
    return [(2 * s + (1 - c), (x, y, 1 - c), s, s) for s in range(4)]


def plan_chips_by_shard(x, y, c):
    me = 2 * x + y
    return [(2 * px + py, (px, py, c), me, 2 * px + py) for px, py in _other_chips(x, y)]


def plan_chips_half(x, y, c):
    me = 2 * x + y
    return [(c, (px, py, c), me, 2 * px + py) for px, py in _other_chips(x, y)]


def swap_halves(name, bufs):
    n_arr = len(bufs)

    def body(*refs):
        buf_refs = refs[n_arr:2 * n_arr]
        send_sems, recv_sems = refs[2 * n_arr:]
        x, y, c = _coords()
        out = [pltpu.make_async_remote_copy(b.at[c], b.at[c], send_sems.at[q], recv_sems.at[q],
                                            device_id=(x, y, 1 - c), device_id_type=MESH)
               for q, b in enumerate(buf_refs)]
        for cp in out:
            cp.start()
        for q, b in enumerate(buf_refs):
            pltpu.make_async_remote_copy(b.at[1 - c], b.at[1 - c], send_sems.at[q], recv_sems.at[q],
                                         device_id=(x, y, 1 - c), device_id_type=MESH).wait_recv()
        for cp in out:
            cp.wait_send()

    return _pcall(
        body, name=name, in_specs=[ANY_SPEC] * n_arr, out_specs=[ANY_SPEC] * n_arr,
        out_shape=[jax.ShapeDtypeStruct(b.shape, b.dtype) for b in bufs],
        input_output_aliases={q: q for q in range(n_arr)},
        scratch_shapes=[pltpu.SemaphoreType.DMA((n_arr,)), pltpu.SemaphoreType.DMA((n_arr,))],
    )(*bufs)


HBM_SPEC = pl.BlockSpec(memory_space=pltpu.HBM)
SEM_SPEC = pl.BlockSpec(memory_space=pltpu.SEMAPHORE)
DATAFLOW = pltpu.SideEffectType.DATAFLOW_SIDE_EFFECTING


def _in_hbm(a):
    return pltpu.with_memory_space_constraint(a, pltpu.HBM)


def push_start(name, srcs, lands, plan_fn, after):
    n_arr = len(srcs)
    n_send = len(plan_fn(0, 0, 0))
    arrays = list(srcs) + ([] if lands is None else list(lands))
    if after is None:
        after = jnp.zeros((8, LANES), F32)
    n_all = len(arrays)

    def body(*refs):
        src_refs, land_refs = refs[:n_arr], refs[n_all - n_arr:n_all]
        send_sems, recv_sems = refs[n_all + 1:n_all + 3]
        token = refs[-1]
        sends = plan_fn(*_coords())
        for q, (src_ref, land_ref) in enumerate(zip(src_refs, land_refs)):
            for k, (si, peer, di, _) in enumerate(sends):
                pltpu.make_async_remote_copy(src_ref.at[si], land_ref.at[di], send_sems.at[q * n_send + k],
                                             recv_sems.at[q * n_send + k], device_id=peer,
                                             device_id_type=MESH).start()
        token[...] = jnp.zeros_like(token)

    sems = pltpu.SemaphoreType.DMA((n_arr * n_send,))
    out = _pcall(
        body, name=name,
        in_specs=[HBM_SPEC] * n_all + [ANY_SPEC],
        out_specs=[SEM_SPEC, SEM_SPEC] + [HBM_SPEC] * n_all + [pl.BlockSpec(memory_space=pltpu.VMEM)],
        out_shape=[sems, sems] + [pltpu.HBM(a.shape, a.dtype) for a in arrays]
        + [jax.ShapeDtypeStruct((8, LANES), F32)],
        input_output_aliases={q: 2 + q for q in range(n_all)},
        compiler_params=pltpu.CompilerParams(has_side_effects=DATAFLOW),
    )(*[_in_hbm(a) for a in arrays], after)
    return out[:-1], out[-1]


def push_wait(name, state, plan_fn, after, in_place=False, both=False):
    arrays = list(state[2:])
    n_all = len(arrays)
    n_arr = n_all if in_place else n_all // 2
    n_send = len(plan_fn(0, 0, 0))
    send_sems, recv_sems = state[:2]

    def body(*refs):
        src_refs, land_refs = refs[:n_arr], refs[n_all - n_arr:n_all]
        send_ref, recv_ref = refs[n_all:n_all + 2]
        sends = plan_fn(*_coords())
        for q, (src_ref, land_ref) in enumerate(zip(src_refs, land_refs)):
            for k, (si, peer, _, ri) in enumerate(sends):
                cp = pltpu.make_async_remote_copy(src_ref.at[si], land_ref.at[ri], send_ref.at[q * n_send + k],
                                                  recv_ref.at[q * n_send + k], device_id=peer, device_id_type=MESH)
                cp.wait_send()
                cp.wait_recv()

    out = _pcall(
        body, name=name,
        in_specs=[HBM_SPEC] * n_all + [SEM_SPEC, SEM_SPEC, ANY_SPEC],
        out_specs=[HBM_SPEC] * n_all,
        out_shape=[pltpu.HBM(a.shape, a.dtype) for a in arrays],
        input_output_aliases={q: q for q in range(n_all)},
        compiler_params=pltpu.CompilerParams(has_side_effects=DATAFLOW),
    )(*arrays, send_sems, recv_sems, after)
    return list(out) if both else list(out[n_all - n_arr:])


def plan_chips_gather(x, y, c):
    me = 2 * x + y
    return [(c, (px, py, c), 2 * me + c, 2 * (2 * px + py) + c) for px, py in _other_chips(x, y)]


def plan_forward(x, y, c):
    return [(2 * (2 * px + py) + c, (x, y, 1 - c), 2 * (2 * px + py) + c, 2 * (2 * px + py) + (1 - c))
            for px, py in _other_chips(x, y)]


def add_kept(g8, other, name):
    _, rows, cols = other.shape
    tr = _div_tile(rows, 512, 16)

    def body(c_ref, a_ref, b_ref, o_ref):
        o_ref[...] = (a_ref[...].astype(F32) + b_ref[...].astype(F32)).astype(BF16)

    spec = pl.BlockSpec((None, tr, cols), lambda s, i, c_ref: (s, i, 0))
    grid_spec = pltpu.PrefetchScalarGridSpec(
        num_scalar_prefetch=1, grid=(4, rows // tr),
        in_specs=[pl.BlockSpec((None, tr, cols), lambda s, i, c_ref: (2 * s + c_ref[0], i, 0)), spec],
        out_specs=spec)
    return _pcall(body, name=name, grid_spec=grid_spec, out_shape=jax.ShapeDtypeStruct((4, rows, cols), BF16),
                  compiler_params=_cparams(("parallel", "parallel")))(
        lax.axis_index("c").astype(jnp.int32).reshape(1), g8, other)


def reduce_scatter_begin(grads, names, tag):
    g8 = [g.reshape(8, g.shape[1] // 2, g.shape[2]) for g in grads]
    return _chip_sums_start(g8, push(tag + "_sibling", g8, 4, plan_sibling_quarters), names, tag)


def _chip_sums_start(g8, from_sibling, names, tag):
    chip_sum = [add_kept(a, b, tag + "_add_" + n) for a, b, n in zip(g8, from_sibling, names)]
    return push_start(tag + "_start", chip_sum, chip_sum, plan_chips_by_shard, None)


def sibling_exchange_begin(grads, tag):
    g8 = [g.reshape(8, g.shape[1] // 2, g.shape[2]) for g in grads]
    lands = [lax.empty((4,) + g.shape[1:], g.dtype) for g in g8]
    return push_start(tag + "_sib_start", g8, lands, plan_sibling_quarters, None)


def reduce_scatter_continue(state, names, tag, after):
    n = len(names)
    arrays = push_wait(tag + "_sib_wait", state, plan_sibling_quarters, after, both=True)
    return _chip_sums_start(arrays[:n], arrays[n:], names, tag)


def reduce_scatter_end(state, names, tag, after):
    by_chip = push_wait(tag + "_wait", state, plan_chips_by_shard, after)
    both = swap_halves(tag + "_halves", [sum_slots(q, tag + "_sum_" + n, for_swap=True)
                                         for q, n in zip(by_chip, names)])
    return [b.reshape(2 * b.shape[1], b.shape[2]) for b in both]


def _landing(mine):
    return [jnp.broadcast_to(m[None], (4,) + m.shape).reshape((8,) + m.shape[1:]) for m in mine]


def gather_begin(tag, mine, after, lands=None):
    return push_start(tag + "_start", mine, _landing(mine) if lands is None else lands, plan_chips_gather, after)


def gather_forward(tag, state, after):
    lands = push_wait(tag + "_wait", state, plan_chips_gather, after)
    return push_start(tag + "_fwd_start", lands, None, plan_forward, None)


def gather_end(tag, state, after):
    bufs = push_wait(tag + "_fwd_wait", state, plan_forward, after, in_place=True)
    return [g.reshape(4, 2 * g.shape[1], g.shape[2]) for g in bufs]


PACK_QUANTUM = 8 * C


def _pack(parts, dtype):
    pieces = []
    for p in parts:
        flat = p.astype(dtype).reshape(-1)
        pieces.append(jnp.pad(flat, (0, -flat.shape[0] % PACK_QUANTUM)).reshape(-1, C))
    rows = sum(p.shape[0] for p in pieces)
    if rows % 16:
        pieces.append(jnp.zeros((8, C), dtype))
    return jnp.concatenate(pieces, axis=0)


def _unpack(packed, shapes):
    out, at = [], 0
    for s in shapes:
        n = int(np.prod(s))
        rows = -(-n // PACK_QUANTUM) * 8
        out.append(packed[at:at + rows].reshape(-1)[:n].reshape(s))
        at += rows
    return out


def all_reduce_begin(vec):
    rows, width = vec.shape
    (pair,) = swap_halves("ar_sibling", [jnp.broadcast_to(vec[None], (2, rows, width))])
    chip_sum = sum_slots(pair, "ar_sum_sibling").reshape(2, rows // 2, width)
    mine = lax.dynamic_index_in_dim(chip_sum, lax.axis_index("c"), axis=0, keepdims=True)
    return push_start("ar_start", [chip_sum], [jnp.broadcast_to(mine, (4, rows // 2, width))], plan_chips_half,
                      None)


def all_reduce_end(state, after):
    (by_chip,) = push_wait("ar_wait", state, plan_chips_half, after)
    (both,) = swap_halves("ar_halves", [sum_slots(by_chip, "ar_sum_chips", for_swap=True)])
    return both.reshape(2 * both.shape[1], both.shape[2])


def kernel(x, norm1_g, w_in, shift_mu, w0, w2, a0, a2, g2, k_k, k_a, r_k, gn_w, gn_b, sgu_ln_g, sgu_ln_b, sgu_w, sgu_b, sgu_out_g, w_out, norm2_g, w_gate, w_up, w_down, final_g, loss_target, m_norm1_g, m_w_in, m_shift_mu, m_w0, m_w2, m_a0, m_a2, m_g2, m_k_k, m_k_a, m_r_k, m_gn_w, m_gn_b, m_sgu_ln_g, m_sgu_ln_b, m_sgu_w, m_sgu_b, m_sgu_out_g, m_w_out, m_norm2_g, m_w_gate, m_w_up, m_w_down, m_final_g, v_norm1_g, v_w_in, v_shift_mu, v_w0, v_w2, v_a0, v_a2, v_g2, v_k_k, v_k_a, v_r_k, v_gn_w, v_gn_b, v_sgu_ln_g, v_sgu_ln_b, v_sgu_w, v_sgu_b, v_sgu_out_g, v_w_out, v_norm2_g, v_w_gate, v_w_up, v_w_down, v_final_g):
    args = dict(locals())
    names = ["norm1_g", "w_in", "shift_mu", "w0", "w2", "a0", "a2", "g2", "k_k", "k_a", "r_k", "gn_w", "gn_b",
             "sgu_ln_g", "sgu_ln_b", "sgu_w", "sgu_b", "sgu_out_g", "w_out", "norm2_g", "w_gate", "w_up", "w_down",
             "final_g"]
    big = ["w_in", "w_out", "w_gate", "w_up", "w_down"]
    xi, yi = lax.axis_index("x"), lax.axis_index("y")
    chip = 2 * xi + yi
    x2d, tgt = x[0], loss_target[0]
    t = x2d.shape[0]

    lora_names = ["w2", "a2", "g2"]
    first, later = ["w_in"] + lora_names, ["w_out", "w_gate", "w_up", "w_down"]

    def halves(n, token=None):
        m = args[n][0] if token is None else args[n][0] + token[0, 0]
        m = m.astype(BF16)
        return m.reshape(2, m.shape[0] // 2, m.shape[1])

    first_state, first_token = gather_begin("g0", [halves(n) for n in first], norm1_g)
    later_mine = [halves(n, first_token) for n in later]
    later_lands = _landing(later_mine)
    (h1,) = row_fwd("norm1", rms_fn, [Tiled(x2d)], [norm1_g + first_token[0, 0]], [], [(D, BF16)], _tile(t, 256))
    prepared = functools.reduce(lambda p, q: p + q, [a[0, 0, :1].astype(F32) for a in later_lands])
    first_state, first_token = gather_forward("g0", first_state, prepared + h1[0, :1].astype(F32))
    got = gather_end("g0", first_state, first_token)
    later_state, later_token = gather_begin("g1", later_mine, got[0], later_lands)
    full = {n: blk.transpose(1, 0, 2).reshape(blk.shape[1], 4 * blk.shape[2]) for n, blk in zip(first[1:], got[1:])}
    s0, s1, s2, s3 = got[0]
    w_in_p = jnp.concatenate([s0, s1, s2[:, :CUT_A], s2[:, CUT_B:], s3, s2[:, CUT_A:CUT_B],
                              jnp.zeros((D, P_W - D_IN), BF16)], axis=1)
    mu_rkv = shift_mu[:, :3 * C] + later_token[0, 0]
    mu_lora = jnp.pad(shift_mu[:, 3 * C:], ((0, 0), (0, LORA_W - 288)))
    w2p = jnp.pad(full["w2"], ((0, 64), (0, 0)))
    a2p = jnp.pad(full["a2"], ((64, 0), (0, 0)))
    g2p = jnp.pad(full["g2"], ((0, 96), (0, 0)))
    ii = np.arange(C)
    jm = _scan_consts()[2]
    e_np = (np.arange(LANES)[:, None] == (ii[None, :] // HEAD))
    e_mat, e_t = jnp.asarray(e_np, BF16), jnp.asarray(e_np.T, BF16)
    b_t = jnp.pad(sgu_b[0].T, ((0, 0), (0, LANES - HEADS)))
    r_k_flat = r_k.reshape(1, C)
    tm_row = _tile(t, 256)
    tm_sgu_b = _tile(t, 128)

    p = mm(h1, w_in_p, name="proj_in")
    ps_rkv = shift_fwd(p, mu_rkv, 0, "shift_rkv")
    ps_lora = shift_fwd(p, mu_lora, P_LORA // LORA_W, "shift_lora")
    pre_par = [w0, w2p, a0, a2p, g2p, k_k, k_a]
    r_, w_, k2, v_, an, b_, g_ = row_fwd("pre", pre_fn, [Tiled(ps_rkv), Tiled(ps_lora)], pre_par, [jm],
                                         [(C, F32)] * 7, tm_row)
    v3 = to_cols(v_)
    y3, ckpt = wkv_fwd(r_, w_, k2, v3, an, b_)
    later_state, later_token = gather_forward("g1", later_state, y3)
    y_ = from_cols(y3)
    post_in = [Tiled(z) for z in (y_, r_, k2, v_, g_)]
    post_par = [gn_w, gn_b, r_k_flat]
    (y_rwkv,) = row_fwd("post", post_fn, post_in, [gn_w + later_token[0, 0]] + post_par[1:], [jm], [(C, BF16)],
                        tm_row)
    sgu_in = [Tiled(p, C, 3), Tiled(p, C, 4)]
    sgu_par = [sgu_ln_g, sgu_ln_b, sgu_w[0], b_t, sgu_out_g]
    (y_sgu,) = row_fwd("sgu", sgu_fn, sgu_in, sgu_par, [e_mat, e_t], [(C, BF16)], tm_row)
    y_cat = jnp.concatenate([y_rwkv, y_sgu], axis=1)
    for n, blk in zip(later, gather_end("g1", later_state, y_cat)):
        full[n] = blk if n in ("w_gate", "w_up") else blk.reshape(4 * blk.shape[1], blk.shape[2])
    x_mid = mm(y_cat, full["w_out"], res=x2d, name="proj_out")
    (h2,) = row_fwd("norm2", rms_fn, [Tiled(x_mid)], [norm2_g], [], [(D, BF16)], tm_row)
    gate = mm(h2, full["w_gate"], shards="b", name="ffn_gate")
    up, act = mm(h2, full["w_up"], shards="b", name="ffn_up",
                 post=(lambda up_tile, gate_tile: (up_tile, swiglu_fn(gate_tile, up_tile)[0]), [gate], [F32, BF16]))
    x_out = mm(act, full["w_down"], res=x_mid, name="ffn_down")
    loss_part, dx_out, d_final_g, dx_out_b = loss_head(x_out, tgt, final_g.reshape(1, D))

    grads = {"final_g": d_final_g.reshape(D)}
    d_gate, d_up = mm(dx_out_b, full["w_down"], tb=True, name="d_act",
                      post=(lambda d_act, g_tile, u_tile: jax.vjp(swiglu_fn, g_tile, u_tile)[1]((d_act,)), [gate, up],
                            [BF16, BF16]))
    grads["w_down"] = mm(act, dx_out_b, ta=True, out_dtype=BF16, name="d_w_down")
    grads["w_gate"] = mm(h2, d_gate, shards="out", out_dtype=BF16, name="d_w_gate")
    grads["w_up"] = mm(h2, d_up, shards="out", out_dtype=BF16, name="d_w_up")
    d_h2 = mm(d_gate, full["w_gate"], shards="bt", name="d_h2_gate")
    d_h2 = mm(d_up, full["w_up"], shards="bt", res=d_h2, name="d_h2_up")
    dx_mid, grads["norm2_g"], dx_mid_b = row_bwd("norm2_b", rms_res_fn, [Tiled(x_mid)], [norm2_g], [],
                                                 [Tiled(d_h2), Tiled(dx_out)], tm_row, bf16_copy_of=(0,))
    d_ycat = mm(dx_mid_b, full["w_out"], tb=True, name="d_ycat")
    grads["w_out"] = mm(y_cat, dx_mid_b, ta=True, out_dtype=BF16, name="d_w_out")
    rs_state, rs_token = sibling_exchange_begin([grads[n].reshape((4,) + args[n].shape[1:]) for n in later], "rs2")
    d_pu, d_pv, grads["sgu_ln_g"], grads["sgu_ln_b"], d_sgu_w, d_b_t, grads["sgu_out_g"] = row_bwd(
        "sgu_b", sgu_fn, sgu_in, sgu_par[:-1] + [sgu_out_g + rs_token[0, 0]], [e_mat, e_t], [Tiled(d_ycat, C, 1)],
        tm_sgu_b, tiled_out_dtypes=[BF16, BF16])
    grads["sgu_w"] = d_sgu_w[None]
    grads["sgu_b"] = d_b_t[:, :HEADS].T[None]
    post_par_b = [gn_w + rs_token[0, 0]] + post_par[1:]
    d_y, d_r1, d_k1, d_v1, d_g, grads["gn_w"], grads["gn_b"], d_r_k = row_bwd(
        "post_b", post_fn, post_in, post_par_b, [jm], [Tiled(d_ycat, C, 0)], tm_row)
    grads["r_k"] = d_r_k.reshape(r_k.shape)
    rs_state, rs_token = reduce_scatter_continue(rs_state, later, "rs2", d_y[0, :1] + d_pu[0, :1].astype(F32))
    d_r2, d_w, d_k2, d_a, d_b, d_v3 = wkv_bwd(r_, w_, k2, v3, an, b_, ckpt, to_cols(d_y), rs_token)
    d_v2 = from_cols(d_v3)
    pre_cts = [(Tiled(d_r1), Tiled(d_r2)), Tiled(d_w), (Tiled(d_k1), Tiled(d_k2)), (Tiled(d_v1), Tiled(d_v2)),
               Tiled(d_a), Tiled(d_b), Tiled(d_g)]
    d_ps_rkv, d_ps_lora, grads["w0"], d_w2p, grads["a0"], d_a2p, d_g2p, grads["k_k"], grads["k_a"] = row_bwd(
        "pre_b", pre_fn, [Tiled(ps_rkv), Tiled(ps_lora)], pre_par, [jm], pre_cts, tm_row)
    grads["w2"], grads["a2"], grads["g2"] = d_w2p[None, :64], d_a2p[None, 64:], d_g2p[None, :160]
    d_p_rkv, d_mu_rkv = shift_bwd(d_ps_rkv, p, mu_rkv, 0, "shift_rkv_b")
    d_p_lora, d_mu_lora = shift_bwd(d_ps_lora, p, mu_lora, P_LORA // LORA_W, "shift_lora_b")
    grads["shift_mu"] = jnp.concatenate([d_mu_rkv, d_mu_lora[:, :288]], axis=1)
    d_p = jnp.concatenate([d_p_rkv, d_pu, d_pv, d_p_lora], axis=1)
    d_h1 = mm(d_p, w_in_p, tb=True, name="d_h1")
    d_w_in_p = mm(h1, d_p, ta=True, out_dtype=BF16, name="d_w_in")
    ws = D_IN // 4
    shard2 = jnp.concatenate([d_w_in_p[:, 2 * ws:3 * C], d_w_in_p[:, P_LORA:P_LORA + 288],
                              d_w_in_p[:, 3 * C:3 * C + ws - CUT_B]], axis=1)
    grads["w_in"] = jnp.stack([d_w_in_p[:, :ws], d_w_in_p[:, ws:2 * ws], shard2,
                               d_w_in_p[:, 3 * C + ws - CUT_B:P_LORA]])
    dx, grads["norm1_g"] = row_bwd("norm1_b", rms_res_fn, [Tiled(x2d)], [norm1_g], [],
                                   [Tiled(d_h1), Tiled(dx_mid)], tm_row)

    small = [n for n in names if n not in big]
    small_shapes = [(1, 64, C) if n in ("w2", "a2") else (1, 160, C) if n == "g2" else args[n].shape for n in small]
    vec = _pack([grads[n] for n in small] + [loss_part[:, :1]], F32)
    w_in_state, w_in_token = reduce_scatter_begin([grads["w_in"]], ["w_in"], "rs1")
    small_state, small_token = all_reduce_begin(vec)
    total = {n: g[None] for n, g in zip(later, reduce_scatter_end(rs_state, later, "rs2", w_in_token + small_token))}
    delta, new_m, new_v = {}, {}, {}

    def adam_big(n):
        s = args[n].shape
        view = (lambda a: a[0].T) if n == "w_in" else (lambda a: a[0])
        back = (lambda a: a.T.reshape(s)) if n == "w_in" else (lambda a: a.reshape(s))
        d_, m_, v__ = adamw(view(args[n]), view(total[n]), view(args["m_" + n]), view(args["v_" + n]), "adamw_" + n)
        delta[n], new_m[n], new_v[n] = back(d_), back(m_), back(v__)

    for n in later:
        adam_big(n)
    total["w_in"] = reduce_scatter_end(w_in_state, ["w_in"], "rs1", new_v["w_down"])[0][None]
    *small_sums, loss = _unpack(all_reduce_end(small_state, new_v["w_down"]), small_shapes + [()])
    for n, g_sum in zip(small, small_sums):
        total[n] = lax.dynamic_slice_in_dim(g_sum, chip * 256, 256, axis=2) if n in lora_names else g_sum
    adam_big("w_in")
    flat = [_pack([src[n] for n in small], F32)
            for src in (args, total, {n: args["m_" + n] for n in small}, {n: args["v_" + n] for n in small})]
    outs = adamw(*flat, "adamw_small")
    for res, o in zip((delta, new_m, new_v), outs):
        res.update(zip(small, _unpack(o, [args[n].shape for n in small])))
    return (loss, dx[None], *[total[n] for n in names], *[delta[n] for n in names],
            *[new_m[n] for n in names], *[new_v[n] for n in names])
```

```python
import functools

import numpy as np
import jax
import jax.numpy as jnp
from jax import lax
from jax.experimental import pallas as pl
from jax.experimental.pallas import tpu as pltpu

F32 = jnp.float32
BF16 = jnp.bfloat16

D = 2048
C = 1024
HEADS = 16
HEAD = 64
LANES = 128
P_W = 5632
P_LORA = 5120
LORA_W = 512
D_IN = 5408
CUT_A = 3 * C - 2 * (D_IN // 4)
CUT_B = CUT_A + 288
CHUNK = 128
RMS_EPS = 1e-6
LN_EPS = 1e-5
GN_EPS = 64e-5
L2_EPS = 1e-12
VMEM_LIMIT = 56 * 1024 * 1024


def _pcall(body, **kw):
    return pl.pallas_call(body, **kw)


def _cparams(sem):
    return pltpu.CompilerParams(dimension_semantics=sem, vmem_limit_bytes=VMEM_LIMIT)


def _tile(n, most):
    t = most
    while t > 8 and n % t:
        t //= 2
    assert n % t == 0, (n, most)
    return t


MM_TILE = (2048, 512, 2048)
MM_VMEM = 40 * 1024 * 1024


def _div_tile(n, most, quantum=LANES):
    for t in range(min(n, most) // quantum * quantum, 0, -quantum):
        if n % t == 0:
            return t
    raise ValueError((n, most, quantum))


def _mm_tiles(m, n, k, a_bytes, b_bytes, r_bytes, o_bytes):
    tm, tn, tk = _div_tile(m, MM_TILE[0]), _div_tile(n, MM_TILE[1]), _div_tile(k, MM_TILE[2])

    def need(tm, tk):
        return 2 * (tm * tk * a_bytes + tk * tn * b_bytes + tm * tn * (r_bytes + o_bytes)) + tm * tn * 4

    while need(tm, tk) > MM_VMEM:
        if tk >= tm and tk > 512:
            tk = _div_tile(k, tk - LANES)
        else:
            tm = _div_tile(m, tm - LANES)
    return tm, tn, tk


def mm(a, b, *, ta=False, tb=False, res=None, out_dtype=F32, name, shards=None, post=None):
    if shards is not None:
        return _mm_shards(a, b, res, out_dtype, name, shards, post)
    m, k = (a.shape[1], a.shape[0]) if ta else a.shape
    n = b.shape[0] if tb else b.shape[1]
    assert (b.shape[1] if tb else b.shape[0]) == k
    if post is None:
        tm, tn, tk = _mm_tiles(m, n, k, a.dtype.itemsize, b.dtype.itemsize,
                               0 if res is None else res.dtype.itemsize, jnp.dtype(out_dtype).itemsize)
    else:
        tm, tn, tk = _div_tile(m, 1024), _div_tile(n, MM_TILE[1]), _div_tile(k, MM_TILE[2])
    nk = k // tk
    dims = (((0 if ta else 1,), (1 if tb else 0,)), ((), ()))
    a_spec = pl.BlockSpec((tk, tm), lambda i, j, l: (l, i)) if ta else pl.BlockSpec((tm, tk), lambda i, j, l: (i, l))
    b_spec = pl.BlockSpec((tn, tk), lambda i, j, l: (j, l)) if tb else pl.BlockSpec((tk, tn), lambda i, j, l: (l, j))
    o_spec = pl.BlockSpec((tm, tn), lambda i, j, l: (i, j))
    return _mm_call(a, b, res, dims, (m // tm, n // tn, nk), a_spec, b_spec, o_spec, o_spec, (tm, tn),
                    jax.ShapeDtypeStruct((m, n), out_dtype), name, post)


def _mm_shards(a, b, res, out_dtype, name, shards, post=None):
    assert post is None or shards == "b"
    if shards == "b":
        (m, k), ns = a.shape, b.shape[2]
        tm, tk = _div_tile(m, 1024 if post is None else 512), _div_tile(k, 2048)
        grid, dims, acc = (m // tm, 4, k // tk), (((1,), (0,)), ((), ())), (tm, ns)
        a_spec = pl.BlockSpec((tm, tk), lambda i, s, l: (i, l))
        b_spec = pl.BlockSpec((None, tk, ns), lambda i, s, l: (s, l, 0))
        o_spec = pl.BlockSpec((tm, ns), lambda i, s, l: (i, s))
        out = jax.ShapeDtypeStruct((m, 4 * ns), out_dtype)
    elif shards == "bt":
        m, (_, n, ns) = a.shape[0], b.shape
        tm, tn = _div_tile(m, 2048), _div_tile(n, 512)
        grid, dims, acc = (m // tm, n // tn, 4), (((1,), (1,)), ((), ())), (tm, tn)
        a_spec = pl.BlockSpec((tm, ns), lambda i, j, s: (i, s))
        b_spec = pl.BlockSpec((None, tn, ns), lambda i, j, s: (s, j, 0))
        o_spec = pl.BlockSpec((tm, tn), lambda i, j, s: (i, j))
        out = jax.ShapeDtypeStruct((m, n), out_dtype)
    else:
        (t, m), ns = a.shape, b.shape[1] // 4
        tm, tk = _div_tile(m, 1024), _div_tile(t, 2048)
        grid, dims, acc = (m // tm, 4, t // tk), (((0,), (0,)), ((), ())), (tm, ns)
        a_spec = pl.BlockSpec((tk, tm), lambda i, s, l: (l, i))
        b_spec = pl.BlockSpec((tk, ns), lambda i, s, l: (l, s))
        o_spec = pl.BlockSpec((None, tm, ns), lambda i, s, l: (s, i, 0))
        out = jax.ShapeDtypeStruct((4, m, ns), out_dtype)
    return _mm_call(a, b, res, dims, grid, a_spec, b_spec, o_spec, o_spec, acc, out, name, post)


def ffn_in(h, w_gate, w_up, tile_fn):
    (m, k), ns = h.shape, w_gate.shape[2]
    tm = _div_tile(m, 256)

    def body(h_ref, g_ref, u_ref, gate_ref, up_ref, act_ref):
        hv = h_ref[...]
        gate = jnp.dot(hv, g_ref[...], preferred_element_type=F32)
        up = jnp.dot(hv, u_ref[...], preferred_element_type=F32)
        gate_ref[...] = gate
        up_ref[...] = up
        act_ref[...] = tile_fn(gate, up).astype(BF16)

    w_spec = pl.BlockSpec((None, k, ns), lambda s, i: (s, 0, 0))
    o_spec = pl.BlockSpec((tm, ns), lambda s, i: (i, s))
    return _pcall(
        body, name="ffn_in", grid=(4, m // tm),
        in_specs=[pl.BlockSpec((tm, k), lambda s, i: (i, 0)), w_spec, w_spec], out_specs=[o_spec] * 3,
        out_shape=[jax.ShapeDtypeStruct((m, 4 * ns), dt) for dt in (F32, F32, BF16)],
        compiler_params=_cparams(("parallel", "parallel")),
    )(h, w_gate, w_up)


def _mm_call(a, b, res, dims, grid, a_spec, b_spec, r_spec, o_spec, acc_shape, out, name, post=None):
    nk = grid[2]
    post_fn, post_in, post_dtypes = (None, [], [out.dtype]) if post is None else post
    n_extra = (res is not None) + len(post_in)

    def body(*refs):
        a_ref, b_ref = refs[:2]
        r_ref = None if res is None else refs[2]
        p_refs = refs[2 + (res is not None):2 + n_extra]
        o_refs = refs[2 + n_extra:2 + n_extra + len(post_dtypes)]

        def finish(acc):
            if r_ref is not None:
                acc = acc + r_ref[...].astype(F32)
            outs = (acc,) if post_fn is None else post_fn(acc, *[p[...] for p in p_refs])
            for o_ref, o in zip(o_refs, outs):
                o_ref[...] = o.astype(o_ref.dtype)

        prod = lax.dot_general(a_ref[...].astype(BF16), b_ref[...].astype(BF16), dims, preferred_element_type=F32)
        if nk == 1:
            finish(prod)
            return
        acc_ref = refs[-1]
        kk = pl.program_id(2)

        @pl.when(kk == 0)
        def _():
            acc_ref[...] = prod

        @pl.when(kk > 0)
        def _():
            acc_ref[...] += prod

        @pl.when(kk == nk - 1)
        def _():
            finish(acc_ref[...])

    in_specs = [a_spec, b_spec] + ([r_spec] if res is not None else []) + [o_spec] * len(post_in)
    args = (a, b) + ((res,) if res is not None else ()) + tuple(post_in)
    outs = _pcall(
        body, name=name, grid=grid, in_specs=in_specs, out_specs=[o_spec] * len(post_dtypes),
        out_shape=[jax.ShapeDtypeStruct(out.shape, dt) for dt in post_dtypes],
        scratch_shapes=[] if nk == 1 else [pltpu.VMEM(acc_shape, F32)],
        compiler_params=_cparams(("parallel", "parallel", "arbitrary")),
    )(*args)
    return outs[0] if post is None else outs


class Tiled:
    def __init__(self, arr, width=None, col=0):
        self.arr, self.width, self.col = arr, (arr.shape[1] if width is None else width), col

    def spec(self, tm):
        col = self.col
        return pl.BlockSpec((tm, self.width), lambda i: (i, col))


def _full_spec(p):
    nd = p.ndim
    return pl.BlockSpec(p.shape, lambda i: (0,) * nd)


def row_fwd(name, fn, tiled, params, consts, outs, tm):
    t = tiled[0].arr.shape[0]
    n_in = len(tiled) + len(params) + len(consts)

    def body(*refs):
        res = fn(*[r[...] for r in refs[:n_in]])
        for o_ref, r in zip(refs[n_in:], res):
            o_ref[...] = r.astype(o_ref.dtype)

    return _pcall(
        body, name=name, grid=(t // tm,),
        in_specs=[x.spec(tm) for x in tiled] + [_full_spec(p) for p in params + consts],
        out_specs=[pl.BlockSpec((tm, w), lambda i: (i, 0)) for w, _ in outs],
        out_shape=[jax.ShapeDtypeStruct((t, w), dt) for w, dt in outs],
        compiler_params=_cparams(("parallel",)),
    )(*[x.arr for x in tiled], *params, *consts)


def row_bwd(name, fn, tiled, params, consts, cts, tm, tiled_out_dtypes=None, bf16_copy_of=()):
    t = tiled[0].arr.shape[0]
    nt, npar, ncon = len(tiled), len(params), len(consts)
    cts = [c if isinstance(c, tuple) else (c,) for c in cts]
    flat_cts = [x for c in cts for x in c]
    tiled_out_dtypes = tiled_out_dtypes or [F32] * nt
    copies = [tiled[j] for j in bf16_copy_of]

    def body(*refs):
        n_in = nt + npar + ncon
        ins = [r[...].astype(F32) for r in refs[:nt + npar]]
        con = [r[...] for r in refs[nt + npar:n_in]]
        ct_refs = list(refs[n_in:n_in + len(flat_cts)])
        o = refs[n_in + len(flat_cts):]
        ct = []
        for c in cts:
            parts = [ct_refs.pop(0)[...].astype(F32) for _ in c]
            ct.append(functools.reduce(lambda p, q: p + q, parts))
        _, vjp = jax.vjp(lambda *a: fn(*a, *con), *ins)
        g = vjp(tuple(ct))
        for j in range(nt):
            o[j][...] = g[j].astype(o[j].dtype)
        for n, j in enumerate(bf16_copy_of):
            o[nt + npar + n][...] = g[j].astype(BF16)
        first = pl.program_id(0) == 0

        @pl.when(first)
        def _():
            for j in range(npar):
                o[nt + j][...] = g[nt + j]

        @pl.when(jnp.logical_not(first))
        def _():
            for j in range(npar):
                o[nt + j][...] += g[nt + j]

    return _pcall(
        body, name=name, grid=(t // tm,),
        in_specs=[x.spec(tm) for x in tiled] + [_full_spec(p) for p in params + consts]
        + [x.spec(tm) for x in flat_cts],
        out_specs=[pl.BlockSpec((tm, x.width), lambda i: (i, 0)) for x in tiled] + [_full_spec(p) for p in params]
        + [pl.BlockSpec((tm, x.width), lambda i: (i, 0)) for x in copies],
        out_shape=[jax.ShapeDtypeStruct((t, x.width), dt) for x, dt in zip(tiled, tiled_out_dtypes)]
        + [jax.ShapeDtypeStruct(p.shape, F32) for p in params]
        + [jax.ShapeDtypeStruct((t, x.width), BF16) for x in copies],
        compiler_params=_cparams(("arbitrary",)),
    )(*[x.arr for x in tiled], *params, *consts, *[x.arr for x in flat_cts])


def _split_dot(x, w):
    hi = x.astype(BF16)
    lo = (x - hi.astype(F32)).astype(BF16)
    return jnp.dot(hi, w, preferred_element_type=F32) + jnp.dot(lo, w, preferred_element_type=F32)


def _headsum_blocks(x, j2k):
    out = []
    for c0 in range(0, x.shape[1], LANES):
        blk = x[:, c0:c0 + LANES]
        hi = blk.astype(BF16)
        lo = (blk - hi.astype(F32)).astype(BF16)
        out.append(jnp.dot(jnp.concatenate([hi, lo], axis=1), j2k, preferred_element_type=F32))
    return jnp.concatenate(out, axis=1)


@jax.custom_vjp
def headsum(x, j2k):
    return _headsum_blocks(x, j2k)


def _headsum_fwd(x, j2k):
    return _headsum_blocks(x, j2k), j2k


def _headsum_bwd(j2k, ct):
    return _headsum_blocks(ct, j2k), jnp.zeros_like(j2k)


headsum.defvjp(_headsum_fwd, _headsum_bwd)


def _bdot(x, w):
    return jnp.dot(x.astype(BF16), w.astype(BF16), preferred_element_type=F32)


def _sigmoid(x):
    return jax.nn.sigmoid(x)


def _softplus(x):
    return jnp.maximum(x, 0.0) + jnp.log(1.0 + jnp.exp(-jnp.abs(x)))


def rms_fn(x, g):
    return (x * lax.rsqrt(jnp.mean(x * x, axis=-1, keepdims=True) + RMS_EPS) * g,)


def rms_res_fn(x, g):
    return rms_fn(x, g)[0], x


def pre_fn(rkv, lora, w0, w2p, a0, a2p, g2p, k_k, k_a, jm):
    r, k, v = rkv[:, :C], rkv[:, C:2 * C], rkv[:, 2 * C:]
    zwa, zg = lora[:, :LANES], lora[:, LANES:LANES + 256]
    w_log = -_softplus(-(w0 + _bdot(jnp.tanh(zwa), w2p))) - 0.5
    decay = jnp.exp(-jnp.exp(w_log))
    a = _sigmoid(a0 + _bdot(zwa, a2p))
    g = _bdot(_sigmoid(zg), g2p)
    kk = k * k_k
    kk = kk / jnp.maximum(jnp.sqrt(headsum(kk * kk, jm)), L2_EPS)
    k2 = k * (1.0 + (a - 1.0) * k_a)
    return r, decay, k2, v, -kk, kk * a, g


def post_fn(y, r, k2, v, g, gn_w, gn_b, r_k, jm):
    mu = headsum(y, jm) * (1.0 / HEAD)
    yc = y - mu
    var = headsum(yc * yc, jm) * (1.0 / HEAD)
    yn = yc * lax.rsqrt(var + GN_EPS) * gn_w + gn_b
    bonus = headsum(r * k2 * r_k, jm) * v
    return ((yn + bonus) * g,)


def _gelu(x):
    return 0.5 * x * (1.0 + lax.erf(x * np.float32(1.0 / np.sqrt(2.0))))


@jax.custom_vjp
def expand_groups(b_t, e, e_t):
    return _split_dot(b_t, e)


def _expand_groups_fwd(b_t, e, e_t):
    return _split_dot(b_t, e), (e, e_t)


def _expand_groups_bwd(res, ct):
    e, e_t = res
    return _split_dot(ct, e_t), jnp.zeros_like(e), jnp.zeros_like(e_t)


expand_groups.defvjp(_expand_groups_fwd, _expand_groups_bwd)


def sgu_fn(pu, pv, ln_g, ln_b, w_s, b_t, out_g, e, e_t):
    rows = pu.shape[0]
    b_exp = expand_groups(b_t, e, e_t)
    u, v = _gelu(pu), _gelu(pv)
    mu = jnp.mean(v, axis=-1, keepdims=True)
    vc = v - mu
    var = jnp.mean(vc * vc, axis=-1, keepdims=True)
    v = vc * lax.rsqrt(var + LN_EPS) * ln_g + ln_b
    tri = lax.broadcasted_iota(jnp.int32, (CHUNK, CHUNK), 0) >= lax.broadcasted_iota(jnp.int32, (CHUNK, CHUNK), 1)
    left = lax.broadcasted_iota(jnp.int32, (CHUNK, LANES), 1) < HEAD
    chunks = []
    for c0 in range(0, rows, CHUNK):
        cols = []
        for gp in range(HEADS // 2):
            vp = v[c0:c0 + CHUNK, gp * LANES:(gp + 1) * LANES]
            wa = jnp.where(tri, w_s[2 * gp], 0.0)
            wb = jnp.where(tri, w_s[2 * gp + 1], 0.0)
            cols.append(jnp.where(left, _bdot(wa, vp), _bdot(wb, vp)))
        chunks.append(jnp.concatenate(cols, axis=1) + b_exp)
    s = jnp.concatenate(chunks, axis=0) if len(chunks) > 1 else chunks[0]
    y = u * s
    return (y * lax.rsqrt(jnp.mean(y * y, axis=-1, keepdims=True) + RMS_EPS) * out_g,)


def swiglu_fn(gate, up):
    return (gate * _sigmoid(gate) * up,)


SCAN_CHUNK = 32
NB = C // LANES
FWD_GROUP = 2
BWD_GROUP = 8


def to_cols(a):
    t = a.shape[0]
    return a.reshape(t // 8, LANES, HEAD).transpose(0, 2, 1)


def from_cols(a):
    return a.transpose(0, 2, 1).reshape(a.shape[0] * 8, C)


def _scan_consts():
    i = np.arange(LANES)
    spread = (i[:, None] % 2) == (i[None, :] // HEAD)
    pick = (i[:, None] // HEAD) == (i[None, :] % 2)
    j2 = (i[:, None] // HEAD) == (i[None, :] // HEAD)
    sel = (i[None, :] // 2) == (np.arange(8 * NB * HEAD)[:, None] // HEAD)
    return (jnp.asarray(spread, BF16), jnp.asarray(pick, BF16), jnp.asarray(np.concatenate([j2, j2], 0), BF16),
            jnp.asarray(sel, BF16))


def _stack(blocks):
    return jnp.concatenate(blocks, axis=0) if len(blocks) > 1 else blocks[0]


def _unstack(x, n):
    return [x[i * HEAD:(i + 1) * HEAD] for i in range(n)]


def _headsums(blocks, j2k, group):
    res = []
    for g0 in range(0, len(blocks), group):
        x = _stack(blocks[g0:g0 + group])
        hi = x.astype(BF16)
        lo = (x - hi.astype(F32)).astype(BF16)
        out = jnp.dot(jnp.concatenate([hi, lo], axis=1), j2k, preferred_element_type=F32)
        res += _unstack(out, len(blocks[g0:g0 + group]))
    return res


def _headsums_out(blocks, pick):
    return _unstack(jnp.dot(_stack(blocks).astype(BF16), pick, preferred_element_type=F32), len(blocks))


def _expand8(tile, sel, spread):
    lhs = jnp.tile(tile.astype(BF16), (8 * NB, 1)) * sel
    return _unstack(jnp.dot(lhs, spread, preferred_element_type=F32), 8 * NB)


def _collapse(tile, blocks, first, lane_pair):
    for n, blk in enumerate(blocks):
        tile = jnp.where(jnp.tile(lane_pair == first + n, (HEAD // 8, 1)), blk, tile)
    return tile


def _row(tile, j, cb):
    return jnp.broadcast_to(tile[j:j + 1, cb * LANES:(cb + 1) * LANES], (HEAD, LANES))


def _scan_step(s_ref, rows, vexp, j2k, j, hist=None):
    w_t, k_t, a_t, b_t, r_t = rows
    s = [s_ref[:, cb * LANES:(cb + 1) * LANES] for cb in range(NB)]
    sab = _headsums([s[cb] * _row(a_t, j, cb) for cb in range(NB)], j2k, FWD_GROUP if hist is None else BWD_GROUP)
    out = []
    for cb in range(NB):
        sl = slice(cb * LANES, (cb + 1) * LANES)
        s_new = s[cb] * _row(w_t, j, cb) + sab[cb] * _row(b_t, j, cb) + vexp[j * NB + cb] * _row(k_t, j, cb)
        s_ref[:, sl] = s_new
        if hist is None:
            out.append(s_new * _row(r_t, j, cb))
        else:
            s_hist, sab_hist, idx = hist
            s_hist[idx + 1, :, sl] = s_new
            sab_hist[idx, :, sl] = sab[cb]
    return out


def wkv_fwd(r, w, k, v3, a, b):
    t = r.shape[0]
    nc = t // SCAN_CHUNK
    n8 = SCAN_CHUNK // 8

    def body(r_ref, w_ref, k_ref, a_ref, b_ref, v3_ref, spread_ref, pick_ref, j2k_ref, sel_ref, y3_ref, ck_ref,
             s_ref):
        @pl.when(pl.program_id(0) == 0)
        def _():
            s_ref[...] = jnp.zeros_like(s_ref)

        ck_ref[0] = s_ref[...]
        spread, pick, j2k, sel = spread_ref[...], pick_ref[...], j2k_ref[...], sel_ref[...]
        lane_pair = lax.broadcasted_iota(jnp.int32, (8, LANES), 1) // 2

        def t8_body(t8, carry):
            row0 = pl.multiple_of(t8 * 8, 8)
            rows = [ref[pl.ds(row0, 8), :] for ref in (w_ref, k_ref, a_ref, b_ref, r_ref)]
            vexp = _expand8(v3_ref[t8], sel, spread)
            y3 = jnp.zeros((HEAD, LANES), F32)
            for j in range(8):
                y3 = _collapse(y3, _headsums_out(_scan_step(s_ref, rows, vexp, j2k, j), pick), j * NB, lane_pair)
            y3_ref[t8] = y3
            return carry

        lax.fori_loop(0, n8, t8_body, 0)

    row_spec = pl.BlockSpec((SCAN_CHUNK, C), lambda i: (i, 0))
    col_spec = pl.BlockSpec((n8, HEAD, LANES), lambda i: (i, 0, 0))
    consts = _scan_consts()
    return _pcall(
        body, name="wkv_fwd", grid=(nc,),
        in_specs=[row_spec] * 5 + [col_spec] + [pl.BlockSpec(c.shape, lambda i: (0, 0)) for c in consts],
        out_specs=[col_spec, pl.BlockSpec((1, HEAD, C), lambda i: (i, 0, 0))],
        out_shape=[jax.ShapeDtypeStruct((t // 8, HEAD, LANES), F32), jax.ShapeDtypeStruct((nc, HEAD, C), F32)],
        scratch_shapes=[pltpu.VMEM((HEAD, C), F32)],
        compiler_params=_cparams(("arbitrary",)),
    )(r, w, k, a, b, v3, *consts)


def wkv_bwd(r, w, k, v3, a, b, ckpt, dy3):
    t = r.shape[0]
    nc = t // SCAN_CHUNK
    n8 = SCAN_CHUNK // 8

    def body(r_ref, w_ref, k_ref, a_ref, b_ref, v3_ref, dy3_ref, ck_ref, spread_ref, pick_ref, j2k_ref, sel_ref,
             dr_ref, dw_ref, dk_ref, da_ref, db_ref, dv3_ref, s_ref, g_ref, s_hist, sab_hist):
        @pl.when(pl.program_id(0) == 0)
        def _():
            g_ref[...] = jnp.zeros_like(g_ref)

        spread, pick, j2k, sel = spread_ref[...], pick_ref[...], j2k_ref[...], sel_ref[...]
        lane_pair = lax.broadcasted_iota(jnp.int32, (8, LANES), 1) // 2
        sub = lax.broadcasted_iota(jnp.int32, (8, LANES), 0)
        s_ref[...] = ck_ref[0]
        s_hist[0] = ck_ref[0]

        def redo(t8, carry):
            row0 = pl.multiple_of(t8 * 8, 8)
            rows = [ref[pl.ds(row0, 8), :] for ref in (w_ref, k_ref, a_ref, b_ref, r_ref)]
            vexp = _expand8(v3_ref[t8], sel, spread)
            for j in range(8):
                _scan_step(s_ref, rows, vexp, j2k, j, hist=(s_hist, sab_hist, t8 * 8 + j))
            return carry

        lax.fori_loop(0, n8, redo, 0)

        def back(q, carry):
            t8 = n8 - 1 - q
            row0 = pl.multiple_of(t8 * 8, 8)
            w_t, k_t, a_t, b_t, r_t = [ref[pl.ds(row0, 8), :] for ref in (w_ref, k_ref, a_ref, b_ref, r_ref)]
            vexp = _expand8(v3_ref[t8], sel, spread)
            dyexp = _expand8(dy3_ref[t8], sel, spread)
            dv3 = jnp.zeros((HEAD, LANES), F32)
            tiles = [[jnp.zeros((8, LANES), F32) for _ in range(NB)] for _ in range(5)]
            for j in range(7, -1, -1):
                idx = t8 * 8 + j
                g = [g_ref[:, cb * LANES:(cb + 1) * LANES] + dyexp[j * NB + cb] * _row(r_t, j, cb)
                     for cb in range(NB)]
                dsab = _headsums([g[cb] * _row(b_t, j, cb) for cb in range(NB)], j2k, BWD_GROUP)
                dv3 = _collapse(dv3, _headsums_out([g[cb] * _row(k_t, j, cb) for cb in range(NB)], pick), j * NB,
                                lane_pair)
                for cb in range(NB):
                    sl = slice(cb * LANES, (cb + 1) * LANES)
                    s_new = s_hist[idx + 1, :, sl]
                    s_old = s_hist[idx, :, sl]
                    sab = sab_hist[idx, :, sl]
                    sums = (s_new * dyexp[j * NB + cb], g[cb] * s_old, g[cb] * vexp[j * NB + cb],
                            s_old * dsab[cb], g[cb] * sab)
                    for n, prod in enumerate(sums):
                        rowsum = jnp.broadcast_to(jnp.sum(prod, axis=0, keepdims=True), (8, LANES))
                        tiles[n][cb] = jnp.where(sub == j, rowsum, tiles[n][cb])
                    g_ref[:, sl] = g[cb] * _row(w_t, j, cb) + dsab[cb] * _row(a_t, j, cb)
            dv3_ref[t8] = dv3
            for n, ref in enumerate((dr_ref, dw_ref, dk_ref, da_ref, db_ref)):
                for cb in range(NB):
                    ref[pl.ds(row0, 8), cb * LANES:(cb + 1) * LANES] = tiles[n][cb]
            return carry

        lax.fori_loop(0, n8, back, 0)

    row_spec = pl.BlockSpec((SCAN_CHUNK, C), lambda i: (nc - 1 - i, 0))
    col_spec = pl.BlockSpec((n8, HEAD, LANES), lambda i: (nc - 1 - i, 0, 0))
    consts = _scan_consts()
    return _pcall(
        body, name="wkv_bwd", grid=(nc,),
        in_specs=[row_spec] * 5 + [col_spec, col_spec, pl.BlockSpec((1, HEAD, C), lambda i: (nc - 1 - i, 0, 0))]
        + [pl.BlockSpec(c.shape, lambda i: (0, 0)) for c in consts],
        out_specs=[row_spec] * 5 + [col_spec],
        out_shape=[jax.ShapeDtypeStruct((t, C), F32)] * 5 + [jax.ShapeDtypeStruct((t // 8, HEAD, LANES), F32)],
        scratch_shapes=[pltpu.VMEM((HEAD, C), F32), pltpu.VMEM((HEAD, C), F32),
                        pltpu.VMEM((SCAN_CHUNK + 1, HEAD, C), F32), pltpu.VMEM((SCAN_CHUNK, HEAD, C), F32)],
        compiler_params=_cparams(("arbitrary",)),
    )(r, w, k, a, b, v3, dy3, ckpt, *consts)


def _prev_rows(cur, before, first_tile):
    last = jnp.where(first_tile, 0.0, before[7:8, :])
    row = lax.broadcasted_iota(jnp.int32, cur.shape, 0)
    return jnp.where(row == 0, last, pltpu.roll(cur, 1, 0))


def shift_fwd(p, mu, col, name):
    t, width = p.shape[0], mu.shape[1]
    tm = _tile(t, 256)

    def body(p_ref, before_ref, mu_ref, o_ref):
        cur = p_ref[...]
        prev = _prev_rows(cur, before_ref[...], pl.program_id(0) == 0)
        o_ref[...] = cur + (prev - cur) * mu_ref[...]

    return _pcall(
        body, name=name, grid=(t // tm,),
        in_specs=[pl.BlockSpec((tm, width), lambda i: (i, col)),
                  pl.BlockSpec((8, width), lambda i: (jnp.maximum(i * (tm // 8) - 1, 0), col)),
                  pl.BlockSpec((1, width), lambda i: (0, 0))],
        out_specs=pl.BlockSpec((tm, width), lambda i: (i, 0)),
        out_shape=jax.ShapeDtypeStruct((t, width), F32),
        compiler_params=_cparams(("parallel",)),
    )(p, p, mu)


def shift_bwd(dps, p, mu, col, name):
    t, width = dps.shape
    tm = _tile(t, 256)
    nt = t // tm

    def body(d_ref, after_ref, p_ref, before_ref, mu_ref, dp_ref, dmu_ref):
        i = pl.program_id(0)
        d, cur, mu_v = d_ref[...], p_ref[...], mu_ref[...]
        row = lax.broadcasted_iota(jnp.int32, d.shape, 0)
        first_after = jnp.where(i == nt - 1, 0.0, after_ref[0:1, :])
        nxt = jnp.where(row == tm - 1, first_after, pltpu.roll(d, tm - 1, 0))
        dp_ref[...] = (d * (1.0 - mu_v) + nxt * mu_v).astype(BF16)
        prev = _prev_rows(cur, before_ref[...], i == 0)
        part = jnp.sum(d * (prev - cur), axis=0, keepdims=True)

        @pl.when(i == 0)
        def _():
            dmu_ref[...] = part

        @pl.when(i > 0)
        def _():
            dmu_ref[...] += part

    return _pcall(
        body, name=name, grid=(nt,),
        in_specs=[pl.BlockSpec((tm, width), lambda i: (i, 0)),
                  pl.BlockSpec((8, width), lambda i: (jnp.minimum((i + 1) * (tm // 8), t // 8 - 1), 0)),
                  pl.BlockSpec((tm, width), lambda i: (i, col)),
                  pl.BlockSpec((8, width), lambda i: (jnp.maximum(i * (tm // 8) - 1, 0), col)),
                  pl.BlockSpec((1, width), lambda i: (0, 0))],
        out_specs=[pl.BlockSpec((tm, width), lambda i: (i, 0)), pl.BlockSpec((1, width), lambda i: (0, 0))],
        out_shape=[jax.ShapeDtypeStruct((t, width), BF16), jax.ShapeDtypeStruct((1, width), F32)],
        compiler_params=_cparams(("arbitrary",)),
    )(dps, dps, p, p, mu)


def loss_head(x3, tgt, g):
    t, d = x3.shape
    tm = _tile(t, 256)

    def body(x_ref, t_ref, g_ref, loss_ref, dx_ref, dg_ref, dxb_ref):
        (y,), vjp = jax.vjp(lambda a, b: rms_fn(a, b), x_ref[...], g_ref[...])
        diff = y - t_ref[...]
        part = 0.5 * jnp.sum(jnp.mean(diff * diff, axis=-1, keepdims=True), axis=0, keepdims=True)
        dx, dg = vjp((diff * (1.0 / d),))
        dx_ref[...] = dx
        dxb_ref[...] = dx.astype(BF16)
        part = jnp.broadcast_to(part, (1, LANES))
        first = pl.program_id(0) == 0

        @pl.when(first)
        def _():
            loss_ref[...] = part
            dg_ref[...] = dg

        @pl.when(jnp.logical_not(first))
        def _():
            loss_ref[...] += part
            dg_ref[...] += dg

    row = pl.BlockSpec((tm, d), lambda i: (i, 0))
    vec = pl.BlockSpec((1, d), lambda i: (0, 0))
    return _pcall(
        body, name="loss_head", grid=(t // tm,), in_specs=[row, row, vec],
        out_specs=[pl.BlockSpec((1, LANES), lambda i: (0, 0)), row, vec, row],
        out_shape=[jax.ShapeDtypeStruct((1, LANES), F32), jax.ShapeDtypeStruct((t, d), F32),
                   jax.ShapeDtypeStruct((1, d), F32), jax.ShapeDtypeStruct((t, d), BF16)],
        compiler_params=_cparams(("arbitrary",)),
    )(x3, tgt, g)


ADAM_LR, ADAM_B1, ADAM_B2, ADAM_EPS, ADAM_WD, ADAM_STEP = 0.001, 0.9, 0.999, 1e-08, 0.01, 10


def adamw(w, g, m, v, name):
    rows, width = w.shape
    tr = _div_tile(rows, 256, 8)

    def body(w_ref, g_ref, m_ref, v_ref, d_ref, nm_ref, nv_ref):
        gv = g_ref[...]
        m_new = ADAM_B1 * m_ref[...] + (1.0 - ADAM_B1) * gv
        v_new = ADAM_B2 * v_ref[...] + (1.0 - ADAM_B2) * (gv * gv)
        m_hat = m_new / (1.0 - ADAM_B1 ** ADAM_STEP)
        v_hat = v_new / (1.0 - ADAM_B2 ** ADAM_STEP)
        d_ref[...] = -ADAM_LR * (m_hat / (jnp.sqrt(v_hat) + ADAM_EPS) + ADAM_WD * w_ref[...])
        nm_ref[...] = m_new
        nv_ref[...] = v_new

    spec = pl.BlockSpec((tr, width), lambda i: (i, 0))
    return _pcall(
        body, name=name, grid=(rows // tr,), in_specs=[spec] * 4, out_specs=[spec] * 3,
        out_shape=[jax.ShapeDtypeStruct((rows, width), F32)] * 3,
        compiler_params=_cparams(("parallel",)),
    )(w, g, m, v)


def sum_slots(q, name, for_swap=False):
    n, rows, width = q.shape
    tr = _div_tile(rows, 512, 32 // q.dtype.itemsize)

    def body(*refs):
        refs = refs[for_swap:]
        acc = refs[0][...].astype(F32)
        for r in refs[1:n]:
            acc = acc + r[...].astype(F32)
        refs[n][...] = acc

    if not for_swap:
        specs = [pl.BlockSpec((None, tr, width), functools.partial(lambda s, i: (s, i, 0), s)) for s in range(n)]
        return _pcall(body, name=name, grid=(rows // tr,), in_specs=specs,
                      out_specs=pl.BlockSpec((tr, width), lambda i: (i, 0)),
                      out_shape=jax.ShapeDtypeStruct((rows, width), F32),
                      compiler_params=_cparams(("parallel",)))(*([q] * n))
    specs = [pl.BlockSpec((None, tr, width), functools.partial(lambda s, i, c_ref: (s, i, 0), s)) for s in range(n)]
    grid_spec = pltpu.PrefetchScalarGridSpec(
        num_scalar_prefetch=1, grid=(rows // tr,), in_specs=specs,
        out_specs=pl.BlockSpec((None, tr, width), lambda i, c_ref: (c_ref[0], i, 0)))
    return _pcall(body, name=name, grid_spec=grid_spec, out_shape=jax.ShapeDtypeStruct((2, rows, width), F32),
                  compiler_params=_cparams(("parallel",)))(
        lax.axis_index("c").astype(jnp.int32).reshape(1), *([q] * n))


MESH = pl.DeviceIdType.MESH
ANY_SPEC = pl.BlockSpec(memory_space=pl.ANY)


def _coords():
    return lax.axis_index("x"), lax.axis_index("y"), lax.axis_index("c")


def _other_chips(x, y):
    return [(1 - x, y), (x, 1 - y), (1 - x, 1 - y)]


def push(name, srcs, n_dst, plan_fn):
    n_arr = len(srcs)
    n_send = len(plan_fn(0, 0, 0))

    def body(*refs):
        src_refs, dst_refs = refs[:n_arr], refs[n_arr:2 * n_arr]
        send_sems, recv_sems = refs[2 * n_arr:]
        sends = plan_fn(*_coords())
        out = []
        for q, (src_ref, dst_ref) in enumerate(zip(src_refs, dst_refs)):
            out += [pltpu.make_async_remote_copy(src_ref.at[si], dst_ref.at[di], send_sems.at[q * n_send + k],
                                                 recv_sems.at[q * n_send + k], device_id=peer, device_id_type=MESH)
                    for k, (si, peer, di, _) in enumerate(sends)]
        for cp in out:
            cp.start()
        for q, (src_ref, dst_ref) in enumerate(zip(src_refs, dst_refs)):
            for k, (si, peer, _, ri) in enumerate(sends):
                pltpu.make_async_remote_copy(src_ref.at[si], dst_ref.at[ri], send_sems.at[q * n_send + k],
                                             recv_sems.at[q * n_send + k], device_id=peer,
                                             device_id_type=MESH).wait_recv()
        for cp in out:
            cp.wait_send()

    return _pcall(
        body, name=name, in_specs=[ANY_SPEC] * n_arr, out_specs=[ANY_SPEC] * n_arr,
        out_shape=[jax.ShapeDtypeStruct((n_dst,) + s.shape[1:], s.dtype) for s in srcs],
        scratch_shapes=[pltpu.SemaphoreType.DMA((n_arr * n_send,)), pltpu.SemaphoreType.DMA((n_arr * n_send,))],
    )(*srcs)


def plan_sibling_quarters(x, y, c):
    return [(2 * s + (1 - c), (x, y, 1 - c), s, s) for s in range(4)]


def plan_chips_by_shard(x, y, c):
    me = 2 * x + y
    return [(2 * px + py, (px, py, c), me, 2 * px + py) for px, py in _other_chips(x, y)]


def plan_chips_half(x, y, c):
    me = 2 * x + y
    return [(c, (px, py, c), me, 2 * px + py) for px, py in _other_chips(x, y)]


def swap_halves(name, bufs):
    n_arr = len(bufs)

    def body(*refs):
        buf_refs = refs[n_arr:2 * n_arr]
        send_sems, recv_sems = refs[2 * n_arr:]
        x, y, c = _coords()
        out = [pltpu.make_async_remote_copy(b.at[c], b.at[c], send_sems.at[q], recv_sems.at[q],
                                            device_id=(x, y, 1 - c), device_id_type=MESH)
               for q, b in enumerate(buf_refs)]
        for cp in out:
            cp.start()
        for q, b in enumerate(buf_refs):
            pltpu.make_async_remote_copy(b.at[1 - c], b.at[1 - c], send_sems.at[q], recv_sems.at[q],
                                         device_id=(x, y, 1 - c), device_id_type=MESH).wait_recv()
        for cp in out:
            cp.wait_send()

    return _pcall(
        body, name=name, in_specs=[ANY_SPEC] * n_arr, out_specs=[ANY_SPEC] * n_arr,
        out_shape=[jax.ShapeDtypeStruct(b.shape, b.dtype) for b in bufs],
        input_output_aliases={q: q for q in range(n_arr)},
        scratch_shapes=[pltpu.SemaphoreType.DMA((n_arr,)), pltpu.SemaphoreType.DMA((n_arr,))],
    )(*bufs)


HBM_SPEC = pl.BlockSpec(memory_space=pltpu.HBM)
SEM_SPEC = pl.BlockSpec(memory_space=pltpu.SEMAPHORE)
DATAFLOW = pltpu.SideEffectType.DATAFLOW_SIDE_EFFECTING


def _in_hbm(a):
    return pltpu.with_memory_space_constraint(a, pltpu.HBM)


def push_start(name, srcs, lands, plan_fn, after):
    n_arr = len(srcs)
    n_send = len(plan_fn(0, 0, 0))
    arrays = list(srcs) + ([] if lands is None else list(lands))
    if after is None:
        after = jnp.zeros((8, LANES), F32)
    n_all = len(arrays)

    def body(*refs):
        src_refs, land_refs = refs[:n_arr], refs[n_all - n_arr:n_all]
        send_sems, recv_sems = refs[n_all + 1:n_all + 3]
        token = refs[-1]
        sends = plan_fn(*_coords())
        for q, (src_ref, land_ref) in enumerate(zip(src_refs, land_refs)):
            for k, (si, peer, di, _) in enumerate(sends):
                pltpu.make_async_remote_copy(src_ref.at[si], land_ref.at[di], send_sems.at[q * n_send + k],
                                             recv_sems.at[q * n_send + k], device_id=peer,
                                             device_id_type=MESH).start()
        token[...] = jnp.zeros_like(token)

    sems = pltpu.SemaphoreType.DMA((n_arr * n_send,))
    out = _pcall(
        body, name=name,
        in_specs=[HBM_SPEC] * n_all + [ANY_SPEC],
        out_specs=[SEM_SPEC, SEM_SPEC] + [HBM_SPEC] * n_all + [pl.BlockSpec(memory_space=pltpu.VMEM)],
        out_shape=[sems, sems] + [pltpu.HBM(a.shape, a.dtype) for a in arrays]
        + [jax.ShapeDtypeStruct((8, LANES), F32)],
        input_output_aliases={q: 2 + q for q in range(n_all)},
        compiler_params=pltpu.CompilerParams(has_side_effects=DATAFLOW),
    )(*[_in_hbm(a) for a in arrays], after)
    return out[:-1], out[-1]


def push_wait(name, state, plan_fn, after, in_place=False, both=False):
    arrays = list(state[2:])
    n_all = len(arrays)
    n_arr = n_all if in_place else n_all // 2
    n_send = len(plan_fn(0, 0, 0))
    send_sems, recv_sems = state[:2]

    def body(*refs):
        src_refs, land_refs = refs[:n_arr], refs[n_all - n_arr:n_all]
        send_ref, recv_ref = refs[n_all:n_all + 2]
        sends = plan_fn(*_coords())
        for q, (src_ref, land_ref) in enumerate(zip(src_refs, land_refs)):
            for k, (si, peer, _, ri) in enumerate(sends):
                cp = pltpu.make_async_remote_copy(src_ref.at[si], land_ref.at[ri], send_ref.at[q * n_send + k],
                                                  recv_ref.at[q * n_send + k], device_id=peer, device_id_type=MESH)
                cp.wait_send()
                cp.wait_recv()

    out = _pcall(
        body, name=name,
        in_specs=[HBM_SPEC] * n_all + [SEM_SPEC, SEM_SPEC, ANY_SPEC],
        out_specs=[HBM_SPEC] * n_all,
        out_shape=[pltpu.HBM(a.shape, a.dtype) for a in arrays],
        input_output_aliases={q: q for q in range(n_all)},
        compiler_params=pltpu.CompilerParams(has_side_effects=DATAFLOW),
    )(*arrays, send_sems, recv_sems, after)
    return list(out) if both else list(out[n_all - n_arr:])


def plan_chips_gather(x, y, c):
    me = 2 * x + y
    return [(c, (px, py, c), 2 * me + c, 2 * (2 * px + py) + c) for px, py in _other_chips(x, y)]


def plan_forward(x, y, c):
    return [(2 * (2 * px + py) + c, (x, y, 1 - c), 2 * (2 * px + py) + c, 2 * (2 * px + py) + (1 - c))
            for px, py in _other_chips(x, y)]


def add_kept(g8, other, name):
    _, rows, cols = other.shape
    tr = _div_tile(rows, 512, 16)

    def body(c_ref, a_ref, b_ref, o_ref):
        o_ref[...] = (a_ref[...].astype(F32) + b_ref[...].astype(F32)).astype(BF16)

    spec = pl.BlockSpec((None, tr, cols), lambda s, i, c_ref: (s, i, 0))
    grid_spec = pltpu.PrefetchScalarGridSpec(
        num_scalar_prefetch=1, grid=(4, rows // tr),
        in_specs=[pl.BlockSpec((None, tr, cols), lambda s, i, c_ref: (2 * s + c_ref[0], i, 0)), spec],
        out_specs=spec)
    return _pcall(body, name=name, grid_spec=grid_spec, out_shape=jax.ShapeDtypeStruct((4, rows, cols), BF16),
                  compiler_params=_cparams(("parallel", "parallel")))(
        lax.axis_index("c").astype(jnp.int32).reshape(1), g8, other)


def reduce_scatter_begin(grads, names, tag):
    g8 = [g.reshape(8, g.shape[1] // 2, g.shape[2]) for g in grads]
    return _chip_sums_start(g8, push(tag + "_sibling", g8, 4, plan_sibling_quarters), names, tag)


def _chip_sums_start(g8, from_sibling, names, tag):
    chip_sum = [add_kept(a, b, tag + "_add_" + n) for a, b, n in zip(g8, from_sibling, names)]
    return push_start(tag + "_start", chip_sum, chip_sum, plan_chips_by_shard, None)


def sibling_exchange_begin(grads, tag):
    g8 = [g.reshape(8, g.shape[1] // 2, g.shape[2]) for g in grads]
    lands = [lax.empty((4,) + g.shape[1:], g.dtype) for g in g8]
    return push_start(tag + "_sib_start", g8, lands, plan_sibling_quarters, None)


def reduce_scatter_continue(state, names, tag, after):
    n = len(names)
    arrays = push_wait(tag + "_sib_wait", state, plan_sibling_quarters, after, both=True)
    return _chip_sums_start(arrays[:n], arrays[n:], names, tag)


def reduce_scatter_end(state, names, tag, after):
    by_chip = push_wait(tag + "_wait", state, plan_chips_by_shard, after)
    both = swap_halves(tag + "_halves", [sum_slots(q, tag + "_sum_" + n, for_swap=True)
                                         for q, n in zip(by_chip, names)])
    return [b.reshape(2 * b.shape[1], b.shape[2]) for b in both]


def _landing(mine):
    return [jnp.broadcast_to(m[None], (4,) + m.shape).reshape((8,) + m.shape[1:]) for m in mine]


def gather_begin(tag, mine, after, lands=None):
    return push_start(tag + "_start", mine, _landing(mine) if lands is None else lands, plan_chips_gather, after)


def gather_forward(tag, state, after):
    lands = push_wait(tag + "_wait", state, plan_chips_gather, after)
    return push_start(tag + "_fwd_start", lands, None, plan_forward, None)


def gather_end(tag, state, after):
    bufs = push_wait(tag + "_fwd_wait", state, plan_forward, after, in_place=True)
    return [g.reshape(4, 2 * g.shape[1], g.shape[2]) for g in bufs]


PACK_QUANTUM = 8 * C


def _pack(parts, dtype):
    pieces = []
    for p in parts:
        flat = p.astype(dtype).reshape(-1)
        pieces.append(jnp.pad(flat, (0, -flat.shape[0] % PACK_QUANTUM)).reshape(-1, C))
    rows = sum(p.shape[0] for p in pieces)
    if rows % 16:
        pieces.append(jnp.zeros((8, C), dtype))
    return jnp.concatenate(pieces, axis=0)


def _unpack(packed, shapes):
    out, at = [], 0
    for s in shapes:
        n = int(np.prod(s))
        rows = -(-n // PACK_QUANTUM) * 8
        out.append(packed[at:at + rows].reshape(-1)[:n].reshape(s))
        at += rows
    return out


def all_reduce_begin(vec):
    rows, width = vec.shape
    (pair,) = swap_halves("ar_sibling", [jnp.broadcast_to(vec[None], (2, rows, width))])
    chip_sum = sum_slots(pair, "ar_sum_sibling").reshape(2, rows // 2, width)
    mine = lax.dynamic_index_in_dim(chip_sum, lax.axis_index("c"), axis=0, keepdims=True)
    return push_start("ar_start", [chip_sum], [jnp.broadcast_to(mine, (4, rows // 2, width))], plan_chips_half,
                      None)


def all_reduce_end(state, after):
    (by_chip,) = push_wait("ar_wait", state, plan_chips_half, after)
    (both,) = swap_halves("ar_halves", [sum_slots(by_chip, "ar_sum_chips", for_swap=True)])
    return both.reshape(2 * both.shape[1], both.shape[2])


def kernel(x, norm1_g, w_in, shift_mu, w0, w2, a0, a2, g2, k_k, k_a, r_k, gn_w, gn_b, sgu_ln_g, sgu_ln_b, sgu_w, sgu_b, sgu_out_g, w_out, norm2_g, w_gate, w_up, w_down, final_g, loss_target, m_norm1_g, m_w_in, m_shift_mu, m_w0, m_w2, m_a0, m_a2, m_g2, m_k_k, m_k_a, m_r_k, m_gn_w, m_gn_b, m_sgu_ln_g, m_sgu_ln_b, m_sgu_w, m_sgu_b, m_sgu_out_g, m_w_out, m_norm2_g, m_w_gate, m_w_up, m_w_down, m_final_g, v_norm1_g, v_w_in, v_shift_mu, v_w0, v_w2, v_a0, v_a2, v_g2, v_k_k, v_k_a, v_r_k, v_gn_w, v_gn_b, v_sgu_ln_g, v_sgu_ln_b, v_sgu_w, v_sgu_b, v_sgu_out_g, v_w_out, v_norm2_g, v_w_gate, v_w_up, v_w_down, v_final_g):
    args = dict(locals())
    names = ["norm1_g", "w_in", "shift_mu", "w0", "w2", "a0", "a2", "g2", "k_k", "k_a", "r_k", "gn_w", "gn_b",
             "sgu_ln_g", "sgu_ln_b", "sgu_w", "sgu_b", "sgu_out_g", "w_out", "norm2_g", "w_gate", "w_up", "w_down",
             "final_g"]
    big = ["w_in", "w_out", "w_gate", "w_up", "w_down"]
    xi, yi = lax.axis_index("x"), lax.axis_index("y")
    chip = 2 * xi + yi
    x2d, tgt = x[0], loss_target[0]
    t = x2d.shape[0]

    lora_names = ["w2", "a2", "g2"]
    first, later = ["w_in"] + lora_names, ["w_out", "w_gate", "w_up", "w_down"]

    def halves(n, token=None):
        m = args[n][0] if token is None else args[n][0] + token[0, 0]
        m = m.astype(BF16)
        return m.reshape(2, m.shape[0] // 2, m.shape[1])

    first_state, first_token = gather_begin("g0", [halves(n) for n in first], norm1_g)
    later_mine = [halves(n, first_token) for n in later]
    later_lands = _landing(later_mine)
    (h1,) = row_fwd("norm1", rms_fn, [Tiled(x2d)], [norm1_g + first_token[0, 0]], [], [(D, BF16)], _tile(t, 256))
    prepared = functools.reduce(lambda p, q: p + q, [a[0, 0, :1].astype(F32) for a in later_lands])
    first_state, first_token = gather_forward("g0", first_state, prepared + h1[0, :1].astype(F32))
    got = gather_end("g0", first_state, first_token)
    later_state, later_token = gather_begin("g1", later_mine, got[0], later_lands)
    full = {n: blk.transpose(1, 0, 2).reshape(blk.shape[1], 4 * blk.shape[2]) for n, blk in zip(first[1:], got[1:])}
    s0, s1, s2, s3 = got[0]
    w_in_p = jnp.concatenate([s0, s1, s2[:, :CUT_A], s2[:, CUT_B:], s3, s2[:, CUT_A:CUT_B],
                              jnp.zeros((D, P_W - D_IN), BF16)], axis=1)
    mu_rkv = shift_mu[:, :3 * C] + later_token[0, 0]
    mu_lora = jnp.pad(shift_mu[:, 3 * C:], ((0, 0), (0, LORA_W - 288)))
    w2p = jnp.pad(full["w2"], ((0, 64), (0, 0)))
    a2p = jnp.pad(full["a2"], ((64, 0), (0, 0)))
    g2p = jnp.pad(full["g2"], ((0, 96), (0, 0)))
    ii = np.arange(C)
    jm = _scan_consts()[2]
    e_np = (np.arange(LANES)[:, None] == (ii[None, :] // HEAD))
    e_mat, e_t = jnp.asarray(e_np, BF16), jnp.asarray(e_np.T, BF16)
    b_t = jnp.pad(sgu_b[0].T, ((0, 0), (0, LANES - HEADS)))
    r_k_flat = r_k.reshape(1, C)
    tm_row = _tile(t, 256)
    tm_sgu_b = _tile(t, 128)

    p = mm(h1, w_in_p, name="proj_in")
    ps_rkv = shift_fwd(p, mu_rkv, 0, "shift_rkv")
    ps_lora = shift_fwd(p, mu_lora, P_LORA // LORA_W, "shift_lora")
    pre_par = [w0, w2p, a0, a2p, g2p, k_k, k_a]
    r_, w_, k2, v_, an, b_, g_ = row_fwd("pre", pre_fn, [Tiled(ps_rkv), Tiled(ps_lora)], pre_par, [jm],
                                         [(C, F32)] * 7, tm_row)
    v3 = to_cols(v_)
    y3, ckpt = wkv_fwd(r_, w_, k2, v3, an, b_)
    later_state, later_token = gather_forward("g1", later_state, y3)
    y_ = from_cols(y3 + later_token[0, 0])
    post_in = [Tiled(z) for z in (y_, r_, k2, v_, g_)]
    post_par = [gn_w, gn_b, r_k_flat]
    (y_rwkv,) = row_fwd("post", post_fn, post_in, post_par, [jm], [(C, BF16)], tm_row)
    sgu_in = [Tiled(p, C, 3), Tiled(p, C, 4)]
    sgu_par = [sgu_ln_g, sgu_ln_b, sgu_w[0], b_t, sgu_out_g]
    (y_sgu,) = row_fwd("sgu", sgu_fn, sgu_in, sgu_par, [e_mat, e_t], [(C, BF16)], tm_row)
    y_cat = jnp.concatenate([y_rwkv, y_sgu], axis=1)
    for n, blk in zip(later, gather_end("g1", later_state, y_cat)):
        full[n] = blk if n in ("w_gate", "w_up") else blk.reshape(4 * blk.shape[1], blk.shape[2])
    x_mid = mm(y_cat, full["w_out"], res=x2d, name="proj_out")
    (h2,) = row_fwd("norm2", rms_fn, [Tiled(x_mid)], [norm2_g], [], [(D, BF16)], tm_row)
    gate, up, act = ffn_in(h2, full["w_gate"], full["w_up"], lambda g_tile, u_tile: swiglu_fn(g_tile, u_tile)[0])
    x_out = mm(act, full["w_down"], res=x_mid, name="ffn_down")
    loss_part, dx_out, d_final_g, dx_out_b = loss_head(x_out, tgt, final_g.reshape(1, D))

    grads = {"final_g": d_final_g.reshape(D)}
    d_gate, d_up = mm(dx_out_b, full["w_down"], tb=True, name="d_act",
                      post=(lambda d_act, g_tile, u_tile: jax.vjp(swiglu_fn, g_tile, u_tile)[1]((d_act,)), [gate, up],
                            [BF16, BF16]))
    grads["w_down"] = mm(act, dx_out_b, ta=True, out_dtype=BF16, name="d_w_down")
    grads["w_gate"] = mm(h2, d_gate, shards="out", out_dtype=BF16, name="d_w_gate")
    grads["w_up"] = mm(h2, d_up, shards="out", out_dtype=BF16, name="d_w_up")
    d_h2 = mm(d_gate, full["w_gate"], shards="bt", name="d_h2_gate")
    d_h2 = mm(d_up, full["w_up"], shards="bt", res=d_h2, name="d_h2_up")
    dx_mid, grads["norm2_g"], dx_mid_b = row_bwd("norm2_b", rms_res_fn, [Tiled(x_mid)], [norm2_g], [],
                                                 [Tiled(d_h2), Tiled(dx_out)], tm_row, bf16_copy_of=(0,))
    d_ycat = mm(dx_mid_b, full["w_out"], tb=True, name="d_ycat")
    grads["w_out"] = mm(y_cat, dx_mid_b, ta=True, out_dtype=BF16, name="d_w_out")
    rs_state, rs_token = sibling_exchange_begin([grads[n].reshape((4,) + args[n].shape[1:]) for n in later], "rs2")
    d_pu, d_pv, grads["sgu_ln_g"], grads["sgu_ln_b"], d_sgu_w, d_b_t, grads["sgu_out_g"] = row_bwd(
        "sgu_b", sgu_fn, sgu_in, sgu_par[:-1] + [sgu_out_g + rs_token[0, 0]], [e_mat, e_t], [Tiled(d_ycat, C, 1)],
        tm_sgu_b, tiled_out_dtypes=[BF16, BF16])
    grads["sgu_w"] = d_sgu_w[None]
    grads["sgu_b"] = d_b_t[:, :HEADS].T[None]
    post_par_b = [gn_w + rs_token[0, 0]] + post_par[1:]
    d_y, d_r1, d_k1, d_v1, d_g, grads["gn_w"], grads["gn_b"], d_r_k = row_bwd(
        "post_b", post_fn, post_in, post_par_b, [jm], [Tiled(d_ycat, C, 0)], tm_row)
    grads["r_k"] = d_r_k.reshape(r_k.shape)
    rs_state, rs_token = reduce_scatter_continue(rs_state, later, "rs2", d_y[0, :1] + d_pu[0, :1].astype(F32))
    d_r2, d_w, d_k2, d_a, d_b, d_v3 = wkv_bwd(r_, w_, k2, v3, an, b_, ckpt, to_cols(d_y + rs_token[0, 0]))
    d_v2 = from_cols(d_v3)
    pre_cts = [(Tiled(d_r1), Tiled(d_r2)), Tiled(d_w), (Tiled(d_k1), Tiled(d_k2)), (Tiled(d_v1), Tiled(d_v2)),
               Tiled(d_a), Tiled(d_b), Tiled(d_g)]
    d_ps_rkv, d_ps_lora, grads["w0"], d_w2p, grads["a0"], d_a2p, d_g2p, grads["k_k"], grads["k_a"] = row_bwd(
        "pre_b", pre_fn, [Tiled(ps_rkv), Tiled(ps_lora)], pre_par, [jm], pre_cts, tm_row)
    grads["w2"], grads["a2"], grads["g2"] = d_w2p[None, :64], d_a2p[None, 64:], d_g2p[None, :160]
    d_p_rkv, d_mu_rkv = shift_bwd(d_ps_rkv, p, mu_rkv, 0, "shift_rkv_b")
    d_p_lora, d_mu_lora = shift_bwd(d_ps_lora, p, mu_lora, P_LORA // LORA_W, "shift_lora_b")
    grads["shift_mu"] = jnp.concatenate([d_mu_rkv, d_mu_lora[:, :288]], axis=1)
    d_p = jnp.concatenate([d_p_rkv, d_pu, d_pv, d_p_lora], axis=1)
    d_h1 = mm(d_p, w_in_p, tb=True, name="d_h1")
    d_w_in_p = mm(h1, d_p, ta=True, out_dtype=BF16, name="d_w_in")
    ws = D_IN // 4
    shard2 = jnp.concatenate([d_w_in_p[:, 2 * ws:3 * C], d_w_in_p[:, P_LORA:P_LORA + 288],
                              d_w_in_p[:, 3 * C:3 * C + ws - CUT_B]], axis=1)
    grads["w_in"] = jnp.stack([d_w_in_p[:, :ws], d_w_in_p[:, ws:2 * ws], shard2,
                               d_w_in_p[:, 3 * C + ws - CUT_B:P_LORA]])
    dx, grads["norm1_g"] = row_bwd("norm1_b", rms_res_fn, [Tiled(x2d)], [norm1_g], [],
                                   [Tiled(d_h1), Tiled(dx_mid)], tm_row)

    small = [n for n in names if n not in big]
    small_shapes = [(1, 64, C) if n in ("w2", "a2") else (1, 160, C) if n == "g2" else args[n].shape for n in small]
    vec = _pack([grads[n] for n in small] + [loss_part[:, :1]], F32)
    w_in_state, w_in_token = reduce_scatter_begin([grads["w_in"]], ["w_in"], "rs1")
    small_state, small_token = all_reduce_begin(vec)
    total = {n: g[None] for n, g in zip(later, reduce_scatter_end(rs_state, later, "rs2", w_in_token + small_token))}
    delta, new_m, new_v = {}, {}, {}

    def adam_big(n):
        s = args[n].shape
        view = (lambda a: a[0].T) if n == "w_in" else (lambda a: a[0])
        back = (lambda a: a.T.reshape(s)) if n == "w_in" else (lambda a: a.reshape(s))
        d_, m_, v__ = adamw(view(args[n]), view(total[n]), view(args["m_" + n]), view(args["v_" + n]), "adamw_" + n)
        delta[n], new_m[n], new_v[n] = back(d_), back(m_), back(v__)

    for n in later:
        adam_big(n)
    total["w_in"] = reduce_scatter_end(w_in_state, ["w_in"], "rs1", new_v["w_down"])[0][None]
    *small_sums, loss = _unpack(all_reduce_end(small_state, new_v["w_down"]), small_shapes + [()])
    for n, g_sum in zip(small, small_sums):
        total[n] = lax.dynamic_slice_in_dim(g_sum, chip * 256, 256, axis=2) if n in lora_names else g_sum
    adam_big("w_in")
    flat = [_pack([src[n] for n in small], F32)
            for src in (args, total, {n: args["m_" + n] for n in small}, {n: args["v_" + n] for n in small})]
    outs = adamw(*flat, "adamw_small")
    for res, o in zip((delta, new_m, new_v), outs):
        res.update(zip(small, _unpack(o, [args[n].shape for n in small])))
    return (loss, dx[None], *[total[n] for n in names], *[delta[n] for n in names],
            *[new_m[n] for n in names], *[new_v[n] for n in names])
```

```python
import functools

import numpy as np
import jax
import jax.numpy as jnp
from jax import lax
from jax.experimental import pallas as pl
from jax.experimental.pallas import tpu as pltpu

F32 = jnp.float32
BF16 = jnp.bfloat16

D = 2048
C = 1024
HEADS = 16
HEAD = 64
LANES = 128
P_W = 5632
P_LORA = 5120
LORA_W = 512
D_IN = 5408
CUT_A = 3 * C - 2 * (D_IN // 4)
CUT_B = CUT_A + 288
CHUNK = 128
RMS_EPS = 1e-6
LN_EPS = 1e-5
GN_EPS = 64e-5
L2_EPS = 1e-12
VMEM_LIMIT = 56 * 1024 * 1024


def _pcall(body, **kw):
    return pl.pallas_call(body, **kw)


def _cparams(sem):
    return pltpu.CompilerParams(dimension_semantics=sem, vmem_limit_bytes=VMEM_LIMIT)


def _tile(n, most):
    t = most
    while t > 8 and n % t:
        t //= 2
    assert n % t == 0, (n, most)
    return t


MM_TILE = (2048, 512, 2048)
MM_VMEM = 40 * 1024 * 1024


def _div_tile(n, most, quantum=LANES):
    for t in range(min(n, most) // quantum * quantum, 0, -quantum):
        if n % t == 0:
            return t
    raise ValueError((n, most, quantum))


def _mm_tiles(m, n, k, a_bytes, b_bytes, r_bytes, o_bytes):
    tm, tn, tk = _div_tile(m, MM_TILE[0]), _div_tile(n, MM_TILE[1]), _div_tile(k, MM_TILE[2])

    def need(tm, tk):
        return 2 * (tm * tk * a_bytes + tk * tn * b_bytes + tm * tn * (r_bytes + o_bytes)) + tm * tn * 4

    while need(tm, tk) > MM_VMEM:
        if tk >= tm and tk > 512:
            tk = _div_tile(k, tk - LANES)
        else:
            tm = _div_tile(m, tm - LANES)
    return tm, tn, tk


def mm(a, b, *, ta=False, tb=False, res=None, out_dtype=F32, name, shards=None, post=None):
    if shards is not None:
        return _mm_shards(a, b, res, out_dtype, name, shards, post)
    m, k = (a.shape[1], a.shape[0]) if ta else a.shape
    n = b.shape[0] if tb else b.shape[1]
    assert (b.shape[1] if tb else b.shape[0]) == k
    if post is None:
        tm, tn, tk = _mm_tiles(m, n, k, a.dtype.itemsize, b.dtype.itemsize,
                               0 if res is None else res.dtype.itemsize, jnp.dtype(out_dtype).itemsize)
    else:
        tm, tn, tk = _div_tile(m, 1024), _div_tile(n, MM_TILE[1]), _div_tile(k, MM_TILE[2])
    nk = k // tk
    dims = (((0 if ta else 1,), (1 if tb else 0,)), ((), ()))
    a_spec = pl.BlockSpec((tk, tm), lambda i, j, l: (l, i)) if ta else pl.BlockSpec((tm, tk), lambda i, j, l: (i, l))
    b_spec = pl.BlockSpec((tn, tk), lambda i, j, l: (j, l)) if tb else pl.BlockSpec((tk, tn), lambda i, j, l: (l, j))
    o_spec = pl.BlockSpec((tm, tn), lambda i, j, l: (i, j))
    return _mm_call(a, b, res, dims, (m // tm, n // tn, nk), a_spec, b_spec, o_spec, o_spec, (tm, tn),
                    jax.ShapeDtypeStruct((m, n), out_dtype), name, post)


def _mm_shards(a, b, res, out_dtype, name, shards, post=None):
    assert post is None or shards == "b"
    if shards == "b":
        (m, k), ns = a.shape, b.shape[2]
        tm, tk = _div_tile(m, 1024 if post is None else 512), _div_tile(k, 2048)
        grid, dims, acc = (m // tm, 4, k // tk), (((1,), (0,)), ((), ())), (tm, ns)
        a_spec = pl.BlockSpec((tm, tk), lambda i, s, l: (i, l))
        b_spec = pl.BlockSpec((None, tk, ns), lambda i, s, l: (s, l, 0))
        o_spec = pl.BlockSpec((tm, ns), lambda i, s, l: (i, s))
        out = jax.ShapeDtypeStruct((m, 4 * ns), out_dtype)
    elif shards == "bt":
        m, (_, n, ns) = a.shape[0], b.shape
        tm, tn = _div_tile(m, 2048), _div_tile(n, 512)
        grid, dims, acc = (m // tm, n // tn, 4), (((1,), (1,)), ((), ())), (tm, tn)
        a_spec = pl.BlockSpec((tm, ns), lambda i, j, s: (i, s))
        b_spec = pl.BlockSpec((None, tn, ns), lambda i, j, s: (s, j, 0))
        o_spec = pl.BlockSpec((tm, tn), lambda i, j, s: (i, j))
        out = jax.ShapeDtypeStruct((m, n), out_dtype)
    else:
        (t, m), ns = a.shape, b.shape[1] // 4
        tm, tk = _div_tile(m, 1024), _div_tile(t, 2048)
        grid, dims, acc = (m // tm, 4, t // tk), (((0,), (0,)), ((), ())), (tm, ns)
        a_spec = pl.BlockSpec((tk, tm), lambda i, s, l: (l, i))
        b_spec = pl.BlockSpec((tk, ns), lambda i, s, l: (l, s))
        o_spec = pl.BlockSpec((None, tm, ns), lambda i, s, l: (s, i, 0))
        out = jax.ShapeDtypeStruct((4, m, ns), out_dtype)
    return _mm_call(a, b, res, dims, grid, a_spec, b_spec, o_spec, o_spec, acc, out, name, post)


def ffn_in(h, w_gate, w_up, tile_fn):
    (m, k), ns = h.shape, w_gate.shape[2]
    tm = _div_tile(m, 256)

    def body(h_ref, g_ref, u_ref, gate_ref, up_ref, act_ref):
        hv = h_ref[...]
        gate = jnp.dot(hv, g_ref[...], preferred_element_type=F32)
        up = jnp.dot(hv, u_ref[...], preferred_element_type=F32)
        gate_ref[...] = gate
        up_ref[...] = up
        act_ref[...] = tile_fn(gate, up).astype(BF16)

    w_spec = pl.BlockSpec((None, k, ns), lambda s, i: (s, 0, 0))
    o_spec = pl.BlockSpec((tm, ns), lambda s, i: (i, s))
    return _pcall(
        body, name="ffn_in", grid=(4, m // tm),
        in_specs=[pl.BlockSpec((tm, k), lambda s, i: (i, 0)), w_spec, w_spec], out_specs=[o_spec] * 3,
        out_shape=[jax.ShapeDtypeStruct((m, 4 * ns), dt) for dt in (F32, F32, BF16)],
        compiler_params=_cparams(("parallel", "parallel")),
    )(h, w_gate, w_up)


def _mm_call(a, b, res, dims, grid, a_spec, b_spec, r_spec, o_spec, acc_shape, out, name, post=None):
    nk = grid[2]
    post_fn, post_in, post_dtypes = (None, [], [out.dtype]) if post is None else post
    n_extra = (res is not None) + len(post_in)

    def body(*refs):
        a_ref, b_ref = refs[:2]
        r_ref = None if res is None else refs[2]
        p_refs = refs[2 + (res is not None):2 + n_extra]
        o_refs = refs[2 + n_extra:2 + n_extra + len(post_dtypes)]

        def finish(acc):
            if r_ref is not None:
                acc = acc + r_ref[...].astype(F32)
            outs = (acc,) if post_fn is None else post_fn(acc, *[p[...] for p in p_refs])
            for o_ref, o in zip(o_refs, outs):
                o_ref[...] = o.astype(o_ref.dtype)

        prod = lax.dot_general(a_ref[...].astype(BF16), b_ref[...].astype(BF16), dims, preferred_element_type=F32)
        if nk == 1:
            finish(prod)
            return
        acc_ref = refs[-1]
        kk = pl.program_id(2)

        @pl.when(kk == 0)
        def _():
            acc_ref[...] = prod

        @pl.when(kk > 0)
        def _():
            acc_ref[...] += prod

        @pl.when(kk == nk - 1)
        def _():
            finish(acc_ref[...])

    in_specs = [a_spec, b_spec] + ([r_spec] if res is not None else []) + [o_spec] * len(post_in)
    args = (a, b) + ((res,) if res is not None else ()) + tuple(post_in)
    outs = _pcall(
        body, name=name, grid=grid, in_specs=in_specs, out_specs=[o_spec] * len(post_dtypes),
        out_shape=[jax.ShapeDtypeStruct(out.shape, dt) for dt in post_dtypes],
        scratch_shapes=[] if nk == 1 else [pltpu.VMEM(acc_shape, F32)],
        compiler_params=_cparams(("parallel", "parallel", "arbitrary")),
    )(*args)
    return outs[0] if post is None else outs


class Tiled:
    def __init__(self, arr, width=None, col=0):
        self.arr, self.width, self.col = arr, (arr.shape[1] if width is None else width), col

    def spec(self, tm):
        col = self.col
        return pl.BlockSpec((tm, self.width), lambda i: (i, col))


def _full_spec(p):
    nd = p.ndim
    return pl.BlockSpec(p.shape, lambda i: (0,) * nd)


def row_fwd(name, fn, tiled, params, consts, outs, tm):
    t = tiled[0].arr.shape[0]
    n_in = len(tiled) + len(params) + len(consts)

    def body(*refs):
        res = fn(*[r[...] for r in refs[:n_in]])
        for o_ref, r in zip(refs[n_in:], res):
            o_ref[...] = r.astype(o_ref.dtype)

    return _pcall(
        body, name=name, grid=(t // tm,),
        in_specs=[x.spec(tm) for x in tiled] + [_full_spec(p) for p in params + consts],
        out_specs=[pl.BlockSpec((tm, w), lambda i: (i, 0)) for w, _ in outs],
        out_shape=[jax.ShapeDtypeStruct((t, w), dt) for w, dt in outs],
        compiler_params=_cparams(("parallel",)),
    )(*[x.arr for x in tiled], *params, *consts)


def row_bwd(name, fn, tiled, params, consts, cts, tm, tiled_out_dtypes=None, bf16_copy_of=()):
    t = tiled[0].arr.shape[0]
    nt, npar, ncon = len(tiled), len(params), len(consts)
    cts = [c if isinstance(c, tuple) else (c,) for c in cts]
    flat_cts = [x for c in cts for x in c]
    tiled_out_dtypes = tiled_out_dtypes or [F32] * nt
    copies = [tiled[j] for j in bf16_copy_of]

    def body(*refs):
        n_in = nt + npar + ncon
        ins = [r[...].astype(F32) for r in refs[:nt + npar]]
        con = [r[...] for r in refs[nt + npar:n_in]]
        ct_refs = list(refs[n_in:n_in + len(flat_cts)])
        o = refs[n_in + len(flat_cts):]
        ct = []
        for c in cts:
            parts = [ct_refs.pop(0)[...].astype(F32) for _ in c]
            ct.append(functools.reduce(lambda p, q: p + q, parts))
        _, vjp = jax.vjp(lambda *a: fn(*a, *con), *ins)
        g = vjp(tuple(ct))
        for j in range(nt):
            o[j][...] = g[j].astype(o[j].dtype)
        for n, j in enumerate(bf16_copy_of):
            o[nt + npar + n][...] = g[j].astype(BF16)
        first = pl.program_id(0) == 0

        @pl.when(first)
        def _():
            for j in range(npar):
                o[nt + j][...] = g[nt + j]

        @pl.when(jnp.logical_not(first))
        def _():
            for j in range(npar):
                o[nt + j][...] += g[nt + j]

    return _pcall(
        body, name=name, grid=(t // tm,),
        in_specs=[x.spec(tm) for x in tiled] + [_full_spec(p) for p in params + consts]
        + [x.spec(tm) for x in flat_cts],
        out_specs=[pl.BlockSpec((tm, x.width), lambda i: (i, 0)) for x in tiled] + [_full_spec(p) for p in params]
        + [pl.BlockSpec((tm, x.width), lambda i: (i, 0)) for x in copies],
        out_shape=[jax.ShapeDtypeStruct((t, x.width), dt) for x, dt in zip(tiled, tiled_out_dtypes)]
        + [jax.ShapeDtypeStruct(p.shape, F32) for p in params]
        + [jax.ShapeDtypeStruct((t, x.width), BF16) for x in copies],
        compiler_params=_cparams(("arbitrary",)),
    )(*[x.arr for x in tiled], *params, *consts, *[x.arr for x in flat_cts])


def _split_dot(x, w):
    hi = x.astype(BF16)
    lo = (x - hi.astype(F32)).astype(BF16)
    return jnp.dot(hi, w, preferred_element_type=F32) + jnp.dot(lo, w, preferred_element_type=F32)


def _headsum_blocks(x, j2k):
    out = []
    for c0 in range(0, x.shape[1], LANES):
        blk = x[:, c0:c0 + LANES]
        hi = blk.astype(BF16)
        lo = (blk - hi.astype(F32)).astype(BF16)
        out.append(jnp.dot(jnp.concatenate([hi, lo], axis=1), j2k, preferred_element_type=F32))
    return jnp.concatenate(out, axis=1)


@jax.custom_vjp
def headsum(x, j2k):
    return _headsum_blocks(x, j2k)


def _headsum_fwd(x, j2k):
    return _headsum_blocks(x, j2k), j2k


def _headsum_bwd(j2k, ct):
    return _headsum_blocks(ct, j2k), jnp.zeros_like(j2k)


headsum.defvjp(_headsum_fwd, _headsum_bwd)


def _bdot(x, w):
    return jnp.dot(x.astype(BF16), w.astype(BF16), preferred_element_type=F32)


def _sigmoid(x):
    return jax.nn.sigmoid(x)


def _softplus(x):
    return jnp.maximum(x, 0.0) + jnp.log(1.0 + jnp.exp(-jnp.abs(x)))


def rms_fn(x, g):
    return (x * lax.rsqrt(jnp.mean(x * x, axis=-1, keepdims=True) + RMS_EPS) * g,)


def rms_res_fn(x, g):
    return rms_fn(x, g)[0], x


def pre_fn(rkv, lora, w0, w2p, a0, a2p, g2p, k_k, k_a, jm):
    r, k, v = rkv[:, :C], rkv[:, C:2 * C], rkv[:, 2 * C:]
    zwa, zg = lora[:, :LANES], lora[:, LANES:LANES + 256]
    w_log = -_softplus(-(w0 + _bdot(jnp.tanh(zwa), w2p))) - 0.5
    decay = jnp.exp(-jnp.exp(w_log))
    a = _sigmoid(a0 + _bdot(zwa, a2p))
    g = _bdot(_sigmoid(zg), g2p)
    kk = k * k_k
    kk = kk / jnp.maximum(jnp.sqrt(headsum(kk * kk, jm)), L2_EPS)
    k2 = k * (1.0 + (a - 1.0) * k_a)
    return r, decay, k2, v, -kk, kk * a, g


def post_fn(y, r, k2, v, g, gn_w, gn_b, r_k, jm):
    mu = headsum(y, jm) * (1.0 / HEAD)
    yc = y - mu
    var = headsum(yc * yc, jm) * (1.0 / HEAD)
    yn = yc * lax.rsqrt(var + GN_EPS) * gn_w + gn_b
    bonus = headsum(r * k2 * r_k, jm) * v
    return ((yn + bonus) * g,)


def _gelu(x):
    return 0.5 * x * (1.0 + lax.erf(x * np.float32(1.0 / np.sqrt(2.0))))


@jax.custom_vjp
def expand_groups(b_t, e, e_t):
    return _split_dot(b_t, e)


def _expand_groups_fwd(b_t, e, e_t):
    return _split_dot(b_t, e), (e, e_t)


def _expand_groups_bwd(res, ct):
    e, e_t = res
    return _split_dot(ct, e_t), jnp.zeros_like(e), jnp.zeros_like(e_t)


expand_groups.defvjp(_expand_groups_fwd, _expand_groups_bwd)


def sgu_fn(pu, pv, ln_g, ln_b, w_s, b_t, out_g, e, e_t):
    rows = pu.shape[0]
    b_exp = expand_groups(b_t, e, e_t)
    u, v = _gelu(pu), _gelu(pv)
    mu = jnp.mean(v, axis=-1, keepdims=True)
    vc = v - mu
    var = jnp.mean(vc * vc, axis=-1, keepdims=True)
    v = vc * lax.rsqrt(var + LN_EPS) * ln_g + ln_b
    tri = lax.broadcasted_iota(jnp.int32, (CHUNK, CHUNK), 0) >= lax.broadcasted_iota(jnp.int32, (CHUNK, CHUNK), 1)
    left = lax.broadcasted_iota(jnp.int32, (CHUNK, LANES), 1) < HEAD
    chunks = []
    for c0 in range(0, rows, CHUNK):
        cols = []
        for gp in range(HEADS // 2):
            vp = v[c0:c0 + CHUNK, gp * LANES:(gp + 1) * LANES]
            wa = jnp.where(tri, w_s[2 * gp], 0.0)
            wb = jnp.where(tri, w_s[2 * gp + 1], 0.0)
            cols.append(jnp.where(left, _bdot(wa, vp), _bdot(wb, vp)))
        chunks.append(jnp.concatenate(cols, axis=1) + b_exp)
    s = jnp.concatenate(chunks, axis=0) if len(chunks) > 1 else chunks[0]
    y = u * s
    return (y * lax.rsqrt(jnp.mean(y * y, axis=-1, keepdims=True) + RMS_EPS) * out_g,)


def swiglu_fn(gate, up):
    return (gate * _sigmoid(gate) * up,)


SCAN_CHUNK = 64
NB = C // LANES
FWD_GROUP = 2
BWD_GROUP = 8


def to_cols(a):
    t = a.shape[0]
    return a.reshape(t // 8, LANES, HEAD).transpose(0, 2, 1)


def from_cols(a):
    return a.transpose(0, 2, 1).reshape(a.shape[0] * 8, C)


def _scan_consts():
    i = np.arange(LANES)
    spread = (i[:, None] % 2) == (i[None, :] // HEAD)
    pick = (i[:, None] // HEAD) == (i[None, :] % 2)
    j2 = (i[:, None] // HEAD) == (i[None, :] // HEAD)
    sel = (i[None, :] // 2) == (np.arange(8 * NB * HEAD)[:, None] // HEAD)
    return (jnp.asarray(spread, BF16), jnp.asarray(pick, BF16), jnp.asarray(np.concatenate([j2, j2], 0), BF16),
            jnp.asarray(sel, BF16))


def _stack(blocks):
    return jnp.concatenate(blocks, axis=0) if len(blocks) > 1 else blocks[0]


def _unstack(x, n):
    return [x[i * HEAD:(i + 1) * HEAD] for i in range(n)]


def _headsums(blocks, j2k, group):
    res = []
    for g0 in range(0, len(blocks), group):
        x = _stack(blocks[g0:g0 + group])
        hi = x.astype(BF16)
        lo = (x - hi.astype(F32)).astype(BF16)
        out = jnp.dot(jnp.concatenate([hi, lo], axis=1), j2k, preferred_element_type=F32)
        res += _unstack(out, len(blocks[g0:g0 + group]))
    return res


def _headsums_out(blocks, pick):
    return _unstack(jnp.dot(_stack(blocks).astype(BF16), pick, preferred_element_type=F32), len(blocks))


def _expand8(tile, sel, spread):
    lhs = jnp.tile(tile.astype(BF16), (8 * NB, 1)) * sel
    return _unstack(jnp.dot(lhs, spread, preferred_element_type=F32), 8 * NB)


def _collapse(tile, blocks, first, lane_pair):
    for n, blk in enumerate(blocks):
        tile = jnp.where(jnp.tile(lane_pair == first + n, (HEAD // 8, 1)), blk, tile)
    return tile


def _row(tile, j, cb):
    return jnp.broadcast_to(tile[j:j + 1, cb * LANES:(cb + 1) * LANES], (HEAD, LANES))


def _scan_step(s_ref, rows, vexp, j2k, j, hist=None):
    w_t, k_t, a_t, b_t, r_t = rows
    s = [s_ref[:, cb * LANES:(cb + 1) * LANES] for cb in range(NB)]
    sab = _headsums([s[cb] * _row(a_t, j, cb) for cb in range(NB)], j2k, FWD_GROUP if hist is None else BWD_GROUP)
    out = []
    for cb in range(NB):
        sl = slice(cb * LANES, (cb + 1) * LANES)
        s_new = s[cb] * _row(w_t, j, cb) + sab[cb] * _row(b_t, j, cb) + vexp[j * NB + cb] * _row(k_t, j, cb)
        s_ref[:, sl] = s_new
        if hist is None:
            out.append(s_new * _row(r_t, j, cb))
        else:
            s_hist, sab_hist, idx = hist
            s_hist[idx + 1, :, sl] = s_new
            sab_hist[idx, :, sl] = sab[cb]
    return out


def wkv_fwd(r, w, k, v3, a, b):
    t = r.shape[0]
    nc = t // SCAN_CHUNK
    n8 = SCAN_CHUNK // 8

    def body(r_ref, w_ref, k_ref, a_ref, b_ref, v3_ref, spread_ref, pick_ref, j2k_ref, sel_ref, y3_ref, ck_ref,
             s_ref):
        @pl.when(pl.program_id(0) == 0)
        def _():
            s_ref[...] = jnp.zeros_like(s_ref)

        ck_ref[0] = s_ref[...]
        spread, pick, j2k, sel = spread_ref[...], pick_ref[...], j2k_ref[...], sel_ref[...]
        lane_pair = lax.broadcasted_iota(jnp.int32, (8, LANES), 1) // 2

        def t8_body(t8, carry):
            row0 = pl.multiple_of(t8 * 8, 8)
            rows = [ref[pl.ds(row0, 8), :] for ref in (w_ref, k_ref, a_ref, b_ref, r_ref)]
            vexp = _expand8(v3_ref[t8], sel, spread)
            y3 = jnp.zeros((HEAD, LANES), F32)
            for j in range(8):
                y3 = _collapse(y3, _headsums_out(_scan_step(s_ref, rows, vexp, j2k, j), pick), j * NB, lane_pair)
            y3_ref[t8] = y3
            return carry

        lax.fori_loop(0, n8, t8_body, 0)

    row_spec = pl.BlockSpec((SCAN_CHUNK, C), lambda i: (i, 0))
    col_spec = pl.BlockSpec((n8, HEAD, LANES), lambda i: (i, 0, 0))
    consts = _scan_consts()
    return _pcall(
        body, name="wkv_fwd", grid=(nc,),
        in_specs=[row_spec] * 5 + [col_spec] + [pl.BlockSpec(c.shape, lambda i: (0, 0)) for c in consts],
        out_specs=[col_spec, pl.BlockSpec((1, HEAD, C), lambda i: (i, 0, 0))],
        out_shape=[jax.ShapeDtypeStruct((t // 8, HEAD, LANES), F32), jax.ShapeDtypeStruct((nc, HEAD, C), F32)],
        scratch_shapes=[pltpu.VMEM((HEAD, C), F32)],
        compiler_params=_cparams(("arbitrary",)),
    )(r, w, k, a, b, v3, *consts)


def wkv_bwd(r, w, k, v3, a, b, ckpt, dy3):
    t = r.shape[0]
    nc = t // SCAN_CHUNK
    n8 = SCAN_CHUNK // 8

    def body(r_ref, w_ref, k_ref, a_ref, b_ref, v3_ref, dy3_ref, ck_ref, spread_ref, pick_ref, j2k_ref, sel_ref,
             dr_ref, dw_ref, dk_ref, da_ref, db_ref, dv3_ref, s_ref, g_ref, s_hist, sab_hist):
        @pl.when(pl.program_id(0) == 0)
        def _():
            g_ref[...] = jnp.zeros_like(g_ref)

        spread, pick, j2k, sel = spread_ref[...], pick_ref[...], j2k_ref[...], sel_ref[...]
        lane_pair = lax.broadcasted_iota(jnp.int32, (8, LANES), 1) // 2
        sub = lax.broadcasted_iota(jnp.int32, (8, LANES), 0)
        s_ref[...] = ck_ref[0]
        s_hist[0] = ck_ref[0]

        def redo(t8, carry):
            row0 = pl.multiple_of(t8 * 8, 8)
            rows = [ref[pl.ds(row0, 8), :] for ref in (w_ref, k_ref, a_ref, b_ref, r_ref)]
            vexp = _expand8(v3_ref[t8], sel, spread)
            for j in range(8):
                _scan_step(s_ref, rows, vexp, j2k, j, hist=(s_hist, sab_hist, t8 * 8 + j))
            return carry

        lax.fori_loop(0, n8, redo, 0)

        def back(q, carry):
            t8 = n8 - 1 - q
            row0 = pl.multiple_of(t8 * 8, 8)
            w_t, k_t, a_t, b_t, r_t = [ref[pl.ds(row0, 8), :] for ref in (w_ref, k_ref, a_ref, b_ref, r_ref)]
            vexp = _expand8(v3_ref[t8], sel, spread)
            dyexp = _expand8(dy3_ref[t8], sel, spread)
            dv3 = jnp.zeros((HEAD, LANES), F32)
            tiles = [[jnp.zeros((8, LANES), F32) for _ in range(NB)] for _ in range(5)]
            for j in range(7, -1, -1):
                idx = t8 * 8 + j
                g = [g_ref[:, cb * LANES:(cb + 1) * LANES] + dyexp[j * NB + cb] * _row(r_t, j, cb)
                     for cb in range(NB)]
                dsab = _headsums([g[cb] * _row(b_t, j, cb) for cb in range(NB)], j2k, BWD_GROUP)
                dv3 = _collapse(dv3, _headsums_out([g[cb] * _row(k_t, j, cb) for cb in range(NB)], pick), j * NB,
                                lane_pair)
                for cb in range(NB):
                    sl = slice(cb * LANES, (cb + 1) * LANES)
                    s_new = s_hist[idx + 1, :, sl]
                    s_old = s_hist[idx, :, sl]
                    sab = sab_hist[idx, :, sl]
                    sums = (s_new * dyexp[j * NB + cb], g[cb] * s_old, g[cb] * vexp[j * NB + cb],
                            s_old * dsab[cb], g[cb] * sab)
                    for n, prod in enumerate(sums):
                        rowsum = jnp.broadcast_to(jnp.sum(prod, axis=0, keepdims=True), (8, LANES))
                        tiles[n][cb] = jnp.where(sub == j, rowsum, tiles[n][cb])
                    g_ref[:, sl] = g[cb] * _row(w_t, j, cb) + dsab[cb] * _row(a_t, j, cb)
            dv3_ref[t8] = dv3
            for n, ref in enumerate((dr_ref, dw_ref, dk_ref, da_ref, db_ref)):
                for cb in range(NB):
                    ref[pl.ds(row0, 8), cb * LANES:(cb + 1) * LANES] = tiles[n][cb]
            return carry

        lax.fori_loop(0, n8, back, 0)

    row_spec = pl.BlockSpec((SCAN_CHUNK, C), lambda i: (nc - 1 - i, 0))
    col_spec = pl.BlockSpec((n8, HEAD, LANES), lambda i: (nc - 1 - i, 0, 0))
    consts = _scan_consts()
    return _pcall(
        body, name="wkv_bwd", grid=(nc,),
        in_specs=[row_spec] * 5 + [col_spec, col_spec, pl.BlockSpec((1, HEAD, C), lambda i: (nc - 1 - i, 0, 0))]
        + [pl.BlockSpec(c.shape, lambda i: (0, 0)) for c in consts],
        out_specs=[row_spec] * 5 + [col_spec],
        out_shape=[jax.ShapeDtypeStruct((t, C), F32)] * 5 + [jax.ShapeDtypeStruct((t // 8, HEAD, LANES), F32)],
        scratch_shapes=[pltpu.VMEM((HEAD, C), F32), pltpu.VMEM((HEAD, C), F32),
                        pltpu.VMEM((SCAN_CHUNK + 1, HEAD, C), F32), pltpu.VMEM((SCAN_CHUNK, HEAD, C), F32)],
        compiler_params=_cparams(("arbitrary",)),
    )(r, w, k, a, b, v3, dy3, ckpt, *consts)


def _prev_rows(cur, before, first_tile):
    last = jnp.where(first_tile, 0.0, before[7:8, :])
    row = lax.broadcasted_iota(jnp.int32, cur.shape, 0)
    return jnp.where(row == 0, last, pltpu.roll(cur, 1, 0))


def shift_fwd(p, mu, col, name):
    t, width = p.shape[0], mu.shape[1]
    tm = _tile(t, 256)

    def body(p_ref, before_ref, mu_ref, o_ref):
        cur = p_ref[...]
        prev = _prev_rows(cur, before_ref[...], pl.program_id(0) == 0)
        o_ref[...] = cur + (prev - cur) * mu_ref[...]

    return _pcall(
        body, name=name, grid=(t // tm,),
        in_specs=[pl.BlockSpec((tm, width), lambda i: (i, col)),
                  pl.BlockSpec((8, width), lambda i: (jnp.maximum(i * (tm // 8) - 1, 0), col)),
                  pl.BlockSpec((1, width), lambda i: (0, 0))],
        out_specs=pl.BlockSpec((tm, width), lambda i: (i, 0)),
        out_shape=jax.ShapeDtypeStruct((t, width), F32),
        compiler_params=_cparams(("parallel",)),
    )(p, p, mu)


def shift_bwd(dps, p, mu, col, name):
    t, width = dps.shape
    tm = _tile(t, 256)
    nt = t // tm

    def body(d_ref, after_ref, p_ref, before_ref, mu_ref, dp_ref, dmu_ref):
        i = pl.program_id(0)
        d, cur, mu_v = d_ref[...], p_ref[...], mu_ref[...]
        row = lax.broadcasted_iota(jnp.int32, d.shape, 0)
        first_after = jnp.where(i == nt - 1, 0.0, after_ref[0:1, :])
        nxt = jnp.where(row == tm - 1, first_after, pltpu.roll(d, tm - 1, 0))
        dp_ref[...] = (d * (1.0 - mu_v) + nxt * mu_v).astype(BF16)
        prev = _prev_rows(cur, before_ref[...], i == 0)
        part = jnp.sum(d * (prev - cur), axis=0, keepdims=True)

        @pl.when(i == 0)
        def _():
            dmu_ref[...] = part

        @pl.when(i > 0)
        def _():
            dmu_ref[...] += part

    return _pcall(
        body, name=name, grid=(nt,),
        in_specs=[pl.BlockSpec((tm, width), lambda i: (i, 0)),
                  pl.BlockSpec((8, width), lambda i: (jnp.minimum((i + 1) * (tm // 8), t // 8 - 1), 0)),
                  pl.BlockSpec((tm, width), lambda i: (i, col)),
                  pl.BlockSpec((8, width), lambda i: (jnp.maximum(i * (tm // 8) - 1, 0), col)),
                  pl.BlockSpec((1, width), lambda i: (0, 0))],
        out_specs=[pl.BlockSpec((tm, width), lambda i: (i, 0)), pl.BlockSpec((1, width), lambda i: (0, 0))],
        out_shape=[jax.ShapeDtypeStruct((t, width), BF16), jax.ShapeDtypeStruct((1, width), F32)],
        compiler_params=_cparams(("arbitrary",)),
    )(dps, dps, p, p, mu)


def loss_head(x3, tgt, g):
    t, d = x3.shape
    tm = _tile(t, 256)

    def body(x_ref, t_ref, g_ref, loss_ref, dx_ref, dg_ref, dxb_ref):
        (y,), vjp = jax.vjp(lambda a, b: rms_fn(a, b), x_ref[...], g_ref[...])
        diff = y - t_ref[...]
        part = 0.5 * jnp.sum(jnp.mean(diff * diff, axis=-1, keepdims=True), axis=0, keepdims=True)
        dx, dg = vjp((diff * (1.0 / d),))
        dx_ref[...] = dx
        dxb_ref[...] = dx.astype(BF16)
        part = jnp.broadcast_to(part, (1, LANES))
        first = pl.program_id(0) == 0

        @pl.when(first)
        def _():
            loss_ref[...] = part
            dg_ref[...] = dg

        @pl.when(jnp.logical_not(first))
        def _():
            loss_ref[...] += part
            dg_ref[...] += dg

    row = pl.BlockSpec((tm, d), lambda i: (i, 0))
    vec = pl.BlockSpec((1, d), lambda i: (0, 0))
    return _pcall(
        body, name="loss_head", grid=(t // tm,), in_specs=[row, row, vec],
        out_specs=[pl.BlockSpec((1, LANES), lambda i: (0, 0)), row, vec, row],
        out_shape=[jax.ShapeDtypeStruct((1, LANES), F32), jax.ShapeDtypeStruct((t, d), F32),
                   jax.ShapeDtypeStruct((1, d), F32), jax.ShapeDtypeStruct((t, d), BF16)],
        compiler_params=_cparams(("arbitrary",)),
    )(x3, tgt, g)


ADAM_LR, ADAM_B1, ADAM_B2, ADAM_EPS, ADAM_WD, ADAM_STEP = 0.001, 0.9, 0.999, 1e-08, 0.01, 10


def adamw(w, g, m, v, name):
    rows, width = w.shape
    tr = _div_tile(rows, 256, 8)

    def body(w_ref, g_ref, m_ref, v_ref, d_ref, nm_ref, nv_ref):
        gv = g_ref[...]
        m_new = ADAM_B1 * m_ref[...] + (1.0 - ADAM_B1) * gv
        v_new = ADAM_B2 * v_ref[...] + (1.0 - ADAM_B2) * (gv * gv)
        m_hat = m_new / (1.0 - ADAM_B1 ** ADAM_STEP)
        v_hat = v_new / (1.0 - ADAM_B2 ** ADAM_STEP)
        d_ref[...] = -ADAM_LR * (m_hat / (jnp.sqrt(v_hat) + ADAM_EPS) + ADAM_WD * w_ref[...])
        nm_ref[...] = m_new
        nv_ref[...] = v_new

    spec = pl.BlockSpec((tr, width), lambda i: (i, 0))
    return _pcall(
        body, name=name, grid=(rows // tr,), in_specs=[spec] * 4, out_specs=[spec] * 3,
        out_shape=[jax.ShapeDtypeStruct((rows, width), F32)] * 3,
        compiler_params=_cparams(("parallel",)),
    )(w, g, m, v)


def sum_slots(q, name, for_swap=False):
    n, rows, width = q.shape
    tr = _div_tile(rows, 512, 32 // q.dtype.itemsize)

    def body(*refs):
        refs = refs[for_swap:]
        acc = refs[0][...].astype(F32)
        for r in refs[1:n]:
            acc = acc + r[...].astype(F32)
        refs[n][...] = acc

    if not for_swap:
        specs = [pl.BlockSpec((None, tr, width), functools.partial(lambda s, i: (s, i, 0), s)) for s in range(n)]
        return _pcall(body, name=name, grid=(rows // tr,), in_specs=specs,
                      out_specs=pl.BlockSpec((tr, width), lambda i: (i, 0)),
                      out_shape=jax.ShapeDtypeStruct((rows, width), F32),
                      compiler_params=_cparams(("parallel",)))(*([q] * n))
    specs = [pl.BlockSpec((None, tr, width), functools.partial(lambda s, i, c_ref: (s, i, 0), s)) for s in range(n)]
    grid_spec = pltpu.PrefetchScalarGridSpec(
        num_scalar_prefetch=1, grid=(rows // tr,), in_specs=specs,
        out_specs=pl.BlockSpec((None, tr, width), lambda i, c_ref: (c_ref[0], i, 0)))
    return _pcall(body, name=name, grid_spec=grid_spec, out_shape=jax.ShapeDtypeStruct((2, rows, width), F32),
                  compiler_params=_cparams(("parallel",)))(
        lax.axis_index("c").astype(jnp.int32).reshape(1), *([q] * n))


MESH = pl.DeviceIdType.MESH
ANY_SPEC = pl.BlockSpec(memory_space=pl.ANY)


def _coords():
    return lax.axis_index("x"), lax.axis_index("y"), lax.axis_index("c")


def _other_chips(x, y):
    return [(1 - x, y), (x, 1 - y), (1 - x, 1 - y)]


def push(name, srcs, n_dst, plan_fn):
    n_arr = len(srcs)
    n_send = len(plan_fn(0, 0, 0))

    def body(*refs):
        src_refs, dst_refs = refs[:n_arr], refs[n_arr:2 * n_arr]
        send_sems, recv_sems = refs[2 * n_arr:]
        sends = plan_fn(*_coords())
        out = []
        for q, (src_ref, dst_ref) in enumerate(zip(src_refs, dst_refs)):
            out += [pltpu.make_async_remote_copy(src_ref.at[si], dst_ref.at[di], send_sems.at[q * n_send + k],
                                                 recv_sems.at[q * n_send + k], device_id=peer, device_id_type=MESH)
                    for k, (si, peer, di, _) in enumerate(sends)]
        for cp in out:
            cp.start()
        for q, (src_ref, dst_ref) in enumerate(zip(src_refs, dst_refs)):
            for k, (si, peer, _, ri) in enumerate(sends):
                pltpu.make_async_remote_copy(src_ref.at[si], dst_ref.at[ri], send_sems.at[q * n_send + k],
                                             recv_sems.at[q * n_send + k], device_id=peer,
                                             device_id_type=MESH).wait_recv()
        for cp in out:
            cp.wait_send()

    return _pcall(
        body, name=name, in_specs=[ANY_SPEC] * n_arr, out_specs=[ANY_SPEC] * n_arr,
        out_shape=[jax.ShapeDtypeStruct((n_dst,) + s.shape[1:], s.dtype) for s in srcs],
        scratch_shapes=[pltpu.SemaphoreType.DMA((n_arr * n_send,)), pltpu.SemaphoreType.DMA((n_arr * n_send,))],
    )(*srcs)


def plan_sibling_quarters(x, y, c):
    return [(2 * s + (1 - c), (x, y, 1 - c), s, s) for s in range(4)]


def plan_chips_by_shard(x, y, c):
    me = 2 * x + y
    return [(2 * px + py, (px, py, c), me, 2 * px + py) for px, py in _other_chips(x, y)]


def plan_chips_half(x, y, c):
    me = 2 * x + y
    return [(c, (px, py, c), me, 2 * px + py) for px, py in _other_chips(x, y)]


def swap_halves(name, bufs):
    n_arr = len(bufs)

    def body(*refs):
        buf_refs = refs[n_arr:2 * n_arr]
        send_sems, recv_sems = refs[2 * n_arr:]
        x, y, c = _coords()
        out = [pltpu.make_async_remote_copy(b.at[c], b.at[c], send_sems.at[q], recv_sems.at[q],
                                            device_id=(x, y, 1 - c), device_id_type=MESH)
               for q, b in enumerate(buf_refs)]
        for cp in out:
            cp.start()
        for q, b in enumerate(buf_refs):
            pltpu.make_async_remote_copy(b.at[1 - c], b.at[1 - c], send_sems.at[q], recv_sems.at[q],
                                         device_id=(x, y, 1 - c), device_id_type=MESH).wait_recv()
        for cp in out:
            cp.wait_send()

    return _pcall(
        body, name=name, in_specs=[ANY_SPEC] * n_arr, out_specs=[ANY_SPEC] * n_arr,
        out_shape=[jax.ShapeDtypeStruct(b.shape, b.dtype) for b in bufs],
        input_output_aliases={q: q for q in range(n_arr)},
        scratch_shapes=[pltpu.SemaphoreType.DMA((n_arr,)), pltpu.SemaphoreType.DMA((n_arr,))],
    )(*bufs)


HBM_SPEC = pl.BlockSpec(memory_space=pltpu.HBM)
SEM_SPEC = pl.BlockSpec(memory_space=pltpu.SEMAPHORE)
DATAFLOW = pltpu.SideEffectType.DATAFLOW_SIDE_EFFECTING


def _in_hbm(a):
    return pltpu.with_memory_space_constraint(a, pltpu.HBM)


def push_start(name, srcs, lands, plan_fn, after):
    n_arr = len(srcs)
    n_send = len(plan_fn(0, 0, 0))
    arrays = list(srcs) + ([] if lands is None else list(lands))
    if after is None:
        after = jnp.zeros((8, LANES), F32)
    n_all = len(arrays)

    def body(*refs):
        src_refs, land_refs = refs[:n_arr], refs[n_all - n_arr:n_all]
        send_sems, recv_sems = refs[n_all + 1:n_all + 3]
        token = refs[-1]
        sends = plan_fn(*_coords())
        for q, (src_ref, land_ref) in enumerate(zip(src_refs, land_refs)):
            for k, (si, peer, di, _) in enumerate(sends):
                pltpu.make_async_remote_copy(src_ref.at[si], land_ref.at[di], send_sems.at[q * n_send + k],
                                             recv_sems.at[q * n_send + k], device_id=peer,
                                             device_id_type=MESH).start()
        token[...] = jnp.zeros_like(token)

    sems = pltpu.SemaphoreType.DMA((n_arr * n_send,))
    out = _pcall(
        body, name=name,
        in_specs=[HBM_SPEC] * n_all + [ANY_SPEC],
        out_specs=[SEM_SPEC, SEM_SPEC] + [HBM_SPEC] * n_all + [pl.BlockSpec(memory_space=pltpu.VMEM)],
        out_shape=[sems, sems] + [pltpu.HBM(a.shape, a.dtype) for a in arrays]
        + [jax.ShapeDtypeStruct((8, LANES), F32)],
        input_output_aliases={q: 2 + q for q in range(n_all)},
        compiler_params=pltpu.CompilerParams(has_side_effects=DATAFLOW),
    )(*[_in_hbm(a) for a in arrays], after)
    return out[:-1], out[-1]


def push_wait(name, state, plan_fn, after, in_place=False, both=False):
    arrays = list(state[2:])
    n_all = len(arrays)
    n_arr = n_all if in_place else n_all // 2
    n_send = len(plan_fn(0, 0, 0))
    send_sems, recv_sems = state[:2]

    def body(*refs):
        src_refs, land_refs = refs[:n_arr], refs[n_all - n_arr:n_all]
        send_ref, recv_ref = refs[n_all:n_all + 2]
        sends = plan_fn(*_coords())
        for q, (src_ref, land_ref) in enumerate(zip(src_refs, land_refs)):
            for k, (si, peer, _, ri) in enumerate(sends):
                cp = pltpu.make_async_remote_copy(src_ref.at[si], land_ref.at[ri], send_ref.at[q * n_send + k],
                                                  recv_ref.at[q * n_send + k], device_id=peer, device_id_type=MESH)
                cp.wait_send()
                cp.wait_recv()

    out = _pcall(
        body, name=name,
        in_specs=[HBM_SPEC] * n_all + [SEM_SPEC, SEM_SPEC, ANY_SPEC],
        out_specs=[HBM_SPEC] * n_all,
        out_shape=[pltpu.HBM(a.shape, a.dtype) for a in arrays],
        input_output_aliases={q: q for q in range(n_all)},
        compiler_params=pltpu.CompilerParams(has_side_effects=DATAFLOW),
    )(*arrays, send_sems, recv_sems, after)
    return list(out) if both else list(out[n_all - n_arr:])


def plan_chips_gather(x, y, c):
    me = 2 * x + y
    return [(c, (px, py, c), 2 * me + c, 2 * (2 * px + py) + c) for px, py in _other_chips(x, y)]


def plan_forward(x, y, c):
    return [(2 * (2 * px + py) + c, (x, y, 1 - c), 2 * (2 * px + py) + c, 2 * (2 * px + py) + (1 - c))
            for px, py in _other_chips(x, y)]


def add_kept(g8, other, name):
    _, rows, cols = other.shape
    tr = _div_tile(rows, 512, 16)

    def body(c_ref, a_ref, b_ref, o_ref):
        o_ref[...] = (a_ref[...].astype(F32) + b_ref[...].astype(F32)).astype(BF16)

    spec = pl.BlockSpec((None, tr, cols), lambda s, i, c_ref: (s, i, 0))
    grid_spec = pltpu.PrefetchScalarGridSpec(
        num_scalar_prefetch=1, grid=(4, rows // tr),
        in_specs=[pl.BlockSpec((None, tr, cols), lambda s, i, c_ref: (2 * s + c_ref[0], i, 0)), spec],
        out_specs=spec)
    return _pcall(body, name=name, grid_spec=grid_spec, out_shape=jax.ShapeDtypeStruct((4, rows, cols), BF16),
                  compiler_params=_cparams(("parallel", "parallel")))(
        lax.axis_index("c").astype(jnp.int32).reshape(1), g8, other)


def reduce_scatter_begin(grads, names, tag):
    g8 = [g.reshape(8, g.shape[1] // 2, g.shape[2]) for g in grads]
    return _chip_sums_start(g8, push(tag + "_sibling", g8, 4, plan_sibling_quarters), names, tag)


def _chip_sums_start(g8, from_sibling, names, tag):
    chip_sum = [add_kept(a, b, tag + "_add_" + n) for a, b, n in zip(g8, from_sibling, names)]
    return push_start(tag + "_start", chip_sum, chip_sum, plan_chips_by_shard, None)


def sibling_exchange_begin(grads, tag):
    g8 = [g.reshape(8, g.shape[1] // 2, g.shape[2]) for g in grads]
    lands = [lax.empty((4,) + g.shape[1:], g.dtype) for g in g8]
    return push_start(tag + "_sib_start", g8, lands, plan_sibling_quarters, None)


def reduce_scatter_continue(state, names, tag, after):
    n = len(names)
    arrays = push_wait(tag + "_sib_wait", state, plan_sibling_quarters, after, both=True)
    return _chip_sums_start(arrays[:n], arrays[n:], names, tag)


def reduce_scatter_end(state, names, tag, after):
    by_chip = push_wait(tag + "_wait", state, plan_chips_by_shard, after)
    both = swap_halves(tag + "_halves", [sum_slots(q, tag + "_sum_" + n, for_swap=True)
                                         for q, n in zip(by_chip, names)])
    return [b.reshape(2 * b.shape[1], b.shape[2]) for b in both]


def _landing(mine):
    return [jnp.broadcast_to(m[None], (4,) + m.shape).reshape((8,) + m.shape[1:]) for m in mine]


def gather_begin(tag, mine, after, lands=None):
    return push_start(tag + "_start", mine, _landing(mine) if lands is None else lands, plan_chips_gather, after)


def gather_forward(tag, state, after):
    lands = push_wait(tag + "_wait", state, plan_chips_gather, after)
    return push_start(tag + "_fwd_start", lands, None, plan_forward, None)


def gather_end(tag, state, after):
    bufs = push_wait(tag + "_fwd_wait", state, plan_forward, after, in_place=True)
    return [g.reshape(4, 2 * g.shape[1], g.shape[2]) for g in bufs]


PACK_QUANTUM = 8 * C


def _pack(parts, dtype):
    pieces = []
    for p in parts:
        flat = p.astype(dtype).reshape(-1)
        pieces.append(jnp.pad(flat, (0, -flat.shape[0] % PACK_QUANTUM)).reshape(-1, C))
    rows = sum(p.shape[0] for p in pieces)
    if rows % 16:
        pieces.append(jnp.zeros((8, C), dtype))
    return jnp.concatenate(pieces, axis=0)


def _unpack(packed, shapes):
    out, at = [], 0
    for s in shapes:
        n = int(np.prod(s))
        rows = -(-n // PACK_QUANTUM) * 8
        out.append(packed[at:at + rows].reshape(-1)[:n].reshape(s))
        at += rows
    return out


def all_reduce_begin(vec):
    rows, width = vec.shape
    (pair,) = swap_halves("ar_sibling", [jnp.broadcast_to(vec[None], (2, rows, width))])
    chip_sum = sum_slots(pair, "ar_sum_sibling").reshape(2, rows // 2, width)
    mine = lax.dynamic_index_in_dim(chip_sum, lax.axis_index("c"), axis=0, keepdims=True)
    return push_start("ar_start", [chip_sum], [jnp.broadcast_to(mine, (4, rows // 2, width))], plan_chips_half,
                      None)


def all_reduce_end(state, after):
    (by_chip,) = push_wait("ar_wait", state, plan_chips_half, after)
    (both,) = swap_halves("ar_halves", [sum_slots(by_chip, "ar_sum_chips", for_swap=True)])
    return both.reshape(2 * both.shape[1], both.shape[2])


def kernel(x, norm1_g, w_in, shift_mu, w0, w2, a0, a2, g2, k_k, k_a, r_k, gn_w, gn_b, sgu_ln_g, sgu_ln_b, sgu_w, sgu_b, sgu_out_g, w_out, norm2_g, w_gate, w_up, w_down, final_g, loss_target, m_norm1_g, m_w_in, m_shift_mu, m_w0, m_w2, m_a0, m_a2, m_g2, m_k_k, m_k_a, m_r_k, m_gn_w, m_gn_b, m_sgu_ln_g, m_sgu_ln_b, m_sgu_w, m_sgu_b, m_sgu_out_g, m_w_out, m_norm2_g, m_w_gate, m_w_up, m_w_down, m_final_g, v_norm1_g, v_w_in, v_shift_mu, v_w0, v_w2, v_a0, v_a2, v_g2, v_k_k, v_k_a, v_r_k, v_gn_w, v_gn_b, v_sgu_ln_g, v_sgu_ln_b, v_sgu_w, v_sgu_b, v_sgu_out_g, v_w_out, v_norm2_g, v_w_gate, v_w_up, v_w_down, v_final_g):
    args = dict(locals())
    names = ["norm1_g", "w_in", "shift_mu", "w0", "w2", "a0", "a2", "g2", "k_k", "k_a", "r_k", "gn_w", "gn_b",
             "sgu_ln_g", "sgu_ln_b", "sgu_w", "sgu_b", "sgu_out_g", "w_out", "norm2_g", "w_gate", "w_up", "w_down",
             "final_g"]
    big = ["w_in", "w_out", "w_gate", "w_up", "w_down"]
    xi, yi = lax.axis_index("x"), lax.axis_index("y")
    chip = 2 * xi + yi
    x2d, tgt = x[0], loss_target[0]
    t = x2d.shape[0]

    lora_names = ["w2", "a2", "g2"]
    first, later = ["w_in"] + lora_names, ["w_out", "w_gate", "w_up", "w_down"]

    def halves(n, token=None):
        m = args[n][0] if token is None else args[n][0] + token[0, 0]
        m = m.astype(BF16)
        return m.reshape(2, m.shape[0] // 2, m.shape[1])

    first_state, first_token = gather_begin("g0", [halves(n) for n in first], norm1_g)
    later_mine = [halves(n, first_token) for n in later]
    later_lands = _landing(later_mine)
    (h1,) = row_fwd("norm1", rms_fn, [Tiled(x2d)], [norm1_g + first_token[0, 0]], [], [(D, BF16)], _tile(t, 256))
    prepared = functools.reduce(lambda p, q: p + q, [a[0, 0, :1].astype(F32) for a in later_lands])
    first_state, first_token = gather_forward("g0", first_state, prepared + h1[0, :1].astype(F32))
    got = gather_end("g0", first_state, first_token)
    later_state, later_token = gather_begin("g1", later_mine, got[0], later_lands)
    full = {n: blk.transpose(1, 0, 2).reshape(blk.shape[1], 4 * blk.shape[2]) for n, blk in zip(first[1:], got[1:])}
    s0, s1, s2, s3 = got[0]
    w_in_p = jnp.concatenate([s0, s1, s2[:, :CUT_A], s2[:, CUT_B:], s3, s2[:, CUT_A:CUT_B],
                              jnp.zeros((D, P_W - D_IN), BF16)], axis=1)
    mu_rkv = shift_mu[:, :3 * C] + later_token[0, 0]
    mu_lora = jnp.pad(shift_mu[:, 3 * C:], ((0, 0), (0, LORA_W - 288)))
    w2p = jnp.pad(full["w2"], ((0, 64), (0, 0)))
    a2p = jnp.pad(full["a2"], ((64, 0), (0, 0)))
    g2p = jnp.pad(full["g2"], ((0, 96), (0, 0)))
    ii = np.arange(C)
    jm = _scan_consts()[2]
    e_np = (np.arange(LANES)[:, None] == (ii[None, :] // HEAD))
    e_mat, e_t = jnp.asarray(e_np, BF16), jnp.asarray(e_np.T, BF16)
    b_t = jnp.pad(sgu_b[0].T, ((0, 0), (0, LANES - HEADS)))
    r_k_flat = r_k.reshape(1, C)
    tm_row = _tile(t, 256)
    tm_sgu_b = _tile(t, 128)

    p = mm(h1, w_in_p, name="proj_in")
    ps_rkv = shift_fwd(p, mu_rkv, 0, "shift_rkv")
    ps_lora = shift_fwd(p, mu_lora, P_LORA // LORA_W, "shift_lora")
    pre_par = [w0, w2p, a0, a2p, g2p, k_k, k_a]
    r_, w_, k2, v_, an, b_, g_ = row_fwd("pre", pre_fn, [Tiled(ps_rkv), Tiled(ps_lora)], pre_par, [jm],
                                         [(C, F32)] * 7, tm_row)
    v3 = to_cols(v_)
    y3, ckpt = wkv_fwd(r_, w_, k2, v3, an, b_)
    later_state, later_token = gather_forward("g1", later_state, y3)
    y_ = from_cols(y3 + later_token[0, 0])
    post_in = [Tiled(z) for z in (y_, r_, k2, v_, g_)]
    post_par = [gn_w, gn_b, r_k_flat]
    (y_rwkv,) = row_fwd("post", post_fn, post_in, post_par, [jm], [(C, BF16)], tm_row)
    sgu_in = [Tiled(p, C, 3), Tiled(p, C, 4)]
    sgu_par = [sgu_ln_g, sgu_ln_b, sgu_w[0], b_t, sgu_out_g]
    (y_sgu,) = row_fwd("sgu", sgu_fn, sgu_in, sgu_par, [e_mat, e_t], [(C, BF16)], tm_row)
    y_cat = jnp.concatenate([y_rwkv, y_sgu], axis=1)
    for n, blk in zip(later, gather_end("g1", later_state, y_cat)):
        full[n] = blk if n in ("w_gate", "w_up") else blk.reshape(4 * blk.shape[1], blk.shape[2])
    x_mid = mm(y_cat, full["w_out"], res=x2d, name="proj_out")
    (h2,) = row_fwd("norm2", rms_fn, [Tiled(x_mid)], [norm2_g], [], [(D, BF16)], tm_row)
    gate, up, act = ffn_in(h2, full["w_gate"], full["w_up"], lambda g_tile, u_tile: swiglu_fn(g_tile, u_tile)[0])
    x_out = mm(act, full["w_down"], res=x_mid, name="ffn_down")
    loss_part, dx_out, d_final_g, dx_out_b = loss_head(x_out, tgt, final_g.reshape(1, D))

    grads = {"final_g": d_final_g.reshape(D)}
    d_gate, d_up = mm(dx_out_b, full["w_down"], tb=True, name="d_act",
                      post=(lambda d_act, g_tile, u_tile: jax.vjp(swiglu_fn, g_tile, u_tile)[1]((d_act,)), [gate, up],
                            [BF16, BF16]))
    grads["w_down"] = mm(act, dx_out_b, ta=True, out_dtype=BF16, name="d_w_down")
    grads["w_gate"] = mm(h2, d_gate, shards="out", out_dtype=BF16, name="d_w_gate")
    grads["w_up"] = mm(h2, d_up, shards="out", out_dtype=BF16, name="d_w_up")
    d_h2 = mm(d_gate, full["w_gate"], shards="bt", name="d_h2_gate")
    d_h2 = mm(d_up, full["w_up"], shards="bt", res=d_h2, name="d_h2_up")
    dx_mid, grads["norm2_g"], dx_mid_b = row_bwd("norm2_b", rms_res_fn, [Tiled(x_mid)], [norm2_g], [],
                                                 [Tiled(d_h2), Tiled(dx_out)], tm_row, bf16_copy_of=(0,))
    d_ycat = mm(dx_mid_b, full["w_out"], tb=True, name="d_ycat")
    grads["w_out"] = mm(y_cat, dx_mid_b, ta=True, out_dtype=BF16, name="d_w_out")
    rs_state, rs_token = sibling_exchange_begin([grads[n].reshape((4,) + args[n].shape[1:]) for n in later], "rs2")
    d_pu, d_pv, grads["sgu_ln_g"], grads["sgu_ln_b"], d_sgu_w, d_b_t, grads["sgu_out_g"] = row_bwd(
        "sgu_b", sgu_fn, sgu_in, sgu_par[:-1] + [sgu_out_g + rs_token[0, 0]], [e_mat, e_t], [Tiled(d_ycat, C, 1)],
        tm_sgu_b, tiled_out_dtypes=[BF16, BF16])
    grads["sgu_w"] = d_sgu_w[None]
    grads["sgu_b"] = d_b_t[:, :HEADS].T[None]
    post_par_b = [gn_w + rs_token[0, 0]] + post_par[1:]
    d_y, d_r1, d_k1, d_v1, d_g, grads["gn_w"], grads["gn_b"], d_r_k = row_bwd(
        "post_b", post_fn, post_in, post_par_b, [jm], [Tiled(d_ycat, C, 0)], tm_row)
    grads["r_k"] = d_r_k.reshape(r_k.shape)
    rs_state, rs_token = reduce_scatter_continue(rs_state, later, "rs2", d_y[0, :1] + d_pu[0, :1].astype(F32))
    d_r2, d_w, d_k2, d_a, d_b, d_v3 = wkv_bwd(r_, w_, k2, v3, an, b_, ckpt, to_cols(d_y + rs_token[0, 0]))
    d_v2 = from_cols(d_v3)
    pre_cts = [(Tiled(d_r1), Tiled(d_r2)), Tiled(d_w), (Tiled(d_k1), Tiled(d_k2)), (Tiled(d_v1), Tiled(d_v2)),
               Tiled(d_a), Tiled(d_b), Tiled(d_g)]
    d_ps_rkv, d_ps_lora, grads["w0"], d_w2p, grads["a0"], d_a2p, d_g2p, grads["k_k"], grads["k_a"] = row_bwd(
        "pre_b", pre_fn, [Tiled(ps_rkv), Tiled(ps_lora)], pre_par, [jm], pre_cts, tm_row)
    grads["w2"], grads["a2"], grads["g2"] = d_w2p[None, :64], d_a2p[None, 64:], d_g2p[None, :160]
    d_p_rkv, d_mu_rkv = shift_bwd(d_ps_rkv, p, mu_rkv, 0, "shift_rkv_b")
    d_p_lora, d_mu_lora = shift_bwd(d_ps_lora, p, mu_lora, P_LORA // LORA_W, "shift_lora_b")
    grads["shift_mu"] = jnp.concatenate([d_mu_rkv, d_mu_lora[:, :288]], axis=1)
    d_p = jnp.concatenate([d_p_rkv, d_pu, d_pv, d_p_lora], axis=1)
    d_h1 = mm(d_p, w_in_p, tb=True, name="d_h1")
    d_w_in_p = mm(h1, d_p, ta=True, out_dtype=BF16, name="d_w_in")
    ws = D_IN // 4
    shard2 = jnp.concatenate([d_w_in_p[:, 2 * ws:3 * C], d_w_in_p[:, P_LORA:P_LORA + 288],
                              d_w_in_p[:, 3 * C:3 * C + ws - CUT_B]], axis=1)
    grads["w_in"] = jnp.stack([d_w_in_p[:, :ws], d_w_in_p[:, ws:2 * ws], shard2,
                               d_w_in_p[:, 3 * C + ws - CUT_B:P_LORA]])
    dx, grads["norm1_g"] = row_bwd("norm1_b", rms_res_fn, [Tiled(x2d)], [norm1_g], [],
                                   [Tiled(d_h1), Tiled(dx_mid)], tm_row)

    small = [n for n in names if n not in big]
    small_shapes = [(1, 64, C) if n in ("w2", "a2") else (1, 160, C) if n == "g2" else args[n].shape for n in small]
    vec = _pack([grads[n] for n in small] + [loss_part[:, :1]], F32)
    w_in_state, w_in_token = reduce_scatter_begin([grads["w_in"]], ["w_in"], "rs1")
    small_state, small_token = all_reduce_begin(vec)
    total = {n: g[None] for n, g in zip(later, reduce_scatter_end(rs_state, later, "rs2", w_in_token + small_token))}
    delta, new_m, new_v = {}, {}, {}

    def adam_big(n):
        s = args[n].shape
        view = (lambda a: a[0].T) if n == "w_in" else (lambda a: a[0])
        back = (lambda a: a.T.reshape(s)) if n == "w_in" else (lambda a: a.reshape(s))
        d_, m_, v__ = adamw(view(args[n]), view(total[n]), view(args["m_" + n]), view(args["v_" + n]), "adamw_" + n)
        delta[n], new_m[n], new_v[n] = back(d_), back(m_), back(v__)

    for n in later:
        adam_big(n)
    total["w_in"] = reduce_scatter_end(w_in_state, ["w_in"], "rs1", new_v["w_down"])[0][None]
    *small_sums, loss = _unpack(all_reduce_end(small_state, new_v["w_down"]), small_shapes + [()])
    for n, g_sum in zip(small, small_sums):
        total[n] = lax.dynamic_slice_in_dim(g_sum, chip * 256, 256, axis=2) if n in lora_names else g_sum
    adam_big("w_in")
    flat = [_pack([src[n] for n in small], F32)
            for src in (args, total, {n: args["m_" + n] for n in small}, {n: args["v_" + n] for n in small})]
    outs = adamw(*flat, "adamw_small")
    for res, o in zip((delta, new_m, new_v), outs):
        res.update(zip(small, _unpack(o, [args[n].shape for n in small])))
    return (loss, dx[None], *[total[n] for n in names], *[delta[n] for n in names],
            *[new_m[n] for n in names], *[new_v[n] for n in names])
```

```python
import functools

import numpy as np
import jax
import jax.numpy as jnp
from jax import lax
from jax.experimental import pallas as pl
from jax.experimental.pallas import tpu as pltpu

F32 = jnp.float32
BF16 = jnp.bfloat16

D = 2048
C = 1024
HEADS = 16
HEAD = 64
LANES = 128
P_W = 5632
P_LORA = 5120
LORA_W = 512
D_IN = 5408
CUT_A = 3 * C - 2 * (D_IN // 4)
CUT_B = CUT_A + 288
CHUNK = 128
RMS_EPS = 1e-6
LN_EPS = 1e-5
GN_EPS = 64e-5
L2_EPS = 1e-12
VMEM_LIMIT = 56 * 1024 * 1024


def _pcall(body, **kw):
    return pl.pallas_call(body, **kw)


def _cparams(sem):
    return pltpu.CompilerParams(dimension_semantics=sem, vmem_limit_bytes=VMEM_LIMIT)


def _tile(n, most):
    t = most
    while t > 8 and n % t:
        t //= 2
    assert n % t == 0, (n, most)
    return t


MM_TILE = (2048, 512, 2048)
MM_VMEM = 40 * 1024 * 1024


def _div_tile(n, most, quantum=LANES):
    for t in range(min(n, most) // quantum * quantum, 0, -quantum):
        if n % t == 0:
            return t
    raise ValueError((n, most, quantum))


def _mm_tiles(m, n, k, a_bytes, b_bytes, r_bytes, o_bytes):
    tm, tn, tk = _div_tile(m, MM_TILE[0]), _div_tile(n, MM_TILE[1]), _div_tile(k, MM_TILE[2])

    def need(tm, tk):
        return 2 * (tm * tk * a_bytes + tk * tn * b_bytes + tm * tn * (r_bytes + o_bytes)) + tm * tn * 4

    while need(tm, tk) > MM_VMEM:
        if tk >= tm and tk > 512:
            tk = _div_tile(k, tk - LANES)
        else:
            tm = _div_tile(m, tm - LANES)
    return tm, tn, tk


def mm(a, b, *, ta=False, tb=False, res=None, out_dtype=F32, name, shards=None, post=None):
    if shards is not None:
        return _mm_shards(a, b, res, out_dtype, name, shards, post)
    m, k = (a.shape[1], a.shape[0]) if ta else a.shape
    n = b.shape[0] if tb else b.shape[1]
    assert (b.shape[1] if tb else b.shape[0]) == k
    if post is None:
        tm, tn, tk = _mm_tiles(m, n, k, a.dtype.itemsize, b.dtype.itemsize,
                               0 if res is None else res.dtype.itemsize, jnp.dtype(out_dtype).itemsize)
    else:
        tm, tn, tk = _div_tile(m, 1024), _div_tile(n, MM_TILE[1]), _div_tile(k, MM_TILE[2])
    nk = k // tk
    dims = (((0 if ta else 1,), (1 if tb else 0,)), ((), ()))
    a_spec = pl.BlockSpec((tk, tm), lambda i, j, l: (l, i)) if ta else pl.BlockSpec((tm, tk), lambda i, j, l: (i, l))
    b_spec = pl.BlockSpec((tn, tk), lambda i, j, l: (j, l)) if tb else pl.BlockSpec((tk, tn), lambda i, j, l: (l, j))
    o_spec = pl.BlockSpec((tm, tn), lambda i, j, l: (i, j))
    return _mm_call(a, b, res, dims, (m // tm, n // tn, nk), a_spec, b_spec, o_spec, o_spec, (tm, tn),
                    jax.ShapeDtypeStruct((m, n), out_dtype), name, post)


def _mm_shards(a, b, res, out_dtype, name, shards, post=None):
    assert post is None or shards == "b"
    if shards == "b":
        (m, k), ns = a.shape, b.shape[2]
        tm, tk = _div_tile(m, 1024 if post is None else 512), _div_tile(k, 2048)
        grid, dims, acc = (m // tm, 4, k // tk), (((1,), (0,)), ((), ())), (tm, ns)
        a_spec = pl.BlockSpec((tm, tk), lambda i, s, l: (i, l))
        b_spec = pl.BlockSpec((None, tk, ns), lambda i, s, l: (s, l, 0))
        o_spec = pl.BlockSpec((tm, ns), lambda i, s, l: (i, s))
        out = jax.ShapeDtypeStruct((m, 4 * ns), out_dtype)
    elif shards == "bt":
        m, (_, n, ns) = a.shape[0], b.shape
        tm, tn = _div_tile(m, 2048), _div_tile(n, 512)
        grid, dims, acc = (m // tm, n // tn, 4), (((1,), (1,)), ((), ())), (tm, tn)
        a_spec = pl.BlockSpec((tm, ns), lambda i, j, s: (i, s))
        b_spec = pl.BlockSpec((None, tn, ns), lambda i, j, s: (s, j, 0))
        o_spec = pl.BlockSpec((tm, tn), lambda i, j, s: (i, j))
        out = jax.ShapeDtypeStruct((m, n), out_dtype)
    else:
        (t, m), ns = a.shape, b.shape[1] // 4
        tm, tk = _div_tile(m, 1024), _div_tile(t, 2048)
        grid, dims, acc = (m // tm, 4, t // tk), (((0,), (0,)), ((), ())), (tm, ns)
        a_spec = pl.BlockSpec((tk, tm), lambda i, s, l: (l, i))
        b_spec = pl.BlockSpec((tk, ns), lambda i, s, l: (l, s))
        o_spec = pl.BlockSpec((None, tm, ns), lambda i, s, l: (s, i, 0))
        out = jax.ShapeDtypeStruct((4, m, ns), out_dtype)
    return _mm_call(a, b, res, dims, grid, a_spec, b_spec, o_spec, o_spec, acc, out, name, post)


def ffn_in(h, w_gate, w_up, tile_fn):
    (m, k), ns = h.shape, w_gate.shape[2]
    tm = _div_tile(m, 256)

    def body(h_ref, g_ref, u_ref, gate_ref, up_ref, act_ref):
        hv = h_ref[...]
        gate = jnp.dot(hv, g_ref[...], preferred_element_type=F32)
        up = jnp.dot(hv, u_ref[...], preferred_element_type=F32)
        gate_ref[...] = gate
        up_ref[...] = up
        act_ref[...] = tile_fn(gate, up).astype(BF16)

    w_spec = pl.BlockSpec((None, k, ns), lambda s, i: (s, 0, 0))
    o_spec = pl.BlockSpec((tm, ns), lambda s, i: (i, s))
    return _pcall(
        body, name="ffn_in", grid=(4, m // tm),
        in_specs=[pl.BlockSpec((tm, k), lambda s, i: (i, 0)), w_spec, w_spec], out_specs=[o_spec] * 3,
        out_shape=[jax.ShapeDtypeStruct((m, 4 * ns), dt) for dt in (F32, F32, BF16)],
        compiler_params=_cparams(("parallel", "parallel")),
    )(h, w_gate, w_up)


def ffn_in_bwd(d_gate, d_up, w_gate, w_up):
    m, (_, n, ns) = d_gate.shape[0], w_gate.shape
    tm, tn = _div_tile(m, 2048), _div_tile(n, 512)
    dims = (((1,), (1,)), ((), ()))

    def body(g_ref, u_ref, wg_ref, wu_ref, o_ref, acc_ref):
        s = pl.program_id(2)
        prod = (lax.dot_general(g_ref[...], wg_ref[...], dims, preferred_element_type=F32)
                + lax.dot_general(u_ref[...], wu_ref[...], dims, preferred_element_type=F32))

        @pl.when(s == 0)
        def _():
            acc_ref[...] = prod

        @pl.when(s > 0)
        def _():
            acc_ref[...] += prod

        @pl.when(s == 3)
        def _():
            o_ref[...] = acc_ref[...]

    a_spec = pl.BlockSpec((tm, ns), lambda i, j, s: (i, s))
    b_spec = pl.BlockSpec((None, tn, ns), lambda i, j, s: (s, j, 0))
    return _pcall(
        body, name="ffn_in_bwd", grid=(m // tm, n // tn, 4), in_specs=[a_spec, a_spec, b_spec, b_spec],
        out_specs=pl.BlockSpec((tm, tn), lambda i, j, s: (i, j)), out_shape=jax.ShapeDtypeStruct((m, n), F32),
        scratch_shapes=[pltpu.VMEM((tm, tn), F32)],
        compiler_params=_cparams(("parallel", "parallel", "arbitrary")),
    )(d_gate, d_up, w_gate, w_up)


def _mm_call(a, b, res, dims, grid, a_spec, b_spec, r_spec, o_spec, acc_shape, out, name, post=None):
    nk = grid[2]
    post_fn, post_in, post_dtypes = (None, [], [out.dtype]) if post is None else post
    n_extra = (res is not None) + len(post_in)

    def body(*refs):
        a_ref, b_ref = refs[:2]
        r_ref = None if res is None else refs[2]
        p_refs = refs[2 + (res is not None):2 + n_extra]
        o_refs = refs[2 + n_extra:2 + n_extra + len(post_dtypes)]

        def finish(acc):
            if r_ref is not None:
                acc = acc + r_ref[...].astype(F32)
            outs = (acc,) if post_fn is None else post_fn(acc, *[p[...] for p in p_refs])
            for o_ref, o in zip(o_refs, outs):
                o_ref[...] = o.astype(o_ref.dtype)

        prod = lax.dot_general(a_ref[...].astype(BF16), b_ref[...].astype(BF16), dims, preferred_element_type=F32)
        if nk == 1:
            finish(prod)
            return
        acc_ref = refs[-1]
        kk = pl.program_id(2)

        @pl.when(kk == 0)
        def _():
            acc_ref[...] = prod

        @pl.when(kk > 0)
        def _():
            acc_ref[...] += prod

        @pl.when(kk == nk - 1)
        def _():
            finish(acc_ref[...])

    in_specs = [a_spec, b_spec] + ([r_spec] if res is not None else []) + [o_spec] * len(post_in)
    args = (a, b) + ((res,) if res is not None else ()) + tuple(post_in)
    outs = _pcall(
        body, name=name, grid=grid, in_specs=in_specs, out_specs=[o_spec] * len(post_dtypes),
        out_shape=[jax.ShapeDtypeStruct(out.shape, dt) for dt in post_dtypes],
        scratch_shapes=[] if nk == 1 else [pltpu.VMEM(acc_shape, F32)],
        compiler_params=_cparams(("parallel", "parallel", "arbitrary")),
    )(*args)
    return outs[0] if post is None else outs


class Tiled:
    def __init__(self, arr, width=None, col=0):
        self.arr, self.width, self.col = arr, (arr.shape[1] if width is None else width), col

    def spec(self, tm):
        col = self.col
        return pl.BlockSpec((tm, self.width), lambda i: (i, col))


def _full_spec(p):
    nd = p.ndim
    return pl.BlockSpec(p.shape, lambda i: (0,) * nd)


def row_fwd(name, fn, tiled, params, consts, outs, tm):
    t = tiled[0].arr.shape[0]
    n_in = len(tiled) + len(params) + len(consts)

    def body(*refs):
        res = fn(*[r[...] for r in refs[:n_in]])
        for o_ref, r in zip(refs[n_in:], res):
            o_ref[...] = r.astype(o_ref.dtype)

    return _pcall(
        body, name=name, grid=(t // tm,),
        in_specs=[x.spec(tm) for x in tiled] + [_full_spec(p) for p in params + consts],
        out_specs=[pl.BlockSpec((tm, w), lambda i: (i, 0)) for w, _ in outs],
        out_shape=[jax.ShapeDtypeStruct((t, w), dt) for w, dt in outs],
        compiler_params=_cparams(("parallel",)),
    )(*[x.arr for x in tiled], *params, *consts)


def row_bwd(name, fn, tiled, params, consts, cts, tm, tiled_out_dtypes=None, bf16_copy_of=()):
    t = tiled[0].arr.shape[0]
    nt, npar, ncon = len(tiled), len(params), len(consts)
    cts = [c if isinstance(c, tuple) else (c,) for c in cts]
    flat_cts = [x for c in cts for x in c]
    tiled_out_dtypes = tiled_out_dtypes or [F32] * nt
    copies = [tiled[j] for j in bf16_copy_of]

    def body(*refs):
        n_in = nt + npar + ncon
        ins = [r[...].astype(F32) for r in refs[:nt + npar]]
        con = [r[...] for r in refs[nt + npar:n_in]]
        ct_refs = list(refs[n_in:n_in + len(flat_cts)])
        o = refs[n_in + len(flat_cts):]
        ct = []
        for c in cts:
            parts = [ct_refs.pop(0)[...].astype(F32) for _ in c]
            ct.append(functools.reduce(lambda p, q: p + q, parts))
        _, vjp = jax.vjp(lambda *a: fn(*a, *con), *ins)
        g = vjp(tuple(ct))
        for j in range(nt):
            o[j][...] = g[j].astype(o[j].dtype)
        for n, j in enumerate(bf16_copy_of):
            o[nt + npar + n][...] = g[j].astype(BF16)
        first = pl.program_id(0) == 0

        @pl.when(first)
        def _():
            for j in range(npar):
                o[nt + j][...] = g[nt + j]

        @pl.when(jnp.logical_not(first))
        def _():
            for j in range(npar):
                o[nt + j][...] += g[nt + j]

    return _pcall(
        body, name=name, grid=(t // tm,),
        in_specs=[x.spec(tm) for x in tiled] + [_full_spec(p) for p in params + consts]
        + [x.spec(tm) for x in flat_cts],
        out_specs=[pl.BlockSpec((tm, x.width), lambda i: (i, 0)) for x in tiled] + [_full_spec(p) for p in params]
        + [pl.BlockSpec((tm, x.width), lambda i: (i, 0)) for x in copies],
        out_shape=[jax.ShapeDtypeStruct((t, x.width), dt) for x, dt in zip(tiled, tiled_out_dtypes)]
        + [jax.ShapeDtypeStruct(p.shape, F32) for p in params]
        + [jax.ShapeDtypeStruct((t, x.width), BF16) for x in copies],
        compiler_params=_cparams(("arbitrary",)),
    )(*[x.arr for x in tiled], *params, *consts, *[x.arr for x in flat_cts])


def _split_dot(x, w):
    hi = x.astype(BF16)
    lo = (x - hi.astype(F32)).astype(BF16)
    return jnp.dot(hi, w, preferred_element_type=F32) + jnp.dot(lo, w, preferred_element_type=F32)


def _headsum_blocks(x, j2k):
    out = []
    for c0 in range(0, x.shape[1], LANES):
        blk = x[:, c0:c0 + LANES]
        hi = blk.astype(BF16)
        lo = (blk - hi.astype(F32)).astype(BF16)
        out.append(jnp.dot(jnp.concatenate([hi, lo], axis=1), j2k, preferred_element_type=F32))
    return jnp.concatenate(out, axis=1)


@jax.custom_vjp
def headsum(x, j2k):
    return _headsum_blocks(x, j2k)


def _headsum_fwd(x, j2k):
    return _headsum_blocks(x, j2k), j2k


def _headsum_bwd(j2k, ct):
    return _headsum_blocks(ct, j2k), jnp.zeros_like(j2k)


headsum.defvjp(_headsum_fwd, _headsum_bwd)


def _bdot(x, w):
    return jnp.dot(x.astype(BF16), w.astype(BF16), preferred_element_type=F32)


def _sigmoid(x):
    return jax.nn.sigmoid(x)


def _softplus(x):
    return jnp.maximum(x, 0.0) + jnp.log(1.0 + jnp.exp(-jnp.abs(x)))


def rms_fn(x, g):
    return (x * lax.rsqrt(jnp.mean(x * x, axis=-1, keepdims=True) + RMS_EPS) * g,)


def rms_res_fn(x, g):
    return rms_fn(x, g)[0], x


def pre_fn(rkv, lora, w0, w2p, a0, a2p, g2p, k_k, k_a, jm):
    r, k, v = rkv[:, :C], rkv[:, C:2 * C], rkv[:, 2 * C:]
    zwa, zg = lora[:, :LANES], lora[:, LANES:LANES + 256]
    w_log = -_softplus(-(w0 + _bdot(jnp.tanh(zwa), w2p))) - 0.5
    decay = jnp.exp(-jnp.exp(w_log))
    a = _sigmoid(a0 + _bdot(zwa, a2p))
    g = _bdot(_sigmoid(zg), g2p)
    kk = k * k_k
    kk = kk / jnp.maximum(jnp.sqrt(headsum(kk * kk, jm)), L2_EPS)
    k2 = k * (1.0 + (a - 1.0) * k_a)
    return r, decay, k2, v, -kk, kk * a, g


def post_fn(y, r, k2, v, g, gn_w, gn_b, r_k, jm):
    mu = headsum(y, jm) * (1.0 / HEAD)
    yc = y - mu
    var = headsum(yc * yc, jm) * (1.0 / HEAD)
    yn = yc * lax.rsqrt(var + GN_EPS) * gn_w + gn_b
    bonus = headsum(r * k2 * r_k, jm) * v
    return ((yn + bonus) * g,)


def _gelu(x):
    return 0.5 * x * (1.0 + lax.erf(x * np.float32(1.0 / np.sqrt(2.0))))


@jax.custom_vjp
def expand_groups(b_t, e, e_t):
    return _split_dot(b_t, e)


def _expand_groups_fwd(b_t, e, e_t):
    return _split_dot(b_t, e), (e, e_t)


def _expand_groups_bwd(res, ct):
    e, e_t = res
    return _split_dot(ct, e_t), jnp.zeros_like(e), jnp.zeros_like(e_t)


expand_groups.defvjp(_expand_groups_fwd, _expand_groups_bwd)


def sgu_fn(pu, pv, ln_g, ln_b, w_s, b_t, out_g, e, e_t):
    rows = pu.shape[0]
    b_exp = expand_groups(b_t, e, e_t)
    u, v = _gelu(pu), _gelu(pv)
    mu = jnp.mean(v, axis=-1, keepdims=True)
    vc = v - mu
    var = jnp.mean(vc * vc, axis=-1, keepdims=True)
    v = vc * lax.rsqrt(var + LN_EPS) * ln_g + ln_b
    tri = lax.broadcasted_iota(jnp.int32, (CHUNK, CHUNK), 0) >= lax.broadcasted_iota(jnp.int32, (CHUNK, CHUNK), 1)
    left = lax.broadcasted_iota(jnp.int32, (CHUNK, LANES), 1) < HEAD
    chunks = []
    for c0 in range(0, rows, CHUNK):
        cols = []
        for gp in range(HEADS // 2):
            vp = v[c0:c0 + CHUNK, gp * LANES:(gp + 1) * LANES]
            wa = jnp.where(tri, w_s[2 * gp], 0.0)
            wb = jnp.where(tri, w_s[2 * gp + 1], 0.0)
            cols.append(jnp.where(left, _bdot(wa, vp), _bdot(wb, vp)))
        chunks.append(jnp.concatenate(cols, axis=1) + b_exp)
    s = jnp.concatenate(chunks, axis=0) if len(chunks) > 1 else chunks[0]
    y = u * s
    return (y * lax.rsqrt(jnp.mean(y * y, axis=-1, keepdims=True) + RMS_EPS) * out_g,)


def swiglu_fn(gate, up):
    return (gate * _sigmoid(gate) * up,)


SCAN_CHUNK = 64
NB = C // LANES
FWD_GROUP = 2
BWD_GROUP = 8


def to_cols(a):
    t = a.shape[0]
    return a.reshape(t // 8, LANES, HEAD).transpose(0, 2, 1)


def from_cols(a):
    return a.transpose(0, 2, 1).reshape(a.shape[0] * 8, C)


def _scan_consts():
    i = np.arange(LANES)
    spread = (i[:, None] % 2) == (i[None, :] // HEAD)
    pick = (i[:, None] // HEAD) == (i[None, :] % 2)
    j2 = (i[:, None] // HEAD) == (i[None, :] // HEAD)
    sel = (i[None, :] // 2) == (np.arange(8 * NB * HEAD)[:, None] // HEAD)
    return (jnp.asarray(spread, BF16), jnp.asarray(pick, BF16), jnp.asarray(np.concatenate([j2, j2], 0), BF16),
            jnp.asarray(sel, BF16))


def _stack(blocks):
    return jnp.concatenate(blocks, axis=0) if len(blocks) > 1 else blocks[0]


def _unstack(x, n):
    return [x[i * HEAD:(i + 1) * HEAD] for i in range(n)]


def _headsums(blocks, j2k, group):
    res = []
    for g0 in range(0, len(blocks), group):
        x = _stack(blocks[g0:g0 + group])
        hi = x.astype(BF16)
        lo = (x - hi.astype(F32)).astype(BF16)
        out = jnp.dot(jnp.concatenate([hi, lo], axis=1), j2k, preferred_element_type=F32)
        res += _unstack(out, len(blocks[g0:g0 + group]))
    return res


def _headsums_out(blocks, pick):
    return _unstack(jnp.dot(_stack(blocks).astype(BF16), pick, preferred_element_type=F32), len(blocks))


def _expand8(tile, sel, spread):
    lhs = jnp.tile(tile.astype(BF16), (8 * NB, 1)) * sel
    return _unstack(jnp.dot(lhs, spread, preferred_element_type=F32), 8 * NB)


def _collapse(tile, blocks, first, lane_pair):
    for n, blk in enumerate(blocks):
        tile = jnp.where(jnp.tile(lane_pair == first + n, (HEAD // 8, 1)), blk, tile)
    return tile


def _row(tile, j, cb):
    return jnp.broadcast_to(tile[j:j + 1, cb * LANES:(cb + 1) * LANES], (HEAD, LANES))


def _scan_step(s_ref, rows, vexp, j2k, j, hist=None):
    w_t, k_t, a_t, b_t, r_t = rows
    s = [s_ref[:, cb * LANES:(cb + 1) * LANES] for cb in range(NB)]
    sab = _headsums([s[cb] * _row(a_t, j, cb) for cb in range(NB)], j2k, FWD_GROUP if hist is None else BWD_GROUP)
    out = []
    for cb in range(NB):
        sl = slice(cb * LANES, (cb + 1) * LANES)
        s_new = s[cb] * _row(w_t, j, cb) + sab[cb] * _row(b_t, j, cb) + vexp[j * NB + cb] * _row(k_t, j, cb)
        s_ref[:, sl] = s_new
        if hist is None:
            out.append(s_new * _row(r_t, j, cb))
        else:
            s_hist, sab_hist, idx = hist
            s_hist[idx + 1, :, sl] = s_new
            sab_hist[idx, :, sl] = sab[cb]
    return out


def wkv_fwd(r, w, k, v3, a, b):
    t = r.shape[0]
    nc = t // SCAN_CHUNK
    n8 = SCAN_CHUNK // 8

    def body(r_ref, w_ref, k_ref, a_ref, b_ref, v3_ref, spread_ref, pick_ref, j2k_ref, sel_ref, y3_ref, ck_ref,
             s_ref):
        @pl.when(pl.program_id(0) == 0)
        def _():
            s_ref[...] = jnp.zeros_like(s_ref)

        ck_ref[0] = s_ref[...]
        spread, pick, j2k, sel = spread_ref[...], pick_ref[...], j2k_ref[...], sel_ref[...]
        lane_pair = lax.broadcasted_iota(jnp.int32, (8, LANES), 1) // 2

        def t8_body(t8, carry):
            row0 = pl.multiple_of(t8 * 8, 8)
            rows = [ref[pl.ds(row0, 8), :] for ref in (w_ref, k_ref, a_ref, b_ref, r_ref)]
            vexp = _expand8(v3_ref[t8], sel, spread)
            y3 = jnp.zeros((HEAD, LANES), F32)
            for j in range(8):
                y3 = _collapse(y3, _headsums_out(_scan_step(s_ref, rows, vexp, j2k, j), pick), j * NB, lane_pair)
            y3_ref[t8] = y3
            return carry

        lax.fori_loop(0, n8, t8_body, 0)

    row_spec = pl.BlockSpec((SCAN_CHUNK, C), lambda i: (i, 0))
    col_spec = pl.BlockSpec((n8, HEAD, LANES), lambda i: (i, 0, 0))
    consts = _scan_consts()
    return _pcall(
        body, name="wkv_fwd", grid=(nc,),
        in_specs=[row_spec] * 5 + [col_spec] + [pl.BlockSpec(c.shape, lambda i: (0, 0)) for c in consts],
        out_specs=[col_spec, pl.BlockSpec((1, HEAD, C), lambda i: (i, 0, 0))],
        out_shape=[jax.ShapeDtypeStruct((t // 8, HEAD, LANES), F32), jax.ShapeDtypeStruct((nc, HEAD, C), F32)],
        scratch_shapes=[pltpu.VMEM((HEAD, C), F32)],
        compiler_params=_cparams(("arbitrary",)),
    )(r, w, k, a, b, v3, *consts)


def wkv_bwd(r, w, k, v3, a, b, ckpt, dy3):
    t = r.shape[0]
    nc = t // SCAN_CHUNK
    n8 = SCAN_CHUNK // 8

    def body(r_ref, w_ref, k_ref, a_ref, b_ref, v3_ref, dy3_ref, ck_ref, spread_ref, pick_ref, j2k_ref, sel_ref,
             dr_ref, dw_ref, dk_ref, da_ref, db_ref, dv3_ref, s_ref, g_ref, s_hist, sab_hist):
        @pl.when(pl.program_id(0) == 0)
        def _():
            g_ref[...] = jnp.zeros_like(g_ref)

        spread, pick, j2k, sel = spread_ref[...], pick_ref[...], j2k_ref[...], sel_ref[...]
        lane_pair = lax.broadcasted_iota(jnp.int32, (8, LANES), 1) // 2
        sub = lax.broadcasted_iota(jnp.int32, (8, LANES), 0)
        s_ref[...] = ck_ref[0]
        s_hist[0] = ck_ref[0]

        def redo(t8, carry):
            row0 = pl.multiple_of(t8 * 8, 8)
            rows = [ref[pl.ds(row0, 8), :] for ref in (w_ref, k_ref, a_ref, b_ref, r_ref)]
            vexp = _expand8(v3_ref[t8], sel, spread)
            for j in range(8):
                _scan_step(s_ref, rows, vexp, j2k, j, hist=(s_hist, sab_hist, t8 * 8 + j))
            return carry

        lax.fori_loop(0, n8, redo, 0)

        def back(q, carry):
            t8 = n8 - 1 - q
            row0 = pl.multiple_of(t8 * 8, 8)
            w_t, k_t, a_t, b_t, r_t = [ref[pl.ds(row0, 8), :] for ref in (w_ref, k_ref, a_ref, b_ref, r_ref)]
            vexp = _expand8(v3_ref[t8], sel, spread)
            dyexp = _expand8(dy3_ref[t8], sel, spread)
            dv3 = jnp.zeros((HEAD, LANES), F32)
            tiles = [[jnp.zeros((8, LANES), F32) for _ in range(NB)] for _ in range(5)]
            for j in range(7, -1, -1):
                idx = t8 * 8 + j
                g = [g_ref[:, cb * LANES:(cb + 1) * LANES] + dyexp[j * NB + cb] * _row(r_t, j, cb)
                     for cb in range(NB)]
                dsab = _headsums([g[cb] * _row(b_t, j, cb) for cb in range(NB)], j2k, BWD_GROUP)
                dv3 = _collapse(dv3, _headsums_out([g[cb] * _row(k_t, j, cb) for cb in range(NB)], pick), j * NB,
                                lane_pair)
                for cb in range(NB):
                    sl = slice(cb * LANES, (cb + 1) * LANES)
                    s_new = s_hist[idx + 1, :, sl]
                    s_old = s_hist[idx, :, sl]
                    sab = sab_hist[idx, :, sl]
                    sums = (s_new * dyexp[j * NB + cb], g[cb] * s_old, g[cb] * vexp[j * NB + cb],
                            s_old * dsab[cb], g[cb] * sab)
                    for n, prod in enumerate(sums):
                        rowsum = jnp.broadcast_to(jnp.sum(prod, axis=0, keepdims=True), (8, LANES))
                        tiles[n][cb] = jnp.where(sub == j, rowsum, tiles[n][cb])
                    g_ref[:, sl] = g[cb] * _row(w_t, j, cb) + dsab[cb] * _row(a_t, j, cb)
            dv3_ref[t8] = dv3
            for n, ref in enumerate((dr_ref, dw_ref, dk_ref, da_ref, db_ref)):
                for cb in range(NB):
                    ref[pl.ds(row0, 8), cb * LANES:(cb + 1) * LANES] = tiles[n][cb]
            return carry

        lax.fori_loop(0, n8, back, 0)

    row_spec = pl.BlockSpec((SCAN_CHUNK, C), lambda i: (nc - 1 - i, 0))
    col_spec = pl.BlockSpec((n8, HEAD, LANES), lambda i: (nc - 1 - i, 0, 0))
    consts = _scan_consts()
    return _pcall(
        body, name="wkv_bwd", grid=(nc,),
        in_specs=[row_spec] * 5 + [col_spec, col_spec, pl.BlockSpec((1, HEAD, C), lambda i: (nc - 1 - i, 0, 0))]
        + [pl.BlockSpec(c.shape, lambda i: (0, 0)) for c in consts],
        out_specs=[row_spec] * 5 + [col_spec],
        out_shape=[jax.ShapeDtypeStruct((t, C), F32)] * 5 + [jax.ShapeDtypeStruct((t // 8, HEAD, LANES), F32)],
        scratch_shapes=[pltpu.VMEM((HEAD, C), F32), pltpu.VMEM((HEAD, C), F32),
                        pltpu.VMEM((SCAN_CHUNK + 1, HEAD, C), F32), pltpu.VMEM((SCAN_CHUNK, HEAD, C), F32)],
        compiler_params=_cparams(("arbitrary",)),
    )(r, w, k, a, b, v3, dy3, ckpt, *consts)


def _prev_rows(cur, before, first_tile):
    last = jnp.where(first_tile, 0.0, before[7:8, :])
    row = lax.broadcasted_iota(jnp.int32, cur.shape, 0)
    return jnp.where(row == 0, last, pltpu.roll(cur, 1, 0))


def shift_fwd(p, mu, col, name):
    t, width = p.shape[0], mu.shape[1]
    tm = _tile(t, 256)

    def body(p_ref, before_ref, mu_ref, o_ref):
        cur = p_ref[...]
        prev = _prev_rows(cur, before_ref[...], pl.program_id(0) == 0)
        o_ref[...] = cur + (prev - cur) * mu_ref[...]

    return _pcall(
        body, name=name, grid=(t // tm,),
        in_specs=[pl.BlockSpec((tm, width), lambda i: (i, col)),
                  pl.BlockSpec((8, width), lambda i: (jnp.maximum(i * (tm // 8) - 1, 0), col)),
                  pl.BlockSpec((1, width), lambda i: (0, 0))],
        out_specs=pl.BlockSpec((tm, width), lambda i: (i, 0)),
        out_shape=jax.ShapeDtypeStruct((t, width), F32),
        compiler_params=_cparams(("parallel",)),
    )(p, p, mu)


def shift_bwd(dps, p, mu, col, name):
    t, width = dps.shape
    tm = _tile(t, 256)
    nt = t // tm

    def body(d_ref, after_ref, p_ref, before_ref, mu_ref, dp_ref, dmu_ref):
        i = pl.program_id(0)
        d, cur, mu_v = d_ref[...], p_ref[...], mu_ref[...]
        row = lax.broadcasted_iota(jnp.int32, d.shape, 0)
        first_after = jnp.where(i == nt - 1, 0.0, after_ref[0:1, :])
        nxt = jnp.where(row == tm - 1, first_after, pltpu.roll(d, tm - 1, 0))
        dp_ref[...] = (d * (1.0 - mu_v) + nxt * mu_v).astype(BF16)
        prev = _prev_rows(cur, before_ref[...], i == 0)
        part = jnp.sum(d * (prev - cur), axis=0, keepdims=True)

        @pl.when(i == 0)
        def _():
            dmu_ref[...] = part

        @pl.when(i > 0)
        def _():
            dmu_ref[...] += part

    return _pcall(
        body, name=name, grid=(nt,),
        in_specs=[pl.BlockSpec((tm, width), lambda i: (i, 0)),
                  pl.BlockSpec((8, width), lambda i: (jnp.minimum((i + 1) * (tm // 8), t // 8 - 1), 0)),
                  pl.BlockSpec((tm, width), lambda i: (i, col)),
                  pl.BlockSpec((8, width), lambda i: (jnp.maximum(i * (tm // 8) - 1, 0), col)),
                  pl.BlockSpec((1, width), lambda i: (0, 0))],
        out_specs=[pl.BlockSpec((tm, width), lambda i: (i, 0)), pl.BlockSpec((1, width), lambda i: (0, 0))],
        out_shape=[jax.ShapeDtypeStruct((t, width), BF16), jax.ShapeDtypeStruct((1, width), F32)],
        compiler_params=_cparams(("arbitrary",)),
    )(dps, dps, p, p, mu)


def loss_head(x3, tgt, g):
    t, d = x3.shape
    tm = _tile(t, 256)

    def body(x_ref, t_ref, g_ref, loss_ref, dx_ref, dg_ref, dxb_ref):
        (y,), vjp = jax.vjp(lambda a, b: rms_fn(a, b), x_ref[...], g_ref[...])
        diff = y - t_ref[...]
        part = 0.5 * jnp.sum(jnp.mean(diff * diff, axis=-1, keepdims=True), axis=0, keepdims=True)
        dx, dg = vjp((diff * (1.0 / d),))
        dx_ref[...] = dx
        dxb_ref[...] = dx.astype(BF16)
        part = jnp.broadcast_to(part, (1, LANES))
        first = pl.program_id(0) == 0

        @pl.when(first)
        def _():
            loss_ref[...] = part
            dg_ref[...] = dg

        @pl.when(jnp.logical_not(first))
        def _():
            loss_ref[...] += part
            dg_ref[...] += dg

    row = pl.BlockSpec((tm, d), lambda i: (i, 0))
    vec = pl.BlockSpec((1, d), lambda i: (0, 0))
    return _pcall(
        body, name="loss_head", grid=(t // tm,), in_specs=[row, row, vec],
        out_specs=[pl.BlockSpec((1, LANES), lambda i: (0, 0)), row, vec, row],
        out_shape=[jax.ShapeDtypeStruct((1, LANES), F32), jax.ShapeDtypeStruct((t, d), F32),
                   jax.ShapeDtypeStruct((1, d), F32), jax.ShapeDtypeStruct((t, d), BF16)],
        compiler_params=_cparams(("arbitrary",)),
    )(x3, tgt, g)


ADAM_LR, ADAM_B1, ADAM_B2, ADAM_EPS, ADAM_WD, ADAM_STEP = 0.001, 0.9, 0.999, 1e-08, 0.01, 10


def adamw(w, g, m, v, name):
    rows, width = w.shape
    tr = _div_tile(rows, 256, 8)

    def body(w_ref, g_ref, m_ref, v_ref, d_ref, nm_ref, nv_ref):
        gv = g_ref[...]
        m_new = ADAM_B1 * m_ref[...] + (1.0 - ADAM_B1) * gv
        v_new = ADAM_B2 * v_ref[...] + (1.0 - ADAM_B2) * (gv * gv)
        m_hat = m_new / (1.0 - ADAM_B1 ** ADAM_STEP)
        v_hat = v_new / (1.0 - ADAM_B2 ** ADAM_STEP)
        d_ref[...] = -ADAM_LR * (m_hat / (jnp.sqrt(v_hat) + ADAM_EPS) + ADAM_WD * w_ref[...])
        nm_ref[...] = m_new
        nv_ref[...] = v_new

    spec = pl.BlockSpec((tr, width), lambda i: (i, 0))
    return _pcall(
        body, name=name, grid=(rows // tr,), in_specs=[spec] * 4, out_specs=[spec] * 3,
        out_shape=[jax.ShapeDtypeStruct((rows, width), F32)] * 3,
        compiler_params=_cparams(("parallel",)),
    )(w, g, m, v)


def sum_slots(q, name, for_swap=False):
    n, rows, width = q.shape
    tr = _div_tile(rows, 512, 32 // q.dtype.itemsize)

    def body(*refs):
        refs = refs[for_swap:]
        acc = refs[0][...].astype(F32)
        for r in refs[1:n]:
            acc = acc + r[...].astype(F32)
        refs[n][...] = acc

    if not for_swap:
        specs = [pl.BlockSpec((None, tr, width), functools.partial(lambda s, i: (s, i, 0), s)) for s in range(n)]
        return _pcall(body, name=name, grid=(rows // tr,), in_specs=specs,
                      out_specs=pl.BlockSpec((tr, width), lambda i: (i, 0)),
                      out_shape=jax.ShapeDtypeStruct((rows, width), F32),
                      compiler_params=_cparams(("parallel",)))(*([q] * n))
    specs = [pl.BlockSpec((None, tr, width), functools.partial(lambda s, i, c_ref: (s, i, 0), s)) for s in range(n)]
    grid_spec = pltpu.PrefetchScalarGridSpec(
        num_scalar_prefetch=1, grid=(rows // tr,), in_specs=specs,
        out_specs=pl.BlockSpec((None, tr, width), lambda i, c_ref: (c_ref[0], i, 0)))
    return _pcall(body, name=name, grid_spec=grid_spec, out_shape=jax.ShapeDtypeStruct((2, rows, width), F32),
                  compiler_params=_cparams(("parallel",)))(
        lax.axis_index("c").astype(jnp.int32).reshape(1), *([q] * n))


MESH = pl.DeviceIdType.MESH
ANY_SPEC = pl.BlockSpec(memory_space=pl.ANY)


def _coords():
    return lax.axis_index("x"), lax.axis_index("y"), lax.axis_index("c")


def _other_chips(x, y):
    return [(1 - x, y), (x, 1 - y), (1 - x, 1 - y)]


def push(name, srcs, n_dst, plan_fn):
    n_arr = len(srcs)
    n_send = len(plan_fn(0, 0, 0))

    def body(*refs):
        src_refs, dst_refs = refs[:n_arr], refs[n_arr:2 * n_arr]
        send_sems, recv_sems = refs[2 * n_arr:]
        sends = plan_fn(*_coords())
        out = []
        for q, (src_ref, dst_ref) in enumerate(zip(src_refs, dst_refs)):
            out += [pltpu.make_async_remote_copy(src_ref.at[si], dst_ref.at[di], send_sems.at[q * n_send + k],
                                                 recv_sems.at[q * n_send + k], device_id=peer, device_id_type=MESH)
                    for k, (si, peer, di, _) in enumerate(sends)]
        for cp in out:
            cp.start()
        for q, (src_ref, dst_ref) in enumerate(zip(src_refs, dst_refs)):
            for k, (si, peer, _, ri) in enumerate(sends):
                pltpu.make_async_remote_copy(src_ref.at[si], dst_ref.at[ri], send_sems.at[q * n_send + k],
                                             recv_sems.at[q * n_send + k], device_id=peer,
                                             device_id_type=MESH).wait_recv()
        for cp in out:
            cp.wait_send()

    return _pcall(
        body, name=name, in_specs=[ANY_SPEC] * n_arr, out_specs=[ANY_SPEC] * n_arr,
        out_shape=[jax.ShapeDtypeStruct((n_dst,) + s.shape[1:], s.dtype) for s in srcs],
        scratch_shapes=[pltpu.SemaphoreType.DMA((n_arr * n_send,)), pltpu.SemaphoreType.DMA((n_arr * n_send,))],
    )(*srcs)


def plan_sibling_quarters(x, y, c):
    return [(2 * s + (1 - c), (x, y, 1 - c), s, s) for s in range(4)]


def plan_chips_by_shard(x, y, c):
    me = 2 * x + y
    return [(2 * px + py, (px, py, c), me, 2 * px + py) for px, py in _other_chips(x, y)]


def plan_chips_half(x, y, c):
    me = 2 * x + y
    return [(c, (px, py, c), me, 2 * px + py) for px, py in _other_chips(x, y)]


def swap_halves(name, bufs):
    n_arr = len(bufs)

    def body(*refs):
        buf_refs = refs[n_arr:2 * n_arr]
        send_sems, recv_sems = refs[2 * n_arr:]
        x, y, c = _coords()
        out = [pltpu.make_async_remote_copy(b.at[c], b.at[c], send_sems.at[q], recv_sems.at[q],
                                            device_id=(x, y, 1 - c), device_id_type=MESH)
               for q, b in enumerate(buf_refs)]
        for cp in out:
            cp.start()
        for q, b in enumerate(buf_refs):
            pltpu.make_async_remote_copy(b.at[1 - c], b.at[1 - c], send_sems.at[q], recv_sems.at[q],
                                         device_id=(x, y, 1 - c), device_id_type=MESH).wait_recv()
        for cp in out:
            cp.wait_send()

    return _pcall(
        body, name=name, in_specs=[ANY_SPEC] * n_arr, out_specs=[ANY_SPEC] * n_arr,
        out_shape=[jax.ShapeDtypeStruct(b.shape, b.dtype) for b in bufs],
        input_output_aliases={q: q for q in range(n_arr)},
        scratch_shapes=[pltpu.SemaphoreType.DMA((n_arr,)), pltpu.SemaphoreType.DMA((n_arr,))],
    )(*bufs)


HBM_SPEC = pl.BlockSpec(memory_space=pltpu.HBM)
SEM_SPEC = pl.BlockSpec(memory_space=pltpu.SEMAPHORE)
DATAFLOW = pltpu.SideEffectType.DATAFLOW_SIDE_EFFECTING


def _in_hbm(a):
    return pltpu.with_memory_space_constraint(a, pltpu.HBM)


def push_start(name, srcs, lands, plan_fn, after):
    n_arr = len(srcs)
    n_send = len(plan_fn(0, 0, 0))
    arrays = list(srcs) + ([] if lands is None else list(lands))
    if after is None:
        after = jnp.zeros((8, LANES), F32)
    n_all = len(arrays)

    def body(*refs):
        src_refs, land_refs = refs[:n_arr], refs[n_all - n_arr:n_all]
        send_sems, recv_sems = refs[n_all + 1:n_all + 3]
        token = refs[-1]
        sends = plan_fn(*_coords())
        for q, (src_ref, land_ref) in enumerate(zip(src_refs, land_refs)):
            for k, (si, peer, di, _) in enumerate(sends):
                pltpu.make_async_remote_copy(src_ref.at[si], land_ref.at[di], send_sems.at[q * n_send + k],
                                             recv_sems.at[q * n_send + k], device_id=peer,
                                             device_id_type=MESH).start()
        token[...] = jnp.zeros_like(token)

    sems = pltpu.SemaphoreType.DMA((n_arr * n_send,))
    out = _pcall(
        body, name=name,
        in_specs=[HBM_SPEC] * n_all + [ANY_SPEC],
        out_specs=[SEM_SPEC, SEM_SPEC] + [HBM_SPEC] * n_all + [pl.BlockSpec(memory_space=pltpu.VMEM)],
        out_shape=[sems, sems] + [pltpu.HBM(a.shape, a.dtype) for a in arrays]
        + [jax.ShapeDtypeStruct((8, LANES), F32)],
        input_output_aliases={q: 2 + q for q in range(n_all)},
        compiler_params=pltpu.CompilerParams(has_side_effects=DATAFLOW),
    )(*[_in_hbm(a) for a in arrays], after)
    return out[:-1], out[-1]


def push_wait(name, state, plan_fn, after, in_place=False, both=False):
    arrays = list(state[2:])
    n_all = len(arrays)
    n_arr = n_all if in_place else n_all // 2
    n_send = len(plan_fn(0, 0, 0))
    send_sems, recv_sems = state[:2]

    def body(*refs):
        src_refs, land_refs = refs[:n_arr], refs[n_all - n_arr:n_all]
        send_ref, recv_ref = refs[n_all:n_all + 2]
        sends = plan_fn(*_coords())
        for q, (src_ref, land_ref) in enumerate(zip(src_refs, land_refs)):
            for k, (si, peer, _, ri) in enumerate(sends):
                cp = pltpu.make_async_remote_copy(src_ref.at[si], land_ref.at[ri], send_ref.at[q * n_send + k],
                                                  recv_ref.at[q * n_send + k], device_id=peer, device_id_type=MESH)
                cp.wait_send()
                cp.wait_recv()

    out = _pcall(
        body, name=name,
        in_specs=[HBM_SPEC] * n_all + [SEM_SPEC, SEM_SPEC, ANY_SPEC],
        out_specs=[HBM_SPEC] * n_all,
        out_shape=[pltpu.HBM(a.shape, a.dtype) for a in arrays],
        input_output_aliases={q: q for q in range(n_all)},
        compiler_params=pltpu.CompilerParams(has_side_effects=DATAFLOW),
    )(*arrays, send_sems, recv_sems, after)
    return list(out) if both else list(out[n_all - n_arr:])


def plan_chips_gather(x, y, c):
    me = 2 * x + y
    return [(c, (px, py, c), 2 * me + c, 2 * (2 * px + py) + c) for px, py in _other_chips(x, y)]


def plan_forward(x, y, c):
    return [(2 * (2 * px + py) + c, (x, y, 1 - c), 2 * (2 * px + py) + c, 2 * (2 * px + py) + (1 - c))
            for px, py in _other_chips(x, y)]


def add_kept(g8, other, name):
    _, rows, cols = other.shape
    tr = _div_tile(rows, 512, 16)

    def body(c_ref, a_ref, b_ref, o_ref):
        o_ref[...] = (a_ref[...].astype(F32) + b_ref[...].astype(F32)).astype(BF16)

    spec = pl.BlockSpec((None, tr, cols), lambda s, i, c_ref: (s, i, 0))
    grid_spec = pltpu.PrefetchScalarGridSpec(
        num_scalar_prefetch=1, grid=(4, rows // tr),
        in_specs=[pl.BlockSpec((None, tr, cols), lambda s, i, c_ref: (2 * s + c_ref[0], i, 0)), spec],
        out_specs=spec)
    return _pcall(body, name=name, grid_spec=grid_spec, out_shape=jax.ShapeDtypeStruct((4, rows, cols), BF16),
                  compiler_params=_cparams(("parallel", "parallel")))(
        lax.axis_index("c").astype(jnp.int32).reshape(1), g8, other)


def reduce_scatter_begin(grads, names, tag):
    g8 = [g.reshape(8, g.shape[1] // 2, g.shape[2]) for g in grads]
    return _chip_sums_start(g8, push(tag + "_sibling", g8, 4, plan_sibling_quarters), names, tag)


def _chip_sums_start(g8, from_sibling, names, tag):
    chip_sum = [add_kept(a, b, tag + "_add_" + n) for a, b, n in zip(g8, from_sibling, names)]
    return push_start(tag + "_start", chip_sum, chip_sum, plan_chips_by_shard, None)


def sibling_exchange_begin(grads, tag):
    g8 = [g.reshape(8, g.shape[1] // 2, g.shape[2]) for g in grads]
    lands = [lax.empty((4,) + g.shape[1:], g.dtype) for g in g8]
    return push_start(tag + "_sib_start", g8, lands, plan_sibling_quarters, None)


def reduce_scatter_continue(state, names, tag, after):
    n = len(names)
    arrays = push_wait(tag + "_sib_wait", state, plan_sibling_quarters, after, both=True)
    return _chip_sums_start(arrays[:n], arrays[n:], names, tag)


def reduce_scatter_end(state, names, tag, after):
    by_chip = push_wait(tag + "_wait", state, plan_chips_by_shard, after)
    both = swap_halves(tag + "_halves", [sum_slots(q, tag + "_sum_" + n, for_swap=True)
                                         for q, n in zip(by_chip, names)])
    return [b.reshape(2 * b.shape[1], b.shape[2]) for b in both]


def _landing(mine):
    return [jnp.broadcast_to(m[None], (4,) + m.shape).reshape((8,) + m.shape[1:]) for m in mine]


def gather_begin(tag, mine, after, lands=None):
    return push_start(tag + "_start", mine, _landing(mine) if lands is None else lands, plan_chips_gather, after)


def gather_forward(tag, state, after):
    lands = push_wait(tag + "_wait", state, plan_chips_gather, after)
    return push_start(tag + "_fwd_start", lands, None, plan_forward, None)


def gather_end(tag, state, after):
    bufs = push_wait(tag + "_fwd_wait", state, plan_forward, after, in_place=True)
    return [g.reshape(4, 2 * g.shape[1], g.shape[2]) for g in bufs]


PACK_QUANTUM = 8 * C


def _pack(parts, dtype):
    pieces = []
    for p in parts:
        flat = p.astype(dtype).reshape(-1)
        pieces.append(jnp.pad(flat, (0, -flat.shape[0] % PACK_QUANTUM)).reshape(-1, C))
    rows = sum(p.shape[0] for p in pieces)
    if rows % 16:
        pieces.append(jnp.zeros((8, C), dtype))
    return jnp.concatenate(pieces, axis=0)


def _unpack(packed, shapes):
    out, at = [], 0
    for s in shapes:
        n = int(np.prod(s))
        rows = -(-n // PACK_QUANTUM) * 8
        out.append(packed[at:at + rows].reshape(-1)[:n].reshape(s))
        at += rows
    return out


def all_reduce_begin(vec):
    rows, width = vec.shape
    (pair,) = swap_halves("ar_sibling", [jnp.broadcast_to(vec[None], (2, rows, width))])
    chip_sum = sum_slots(pair, "ar_sum_sibling").reshape(2, rows // 2, width)
    mine = lax.dynamic_index_in_dim(chip_sum, lax.axis_index("c"), axis=0, keepdims=True)
    return push_start("ar_start", [chip_sum], [jnp.broadcast_to(mine, (4, rows // 2, width))], plan_chips_half,
                      None)


def all_reduce_end(state, after):
    (by_chip,) = push_wait("ar_wait", state, plan_chips_half, after)
    (both,) = swap_halves("ar_halves", [sum_slots(by_chip, "ar_sum_chips", for_swap=True)])
    return both.reshape(2 * both.shape[1], both.shape[2])


def kernel(x, norm1_g, w_in, shift_mu, w0, w2, a0, a2, g2, k_k, k_a, r_k, gn_w, gn_b, sgu_ln_g, sgu_ln_b, sgu_w, sgu_b, sgu_out_g, w_out, norm2_g, w_gate, w_up, w_down, final_g, loss_target, m_norm1_g, m_w_in, m_shift_mu, m_w0, m_w2, m_a0, m_a2, m_g2, m_k_k, m_k_a, m_r_k, m_gn_w, m_gn_b, m_sgu_ln_g, m_sgu_ln_b, m_sgu_w, m_sgu_b, m_sgu_out_g, m_w_out, m_norm2_g, m_w_gate, m_w_up, m_w_down, m_final_g, v_norm1_g, v_w_in, v_shift_mu, v_w0, v_w2, v_a0, v_a2, v_g2, v_k_k, v_k_a, v_r_k, v_gn_w, v_gn_b, v_sgu_ln_g, v_sgu_ln_b, v_sgu_w, v_sgu_b, v_sgu_out_g, v_w_out, v_norm2_g, v_w_gate, v_w_up, v_w_down, v_final_g):
    args = dict(locals())
    names = ["norm1_g", "w_in", "shift_mu", "w0", "w2", "a0", "a2", "g2", "k_k", "k_a", "r_k", "gn_w", "gn_b",
             "sgu_ln_g", "sgu_ln_b", "sgu_w", "sgu_b", "sgu_out_g", "w_out", "norm2_g", "w_gate", "w_up", "w_down",
             "final_g"]
    big = ["w_in", "w_out", "w_gate", "w_up", "w_down"]
    xi, yi = lax.axis_index("x"), lax.axis_index("y")
    chip = 2 * xi + yi
    x2d, tgt = x[0], loss_target[0]
    t = x2d.shape[0]

    lora_names = ["w2", "a2", "g2"]
    first, later = ["w_in"] + lora_names, ["w_out", "w_gate", "w_up", "w_down"]

    def halves(n, token=None):
        m = args[n][0] if token is None else args[n][0] + token[0, 0]
        m = m.astype(BF16)
        return m.reshape(2, m.shape[0] // 2, m.shape[1])

    first_state, first_token = gather_begin("g0", [halves(n) for n in first], norm1_g)
    later_mine = [halves(n, first_token) for n in later]
    later_lands = _landing(later_mine)
    (h1,) = row_fwd("norm1", rms_fn, [Tiled(x2d)], [norm1_g + first_token[0, 0]], [], [(D, BF16)], _tile(t, 256))
    prepared = functools.reduce(lambda p, q: p + q, [a[0, 0, :1].astype(F32) for a in later_lands])
    first_state, first_token = gather_forward("g0", first_state, prepared + h1[0, :1].astype(F32))
    got = gather_end("g0", first_state, first_token)
    later_state, later_token = gather_begin("g1", later_mine, got[0], later_lands)
    full = {n: blk.transpose(1, 0, 2).reshape(blk.shape[1], 4 * blk.shape[2]) for n, blk in zip(first[1:], got[1:])}
    s0, s1, s2, s3 = got[0]
    w_in_p = jnp.concatenate([s0, s1, s2[:, :CUT_A], s2[:, CUT_B:], s3, s2[:, CUT_A:CUT_B],
                              jnp.zeros((D, P_W - D_IN), BF16)], axis=1)
    mu_rkv = shift_mu[:, :3 * C] + later_token[0, 0]
    mu_lora = jnp.pad(shift_mu[:, 3 * C:], ((0, 0), (0, LORA_W - 288)))
    w2p = jnp.pad(full["w2"], ((0, 64), (0, 0)))
    a2p = jnp.pad(full["a2"], ((64, 0), (0, 0)))
    g2p = jnp.pad(full["g2"], ((0, 96), (0, 0)))
    ii = np.arange(C)
    jm = _scan_consts()[2]
    e_np = (np.arange(LANES)[:, None] == (ii[None, :] // HEAD))
    e_mat, e_t = jnp.asarray(e_np, BF16), jnp.asarray(e_np.T, BF16)
    b_t = jnp.pad(sgu_b[0].T, ((0, 0), (0, LANES - HEADS)))
    r_k_flat = r_k.reshape(1, C)
    tm_row = _tile(t, 256)
    tm_sgu_b = _tile(t, 128)

    p = mm(h1, w_in_p, name="proj_in")
    ps_rkv = shift_fwd(p, mu_rkv, 0, "shift_rkv")
    ps_lora = shift_fwd(p, mu_lora, P_LORA // LORA_W, "shift_lora")
    pre_par = [w0, w2p, a0, a2p, g2p, k_k, k_a]
    r_, w_, k2, v_, an, b_, g_ = row_fwd("pre", pre_fn, [Tiled(ps_rkv), Tiled(ps_lora)], pre_par, [jm],
                                         [(C, F32)] * 7, tm_row)
    v3 = to_cols(v_)
    y3, ckpt = wkv_fwd(r_, w_, k2, v3, an, b_)
    later_state, later_token = gather_forward("g1", later_state, y3)
    y_ = from_cols(y3 + later_token[0, 0])
    post_in = [Tiled(z) for z in (y_, r_, k2, v_, g_)]
    post_par = [gn_w, gn_b, r_k_flat]
    (y_rwkv,) = row_fwd("post", post_fn, post_in, post_par, [jm], [(C, BF16)], tm_row)
    sgu_in = [Tiled(p, C, 3), Tiled(p, C, 4)]
    sgu_par = [sgu_ln_g, sgu_ln_b, sgu_w[0], b_t, sgu_out_g]
    (y_sgu,) = row_fwd("sgu", sgu_fn, sgu_in, sgu_par, [e_mat, e_t], [(C, BF16)], tm_row)
    y_cat = jnp.concatenate([y_rwkv, y_sgu], axis=1)
    for n, blk in zip(later, gather_end("g1", later_state, y_cat)):
        full[n] = blk if n in ("w_gate", "w_up") else blk.reshape(4 * blk.shape[1], blk.shape[2])
    x_mid = mm(y_cat, full["w_out"], res=x2d, name="proj_out")
    (h2,) = row_fwd("norm2", rms_fn, [Tiled(x_mid)], [norm2_g], [], [(D, BF16)], tm_row)
    gate, up, act = ffn_in(h2, full["w_gate"], full["w_up"], lambda g_tile, u_tile: swiglu_fn(g_tile, u_tile)[0])
    x_out = mm(act, full["w_down"], res=x_mid, name="ffn_down")
    loss_part, dx_out, d_final_g, dx_out_b = loss_head(x_out, tgt, final_g.reshape(1, D))

    grads = {"final_g": d_final_g.reshape(D)}
    d_gate, d_up = mm(dx_out_b, full["w_down"], tb=True, name="d_act",
                      post=(lambda d_act, g_tile, u_tile: jax.vjp(swiglu_fn, g_tile, u_tile)[1]((d_act,)), [gate, up],
                            [BF16, BF16]))
    grads["w_down"] = mm(act, dx_out_b, ta=True, out_dtype=BF16, name="d_w_down")
    grads["w_gate"] = mm(h2, d_gate, shards="out", out_dtype=BF16, name="d_w_gate")
    grads["w_up"] = mm(h2, d_up, shards="out", out_dtype=BF16, name="d_w_up")
    d_h2 = ffn_in_bwd(d_gate, d_up, full["w_gate"], full["w_up"])
    dx_mid, grads["norm2_g"], dx_mid_b = row_bwd("norm2_b", rms_res_fn, [Tiled(x_mid)], [norm2_g], [],
                                                 [Tiled(d_h2), Tiled(dx_out)], tm_row, bf16_copy_of=(0,))
    d_ycat = mm(dx_mid_b, full["w_out"], tb=True, name="d_ycat")
    grads["w_out"] = mm(y_cat, dx_mid_b, ta=True, out_dtype=BF16, name="d_w_out")
    rs_state, rs_token = sibling_exchange_begin([grads[n].reshape((4,) + args[n].shape[1:]) for n in later], "rs2")
    d_pu, d_pv, grads["sgu_ln_g"], grads["sgu_ln_b"], d_sgu_w, d_b_t, grads["sgu_out_g"] = row_bwd(
        "sgu_b", sgu_fn, sgu_in, sgu_par[:-1] + [sgu_out_g + rs_token[0, 0]], [e_mat, e_t], [Tiled(d_ycat, C, 1)],
        tm_sgu_b, tiled_out_dtypes=[BF16, BF16])
    grads["sgu_w"] = d_sgu_w[None]
    grads["sgu_b"] = d_b_t[:, :HEADS].T[None]
    post_par_b = [gn_w + rs_token[0, 0]] + post_par[1:]
    d_y, d_r1, d_k1, d_v1, d_g, grads["gn_w"], grads["gn_b"], d_r_k = row_bwd(
        "post_b", post_fn, post_in, post_par_b, [jm], [Tiled(d_ycat, C, 0)], tm_row)
    grads["r_k"] = d_r_k.reshape(r_k.shape)
    rs_state, rs_token = reduce_scatter_continue(rs_state, later, "rs2", d_y[0, :1] + d_pu[0, :1].astype(F32))
    d_r2, d_w, d_k2, d_a, d_b, d_v3 = wkv_bwd(r_, w_, k2, v3, an, b_, ckpt, to_cols(d_y + rs_token[0, 0]))
    d_v2 = from_cols(d_v3)
    pre_cts = [(Tiled(d_r1), Tiled(d_r2)), Tiled(d_w), (Tiled(d_k1), Tiled(d_k2)), (Tiled(d_v1), Tiled(d_v2)),
               Tiled(d_a), Tiled(d_b), Tiled(d_g)]
    d_ps_rkv, d_ps_lora, grads["w0"], d_w2p, grads["a0"], d_a2p, d_g2p, grads["k_k"], grads["k_a"] = row_bwd(
        "pre_b", pre_fn, [Tiled(ps_rkv), Tiled(ps_lora)], pre_par, [jm], pre_cts, tm_row)
    grads["w2"], grads["a2"], grads["g2"] = d_w2p[None, :64], d_a2p[None, 64:], d_g2p[None, :160]
    d_p_rkv, d_mu_rkv = shift_bwd(d_ps_rkv, p, mu_rkv, 0, "shift_rkv_b")
    d_p_lora, d_mu_lora = shift_bwd(d_ps_lora, p, mu_lora, P_LORA // LORA_W, "shift_lora_b")
    grads["shift_mu"] = jnp.concatenate([d_mu_rkv, d_mu_lora[:, :288]], axis=1)
    d_p = jnp.concatenate([d_p_rkv, d_pu, d_pv, d_p_lora], axis=1)
    d_h1 = mm(d_p, w_in_p, tb=True, name="d_h1")
    d_w_in_p = mm(h1, d_p, ta=True, out_dtype=BF16, name="d_w_in")
    ws = D_IN // 4
    shard2 = jnp.concatenate([d_w_in_p[:, 2 * ws:3 * C], d_w_in_p[:, P_LORA:P_LORA + 288],
                              d_w_in_p[:, 3 * C:3 * C + ws - CUT_B]], axis=1)
    grads["w_in"] = jnp.stack([d_w_in_p[:, :ws], d_w_in_p[:, ws:2 * ws], shard2,
                               d_w_in_p[:, 3 * C + ws - CUT_B:P_LORA]])
    dx, grads["norm1_g"] = row_bwd("norm1_b", rms_res_fn, [Tiled(x2d)], [norm1_g], [],
                                   [Tiled(d_h1), Tiled(dx_mid)], tm_row)

    small = [n for n in names if n not in big]
    small_shapes = [(1, 64, C) if n in ("w2", "a2") else (1, 160, C) if n == "g2" else args[n].shape for n in small]
    vec = _pack([grads[n] for n in small] + [loss_part[:, :1]], F32)
    w_in_state, w_in_token = reduce_scatter_begin([grads["w_in"]], ["w_in"], "rs1")
    small_state, small_token = all_reduce_begin(vec)
    total = {n: g[None] for n, g in zip(later, reduce_scatter_end(rs_state, later, "rs2", w_in_token + small_token))}
    delta, new_m, new_v = {}, {}, {}

    def adam_big(n):
        s = args[n].shape
        view = (lambda a: a[0].T) if n == "w_in" else (lambda a: a[0])
        back = (lambda a: a.T.reshape(s)) if n == "w_in" else (lambda a: a.reshape(s))
        d_, m_, v__ = adamw(view(args[n]), view(total[n]), view(args["m_" + n]), view(args["v_" + n]), "adamw_" + n)
        delta[n], new_m[n], new_v[n] = back(d_), back(m_), back(v__)

    for n in later:
        adam_big(n)
    total["w_in"] = reduce_scatter_end(w_in_state, ["w_in"], "rs1", new_v["w_down"])[0][None]
    *small_sums, loss = _unpack(all_reduce_end(small_state, new_v["w_down"]), small_shapes + [()])
    for n, g_sum in zip(small, small_sums):
        total[n] = lax.dynamic_slice_in_dim(g_sum, chip * 256, 256, axis=2) if n in lora_names else g_sum
    adam_big("w_in")
    flat = [_pack([src[n] for n in small], F32)
            for src in (args, total, {n: args["m_" + n] for n in small}, {n: args["v_" + n] for n in small})]
    outs = adamw(*flat, "adamw_small")
    for res, o in zip((delta, new_m, new_v), outs):
        res.update(zip(small, _unpack(o, [args[n].shape for n in small])))
    return (loss, dx[None], *[total[n] for n in names], *[delta[n] for n in names],
            *[new_m[n] for n in names], *[new_v[n] for n in names])
```
